```python
import jax, jax.numpy as jnp
from jax import lax
import numpy as np

D_MODEL = 2048
BATCH = 4
SEQ = 2048
DEPTH = 2

N_META = 16
CHUNK = 64
Q_BLOCK = 128
CONV_K = 4
EPS = 1e-6

A_HEADS = 4
A_DK = 128
A_DV = D_MODEL // (2 * A_HEADS)
B_HEADS = 4
B_DK = 128
B_DV = D_MODEL // (2 * B_HEADS)
GATE_RANK = 16
GATE_TAU = 16.0
C_HEADS = 8
C_DK = 128
C_DV = D_MODEL // (2 * C_HEADS)
D_HEADS = 8
D_NOPE = 128
D_ROPE = 64
D_V = D_MODEL // (2 * D_HEADS)
Q_LORA = 512
KV_LORA = 256
ROPE_THETA = 10000.0
N_GROUPS = 4
EXPERTS_PER_GROUP = 8
N_EXPERTS = N_GROUPS * EXPERTS_PER_GROUP
TOP_K = 2
D_EXPERT = 512
MOE_BLOCK = 128

N_EVEN = (DEPTH + 1) // 2
N_ODD = DEPTH // 2

EVEN_SPLITS = (A_HEADS * A_DK, A_HEADS * A_DK, A_HEADS * A_DV, A_HEADS * A_DV, A_HEADS, A_HEADS,
               B_HEADS * B_DK, B_HEADS * B_DK, B_HEADS * B_DV, B_HEADS * B_DV, GATE_RANK)
ODD_SPLITS = (C_HEADS * C_DK, C_HEADS * C_DK, C_HEADS * C_DV, C_HEADS * C_DV, Q_LORA, KV_LORA, D_ROPE)

kernel_name = "hybrid_mlstm_gla_hgrn2_mla_hmoe"

F32 = jnp.float32


def rms_norm(u, g):
    uf = u.astype(F32)
    y = uf * lax.rsqrt(jnp.mean(uf * uf, axis=-1, keepdims=True) + EPS)
    return (y * g.astype(F32)).astype(u.dtype)


def head_rms_norm(u, g, n_heads):
    B, T, W = u.shape
    y = rms_norm(u.reshape(B, T, n_heads, W // n_heads), g.reshape(n_heads, W // n_heads))
    return y.reshape(B, T, W)


def split_cols(z, sizes):
    offs = np.cumsum(np.array(sizes))[:-1].tolist()
    return jnp.split(z, offs, axis=-1)


def to_heads(u, n_heads):
    B, T, W = u.shape
    return u.reshape(B, T, n_heads, W // n_heads).transpose(0, 2, 1, 3)


def from_heads(u):
    B, H, T, d = u.shape
    return u.transpose(0, 2, 1, 3).reshape(B, T, H * d)


def meta_and_chunks(u):
    B, H = u.shape[:2]
    rest = u.shape[3:]
    meta = u[:, :, :N_META].reshape(B, H, 1, N_META, *rest)
    real = u[:, :, N_META:].reshape(B, H, -1, CHUNK, *rest)
    return meta, real


def join_chunks(meta, real):
    B, H = meta.shape[:2]
    d = meta.shape[-1]
    return jnp.concatenate([meta.reshape(B, H, -1, d), real.reshape(B, H, -1, d)], axis=2)


def causal_conv(u, w, b):
    T = u.shape[1]
    up = jnp.pad(u, ((0, 0), (CONV_K - 1, 0), (0, 0)))
    y = b
    for j in range(CONV_K):
        y = y + up[:, j:j + T] * w[j]
    return y


def mlstm_chunks(q, k, v, ig, lf, state):
    L = q.shape[3]
    b = jnp.cumsum(lf, axis=-1)
    b_end = b[..., -1]
    w_end = b_end[..., None] - b + ig
    m_loc = jnp.max(w_end, axis=-1)
    k_w = k * jnp.exp(w_end - m_loc[..., None])[..., None]
    C_loc = jnp.einsum('bhcsk,bhcsv->bhckv', k_w, v)
    n_loc = jnp.sum(k_w, axis=3)

    def step(carry, inp):
        C, n, m = carry
        Cl, nl, ml, bl = inp
        m_new = jnp.maximum(bl + m, ml)
        a = jnp.exp(bl + m - m_new)
        c = jnp.exp(ml - m_new)
        new = (a[..., None, None] * C + c[..., None, None] * Cl, a[..., None] * n + c[..., None] * nl, m_new)
        return new, (C, n, m)

    xs = (jnp.moveaxis(C_loc, 2, 0), jnp.moveaxis(n_loc, 2, 0), jnp.moveaxis(m_loc, 2, 0), jnp.moveaxis(b_end, 2, 0))
    final, (C_in, n_in, m_in) = lax.scan(step, state, xs)
    C_in = jnp.moveaxis(C_in, 0, 2)
    n_in = jnp.moveaxis(n_in, 0, 2)
    m_in = jnp.moveaxis(m_in, 0, 2)

    causal = jnp.tril(jnp.ones((L, L), dtype=bool))
    d = jnp.where(causal, b[..., :, None] - b[..., None, :] + ig[..., None, :], -jnp.inf)
    inter = b + m_in[..., None]
    m_t = jnp.maximum(inter, jnp.max(d, axis=-1))
    s = jnp.einsum('bhctk,bhcsk->bhcts', q, k) * jnp.exp(d - m_t[..., None])
    a_t = jnp.exp(inter - m_t)
    num = jnp.einsum('bhcts,bhcsv->bhctv', s, v) + a_t[..., None] * jnp.einsum('bhctk,bhckv->bhctv', q, C_in)
    den = jnp.sum(s, axis=-1) + a_t * jnp.einsum('bhctk,bhck->bhct', q, n_in)
    h = num / jnp.maximum(jnp.abs(den), jnp.exp(-m_t))[..., None]
    return h, final


def gla_chunks(q, k, v, lg, state):
    L = q.shape[3]
    g = jnp.cumsum(lg, axis=3)
    g_end = g[:, :, :, -1]
    g_ref = g[:, :, :, L // 2][:, :, :, None]
    q_in = q * jnp.exp(g - g_ref)
    k_in = k * jnp.exp(g_ref - g)
    causal = jnp.tril(jnp.ones((L, L), dtype=bool))
    s = jnp.where(causal, jnp.einsum('bhctk,bhcsk->bhcts', q_in, k_in), 0.0)
    intra = jnp.einsum('bhcts,bhcsv->bhctv', s, v)
    S_loc = jnp.einsum('bhcsk,bhcsv->bhckv', k * jnp.exp(g_end[:, :, :, None] - g), v)
    decay = jnp.exp(g_end)

    def step(S, inp):
        dc, Sl = inp
        return dc[..., None] * S + Sl, S

    final, S_in = lax.scan(step, state, (jnp.moveaxis(decay, 2, 0), jnp.moveaxis(S_loc, 2, 0)))
    S_in = jnp.moveaxis(S_in, 0, 2)
    inter = jnp.einsum('bhctk,bhckv->bhctv', q * jnp.exp(g), S_in)
    return intra + inter, final


def run_mlstm(q, k, v, ig, lf):
    B, H, T, dk = q.shape
    dv = v.shape[-1]
    state = (jnp.zeros((B, H, dk, dv), F32), jnp.zeros((B, H, dk), F32), jnp.zeros((B, H), F32))
    metas, reals = zip(*[meta_and_chunks(u) for u in (q, k, v, ig, lf)])
    h_meta, state = mlstm_chunks(*metas, state)
    h_real, _ = mlstm_chunks(*reals, state)
    return join_chunks(h_meta, h_real)


def run_gla(q, k, v, lg):
    B, H, T, dk = q.shape
    dv = v.shape[-1]
    state = jnp.zeros((B, H, dk, dv), F32)
    metas, reals = zip(*[meta_and_chunks(u) for u in (q, k, v, lg)])
    o_meta, state = gla_chunks(*metas, state)
    o_real, _ = gla_chunks(*reals, state)
    return join_chunks(o_meta, o_real)


def rope(u, pos):
    half = u.shape[-1] // 2
    inv = ROPE_THETA ** (-jnp.arange(half, dtype=F32) / half)
    ang = pos[:, None] * inv[None, :]
    cos = jnp.cos(ang)[:, None, :].astype(u.dtype)
    sin = jnp.sin(ang)[:, None, :].astype(u.dtype)
    u1, u2 = u[..., :half], u[..., half:]
    return jnp.concatenate([u1 * cos - u2 * sin, u1 * sin + u2 * cos], axis=-1)


def block_causal_attention(q, k, v):
    B, T, H, dqk = q.shape
    nb = -(-T // Q_BLOCK)
    Tp = nb * Q_BLOCK
    qb = jnp.pad(q, ((0, 0), (0, Tp - T), (0, 0), (0, 0))).reshape(B, nb, Q_BLOCK, H, dqk).transpose(1, 0, 2, 3, 4)
    kpos = jnp.arange(T)
    scale = dqk ** -0.5

    def one_block(args):
        qi, bi = args
        s = jnp.einsum('bqhd,bkhd->bhqk', qi, k).astype(F32) * scale
        qpos = bi * Q_BLOCK + jnp.arange(Q_BLOCK)
        s = jnp.where(kpos[None, :] <= qpos[:, None], s, -jnp.inf)
        p = jax.nn.softmax(s, axis=-1).astype(v.dtype)
        return jnp.einsum('bhqk,bkhd->bqhd', p, v)

    o = lax.map(one_block, (qb, jnp.arange(nb)))
    return o.transpose(1, 0, 2, 3, 4).reshape(B, Tp, H, -1)[:, :T]


def mla(cq, ckv, kr, pos, q_a_norm, w_q_up, kv_a_norm, w_kv_up, q_norm, k_norm):
    B, T, _ = cq.shape
    q = (rms_norm(cq, q_a_norm) @ w_q_up).reshape(B, T, D_HEADS, D_NOPE + D_ROPE)
    kv = (rms_norm(ckv, kv_a_norm) @ w_kv_up).reshape(B, T, D_HEADS, D_NOPE + D_V)
    k = jnp.concatenate([kv[..., :D_NOPE], jnp.broadcast_to(kr[:, :, None, :], (B, T, D_HEADS, D_ROPE))], axis=-1)
    q = rms_norm(q, q_norm)
    k = rms_norm(k, k_norm)
    q = jnp.concatenate([q[..., :D_NOPE], rope(q[..., D_NOPE:], pos)], axis=-1)
    k = jnp.concatenate([k[..., :D_NOPE], rope(k[..., D_NOPE:], pos)], axis=-1)
    o = block_causal_attention(q, k, kv[..., D_NOPE:])
    return o.reshape(B, T, D_HEADS * D_V)


def even_layer(x, norm_g, w_in, conv_w, conv_b, b_i, b_f, a_norm, w_gate2, b_gate, b_norm, w_out):
    h = rms_norm(x, norm_g)
    aq, ak, av, ao, ai, af, bq, bk, bv, bg, ba = split_cols(h @ w_in, EVEN_SPLITS)
    qk = jax.nn.silu(causal_conv(jnp.concatenate([aq, ak], axis=-1), conv_w, conv_b))
    aq, ak = qk[..., :A_HEADS * A_DK], qk[..., A_HEADS * A_DK:]
    ig = (ai.astype(F32) + b_i).transpose(0, 2, 1)
    lf = jax.nn.log_sigmoid(af.astype(F32) + b_f).transpose(0, 2, 1)
    ha = run_mlstm(to_heads(aq, A_HEADS), to_heads(ak, A_HEADS) * (A_DK ** -0.5), to_heads(av, A_HEADS), ig, lf)
    ya = jax.nn.sigmoid(ao) * head_rms_norm(from_heads(ha).astype(x.dtype), a_norm, A_HEADS)
    lg = jax.nn.log_sigmoid((ba @ w_gate2).astype(F32) + b_gate) / GATE_TAU
    hb = run_gla(to_heads(bq, B_HEADS) * (B_DK ** -0.5), to_heads(bk, B_HEADS), to_heads(bv, B_HEADS), to_heads(lg, B_HEADS))
    yb = jax.nn.silu(bg) * head_rms_norm(from_heads(hb).astype(x.dtype), b_norm, B_HEADS)
    return x + jnp.concatenate([ya, yb], axis=-1) @ w_out


def odd_layer(x, pos, lb, norm_g, w_in, c_norm, q_a_norm, w_q_up, kv_a_norm, w_kv_up, q_norm, k_norm, w_out):
    h = rms_norm(x, norm_g)
    cq, cf, ci, cg, dq, dkv, dkr = split_cols(h @ w_in, ODD_SPLITS)
    fpre = cf.astype(F32)
    lf = jnp.logaddexp(jnp.log(lb), jnp.log1p(-lb) + jax.nn.log_sigmoid(fpre))
    kc = (1.0 - lb) * jax.nn.sigmoid(-fpre)
    hc = run_gla(to_heads(cq, C_HEADS), to_heads(kc, C_HEADS), to_heads(ci, C_HEADS), to_heads(lf, C_HEADS))
    yc = jax.nn.sigmoid(cg) * head_rms_norm(from_heads(hc).astype(x.dtype), c_norm, C_HEADS)
    yd = mla(dq, dkv, dkr, pos, q_a_norm, w_q_up, kv_a_norm, w_kv_up, q_norm, k_norm)
    return x + jnp.concatenate([yc, yd], axis=-1) @ w_out


def hier_moe(h, w_group, b_group, w_expert, b_expert, w1, w3, w2):
    B, T, D = h.shape
    xf = h.reshape(-1, D)
    N = xf.shape[0]
    p_group = jax.nn.softmax((xf @ w_group).astype(F32) + b_group, axis=-1)
    p_top, grp = lax.top_k(p_group, 1)
    e_logits = ((xf @ w_expert).astype(F32) + b_expert).reshape(N, N_GROUPS, EXPERTS_PER_GROUP)
    e_sel = jnp.take_along_axis(e_logits, grp[:, :, None], axis=1)[:, 0]
    top_p, top_i = lax.top_k(jax.nn.softmax(e_sel, axis=-1), TOP_K)
    gates = p_top * top_p / jnp.sum(top_p, axis=-1, keepdims=True)
    expert_idx = grp * EXPERTS_PER_GROUP + top_i

    A = N * TOP_K
    n_blocks = -(-A // MOE_BLOCK) + N_EXPERTS
    P = n_blocks * MOE_BLOCK
    flat_e = expert_idx.reshape(-1)
    order = jnp.argsort(flat_e)
    sorted_e = flat_e[order]
    counts = jnp.bincount(flat_e, length=N_EXPERTS)
    padded = (counts + MOE_BLOCK - 1) // MOE_BLOCK * MOE_BLOCK
    pad_end = jnp.cumsum(padded)
    pad_start = pad_end - padded
    start = jnp.cumsum(counts) - counts
    dest = pad_start[sorted_e] + jnp.arange(A) - start[sorted_e]
    slot_token = jnp.full((P,), N, jnp.int32).at[dest].set((order // TOP_K).astype(jnp.int32))
    slot_gate = jnp.zeros((P,), gates.dtype).at[dest].set(gates.reshape(-1)[order])
    block_expert = jnp.minimum(jnp.searchsorted(pad_end, jnp.arange(n_blocks) * MOE_BLOCK, side='right'), N_EXPERTS - 1)
    xs = jnp.concatenate([xf, jnp.zeros((1, D), xf.dtype)], axis=0)[slot_token].reshape(n_blocks, MOE_BLOCK, D)

    def expert_block(args):
        xb, e = args
        return (jax.nn.silu(xb @ w1[e]) * (xb @ w3[e])) @ w2[e]

    ys = lax.map(expert_block, (xs, block_expert)).reshape(P, D)
    ys = ys * slot_gate[:, None].astype(ys.dtype)
    out = jnp.zeros((N + 1, D), ys.dtype).at[slot_token].add(ys)[:N]
    return out.reshape(B, T, D).astype(h.dtype)


def setup_inputs(seed: int = 0) -> dict:
    key = jax.random.key(seed)
    ks = iter(jax.random.split(key, 48))

    def normal(shape, scale):
        return scale * jax.random.normal(next(ks), shape, F32)

    def gain(shape):
        return 1.0 + normal(shape, 0.05)

    even_cols = sum(EVEN_SPLITS)
    odd_cols = sum(ODD_SPLITS)
    return {
        "x": normal((BATCH, SEQ, D_MODEL), 1.0),
        "meta_tokens": normal((N_META, D_MODEL), 1.0),
        "ab_norm": gain((N_EVEN, D_MODEL)),
        "ab_w_in": normal((N_EVEN, D_MODEL, even_cols), D_MODEL ** -0.5),
        "a_conv_w": normal((N_EVEN, CONV_K, 2 * A_HEADS * A_DK), CONV_K ** -0.5),
        "a_conv_b": normal((N_EVEN, 2 * A_HEADS * A_DK), 0.02),
        "a_b_i": normal((N_EVEN, A_HEADS), 0.1),
        "a_b_f": jnp.linspace(3.0, 6.0, A_HEADS, dtype=F32) + normal((N_EVEN, A_HEADS), 0.1),
        "a_head_norm": gain((N_EVEN, A_HEADS * A_DV)),
        "b_w_gate2": normal((N_EVEN, GATE_RANK, B_HEADS * B_DK), GATE_RANK ** -0.5),
        "b_b_gate": normal((N_EVEN, B_HEADS * B_DK), 0.1),
        "b_head_norm": gain((N_EVEN, B_HEADS * B_DV)),
        "ab_w_out": normal((N_EVEN, D_MODEL, D_MODEL), D_MODEL ** -0.5),
        "cd_norm": gain((N_ODD, D_MODEL)),
        "cd_w_in": normal((N_ODD, D_MODEL, odd_cols), D_MODEL ** -0.5),
        "c_lower_bound": normal((DEPTH, C_HEADS * C_DK), 0.5),
        "c_head_norm": gain((N_ODD, C_HEADS * C_DV)),
        "d_q_a_norm": gain((N_ODD, Q_LORA)),
        "d_w_q_up": normal((N_ODD, Q_LORA, D_HEADS * (D_NOPE + D_ROPE)), Q_LORA ** -0.5),
        "d_kv_a_norm": gain((N_ODD, KV_LORA)),
        "d_w_kv_up": normal((N_ODD, KV_LORA, D_HEADS * (D_NOPE + D_V)), KV_LORA ** -0.5),
        "d_q_norm": gain((N_ODD, D_NOPE + D_ROPE)),
        "d_k_norm": gain((N_ODD, D_NOPE + D_ROPE)),
        "cd_w_out": normal((N_ODD, D_MODEL, D_MODEL), D_MODEL ** -0.5),
        "moe_norm": gain((DEPTH, D_MODEL)),
        "moe_w_group": normal((DEPTH, D_MODEL, N_GROUPS), D_MODEL ** -0.5),
        "moe_b_group": normal((DEPTH, N_GROUPS), 0.01),
        "moe_w_expert": normal((DEPTH, D_MODEL, N_EXPERTS), D_MODEL ** -0.5),
        "moe_b_expert": normal((DEPTH, N_EXPERTS), 0.01),
        "moe_w1": normal((DEPTH, N_EXPERTS, D_MODEL, D_EXPERT), D_MODEL ** -0.5),
        "moe_w3": normal((DEPTH, N_EXPERTS, D_MODEL, D_EXPERT), D_MODEL ** -0.5),
        "moe_w2": normal((DEPTH, N_EXPERTS, D_EXPERT, D_MODEL), D_EXPERT ** -0.5),
    }


def reference(x, meta_tokens, ab_norm, ab_w_in, a_conv_w, a_conv_b, a_b_i, a_b_f, a_head_norm,
              b_w_gate2, b_b_gate, b_head_norm, ab_w_out, cd_norm, cd_w_in, c_lower_bound, c_head_norm,
              d_q_a_norm, d_w_q_up, d_kv_a_norm, d_w_kv_up, d_q_norm, d_k_norm, cd_w_out,
              moe_norm, moe_w_group, moe_b_group, moe_w_expert, moe_b_expert, moe_w1, moe_w3, moe_w2):
    B = x.shape[0]
    h = jnp.concatenate([jnp.broadcast_to(meta_tokens.astype(x.dtype)[None], (B, N_META, D_MODEL)), x], axis=1)
    T = h.shape[1]
    pos = jnp.arange(T, dtype=F32)
    lb_cum = jnp.cumsum(jax.nn.softmax(c_lower_bound.astype(F32), axis=0), axis=0)
    lower_bounds = lb_cum - lb_cum[0]
    for layer in range(DEPTH):
        j = layer // 2
        if layer % 2 == 0:
            h = even_layer(h, ab_norm[j], ab_w_in[j], a_conv_w[j], a_conv_b[j], a_b_i[j], a_b_f[j], a_head_norm[j],
                           b_w_gate2[j], b_b_gate[j], b_head_norm[j], ab_w_out[j])
        else:
            h = odd_layer(h, pos, lower_bounds[layer], cd_norm[j], cd_w_in[j], c_head_norm[j], d_q_a_norm[j],
                          d_w_q_up[j], d_kv_a_norm[j], d_w_kv_up[j], d_q_norm[j], d_k_norm[j], cd_w_out[j])
        h = h + hier_moe(rms_norm(h, moe_norm[layer]), moe_w_group[layer], moe_b_group[layer], moe_w_expert[layer],
                         moe_b_expert[layer], moe_w1[layer], moe_w3[layer], moe_w2[layer])
    return h[:, N_META:]
```

```python
import functools
import math

import jax
import jax.numpy as jnp
from jax import lax
from jax.experimental import pallas as pl
from jax.experimental.pallas import tpu as pltpu

F32 = jnp.float32
BF16 = jnp.bfloat16
HIGHEST = lax.Precision.HIGHEST

D_MODEL = 2048
N_META = 16
CHUNK = 64
CONV_K = 4
EPS = 1e-6
A_HEADS, A_DK, A_DV = 4, 128, 256
B_HEADS, B_DK, B_DV = 4, 128, 256
GATE_RANK = 16
GATE_TAU = 16.0
C_HEADS, C_DK, C_DV = 8, 128, 128
D_HEADS, D_NOPE, D_ROPE, D_V = 8, 128, 64, 128
Q_LORA, KV_LORA = 512, 256
ROPE_THETA = 10000.0
N_GROUPS, EXPERTS_PER_GROUP = 4, 8
N_EXPERTS = N_GROUPS * EXPERTS_PER_GROUP
TOP_K = 2
D_EXPERT = 512

LANES = 128
MXU_DIM = 256
BF16_ROWS = 16
VMEM_LIMIT = 56 * 1024 * 1024
MOE_BLOCK = 128
ATT_BLOCK = 256
ATT_PAD = ATT_BLOCK - N_META

_NT = (((1,), (1,)), ((), ()))
_TN = (((0,), (0,)), ((), ()))


def _dot(a, b, precision=None):
    return jnp.dot(a, b, preferred_element_type=F32, precision=precision)


def _dot_nt(a, b):
    return lax.dot_general(a, b, _NT, preferred_element_type=F32)


def _dot_tn(a, b):
    return lax.dot_general(a, b, _TN, preferred_element_type=F32)


def _bf(x):
    return x.astype(BF16)


def _log_sigmoid(x):
    return jnp.minimum(x, 0.0) - jnp.log1p(jnp.exp(-jnp.abs(x)))


def _sigmoid(x):
    return 1.0 / (1.0 + jnp.exp(-x))


def _silu(x):
    return x * _sigmoid(x)


def _row_tile(m, cap):
    best = None
    for t in range(BF16_ROWS, min(m, cap) + 1, BF16_ROWS):
        if m % t == 0:
            best = t
    assert best is not None, m
    return best


def _col_tile(n, cap):
    best = None
    for t in range(MXU_DIM, min(n, cap) + 1, MXU_DIM):
        if n % t == 0:
            best = t
    assert best is not None, n
    return best


def _params(*sem):
    return pltpu.CompilerParams(dimension_semantics=sem, vmem_limit_bytes=VMEM_LIMIT)


def _normproj_kernel(x_ref, g_ref, w_ref, o_ref, xs_ref):
    tm = xs_ref.shape[0]

    @pl.when(pl.program_id(1) == 0)
    def _():
        def body(c, carry):
            r0 = pl.multiple_of(c * BF16_ROWS, BF16_ROWS)
            x = x_ref[pl.ds(r0, BF16_ROWS), :]
            ms = jnp.mean(x * x, axis=-1, keepdims=True)
            xs_ref[pl.ds(r0, BF16_ROWS), :] = _bf(x * lax.rsqrt(ms + EPS) * g_ref[...])
            return carry

        lax.fori_loop(0, tm // BF16_ROWS, body, 0)

    o_ref[...] = _dot(xs_ref[...], w_ref[...]).astype(o_ref.dtype)


def _normproj(x2d, gain, w, *, x_col_block=0, out_dtype=F32):
    m = x2d.shape[0]
    k, n = w.shape
    tm = _row_tile(m, 688)
    tn = _col_tile(n, 1280)
    return pl.pallas_call(
        _normproj_kernel,
        grid=(m // tm, n // tn),
        in_specs=[
            pl.BlockSpec((tm, k), lambda i, j: (i, x_col_block)),
            pl.BlockSpec((1, k), lambda i, j: (0, 0)),
            pl.BlockSpec((k, tn), lambda i, j: (0, j)),
        ],
        out_specs=pl.BlockSpec((tm, tn), lambda i, j: (i, j)),
        out_shape=jax.ShapeDtypeStruct((m, n), out_dtype),
        scratch_shapes=[pltpu.VMEM((tm, k), BF16)],
        compiler_params=_params("parallel", "arbitrary"),
        name="normproj",
    )(x2d, gain.reshape(1, k).astype(F32), w)


def _outproj_kernel(ya_ref, yb_ref, w_ref, r_ref, o_ref):
    ka = ya_ref.shape[1]
    acc = _dot(ya_ref[...], w_ref[:ka, :]) + _dot(yb_ref[...], w_ref[ka:, :])
    o_ref[...] = r_ref[...] + acc


def _outproj(ya, yb, w, res):
    m, ka = ya.shape
    kb = yb.shape[1]
    n = w.shape[1]
    tm = _row_tile(m, 1376)
    tn = _col_tile(n, 1024)
    return pl.pallas_call(
        _outproj_kernel,
        grid=(m // tm, n // tn),
        in_specs=[
            pl.BlockSpec((tm, ka), lambda i, j: (i, 0)),
            pl.BlockSpec((tm, kb), lambda i, j: (i, 0)),
            pl.BlockSpec((ka + kb, tn), lambda i, j: (0, j)),
            pl.BlockSpec((tm, tn), lambda i, j: (i, j)),
        ],
        out_specs=pl.BlockSpec((tm, tn), lambda i, j: (i, j)),
        out_shape=jax.ShapeDtypeStruct((m, n), F32),
        compiler_params=_params("parallel", "arbitrary"),
        name="outproj",
    )(ya, yb, w, res)


def _mlstm_kernel(bi_ref, bf_ref, q_ref, k_ref, v_ref, og_ref, aic_ref, afc_ref, air_ref, afr_ref,
                  cwq_ref, cwk_ref, cbq_ref, cbk_ref, hn_ref, o_ref, c_ref, n_ref, m_ref):
    t_total = q_ref.shape[1]
    n_chunks = (t_total - N_META) // CHUNK
    head = pl.program_id(1)
    b_i = bi_ref[head]
    b_f = bf_ref[head]
    cwq, cwk = cwq_ref[...], cwk_ref[...]
    cbq, cbk = cbq_ref[...], cbk_ref[...]
    gain = hn_ref[...]

    c_ref[...] = jnp.zeros_like(c_ref)
    n_ref[...] = jnp.zeros_like(n_ref)
    m_ref[...] = jnp.zeros_like(m_ref)

    def conv(win, cw, cb, length):
        y = cb
        for j in range(CONV_K):
            y = y + win[8 - (CONV_K - 1) + j:8 - (CONV_K - 1) + j + length, :] * cw[j:j + 1, :]
        return _silu(y)

    def chunk(o, ci, length, qwin, kwin):
        q = conv(qwin, cwq, cbq, length)
        k = conv(kwin, cwk, cbk, length) * (A_DK ** -0.5)
        v = v_ref[0, pl.ds(o, length), :]
        ig_c = aic_ref[0, 0, pl.ds(o, length), :] + b_i
        lf_c = _log_sigmoid(afc_ref[0, 0, pl.ds(o, length), :] + b_f)
        ig_r = air_ref[0, 0, pl.ds(ci, 1), :][:, :length] + b_i
        lf_r = _log_sigmoid(afr_ref[0, 0, pl.ds(ci, 1), :][:, :length] + b_f)

        row = lax.broadcasted_iota(jnp.int32, (length, length), 0)
        col = lax.broadcasted_iota(jnp.int32, (length, length), 1)
        causal = col <= row
        b_c = jnp.sum(jnp.where(causal, lf_r, 0.0), axis=1, keepdims=True)
        b_r = jnp.sum(jnp.where(row <= col, lf_c, 0.0), axis=0, keepdims=True)
        b_end = b_c[length - 1:length, :]

        m_in = m_ref[...]
        c_in = c_ref[...]
        n_in = n_ref[...]

        w_end = b_end - b_c + ig_c
        m_loc = jnp.max(w_end, axis=0, keepdims=True)
        k_w = k * jnp.exp(w_end - m_loc)
        c_loc = _dot_tn(_bf(k_w), _bf(v))
        n_loc = jnp.sum(k_w, axis=0, keepdims=True)

        d = jnp.where(causal, b_c - b_r + ig_r, -jnp.inf)
        inter = b_c + m_in
        m_t = jnp.maximum(inter, jnp.max(d, axis=1, keepdims=True))
        s = _dot_nt(_bf(q), _bf(k)) * jnp.exp(d - m_t)
        a_t = jnp.exp(inter - m_t)
        num = _dot(_bf(s), _bf(v)) + a_t * _dot(_bf(q), _bf(c_in))
        den = jnp.sum(s, axis=1, keepdims=True) + a_t * jnp.sum(q * n_in, axis=1, keepdims=True)
        h = num / jnp.maximum(jnp.abs(den), jnp.exp(-m_t))

        m_new = jnp.maximum(b_end + m_in, m_loc)
        a = jnp.exp(b_end + m_in - m_new)
        c = jnp.exp(m_loc - m_new)
        c_ref[...] = a * c_in + c * c_loc
        n_ref[...] = a * n_in + c * n_loc
        m_ref[...] = m_new

        hn = h * lax.rsqrt(jnp.mean(h * h, axis=-1, keepdims=True) + EPS) * gain
        y = _sigmoid(og_ref[0, pl.ds(o, length), :]) * hn
        o_ref[0, pl.ds(o, length), :] = y.astype(o_ref.dtype)

    zeros8 = jnp.zeros((8, A_DK), F32)
    chunk(0, 0, N_META,
          jnp.concatenate([zeros8, q_ref[0, 0:N_META, :]], axis=0),
          jnp.concatenate([zeros8, k_ref[0, 0:N_META, :]], axis=0))

    def body(c, carry):
        o = pl.multiple_of(N_META + c * CHUNK, BF16_ROWS)
        w0 = pl.multiple_of(N_META - 8 + c * CHUNK, 8)
        chunk(o, c + 1, CHUNK, q_ref[0, pl.ds(w0, CHUNK + 8), :], k_ref[0, pl.ds(w0, CHUNK + 8), :])
        return carry

    lax.fori_loop(0, n_chunks, body, 0)


def _mlstm(z, ai_c, af_c, ai_r, af_r, conv_w, conv_b, b_i, b_f, head_norm):
    b, t, _ = z.shape
    hk = A_HEADS * A_DK
    smem = pl.BlockSpec(memory_space=pltpu.SMEM)
    col = lambda blk: (lambda bi, h: (bi, 0, blk + h))
    return pl.pallas_call(
        _mlstm_kernel,
        grid=(b, A_HEADS),
        in_specs=[
            smem, smem,
            pl.BlockSpec((1, t, A_DK), col(0)),
            pl.BlockSpec((1, t, A_DK), col(hk // A_DK)),
            pl.BlockSpec((1, t, A_DV), col(2 * hk // A_DV)),
            pl.BlockSpec((1, t, A_DV), col((2 * hk + A_HEADS * A_DV) // A_DV)),
            pl.BlockSpec((1, 1, t, 1), lambda bi, h: (bi, h, 0, 0)),
            pl.BlockSpec((1, 1, t, 1), lambda bi, h: (bi, h, 0, 0)),
            pl.BlockSpec((1, 1) + ai_r.shape[2:], lambda bi, h: (bi, h, 0, 0)),
            pl.BlockSpec((1, 1) + af_r.shape[2:], lambda bi, h: (bi, h, 0, 0)),
            pl.BlockSpec((CONV_K, A_DK), lambda bi, h: (0, h)),
            pl.BlockSpec((CONV_K, A_DK), lambda bi, h: (0, A_HEADS + h)),
            pl.BlockSpec((1, A_DK), lambda bi, h: (0, h)),
            pl.BlockSpec((1, A_DK), lambda bi, h: (0, A_HEADS + h)),
            pl.BlockSpec((1, A_DV), lambda bi, h: (0, h)),
        ],
        out_specs=pl.BlockSpec((1, t, A_DV), lambda bi, h: (bi, 0, h)),
        out_shape=jax.ShapeDtypeStruct((b, t, A_HEADS * A_DV), BF16),
        scratch_shapes=[pltpu.VMEM((A_DK, A_DV), F32), pltpu.VMEM((1, A_DK), F32), pltpu.VMEM((1, 1), F32)],
        compiler_params=_params("parallel", "parallel"),
        name="mlstm",
    )(b_i, b_f, z, z, z, z, ai_c, af_c, ai_r, af_r, conv_w, conv_w, conv_b.reshape(1, -1), conv_b.reshape(1, -1),
      head_norm.reshape(1, -1))


def _gla_kernel(q_ref, k_ref, v_ref, og_ref, g_ref, p0_ref, p1_ref, p2_ref, hn_ref, o_ref, st_ref, *, mode):
    t_total = q_ref.shape[1]
    n_chunks = (t_total - N_META) // CHUNK
    dk = q_ref.shape[2]
    gain = hn_ref[...]
    st_ref[...] = jnp.zeros_like(st_ref)

    def chunk(o, length):
        q = q_ref[0, pl.ds(o, length), :]
        v = v_ref[0, pl.ds(o, length), :]
        if mode == "gla":
            pre = _dot(g_ref[0, pl.ds(o, length), :], p0_ref[0], precision=HIGHEST) + p1_ref[...]
            lg = _log_sigmoid(pre) / GATE_TAU
            q = q * (dk ** -0.5)
            k = k_ref[0, pl.ds(o, length), :]
        else:
            fpre = g_ref[0, pl.ds(o, length), :]
            a = p0_ref[...]
            bb = p1_ref[...] + _log_sigmoid(fpre)
            hi = jnp.maximum(a, bb)
            lg = hi + jnp.log1p(jnp.exp(-jnp.abs(a - bb)))
            k = p2_ref[...] * _sigmoid(-fpre)

        row = lax.broadcasted_iota(jnp.int32, (length, length), 0)
        col = lax.broadcasted_iota(jnp.int32, (length, length), 1)
        causal = col <= row
        g = _dot(causal.astype(F32), lg, precision=HIGHEST)
        g_end = g[length - 1:length, :]
        g_mid = g[length // 2:length // 2 + 1, :]
        s = jnp.where(causal, _dot_nt(_bf(q * jnp.exp(g - g_mid)), _bf(k * jnp.exp(g_mid - g))), 0.0)
        vb = _bf(v)
        st_in = st_ref[...]
        out = _dot(_bf(s), vb) + _dot_nt(_bf(q * jnp.exp(g)), _bf(st_in))
        st_ref[...] = st_in * jnp.exp(g_end) + _dot_tn(vb, _bf(k * jnp.exp(g_end - g)))

        hn = out * lax.rsqrt(jnp.mean(out * out, axis=-1, keepdims=True) + EPS) * gain
        og = og_ref[0, pl.ds(o, length), :]
        gate = _silu(og) if mode == "gla" else _sigmoid(og)
        o_ref[0, pl.ds(o, length), :] = (gate * hn).astype(o_ref.dtype)

    chunk(0, N_META)

    def body(c, carry):
        chunk(pl.multiple_of(N_META + c * CHUNK, BF16_ROWS), CHUNK)
        return carry

    lax.fori_loop(0, n_chunks, body, 0)


def _gla_call(z, heads, dk, dv, blocks, gate_width, params, head_norm, mode):
    b, t, _ = z.shape
    q0, k0, v0, og0, g0 = blocks
    zspec = lambda width, off, per_head=True: pl.BlockSpec(
        (1, t, width), (lambda bi, h: (bi, 0, off // width + (h if per_head else 0))))
    (p0, s0), (p1, s1), (p2, s2) = params
    return pl.pallas_call(
        functools.partial(_gla_kernel, mode=mode),
        grid=(b, heads),
        in_specs=[
            zspec(dk, q0), zspec(dk, k0), zspec(dv, v0), zspec(dv, og0),
            zspec(gate_width, g0, per_head=(mode != "gla")),
            s0, s1, s2,
            pl.BlockSpec((1, dv), lambda bi, h: (0, h)),
        ],
        out_specs=pl.BlockSpec((1, t, dv), lambda bi, h: (bi, 0, h)),
        out_shape=jax.ShapeDtypeStruct((b, t, heads * dv), BF16),
        scratch_shapes=[pltpu.VMEM((dv, dk), F32)],
        compiler_params=_params("parallel", "parallel"),
        name="gla_" + mode,
    )(z, z, z, z, z, p0, p1, p2, head_norm.reshape(1, -1))


def _mla_kernel(qn_ref, qr_ref, kn_ref, v_ref, kr_ref, tab_ref, gqn_ref, gqr_ref, gkn_ref, gkr_ref,
                o_ref, qf_ref, kf_ref, vf_ref):
    t_total = qn_ref.shape[1]
    n_blocks = (ATT_PAD + t_total) // ATT_BLOCK
    dqk = D_NOPE + D_ROPE
    scale = dqk ** -0.5
    rows = 3 * BF16_ROWS
    lane = lax.broadcasted_iota(jnp.int32, (rows, LANES), 1)
    first_half = lane < D_ROPE

    qf_ref[0:ATT_PAD, :] = jnp.zeros((ATT_PAD, 2 * LANES), BF16)
    kf_ref[0:ATT_PAD, :] = jnp.zeros((ATT_PAD, 2 * LANES), BF16)
    vf_ref[0:ATT_PAD, :] = jnp.zeros((ATT_PAD, D_V), BF16)

    def rope_pair(x, gains, tab):
        p = x * gains * tab
        return p + pltpu.roll(p, D_ROPE, 1)

    def prep(c, carry):
        r0 = pl.multiple_of(c * rows, BF16_ROWS)
        dst = pl.multiple_of(ATT_PAD + c * rows, BF16_ROWS)
        tab = tab_ref[pl.ds(r0, rows), :]
        qn = qn_ref[0, pl.ds(r0, rows), :]
        qr = qr_ref[0, pl.ds(r0, rows), :]
        ssq = jnp.sum(qn * qn, axis=-1, keepdims=True) + 0.5 * jnp.sum(qr * qr, axis=-1, keepdims=True)
        rq = lax.rsqrt(ssq / dqk + EPS) * scale
        qf_ref[pl.ds(dst, rows), 0:LANES] = _bf(qn * gqn_ref[...] * rq)
        qf_ref[pl.ds(dst, rows), LANES:2 * LANES] = _bf(rope_pair(qr, gqr_ref[...], tab) * rq)
        kn = kn_ref[0, pl.ds(r0, rows), :]
        kr = kr_ref[0, pl.ds(r0, rows), :]
        ssk = jnp.sum(kn * kn, axis=-1, keepdims=True) + 0.5 * jnp.sum(kr * kr, axis=-1, keepdims=True)
        rk = lax.rsqrt(ssk / dqk + EPS)
        kf_ref[pl.ds(dst, rows), 0:LANES] = _bf(kn * gkn_ref[...] * rk)
        kf_ref[pl.ds(dst, rows), LANES:2 * LANES] = _bf(
            jnp.where(first_half, rope_pair(kr, gkr_ref[...], tab) * rk, 0.0))
        vf_ref[pl.ds(dst, rows), :] = _bf(v_ref[0, pl.ds(r0, rows), :])
        return carry

    lax.fori_loop(0, t_total // rows, prep, 0)

    qpos = lax.broadcasted_iota(jnp.int32, (ATT_BLOCK, ATT_BLOCK), 0)
    kpos = lax.broadcasted_iota(jnp.int32, (ATT_BLOCK, ATT_BLOCK), 1)

    def attend(qi):
        q0 = pl.multiple_of(qi * ATT_BLOCK, ATT_BLOCK)
        q = qf_ref[pl.ds(q0, ATT_BLOCK), :]

        def kv_step(kj, carry):
            m, l, acc = carry
            k0 = pl.multiple_of(kj * ATT_BLOCK, ATT_BLOCK)
            s = _dot_nt(q, kf_ref[pl.ds(k0, ATT_BLOCK), :])
            ok = (kpos + k0 <= qpos + q0) & (kpos + k0 >= ATT_PAD)
            s = jnp.where(ok, s, -jnp.inf)
            m_new = jnp.maximum(m, jnp.max(s, axis=-1, keepdims=True))
            m_safe = jnp.where(m_new == -jnp.inf, 0.0, m_new)
            p = jnp.exp(s - m_safe)
            alpha = jnp.exp(m - m_safe)
            l = alpha * l + jnp.sum(p, axis=-1, keepdims=True)
            acc = alpha * acc + _dot(_bf(p), vf_ref[pl.ds(k0, ATT_BLOCK), :])
            return m_new, l, acc

        init = (jnp.full((ATT_BLOCK, 1), -jnp.inf, F32), jnp.zeros((ATT_BLOCK, 1), F32),
                jnp.zeros((ATT_BLOCK, D_V), F32))
        m, l, acc = lax.fori_loop(0, qi + 1, kv_step, init)
        return acc / jnp.where(l == 0.0, 1.0, l)

    out0 = attend(0)
    o_ref[0, 0:N_META, :] = out0[ATT_PAD:, :].astype(o_ref.dtype)

    def q_step(qi, carry):
        out = attend(qi)
        dst = pl.multiple_of(qi * ATT_BLOCK - ATT_PAD, BF16_ROWS)
        o_ref[0, pl.ds(dst, ATT_BLOCK), :] = out.astype(o_ref.dtype)
        return carry

    lax.fori_loop(1, n_blocks, q_step, 0)


def _mla(qn, kvn, z, kr_col, tab, gqn, gqr, gkn, gkr):
    b, t, _ = qn.shape
    assert (ATT_PAD + t) % ATT_BLOCK == 0 and t % (3 * BF16_ROWS) == 0
    tp = ATT_PAD + t
    hspec = lambda off: pl.BlockSpec((1, t, LANES), lambda bi, h: (bi, 0, off + h))
    gspec = pl.BlockSpec((1, LANES), lambda bi, h: (0, 0))
    return pl.pallas_call(
        _mla_kernel,
        grid=(b, D_HEADS),
        in_specs=[
            hspec(0), hspec(D_HEADS), hspec(0), hspec(D_HEADS),
            pl.BlockSpec((1, t, LANES), lambda bi, h: (bi, 0, kr_col // LANES)),
            pl.BlockSpec((t, LANES), lambda bi, h: (0, 0)),
            gspec, gspec, gspec, gspec,
        ],
        out_specs=pl.BlockSpec((1, t, D_V), lambda bi, h: (bi, 0, h)),
        out_shape=jax.ShapeDtypeStruct((b, t, D_HEADS * D_V), BF16),
        scratch_shapes=[pltpu.VMEM((tp, 2 * LANES), BF16), pltpu.VMEM((tp, 2 * LANES), BF16),
                        pltpu.VMEM((tp, D_V), BF16)],
        compiler_params=_params("parallel", "parallel"),
        name="mla",
    )(qn, qn, kvn, kvn, z, tab, gqn, gqr, gkn, gkr)


def _router_kernel(x_ref, g_ref, w_ref, b_ref, gate_ref, idx_ref):
    x = x_ref[...]
    ms = jnp.mean(x * x, axis=-1, keepdims=True)
    xn = x * lax.rsqrt(ms + EPS) * g_ref[...]
    logits = _dot(xn, w_ref[...], precision=HIGHEST) + b_ref[...]
    lane = lax.broadcasted_iota(jnp.int32, logits.shape, 1)
    lane_f = lane.astype(F32)
    neg = -jnp.inf
    big = float(LANES)

    is_group = lane < N_GROUPS
    g_max = jnp.max(jnp.where(is_group, logits, neg), axis=-1, keepdims=True)
    g_sum = jnp.sum(jnp.where(is_group, jnp.exp(logits - g_max), 0.0), axis=-1, keepdims=True)
    p_top = 1.0 / g_sum
    grp = jnp.min(jnp.where(is_group & (logits == g_max), lane_f, big), axis=-1, keepdims=True)

    e_lo = N_GROUPS + grp * EXPERTS_PER_GROUP
    in_grp = (lane_f >= e_lo) & (lane_f < e_lo + EXPERTS_PER_GROUP)
    e_max = jnp.max(jnp.where(in_grp, logits, neg), axis=-1, keepdims=True)
    e_sum = jnp.sum(jnp.where(in_grp, jnp.exp(logits - e_max), 0.0), axis=-1, keepdims=True)
    i1 = jnp.min(jnp.where(in_grp & (logits == e_max), lane_f, big), axis=-1, keepdims=True)
    rest = in_grp & (lane_f != i1)
    e_2nd = jnp.max(jnp.where(rest, logits, neg), axis=-1, keepdims=True)
    i2 = jnp.min(jnp.where(rest & (logits == e_2nd), lane_f, big), axis=-1, keepdims=True)
    p1 = 1.0 / e_sum
    p2 = jnp.exp(e_2nd - e_max) / e_sum
    tot = p1 + p2
    gate_ref[...] = jnp.where(lane == 0, p_top * p1 / tot, jnp.where(lane == 1, p_top * p2 / tot, 0.0))
    idx_ref[...] = jnp.where(lane == 0, i1 - N_GROUPS, jnp.where(lane == 1, i2 - N_GROUPS, 0.0)).astype(jnp.int32)


def _router(x2d, gain, w_group, b_group, w_expert, b_expert):
    m, d = x2d.shape
    tm = _row_tile(m, 688)
    pad = LANES - N_GROUPS - N_EXPERTS
    w = jnp.concatenate([w_group, w_expert, jnp.zeros((d, pad), F32)], axis=1)
    bias = jnp.concatenate([b_group, b_expert, jnp.zeros((pad,), F32)]).reshape(1, LANES)
    return pl.pallas_call(
        _router_kernel,
        grid=(m // tm,),
        in_specs=[
            pl.BlockSpec((tm, d), lambda i: (i, 0)),
            pl.BlockSpec((1, d), lambda i: (0, 0)),
            pl.BlockSpec((d, LANES), lambda i: (0, 0)),
            pl.BlockSpec((1, LANES), lambda i: (0, 0)),
        ],
        out_specs=[pl.BlockSpec((tm, LANES), lambda i: (i, 0)), pl.BlockSpec((tm, LANES), lambda i: (i, 0))],
        out_shape=[jax.ShapeDtypeStruct((m, LANES), F32), jax.ShapeDtypeStruct((m, LANES), jnp.int32)],
        compiler_params=_params("parallel"),
        name="router",
    )(x2d, gain.reshape(1, d), w, bias)


def _expert_kernel(be_ref, nu_ref, src_ref, dst_ref, gate_ref, g_ref, x_hbm, w1_ref, w3_ref, w2_ref, o_hbm,
                   xbuf, ybuf, w1s, w3s, w2s, sem_in, sem_out):
    i = pl.program_id(0)

    def gather_copy(r, row):
        return pltpu.make_async_copy(x_hbm.at[pl.ds(row, 1), :], xbuf.at[pl.ds(r, 1), :], sem_in)

    def scatter_copy(r, row):
        return pltpu.make_async_copy(ybuf.at[pl.ds(r, 1), :], o_hbm.at[pl.ds(row, 1), :], sem_out)

    @pl.when(i < nu_ref[0])
    def _():
        def issue(r, carry):
            gather_copy(r, src_ref[0, 0, r]).start()
            return carry

        lax.fori_loop(0, MOE_BLOCK, issue, 0)

        @pl.when((i == 0) | (be_ref[i] != be_ref[jnp.maximum(i - 1, 0)]))
        def _():
            w1s[...] = _bf(w1_ref[0])
            w3s[...] = _bf(w3_ref[0])
            w2s[...] = _bf(w2_ref[0])

        def wait_in(r, carry):
            gather_copy(r, 0).wait()
            return carry

        lax.fori_loop(0, MOE_BLOCK, wait_in, 0)

        x = xbuf[...]
        ms = jnp.mean(x * x, axis=-1, keepdims=True)
        xb = _bf(x * lax.rsqrt(ms + EPS) * g_ref[...])
        h1 = _dot(xb, w1s[...])
        h3 = _dot(xb, w3s[...])
        y = _dot(_bf(_silu(h1) * h3), w2s[...])
        ybuf[...] = y * gate_ref[0]

        def scatter(r, carry):
            row = dst_ref[0, 0, r]

            @pl.when(row >= 0)
            def _():
                scatter_copy(r, row).start()

            return carry

        lax.fori_loop(0, MOE_BLOCK, scatter, 0)

        def wait_out(r, carry):
            @pl.when(dst_ref[0, 0, r] >= 0)
            def _():
                scatter_copy(r, 0).wait()

            return carry

        lax.fori_loop(0, MOE_BLOCK, wait_out, 0)


def _moe(x2d, gain, w_group, b_group, w_expert, b_expert, w1, w3, w2):
    n, d = x2d.shape
    gates_l, idx_l = _router(x2d, gain, w_group, b_group, w_expert, b_expert)
    gates = gates_l[:, :TOP_K]
    expert_idx = idx_l[:, :TOP_K]

    a = n * TOP_K
    n_blocks = -(-a // MOE_BLOCK) + N_EXPERTS
    p = n_blocks * MOE_BLOCK
    flat_e = expert_idx.reshape(-1)
    order = jnp.argsort(flat_e)
    sorted_e = flat_e[order]
    counts = jnp.bincount(flat_e, length=N_EXPERTS)
    padded = (counts + MOE_BLOCK - 1) // MOE_BLOCK * MOE_BLOCK
    pad_end = jnp.cumsum(padded)
    pad_start = pad_end - padded
    start = jnp.cumsum(counts) - counts
    dest = pad_start[sorted_e] + jnp.arange(a) - start[sorted_e]
    tok = (order // TOP_K).astype(jnp.int32)
    kk = (order % TOP_K).astype(jnp.int32)
    src_row = jnp.zeros((p,), jnp.int32).at[dest].set(tok)
    dst_row = jnp.full((p,), -1, jnp.int32).at[dest].set(kk * n + tok)
    slot_gate = jnp.zeros((p,), F32).at[dest].set(gates.reshape(-1)[order])
    block_expert = jnp.minimum(
        jnp.searchsorted(pad_end, jnp.arange(n_blocks) * MOE_BLOCK, side="right"), N_EXPERTS - 1).astype(jnp.int32)
    n_used = (pad_end[-1] // MOE_BLOCK).astype(jnp.int32).reshape(1)

    idx_spec = pl.BlockSpec((1, 1, MOE_BLOCK), lambda i, be, nu: (i, 0, 0), memory_space=pltpu.SMEM)
    wspec = lambda shape: pl.BlockSpec((1,) + shape, lambda i, be, nu: (be[i], 0, 0))
    out2 = pl.pallas_call(
        _expert_kernel,
        grid_spec=pltpu.PrefetchScalarGridSpec(
            num_scalar_prefetch=2,
            grid=(n_blocks,),
            in_specs=[
                idx_spec, idx_spec,
                pl.BlockSpec((1, MOE_BLOCK, 1), lambda i, be, nu: (i, 0, 0)),
                pl.BlockSpec((1, d), lambda i, be, nu: (0, 0)),
                pl.BlockSpec(memory_space=pl.ANY),
                wspec((d, D_EXPERT)), wspec((d, D_EXPERT)), wspec((D_EXPERT, d)),
            ],
            out_specs=pl.BlockSpec(memory_space=pl.ANY),
            scratch_shapes=[
                pltpu.VMEM((MOE_BLOCK, d), F32), pltpu.VMEM((MOE_BLOCK, d), F32),
                pltpu.VMEM((d, D_EXPERT), BF16), pltpu.VMEM((d, D_EXPERT), BF16), pltpu.VMEM((D_EXPERT, d), BF16),
                pltpu.SemaphoreType.DMA, pltpu.SemaphoreType.DMA,
            ],
        ),
        out_shape=jax.ShapeDtypeStruct((TOP_K * n, d), F32),
        compiler_params=_params("arbitrary"),
        name="moe_experts",
    )(block_expert, n_used, src_row.reshape(n_blocks, 1, MOE_BLOCK), dst_row.reshape(n_blocks, 1, MOE_BLOCK),
      slot_gate.reshape(n_blocks, MOE_BLOCK, 1), gain.reshape(1, d), x2d, w1, w3, w2)
    return _combine(x2d, out2.reshape(TOP_K, n, d))


def _combine_kernel(x_ref, a_ref, b_ref, o_ref):
    o_ref[...] = x_ref[...] + (a_ref[0] + b_ref[0])


def _combine(x2d, out2):
    n, d = x2d.shape
    tm = _row_tile(n, 688)
    return pl.pallas_call(
        _combine_kernel,
        grid=(n // tm,),
        in_specs=[
            pl.BlockSpec((tm, d), lambda i: (i, 0)),
            pl.BlockSpec((1, tm, d), lambda i: (0, i, 0)),
            pl.BlockSpec((1, tm, d), lambda i: (1, i, 0)),
        ],
        out_specs=pl.BlockSpec((tm, d), lambda i: (i, 0)),
        out_shape=jax.ShapeDtypeStruct((n, d), F32),
        compiler_params=_params("parallel"),
        name="moe_combine",
    )(x2d, out2, out2)


def _gate_layouts(cols, heads, t):
    b = cols.shape[0]
    rows = cols.transpose(0, 2, 1)
    meta = jnp.pad(rows[:, :, :N_META], ((0, 0), (0, 0), (0, CHUNK - N_META)))
    real = rows[:, :, N_META:].reshape(b, heads, -1, CHUNK)
    return rows[..., None], jnp.concatenate([meta[:, :, None, :], real], axis=2)


def _even_layer(x, norm_g, w_in, conv_w, conv_b, b_i, b_f, a_norm, w_gate2, b_gate, b_norm, w_out):
    b, t, d = x.shape
    n = b * t
    a_w = 2 * A_HEADS * A_DK + 2 * A_HEADS * A_DV
    g_w = 2 * A_HEADS
    b_w = 2 * B_HEADS * B_DK + 2 * B_HEADS * B_DV
    main = a_w + b_w
    gate_cols = g_w + GATE_RANK
    w = jnp.concatenate([w_in[:, :a_w], w_in[:, a_w + g_w:a_w + g_w + b_w], w_in[:, a_w:a_w + g_w],
                         w_in[:, a_w + g_w + b_w:], jnp.zeros((d, MXU_DIM - gate_cols), F32)], axis=1).astype(BF16)
    z = _normproj(x.reshape(n, d), norm_g, w).reshape(b, t, main + MXU_DIM)

    gates = z[:, :, main:main + g_w]
    ai_c, ai_r = _gate_layouts(gates[..., :A_HEADS], A_HEADS, t)
    af_c, af_r = _gate_layouts(gates[..., A_HEADS:], A_HEADS, t)
    ya = _mlstm(z, ai_c, af_c, ai_r, af_r, conv_w, conv_b, b_i, b_f, a_norm)

    wg = jnp.zeros((B_HEADS, MXU_DIM, B_DK), F32).at[:, g_w:g_w + GATE_RANK, :].set(
        w_gate2.reshape(GATE_RANK, B_HEADS, B_DK).transpose(1, 0, 2))
    dummy = jnp.zeros((1, B_HEADS * B_DK), F32)
    hspec = pl.BlockSpec((1, B_DK), lambda bi, h: (0, h))
    yb = _gla_call(
        z, B_HEADS, B_DK, B_DV,
        (a_w, a_w + B_HEADS * B_DK, a_w + 2 * B_HEADS * B_DK, a_w + 2 * B_HEADS * B_DK + B_HEADS * B_DV, main),
        MXU_DIM,
        ((wg, pl.BlockSpec((1, MXU_DIM, B_DK), lambda bi, h: (h, 0, 0))),
         (b_gate.reshape(1, -1), hspec), (dummy, hspec)),
        b_norm, "gla")
    return _outproj(ya.reshape(n, -1), yb.reshape(n, -1), w_out.astype(BF16), x.reshape(n, d)).reshape(b, t, d)


def _odd_layer(x, lb, norm_g, w_in, c_norm, q_a_norm, w_q_up, kv_a_norm, w_kv_up, q_norm, k_norm, w_out):
    b, t, d = x.shape
    n = b * t
    c_w = 2 * C_HEADS * C_DK + 2 * C_HEADS * C_DV
    swap = (jnp.arange(D_ROPE) + D_ROPE // 2) % D_ROPE
    kr0 = c_w + Q_LORA + KV_LORA
    used = kr0 + 2 * D_ROPE
    total = -(-used // MXU_DIM) * MXU_DIM
    w = jnp.concatenate([w_in, w_in[:, kr0:kr0 + D_ROPE][:, swap], jnp.zeros((d, total - used), F32)],
                        axis=1).astype(BF16)
    z2 = _normproj(x.reshape(n, d), norm_g, w)
    z = z2.reshape(b, t, total)

    hspec = pl.BlockSpec((1, C_DK), lambda bi, h: (0, h))
    yc = _gla_call(
        z, C_HEADS, C_DK, C_DV,
        (0, C_HEADS * C_DK, 2 * C_HEADS * C_DK, 2 * C_HEADS * C_DK + C_HEADS * C_DV, C_HEADS * C_DK),
        C_DK,
        ((jnp.log(lb).reshape(1, -1), hspec), (jnp.log1p(-lb).reshape(1, -1), hspec), ((1.0 - lb).reshape(1, -1), hspec)),
        c_norm, "hgrn")

    dq = D_NOPE + D_ROPE
    wq = w_q_up.reshape(Q_LORA, D_HEADS, dq)
    wq_rope = wq[:, :, D_NOPE:]
    wq_p = jnp.concatenate([wq[:, :, :D_NOPE].reshape(Q_LORA, -1),
                            jnp.concatenate([wq_rope, wq_rope[:, :, swap]], axis=-1).reshape(Q_LORA, -1)],
                           axis=1).astype(BF16)
    wkv = w_kv_up.reshape(KV_LORA, D_HEADS, D_NOPE + D_V)
    wkv_p = jnp.concatenate([wkv[:, :, :D_NOPE].reshape(KV_LORA, -1), wkv[:, :, D_NOPE:].reshape(KV_LORA, -1)],
                            axis=1).astype(BF16)
    qn = _normproj(z2, q_a_norm, wq_p, x_col_block=c_w // Q_LORA).reshape(b, t, -1)
    kvn = _normproj(z2, kv_a_norm, wkv_p, x_col_block=(c_w + Q_LORA) // KV_LORA).reshape(b, t, -1)

    pos = jnp.arange(t, dtype=F32)
    half = D_ROPE // 2
    inv = ROPE_THETA ** (-jnp.arange(half, dtype=F32) / half)
    ang = pos[:, None] * inv[None, :]
    cos, sin = jnp.cos(ang), jnp.sin(ang)
    tab = jnp.concatenate([cos, cos, -sin, sin], axis=1)
    pair = lambda g: jnp.concatenate([g[D_NOPE:], g[D_NOPE:][swap]]).reshape(1, LANES)
    yd = _mla(qn, kvn, z, kr0, tab, q_norm[:D_NOPE].reshape(1, LANES), pair(q_norm),
              k_norm[:D_NOPE].reshape(1, LANES), pair(k_norm))
    return _outproj(yc.reshape(n, -1), yd.reshape(n, -1), w_out.astype(BF16), x.reshape(n, d)).reshape(b, t, d)


def kernel(x, meta_tokens, ab_norm, ab_w_in, a_conv_w, a_conv_b, a_b_i, a_b_f, a_head_norm, b_w_gate2, b_b_gate, b_head_norm, ab_w_out, cd_norm, cd_w_in, c_lower_bound, c_head_norm, d_q_a_norm, d_w_q_up, d_kv_a_norm, d_w_kv_up, d_q_norm, d_k_norm, cd_w_out, moe_norm, moe_w_group, moe_b_group, moe_w_expert, moe_b_expert, moe_w1, moe_w3, moe_w2):
    b = x.shape[0]
    depth = moe_norm.shape[0]
    h = jnp.concatenate([jnp.broadcast_to(meta_tokens.astype(x.dtype)[None], (b, N_META, D_MODEL)), x], axis=1)
    t = h.shape[1]
    lb_cum = jnp.cumsum(jax.nn.softmax(c_lower_bound.astype(F32), axis=0), axis=0)
    lower_bounds = lb_cum - lb_cum[0]
    for layer in range(depth):
        j = layer // 2
        if layer % 2 == 0:
            h = _even_layer(h, ab_norm[j], ab_w_in[j], a_conv_w[j], a_conv_b[j], a_b_i[j], a_b_f[j], a_head_norm[j],
                            b_w_gate2[j], b_b_gate[j], b_head_norm[j], ab_w_out[j])
        else:
            h = _odd_layer(h, lower_bounds[layer], cd_norm[j], cd_w_in[j], c_head_norm[j], d_q_a_norm[j],
                           d_w_q_up[j], d_kv_a_norm[j], d_w_kv_up[j], d_q_norm[j], d_k_norm[j], cd_w_out[j])
        h = _moe(h.reshape(b * t, D_MODEL), moe_norm[layer], moe_w_group[layer], moe_b_group[layer],
                 moe_w_expert[layer], moe_b_expert[layer], moe_w1[layer], moe_w3[layer], moe_w2[layer]
                 ).reshape(b, t, D_MODEL)
    return h[:, N_META:]
```

```python
import functools
import math

import jax
import jax.numpy as jnp
from jax import lax
from jax.experimental import pallas as pl
from jax.experimental.pallas import tpu as pltpu

F32 = jnp.float32
BF16 = jnp.bfloat16
HIGHEST = lax.Precision.HIGHEST

D_MODEL = 2048
N_META = 16
CHUNK = 64
CONV_K = 4
EPS = 1e-6
A_HEADS, A_DK, A_DV = 4, 128, 256
B_HEADS, B_DK, B_DV = 4, 128, 256
GATE_RANK = 16
GATE_TAU = 16.0
C_HEADS, C_DK, C_DV = 8, 128, 128
D_HEADS, D_NOPE, D_ROPE, D_V = 8, 128, 64, 128
Q_LORA, KV_LORA = 512, 256
ROPE_THETA = 10000.0
N_GROUPS, EXPERTS_PER_GROUP = 4, 8
N_EXPERTS = N_GROUPS * EXPERTS_PER_GROUP
TOP_K = 2
D_EXPERT = 512

LANES = 128
MXU_DIM = 256
BF16_ROWS = 16
VMEM_LIMIT = 56 * 1024 * 1024
MOE_BLOCK = MXU_DIM
ATT_BLOCK = 256
ATT_PAD = ATT_BLOCK - N_META

_NT = (((1,), (1,)), ((), ()))
_TN = (((0,), (0,)), ((), ()))


def _dot(a, b, precision=None):
    return jnp.dot(a, b, preferred_element_type=F32, precision=precision)


def _dot_nt(a, b):
    return lax.dot_general(a, b, _NT, preferred_element_type=F32)


def _dot_tn(a, b):
    return lax.dot_general(a, b, _TN, preferred_element_type=F32)


def _bf(x):
    return x.astype(BF16)


def _log_sigmoid(x):
    return jnp.minimum(x, 0.0) - jnp.log1p(jnp.exp(-jnp.abs(x)))


def _sigmoid(x):
    return 1.0 / (1.0 + jnp.exp(-x))


def _silu(x):
    return x * _sigmoid(x)


def _row_tile(m, cap):
    best = None
    for t in range(BF16_ROWS, min(m, cap) + 1, BF16_ROWS):
        if m % t == 0:
            best = t
    assert best is not None, m
    return best


def _col_tile(n, cap):
    best = None
    for t in range(MXU_DIM, min(n, cap) + 1, MXU_DIM):
        if n % t == 0:
            best = t
    assert best is not None, n
    return best


def _params(*sem):
    return pltpu.CompilerParams(dimension_semantics=sem, vmem_limit_bytes=VMEM_LIMIT)


def _normproj_kernel(x_ref, g_ref, w_ref, o_ref, xs_ref):
    tm = xs_ref.shape[0]

    @pl.when(pl.program_id(1) == 0)
    def _():
        def body(c, carry):
            r0 = pl.multiple_of(c * BF16_ROWS, BF16_ROWS)
            x = x_ref[pl.ds(r0, BF16_ROWS), :]
            ms = jnp.mean(x * x, axis=-1, keepdims=True)
            xs_ref[pl.ds(r0, BF16_ROWS), :] = _bf(x * lax.rsqrt(ms + EPS) * g_ref[...])
            return carry

        lax.fori_loop(0, tm // BF16_ROWS, body, 0)

    o_ref[...] = _dot(xs_ref[...], w_ref[...]).astype(o_ref.dtype)


def _normproj(x2d, gain, w, *, x_col_block=0, out_dtype=F32):
    m = x2d.shape[0]
    k, n = w.shape
    tm = _row_tile(m, 688)
    tn = _col_tile(n, 1280)
    return pl.pallas_call(
        _normproj_kernel,
        grid=(m // tm, n // tn),
        in_specs=[
            pl.BlockSpec((tm, k), lambda i, j: (i, x_col_block)),
            pl.BlockSpec((1, k), lambda i, j: (0, 0)),
            pl.BlockSpec((k, tn), lambda i, j: (0, j)),
        ],
        out_specs=pl.BlockSpec((tm, tn), lambda i, j: (i, j)),
        out_shape=jax.ShapeDtypeStruct((m, n), out_dtype),
        scratch_shapes=[pltpu.VMEM((tm, k), BF16)],
        compiler_params=_params("parallel", "arbitrary"),
        name="normproj",
    )(x2d, gain.reshape(1, k).astype(F32), w)


def _outproj_kernel(ya_ref, yb_ref, w_ref, r_ref, o_ref):
    ka = ya_ref.shape[1]
    acc = _dot(ya_ref[...], w_ref[:ka, :]) + _dot(yb_ref[...], w_ref[ka:, :])
    o_ref[...] = r_ref[...] + acc


def _outproj(ya, yb, w, res):
    m, ka = ya.shape
    kb = yb.shape[1]
    n = w.shape[1]
    tm = _row_tile(m, 1376)
    tn = _col_tile(n, 1024)
    return pl.pallas_call(
        _outproj_kernel,
        grid=(m // tm, n // tn),
        in_specs=[
            pl.BlockSpec((tm, ka), lambda i, j: (i, 0)),
            pl.BlockSpec((tm, kb), lambda i, j: (i, 0)),
            pl.BlockSpec((ka + kb, tn), lambda i, j: (0, j)),
            pl.BlockSpec((tm, tn), lambda i, j: (i, j)),
        ],
        out_specs=pl.BlockSpec((tm, tn), lambda i, j: (i, j)),
        out_shape=jax.ShapeDtypeStruct((m, n), F32),
        compiler_params=_params("parallel", "arbitrary"),
        name="outproj",
    )(ya, yb, w, res)


def _mlstm_kernel(bi_ref, bf_ref, q_ref, k_ref, v_ref, og_ref, aic_ref, afc_ref, air_ref, afr_ref,
                  cwq_ref, cwk_ref, cbq_ref, cbk_ref, hn_ref, o_ref, c_ref, n_ref, m_ref):
    t_total = q_ref.shape[1]
    n_chunks = (t_total - N_META) // CHUNK
    head = pl.program_id(1)
    b_i = bi_ref[head]
    b_f = bf_ref[head]
    cwq, cwk = cwq_ref[...], cwk_ref[...]
    cbq, cbk = cbq_ref[...], cbk_ref[...]
    gain = hn_ref[...]

    c_ref[...] = jnp.zeros_like(c_ref)
    n_ref[...] = jnp.zeros_like(n_ref)
    m_ref[...] = jnp.zeros_like(m_ref)

    def conv(win, cw, cb, length):
        y = cb
        for j in range(CONV_K):
            y = y + win[8 - (CONV_K - 1) + j:8 - (CONV_K - 1) + j + length, :] * cw[j:j + 1, :]
        return _silu(y)

    def chunk(o, ci, length, qwin, kwin):
        q = conv(qwin, cwq, cbq, length)
        k = conv(kwin, cwk, cbk, length) * (A_DK ** -0.5)
        v = v_ref[0, pl.ds(o, length), :]
        ig_c = aic_ref[0, 0, pl.ds(o, length), :] + b_i
        lf_c = _log_sigmoid(afc_ref[0, 0, pl.ds(o, length), :] + b_f)
        ig_r = air_ref[0, 0, pl.ds(ci, 1), :][:, :length] + b_i
        lf_r = _log_sigmoid(afr_ref[0, 0, pl.ds(ci, 1), :][:, :length] + b_f)

        row = lax.broadcasted_iota(jnp.int32, (length, length), 0)
        col = lax.broadcasted_iota(jnp.int32, (length, length), 1)
        causal = col <= row
        b_c = jnp.sum(jnp.where(causal, lf_r, 0.0), axis=1, keepdims=True)
        b_r = jnp.sum(jnp.where(row <= col, lf_c, 0.0), axis=0, keepdims=True)
        b_end = b_c[length - 1:length, :]

        m_in = m_ref[...]
        c_in = c_ref[...]
        n_in = n_ref[...]

        w_end = b_end - b_c + ig_c
        m_loc = jnp.max(w_end, axis=0, keepdims=True)
        k_w = k * jnp.exp(w_end - m_loc)
        c_loc = _dot_tn(_bf(k_w), _bf(v))
        n_loc = jnp.sum(k_w, axis=0, keepdims=True)

        d = jnp.where(causal, b_c - b_r + ig_r, -jnp.inf)
        inter = b_c + m_in
        m_t = jnp.maximum(inter, jnp.max(d, axis=1, keepdims=True))
        s = _dot_nt(_bf(q), _bf(k)) * jnp.exp(d - m_t)
        a_t = jnp.exp(inter - m_t)
        num = _dot(_bf(s), _bf(v)) + a_t * _dot(_bf(q), _bf(c_in))
        den = jnp.sum(s, axis=1, keepdims=True) + a_t * jnp.sum(q * n_in, axis=1, keepdims=True)
        h = num / jnp.maximum(jnp.abs(den), jnp.exp(-m_t))

        m_new = jnp.maximum(b_end + m_in, m_loc)
        a = jnp.exp(b_end + m_in - m_new)
        c = jnp.exp(m_loc - m_new)
        c_ref[...] = a * c_in + c * c_loc
        n_ref[...] = a * n_in + c * n_loc
        m_ref[...] = m_new

        hn = h * lax.rsqrt(jnp.mean(h * h, axis=-1, keepdims=True) + EPS) * gain
        y = _sigmoid(og_ref[0, pl.ds(o, length), :]) * hn
        o_ref[0, pl.ds(o, length), :] = y.astype(o_ref.dtype)

    zeros8 = jnp.zeros((8, A_DK), F32)
    chunk(0, 0, N_META,
          jnp.concatenate([zeros8, q_ref[0, 0:N_META, :]], axis=0),
          jnp.concatenate([zeros8, k_ref[0, 0:N_META, :]], axis=0))

    def body(c, carry):
        o = pl.multiple_of(N_META + c * CHUNK, BF16_ROWS)
        w0 = pl.multiple_of(N_META - 8 + c * CHUNK, 8)
        chunk(o, c + 1, CHUNK, q_ref[0, pl.ds(w0, CHUNK + 8), :], k_ref[0, pl.ds(w0, CHUNK + 8), :])
        return carry

    lax.fori_loop(0, n_chunks, body, 0)


def _mlstm(z, ai_c, af_c, ai_r, af_r, conv_w, conv_b, b_i, b_f, head_norm):
    b, t, _ = z.shape
    hk = A_HEADS * A_DK
    smem = pl.BlockSpec(memory_space=pltpu.SMEM)
    col = lambda blk: (lambda bi, h: (bi, 0, blk + h))
    return pl.pallas_call(
        _mlstm_kernel,
        grid=(b, A_HEADS),
        in_specs=[
            smem, smem,
            pl.BlockSpec((1, t, A_DK), col(0)),
            pl.BlockSpec((1, t, A_DK), col(hk // A_DK)),
            pl.BlockSpec((1, t, A_DV), col(2 * hk // A_DV)),
            pl.BlockSpec((1, t, A_DV), col((2 * hk + A_HEADS * A_DV) // A_DV)),
            pl.BlockSpec((1, 1, t, 1), lambda bi, h: (bi, h, 0, 0)),
            pl.BlockSpec((1, 1, t, 1), lambda bi, h: (bi, h, 0, 0)),
            pl.BlockSpec((1, 1) + ai_r.shape[2:], lambda bi, h: (bi, h, 0, 0)),
            pl.BlockSpec((1, 1) + af_r.shape[2:], lambda bi, h: (bi, h, 0, 0)),
            pl.BlockSpec((CONV_K, A_DK), lambda bi, h: (0, h)),
            pl.BlockSpec((CONV_K, A_DK), lambda bi, h: (0, A_HEADS + h)),
            pl.BlockSpec((1, A_DK), lambda bi, h: (0, h)),
            pl.BlockSpec((1, A_DK), lambda bi, h: (0, A_HEADS + h)),
            pl.BlockSpec((1, A_DV), lambda bi, h: (0, h)),
        ],
        out_specs=pl.BlockSpec((1, t, A_DV), lambda bi, h: (bi, 0, h)),
        out_shape=jax.ShapeDtypeStruct((b, t, A_HEADS * A_DV), BF16),
        scratch_shapes=[pltpu.VMEM((A_DK, A_DV), F32), pltpu.VMEM((1, A_DK), F32), pltpu.VMEM((1, 1), F32)],
        compiler_params=_params("parallel", "parallel"),
        name="mlstm",
    )(b_i, b_f, z, z, z, z, ai_c, af_c, ai_r, af_r, conv_w, conv_w, conv_b.reshape(1, -1), conv_b.reshape(1, -1),
      head_norm.reshape(1, -1))


def _gla_kernel(q_ref, k_ref, v_ref, og_ref, g_ref, p0_ref, p1_ref, p2_ref, hn_ref, o_ref, st_ref, *, mode):
    t_total = q_ref.shape[1]
    n_chunks = (t_total - N_META) // CHUNK
    dk = q_ref.shape[2]
    gain = hn_ref[...]
    st_ref[...] = jnp.zeros_like(st_ref)

    def chunk(o, length):
        q = q_ref[0, pl.ds(o, length), :]
        v = v_ref[0, pl.ds(o, length), :]
        if mode == "gla":
            pre = _dot(g_ref[0, pl.ds(o, length), :], p0_ref[0], precision=HIGHEST) + p1_ref[...]
            lg = _log_sigmoid(pre) / GATE_TAU
            q = q * (dk ** -0.5)
            k = k_ref[0, pl.ds(o, length), :]
        else:
            fpre = g_ref[0, pl.ds(o, length), :]
            a = p0_ref[...]
            bb = p1_ref[...] + _log_sigmoid(fpre)
            hi = jnp.maximum(a, bb)
            lg = hi + jnp.log1p(jnp.exp(-jnp.abs(a - bb)))
            k = p2_ref[...] * _sigmoid(-fpre)

        row = lax.broadcasted_iota(jnp.int32, (length, length), 0)
        col = lax.broadcasted_iota(jnp.int32, (length, length), 1)
        causal = col <= row
        g = _dot(causal.astype(F32), lg, precision=HIGHEST)
        g_end = g[length - 1:length, :]
        g_mid = g[length // 2:length // 2 + 1, :]
        s = jnp.where(causal, _dot_nt(_bf(q * jnp.exp(g - g_mid)), _bf(k * jnp.exp(g_mid - g))), 0.0)
        vb = _bf(v)
        st_in = st_ref[...]
        out = _dot(_bf(s), vb) + _dot_nt(_bf(q * jnp.exp(g)), _bf(st_in))
        st_ref[...] = st_in * jnp.exp(g_end) + _dot_tn(vb, _bf(k * jnp.exp(g_end - g)))

        hn = out * lax.rsqrt(jnp.mean(out * out, axis=-1, keepdims=True) + EPS) * gain
        og = og_ref[0, pl.ds(o, length), :]
        gate = _silu(og) if mode == "gla" else _sigmoid(og)
        o_ref[0, pl.ds(o, length), :] = (gate * hn).astype(o_ref.dtype)

    chunk(0, N_META)

    def body(c, carry):
        chunk(pl.multiple_of(N_META + c * CHUNK, BF16_ROWS), CHUNK)
        return carry

    lax.fori_loop(0, n_chunks, body, 0)


def _gla_call(z, heads, dk, dv, blocks, gate_width, params, head_norm, mode):
    b, t, _ = z.shape
    q0, k0, v0, og0, g0 = blocks
    zspec = lambda width, off, per_head=True: pl.BlockSpec(
        (1, t, width), (lambda bi, h: (bi, 0, off // width + (h if per_head else 0))))
    (p0, s0), (p1, s1), (p2, s2) = params
    return pl.pallas_call(
        functools.partial(_gla_kernel, mode=mode),
        grid=(b, heads),
        in_specs=[
            zspec(dk, q0), zspec(dk, k0), zspec(dv, v0), zspec(dv, og0),
            zspec(gate_width, g0, per_head=(mode != "gla")),
            s0, s1, s2,
            pl.BlockSpec((1, dv), lambda bi, h: (0, h)),
        ],
        out_specs=pl.BlockSpec((1, t, dv), lambda bi, h: (bi, 0, h)),
        out_shape=jax.ShapeDtypeStruct((b, t, heads * dv), BF16),
        scratch_shapes=[pltpu.VMEM((dv, dk), F32)],
        compiler_params=_params("parallel", "parallel"),
        name="gla_" + mode,
    )(z, z, z, z, z, p0, p1, p2, head_norm.reshape(1, -1))


def _mla_kernel(qn_ref, qr_ref, kn_ref, v_ref, kr_ref, tab_ref, gqn_ref, gqr_ref, gkn_ref, gkr_ref,
                o_ref, qf_ref, kf_ref, vf_ref):
    t_total = qn_ref.shape[1]
    n_blocks = (ATT_PAD + t_total) // ATT_BLOCK
    dqk = D_NOPE + D_ROPE
    scale = dqk ** -0.5
    rows = 3 * BF16_ROWS
    lane = lax.broadcasted_iota(jnp.int32, (rows, LANES), 1)
    first_half = lane < D_ROPE

    qf_ref[0:ATT_PAD, :] = jnp.zeros((ATT_PAD, 2 * LANES), BF16)
    kf_ref[0:ATT_PAD, :] = jnp.zeros((ATT_PAD, 2 * LANES), BF16)
    vf_ref[0:ATT_PAD, :] = jnp.zeros((ATT_PAD, D_V), BF16)

    def rope_pair(x, gains, tab):
        p = x * gains * tab
        return p + pltpu.roll(p, D_ROPE, 1)

    def prep(c, carry):
        r0 = pl.multiple_of(c * rows, BF16_ROWS)
        dst = pl.multiple_of(ATT_PAD + c * rows, BF16_ROWS)
        tab = tab_ref[pl.ds(r0, rows), :]
        qn = qn_ref[0, pl.ds(r0, rows), :]
        qr = qr_ref[0, pl.ds(r0, rows), :]
        ssq = jnp.sum(qn * qn, axis=-1, keepdims=True) + 0.5 * jnp.sum(qr * qr, axis=-1, keepdims=True)
        rq = lax.rsqrt(ssq / dqk + EPS) * scale
        qf_ref[pl.ds(dst, rows), 0:LANES] = _bf(qn * gqn_ref[...] * rq)
        qf_ref[pl.ds(dst, rows), LANES:2 * LANES] = _bf(rope_pair(qr, gqr_ref[...], tab) * rq)
        kn = kn_ref[0, pl.ds(r0, rows), :]
        kr = kr_ref[0, pl.ds(r0, rows), :]
        ssk = jnp.sum(kn * kn, axis=-1, keepdims=True) + 0.5 * jnp.sum(kr * kr, axis=-1, keepdims=True)
        rk = lax.rsqrt(ssk / dqk + EPS)
        kf_ref[pl.ds(dst, rows), 0:LANES] = _bf(kn * gkn_ref[...] * rk)
        kf_ref[pl.ds(dst, rows), LANES:2 * LANES] = _bf(
            jnp.where(first_half, rope_pair(kr, gkr_ref[...], tab) * rk, 0.0))
        vf_ref[pl.ds(dst, rows), :] = _bf(v_ref[0, pl.ds(r0, rows), :])
        return carry

    lax.fori_loop(0, t_total // rows, prep, 0)

    qpos = lax.broadcasted_iota(jnp.int32, (ATT_BLOCK, ATT_BLOCK), 0)
    kpos = lax.broadcasted_iota(jnp.int32, (ATT_BLOCK, ATT_BLOCK), 1)

    def attend(qi):
        q0 = pl.multiple_of(qi * ATT_BLOCK, ATT_BLOCK)
        q = qf_ref[pl.ds(q0, ATT_BLOCK), :]

        def kv_step(kj, carry):
            m, l, acc = carry
            k0 = pl.multiple_of(kj * ATT_BLOCK, ATT_BLOCK)
            s = _dot_nt(q, kf_ref[pl.ds(k0, ATT_BLOCK), :])
            ok = (kpos + k0 <= qpos + q0) & (kpos + k0 >= ATT_PAD)
            s = jnp.where(ok, s, -jnp.inf)
            m_new = jnp.maximum(m, jnp.max(s, axis=-1, keepdims=True))
            m_safe = jnp.where(m_new == -jnp.inf, 0.0, m_new)
            p = jnp.exp(s - m_safe)
            alpha = jnp.exp(m - m_safe)
            l = alpha * l + jnp.sum(p, axis=-1, keepdims=True)
            acc = alpha * acc + _dot(_bf(p), vf_ref[pl.ds(k0, ATT_BLOCK), :])
            return m_new, l, acc

        init = (jnp.full((ATT_BLOCK, 1), -jnp.inf, F32), jnp.zeros((ATT_BLOCK, 1), F32),
                jnp.zeros((ATT_BLOCK, D_V), F32))
        m, l, acc = lax.fori_loop(0, qi + 1, kv_step, init)
        return acc / jnp.where(l == 0.0, 1.0, l)

    out0 = attend(0)
    o_ref[0, 0:N_META, :] = out0[ATT_PAD:, :].astype(o_ref.dtype)

    def q_step(qi, carry):
        out = attend(qi)
        dst = pl.multiple_of(qi * ATT_BLOCK - ATT_PAD, BF16_ROWS)
        o_ref[0, pl.ds(dst, ATT_BLOCK), :] = out.astype(o_ref.dtype)
        return carry

    lax.fori_loop(1, n_blocks, q_step, 0)


def _mla(qn, kvn, z, kr_col, tab, gqn, gqr, gkn, gkr):
    b, t, _ = qn.shape
    assert (ATT_PAD + t) % ATT_BLOCK == 0 and t % (3 * BF16_ROWS) == 0
    tp = ATT_PAD + t
    hspec = lambda off: pl.BlockSpec((1, t, LANES), lambda bi, h: (bi, 0, off + h))
    gspec = pl.BlockSpec((1, LANES), lambda bi, h: (0, 0))
    return pl.pallas_call(
        _mla_kernel,
        grid=(b, D_HEADS),
        in_specs=[
            hspec(0), hspec(D_HEADS), hspec(0), hspec(D_HEADS),
            pl.BlockSpec((1, t, LANES), lambda bi, h: (bi, 0, kr_col // LANES)),
            pl.BlockSpec((t, LANES), lambda bi, h: (0, 0)),
            gspec, gspec, gspec, gspec,
        ],
        out_specs=pl.BlockSpec((1, t, D_V), lambda bi, h: (bi, 0, h)),
        out_shape=jax.ShapeDtypeStruct((b, t, D_HEADS * D_V), BF16),
        scratch_shapes=[pltpu.VMEM((tp, 2 * LANES), BF16), pltpu.VMEM((tp, 2 * LANES), BF16),
                        pltpu.VMEM((tp, D_V), BF16)],
        compiler_params=_params("parallel", "parallel"),
        name="mla",
    )(qn, qn, kvn, kvn, z, tab, gqn, gqr, gkn, gkr)


def _pack_bf16_pairs(v):
    w = v.shape[1] // 2
    bits = pltpu.bitcast(_bf(v).astype(F32), jnp.uint32)
    return (bits[:, :w] >> 16) | (bits[:, w:] & jnp.uint32(0xFFFF0000))


def _unpack_lo(words):
    return pltpu.bitcast(words << 16, F32)


def _unpack_hi(words):
    return pltpu.bitcast(words & jnp.uint32(0xFFFF0000), F32)


def _router_kernel(x_ref, g_ref, w_ref, b_ref, gate_ref, idx_ref, xg_ref, cnt_ref, carry_ref):
    tm = x_ref.shape[0]

    @pl.when(pl.program_id(0) == 0)
    def _():
        carry_ref[...] = jnp.zeros_like(carry_ref)

    x = x_ref[...]
    ms = jnp.mean(x * x, axis=-1, keepdims=True)
    xn = x * lax.rsqrt(ms + EPS) * g_ref[...]
    logits = _dot(xn, w_ref[...], precision=HIGHEST) + b_ref[...]
    lane = lax.broadcasted_iota(jnp.int32, logits.shape, 1)
    lane_f = lane.astype(F32)
    neg = -jnp.inf
    big = float(LANES)

    is_group = lane < N_GROUPS
    g_max = jnp.max(jnp.where(is_group, logits, neg), axis=-1, keepdims=True)
    g_sum = jnp.sum(jnp.where(is_group, jnp.exp(logits - g_max), 0.0), axis=-1, keepdims=True)
    p_top = 1.0 / g_sum
    grp = jnp.min(jnp.where(is_group & (logits == g_max), lane_f, big), axis=-1, keepdims=True)

    e_lo = N_GROUPS + grp * EXPERTS_PER_GROUP
    in_grp = (lane_f >= e_lo) & (lane_f < e_lo + EXPERTS_PER_GROUP)
    e_max = jnp.max(jnp.where(in_grp, logits, neg), axis=-1, keepdims=True)
    e_sum = jnp.sum(jnp.where(in_grp, jnp.exp(logits - e_max), 0.0), axis=-1, keepdims=True)
    i1 = jnp.min(jnp.where(in_grp & (logits == e_max), lane_f, big), axis=-1, keepdims=True)
    rest = in_grp & (lane_f != i1)
    e_2nd = jnp.max(jnp.where(rest, logits, neg), axis=-1, keepdims=True)
    i2 = jnp.min(jnp.where(rest & (logits == e_2nd), lane_f, big), axis=-1, keepdims=True)
    p1 = 1.0 / e_sum
    p2 = jnp.exp(e_2nd - e_max) / e_sum
    tot = p1 + p2
    gate_ref[...] = jnp.where(lane == 0, p_top * p1 / tot, jnp.where(lane == 1, p_top * p2 / tot, 0.0))

    e1 = i1 - N_GROUPS
    e2 = i2 - N_GROUPS
    hot = jnp.where((lane_f == e1) | (lane_f == e2), 1.0, 0.0)
    row = lax.broadcasted_iota(jnp.int32, (tm, tm), 0)
    col = lax.broadcasted_iota(jnp.int32, (tm, tm), 1)
    before = _dot(jnp.where(col < row, 1.0, 0.0).astype(BF16), _bf(hot)) + carry_ref[...]
    r1 = jnp.sum(jnp.where(lane_f == e1, before, 0.0), axis=-1, keepdims=True)
    r2 = jnp.sum(jnp.where(lane_f == e2, before, 0.0), axis=-1, keepdims=True)
    total = carry_ref[...] + jnp.sum(hot, axis=0, keepdims=True)
    carry_ref[...] = total
    cnt_ref[...] = jnp.broadcast_to(total, cnt_ref.shape).astype(jnp.int32)
    idx_ref[...] = jnp.where(lane == 0, e1, jnp.where(lane == 1, e2, jnp.where(lane == 2, r1, jnp.where(
        lane == 3, r2, 0.0)))).astype(jnp.int32)

    words = _pack_bf16_pairs(xn)
    for s in range(8):
        xg_ref[pl.ds(s, tm, stride=8), :] = words[:, s * LANES:(s + 1) * LANES]


def _router(x2d, gain, w_group, b_group, w_expert, b_expert):
    m, d = x2d.shape
    assert d == 2 * 8 * LANES
    tm = _row_tile(m, 688)
    pad = LANES - N_GROUPS - N_EXPERTS
    w = jnp.concatenate([w_group, w_expert, jnp.zeros((d, pad), F32)], axis=1)
    bias = jnp.concatenate([b_group, b_expert, jnp.zeros((pad,), F32)]).reshape(1, LANES)
    return pl.pallas_call(
        _router_kernel,
        grid=(m // tm,),
        in_specs=[
            pl.BlockSpec((tm, d), lambda i: (i, 0)),
            pl.BlockSpec((1, d), lambda i: (0, 0)),
            pl.BlockSpec((d, LANES), lambda i: (0, 0)),
            pl.BlockSpec((1, LANES), lambda i: (0, 0)),
        ],
        out_specs=[pl.BlockSpec((tm, LANES), lambda i: (i, 0)), pl.BlockSpec((tm, LANES), lambda i: (i, 0)),
                   pl.BlockSpec((tm * 8, LANES), lambda i: (i, 0)), pl.BlockSpec((8, LANES), lambda i: (0, 0))],
        out_shape=[jax.ShapeDtypeStruct((m, LANES), F32), jax.ShapeDtypeStruct((m, LANES), jnp.int32),
                   jax.ShapeDtypeStruct((m * 8, LANES), jnp.uint32), jax.ShapeDtypeStruct((8, LANES), jnp.int32)],
        scratch_shapes=[pltpu.VMEM((1, LANES), F32)],
        compiler_params=_params("arbitrary"),
        name="router",
    )(x2d, gain.reshape(1, d), w, bias)


def _invert_kernel(dest_ref, inv_ref):
    def clear(s, carry):
        inv_ref[s] = -1
        return carry

    lax.fori_loop(0, inv_ref.shape[0], clear, 0, unroll=8)

    def put(f, carry):
        inv_ref[dest_ref[f]] = f
        return carry

    lax.fori_loop(0, dest_ref.shape[0], put, 0, unroll=8)


def _invert(dest, p):
    assert p % 8 == 0 and dest.shape[0] % 8 == 0
    smem = pl.BlockSpec(memory_space=pltpu.SMEM)
    return pl.pallas_call(
        _invert_kernel, in_specs=[smem], out_specs=smem,
        out_shape=jax.ShapeDtypeStruct((p,), jnp.int32), name="moe_invert",
    )(dest)


def _expert_kernel(be_ref, nu_ref, cur_ref, nxt_ref, dst_ref, xg_hbm, w1_ref, w3_ref, w2_ref, o_hbm,
                   xbuf, ybuf, xs_ref, w1s, w3s, w2s, sem_in, sem_out):
    i = pl.program_id(0)
    n_used = nu_ref[0]
    par = i % 2
    half = D_MODEL // 2

    def issue_gathers(tok_ref, slot):
        def body(r, carry):
            src = pl.multiple_of(tok_ref[0, 0, r], 8)
            pltpu.make_async_copy(xg_hbm.at[pl.ds(src, 8), :], xbuf.at[slot, pl.ds(r * 8, 8), :],
                                  sem_in.at[slot]).start()
            return carry

        lax.fori_loop(0, MOE_BLOCK, body, 0, unroll=8)

    def wait_gathers(slot):
        pltpu.make_async_copy(xbuf.at[1 - slot], xbuf.at[slot], sem_in.at[slot]).wait()

    def wait_scatters(slot):
        pltpu.make_async_copy(ybuf.at[slot], ybuf.at[1 - slot], sem_out.at[slot]).wait()

    @pl.when(i == 0)
    def _():
        ybuf[1] = jnp.zeros(ybuf.shape[1:], ybuf.dtype)
        n_real = o_hbm.shape[0] - 2 * MOE_BLOCK * 8
        for q in range(2):
            fill = pltpu.make_async_copy(ybuf.at[1], o_hbm.at[pl.ds(n_real + q * MOE_BLOCK * 8, MOE_BLOCK * 8), :],
                                         sem_out.at[1])
            fill.start()
            fill.wait()

    @pl.when((i == 0) & (n_used > 0))
    def _():
        issue_gathers(cur_ref, 0)

    @pl.when(i + 1 < n_used)
    def _():
        issue_gathers(nxt_ref, 1 - par)

    @pl.when(i < n_used)
    def _():
        @pl.when((i == 0) | (be_ref[i] != be_ref[jnp.maximum(i - 1, 0)]))
        def _():
            w1s[...] = _bf(w1_ref[0])
            w3s[...] = _bf(w3_ref[0])
            w2s[...] = _bf(w2_ref[0])

        wait_gathers(par)

        for s in range(8):
            words = xbuf[par, pl.ds(s, MOE_BLOCK, stride=8), :]
            xs_ref[:, s * LANES:(s + 1) * LANES] = _bf(_unpack_lo(words))
            xs_ref[:, half + s * LANES:half + (s + 1) * LANES] = _bf(_unpack_hi(words))
        xb = xs_ref[...]
        h1 = _dot(xb, w1s[...])
        h3 = _dot(xb, w3s[...])
        y = _dot(_bf(_silu(h1) * h3), w2s[...])

        @pl.when(i >= 2)
        def _():
            wait_scatters(par)

        words = _pack_bf16_pairs(y)
        for s in range(8):
            ybuf[par, pl.ds(s, MOE_BLOCK, stride=8), :] = words[:, s * LANES:(s + 1) * LANES]

        def issue_scatter(r, carry):
            dst = pl.multiple_of(dst_ref[0, 0, r], 8)
            pltpu.make_async_copy(ybuf.at[par, pl.ds(r * 8, 8), :], o_hbm.at[pl.ds(dst, 8), :],
                                  sem_out.at[par]).start()
            return carry

        lax.fori_loop(0, MOE_BLOCK, issue_scatter, 0, unroll=8)

        @pl.when(i == n_used - 1)
        def _():
            wait_scatters(par)

            @pl.when(i >= 1)
            def _():
                wait_scatters(1 - par)


def _moe(x2d, gain, w_group, b_group, w_expert, b_expert, w1, w3, w2):
    n, d = x2d.shape
    gates_l, idx_l, xg, cnt = _router(x2d, gain, w_group, b_group, w_expert, b_expert)

    a = n * TOP_K
    n_blocks = -(-a // MOE_BLOCK) + N_EXPERTS
    p = n_blocks * MOE_BLOCK
    counts = cnt[0, :N_EXPERTS]
    padded = (counts + MOE_BLOCK - 1) // MOE_BLOCK * MOE_BLOCK
    pad_end = jnp.cumsum(padded)
    pad_start = pad_end - padded
    dest = (pad_start[idx_l[:, :TOP_K]] + idx_l[:, TOP_K:2 * TOP_K]).reshape(-1).astype(jnp.int32)
    blk0 = jnp.arange(n_blocks, dtype=jnp.int32) * MOE_BLOCK
    block_expert = jnp.minimum(jnp.searchsorted(pad_end, blk0, side="right"), N_EXPERTS - 1).astype(jnp.int32)
    n_used = (pad_end[-1] // MOE_BLOCK).astype(jnp.int32).reshape(1)
    codes = _invert(dest, p)
    slot = jnp.arange(p, dtype=jnp.int32)
    spare = TOP_K * n + (slot // MOE_BLOCK % 2) * MOE_BLOCK + slot % MOE_BLOCK
    src_tok = ((jnp.maximum(codes, 0) >> 1) * 8).reshape(n_blocks, 1, MOE_BLOCK)
    dst_row = (jnp.where(codes >= 0, (codes & 1) * n + (codes >> 1), spare) * 8).reshape(n_blocks, 1, MOE_BLOCK)

    code_spec = lambda step: pl.BlockSpec(
        (1, 1, MOE_BLOCK), lambda i, be, nu: (jnp.minimum(i + step, n_blocks - 1), 0, 0), memory_space=pltpu.SMEM)
    wspec = lambda shape: pl.BlockSpec((1,) + shape, lambda i, be, nu: (be[i], 0, 0))
    out_rows = TOP_K * n + 2 * MOE_BLOCK
    out2 = pl.pallas_call(
        _expert_kernel,
        grid_spec=pltpu.PrefetchScalarGridSpec(
            num_scalar_prefetch=2,
            grid=(n_blocks,),
            in_specs=[
                code_spec(0), code_spec(1), code_spec(0),
                pl.BlockSpec(memory_space=pl.ANY),
                wspec((d, D_EXPERT)), wspec((d, D_EXPERT)), wspec((D_EXPERT, d)),
            ],
            out_specs=pl.BlockSpec(memory_space=pl.ANY),
            scratch_shapes=[
                pltpu.VMEM((2, MOE_BLOCK * 8, LANES), jnp.uint32), pltpu.VMEM((2, MOE_BLOCK * 8, LANES), jnp.uint32),
                pltpu.VMEM((MOE_BLOCK, d), BF16),
                pltpu.VMEM((d, D_EXPERT), BF16), pltpu.VMEM((d, D_EXPERT), BF16), pltpu.VMEM((D_EXPERT, d), BF16),
                pltpu.SemaphoreType.DMA((2,)), pltpu.SemaphoreType.DMA((2,)),
            ],
        ),
        out_shape=jax.ShapeDtypeStruct((out_rows * 8, LANES), jnp.uint32),
        compiler_params=_params("arbitrary"),
        name="moe_experts",
    )(block_expert, n_used, src_tok, src_tok, dst_row, xg, w1, w3, w2)
    return _combine(x2d, gates_l, out2)


def _combine_kernel(x_ref, gate_ref, a_ref, b_ref, o_ref):
    tm = x_ref.shape[0]
    half = x_ref.shape[1] // 2
    g0 = gate_ref[:, 0:1]
    g1 = gate_ref[:, 1:2]
    for s in range(8):
        wa = a_ref[pl.ds(s, tm, stride=8), :]
        wb = b_ref[pl.ds(s, tm, stride=8), :]
        lo = slice(s * LANES, (s + 1) * LANES)
        hi = slice(half + s * LANES, half + (s + 1) * LANES)
        o_ref[:, lo] = x_ref[:, lo] + (g0 * _unpack_lo(wa) + g1 * _unpack_lo(wb))
        o_ref[:, hi] = x_ref[:, hi] + (g0 * _unpack_hi(wa) + g1 * _unpack_hi(wb))


def _combine(x2d, gates, out2):
    n, d = x2d.shape
    tm = _row_tile(n, 688)
    return pl.pallas_call(
        _combine_kernel,
        grid=(n // tm,),
        in_specs=[
            pl.BlockSpec((tm, d), lambda i: (i, 0)),
            pl.BlockSpec((tm, LANES), lambda i: (i, 0)),
            pl.BlockSpec((tm * 8, LANES), lambda i: (i, 0)),
            pl.BlockSpec((tm * 8, LANES), lambda i: (n // tm + i, 0)),
        ],
        out_specs=pl.BlockSpec((tm, d), lambda i: (i, 0)),
        out_shape=jax.ShapeDtypeStruct((n, d), F32),
        compiler_params=_params("parallel"),
        name="moe_combine",
    )(x2d, gates, out2, out2)


def _gate_layouts(cols, heads, t):
    b = cols.shape[0]
    rows = cols.transpose(0, 2, 1)
    meta = jnp.pad(rows[:, :, :N_META], ((0, 0), (0, 0), (0, CHUNK - N_META)))
    real = rows[:, :, N_META:].reshape(b, heads, -1, CHUNK)
    return rows[..., None], jnp.concatenate([meta[:, :, None, :], real], axis=2)


def _even_layer(x, norm_g, w_in, conv_w, conv_b, b_i, b_f, a_norm, w_gate2, b_gate, b_norm, w_out):
    b, t, d = x.shape
    n = b * t
    a_w = 2 * A_HEADS * A_DK + 2 * A_HEADS * A_DV
    g_w = 2 * A_HEADS
    b_w = 2 * B_HEADS * B_DK + 2 * B_HEADS * B_DV
    main = a_w + b_w
    gate_cols = g_w + GATE_RANK
    w = jnp.concatenate([w_in[:, :a_w], w_in[:, a_w + g_w:a_w + g_w + b_w], w_in[:, a_w:a_w + g_w],
                         w_in[:, a_w + g_w + b_w:], jnp.zeros((d, MXU_DIM - gate_cols), F32)], axis=1).astype(BF16)
    z = _normproj(x.reshape(n, d), norm_g, w).reshape(b, t, main + MXU_DIM)

    gates = z[:, :, main:main + g_w]
    ai_c, ai_r = _gate_layouts(gates[..., :A_HEADS], A_HEADS, t)
    af_c, af_r = _gate_layouts(gates[..., A_HEADS:], A_HEADS, t)
    ya = _mlstm(z, ai_c, af_c, ai_r, af_r, conv_w, conv_b, b_i, b_f, a_norm)

    wg = jnp.zeros((B_HEADS, MXU_DIM, B_DK), F32).at[:, g_w:g_w + GATE_RANK, :].set(
        w_gate2.reshape(GATE_RANK, B_HEADS, B_DK).transpose(1, 0, 2))
    dummy = jnp.zeros((1, B_HEADS * B_DK), F32)
    hspec = pl.BlockSpec((1, B_DK), lambda bi, h: (0, h))
    yb = _gla_call(
        z, B_HEADS, B_DK, B_DV,
        (a_w, a_w + B_HEADS * B_DK, a_w + 2 * B_HEADS * B_DK, a_w + 2 * B_HEADS * B_DK + B_HEADS * B_DV, main),
        MXU_DIM,
        ((wg, pl.BlockSpec((1, MXU_DIM, B_DK), lambda bi, h: (h, 0, 0))),
         (b_gate.reshape(1, -1), hspec), (dummy, hspec)),
        b_norm, "gla")
    return _outproj(ya.reshape(n, -1), yb.reshape(n, -1), w_out.astype(BF16), x.reshape(n, d)).reshape(b, t, d)


def _odd_layer(x, lb, norm_g, w_in, c_norm, q_a_norm, w_q_up, kv_a_norm, w_kv_up, q_norm, k_norm, w_out):
    b, t, d = x.shape
    n = b * t
    c_w = 2 * C_HEADS * C_DK + 2 * C_HEADS * C_DV
    swap = (jnp.arange(D_ROPE) + D_ROPE // 2) % D_ROPE
    kr0 = c_w + Q_LORA + KV_LORA
    used = kr0 + 2 * D_ROPE
    total = -(-used // MXU_DIM) * MXU_DIM
    w = jnp.concatenate([w_in, w_in[:, kr0:kr0 + D_ROPE][:, swap], jnp.zeros((d, total - used), F32)],
                        axis=1).astype(BF16)
    z2 = _normproj(x.reshape(n, d), norm_g, w)
    z = z2.reshape(b, t, total)

    hspec = pl.BlockSpec((1, C_DK), lambda bi, h: (0, h))
    yc = _gla_call(
        z, C_HEADS, C_DK, C_DV,
        (0, C_HEADS * C_DK, 2 * C_HEADS * C_DK, 2 * C_HEADS * C_DK + C_HEADS * C_DV, C_HEADS * C_DK),
        C_DK,
        ((jnp.log(lb).reshape(1, -1), hspec), (jnp.log1p(-lb).reshape(1, -1), hspec), ((1.0 - lb).reshape(1, -1), hspec)),
        c_norm, "hgrn")

    dq = D_NOPE + D_ROPE
    wq = w_q_up.reshape(Q_LORA, D_HEADS, dq)
    wq_rope = wq[:, :, D_NOPE:]
    wq_p = jnp.concatenate([wq[:, :, :D_NOPE].reshape(Q_LORA, -1),
                            jnp.concatenate([wq_rope, wq_rope[:, :, swap]], axis=-1).reshape(Q_LORA, -1)],
                           axis=1).astype(BF16)
    wkv = w_kv_up.reshape(KV_LORA, D_HEADS, D_NOPE + D_V)
    wkv_p = jnp.concatenate([wkv[:, :, :D_NOPE].reshape(KV_LORA, -1), wkv[:, :, D_NOPE:].reshape(KV_LORA, -1)],
                            axis=1).astype(BF16)
    qn = _normproj(z2, q_a_norm, wq_p, x_col_block=c_w // Q_LORA).reshape(b, t, -1)
    kvn = _normproj(z2, kv_a_norm, wkv_p, x_col_block=(c_w + Q_LORA) // KV_LORA).reshape(b, t, -1)

    pos = jnp.arange(t, dtype=F32)
    half = D_ROPE // 2
    inv = ROPE_THETA ** (-jnp.arange(half, dtype=F32) / half)
    ang = pos[:, None] * inv[None, :]
    cos, sin = jnp.cos(ang), jnp.sin(ang)
    tab = jnp.concatenate([cos, cos, -sin, sin], axis=1)
    pair = lambda g: jnp.concatenate([g[D_NOPE:], g[D_NOPE:][swap]]).reshape(1, LANES)
    yd = _mla(qn, kvn, z, kr0, tab, q_norm[:D_NOPE].reshape(1, LANES), pair(q_norm),
              k_norm[:D_NOPE].reshape(1, LANES), pair(k_norm))
    return _outproj(yc.reshape(n, -1), yd.reshape(n, -1), w_out.astype(BF16), x.reshape(n, d)).reshape(b, t, d)


def kernel(x, meta_tokens, ab_norm, ab_w_in, a_conv_w, a_conv_b, a_b_i, a_b_f, a_head_norm, b_w_gate2, b_b_gate, b_head_norm, ab_w_out, cd_norm, cd_w_in, c_lower_bound, c_head_norm, d_q_a_norm, d_w_q_up, d_kv_a_norm, d_w_kv_up, d_q_norm, d_k_norm, cd_w_out, moe_norm, moe_w_group, moe_b_group, moe_w_expert, moe_b_expert, moe_w1, moe_w3, moe_w2):
    b = x.shape[0]
    depth = moe_norm.shape[0]
    h = jnp.concatenate([jnp.broadcast_to(meta_tokens.astype(x.dtype)[None], (b, N_META, D_MODEL)), x], axis=1)
    t = h.shape[1]
    lb_cum = jnp.cumsum(jax.nn.softmax(c_lower_bound.astype(F32), axis=0), axis=0)
    lower_bounds = lb_cum - lb_cum[0]
    for layer in range(depth):
        j = layer // 2
        if layer % 2 == 0:
            h = _even_layer(h, ab_norm[j], ab_w_in[j], a_conv_w[j], a_conv_b[j], a_b_i[j], a_b_f[j], a_head_norm[j],
                            b_w_gate2[j], b_b_gate[j], b_head_norm[j], ab_w_out[j])
        else:
            h = _odd_layer(h, lower_bounds[layer], cd_norm[j], cd_w_in[j], c_head_norm[j], d_q_a_norm[j],
                           d_w_q_up[j], d_kv_a_norm[j], d_w_kv_up[j], d_q_norm[j], d_k_norm[j], cd_w_out[j])
        h = _moe(h.reshape(b * t, D_MODEL), moe_norm[layer], moe_w_group[layer], moe_b_group[layer],
                 moe_w_expert[layer], moe_b_expert[layer], moe_w1[layer], moe_w3[layer], moe_w2[layer]
                 ).reshape(b, t, D_MODEL)
    return h[:, N_META:]
```

```python
import functools
import math

import jax
import jax.numpy as jnp
from jax import lax
from jax.experimental import pallas as pl
from jax.experimental.pallas import tpu as pltpu

F32 = jnp.float32
BF16 = jnp.bfloat16
HIGHEST = lax.Precision.HIGHEST

D_MODEL = 2048
N_META = 16
CHUNK = 64
CONV_K = 4
EPS = 1e-6
A_HEADS, A_DK, A_DV = 4, 128, 256
B_HEADS, B_DK, B_DV = 4, 128, 256
GATE_RANK = 16
GATE_TAU = 16.0
C_HEADS, C_DK, C_DV = 8, 128, 128
D_HEADS, D_NOPE, D_ROPE, D_V = 8, 128, 64, 128
Q_LORA, KV_LORA = 512, 256
ROPE_THETA = 10000.0
N_GROUPS, EXPERTS_PER_GROUP = 4, 8
N_EXPERTS = N_GROUPS * EXPERTS_PER_GROUP
TOP_K = 2
D_EXPERT = 512

LANES = 128
MXU_DIM = 256
BF16_ROWS = 16
VMEM_LIMIT = 56 * 1024 * 1024
MOE_BLOCK = MXU_DIM
ATT_BLOCK = 256
ATT_PAD = ATT_BLOCK - N_META

_NT = (((1,), (1,)), ((), ()))
_TN = (((0,), (0,)), ((), ()))


def _dot(a, b, precision=None):
    return jnp.dot(a, b, preferred_element_type=F32, precision=precision)


def _dot_nt(a, b):
    return lax.dot_general(a, b, _NT, preferred_element_type=F32)


def _dot_tn(a, b):
    return lax.dot_general(a, b, _TN, preferred_element_type=F32)


def _bf(x):
    return x.astype(BF16)


def _log_sigmoid(x):
    return jnp.minimum(x, 0.0) - jnp.log1p(jnp.exp(-jnp.abs(x)))


def _sigmoid(x):
    return 1.0 / (1.0 + jnp.exp(-x))


def _silu(x):
    return x * _sigmoid(x)


def _row_tile(m, cap):
    best = None
    for t in range(BF16_ROWS, min(m, cap) + 1, BF16_ROWS):
        if m % t == 0:
            best = t
    assert best is not None, m
    return best


def _col_tile(n, cap):
    best = None
    for t in range(MXU_DIM, min(n, cap) + 1, MXU_DIM):
        if n % t == 0:
            best = t
    assert best is not None, n
    return best


def _params(*sem):
    return pltpu.CompilerParams(dimension_semantics=sem, vmem_limit_bytes=VMEM_LIMIT)


def _normproj_kernel(x_ref, g_ref, w_ref, o_ref, xs_ref):
    tm = xs_ref.shape[0]

    @pl.when(pl.program_id(1) == 0)
    def _():
        def body(c, carry):
            r0 = pl.multiple_of(c * BF16_ROWS, BF16_ROWS)
            x = x_ref[pl.ds(r0, BF16_ROWS), :]
            ms = jnp.mean(x * x, axis=-1, keepdims=True)
            xs_ref[pl.ds(r0, BF16_ROWS), :] = _bf(x * lax.rsqrt(ms + EPS) * g_ref[...])
            return carry

        lax.fori_loop(0, tm // BF16_ROWS, body, 0)

    o_ref[...] = _dot(xs_ref[...], w_ref[...]).astype(o_ref.dtype)


def _normproj(x2d, gain, w, *, x_col_block=0, out_dtype=F32):
    m = x2d.shape[0]
    k, n = w.shape
    tm = _row_tile(m, 688)
    tn = _col_tile(n, 1280)
    return pl.pallas_call(
        _normproj_kernel,
        grid=(m // tm, n // tn),
        in_specs=[
            pl.BlockSpec((tm, k), lambda i, j: (i, x_col_block)),
            pl.BlockSpec((1, k), lambda i, j: (0, 0)),
            pl.BlockSpec((k, tn), lambda i, j: (0, j)),
        ],
        out_specs=pl.BlockSpec((tm, tn), lambda i, j: (i, j)),
        out_shape=jax.ShapeDtypeStruct((m, n), out_dtype),
        scratch_shapes=[pltpu.VMEM((tm, k), BF16)],
        compiler_params=_params("parallel", "arbitrary"),
        name="normproj",
    )(x2d, gain.reshape(1, k).astype(F32), w)


def _outproj_kernel(ya_ref, yb_ref, w_ref, r_ref, o_ref):
    ka = ya_ref.shape[1]
    acc = _dot(ya_ref[...], w_ref[:ka, :]) + _dot(yb_ref[...], w_ref[ka:, :])
    o_ref[...] = r_ref[...] + acc


def _outproj(ya, yb, w, res):
    m, ka = ya.shape
    kb = yb.shape[1]
    n = w.shape[1]
    tm = _row_tile(m, 1376)
    tn = _col_tile(n, 1024)
    return pl.pallas_call(
        _outproj_kernel,
        grid=(m // tm, n // tn),
        in_specs=[
            pl.BlockSpec((tm, ka), lambda i, j: (i, 0)),
            pl.BlockSpec((tm, kb), lambda i, j: (i, 0)),
            pl.BlockSpec((ka + kb, tn), lambda i, j: (0, j)),
            pl.BlockSpec((tm, tn), lambda i, j: (i, j)),
        ],
        out_specs=pl.BlockSpec((tm, tn), lambda i, j: (i, j)),
        out_shape=jax.ShapeDtypeStruct((m, n), F32),
        compiler_params=_params("parallel", "arbitrary"),
        name="outproj",
    )(ya, yb, w, res)


def _mlstm_kernel(bi_ref, bf_ref, q_ref, k_ref, v_ref, og_ref, aic_ref, afc_ref, air_ref, afr_ref,
                  cwq_ref, cwk_ref, cbq_ref, cbk_ref, hn_ref, o_ref, c_ref, n_ref, m_ref):
    t_total = q_ref.shape[1]
    n_chunks = (t_total - N_META) // CHUNK
    head = pl.program_id(1)
    b_i = bi_ref[head]
    b_f = bf_ref[head]
    cwq, cwk = cwq_ref[...], cwk_ref[...]
    cbq, cbk = cbq_ref[...], cbk_ref[...]
    gain = hn_ref[...]

    c_ref[...] = jnp.zeros_like(c_ref)
    n_ref[...] = jnp.zeros_like(n_ref)
    m_ref[...] = jnp.zeros_like(m_ref)

    def conv(win, cw, cb, length):
        y = cb
        for j in range(CONV_K):
            y = y + win[8 - (CONV_K - 1) + j:8 - (CONV_K - 1) + j + length, :] * cw[j:j + 1, :]
        return _silu(y)

    def chunk(o, ci, length, qwin, kwin):
        q = conv(qwin, cwq, cbq, length)
        k = conv(kwin, cwk, cbk, length) * (A_DK ** -0.5)
        v = v_ref[0, pl.ds(o, length), :]
        ig_c = aic_ref[0, 0, pl.ds(o, length), :] + b_i
        lf_c = _log_sigmoid(afc_ref[0, 0, pl.ds(o, length), :] + b_f)
        ig_r = air_ref[0, 0, pl.ds(ci, 1), :][:, :length] + b_i
        lf_r = _log_sigmoid(afr_ref[0, 0, pl.ds(ci, 1), :][:, :length] + b_f)

        row = lax.broadcasted_iota(jnp.int32, (length, length), 0)
        col = lax.broadcasted_iota(jnp.int32, (length, length), 1)
        causal = col <= row
        b_c = jnp.sum(jnp.where(causal, lf_r, 0.0), axis=1, keepdims=True)
        b_r = jnp.sum(jnp.where(row <= col, lf_c, 0.0), axis=0, keepdims=True)
        b_end = b_c[length - 1:length, :]

        m_in = m_ref[...]
        c_in = c_ref[...]
        n_in = n_ref[...]

        w_end = b_end - b_c + ig_c
        m_loc = jnp.max(w_end, axis=0, keepdims=True)
        k_w = k * jnp.exp(w_end - m_loc)
        c_loc = _dot_tn(_bf(k_w), _bf(v))
        n_loc = jnp.sum(k_w, axis=0, keepdims=True)

        d = jnp.where(causal, b_c - b_r + ig_r, -jnp.inf)
        inter = b_c + m_in
        m_t = jnp.maximum(inter, jnp.max(d, axis=1, keepdims=True))
        s = _dot_nt(_bf(q), _bf(k)) * jnp.exp(d - m_t)
        a_t = jnp.exp(inter - m_t)
        num = _dot(_bf(s), _bf(v)) + a_t * _dot(_bf(q), _bf(c_in))
        den = jnp.sum(s, axis=1, keepdims=True) + a_t * jnp.sum(q * n_in, axis=1, keepdims=True)
        h = num / jnp.maximum(jnp.abs(den), jnp.exp(-m_t))

        m_new = jnp.maximum(b_end + m_in, m_loc)
        a = jnp.exp(b_end + m_in - m_new)
        c = jnp.exp(m_loc - m_new)
        c_ref[...] = a * c_in + c * c_loc
        n_ref[...] = a * n_in + c * n_loc
        m_ref[...] = m_new

        hn = h * lax.rsqrt(jnp.mean(h * h, axis=-1, keepdims=True) + EPS) * gain
        y = _sigmoid(og_ref[0, pl.ds(o, length), :]) * hn
        o_ref[0, pl.ds(o, length), :] = y.astype(o_ref.dtype)

    zeros8 = jnp.zeros((8, A_DK), F32)
    chunk(0, 0, N_META,
          jnp.concatenate([zeros8, q_ref[0, 0:N_META, :]], axis=0),
          jnp.concatenate([zeros8, k_ref[0, 0:N_META, :]], axis=0))

    def body(c, carry):
        o = pl.multiple_of(N_META + c * CHUNK, BF16_ROWS)
        w0 = pl.multiple_of(N_META - 8 + c * CHUNK, 8)
        chunk(o, c + 1, CHUNK, q_ref[0, pl.ds(w0, CHUNK + 8), :], k_ref[0, pl.ds(w0, CHUNK + 8), :])
        return carry

    lax.fori_loop(0, n_chunks, body, 0)


def _mlstm(z, ai_c, af_c, ai_r, af_r, conv_w, conv_b, b_i, b_f, head_norm):
    b, t, _ = z.shape
    hk = A_HEADS * A_DK
    smem = pl.BlockSpec(memory_space=pltpu.SMEM)
    col = lambda blk: (lambda bi, h: (bi, 0, blk + h))
    return pl.pallas_call(
        _mlstm_kernel,
        grid=(b, A_HEADS),
        in_specs=[
            smem, smem,
            pl.BlockSpec((1, t, A_DK), col(0)),
            pl.BlockSpec((1, t, A_DK), col(hk // A_DK)),
            pl.BlockSpec((1, t, A_DV), col(2 * hk // A_DV)),
            pl.BlockSpec((1, t, A_DV), col((2 * hk + A_HEADS * A_DV) // A_DV)),
            pl.BlockSpec((1, 1, t, 1), lambda bi, h: (bi, h, 0, 0)),
            pl.BlockSpec((1, 1, t, 1), lambda bi, h: (bi, h, 0, 0)),
            pl.BlockSpec((1, 1) + ai_r.shape[2:], lambda bi, h: (bi, h, 0, 0)),
            pl.BlockSpec((1, 1) + af_r.shape[2:], lambda bi, h: (bi, h, 0, 0)),
            pl.BlockSpec((CONV_K, A_DK), lambda bi, h: (0, h)),
            pl.BlockSpec((CONV_K, A_DK), lambda bi, h: (0, A_HEADS + h)),
            pl.BlockSpec((1, A_DK), lambda bi, h: (0, h)),
            pl.BlockSpec((1, A_DK), lambda bi, h: (0, A_HEADS + h)),
            pl.BlockSpec((1, A_DV), lambda bi, h: (0, h)),
        ],
        out_specs=pl.BlockSpec((1, t, A_DV), lambda bi, h: (bi, 0, h)),
        out_shape=jax.ShapeDtypeStruct((b, t, A_HEADS * A_DV), BF16),
        scratch_shapes=[pltpu.VMEM((A_DK, A_DV), F32), pltpu.VMEM((1, A_DK), F32), pltpu.VMEM((1, 1), F32)],
        compiler_params=_params("parallel", "parallel"),
        name="mlstm",
    )(b_i, b_f, z, z, z, z, ai_c, af_c, ai_r, af_r, conv_w, conv_w, conv_b.reshape(1, -1), conv_b.reshape(1, -1),
      head_norm.reshape(1, -1))


def _gla_kernel(q_ref, k_ref, v_ref, og_ref, g_ref, p0_ref, p1_ref, p2_ref, hn_ref, o_ref, st_ref, *, mode):
    t_total = q_ref.shape[1]
    n_chunks = (t_total - N_META) // CHUNK
    dk = q_ref.shape[2]
    gain = hn_ref[...]
    st_ref[...] = jnp.zeros_like(st_ref)

    def chunk(o, length):
        q = q_ref[0, pl.ds(o, length), :]
        v = v_ref[0, pl.ds(o, length), :]
        if mode == "gla":
            pre = _dot(g_ref[0, pl.ds(o, length), :], p0_ref[0], precision=HIGHEST) + p1_ref[...]
            lg = _log_sigmoid(pre) / GATE_TAU
            q = q * (dk ** -0.5)
            k = k_ref[0, pl.ds(o, length), :]
        else:
            fpre = g_ref[0, pl.ds(o, length), :]
            a = p0_ref[...]
            bb = p1_ref[...] + _log_sigmoid(fpre)
            hi = jnp.maximum(a, bb)
            lg = hi + jnp.log1p(jnp.exp(-jnp.abs(a - bb)))
            k = p2_ref[...] * _sigmoid(-fpre)

        row = lax.broadcasted_iota(jnp.int32, (length, length), 0)
        col = lax.broadcasted_iota(jnp.int32, (length, length), 1)
        causal = col <= row
        g = _dot(causal.astype(F32), lg, precision=HIGHEST)
        g_end = g[length - 1:length, :]
        g_mid = g[length // 2:length // 2 + 1, :]
        s = jnp.where(causal, _dot_nt(_bf(q * jnp.exp(g - g_mid)), _bf(k * jnp.exp(g_mid - g))), 0.0)
        vb = _bf(v)
        st_in = st_ref[...]
        out = _dot(_bf(s), vb) + _dot_nt(_bf(q * jnp.exp(g)), _bf(st_in))
        st_ref[...] = st_in * jnp.exp(g_end) + _dot_tn(vb, _bf(k * jnp.exp(g_end - g)))

        hn = out * lax.rsqrt(jnp.mean(out * out, axis=-1, keepdims=True) + EPS) * gain
        og = og_ref[0, pl.ds(o, length), :]
        gate = _silu(og) if mode == "gla" else _sigmoid(og)
        o_ref[0, pl.ds(o, length), :] = (gate * hn).astype(o_ref.dtype)

    chunk(0, N_META)

    def body(c, carry):
        chunk(pl.multiple_of(N_META + c * CHUNK, BF16_ROWS), CHUNK)
        return carry

    lax.fori_loop(0, n_chunks, body, 0)


def _gla_call(z, heads, dk, dv, blocks, gate_width, params, head_norm, mode):
    b, t, _ = z.shape
    q0, k0, v0, og0, g0 = blocks
    zspec = lambda width, off, per_head=True: pl.BlockSpec(
        (1, t, width), (lambda bi, h: (bi, 0, off // width + (h if per_head else 0))))
    (p0, s0), (p1, s1), (p2, s2) = params
    return pl.pallas_call(
        functools.partial(_gla_kernel, mode=mode),
        grid=(b, heads),
        in_specs=[
            zspec(dk, q0), zspec(dk, k0), zspec(dv, v0), zspec(dv, og0),
            zspec(gate_width, g0, per_head=(mode != "gla")),
            s0, s1, s2,
            pl.BlockSpec((1, dv), lambda bi, h: (0, h)),
        ],
        out_specs=pl.BlockSpec((1, t, dv), lambda bi, h: (bi, 0, h)),
        out_shape=jax.ShapeDtypeStruct((b, t, heads * dv), BF16),
        scratch_shapes=[pltpu.VMEM((dv, dk), F32)],
        compiler_params=_params("parallel", "parallel"),
        name="gla_" + mode,
    )(z, z, z, z, z, p0, p1, p2, head_norm.reshape(1, -1))


def _mla_kernel(qn_ref, qr_ref, kn_ref, v_ref, kr_ref, tab_ref, gqn_ref, gqr_ref, gkn_ref, gkr_ref,
                o_ref, qf_ref, kf_ref, vf_ref):
    t_total = qn_ref.shape[1]
    n_blocks = (ATT_PAD + t_total) // ATT_BLOCK
    dqk = D_NOPE + D_ROPE
    scale = dqk ** -0.5
    rows = _row_tile(t_total, 768)
    lane = lax.broadcasted_iota(jnp.int32, (rows, LANES), 1)
    first_half = lane < D_ROPE

    qf_ref[0:ATT_PAD, :] = jnp.zeros((ATT_PAD, 2 * LANES), BF16)
    kf_ref[0:ATT_PAD, :] = jnp.zeros((ATT_PAD, 2 * LANES), BF16)
    vf_ref[0:ATT_PAD, :] = jnp.zeros((ATT_PAD, D_V), BF16)

    def rope_pair(x, gains, tab):
        p = x * gains * tab
        return p + pltpu.roll(p, D_ROPE, 1)

    def prep(c):
        r0 = c * rows
        dst = ATT_PAD + c * rows
        tab = tab_ref[pl.ds(r0, rows), :]
        qn = qn_ref[0, pl.ds(r0, rows), :]
        qr = qr_ref[0, pl.ds(r0, rows), :]
        ssq = jnp.sum(qn * qn, axis=-1, keepdims=True) + 0.5 * jnp.sum(qr * qr, axis=-1, keepdims=True)
        rq = lax.rsqrt(ssq / dqk + EPS) * scale
        qf_ref[pl.ds(dst, rows), 0:LANES] = _bf(qn * gqn_ref[...] * rq)
        qf_ref[pl.ds(dst, rows), LANES:2 * LANES] = _bf(rope_pair(qr, gqr_ref[...], tab) * rq)
        kn = kn_ref[0, pl.ds(r0, rows), :]
        kr = kr_ref[0, pl.ds(r0, rows), :]
        ssk = jnp.sum(kn * kn, axis=-1, keepdims=True) + 0.5 * jnp.sum(kr * kr, axis=-1, keepdims=True)
        rk = lax.rsqrt(ssk / dqk + EPS)
        kf_ref[pl.ds(dst, rows), 0:LANES] = _bf(kn * gkn_ref[...] * rk)
        kf_ref[pl.ds(dst, rows), LANES:2 * LANES] = _bf(
            jnp.where(first_half, rope_pair(kr, gkr_ref[...], tab) * rk, 0.0))
        vf_ref[pl.ds(dst, rows), :] = _bf(v_ref[0, pl.ds(r0, rows), :])

    for c in range(t_total // rows):
        prep(c)

    qpos = lax.broadcasted_iota(jnp.int32, (ATT_BLOCK, ATT_BLOCK), 0)
    kpos = lax.broadcasted_iota(jnp.int32, (ATT_BLOCK, ATT_BLOCK), 1)
    neg = -jnp.inf

    for qi in range(n_blocks):
        q = qf_ref[qi * ATT_BLOCK:(qi + 1) * ATT_BLOCK, :]
        s = _dot_nt(q, kf_ref[0:(qi + 1) * ATT_BLOCK, :])
        parts = [s[:, j * ATT_BLOCK:(j + 1) * ATT_BLOCK] for j in range(qi + 1)]
        parts[0] = jnp.where(kpos >= ATT_PAD, parts[0], neg)
        parts[qi] = jnp.where(kpos <= qpos, parts[qi], neg)
        top = functools.reduce(jnp.maximum, parts)
        m = jnp.max(top, axis=-1, keepdims=True)
        if qi == 0:
            m = jnp.where(m == neg, 0.0, m)
        probs = [jnp.exp(part - m) for part in parts]
        l = jnp.sum(functools.reduce(jnp.add, probs), axis=-1, keepdims=True)
        pv = _dot(jnp.concatenate([_bf(pr) for pr in probs], axis=1), vf_ref[0:(qi + 1) * ATT_BLOCK, :])
        if qi == 0:
            out = pv / jnp.where(l == 0.0, 1.0, l)
            o_ref[0, 0:N_META, :] = out[ATT_PAD:, :].astype(o_ref.dtype)
        else:
            dst = qi * ATT_BLOCK - ATT_PAD
            o_ref[0, dst:dst + ATT_BLOCK, :] = (pv / l).astype(o_ref.dtype)


def _mla(qn, kvn, z, kr_col, tab, gqn, gqr, gkn, gkr):
    b, t, _ = qn.shape
    assert (ATT_PAD + t) % ATT_BLOCK == 0 and t % (3 * BF16_ROWS) == 0
    tp = ATT_PAD + t
    hspec = lambda off: pl.BlockSpec((1, t, LANES), lambda bi, h: (bi, 0, off + h))
    gspec = pl.BlockSpec((1, LANES), lambda bi, h: (0, 0))
    return pl.pallas_call(
        _mla_kernel,
        grid=(b, D_HEADS),
        in_specs=[
            hspec(0), hspec(D_HEADS), hspec(0), hspec(D_HEADS),
            pl.BlockSpec((1, t, LANES), lambda bi, h: (bi, 0, kr_col // LANES)),
            pl.BlockSpec((t, LANES), lambda bi, h: (0, 0)),
            gspec, gspec, gspec, gspec,
        ],
        out_specs=pl.BlockSpec((1, t, D_V), lambda bi, h: (bi, 0, h)),
        out_shape=jax.ShapeDtypeStruct((b, t, D_HEADS * D_V), BF16),
        scratch_shapes=[pltpu.VMEM((tp, 2 * LANES), BF16), pltpu.VMEM((tp, 2 * LANES), BF16),
                        pltpu.VMEM((tp, D_V), BF16)],
        compiler_params=_params("parallel", "parallel"),
        name="mla",
    )(qn, qn, kvn, kvn, z, tab, gqn, gqr, gkn, gkr)


def _pack_bf16_pairs(v):
    w = v.shape[1] // 2
    bits = pltpu.bitcast(_bf(v).astype(F32), jnp.uint32)
    return (bits[:, :w] >> 16) | (bits[:, w:] & jnp.uint32(0xFFFF0000))


def _unpack_lo(words):
    return pltpu.bitcast(words << 16, F32)


def _unpack_hi(words):
    return pltpu.bitcast(words & jnp.uint32(0xFFFF0000), F32)


def _router_kernel(x_ref, g_ref, w_ref, b_ref, gate_ref, idx_ref, xg_ref, cnt_ref, carry_ref):
    tm = x_ref.shape[0]

    @pl.when(pl.program_id(0) == 0)
    def _():
        carry_ref[...] = jnp.zeros_like(carry_ref)

    x = x_ref[...]
    ms = jnp.mean(x * x, axis=-1, keepdims=True)
    xn = x * lax.rsqrt(ms + EPS) * g_ref[...]
    logits = _dot(xn, w_ref[...], precision=HIGHEST) + b_ref[...]
    lane = lax.broadcasted_iota(jnp.int32, logits.shape, 1)
    lane_f = lane.astype(F32)
    neg = -jnp.inf
    big = float(LANES)

    is_group = lane < N_GROUPS
    g_max = jnp.max(jnp.where(is_group, logits, neg), axis=-1, keepdims=True)
    g_sum = jnp.sum(jnp.where(is_group, jnp.exp(logits - g_max), 0.0), axis=-1, keepdims=True)
    p_top = 1.0 / g_sum
    grp = jnp.min(jnp.where(is_group & (logits == g_max), lane_f, big), axis=-1, keepdims=True)

    e_lo = N_GROUPS + grp * EXPERTS_PER_GROUP
    in_grp = (lane_f >= e_lo) & (lane_f < e_lo + EXPERTS_PER_GROUP)
    e_max = jnp.max(jnp.where(in_grp, logits, neg), axis=-1, keepdims=True)
    e_sum = jnp.sum(jnp.where(in_grp, jnp.exp(logits - e_max), 0.0), axis=-1, keepdims=True)
    i1 = jnp.min(jnp.where(in_grp & (logits == e_max), lane_f, big), axis=-1, keepdims=True)
    rest = in_grp & (lane_f != i1)
    e_2nd = jnp.max(jnp.where(rest, logits, neg), axis=-1, keepdims=True)
    i2 = jnp.min(jnp.where(rest & (logits == e_2nd), lane_f, big), axis=-1, keepdims=True)
    p1 = 1.0 / e_sum
    p2 = jnp.exp(e_2nd - e_max) / e_sum
    tot = p1 + p2
    gate_ref[...] = jnp.where(lane == 0, p_top * p1 / tot, jnp.where(lane == 1, p_top * p2 / tot, 0.0))

    e1 = i1 - N_GROUPS
    e2 = i2 - N_GROUPS
    hot = jnp.where((lane_f == e1) | (lane_f == e2), 1.0, 0.0)
    row = lax.broadcasted_iota(jnp.int32, (tm, tm), 0)
    col = lax.broadcasted_iota(jnp.int32, (tm, tm), 1)
    before = _dot(jnp.where(col < row, 1.0, 0.0).astype(BF16), _bf(hot)) + carry_ref[...]
    r1 = jnp.sum(jnp.where(lane_f == e1, before, 0.0), axis=-1, keepdims=True)
    r2 = jnp.sum(jnp.where(lane_f == e2, before, 0.0), axis=-1, keepdims=True)
    total = carry_ref[...] + jnp.sum(hot, axis=0, keepdims=True)
    carry_ref[...] = total
    cnt_ref[...] = jnp.broadcast_to(total, cnt_ref.shape).astype(jnp.int32)
    idx_ref[...] = jnp.where(lane == 0, e1, jnp.where(lane == 1, e2, jnp.where(lane == 2, r1, jnp.where(
        lane == 3, r2, 0.0)))).astype(jnp.int32)

    words = _pack_bf16_pairs(xn)
    for s in range(8):
        xg_ref[pl.ds(s, tm, stride=8), :] = words[:, s * LANES:(s + 1) * LANES]


def _router(x2d, gain, w_group, b_group, w_expert, b_expert):
    m, d = x2d.shape
    assert d == 2 * 8 * LANES
    tm = _row_tile(m, 688)
    pad = LANES - N_GROUPS - N_EXPERTS
    w = jnp.concatenate([w_group, w_expert, jnp.zeros((d, pad), F32)], axis=1)
    bias = jnp.concatenate([b_group, b_expert, jnp.zeros((pad,), F32)]).reshape(1, LANES)
    return pl.pallas_call(
        _router_kernel,
        grid=(m // tm,),
        in_specs=[
            pl.BlockSpec((tm, d), lambda i: (i, 0)),
            pl.BlockSpec((1, d), lambda i: (0, 0)),
            pl.BlockSpec((d, LANES), lambda i: (0, 0)),
            pl.BlockSpec((1, LANES), lambda i: (0, 0)),
        ],
        out_specs=[pl.BlockSpec((tm, LANES), lambda i: (i, 0)), pl.BlockSpec((tm, LANES), lambda i: (i, 0)),
                   pl.BlockSpec((tm * 8, LANES), lambda i: (i, 0)), pl.BlockSpec((8, LANES), lambda i: (0, 0))],
        out_shape=[jax.ShapeDtypeStruct((m, LANES), F32), jax.ShapeDtypeStruct((m, LANES), jnp.int32),
                   jax.ShapeDtypeStruct((m * 8, LANES), jnp.uint32), jax.ShapeDtypeStruct((8, LANES), jnp.int32)],
        scratch_shapes=[pltpu.VMEM((1, LANES), F32)],
        compiler_params=_params("arbitrary"),
        name="router",
    )(x2d, gain.reshape(1, d), w, bias)


def _invert_kernel(dest_ref, inv_ref):
    def clear(s, carry):
        inv_ref[s] = -1
        return carry

    lax.fori_loop(0, inv_ref.shape[0], clear, 0, unroll=8)

    def put(f, carry):
        inv_ref[dest_ref[f]] = f
        return carry

    lax.fori_loop(0, dest_ref.shape[0], put, 0, unroll=8)


def _invert(dest, p):
    assert p % 8 == 0 and dest.shape[0] % 8 == 0
    smem = pl.BlockSpec(memory_space=pltpu.SMEM)
    return pl.pallas_call(
        _invert_kernel, in_specs=[smem], out_specs=smem,
        out_shape=jax.ShapeDtypeStruct((p,), jnp.int32), name="moe_invert",
    )(dest)


def _expert_kernel(be_ref, nu_ref, cur_ref, nxt_ref, dst_ref, xg_hbm, w1_ref, w3_ref, w2_ref, o_hbm,
                   xbuf, ybuf, xs_ref, w1s, w3s, w2s, sem_in, sem_out):
    i = pl.program_id(0)
    n_used = nu_ref[0]
    par = i % 2
    half = D_MODEL // 2

    def issue_gathers(tok_ref, slot):
        def body(r, carry):
            src = pl.multiple_of(tok_ref[0, 0, r], 8)
            pltpu.make_async_copy(xg_hbm.at[pl.ds(src, 8), :], xbuf.at[slot, pl.ds(r * 8, 8), :],
                                  sem_in.at[slot]).start()
            return carry

        lax.fori_loop(0, MOE_BLOCK, body, 0, unroll=8)

    def wait_gathers(slot):
        pltpu.make_async_copy(xbuf.at[1 - slot], xbuf.at[slot], sem_in.at[slot]).wait()

    def wait_scatters(slot):
        pltpu.make_async_copy(ybuf.at[slot], ybuf.at[1 - slot], sem_out.at[slot]).wait()

    @pl.when(i == 0)
    def _():
        ybuf[1] = jnp.zeros(ybuf.shape[1:], ybuf.dtype)
        n_real = o_hbm.shape[0] - 2 * MOE_BLOCK * 8
        for q in range(2):
            fill = pltpu.make_async_copy(ybuf.at[1], o_hbm.at[pl.ds(n_real + q * MOE_BLOCK * 8, MOE_BLOCK * 8), :],
                                         sem_out.at[1])
            fill.start()
            fill.wait()

    @pl.when((i == 0) & (n_used > 0))
    def _():
        issue_gathers(cur_ref, 0)

    @pl.when(i + 1 < n_used)
    def _():
        issue_gathers(nxt_ref, 1 - par)

    @pl.when(i < n_used)
    def _():
        @pl.when((i == 0) | (be_ref[i] != be_ref[jnp.maximum(i - 1, 0)]))
        def _():
            w1s[...] = _bf(w1_ref[0])
            w3s[...] = _bf(w3_ref[0])
            w2s[...] = _bf(w2_ref[0])

        wait_gathers(par)

        for s in range(8):
            words = xbuf[par, pl.ds(s, MOE_BLOCK, stride=8), :]
            xs_ref[:, s * LANES:(s + 1) * LANES] = _bf(_unpack_lo(words))
            xs_ref[:, half + s * LANES:half + (s + 1) * LANES] = _bf(_unpack_hi(words))
        xb = xs_ref[...]
        h1 = _dot(xb, w1s[...])
        h3 = _dot(xb, w3s[...])
        y = _dot(_bf(_silu(h1) * h3), w2s[...])

        @pl.when(i >= 2)
        def _():
            wait_scatters(par)

        words = _pack_bf16_pairs(y)
        for s in range(8):
            ybuf[par, pl.ds(s, MOE_BLOCK, stride=8), :] = words[:, s * LANES:(s + 1) * LANES]

        def issue_scatter(r, carry):
            dst = pl.multiple_of(dst_ref[0, 0, r], 8)
            pltpu.make_async_copy(ybuf.at[par, pl.ds(r * 8, 8), :], o_hbm.at[pl.ds(dst, 8), :],
                                  sem_out.at[par]).start()
            return carry

        lax.fori_loop(0, MOE_BLOCK, issue_scatter, 0, unroll=8)

        @pl.when(i == n_used - 1)
        def _():
            wait_scatters(par)

            @pl.when(i >= 1)
            def _():
                wait_scatters(1 - par)


def _moe(x2d, gain, w_group, b_group, w_expert, b_expert, w1, w3, w2):
    n, d = x2d.shape
    gates_l, idx_l, xg, cnt = _router(x2d, gain, w_group, b_group, w_expert, b_expert)

    a = n * TOP_K
    n_blocks = -(-a // MOE_BLOCK) + N_EXPERTS
    p = n_blocks * MOE_BLOCK
    counts = cnt[0, :N_EXPERTS]
    padded = (counts + MOE_BLOCK - 1) // MOE_BLOCK * MOE_BLOCK
    pad_end = jnp.cumsum(padded)
    pad_start = pad_end - padded
    dest = (pad_start[idx_l[:, :TOP_K]] + idx_l[:, TOP_K:2 * TOP_K]).reshape(-1).astype(jnp.int32)
    blk0 = jnp.arange(n_blocks, dtype=jnp.int32) * MOE_BLOCK
    block_expert = jnp.minimum(jnp.searchsorted(pad_end, blk0, side="right"), N_EXPERTS - 1).astype(jnp.int32)
    n_used = (pad_end[-1] // MOE_BLOCK).astype(jnp.int32).reshape(1)
    codes = _invert(dest, p)
    slot = jnp.arange(p, dtype=jnp.int32)
    spare = TOP_K * n + (slot // MOE_BLOCK % 2) * MOE_BLOCK + slot % MOE_BLOCK
    src_tok = ((jnp.maximum(codes, 0) >> 1) * 8).reshape(n_blocks, 1, MOE_BLOCK)
    dst_row = (jnp.where(codes >= 0, (codes & 1) * n + (codes >> 1), spare) * 8).reshape(n_blocks, 1, MOE_BLOCK)

    code_spec = lambda step: pl.BlockSpec(
        (1, 1, MOE_BLOCK), lambda i, be, nu: (jnp.minimum(i + step, n_blocks - 1), 0, 0), memory_space=pltpu.SMEM)
    wspec = lambda shape: pl.BlockSpec((1,) + shape, lambda i, be, nu: (be[i], 0, 0))
    out_rows = TOP_K * n + 2 * MOE_BLOCK
    out2 = pl.pallas_call(
        _expert_kernel,
        grid_spec=pltpu.PrefetchScalarGridSpec(
            num_scalar_prefetch=2,
            grid=(n_blocks,),
            in_specs=[
                code_spec(0), code_spec(1), code_spec(0),
                pl.BlockSpec(memory_space=pl.ANY),
                wspec((d, D_EXPERT)), wspec((d, D_EXPERT)), wspec((D_EXPERT, d)),
            ],
            out_specs=pl.BlockSpec(memory_space=pl.ANY),
            scratch_shapes=[
                pltpu.VMEM((2, MOE_BLOCK * 8, LANES), jnp.uint32), pltpu.VMEM((2, MOE_BLOCK * 8, LANES), jnp.uint32),
                pltpu.VMEM((MOE_BLOCK, d), BF16),
                pltpu.VMEM((d, D_EXPERT), BF16), pltpu.VMEM((d, D_EXPERT), BF16), pltpu.VMEM((D_EXPERT, d), BF16),
                pltpu.SemaphoreType.DMA((2,)), pltpu.SemaphoreType.DMA((2,)),
            ],
        ),
        out_shape=jax.ShapeDtypeStruct((out_rows * 8, LANES), jnp.uint32),
        compiler_params=_params("arbitrary"),
        name="moe_experts",
    )(block_expert, n_used, src_tok, src_tok, dst_row, xg, w1, w3, w2)
    return _combine(x2d, gates_l, out2)


def _combine_kernel(x_ref, gate_ref, a_ref, b_ref, o_ref):
    tm = x_ref.shape[0]
    half = x_ref.shape[1] // 2
    g0 = gate_ref[:, 0:1]
    g1 = gate_ref[:, 1:2]
    for s in range(8):
        wa = a_ref[pl.ds(s, tm, stride=8), :]
        wb = b_ref[pl.ds(s, tm, stride=8), :]
        lo = slice(s * LANES, (s + 1) * LANES)
        hi = slice(half + s * LANES, half + (s + 1) * LANES)
        o_ref[:, lo] = x_ref[:, lo] + (g0 * _unpack_lo(wa) + g1 * _unpack_lo(wb))
        o_ref[:, hi] = x_ref[:, hi] + (g0 * _unpack_hi(wa) + g1 * _unpack_hi(wb))


def _combine(x2d, gates, out2):
    n, d = x2d.shape
    tm = _row_tile(n, 688)
    return pl.pallas_call(
        _combine_kernel,
        grid=(n // tm,),
        in_specs=[
            pl.BlockSpec((tm, d), lambda i: (i, 0)),
            pl.BlockSpec((tm, LANES), lambda i: (i, 0)),
            pl.BlockSpec((tm * 8, LANES), lambda i: (i, 0)),
            pl.BlockSpec((tm * 8, LANES), lambda i: (n // tm + i, 0)),
        ],
        out_specs=pl.BlockSpec((tm, d), lambda i: (i, 0)),
        out_shape=jax.ShapeDtypeStruct((n, d), F32),
        compiler_params=_params("parallel"),
        name="moe_combine",
    )(x2d, gates, out2, out2)


def _gate_layouts(cols, heads, t):
    b = cols.shape[0]
    rows = cols.transpose(0, 2, 1)
    meta = jnp.pad(rows[:, :, :N_META], ((0, 0), (0, 0), (0, CHUNK - N_META)))
    real = rows[:, :, N_META:].reshape(b, heads, -1, CHUNK)
    return rows[..., None], jnp.concatenate([meta[:, :, None, :], real], axis=2)


def _even_layer(x, norm_g, w_in, conv_w, conv_b, b_i, b_f, a_norm, w_gate2, b_gate, b_norm, w_out):
    b, t, d = x.shape
    n = b * t
    a_w = 2 * A_HEADS * A_DK + 2 * A_HEADS * A_DV
    g_w = 2 * A_HEADS
    b_w = 2 * B_HEADS * B_DK + 2 * B_HEADS * B_DV
    main = a_w + b_w
    gate_cols = g_w + GATE_RANK
    w = jnp.concatenate([w_in[:, :a_w], w_in[:, a_w + g_w:a_w + g_w + b_w], w_in[:, a_w:a_w + g_w],
                         w_in[:, a_w + g_w + b_w:], jnp.zeros((d, MXU_DIM - gate_cols), F32)], axis=1).astype(BF16)
    z = _normproj(x.reshape(n, d), norm_g, w).reshape(b, t, main + MXU_DIM)

    gates = z[:, :, main:main + g_w]
    ai_c, ai_r = _gate_layouts(gates[..., :A_HEADS], A_HEADS, t)
    af_c, af_r = _gate_layouts(gates[..., A_HEADS:], A_HEADS, t)
    ya = _mlstm(z, ai_c, af_c, ai_r, af_r, conv_w, conv_b, b_i, b_f, a_norm)

    wg = jnp.zeros((B_HEADS, MXU_DIM, B_DK), F32).at[:, g_w:g_w + GATE_RANK, :].set(
        w_gate2.reshape(GATE_RANK, B_HEADS, B_DK).transpose(1, 0, 2))
    dummy = jnp.zeros((1, B_HEADS * B_DK), F32)
    hspec = pl.BlockSpec((1, B_DK), lambda bi, h: (0, h))
    yb = _gla_call(
        z, B_HEADS, B_DK, B_DV,
        (a_w, a_w + B_HEADS * B_DK, a_w + 2 * B_HEADS * B_DK, a_w + 2 * B_HEADS * B_DK + B_HEADS * B_DV, main),
        MXU_DIM,
        ((wg, pl.BlockSpec((1, MXU_DIM, B_DK), lambda bi, h: (h, 0, 0))),
         (b_gate.reshape(1, -1), hspec), (dummy, hspec)),
        b_norm, "gla")
    return _outproj(ya.reshape(n, -1), yb.reshape(n, -1), w_out.astype(BF16), x.reshape(n, d)).reshape(b, t, d)


def _odd_layer(x, lb, norm_g, w_in, c_norm, q_a_norm, w_q_up, kv_a_norm, w_kv_up, q_norm, k_norm, w_out):
    b, t, d = x.shape
    n = b * t
    c_w = 2 * C_HEADS * C_DK + 2 * C_HEADS * C_DV
    swap = (jnp.arange(D_ROPE) + D_ROPE // 2) % D_ROPE
    kr0 = c_w + Q_LORA + KV_LORA
    used = kr0 + 2 * D_ROPE
    total = -(-used // MXU_DIM) * MXU_DIM
    w = jnp.concatenate([w_in, w_in[:, kr0:kr0 + D_ROPE][:, swap], jnp.zeros((d, total - used), F32)],
                        axis=1).astype(BF16)
    z2 = _normproj(x.reshape(n, d), norm_g, w)
    z = z2.reshape(b, t, total)

    hspec = pl.BlockSpec((1, C_DK), lambda bi, h: (0, h))
    yc = _gla_call(
        z, C_HEADS, C_DK, C_DV,
        (0, C_HEADS * C_DK, 2 * C_HEADS * C_DK, 2 * C_HEADS * C_DK + C_HEADS * C_DV, C_HEADS * C_DK),
        C_DK,
        ((jnp.log(lb).reshape(1, -1), hspec), (jnp.log1p(-lb).reshape(1, -1), hspec), ((1.0 - lb).reshape(1, -1), hspec)),
        c_norm, "hgrn")

    dq = D_NOPE + D_ROPE
    wq = w_q_up.reshape(Q_LORA, D_HEADS, dq)
    wq_rope = wq[:, :, D_NOPE:]
    wq_p = jnp.concatenate([wq[:, :, :D_NOPE].reshape(Q_LORA, -1),
                            jnp.concatenate([wq_rope, wq_rope[:, :, swap]], axis=-1).reshape(Q_LORA, -1)],
                           axis=1).astype(BF16)
    wkv = w_kv_up.reshape(KV_LORA, D_HEADS, D_NOPE + D_V)
    wkv_p = jnp.concatenate([wkv[:, :, :D_NOPE].reshape(KV_LORA, -1), wkv[:, :, D_NOPE:].reshape(KV_LORA, -1)],
                            axis=1).astype(BF16)
    qn = _normproj(z2, q_a_norm, wq_p, x_col_block=c_w // Q_LORA).reshape(b, t, -1)
    kvn = _normproj(z2, kv_a_norm, wkv_p, x_col_block=(c_w + Q_LORA) // KV_LORA).reshape(b, t, -1)

    pos = jnp.arange(t, dtype=F32)
    half = D_ROPE // 2
    inv = ROPE_THETA ** (-jnp.arange(half, dtype=F32) / half)
    ang = pos[:, None] * inv[None, :]
    cos, sin = jnp.cos(ang), jnp.sin(ang)
    tab = jnp.concatenate([cos, cos, -sin, sin], axis=1)
    pair = lambda g: jnp.concatenate([g[D_NOPE:], g[D_NOPE:][swap]]).reshape(1, LANES)
    yd = _mla(qn, kvn, z, kr0, tab, q_norm[:D_NOPE].reshape(1, LANES), pair(q_norm),
              k_norm[:D_NOPE].reshape(1, LANES), pair(k_norm))
    return _outproj(yc.reshape(n, -1), yd.reshape(n, -1), w_out.astype(BF16), x.reshape(n, d)).reshape(b, t, d)


def kernel(x, meta_tokens, ab_norm, ab_w_in, a_conv_w, a_conv_b, a_b_i, a_b_f, a_head_norm, b_w_gate2, b_b_gate, b_head_norm, ab_w_out, cd_norm, cd_w_in, c_lower_bound, c_head_norm, d_q_a_norm, d_w_q_up, d_kv_a_norm, d_w_kv_up, d_q_norm, d_k_norm, cd_w_out, moe_norm, moe_w_group, moe_b_group, moe_w_expert, moe_b_expert, moe_w1, moe_w3, moe_w2):
    b = x.shape[0]
    depth = moe_norm.shape[0]
    h = jnp.concatenate([jnp.broadcast_to(meta_tokens.astype(x.dtype)[None], (b, N_META, D_MODEL)), x], axis=1)
    t = h.shape[1]
    lb_cum = jnp.cumsum(jax.nn.softmax(c_lower_bound.astype(F32), axis=0), axis=0)
    lower_bounds = lb_cum - lb_cum[0]
    for layer in range(depth):
        j = layer // 2
        if layer % 2 == 0:
            h = _even_layer(h, ab_norm[j], ab_w_in[j], a_conv_w[j], a_conv_b[j], a_b_i[j], a_b_f[j], a_head_norm[j],
                            b_w_gate2[j], b_b_gate[j], b_head_norm[j], ab_w_out[j])
        else:
            h = _odd_layer(h, lower_bounds[layer], cd_norm[j], cd_w_in[j], c_head_norm[j], d_q_a_norm[j],
                           d_w_q_up[j], d_kv_a_norm[j], d_w_kv_up[j], d_q_norm[j], d_k_norm[j], cd_w_out[j])
        h = _moe(h.reshape(b * t, D_MODEL), moe_norm[layer], moe_w_group[layer], moe_b_group[layer],
                 moe_w_expert[layer], moe_b_expert[layer], moe_w1[layer], moe_w3[layer], moe_w2[layer]
                 ).reshape(b, t, D_MODEL)
    return h[:, N_META:]
```

```python
import functools
import math

import jax
import jax.numpy as jnp
from jax import lax
from jax.experimental import pallas as pl
from jax.experimental.pallas import tpu as pltpu

F32 = jnp.float32
BF16 = jnp.bfloat16
HIGHEST = lax.Precision.HIGHEST

D_MODEL = 2048
N_META = 16
CHUNK = 64
CONV_K = 4
EPS = 1e-6
A_HEADS, A_DK, A_DV = 4, 128, 256
B_HEADS, B_DK, B_DV = 4, 128, 256
GATE_RANK = 16
GATE_TAU = 16.0
C_HEADS, C_DK, C_DV = 8, 128, 128
D_HEADS, D_NOPE, D_ROPE, D_V = 8, 128, 64, 128
Q_LORA, KV_LORA = 512, 256
ROPE_THETA = 10000.0
N_GROUPS, EXPERTS_PER_GROUP = 4, 8
N_EXPERTS = N_GROUPS * EXPERTS_PER_GROUP
TOP_K = 2
D_EXPERT = 512

LANES = 128
MXU_DIM = 256
BF16_ROWS = 16
VMEM_LIMIT = 56 * 1024 * 1024
MOE_BLOCK = MXU_DIM
ATT_BLOCK = 256
ATT_PAD = ATT_BLOCK - N_META

_NT = (((1,), (1,)), ((), ()))
_TN = (((0,), (0,)), ((), ()))


def _dot(a, b, precision=None):
    return jnp.dot(a, b, preferred_element_type=F32, precision=precision)


def _dot_nt(a, b):
    return lax.dot_general(a, b, _NT, preferred_element_type=F32)


def _dot_tn(a, b):
    return lax.dot_general(a, b, _TN, preferred_element_type=F32)


def _bf(x):
    return x.astype(BF16)


def _log_sigmoid(x):
    return jnp.minimum(x, 0.0) - jnp.log1p(jnp.exp(-jnp.abs(x)))


def _sigmoid(x):
    return 1.0 / (1.0 + jnp.exp(-x))


def _silu(x):
    return x * _sigmoid(x)


def _row_tile(m, cap):
    best = None
    for t in range(BF16_ROWS, min(m, cap) + 1, BF16_ROWS):
        if m % t == 0:
            best = t
    assert best is not None, m
    return best


def _col_tile(n, cap):
    best = None
    for t in range(MXU_DIM, min(n, cap) + 1, MXU_DIM):
        if n % t == 0:
            best = t
    assert best is not None, n
    return best


def _params(*sem):
    return pltpu.CompilerParams(dimension_semantics=sem, vmem_limit_bytes=VMEM_LIMIT)


def _normproj_kernel(x_ref, g_ref, w_ref, o_ref, xs_ref):
    tm = xs_ref.shape[0]

    @pl.when(pl.program_id(1) == 0)
    def _():
        def body(c, carry):
            r0 = pl.multiple_of(c * BF16_ROWS, BF16_ROWS)
            x = x_ref[pl.ds(r0, BF16_ROWS), :]
            ms = jnp.mean(x * x, axis=-1, keepdims=True)
            xs_ref[pl.ds(r0, BF16_ROWS), :] = _bf(x * lax.rsqrt(ms + EPS) * g_ref[...])
            return carry

        lax.fori_loop(0, tm // BF16_ROWS, body, 0)

    o_ref[...] = _dot(xs_ref[...], w_ref[...]).astype(o_ref.dtype)


def _normproj(x2d, gain, w, *, x_col_block=0, out_dtype=F32):
    m = x2d.shape[0]
    k, n = w.shape
    tm = _row_tile(m, 688)
    tn = _col_tile(n, 1280)
    return pl.pallas_call(
        _normproj_kernel,
        grid=(m // tm, n // tn),
        in_specs=[
            pl.BlockSpec((tm, k), lambda i, j: (i, x_col_block)),
            pl.BlockSpec((1, k), lambda i, j: (0, 0)),
            pl.BlockSpec((k, tn), lambda i, j: (0, j)),
        ],
        out_specs=pl.BlockSpec((tm, tn), lambda i, j: (i, j)),
        out_shape=jax.ShapeDtypeStruct((m, n), out_dtype),
        scratch_shapes=[pltpu.VMEM((tm, k), BF16)],
        compiler_params=_params("parallel", "arbitrary"),
        name="normproj",
    )(x2d, gain.reshape(1, k).astype(F32), w)


def _outproj_kernel(ya_ref, yb_ref, w_ref, r_ref, o_ref):
    ka = ya_ref.shape[1]
    acc = _dot(ya_ref[...], w_ref[:ka, :]) + _dot(yb_ref[...], w_ref[ka:, :])
    o_ref[...] = r_ref[...] + acc


def _outproj(ya, yb, w, res):
    m, ka = ya.shape
    kb = yb.shape[1]
    n = w.shape[1]
    tm = _row_tile(m, 1376)
    tn = _col_tile(n, 1024)
    return pl.pallas_call(
        _outproj_kernel,
        grid=(m // tm, n // tn),
        in_specs=[
            pl.BlockSpec((tm, ka), lambda i, j: (i, 0)),
            pl.BlockSpec((tm, kb), lambda i, j: (i, 0)),
            pl.BlockSpec((ka + kb, tn), lambda i, j: (0, j)),
            pl.BlockSpec((tm, tn), lambda i, j: (i, j)),
        ],
        out_specs=pl.BlockSpec((tm, tn), lambda i, j: (i, j)),
        out_shape=jax.ShapeDtypeStruct((m, n), F32),
        compiler_params=_params("parallel", "arbitrary"),
        name="outproj",
    )(ya, yb, w, res)


def _mlstm_kernel(bi_ref, bf_ref, q_ref, k_ref, v_ref, og_ref, aic_ref, afc_ref, air_ref, afr_ref,
                  cwq_ref, cwk_ref, cbq_ref, cbk_ref, hn_ref, o_ref, c_ref, n_ref, m_ref, *, heads):
    t_total = q_ref.shape[1]
    n_chunks = (t_total - N_META) // CHUNK
    head0 = pl.program_id(1) * heads

    c_ref[...] = jnp.zeros_like(c_ref)
    n_ref[...] = jnp.zeros_like(n_ref)
    m_ref[...] = jnp.zeros_like(m_ref)

    def conv(win, cw, cb, length):
        y = cb
        for j in range(CONV_K):
            y = y + win[8 - (CONV_K - 1) + j:8 - (CONV_K - 1) + j + length, :] * cw[j:j + 1, :]
        return _silu(y)

    hs = range(heads)
    kcol = [slice(hh * A_DK, (hh + 1) * A_DK) for hh in hs]
    vcol = [slice(hh * A_DV, (hh + 1) * A_DV) for hh in hs]

    def gates(hh, o, ci, length, causal, upper):
        b_i = bi_ref[head0 + hh]
        b_f = bf_ref[head0 + hh]
        ig_c = aic_ref[0, hh, pl.ds(o, length), :] + b_i
        lf_c = _log_sigmoid(afc_ref[0, hh, pl.ds(o, length), :] + b_f)
        ig_r = air_ref[0, hh, pl.ds(ci, 1), :][:, :length] + b_i
        lf_r = _log_sigmoid(afr_ref[0, hh, pl.ds(ci, 1), :][:, :length] + b_f)
        b_c = jnp.sum(jnp.where(causal, lf_r, 0.0), axis=1, keepdims=True)
        b_r = jnp.sum(jnp.where(upper, lf_c, 0.0), axis=0, keepdims=True)
        b_end = b_c[length - 1:length, :]
        w_end = b_end - b_c + ig_c
        m_loc = jnp.max(w_end, axis=0, keepdims=True)
        d = jnp.where(causal, b_c - b_r + ig_r, -jnp.inf)
        return b_c, b_end, jnp.exp(w_end - m_loc), m_loc, d, jnp.max(d, axis=1, keepdims=True)

    def chunk(o, ci, length, qwin, kwin):
        row = lax.broadcasted_iota(jnp.int32, (length, length), 0)
        col = lax.broadcasted_iota(jnp.int32, (length, length), 1)
        causal = col <= row
        gt = [gates(hh, o, ci, length, causal, row <= col) for hh in hs]
        q = [conv(qwin[hh], cwq_ref[:, kcol[hh]], cbq_ref[:, kcol[hh]], length) for hh in hs]
        k = [conv(kwin[hh], cwk_ref[:, kcol[hh]], cbk_ref[:, kcol[hh]], length) * (A_DK ** -0.5) for hh in hs]
        vb = [_bf(v_ref[0, pl.ds(o, length), vcol[hh]]) for hh in hs]
        qb = [_bf(q[hh]) for hh in hs]
        k_w = [k[hh] * gt[hh][2] for hh in hs]
        qk = [_dot_nt(qb[hh], _bf(k[hh])) for hh in hs]
        c_in = [c_ref[hh] for hh in hs]
        q_c = [_dot(qb[hh], _bf(c_in[hh])) for hh in hs]
        c_loc = [_dot_tn(_bf(k_w[hh]), vb[hh]) for hh in hs]
        s, a_t, m_t, q_n = [], [], [], []
        for hh in hs:
            b_c, b_end, _, m_loc, d, d_max = gt[hh]
            m_in = m_ref[hh]
            inter = b_c + m_in
            m_t.append(jnp.maximum(inter, d_max))
            s.append(qk[hh] * jnp.exp(d - m_t[hh]))
            a_t.append(jnp.exp(inter - m_t[hh]))
            m_new = jnp.maximum(b_end + m_in, m_loc)
            a = jnp.exp(b_end + m_in - m_new)
            c = jnp.exp(m_loc - m_new)
            n_in = n_ref[hh]
            c_ref[hh] = a * c_in[hh] + c * c_loc[hh]
            n_ref[hh] = a * n_in + c * jnp.sum(k_w[hh], axis=0, keepdims=True)
            m_ref[hh] = m_new
            q_n.append(jnp.sum(q[hh] * n_in, axis=1, keepdims=True))
        num = [_dot(_bf(s[hh]), vb[hh]) + a_t[hh] * q_c[hh] for hh in hs]
        for hh in hs:
            den = jnp.sum(s[hh], axis=1, keepdims=True) + a_t[hh] * q_n[hh]
            h = num[hh] / jnp.maximum(jnp.abs(den), jnp.exp(-m_t[hh]))
            hn = h * lax.rsqrt(jnp.mean(h * h, axis=-1, keepdims=True) + EPS) * hn_ref[:, vcol[hh]]
            y = _sigmoid(og_ref[0, pl.ds(o, length), vcol[hh]]) * hn
            o_ref[0, pl.ds(o, length), vcol[hh]] = y.astype(o_ref.dtype)

    zeros8 = jnp.zeros((8, A_DK), F32)
    chunk(0, 0, N_META,
          [jnp.concatenate([zeros8, q_ref[0, 0:N_META, kcol[hh]]], axis=0) for hh in hs],
          [jnp.concatenate([zeros8, k_ref[0, 0:N_META, kcol[hh]]], axis=0) for hh in hs])

    def body(c, carry):
        o = pl.multiple_of(N_META + c * CHUNK, BF16_ROWS)
        w0 = pl.multiple_of(N_META - 8 + c * CHUNK, 8)
        chunk(o, c + 1, CHUNK, [q_ref[0, pl.ds(w0, CHUNK + 8), kcol[hh]] for hh in hs],
              [k_ref[0, pl.ds(w0, CHUNK + 8), kcol[hh]] for hh in hs])
        return carry

    lax.fori_loop(0, n_chunks, body, 0)


def _mlstm(z, ai_c, af_c, ai_r, af_r, conv_w, conv_b, b_i, b_f, head_norm, heads=2):
    b, t, _ = z.shape
    hk = A_HEADS * A_DK
    wk, wv = heads * A_DK, heads * A_DV
    smem = pl.BlockSpec(memory_space=pltpu.SMEM)
    col = lambda width, off: (lambda bi, g: (bi, 0, off // width + g))
    gate_c = pl.BlockSpec((1, heads, t, 1), lambda bi, g: (bi, g, 0, 0))
    gate_r = pl.BlockSpec((1, heads) + ai_r.shape[2:], lambda bi, g: (bi, g, 0, 0))
    return pl.pallas_call(
        functools.partial(_mlstm_kernel, heads=heads),
        grid=(b, A_HEADS // heads),
        in_specs=[
            smem, smem,
            pl.BlockSpec((1, t, wk), col(wk, 0)),
            pl.BlockSpec((1, t, wk), col(wk, hk)),
            pl.BlockSpec((1, t, wv), col(wv, 2 * hk)),
            pl.BlockSpec((1, t, wv), col(wv, 2 * hk + A_HEADS * A_DV)),
            gate_c, gate_c, gate_r, gate_r,
            pl.BlockSpec((CONV_K, wk), lambda bi, g: (0, g)),
            pl.BlockSpec((CONV_K, wk), lambda bi, g: (0, A_HEADS // heads + g)),
            pl.BlockSpec((1, wk), lambda bi, g: (0, g)),
            pl.BlockSpec((1, wk), lambda bi, g: (0, A_HEADS // heads + g)),
            pl.BlockSpec((1, wv), lambda bi, g: (0, g)),
        ],
        out_specs=pl.BlockSpec((1, t, wv), lambda bi, g: (bi, 0, g)),
        out_shape=jax.ShapeDtypeStruct((b, t, A_HEADS * A_DV), BF16),
        scratch_shapes=[pltpu.VMEM((heads, A_DK, A_DV), F32), pltpu.VMEM((heads, 1, A_DK), F32),
                        pltpu.VMEM((heads, 1, 1), F32)],
        compiler_params=_params("parallel", "parallel"),
        name="mlstm",
    )(b_i, b_f, z, z, z, z, ai_c, af_c, ai_r, af_r, conv_w, conv_w, conv_b.reshape(1, -1), conv_b.reshape(1, -1),
      head_norm.reshape(1, -1))


def _gla_kernel(q_ref, k_ref, v_ref, og_ref, g_ref, p0_ref, p1_ref, p2_ref, hn_ref, o_ref, st_ref, *, mode, heads):
    t_total = q_ref.shape[1]
    n_chunks = (t_total - N_META) // CHUNK
    dv, dk = st_ref.shape[1:]
    st_ref[...] = jnp.zeros_like(st_ref)

    hs = range(heads)
    kcol = [slice(hh * dk, (hh + 1) * dk) for hh in hs]
    vcol = [slice(hh * dv, (hh + 1) * dv) for hh in hs]

    def gate_inputs(hh, o, length):
        q = q_ref[0, pl.ds(o, length), kcol[hh]]
        if mode == "gla":
            pre = _dot(g_ref[0, pl.ds(o, length), :], p0_ref[hh], precision=HIGHEST) + p1_ref[:, kcol[hh]]
            return q * (dk ** -0.5), k_ref[0, pl.ds(o, length), kcol[hh]], _log_sigmoid(pre) / GATE_TAU
        fpre = g_ref[0, pl.ds(o, length), kcol[hh]]
        a = p0_ref[:, kcol[hh]]
        bb = p1_ref[:, kcol[hh]] + _log_sigmoid(fpre)
        lg = jnp.maximum(a, bb) + jnp.log1p(jnp.exp(-jnp.abs(a - bb)))
        return q, p2_ref[:, kcol[hh]] * _sigmoid(-fpre), lg

    def cumsum_time(tri, lg):
        hi = _bf(lg)
        r1 = lg - hi.astype(F32)
        mid = _bf(r1)
        lo = _bf(r1 - mid.astype(F32))
        parts = _dot(tri, jnp.concatenate([hi, mid, lo], axis=1))
        return parts[:, :dk] + (parts[:, dk:2 * dk] + parts[:, 2 * dk:])

    def chunk(o, length):
        row = lax.broadcasted_iota(jnp.int32, (length, length), 0)
        col = lax.broadcasted_iota(jnp.int32, (length, length), 1)
        causal = col <= row
        tri = jnp.where(causal, 1.0, 0.0).astype(BF16)
        qkl = [gate_inputs(hh, o, length) for hh in hs]
        vb = [_bf(v_ref[0, pl.ds(o, length), vcol[hh]]) for hh in hs]
        g = [cumsum_time(tri, qkl[hh][2]) for hh in hs]
        g_end = [g[hh][length - 1:length, :] for hh in hs]
        g_mid = [g[hh][length // 2:length // 2 + 1, :] for hh in hs]
        s = [_dot_nt(_bf(qkl[hh][0] * jnp.exp(g[hh] - g_mid[hh])), _bf(qkl[hh][1] * jnp.exp(g_mid[hh] - g[hh])))
             for hh in hs]
        st_in = [st_ref[hh] for hh in hs]
        inter = [_dot_nt(_bf(qkl[hh][0] * jnp.exp(g[hh])), _bf(st_in[hh])) for hh in hs]
        local = [_dot_tn(vb[hh], _bf(qkl[hh][1] * jnp.exp(g_end[hh] - g[hh]))) for hh in hs]
        for hh in hs:
            st_ref[hh] = st_in[hh] * jnp.exp(g_end[hh]) + local[hh]
        out = [_dot(_bf(jnp.where(causal, s[hh], 0.0)), vb[hh]) + inter[hh] for hh in hs]
        for hh in hs:
            hn = out[hh] * lax.rsqrt(jnp.mean(out[hh] * out[hh], axis=-1, keepdims=True) + EPS) * hn_ref[:, vcol[hh]]
            og = og_ref[0, pl.ds(o, length), vcol[hh]]
            gate = _silu(og) if mode == "gla" else _sigmoid(og)
            o_ref[0, pl.ds(o, length), vcol[hh]] = (gate * hn).astype(o_ref.dtype)

    chunk(0, N_META)

    def body(c, carry):
        chunk(pl.multiple_of(N_META + c * CHUNK, BF16_ROWS), CHUNK)
        return carry

    lax.fori_loop(0, n_chunks, body, 0)


def _gla_call(z, n_heads, heads, dk, dv, blocks, gate_width, params, head_norm, mode):
    b, t, _ = z.shape
    q0, k0, v0, og0, g0 = blocks
    zspec = lambda width, off, grouped=True: pl.BlockSpec(
        (1, t, width), (lambda bi, g: (bi, 0, off // width + (g if grouped else 0))))
    (p0, s0), (p1, s1), (p2, s2) = params
    gate_spec = zspec(gate_width, g0, grouped=False) if mode == "gla" else zspec(heads * dk, g0)
    return pl.pallas_call(
        functools.partial(_gla_kernel, mode=mode, heads=heads),
        grid=(b, n_heads // heads),
        in_specs=[
            zspec(heads * dk, q0), zspec(heads * dk, k0), zspec(heads * dv, v0), zspec(heads * dv, og0),
            gate_spec, s0, s1, s2,
            pl.BlockSpec((1, heads * dv), lambda bi, g: (0, g)),
        ],
        out_specs=pl.BlockSpec((1, t, heads * dv), lambda bi, g: (bi, 0, g)),
        out_shape=jax.ShapeDtypeStruct((b, t, n_heads * dv), BF16),
        scratch_shapes=[pltpu.VMEM((heads, dv, dk), F32)],
        compiler_params=_params("parallel", "parallel"),
        name="gla_" + mode,
    )(z, z, z, z, z, p0, p1, p2, head_norm.reshape(1, -1))


def _mla_kernel(qn_ref, qr_ref, kn_ref, v_ref, kr_ref, tab_ref, gqn_ref, gqr_ref, gkn_ref, gkr_ref,
                o_ref, qf_ref, kf_ref, vf_ref):
    t_total = qn_ref.shape[1]
    n_blocks = (ATT_PAD + t_total) // ATT_BLOCK
    dqk = D_NOPE + D_ROPE
    scale = dqk ** -0.5
    rows = _row_tile(t_total, 768)
    lane = lax.broadcasted_iota(jnp.int32, (rows, LANES), 1)
    first_half = lane < D_ROPE

    qf_ref[0:ATT_PAD, :] = jnp.zeros((ATT_PAD, 2 * LANES), BF16)
    kf_ref[0:ATT_PAD, :] = jnp.zeros((ATT_PAD, 2 * LANES), BF16)
    vf_ref[0:ATT_PAD, :] = jnp.zeros((ATT_PAD, D_V), BF16)

    def rope_pair(x, gains, tab):
        p = x * gains * tab
        return p + pltpu.roll(p, D_ROPE, 1)

    def prep(c):
        r0 = c * rows
        dst = ATT_PAD + c * rows
        tab = tab_ref[pl.ds(r0, rows), :]
        qn = qn_ref[0, pl.ds(r0, rows), :]
        qr = qr_ref[0, pl.ds(r0, rows), :]
        ssq = jnp.sum(qn * qn, axis=-1, keepdims=True) + 0.5 * jnp.sum(qr * qr, axis=-1, keepdims=True)
        rq = lax.rsqrt(ssq / dqk + EPS) * scale
        qf_ref[pl.ds(dst, rows), 0:LANES] = _bf(qn * gqn_ref[...] * rq)
        qf_ref[pl.ds(dst, rows), LANES:2 * LANES] = _bf(rope_pair(qr, gqr_ref[...], tab) * rq)
        kn = kn_ref[0, pl.ds(r0, rows), :]
        kr = kr_ref[0, pl.ds(r0, rows), :]
        ssk = jnp.sum(kn * kn, axis=-1, keepdims=True) + 0.5 * jnp.sum(kr * kr, axis=-1, keepdims=True)
        rk = lax.rsqrt(ssk / dqk + EPS)
        kf_ref[pl.ds(dst, rows), 0:LANES] = _bf(kn * gkn_ref[...] * rk)
        kf_ref[pl.ds(dst, rows), LANES:2 * LANES] = _bf(
            jnp.where(first_half, rope_pair(kr, gkr_ref[...], tab) * rk, 0.0))
        vf_ref[pl.ds(dst, rows), :] = _bf(v_ref[0, pl.ds(r0, rows), :])

    for c in range(t_total // rows):
        prep(c)

    qpos = lax.broadcasted_iota(jnp.int32, (ATT_BLOCK, ATT_BLOCK), 0)
    kpos = lax.broadcasted_iota(jnp.int32, (ATT_BLOCK, ATT_BLOCK), 1)
    neg = -jnp.inf

    for qi in range(n_blocks):
        q = qf_ref[qi * ATT_BLOCK:(qi + 1) * ATT_BLOCK, :]
        s = _dot_nt(q, kf_ref[0:(qi + 1) * ATT_BLOCK, :])
        parts = [s[:, j * ATT_BLOCK:(j + 1) * ATT_BLOCK] for j in range(qi + 1)]
        parts[0] = jnp.where(kpos >= ATT_PAD, parts[0], neg)
        parts[qi] = jnp.where(kpos <= qpos, parts[qi], neg)
        top = functools.reduce(jnp.maximum, parts)
        m = jnp.max(top, axis=-1, keepdims=True)
        if qi == 0:
            m = jnp.where(m == neg, 0.0, m)
        probs = [jnp.exp(part - m) for part in parts]
        l = jnp.sum(functools.reduce(jnp.add, probs), axis=-1, keepdims=True)
        pv = _dot(jnp.concatenate([_bf(pr) for pr in probs], axis=1), vf_ref[0:(qi + 1) * ATT_BLOCK, :])
        if qi == 0:
            out = pv / jnp.where(l == 0.0, 1.0, l)
            o_ref[0, 0:N_META, :] = out[ATT_PAD:, :].astype(o_ref.dtype)
        else:
            dst = qi * ATT_BLOCK - ATT_PAD
            o_ref[0, dst:dst + ATT_BLOCK, :] = (pv / l).astype(o_ref.dtype)


def _mla(qn, kvn, z, kr_col, tab, gqn, gqr, gkn, gkr):
    b, t, _ = qn.shape
    assert (ATT_PAD + t) % ATT_BLOCK == 0 and t % (3 * BF16_ROWS) == 0
    tp = ATT_PAD + t
    hspec = lambda off: pl.BlockSpec((1, t, LANES), lambda bi, h: (bi, 0, off + h))
    gspec = pl.BlockSpec((1, LANES), lambda bi, h: (0, 0))
    return pl.pallas_call(
        _mla_kernel,
        grid=(b, D_HEADS),
        in_specs=[
            hspec(0), hspec(D_HEADS), hspec(0), hspec(D_HEADS),
            pl.BlockSpec((1, t, LANES), lambda bi, h: (bi, 0, kr_col // LANES)),
            pl.BlockSpec((t, LANES), lambda bi, h: (0, 0)),
            gspec, gspec, gspec, gspec,
        ],
        out_specs=pl.BlockSpec((1, t, D_V), lambda bi, h: (bi, 0, h)),
        out_shape=jax.ShapeDtypeStruct((b, t, D_HEADS * D_V), BF16),
        scratch_shapes=[pltpu.VMEM((tp, 2 * LANES), BF16), pltpu.VMEM((tp, 2 * LANES), BF16),
                        pltpu.VMEM((tp, D_V), BF16)],
        compiler_params=_params("parallel", "parallel"),
        name="mla",
    )(qn, qn, kvn, kvn, z, tab, gqn, gqr, gkn, gkr)


def _pack_bf16_pairs(v):
    w = v.shape[1] // 2
    bits = pltpu.bitcast(_bf(v).astype(F32), jnp.uint32)
    return (bits[:, :w] >> 16) | (bits[:, w:] & jnp.uint32(0xFFFF0000))


def _unpack_lo(words):
    return pltpu.bitcast(words << 16, F32)


def _unpack_hi(words):
    return pltpu.bitcast(words & jnp.uint32(0xFFFF0000), F32)


def _router_kernel(x_ref, g_ref, w_ref, b_ref, gate_ref, idx_ref, xg_ref, cnt_ref, carry_ref):
    tm = x_ref.shape[0]

    @pl.when(pl.program_id(0) == 0)
    def _():
        carry_ref[...] = jnp.zeros_like(carry_ref)

    x = x_ref[...]
    ms = jnp.mean(x * x, axis=-1, keepdims=True)
    xn = x * lax.rsqrt(ms + EPS) * g_ref[...]
    logits = _dot(xn, w_ref[...], precision=HIGHEST) + b_ref[...]
    lane = lax.broadcasted_iota(jnp.int32, logits.shape, 1)
    lane_f = lane.astype(F32)
    neg = -jnp.inf
    big = float(LANES)

    is_group = lane < N_GROUPS
    g_max = jnp.max(jnp.where(is_group, logits, neg), axis=-1, keepdims=True)
    g_sum = jnp.sum(jnp.where(is_group, jnp.exp(logits - g_max), 0.0), axis=-1, keepdims=True)
    p_top = 1.0 / g_sum
    grp = jnp.min(jnp.where(is_group & (logits == g_max), lane_f, big), axis=-1, keepdims=True)

    e_lo = N_GROUPS + grp * EXPERTS_PER_GROUP
    in_grp = (lane_f >= e_lo) & (lane_f < e_lo + EXPERTS_PER_GROUP)
    e_max = jnp.max(jnp.where(in_grp, logits, neg), axis=-1, keepdims=True)
    e_sum = jnp.sum(jnp.where(in_grp, jnp.exp(logits - e_max), 0.0), axis=-1, keepdims=True)
    i1 = jnp.min(jnp.where(in_grp & (logits == e_max), lane_f, big), axis=-1, keepdims=True)
    rest = in_grp & (lane_f != i1)
    e_2nd = jnp.max(jnp.where(rest, logits, neg), axis=-1, keepdims=True)
    i2 = jnp.min(jnp.where(rest & (logits == e_2nd), lane_f, big), axis=-1, keepdims=True)
    p1 = 1.0 / e_sum
    p2 = jnp.exp(e_2nd - e_max) / e_sum
    tot = p1 + p2
    gate_ref[...] = jnp.where(lane == 0, p_top * p1 / tot, jnp.where(lane == 1, p_top * p2 / tot, 0.0))

    e1 = i1 - N_GROUPS
    e2 = i2 - N_GROUPS
    hot = jnp.where((lane_f == e1) | (lane_f == e2), 1.0, 0.0)
    row = lax.broadcasted_iota(jnp.int32, (tm, tm), 0)
    col = lax.broadcasted_iota(jnp.int32, (tm, tm), 1)
    before = _dot(jnp.where(col < row, 1.0, 0.0).astype(BF16), _bf(hot)) + carry_ref[...]
    r1 = jnp.sum(jnp.where(lane_f == e1, before, 0.0), axis=-1, keepdims=True)
    r2 = jnp.sum(jnp.where(lane_f == e2, before, 0.0), axis=-1, keepdims=True)
    total = carry_ref[...] + jnp.sum(hot, axis=0, keepdims=True)
    carry_ref[...] = total
    cnt_ref[...] = jnp.broadcast_to(total, cnt_ref.shape).astype(jnp.int32)
    idx_ref[...] = jnp.where(lane == 0, e1, jnp.where(lane == 1, e2, jnp.where(lane == 2, r1, jnp.where(
        lane == 3, r2, 0.0)))).astype(jnp.int32)

    words = _pack_bf16_pairs(xn)
    for s in range(8):
        xg_ref[pl.ds(s, tm, stride=8), :] = words[:, s * LANES:(s + 1) * LANES]


def _router(x2d, gain, w_group, b_group, w_expert, b_expert):
    m, d = x2d.shape
    assert d == 2 * 8 * LANES
    tm = _row_tile(m, 688)
    pad = LANES - N_GROUPS - N_EXPERTS
    w = jnp.concatenate([w_group, w_expert, jnp.zeros((d, pad), F32)], axis=1)
    bias = jnp.concatenate([b_group, b_expert, jnp.zeros((pad,), F32)]).reshape(1, LANES)
    return pl.pallas_call(
        _router_kernel,
        grid=(m // tm,),
        in_specs=[
            pl.BlockSpec((tm, d), lambda i: (i, 0)),
            pl.BlockSpec((1, d), lambda i: (0, 0)),
            pl.BlockSpec((d, LANES), lambda i: (0, 0)),
            pl.BlockSpec((1, LANES), lambda i: (0, 0)),
        ],
        out_specs=[pl.BlockSpec((tm, LANES), lambda i: (i, 0)), pl.BlockSpec((tm, LANES), lambda i: (i, 0)),
                   pl.BlockSpec((tm * 8, LANES), lambda i: (i, 0)), pl.BlockSpec((8, LANES), lambda i: (0, 0))],
        out_shape=[jax.ShapeDtypeStruct((m, LANES), F32), jax.ShapeDtypeStruct((m, LANES), jnp.int32),
                   jax.ShapeDtypeStruct((m * 8, LANES), jnp.uint32), jax.ShapeDtypeStruct((8, LANES), jnp.int32)],
        scratch_shapes=[pltpu.VMEM((1, LANES), F32)],
        compiler_params=_params("arbitrary"),
        name="router",
    )(x2d, gain.reshape(1, d), w, bias)


def _invert_kernel(dest_ref, inv_ref):
    def clear(s, carry):
        inv_ref[s] = -1
        return carry

    lax.fori_loop(0, inv_ref.shape[0], clear, 0, unroll=8)

    def put(f, carry):
        inv_ref[dest_ref[f]] = f
        return carry

    lax.fori_loop(0, dest_ref.shape[0], put, 0, unroll=8)


def _invert(dest, p):
    assert p % 8 == 0 and dest.shape[0] % 8 == 0
    smem = pl.BlockSpec(memory_space=pltpu.SMEM)
    return pl.pallas_call(
        _invert_kernel, in_specs=[smem], out_specs=smem,
        out_shape=jax.ShapeDtypeStruct((p,), jnp.int32), name="moe_invert",
    )(dest)


def _expert_kernel(be_ref, nu_ref, cur_ref, nxt_ref, dst_ref, xg_hbm, w1_ref, w3_ref, w2_ref, o_hbm,
                   xbuf, ybuf, xs_ref, w1s, w3s, w2s, sem_in, sem_out):
    i = pl.program_id(0)
    n_used = nu_ref[0]
    par = i % 2
    half = D_MODEL // 2

    def issue_gathers(tok_ref, slot):
        def body(r, carry):
            src = pl.multiple_of(tok_ref[0, 0, r], 8)
            pltpu.make_async_copy(xg_hbm.at[pl.ds(src, 8), :], xbuf.at[slot, pl.ds(r * 8, 8), :],
                                  sem_in.at[slot]).start()
            return carry

        lax.fori_loop(0, MOE_BLOCK, body, 0, unroll=8)

    def wait_gathers(slot):
        pltpu.make_async_copy(xbuf.at[1 - slot], xbuf.at[slot], sem_in.at[slot]).wait()

    def wait_scatters(slot):
        pltpu.make_async_copy(ybuf.at[slot], ybuf.at[1 - slot], sem_out.at[slot]).wait()

    @pl.when(i == 0)
    def _():
        ybuf[1] = jnp.zeros(ybuf.shape[1:], ybuf.dtype)
        n_real = o_hbm.shape[0] - 2 * MOE_BLOCK * 8
        for q in range(2):
            fill = pltpu.make_async_copy(ybuf.at[1], o_hbm.at[pl.ds(n_real + q * MOE_BLOCK * 8, MOE_BLOCK * 8), :],
                                         sem_out.at[1])
            fill.start()
            fill.wait()

    @pl.when((i == 0) & (n_used > 0))
    def _():
        issue_gathers(cur_ref, 0)

    @pl.when(i + 1 < n_used)
    def _():
        issue_gathers(nxt_ref, 1 - par)

    @pl.when(i < n_used)
    def _():
        @pl.when((i == 0) | (be_ref[i] != be_ref[jnp.maximum(i - 1, 0)]))
        def _():
            w1s[...] = _bf(w1_ref[0])
            w3s[...] = _bf(w3_ref[0])
            w2s[...] = _bf(w2_ref[0])

        wait_gathers(par)

        for s in range(8):
            words = xbuf[par, pl.ds(s, MOE_BLOCK, stride=8), :]
            xs_ref[:, s * LANES:(s + 1) * LANES] = _bf(_unpack_lo(words))
            xs_ref[:, half + s * LANES:half + (s + 1) * LANES] = _bf(_unpack_hi(words))
        xb = xs_ref[...]
        h1 = _dot(xb, w1s[...])
        h3 = _dot(xb, w3s[...])
        y = _dot(_bf(_silu(h1) * h3), w2s[...])

        @pl.when(i >= 2)
        def _():
            wait_scatters(par)

        words = _pack_bf16_pairs(y)
        for s in range(8):
            ybuf[par, pl.ds(s, MOE_BLOCK, stride=8), :] = words[:, s * LANES:(s + 1) * LANES]

        def issue_scatter(r, carry):
            dst = pl.multiple_of(dst_ref[0, 0, r], 8)
            pltpu.make_async_copy(ybuf.at[par, pl.ds(r * 8, 8), :], o_hbm.at[pl.ds(dst, 8), :],
                                  sem_out.at[par]).start()
            return carry

        lax.fori_loop(0, MOE_BLOCK, issue_scatter, 0, unroll=8)

        @pl.when(i == n_used - 1)
        def _():
            wait_scatters(par)

            @pl.when(i >= 1)
            def _():
                wait_scatters(1 - par)


def _moe(x2d, gain, w_group, b_group, w_expert, b_expert, w1, w3, w2):
    n, d = x2d.shape
    gates_l, idx_l, xg, cnt = _router(x2d, gain, w_group, b_group, w_expert, b_expert)

    a = n * TOP_K
    n_blocks = -(-a // MOE_BLOCK) + N_EXPERTS
    p = n_blocks * MOE_BLOCK
    counts = cnt[0, :N_EXPERTS]
    padded = (counts + MOE_BLOCK - 1) // MOE_BLOCK * MOE_BLOCK
    pad_end = jnp.cumsum(padded)
    pad_start = pad_end - padded
    dest = (pad_start[idx_l[:, :TOP_K]] + idx_l[:, TOP_K:2 * TOP_K]).reshape(-1).astype(jnp.int32)
    blk0 = jnp.arange(n_blocks, dtype=jnp.int32) * MOE_BLOCK
    block_expert = jnp.minimum(jnp.searchsorted(pad_end, blk0, side="right"), N_EXPERTS - 1).astype(jnp.int32)
    n_used = (pad_end[-1] // MOE_BLOCK).astype(jnp.int32).reshape(1)
    codes = _invert(dest, p)
    slot = jnp.arange(p, dtype=jnp.int32)
    spare = TOP_K * n + (slot // MOE_BLOCK % 2) * MOE_BLOCK + slot % MOE_BLOCK
    src_tok = ((jnp.maximum(codes, 0) >> 1) * 8).reshape(n_blocks, 1, MOE_BLOCK)
    dst_row = (jnp.where(codes >= 0, (codes & 1) * n + (codes >> 1), spare) * 8).reshape(n_blocks, 1, MOE_BLOCK)

    code_spec = lambda step: pl.BlockSpec(
        (1, 1, MOE_BLOCK), lambda i, be, nu: (jnp.minimum(i + step, n_blocks - 1), 0, 0), memory_space=pltpu.SMEM)
    wspec = lambda shape: pl.BlockSpec((1,) + shape, lambda i, be, nu: (be[i], 0, 0))
    out_rows = TOP_K * n + 2 * MOE_BLOCK
    out2 = pl.pallas_call(
        _expert_kernel,
        grid_spec=pltpu.PrefetchScalarGridSpec(
            num_scalar_prefetch=2,
            grid=(n_blocks,),
            in_specs=[
                code_spec(0), code_spec(1), code_spec(0),
                pl.BlockSpec(memory_space=pl.ANY),
                wspec((d, D_EXPERT)), wspec((d, D_EXPERT)), wspec((D_EXPERT, d)),
            ],
            out_specs=pl.BlockSpec(memory_space=pl.ANY),
            scratch_shapes=[
                pltpu.VMEM((2, MOE_BLOCK * 8, LANES), jnp.uint32), pltpu.VMEM((2, MOE_BLOCK * 8, LANES), jnp.uint32),
                pltpu.VMEM((MOE_BLOCK, d), BF16),
                pltpu.VMEM((d, D_EXPERT), BF16), pltpu.VMEM((d, D_EXPERT), BF16), pltpu.VMEM((D_EXPERT, d), BF16),
                pltpu.SemaphoreType.DMA((2,)), pltpu.SemaphoreType.DMA((2,)),
            ],
        ),
        out_shape=jax.ShapeDtypeStruct((out_rows * 8, LANES), jnp.uint32),
        compiler_params=_params("arbitrary"),
        name="moe_experts",
    )(block_expert, n_used, src_tok, src_tok, dst_row, xg, w1, w3, w2)
    return _combine(x2d, gates_l, out2)


def _combine_kernel(x_ref, gate_ref, a_ref, b_ref, o_ref):
    tm = x_ref.shape[0]
    half = x_ref.shape[1] // 2
    g0 = gate_ref[:, 0:1]
    g1 = gate_ref[:, 1:2]
    for s in range(8):
        wa = a_ref[pl.ds(s, tm, stride=8), :]
        wb = b_ref[pl.ds(s, tm, stride=8), :]
        lo = slice(s * LANES, (s + 1) * LANES)
        hi = slice(half + s * LANES, half + (s + 1) * LANES)
        o_ref[:, lo] = x_ref[:, lo] + (g0 * _unpack_lo(wa) + g1 * _unpack_lo(wb))
        o_ref[:, hi] = x_ref[:, hi] + (g0 * _unpack_hi(wa) + g1 * _unpack_hi(wb))


def _combine(x2d, gates, out2):
    n, d = x2d.shape
    tm = _row_tile(n, 688)
    return pl.pallas_call(
        _combine_kernel,
        grid=(n // tm,),
        in_specs=[
            pl.BlockSpec((tm, d), lambda i: (i, 0)),
            pl.BlockSpec((tm, LANES), lambda i: (i, 0)),
            pl.BlockSpec((tm * 8, LANES), lambda i: (i, 0)),
            pl.BlockSpec((tm * 8, LANES), lambda i: (n // tm + i, 0)),
        ],
        out_specs=pl.BlockSpec((tm, d), lambda i: (i, 0)),
        out_shape=jax.ShapeDtypeStruct((n, d), F32),
        compiler_params=_params("parallel"),
        name="moe_combine",
    )(x2d, gates, out2, out2)


def _gate_layouts(cols, heads, t):
    b = cols.shape[0]
    rows = cols.transpose(0, 2, 1)
    meta = jnp.pad(rows[:, :, :N_META], ((0, 0), (0, 0), (0, CHUNK - N_META)))
    real = rows[:, :, N_META:].reshape(b, heads, -1, CHUNK)
    return rows[..., None], jnp.concatenate([meta[:, :, None, :], real], axis=2)


def _even_layer(x, norm_g, w_in, conv_w, conv_b, b_i, b_f, a_norm, w_gate2, b_gate, b_norm, w_out):
    b, t, d = x.shape
    n = b * t
    a_w = 2 * A_HEADS * A_DK + 2 * A_HEADS * A_DV
    g_w = 2 * A_HEADS
    b_w = 2 * B_HEADS * B_DK + 2 * B_HEADS * B_DV
    main = a_w + b_w
    gate_cols = g_w + GATE_RANK
    w = jnp.concatenate([w_in[:, :a_w], w_in[:, a_w + g_w:a_w + g_w + b_w], w_in[:, a_w:a_w + g_w],
                         w_in[:, a_w + g_w + b_w:], jnp.zeros((d, MXU_DIM - gate_cols), F32)], axis=1).astype(BF16)
    z = _normproj(x.reshape(n, d), norm_g, w).reshape(b, t, main + MXU_DIM)

    gates = z[:, :, main:main + g_w]
    ai_c, ai_r = _gate_layouts(gates[..., :A_HEADS], A_HEADS, t)
    af_c, af_r = _gate_layouts(gates[..., A_HEADS:], A_HEADS, t)
    ya = _mlstm(z, ai_c, af_c, ai_r, af_r, conv_w, conv_b, b_i, b_f, a_norm)

    wg = jnp.zeros((B_HEADS, MXU_DIM, B_DK), F32).at[:, g_w:g_w + GATE_RANK, :].set(
        w_gate2.reshape(GATE_RANK, B_HEADS, B_DK).transpose(1, 0, 2))
    dummy = jnp.zeros((1, B_HEADS * B_DK), F32)
    hp = 2
    hspec = pl.BlockSpec((1, hp * B_DK), lambda bi, g: (0, g))
    yb = _gla_call(
        z, B_HEADS, hp, B_DK, B_DV,
        (a_w, a_w + B_HEADS * B_DK, a_w + 2 * B_HEADS * B_DK, a_w + 2 * B_HEADS * B_DK + B_HEADS * B_DV, main),
        MXU_DIM,
        ((wg, pl.BlockSpec((hp, MXU_DIM, B_DK), lambda bi, g: (g, 0, 0))),
         (b_gate.reshape(1, -1), hspec), (dummy, hspec)),
        b_norm, "gla")
    return _outproj(ya.reshape(n, -1), yb.reshape(n, -1), w_out.astype(BF16), x.reshape(n, d)).reshape(b, t, d)


def _odd_layer(x, lb, norm_g, w_in, c_norm, q_a_norm, w_q_up, kv_a_norm, w_kv_up, q_norm, k_norm, w_out):
    b, t, d = x.shape
    n = b * t
    c_w = 2 * C_HEADS * C_DK + 2 * C_HEADS * C_DV
    swap = (jnp.arange(D_ROPE) + D_ROPE // 2) % D_ROPE
    kr0 = c_w + Q_LORA + KV_LORA
    used = kr0 + 2 * D_ROPE
    total = -(-used // MXU_DIM) * MXU_DIM
    w = jnp.concatenate([w_in, w_in[:, kr0:kr0 + D_ROPE][:, swap], jnp.zeros((d, total - used), F32)],
                        axis=1).astype(BF16)
    z2 = _normproj(x.reshape(n, d), norm_g, w)
    z = z2.reshape(b, t, total)

    hp = 4
    hspec = pl.BlockSpec((1, hp * C_DK), lambda bi, g: (0, g))
    yc = _gla_call(
        z, C_HEADS, hp, C_DK, C_DV,
        (0, C_HEADS * C_DK, 2 * C_HEADS * C_DK, 2 * C_HEADS * C_DK + C_HEADS * C_DV, C_HEADS * C_DK),
        C_DK,
        ((jnp.log(lb).reshape(1, -1), hspec), (jnp.log1p(-lb).reshape(1, -1), hspec), ((1.0 - lb).reshape(1, -1), hspec)),
        c_norm, "hgrn")

    dq = D_NOPE + D_ROPE
    wq = w_q_up.reshape(Q_LORA, D_HEADS, dq)
    wq_rope = wq[:, :, D_NOPE:]
    wq_p = jnp.concatenate([wq[:, :, :D_NOPE].reshape(Q_LORA, -1),
                            jnp.concatenate([wq_rope, wq_rope[:, :, swap]], axis=-1).reshape(Q_LORA, -1)],
                           axis=1).astype(BF16)
    wkv = w_kv_up.reshape(KV_LORA, D_HEADS, D_NOPE + D_V)
    wkv_p = jnp.concatenate([wkv[:, :, :D_NOPE].reshape(KV_LORA, -1), wkv[:, :, D_NOPE:].reshape(KV_LORA, -1)],
                            axis=1).astype(BF16)
    qn = _normproj(z2, q_a_norm, wq_p, x_col_block=c_w // Q_LORA).reshape(b, t, -1)
    kvn = _normproj(z2, kv_a_norm, wkv_p, x_col_block=(c_w + Q_LORA) // KV_LORA).reshape(b, t, -1)

    pos = jnp.arange(t, dtype=F32)
    half = D_ROPE // 2
    inv = ROPE_THETA ** (-jnp.arange(half, dtype=F32) / half)
    ang = pos[:, None] * inv[None, :]
    cos, sin = jnp.cos(ang), jnp.sin(ang)
    tab = jnp.concatenate([cos, cos, -sin, sin], axis=1)
    pair = lambda g: jnp.concatenate([g[D_NOPE:], g[D_NOPE:][swap]]).reshape(1, LANES)
    yd = _mla(qn, kvn, z, kr0, tab, q_norm[:D_NOPE].reshape(1, LANES), pair(q_norm),
              k_norm[:D_NOPE].reshape(1, LANES), pair(k_norm))
    return _outproj(yc.reshape(n, -1), yd.reshape(n, -1), w_out.astype(BF16), x.reshape(n, d)).reshape(b, t, d)


def kernel(x, meta_tokens, ab_norm, ab_w_in, a_conv_w, a_conv_b, a_b_i, a_b_f, a_head_norm, b_w_gate2, b_b_gate, b_head_norm, ab_w_out, cd_norm, cd_w_in, c_lower_bound, c_head_norm, d_q_a_norm, d_w_q_up, d_kv_a_norm, d_w_kv_up, d_q_norm, d_k_norm, cd_w_out, moe_norm, moe_w_group, moe_b_group, moe_w_expert, moe_b_expert, moe_w1, moe_w3, moe_w2):
    b = x.shape[0]
    depth = moe_norm.shape[0]
    h = jnp.concatenate([jnp.broadcast_to(meta_tokens.astype(x.dtype)[None], (b, N_META, D_MODEL)), x], axis=1)
    t = h.shape[1]
    lb_cum = jnp.cumsum(jax.nn.softmax(c_lower_bound.astype(F32), axis=0), axis=0)
    lower_bounds = lb_cum - lb_cum[0]
    for layer in range(depth):
        j = layer // 2
        if layer % 2 == 0:
            h = _even_layer(h, ab_norm[j], ab_w_in[j], a_conv_w[j], a_conv_b[j], a_b_i[j], a_b_f[j], a_head_norm[j],
                            b_w_gate2[j], b_b_gate[j], b_head_norm[j], ab_w_out[j])
        else:
            h = _odd_layer(h, lower_bounds[layer], cd_norm[j], cd_w_in[j], c_head_norm[j], d_q_a_norm[j],
                           d_w_q_up[j], d_kv_a_norm[j], d_w_kv_up[j], d_q_norm[j], d_k_norm[j], cd_w_out[j])
        h = _moe(h.reshape(b * t, D_MODEL), moe_norm[layer], moe_w_group[layer], moe_b_group[layer],
                 moe_w_expert[layer], moe_b_expert[layer], moe_w1[layer], moe_w3[layer], moe_w2[layer]
                 ).reshape(b, t, D_MODEL)
    return h[:, N_META:]
```

```python
import functools
import math

import jax
import jax.numpy as jnp
from jax import lax
from jax.experimental import pallas as pl
from jax.experimental.pallas import tpu as pltpu

F32 = jnp.float32
BF16 = jnp.bfloat16
HIGHEST = lax.Precision.HIGHEST

D_MODEL = 2048
N_META = 16
CHUNK = 64
CONV_K = 4
EPS = 1e-6
A_HEADS, A_DK, A_DV = 4, 128, 256
B_HEADS, B_DK, B_DV = 4, 128, 256
GATE_RANK = 16
GATE_TAU = 16.0
C_HEADS, C_DK, C_DV = 8, 128, 128
D_HEADS, D_NOPE, D_ROPE, D_V = 8, 128, 64, 128
Q_LORA, KV_LORA = 512, 256
ROPE_THETA = 10000.0
N_GROUPS, EXPERTS_PER_GROUP = 4, 8
N_EXPERTS = N_GROUPS * EXPERTS_PER_GROUP
TOP_K = 2
D_EXPERT = 512

LANES = 128
MXU_DIM = 256
BF16_ROWS = 16
VMEM_LIMIT = 56 * 1024 * 1024
MOE_BLOCK = MXU_DIM
ATT_BLOCK = 256
ATT_PAD = ATT_BLOCK - N_META

_NT = (((1,), (1,)), ((), ()))
_TN = (((0,), (0,)), ((), ()))


def _dot(a, b, precision=None):
    return jnp.dot(a, b, preferred_element_type=F32, precision=precision)


def _dot_nt(a, b):
    return lax.dot_general(a, b, _NT, preferred_element_type=F32)


def _dot_tn(a, b):
    return lax.dot_general(a, b, _TN, preferred_element_type=F32)


def _bf(x):
    return x.astype(BF16)


def _log_sigmoid(x):
    return jnp.minimum(x, 0.0) - jnp.log1p(jnp.exp(-jnp.abs(x)))


def _sigmoid(x):
    return 1.0 / (1.0 + jnp.exp(-x))


def _silu(x):
    return x * _sigmoid(x)


def _row_tile(m, cap):
    best = None
    for t in range(BF16_ROWS, min(m, cap) + 1, BF16_ROWS):
        if m % t == 0:
            best = t
    assert best is not None, m
    return best


def _col_tile(n, cap):
    best = None
    for t in range(MXU_DIM, min(n, cap) + 1, MXU_DIM):
        if n % t == 0:
            best = t
    assert best is not None, n
    return best


def _params(*sem):
    return pltpu.CompilerParams(dimension_semantics=sem, vmem_limit_bytes=VMEM_LIMIT)


_RELAYOUT_ROWS = 256


def _even_weight_kernel(wa_ref, wb_ref, wc_ref, o_ref, *, n_plain, n_shift, shift, gate_cols):
    ob = pl.program_id(0)
    rows = o_ref.shape[0]
    chunks = rows // _RELAYOUT_ROWS

    @pl.when(ob < n_plain)
    def _():
        def body(c, carry):
            r = pl.ds(pl.multiple_of(c * _RELAYOUT_ROWS, _RELAYOUT_ROWS), _RELAYOUT_ROWS)
            o_ref[r, :] = _bf(wa_ref[r, :])
            return carry

        lax.fori_loop(0, chunks, body, 0)

    @pl.when((ob >= n_plain) & (ob < n_plain + n_shift))
    def _():
        def body(c, carry):
            r = pl.ds(pl.multiple_of(c * _RELAYOUT_ROWS, _RELAYOUT_ROWS), _RELAYOUT_ROWS)
            wide = jnp.concatenate([wa_ref[r, :], wb_ref[r, :]], axis=1)
            o_ref[r, :] = _bf(wide[:, shift:shift + MXU_DIM])
            return carry

        lax.fori_loop(0, chunks, body, 0)

    @pl.when(ob == n_plain + n_shift)
    def _():
        lane = lax.broadcasted_iota(jnp.int32, (_RELAYOUT_ROWS, LANES), 1)

        def body(c, carry):
            r = pl.ds(pl.multiple_of(c * _RELAYOUT_ROWS, _RELAYOUT_ROWS), _RELAYOUT_ROWS)
            first = jnp.where(lane < shift, wc_ref[r, :], jnp.where(lane < gate_cols, wb_ref[r, :], 0.0))
            o_ref[r, :] = _bf(jnp.concatenate([first, jnp.zeros_like(first)], axis=1))
            return carry

        lax.fori_loop(0, chunks, body, 0)


def _even_weight(w_in, a_w, g_w, b_w, rank):
    d = w_in.shape[0]
    assert a_w % MXU_DIM == 0 and b_w % MXU_DIM == 0 and g_w + rank <= LANES and d % _RELAYOUT_ROWS == 0
    n_plain, n_shift = a_w // MXU_DIM, b_w // MXU_DIM
    n_out = n_plain + n_shift + 1
    last = n_out - 1

    def b_index(ob):
        return (0, jnp.where(ob < n_plain, 0, jnp.where(ob < last, 2 * (ob + 1), (a_w + g_w + b_w) // LANES)))

    return pl.pallas_call(
        functools.partial(_even_weight_kernel, n_plain=n_plain, n_shift=n_shift, shift=g_w, gate_cols=g_w + rank),
        grid=(n_out,),
        in_specs=[
            pl.BlockSpec((d, MXU_DIM), lambda ob: (0, jnp.minimum(ob, last - 1))),
            pl.BlockSpec((d, LANES), b_index),
            pl.BlockSpec((d, LANES), lambda ob: (0, a_w // LANES)),
        ],
        out_specs=pl.BlockSpec((d, MXU_DIM), lambda ob: (0, ob)),
        out_shape=jax.ShapeDtypeStruct((d, n_out * MXU_DIM), BF16),
        compiler_params=_params("parallel"),
        name="even_weight",
    )(w_in, w_in, w_in)


def _odd_weight_kernel(w_ref, o_ref, *, n_plain):
    ob = pl.program_id(0)
    chunks = o_ref.shape[0] // _RELAYOUT_ROWS
    half = D_ROPE // 2

    def body(c, carry):
        r = pl.ds(pl.multiple_of(c * _RELAYOUT_ROWS, _RELAYOUT_ROWS), _RELAYOUT_ROWS)
        w = w_ref[r, :]

        @pl.when(ob < n_plain)
        def _():
            o_ref[r, :] = _bf(w)

        @pl.when(ob == n_plain)
        def _():
            pair = jnp.concatenate([w[:, :D_ROPE], w[:, half:D_ROPE], w[:, :half]], axis=1)
            o_ref[r, :] = _bf(jnp.concatenate([pair, jnp.zeros_like(pair)], axis=1))

        return carry

    lax.fori_loop(0, chunks, body, 0)


def _odd_weight(w_in, kr0):
    d = w_in.shape[0]
    assert kr0 % MXU_DIM == 0 and w_in.shape[1] == kr0 + D_ROPE and d % _RELAYOUT_ROWS == 0
    n_plain = kr0 // MXU_DIM
    return pl.pallas_call(
        functools.partial(_odd_weight_kernel, n_plain=n_plain),
        grid=(n_plain + 1,),
        in_specs=[pl.BlockSpec((d, MXU_DIM), lambda ob: (0, ob))],
        out_specs=pl.BlockSpec((d, MXU_DIM), lambda ob: (0, ob)),
        out_shape=jax.ShapeDtypeStruct((d, kr0 + MXU_DIM), BF16),
        compiler_params=_params("parallel"),
        name="odd_weight",
    )(w_in)


def _normproj_kernel(x_ref, g_ref, w_ref, o_ref, xs_ref):
    tm = xs_ref.shape[0]

    @pl.when(pl.program_id(1) == 0)
    def _():
        def body(c, carry):
            r0 = pl.multiple_of(c * BF16_ROWS, BF16_ROWS)
            x = x_ref[pl.ds(r0, BF16_ROWS), :]
            ms = jnp.mean(x * x, axis=-1, keepdims=True)
            xs_ref[pl.ds(r0, BF16_ROWS), :] = _bf(x * lax.rsqrt(ms + EPS) * g_ref[...])
            return carry

        lax.fori_loop(0, tm // BF16_ROWS, body, 0)

    o_ref[...] = _dot(xs_ref[...], w_ref[...]).astype(o_ref.dtype)


def _normproj(x2d, gain, w, *, x_col_block=0, out_dtype=F32):
    m = x2d.shape[0]
    k, n = w.shape
    tm = _row_tile(m, 688)
    tn = _col_tile(n, 1280)
    return pl.pallas_call(
        _normproj_kernel,
        grid=(m // tm, n // tn),
        in_specs=[
            pl.BlockSpec((tm, k), lambda i, j: (i, x_col_block)),
            pl.BlockSpec((1, k), lambda i, j: (0, 0)),
            pl.BlockSpec((k, tn), lambda i, j: (0, j)),
        ],
        out_specs=pl.BlockSpec((tm, tn), lambda i, j: (i, j)),
        out_shape=jax.ShapeDtypeStruct((m, n), out_dtype),
        scratch_shapes=[pltpu.VMEM((tm, k), BF16)],
        compiler_params=_params("parallel", "arbitrary"),
        name="normproj",
    )(x2d, gain.reshape(1, k).astype(F32), w)


def _outproj_kernel(ya_ref, yb_ref, w_ref, r_ref, o_ref):
    ka = ya_ref.shape[1]
    acc = _dot(ya_ref[...], w_ref[:ka, :]) + _dot(yb_ref[...], w_ref[ka:, :])
    o_ref[...] = r_ref[...] + acc


def _outproj(ya, yb, w, res):
    m, ka = ya.shape
    kb = yb.shape[1]
    n = w.shape[1]
    tm = _row_tile(m, 1376)
    tn = _col_tile(n, 1024)
    return pl.pallas_call(
        _outproj_kernel,
        grid=(m // tm, n // tn),
        in_specs=[
            pl.BlockSpec((tm, ka), lambda i, j: (i, 0)),
            pl.BlockSpec((tm, kb), lambda i, j: (i, 0)),
            pl.BlockSpec((ka + kb, tn), lambda i, j: (0, j)),
            pl.BlockSpec((tm, tn), lambda i, j: (i, j)),
        ],
        out_specs=pl.BlockSpec((tm, tn), lambda i, j: (i, j)),
        out_shape=jax.ShapeDtypeStruct((m, n), F32),
        compiler_params=_params("parallel", "arbitrary"),
        name="outproj",
    )(ya, yb, w, res)


def _mlstm_kernel(bi_ref, bf_ref, q_ref, k_ref, v_ref, og_ref, aic_ref, afc_ref, air_ref, afr_ref,
                  cwq_ref, cwk_ref, cbq_ref, cbk_ref, hn_ref, o_ref, c_ref, n_ref, m_ref, *, heads):
    t_total = q_ref.shape[1]
    n_chunks = (t_total - N_META) // CHUNK
    head0 = pl.program_id(1) * heads

    c_ref[...] = jnp.zeros_like(c_ref)
    n_ref[...] = jnp.zeros_like(n_ref)
    m_ref[...] = jnp.zeros_like(m_ref)

    def conv(win, cw, cb, length):
        y = cb
        for j in range(CONV_K):
            y = y + win[8 - (CONV_K - 1) + j:8 - (CONV_K - 1) + j + length, :] * cw[j:j + 1, :]
        return _silu(y)

    hs = range(heads)
    kcol = [slice(hh * A_DK, (hh + 1) * A_DK) for hh in hs]
    vcol = [slice(hh * A_DV, (hh + 1) * A_DV) for hh in hs]

    def gates(hh, o, ci, length, causal, upper):
        b_i = bi_ref[head0 + hh]
        b_f = bf_ref[head0 + hh]
        ig_c = aic_ref[0, hh, pl.ds(o, length), :] + b_i
        lf_c = _log_sigmoid(afc_ref[0, hh, pl.ds(o, length), :] + b_f)
        ig_r = air_ref[0, hh, pl.ds(ci, 1), :][:, :length] + b_i
        lf_r = _log_sigmoid(afr_ref[0, hh, pl.ds(ci, 1), :][:, :length] + b_f)
        b_c = jnp.sum(jnp.where(causal, lf_r, 0.0), axis=1, keepdims=True)
        b_r = jnp.sum(jnp.where(upper, lf_c, 0.0), axis=0, keepdims=True)
        b_end = b_c[length - 1:length, :]
        w_end = b_end - b_c + ig_c
        m_loc = jnp.max(w_end, axis=0, keepdims=True)
        d = jnp.where(causal, b_c - b_r + ig_r, -jnp.inf)
        return b_c, b_end, jnp.exp(w_end - m_loc), m_loc, d, jnp.max(d, axis=1, keepdims=True)

    def chunk(o, ci, length, qwin, kwin):
        row = lax.broadcasted_iota(jnp.int32, (length, length), 0)
        col = lax.broadcasted_iota(jnp.int32, (length, length), 1)
        causal = col <= row
        gt = [gates(hh, o, ci, length, causal, row <= col) for hh in hs]
        q = [conv(qwin[hh], cwq_ref[:, kcol[hh]], cbq_ref[:, kcol[hh]], length) for hh in hs]
        k = [conv(kwin[hh], cwk_ref[:, kcol[hh]], cbk_ref[:, kcol[hh]], length) * (A_DK ** -0.5) for hh in hs]
        vb = [_bf(v_ref[0, pl.ds(o, length), vcol[hh]]) for hh in hs]
        qb = [_bf(q[hh]) for hh in hs]
        k_w = [k[hh] * gt[hh][2] for hh in hs]
        qk = [_dot_nt(qb[hh], _bf(k[hh])) for hh in hs]
        c_in = [c_ref[hh] for hh in hs]
        q_c = [_dot(qb[hh], _bf(c_in[hh])) for hh in hs]
        c_loc = [_dot_tn(_bf(k_w[hh]), vb[hh]) for hh in hs]
        s, a_t, m_t, q_n = [], [], [], []
        for hh in hs:
            b_c, b_end, _, m_loc, d, d_max = gt[hh]
            m_in = m_ref[hh]
            inter = b_c + m_in
            m_t.append(jnp.maximum(inter, d_max))
            s.append(qk[hh] * jnp.exp(d - m_t[hh]))
            a_t.append(jnp.exp(inter - m_t[hh]))
            m_new = jnp.maximum(b_end + m_in, m_loc)
            a = jnp.exp(b_end + m_in - m_new)
            c = jnp.exp(m_loc - m_new)
            n_in = n_ref[hh]
            c_ref[hh] = a * c_in[hh] + c * c_loc[hh]
            n_ref[hh] = a * n_in + c * jnp.sum(k_w[hh], axis=0, keepdims=True)
            m_ref[hh] = m_new
            q_n.append(jnp.sum(q[hh] * n_in, axis=1, keepdims=True))
        num = [_dot(_bf(s[hh]), vb[hh]) + a_t[hh] * q_c[hh] for hh in hs]
        for hh in hs:
            den = jnp.sum(s[hh], axis=1, keepdims=True) + a_t[hh] * q_n[hh]
            h = num[hh] / jnp.maximum(jnp.abs(den), jnp.exp(-m_t[hh]))
            hn = h * lax.rsqrt(jnp.mean(h * h, axis=-1, keepdims=True) + EPS) * hn_ref[:, vcol[hh]]
            y = _sigmoid(og_ref[0, pl.ds(o, length), vcol[hh]]) * hn
            o_ref[0, pl.ds(o, length), vcol[hh]] = y.astype(o_ref.dtype)

    zeros8 = jnp.zeros((8, A_DK), F32)
    chunk(0, 0, N_META,
          [jnp.concatenate([zeros8, q_ref[0, 0:N_META, kcol[hh]]], axis=0) for hh in hs],
          [jnp.concatenate([zeros8, k_ref[0, 0:N_META, kcol[hh]]], axis=0) for hh in hs])

    def body(c, carry):
        o = pl.multiple_of(N_META + c * CHUNK, BF16_ROWS)
        w0 = pl.multiple_of(N_META - 8 + c * CHUNK, 8)
        chunk(o, c + 1, CHUNK, [q_ref[0, pl.ds(w0, CHUNK + 8), kcol[hh]] for hh in hs],
              [k_ref[0, pl.ds(w0, CHUNK + 8), kcol[hh]] for hh in hs])
        return carry

    lax.fori_loop(0, n_chunks, body, 0)


def _mlstm(z, ai_c, af_c, ai_r, af_r, conv_w, conv_b, b_i, b_f, head_norm, heads=2):
    b, t, _ = z.shape
    hk = A_HEADS * A_DK
    wk, wv = heads * A_DK, heads * A_DV
    smem = pl.BlockSpec(memory_space=pltpu.SMEM)
    col = lambda width, off: (lambda bi, g: (bi, 0, off // width + g))
    gate_c = pl.BlockSpec((1, heads, t, 1), lambda bi, g: (bi, g, 0, 0))
    gate_r = pl.BlockSpec((1, heads) + ai_r.shape[2:], lambda bi, g: (bi, g, 0, 0))
    return pl.pallas_call(
        functools.partial(_mlstm_kernel, heads=heads),
        grid=(b, A_HEADS // heads),
        in_specs=[
            smem, smem,
            pl.BlockSpec((1, t, wk), col(wk, 0)),
            pl.BlockSpec((1, t, wk), col(wk, hk)),
            pl.BlockSpec((1, t, wv), col(wv, 2 * hk)),
            pl.BlockSpec((1, t, wv), col(wv, 2 * hk + A_HEADS * A_DV)),
            gate_c, gate_c, gate_r, gate_r,
            pl.BlockSpec((CONV_K, wk), lambda bi, g: (0, g)),
            pl.BlockSpec((CONV_K, wk), lambda bi, g: (0, A_HEADS // heads + g)),
            pl.BlockSpec((1, wk), lambda bi, g: (0, g)),
            pl.BlockSpec((1, wk), lambda bi, g: (0, A_HEADS // heads + g)),
            pl.BlockSpec((1, wv), lambda bi, g: (0, g)),
        ],
        out_specs=pl.BlockSpec((1, t, wv), lambda bi, g: (bi, 0, g)),
        out_shape=jax.ShapeDtypeStruct((b, t, A_HEADS * A_DV), BF16),
        scratch_shapes=[pltpu.VMEM((heads, A_DK, A_DV), F32), pltpu.VMEM((heads, 1, A_DK), F32),
                        pltpu.VMEM((heads, 1, 1), F32)],
        compiler_params=_params("parallel", "parallel"),
        name="mlstm",
    )(b_i, b_f, z, z, z, z, ai_c, af_c, ai_r, af_r, conv_w, conv_w, conv_b.reshape(1, -1), conv_b.reshape(1, -1),
      head_norm.reshape(1, -1))


def _gla_kernel(q_ref, k_ref, v_ref, og_ref, g_ref, p0_ref, p1_ref, p2_ref, hn_ref, o_ref, st_ref, *, mode, heads):
    t_total = q_ref.shape[1]
    n_chunks = (t_total - N_META) // CHUNK
    dv, dk = st_ref.shape[1:]
    st_ref[...] = jnp.zeros_like(st_ref)

    hs = range(heads)
    kcol = [slice(hh * dk, (hh + 1) * dk) for hh in hs]
    vcol = [slice(hh * dv, (hh + 1) * dv) for hh in hs]

    def gate_inputs(hh, o, length):
        q = q_ref[0, pl.ds(o, length), kcol[hh]]
        if mode == "gla":
            pre = _dot(g_ref[0, pl.ds(o, length), :], p0_ref[hh], precision=HIGHEST) + p1_ref[:, kcol[hh]]
            return q * (dk ** -0.5), k_ref[0, pl.ds(o, length), kcol[hh]], _log_sigmoid(pre) / GATE_TAU
        fpre = g_ref[0, pl.ds(o, length), kcol[hh]]
        a = p0_ref[:, kcol[hh]]
        bb = p1_ref[:, kcol[hh]] + _log_sigmoid(fpre)
        lg = jnp.maximum(a, bb) + jnp.log1p(jnp.exp(-jnp.abs(a - bb)))
        return q, p2_ref[:, kcol[hh]] * _sigmoid(-fpre), lg

    def cumsum_time(tri, lg):
        hi = _bf(lg)
        r1 = lg - hi.astype(F32)
        mid = _bf(r1)
        lo = _bf(r1 - mid.astype(F32))
        parts = _dot(tri, jnp.concatenate([hi, mid, lo], axis=1))
        return parts[:, :dk] + (parts[:, dk:2 * dk] + parts[:, 2 * dk:])

    def chunk(o, length):
        row = lax.broadcasted_iota(jnp.int32, (length, length), 0)
        col = lax.broadcasted_iota(jnp.int32, (length, length), 1)
        causal = col <= row
        tri = jnp.where(causal, 1.0, 0.0).astype(BF16)
        qkl = [gate_inputs(hh, o, length) for hh in hs]
        vb = [_bf(v_ref[0, pl.ds(o, length), vcol[hh]]) for hh in hs]
        g = [cumsum_time(tri, qkl[hh][2]) for hh in hs]
        g_end = [g[hh][length - 1:length, :] for hh in hs]
        g_mid = [g[hh][length // 2:length // 2 + 1, :] for hh in hs]
        s = [_dot_nt(_bf(qkl[hh][0] * jnp.exp(g[hh] - g_mid[hh])), _bf(qkl[hh][1] * jnp.exp(g_mid[hh] - g[hh])))
             for hh in hs]
        st_in = [st_ref[hh] for hh in hs]
        inter = [_dot_nt(_bf(qkl[hh][0] * jnp.exp(g[hh])), _bf(st_in[hh])) for hh in hs]
        local = [_dot_tn(vb[hh], _bf(qkl[hh][1] * jnp.exp(g_end[hh] - g[hh]))) for hh in hs]
        for hh in hs:
            st_ref[hh] = st_in[hh] * jnp.exp(g_end[hh]) + local[hh]
        out = [_dot(_bf(jnp.where(causal, s[hh], 0.0)), vb[hh]) + inter[hh] for hh in hs]
        for hh in hs:
            hn = out[hh] * lax.rsqrt(jnp.mean(out[hh] * out[hh], axis=-1, keepdims=True) + EPS) * hn_ref[:, vcol[hh]]
            og = og_ref[0, pl.ds(o, length), vcol[hh]]
            gate = _silu(og) if mode == "gla" else _sigmoid(og)
            o_ref[0, pl.ds(o, length), vcol[hh]] = (gate * hn).astype(o_ref.dtype)

    chunk(0, N_META)

    def body(c, carry):
        chunk(pl.multiple_of(N_META + c * CHUNK, BF16_ROWS), CHUNK)
        return carry

    lax.fori_loop(0, n_chunks, body, 0)


def _gla_call(z, n_heads, heads, dk, dv, blocks, gate_width, params, head_norm, mode):
    b, t, _ = z.shape
    q0, k0, v0, og0, g0 = blocks
    zspec = lambda width, off, grouped=True: pl.BlockSpec(
        (1, t, width), (lambda bi, g: (bi, 0, off // width + (g if grouped else 0))))
    (p0, s0), (p1, s1), (p2, s2) = params
    gate_spec = zspec(gate_width, g0, grouped=False) if mode == "gla" else zspec(heads * dk, g0)
    return pl.pallas_call(
        functools.partial(_gla_kernel, mode=mode, heads=heads),
        grid=(b, n_heads // heads),
        in_specs=[
            zspec(heads * dk, q0), zspec(heads * dk, k0), zspec(heads * dv, v0), zspec(heads * dv, og0),
            gate_spec, s0, s1, s2,
            pl.BlockSpec((1, heads * dv), lambda bi, g: (0, g)),
        ],
        out_specs=pl.BlockSpec((1, t, heads * dv), lambda bi, g: (bi, 0, g)),
        out_shape=jax.ShapeDtypeStruct((b, t, n_heads * dv), BF16),
        scratch_shapes=[pltpu.VMEM((heads, dv, dk), F32)],
        compiler_params=_params("parallel", "parallel"),
        name="gla_" + mode,
    )(z, z, z, z, z, p0, p1, p2, head_norm.reshape(1, -1))


def _mla_kernel(qn_ref, qr_ref, kn_ref, v_ref, kr_ref, tab_ref, gqn_ref, gqr_ref, gkn_ref, gkr_ref,
                o_ref, qf_ref, kf_ref, vf_ref):
    t_total = qn_ref.shape[1]
    n_blocks = (ATT_PAD + t_total) // ATT_BLOCK
    dqk = D_NOPE + D_ROPE
    scale = dqk ** -0.5
    rows = _row_tile(t_total, 768)
    lane = lax.broadcasted_iota(jnp.int32, (rows, LANES), 1)
    first_half = lane < D_ROPE

    qf_ref[0:ATT_PAD, :] = jnp.zeros((ATT_PAD, 2 * LANES), BF16)
    kf_ref[0:ATT_PAD, :] = jnp.zeros((ATT_PAD, 2 * LANES), BF16)
    vf_ref[0:ATT_PAD, :] = jnp.zeros((ATT_PAD, D_V), BF16)

    def rope_pair(x, gains, tab):
        p = x * gains * tab
        return p + pltpu.roll(p, D_ROPE, 1)

    def prep(c):
        r0 = c * rows
        dst = ATT_PAD + c * rows
        tab = tab_ref[pl.ds(r0, rows), :]
        qn = qn_ref[0, pl.ds(r0, rows), :]
        qr = qr_ref[0, pl.ds(r0, rows), :]
        ssq = jnp.sum(qn * qn, axis=-1, keepdims=True) + 0.5 * jnp.sum(qr * qr, axis=-1, keepdims=True)
        rq = lax.rsqrt(ssq / dqk + EPS) * scale
        qf_ref[pl.ds(dst, rows), 0:LANES] = _bf(qn * gqn_ref[...] * rq)
        qf_ref[pl.ds(dst, rows), LANES:2 * LANES] = _bf(rope_pair(qr, gqr_ref[...], tab) * rq)
        kn = kn_ref[0, pl.ds(r0, rows), :]
        kr = kr_ref[0, pl.ds(r0, rows), :]
        ssk = jnp.sum(kn * kn, axis=-1, keepdims=True) + 0.5 * jnp.sum(kr * kr, axis=-1, keepdims=True)
        rk = lax.rsqrt(ssk / dqk + EPS)
        kf_ref[pl.ds(dst, rows), 0:LANES] = _bf(kn * gkn_ref[...] * rk)
        kf_ref[pl.ds(dst, rows), LANES:2 * LANES] = _bf(
            jnp.where(first_half, rope_pair(kr, gkr_ref[...], tab) * rk, 0.0))
        vf_ref[pl.ds(dst, rows), :] = _bf(v_ref[0, pl.ds(r0, rows), :])

    for c in range(t_total // rows):
        prep(c)

    qpos = lax.broadcasted_iota(jnp.int32, (ATT_BLOCK, ATT_BLOCK), 0)
    kpos = lax.broadcasted_iota(jnp.int32, (ATT_BLOCK, ATT_BLOCK), 1)
    neg = -jnp.inf

    for qi in range(n_blocks):
        q = qf_ref[qi * ATT_BLOCK:(qi + 1) * ATT_BLOCK, :]
        s = _dot_nt(q, kf_ref[0:(qi + 1) * ATT_BLOCK, :])
        parts = [s[:, j * ATT_BLOCK:(j + 1) * ATT_BLOCK] for j in range(qi + 1)]
        parts[0] = jnp.where(kpos >= ATT_PAD, parts[0], neg)
        parts[qi] = jnp.where(kpos <= qpos, parts[qi], neg)
        top = functools.reduce(jnp.maximum, parts)
        m = jnp.max(top, axis=-1, keepdims=True)
        if qi == 0:
            m = jnp.where(m == neg, 0.0, m)
        probs = [jnp.exp(part - m) for part in parts]
        l = jnp.sum(functools.reduce(jnp.add, probs), axis=-1, keepdims=True)
        pv = _dot(jnp.concatenate([_bf(pr) for pr in probs], axis=1), vf_ref[0:(qi + 1) * ATT_BLOCK, :])
        if qi == 0:
            out = pv / jnp.where(l == 0.0, 1.0, l)
            o_ref[0, 0:N_META, :] = out[ATT_PAD:, :].astype(o_ref.dtype)
        else:
            dst = qi * ATT_BLOCK - ATT_PAD
            o_ref[0, dst:dst + ATT_BLOCK, :] = (pv / l).astype(o_ref.dtype)


def _mla(qn, kvn, z, kr_col, tab, gqn, gqr, gkn, gkr):
    b, t, _ = qn.shape
    assert (ATT_PAD + t) % ATT_BLOCK == 0 and t % (3 * BF16_ROWS) == 0
    tp = ATT_PAD + t
    hspec = lambda off: pl.BlockSpec((1, t, LANES), lambda bi, h: (bi, 0, off + h))
    gspec = pl.BlockSpec((1, LANES), lambda bi, h: (0, 0))
    return pl.pallas_call(
        _mla_kernel,
        grid=(b, D_HEADS),
        in_specs=[
            hspec(0), hspec(D_HEADS), hspec(0), hspec(D_HEADS),
            pl.BlockSpec((1, t, LANES), lambda bi, h: (bi, 0, kr_col // LANES)),
            pl.BlockSpec((t, LANES), lambda bi, h: (0, 0)),
            gspec, gspec, gspec, gspec,
        ],
        out_specs=pl.BlockSpec((1, t, D_V), lambda bi, h: (bi, 0, h)),
        out_shape=jax.ShapeDtypeStruct((b, t, D_HEADS * D_V), BF16),
        scratch_shapes=[pltpu.VMEM((tp, 2 * LANES), BF16), pltpu.VMEM((tp, 2 * LANES), BF16),
                        pltpu.VMEM((tp, D_V), BF16)],
        compiler_params=_params("parallel", "parallel"),
        name="mla",
    )(qn, qn, kvn, kvn, z, tab, gqn, gqr, gkn, gkr)


def _pack_bf16_pairs(v):
    w = v.shape[1] // 2
    bits = pltpu.bitcast(_bf(v).astype(F32), jnp.uint32)
    return (bits[:, :w] >> 16) | (bits[:, w:] & jnp.uint32(0xFFFF0000))


def _unpack_lo(words):
    return pltpu.bitcast(words << 16, F32)


def _unpack_hi(words):
    return pltpu.bitcast(words & jnp.uint32(0xFFFF0000), F32)


def _router_kernel(x_ref, g_ref, w_ref, b_ref, gate_ref, idx_ref, xg_ref, cnt_ref, carry_ref):
    tm = x_ref.shape[0]

    @pl.when(pl.program_id(0) == 0)
    def _():
        carry_ref[...] = jnp.zeros_like(carry_ref)

    x = x_ref[...]
    ms = jnp.mean(x * x, axis=-1, keepdims=True)
    xn = x * lax.rsqrt(ms + EPS) * g_ref[...]
    logits = _dot(xn, w_ref[...], precision=HIGHEST) + b_ref[...]
    lane = lax.broadcasted_iota(jnp.int32, logits.shape, 1)
    lane_f = lane.astype(F32)
    neg = -jnp.inf
    big = float(LANES)

    is_group = lane < N_GROUPS
    g_max = jnp.max(jnp.where(is_group, logits, neg), axis=-1, keepdims=True)
    g_sum = jnp.sum(jnp.where(is_group, jnp.exp(logits - g_max), 0.0), axis=-1, keepdims=True)
    p_top = 1.0 / g_sum
    grp = jnp.min(jnp.where(is_group & (logits == g_max), lane_f, big), axis=-1, keepdims=True)

    e_lo = N_GROUPS + grp * EXPERTS_PER_GROUP
    in_grp = (lane_f >= e_lo) & (lane_f < e_lo + EXPERTS_PER_GROUP)
    e_max = jnp.max(jnp.where(in_grp, logits, neg), axis=-1, keepdims=True)
    e_sum = jnp.sum(jnp.where(in_grp, jnp.exp(logits - e_max), 0.0), axis=-1, keepdims=True)
    i1 = jnp.min(jnp.where(in_grp & (logits == e_max), lane_f, big), axis=-1, keepdims=True)
    rest = in_grp & (lane_f != i1)
    e_2nd = jnp.max(jnp.where(rest, logits, neg), axis=-1, keepdims=True)
    i2 = jnp.min(jnp.where(rest & (logits == e_2nd), lane_f, big), axis=-1, keepdims=True)
    p1 = 1.0 / e_sum
    p2 = jnp.exp(e_2nd - e_max) / e_sum
    tot = p1 + p2
    gate_ref[...] = jnp.where(lane == 0, p_top * p1 / tot, jnp.where(lane == 1, p_top * p2 / tot, 0.0))

    e1 = i1 - N_GROUPS
    e2 = i2 - N_GROUPS
    hot = jnp.where((lane_f == e1) | (lane_f == e2), 1.0, 0.0)
    row = lax.broadcasted_iota(jnp.int32, (tm, tm), 0)
    col = lax.broadcasted_iota(jnp.int32, (tm, tm), 1)
    before = _dot(jnp.where(col < row, 1.0, 0.0).astype(BF16), _bf(hot)) + carry_ref[...]
    r1 = jnp.sum(jnp.where(lane_f == e1, before, 0.0), axis=-1, keepdims=True)
    r2 = jnp.sum(jnp.where(lane_f == e2, before, 0.0), axis=-1, keepdims=True)
    total = carry_ref[...] + jnp.sum(hot, axis=0, keepdims=True)
    carry_ref[...] = total
    cnt_ref[...] = jnp.broadcast_to(total, cnt_ref.shape).astype(jnp.int32)
    idx_ref[...] = jnp.where(lane == 0, e1, jnp.where(lane == 1, e2, jnp.where(lane == 2, r1, jnp.where(
        lane == 3, r2, 0.0)))).astype(jnp.int32)

    words = _pack_bf16_pairs(xn)
    for s in range(8):
        xg_ref[pl.ds(s, tm, stride=8), :] = words[:, s * LANES:(s + 1) * LANES]


def _router(x2d, gain, w_group, b_group, w_expert, b_expert):
    m, d = x2d.shape
    assert d == 2 * 8 * LANES
    tm = _row_tile(m, 688)
    pad = LANES - N_GROUPS - N_EXPERTS
    w = jnp.concatenate([w_group, w_expert, jnp.zeros((d, pad), F32)], axis=1)
    bias = jnp.concatenate([b_group, b_expert, jnp.zeros((pad,), F32)]).reshape(1, LANES)
    return pl.pallas_call(
        _router_kernel,
        grid=(m // tm,),
        in_specs=[
            pl.BlockSpec((tm, d), lambda i: (i, 0)),
            pl.BlockSpec((1, d), lambda i: (0, 0)),
            pl.BlockSpec((d, LANES), lambda i: (0, 0)),
            pl.BlockSpec((1, LANES), lambda i: (0, 0)),
        ],
        out_specs=[pl.BlockSpec((tm, LANES), lambda i: (i, 0)), pl.BlockSpec((tm, LANES), lambda i: (i, 0)),
                   pl.BlockSpec((tm * 8, LANES), lambda i: (i, 0)), pl.BlockSpec((8, LANES), lambda i: (0, 0))],
        out_shape=[jax.ShapeDtypeStruct((m, LANES), F32), jax.ShapeDtypeStruct((m, LANES), jnp.int32),
                   jax.ShapeDtypeStruct((m * 8, LANES), jnp.uint32), jax.ShapeDtypeStruct((8, LANES), jnp.int32)],
        scratch_shapes=[pltpu.VMEM((1, LANES), F32)],
        compiler_params=_params("arbitrary"),
        name="router",
    )(x2d, gain.reshape(1, d), w, bias)


def _invert_kernel(dest_ref, inv_ref):
    def clear(s, carry):
        inv_ref[s] = -1
        return carry

    lax.fori_loop(0, inv_ref.shape[0], clear, 0, unroll=8)

    def put(f, carry):
        inv_ref[dest_ref[f]] = f
        return carry

    lax.fori_loop(0, dest_ref.shape[0], put, 0, unroll=8)


def _invert(dest, p):
    assert p % 8 == 0 and dest.shape[0] % 8 == 0
    smem = pl.BlockSpec(memory_space=pltpu.SMEM)
    return pl.pallas_call(
        _invert_kernel, in_specs=[smem], out_specs=smem,
        out_shape=jax.ShapeDtypeStruct((p,), jnp.int32), name="moe_invert",
    )(dest)


def _expert_kernel(be_ref, nu_ref, src_ref, dst_ref, xg_hbm, w1_ref, w3_ref, w2_ref, o_hbm,
                   xbuf, ybuf, xs_ref, w1s, w3s, w2s, sem_in, sem_out):
    i = pl.program_id(0)
    n_used = nu_ref[0]
    par = i % 2
    half = D_MODEL // 2

    def issue_gathers(block, slot):
        def body(r, carry):
            src = pl.multiple_of(src_ref[block * MOE_BLOCK + r], 8)
            pltpu.make_async_copy(xg_hbm.at[pl.ds(src, 8), :], xbuf.at[slot, pl.ds(r * 8, 8), :],
                                  sem_in.at[slot]).start()
            return carry

        lax.fori_loop(0, MOE_BLOCK, body, 0, unroll=8)

    def wait_gathers(slot):
        pltpu.make_async_copy(xbuf.at[1 - slot], xbuf.at[slot], sem_in.at[slot]).wait()

    def wait_scatters(slot):
        pltpu.make_async_copy(ybuf.at[slot], ybuf.at[1 - slot], sem_out.at[slot]).wait()

    @pl.when(i == 0)
    def _():
        ybuf[1] = jnp.zeros(ybuf.shape[1:], ybuf.dtype)
        n_real = o_hbm.shape[0] - 2 * MOE_BLOCK * 8
        for q in range(2):
            fill = pltpu.make_async_copy(ybuf.at[1], o_hbm.at[pl.ds(n_real + q * MOE_BLOCK * 8, MOE_BLOCK * 8), :],
                                         sem_out.at[1])
            fill.start()
            fill.wait()

    @pl.when((i == 0) & (n_used > 0))
    def _():
        issue_gathers(0, 0)

    @pl.when(i + 1 < n_used)
    def _():
        issue_gathers(i + 1, 1 - par)

    @pl.when(i < n_used)
    def _():
        @pl.when((i == 0) | (be_ref[i] != be_ref[jnp.maximum(i - 1, 0)]))
        def _():
            w1s[...] = _bf(w1_ref[0])
            w3s[...] = _bf(w3_ref[0])
            w2s[...] = _bf(w2_ref[0])

        wait_gathers(par)

        for s in range(8):
            words = xbuf[par, pl.ds(s, MOE_BLOCK, stride=8), :]
            xs_ref[:, s * LANES:(s + 1) * LANES] = _bf(_unpack_lo(words))
            xs_ref[:, half + s * LANES:half + (s + 1) * LANES] = _bf(_unpack_hi(words))
        xb = xs_ref[...]
        h1 = _dot(xb, w1s[...])
        h3 = _dot(xb, w3s[...])
        y = _dot(_bf(_silu(h1) * h3), w2s[...])

        @pl.when(i >= 2)
        def _():
            wait_scatters(par)

        words = _pack_bf16_pairs(y)
        for s in range(8):
            ybuf[par, pl.ds(s, MOE_BLOCK, stride=8), :] = words[:, s * LANES:(s + 1) * LANES]

        def issue_scatter(r, carry):
            dst = pl.multiple_of(dst_ref[i * MOE_BLOCK + r], 8)
            pltpu.make_async_copy(ybuf.at[par, pl.ds(r * 8, 8), :], o_hbm.at[pl.ds(dst, 8), :],
                                  sem_out.at[par]).start()
            return carry

        lax.fori_loop(0, MOE_BLOCK, issue_scatter, 0, unroll=8)

        @pl.when(i == n_used - 1)
        def _():
            wait_scatters(par)

            @pl.when(i >= 1)
            def _():
                wait_scatters(1 - par)


def _moe(x2d, gain, w_group, b_group, w_expert, b_expert, w1, w3, w2):
    n, d = x2d.shape
    gates_l, idx_l, xg, cnt = _router(x2d, gain, w_group, b_group, w_expert, b_expert)

    a = n * TOP_K
    n_blocks = -(-a // MOE_BLOCK) + N_EXPERTS
    p = n_blocks * MOE_BLOCK
    counts = cnt[0, :N_EXPERTS]
    padded = (counts + MOE_BLOCK - 1) // MOE_BLOCK * MOE_BLOCK
    pad_end = jnp.cumsum(padded)
    pad_start = pad_end - padded
    dest = (pad_start[idx_l[:, :TOP_K]] + idx_l[:, TOP_K:2 * TOP_K]).reshape(-1).astype(jnp.int32)
    blk0 = jnp.arange(n_blocks, dtype=jnp.int32) * MOE_BLOCK
    block_expert = jnp.minimum(jnp.searchsorted(pad_end, blk0, side="right"), N_EXPERTS - 1).astype(jnp.int32)
    n_used = (pad_end[-1] // MOE_BLOCK).astype(jnp.int32).reshape(1)
    codes = _invert(dest, p)
    slot = jnp.arange(p, dtype=jnp.int32)
    spare = TOP_K * n + (slot // MOE_BLOCK % 2) * MOE_BLOCK + slot % MOE_BLOCK
    src_tok = (jnp.maximum(codes, 0) >> 1) * 8
    dst_row = jnp.where(codes >= 0, (codes & 1) * n + (codes >> 1), spare) * 8

    wspec = lambda shape: pl.BlockSpec((1,) + shape, lambda i, be, nu, src, dst: (be[i], 0, 0))
    out_rows = TOP_K * n + 2 * MOE_BLOCK
    out2 = pl.pallas_call(
        _expert_kernel,
        grid_spec=pltpu.PrefetchScalarGridSpec(
            num_scalar_prefetch=4,
            grid=(n_blocks,),
            in_specs=[
                pl.BlockSpec(memory_space=pl.ANY),
                wspec((d, D_EXPERT)), wspec((d, D_EXPERT)), wspec((D_EXPERT, d)),
            ],
            out_specs=pl.BlockSpec(memory_space=pl.ANY),
            scratch_shapes=[
                pltpu.VMEM((2, MOE_BLOCK * 8, LANES), jnp.uint32), pltpu.VMEM((2, MOE_BLOCK * 8, LANES), jnp.uint32),
                pltpu.VMEM((MOE_BLOCK, d), BF16),
                pltpu.VMEM((d, D_EXPERT), BF16), pltpu.VMEM((d, D_EXPERT), BF16), pltpu.VMEM((D_EXPERT, d), BF16),
                pltpu.SemaphoreType.DMA((2,)), pltpu.SemaphoreType.DMA((2,)),
            ],
        ),
        out_shape=jax.ShapeDtypeStruct((out_rows * 8, LANES), jnp.uint32),
        compiler_params=_params("arbitrary"),
        name="moe_experts",
    )(block_expert, n_used, src_tok, dst_row, xg, w1, w3, w2)
    return _combine(x2d, gates_l, out2)


def _combine_kernel(x_ref, gate_ref, a_ref, b_ref, o_ref):
    tm = x_ref.shape[0]
    half = x_ref.shape[1] // 2
    g0 = gate_ref[:, 0:1]
    g1 = gate_ref[:, 1:2]
    for s in range(8):
        wa = a_ref[pl.ds(s, tm, stride=8), :]
        wb = b_ref[pl.ds(s, tm, stride=8), :]
        lo = slice(s * LANES, (s + 1) * LANES)
        hi = slice(half + s * LANES, half + (s + 1) * LANES)
        o_ref[:, lo] = x_ref[:, lo] + (g0 * _unpack_lo(wa) + g1 * _unpack_lo(wb))
        o_ref[:, hi] = x_ref[:, hi] + (g0 * _unpack_hi(wa) + g1 * _unpack_hi(wb))


def _combine(x2d, gates, out2):
    n, d = x2d.shape
    tm = _row_tile(n, 688)
    return pl.pallas_call(
        _combine_kernel,
        grid=(n // tm,),
        in_specs=[
            pl.BlockSpec((tm, d), lambda i: (i, 0)),
            pl.BlockSpec((tm, LANES), lambda i: (i, 0)),
            pl.BlockSpec((tm * 8, LANES), lambda i: (i, 0)),
            pl.BlockSpec((tm * 8, LANES), lambda i: (n // tm + i, 0)),
        ],
        out_specs=pl.BlockSpec((tm, d), lambda i: (i, 0)),
        out_shape=jax.ShapeDtypeStruct((n, d), F32),
        compiler_params=_params("parallel"),
        name="moe_combine",
    )(x2d, gates, out2, out2)


def _gate_layouts(cols, heads, t):
    b = cols.shape[0]
    rows = cols.transpose(0, 2, 1)
    meta = jnp.pad(rows[:, :, :N_META], ((0, 0), (0, 0), (0, CHUNK - N_META)))
    real = rows[:, :, N_META:].reshape(b, heads, -1, CHUNK)
    return rows[..., None], jnp.concatenate([meta[:, :, None, :], real], axis=2)


def _even_layer(x, norm_g, w_in, conv_w, conv_b, b_i, b_f, a_norm, w_gate2, b_gate, b_norm, w_out):
    b, t, d = x.shape
    n = b * t
    a_w = 2 * A_HEADS * A_DK + 2 * A_HEADS * A_DV
    g_w = 2 * A_HEADS
    b_w = 2 * B_HEADS * B_DK + 2 * B_HEADS * B_DV
    main = a_w + b_w
    gate_cols = g_w + GATE_RANK
    w = _even_weight(w_in, a_w, g_w, b_w, GATE_RANK)
    z = _normproj(x.reshape(n, d), norm_g, w).reshape(b, t, main + MXU_DIM)

    gates = z[:, :, main:main + g_w]
    ai_c, ai_r = _gate_layouts(gates[..., :A_HEADS], A_HEADS, t)
    af_c, af_r = _gate_layouts(gates[..., A_HEADS:], A_HEADS, t)
    ya = _mlstm(z, ai_c, af_c, ai_r, af_r, conv_w, conv_b, b_i, b_f, a_norm)

    wg = jnp.zeros((B_HEADS, MXU_DIM, B_DK), F32).at[:, g_w:g_w + GATE_RANK, :].set(
        w_gate2.reshape(GATE_RANK, B_HEADS, B_DK).transpose(1, 0, 2))
    dummy = jnp.zeros((1, B_HEADS * B_DK), F32)
    hp = 2
    hspec = pl.BlockSpec((1, hp * B_DK), lambda bi, g: (0, g))
    yb = _gla_call(
        z, B_HEADS, hp, B_DK, B_DV,
        (a_w, a_w + B_HEADS * B_DK, a_w + 2 * B_HEADS * B_DK, a_w + 2 * B_HEADS * B_DK + B_HEADS * B_DV, main),
        MXU_DIM,
        ((wg, pl.BlockSpec((hp, MXU_DIM, B_DK), lambda bi, g: (g, 0, 0))),
         (b_gate.reshape(1, -1), hspec), (dummy, hspec)),
        b_norm, "gla")
    return _outproj(ya.reshape(n, -1), yb.reshape(n, -1), w_out.astype(BF16), x.reshape(n, d)).reshape(b, t, d)


def _odd_layer(x, lb, norm_g, w_in, c_norm, q_a_norm, w_q_up, kv_a_norm, w_kv_up, q_norm, k_norm, w_out):
    b, t, d = x.shape
    n = b * t
    c_w = 2 * C_HEADS * C_DK + 2 * C_HEADS * C_DV
    swap = (jnp.arange(D_ROPE) + D_ROPE // 2) % D_ROPE
    kr0 = c_w + Q_LORA + KV_LORA
    used = kr0 + 2 * D_ROPE
    total = -(-used // MXU_DIM) * MXU_DIM
    z2 = _normproj(x.reshape(n, d), norm_g, _odd_weight(w_in, kr0))
    z = z2.reshape(b, t, total)

    hp = 4
    hspec = pl.BlockSpec((1, hp * C_DK), lambda bi, g: (0, g))
    yc = _gla_call(
        z, C_HEADS, hp, C_DK, C_DV,
        (0, C_HEADS * C_DK, 2 * C_HEADS * C_DK, 2 * C_HEADS * C_DK + C_HEADS * C_DV, C_HEADS * C_DK),
        C_DK,
        ((jnp.log(lb).reshape(1, -1), hspec), (jnp.log1p(-lb).reshape(1, -1), hspec), ((1.0 - lb).reshape(1, -1), hspec)),
        c_norm, "hgrn")

    dq = D_NOPE + D_ROPE
    wq = w_q_up.reshape(Q_LORA, D_HEADS, dq)
    wq_rope = wq[:, :, D_NOPE:]
    wq_p = jnp.concatenate([wq[:, :, :D_NOPE].reshape(Q_LORA, -1),
                            jnp.concatenate([wq_rope, wq_rope[:, :, swap]], axis=-1).reshape(Q_LORA, -1)],
                           axis=1).astype(BF16)
    wkv = w_kv_up.reshape(KV_LORA, D_HEADS, D_NOPE + D_V)
    wkv_p = jnp.concatenate([wkv[:, :, :D_NOPE].reshape(KV_LORA, -1), wkv[:, :, D_NOPE:].reshape(KV_LORA, -1)],
                            axis=1).astype(BF16)
    qn = _normproj(z2, q_a_norm, wq_p, x_col_block=c_w // Q_LORA).reshape(b, t, -1)
    kvn = _normproj(z2, kv_a_norm, wkv_p, x_col_block=(c_w + Q_LORA) // KV_LORA).reshape(b, t, -1)

    pos = jnp.arange(t, dtype=F32)
    half = D_ROPE // 2
    inv = ROPE_THETA ** (-jnp.arange(half, dtype=F32) / half)
    ang = pos[:, None] * inv[None, :]
    cos, sin = jnp.cos(ang), jnp.sin(ang)
    tab = jnp.concatenate([cos, cos, -sin, sin], axis=1)
    pair = lambda g: jnp.concatenate([g[D_NOPE:], g[D_NOPE:][swap]]).reshape(1, LANES)
    yd = _mla(qn, kvn, z, kr0, tab, q_norm[:D_NOPE].reshape(1, LANES), pair(q_norm),
              k_norm[:D_NOPE].reshape(1, LANES), pair(k_norm))
    return _outproj(yc.reshape(n, -1), yd.reshape(n, -1), w_out.astype(BF16), x.reshape(n, d)).reshape(b, t, d)


def kernel(x, meta_tokens, ab_norm, ab_w_in, a_conv_w, a_conv_b, a_b_i, a_b_f, a_head_norm, b_w_gate2, b_b_gate, b_head_norm, ab_w_out, cd_norm, cd_w_in, c_lower_bound, c_head_norm, d_q_a_norm, d_w_q_up, d_kv_a_norm, d_w_kv_up, d_q_norm, d_k_norm, cd_w_out, moe_norm, moe_w_group, moe_b_group, moe_w_expert, moe_b_expert, moe_w1, moe_w3, moe_w2):
    b = x.shape[0]
    depth = moe_norm.shape[0]
    h = jnp.concatenate([jnp.broadcast_to(meta_tokens.astype(x.dtype)[None], (b, N_META, D_MODEL)), x], axis=1)
    t = h.shape[1]
    lb_cum = jnp.cumsum(jax.nn.softmax(c_lower_bound.astype(F32), axis=0), axis=0)
    lower_bounds = lb_cum - lb_cum[0]
    for layer in range(depth):
        j = layer // 2
        if layer % 2 == 0:
            h = _even_layer(h, ab_norm[j], ab_w_in[j], a_conv_w[j], a_conv_b[j], a_b_i[j], a_b_f[j], a_head_norm[j],
                            b_w_gate2[j], b_b_gate[j], b_head_norm[j], ab_w_out[j])
        else:
            h = _odd_layer(h, lower_bounds[layer], cd_norm[j], cd_w_in[j], c_head_norm[j], d_q_a_norm[j],
                           d_w_q_up[j], d_kv_a_norm[j], d_w_kv_up[j], d_q_norm[j], d_k_norm[j], cd_w_out[j])
        h = _moe(h.reshape(b * t, D_MODEL), moe_norm[layer], moe_w_group[layer], moe_b_group[layer],
                 moe_w_expert[layer], moe_b_expert[layer], moe_w1[layer], moe_w3[layer], moe_w2[layer]
                 ).reshape(b, t, D_MODEL)
    return h[:, N_META:]
```

```python
import functools
import math

import jax
import jax.numpy as jnp
from jax import lax
from jax.experimental import pallas as pl
from jax.experimental.pallas import tpu as pltpu

F32 = jnp.float32
BF16 = jnp.bfloat16
HIGHEST = lax.Precision.HIGHEST

D_MODEL = 2048
N_META = 16
CHUNK = 64
CONV_K = 4
EPS = 1e-6
A_HEADS, A_DK, A_DV = 4, 128, 256
B_HEADS, B_DK, B_DV = 4, 128, 256
GATE_RANK = 16
GATE_TAU = 16.0
C_HEADS, C_DK, C_DV = 8, 128, 128
D_HEADS, D_NOPE, D_ROPE, D_V = 8, 128, 64, 128
Q_LORA, KV_LORA = 512, 256
ROPE_THETA = 10000.0
N_GROUPS, EXPERTS_PER_GROUP = 4, 8
N_EXPERTS = N_GROUPS * EXPERTS_PER_GROUP
TOP_K = 2
D_EXPERT = 512

LANES = 128
MXU_DIM = 256
BF16_ROWS = 16
VMEM_LIMIT = 56 * 1024 * 1024
MOE_BLOCK = MXU_DIM
ATT_BLOCK = 256
ATT_PAD = ATT_BLOCK - N_META

_NT = (((1,), (1,)), ((), ()))
_TN = (((0,), (0,)), ((), ()))


def _dot(a, b, precision=None):
    return jnp.dot(a, b, preferred_element_type=F32, precision=precision)


def _dot_nt(a, b):
    return lax.dot_general(a, b, _NT, preferred_element_type=F32)


def _dot_tn(a, b):
    return lax.dot_general(a, b, _TN, preferred_element_type=F32)


def _bf(x):
    return x.astype(BF16)


def _log_sigmoid(x):
    return jnp.minimum(x, 0.0) - jnp.log1p(jnp.exp(-jnp.abs(x)))


def _sigmoid(x):
    return 1.0 / (1.0 + jnp.exp(-x))


def _silu(x):
    return x * _sigmoid(x)


def _row_tile(m, cap):
    best = None
    for t in range(BF16_ROWS, min(m, cap) + 1, BF16_ROWS):
        if m % t == 0:
            best = t
    assert best is not None, m
    return best


def _col_tile(n, cap):
    best = None
    for t in range(MXU_DIM, min(n, cap) + 1, MXU_DIM):
        if n % t == 0:
            best = t
    assert best is not None, n
    return best


def _params(*sem):
    return pltpu.CompilerParams(dimension_semantics=sem, vmem_limit_bytes=VMEM_LIMIT)


_RELAYOUT_ROWS = 256


def _even_weight_kernel(wa_ref, wb_ref, wc_ref, o_ref, *, n_plain, n_shift, shift, gate_cols):
    ob = pl.program_id(0)
    rows = o_ref.shape[0]
    chunks = rows // _RELAYOUT_ROWS

    @pl.when(ob < n_plain)
    def _():
        def body(c, carry):
            r = pl.ds(pl.multiple_of(c * _RELAYOUT_ROWS, _RELAYOUT_ROWS), _RELAYOUT_ROWS)
            o_ref[r, :] = _bf(wa_ref[r, :])
            return carry

        lax.fori_loop(0, chunks, body, 0)

    @pl.when((ob >= n_plain) & (ob < n_plain + n_shift))
    def _():
        def body(c, carry):
            r = pl.ds(pl.multiple_of(c * _RELAYOUT_ROWS, _RELAYOUT_ROWS), _RELAYOUT_ROWS)
            wide = jnp.concatenate([wa_ref[r, :], wb_ref[r, :]], axis=1)
            o_ref[r, :] = _bf(wide[:, shift:shift + MXU_DIM])
            return carry

        lax.fori_loop(0, chunks, body, 0)

    @pl.when(ob == n_plain + n_shift)
    def _():
        lane = lax.broadcasted_iota(jnp.int32, (_RELAYOUT_ROWS, LANES), 1)

        def body(c, carry):
            r = pl.ds(pl.multiple_of(c * _RELAYOUT_ROWS, _RELAYOUT_ROWS), _RELAYOUT_ROWS)
            first = jnp.where(lane < shift, wc_ref[r, :], jnp.where(lane < gate_cols, wb_ref[r, :], 0.0))
            o_ref[r, :] = _bf(jnp.concatenate([first, jnp.zeros_like(first)], axis=1))
            return carry

        lax.fori_loop(0, chunks, body, 0)


def _even_weight(w_in, a_w, g_w, b_w, rank):
    d = w_in.shape[0]
    assert a_w % MXU_DIM == 0 and b_w % MXU_DIM == 0 and g_w + rank <= LANES and d % _RELAYOUT_ROWS == 0
    n_plain, n_shift = a_w // MXU_DIM, b_w // MXU_DIM
    n_out = n_plain + n_shift + 1
    last = n_out - 1

    def b_index(ob):
        return (0, jnp.where(ob < n_plain, 0, jnp.where(ob < last, 2 * (ob + 1), (a_w + g_w + b_w) // LANES)))

    return pl.pallas_call(
        functools.partial(_even_weight_kernel, n_plain=n_plain, n_shift=n_shift, shift=g_w, gate_cols=g_w + rank),
        grid=(n_out,),
        in_specs=[
            pl.BlockSpec((d, MXU_DIM), lambda ob: (0, jnp.minimum(ob, last - 1))),
            pl.BlockSpec((d, LANES), b_index),
            pl.BlockSpec((d, LANES), lambda ob: (0, a_w // LANES)),
        ],
        out_specs=pl.BlockSpec((d, MXU_DIM), lambda ob: (0, ob)),
        out_shape=jax.ShapeDtypeStruct((d, n_out * MXU_DIM), BF16),
        compiler_params=_params("parallel"),
        name="even_weight",
    )(w_in, w_in, w_in)


def _odd_weight_kernel(w_ref, o_ref, *, n_plain):
    ob = pl.program_id(0)
    chunks = o_ref.shape[0] // _RELAYOUT_ROWS
    half = D_ROPE // 2

    def body(c, carry):
        r = pl.ds(pl.multiple_of(c * _RELAYOUT_ROWS, _RELAYOUT_ROWS), _RELAYOUT_ROWS)
        w = w_ref[r, :]

        @pl.when(ob < n_plain)
        def _():
            o_ref[r, :] = _bf(w)

        @pl.when(ob == n_plain)
        def _():
            pair = jnp.concatenate([w[:, :D_ROPE], w[:, half:D_ROPE], w[:, :half]], axis=1)
            o_ref[r, :] = _bf(jnp.concatenate([pair, jnp.zeros_like(pair)], axis=1))

        return carry

    lax.fori_loop(0, chunks, body, 0)


def _odd_weight(w_in, kr0):
    d = w_in.shape[0]
    assert kr0 % MXU_DIM == 0 and w_in.shape[1] == kr0 + D_ROPE and d % _RELAYOUT_ROWS == 0
    n_plain = kr0 // MXU_DIM
    return pl.pallas_call(
        functools.partial(_odd_weight_kernel, n_plain=n_plain),
        grid=(n_plain + 1,),
        in_specs=[pl.BlockSpec((d, MXU_DIM), lambda ob: (0, ob))],
        out_specs=pl.BlockSpec((d, MXU_DIM), lambda ob: (0, ob)),
        out_shape=jax.ShapeDtypeStruct((d, kr0 + MXU_DIM), BF16),
        compiler_params=_params("parallel"),
        name="odd_weight",
    )(w_in)


def _normproj_kernel(x_ref, g_ref, w_ref, o_ref, xs_ref):
    tm = xs_ref.shape[0]

    @pl.when(pl.program_id(1) == 0)
    def _():
        def body(c, carry):
            r0 = pl.multiple_of(c * BF16_ROWS, BF16_ROWS)
            x = x_ref[pl.ds(r0, BF16_ROWS), :]
            ms = jnp.mean(x * x, axis=-1, keepdims=True)
            xs_ref[pl.ds(r0, BF16_ROWS), :] = _bf(x * lax.rsqrt(ms + EPS) * g_ref[...])
            return carry

        lax.fori_loop(0, tm // BF16_ROWS, body, 0)

    o_ref[...] = _dot(xs_ref[...], w_ref[...]).astype(o_ref.dtype)


def _normproj(x2d, gain, w, *, x_col_block=0, out_dtype=F32):
    m = x2d.shape[0]
    k, n = w.shape
    tm = _row_tile(m, 688)
    tn = _col_tile(n, 1280)
    return pl.pallas_call(
        _normproj_kernel,
        grid=(m // tm, n // tn),
        in_specs=[
            pl.BlockSpec((tm, k), lambda i, j: (i, x_col_block)),
            pl.BlockSpec((1, k), lambda i, j: (0, 0)),
            pl.BlockSpec((k, tn), lambda i, j: (0, j)),
        ],
        out_specs=pl.BlockSpec((tm, tn), lambda i, j: (i, j)),
        out_shape=jax.ShapeDtypeStruct((m, n), out_dtype),
        scratch_shapes=[pltpu.VMEM((tm, k), BF16)],
        compiler_params=_params("parallel", "arbitrary"),
        name="normproj",
    )(x2d, gain.reshape(1, k).astype(F32), w)


def _outproj_kernel(ya_ref, yb_ref, w_ref, r_ref, o_ref):
    ka = ya_ref.shape[1]
    acc = _dot(ya_ref[...], w_ref[:ka, :]) + _dot(yb_ref[...], w_ref[ka:, :])
    o_ref[...] = r_ref[...] + acc


def _outproj(ya, yb, w, res):
    m, ka = ya.shape
    kb = yb.shape[1]
    n = w.shape[1]
    tm = _row_tile(m, 1376)
    tn = _col_tile(n, 1024)
    return pl.pallas_call(
        _outproj_kernel,
        grid=(m // tm, n // tn),
        in_specs=[
            pl.BlockSpec((tm, ka), lambda i, j: (i, 0)),
            pl.BlockSpec((tm, kb), lambda i, j: (i, 0)),
            pl.BlockSpec((ka + kb, tn), lambda i, j: (0, j)),
            pl.BlockSpec((tm, tn), lambda i, j: (i, j)),
        ],
        out_specs=pl.BlockSpec((tm, tn), lambda i, j: (i, j)),
        out_shape=jax.ShapeDtypeStruct((m, n), F32),
        compiler_params=_params("parallel", "arbitrary"),
        name="outproj",
    )(ya, yb, w, res)


def _mlstm_kernel(bi_ref, bf_ref, q_ref, k_ref, v_ref, og_ref, aic_ref, afc_ref, air_ref, afr_ref,
                  cwq_ref, cwk_ref, cbq_ref, cbk_ref, hn_ref, o_ref, c_ref, n_ref, m_ref, *, heads):
    t_total = q_ref.shape[1]
    n_chunks = (t_total - N_META) // CHUNK
    head0 = pl.program_id(1) * heads

    c_ref[...] = jnp.zeros_like(c_ref)
    n_ref[...] = jnp.zeros_like(n_ref)
    m_ref[...] = jnp.zeros_like(m_ref)

    def conv(win, cw, cb, length):
        y = cb
        for j in range(CONV_K):
            y = y + win[8 - (CONV_K - 1) + j:8 - (CONV_K - 1) + j + length, :] * cw[j:j + 1, :]
        return _silu(y)

    hs = range(heads)
    kcol = [slice(hh * A_DK, (hh + 1) * A_DK) for hh in hs]
    vcol = [slice(hh * A_DV, (hh + 1) * A_DV) for hh in hs]

    def gates(hh, o, ci, length, causal, upper):
        b_i = bi_ref[head0 + hh]
        b_f = bf_ref[head0 + hh]
        ig_c = aic_ref[0, hh, pl.ds(o, length), :] + b_i
        lf_c = _log_sigmoid(afc_ref[0, hh, pl.ds(o, length), :] + b_f)
        ig_r = air_ref[0, hh, pl.ds(ci, 1), :][:, :length] + b_i
        lf_r = _log_sigmoid(afr_ref[0, hh, pl.ds(ci, 1), :][:, :length] + b_f)
        b_c = jnp.sum(jnp.where(causal, lf_r, 0.0), axis=1, keepdims=True)
        b_r = jnp.sum(jnp.where(upper, lf_c, 0.0), axis=0, keepdims=True)
        b_end = b_c[length - 1:length, :]
        w_end = b_end - b_c + ig_c
        m_loc = jnp.max(w_end, axis=0, keepdims=True)
        d = jnp.where(causal, b_c - b_r + ig_r, -jnp.inf)
        return b_c, b_end, jnp.exp(w_end - m_loc), m_loc, d, jnp.max(d, axis=1, keepdims=True)

    def chunk(o, ci, length, qwin, kwin):
        row = lax.broadcasted_iota(jnp.int32, (length, length), 0)
        col = lax.broadcasted_iota(jnp.int32, (length, length), 1)
        causal = col <= row
        gt = [gates(hh, o, ci, length, causal, row <= col) for hh in hs]
        q = [conv(qwin[hh], cwq_ref[:, kcol[hh]], cbq_ref[:, kcol[hh]], length) for hh in hs]
        k = [conv(kwin[hh], cwk_ref[:, kcol[hh]], cbk_ref[:, kcol[hh]], length) * (A_DK ** -0.5) for hh in hs]
        vb = [_bf(v_ref[0, pl.ds(o, length), vcol[hh]]) for hh in hs]
        qb = [_bf(q[hh]) for hh in hs]
        k_w = [k[hh] * gt[hh][2] for hh in hs]
        qk = [_dot_nt(qb[hh], _bf(k[hh])) for hh in hs]
        c_in = [c_ref[hh] for hh in hs]
        q_c = [_dot(qb[hh], _bf(c_in[hh])) for hh in hs]
        c_loc = [_dot_tn(_bf(k_w[hh]), vb[hh]) for hh in hs]
        s, a_t, m_t, q_n = [], [], [], []
        for hh in hs:
            b_c, b_end, _, m_loc, d, d_max = gt[hh]
            m_in = m_ref[hh]
            inter = b_c + m_in
            m_t.append(jnp.maximum(inter, d_max))
            s.append(qk[hh] * jnp.exp(d - m_t[hh]))
            a_t.append(jnp.exp(inter - m_t[hh]))
            m_new = jnp.maximum(b_end + m_in, m_loc)
            a = jnp.exp(b_end + m_in - m_new)
            c = jnp.exp(m_loc - m_new)
            n_in = n_ref[hh]
            c_ref[hh] = a * c_in[hh] + c * c_loc[hh]
            n_ref[hh] = a * n_in + c * jnp.sum(k_w[hh], axis=0, keepdims=True)
            m_ref[hh] = m_new
            q_n.append(jnp.sum(q[hh] * n_in, axis=1, keepdims=True))
        num = [_dot(_bf(s[hh]), vb[hh]) + a_t[hh] * q_c[hh] for hh in hs]
        for hh in hs:
            den = jnp.sum(s[hh], axis=1, keepdims=True) + a_t[hh] * q_n[hh]
            h = num[hh] / jnp.maximum(jnp.abs(den), jnp.exp(-m_t[hh]))
            hn = h * lax.rsqrt(jnp.mean(h * h, axis=-1, keepdims=True) + EPS) * hn_ref[:, vcol[hh]]
            y = _sigmoid(og_ref[0, pl.ds(o, length), vcol[hh]]) * hn
            o_ref[0, pl.ds(o, length), vcol[hh]] = y.astype(o_ref.dtype)

    zeros8 = jnp.zeros((8, A_DK), F32)
    chunk(0, 0, N_META,
          [jnp.concatenate([zeros8, q_ref[0, 0:N_META, kcol[hh]]], axis=0) for hh in hs],
          [jnp.concatenate([zeros8, k_ref[0, 0:N_META, kcol[hh]]], axis=0) for hh in hs])

    def body(c, carry):
        o = pl.multiple_of(N_META + c * CHUNK, BF16_ROWS)
        w0 = pl.multiple_of(N_META - 8 + c * CHUNK, 8)
        chunk(o, c + 1, CHUNK, [q_ref[0, pl.ds(w0, CHUNK + 8), kcol[hh]] for hh in hs],
              [k_ref[0, pl.ds(w0, CHUNK + 8), kcol[hh]] for hh in hs])
        return carry

    lax.fori_loop(0, n_chunks, body, 0)


def _mlstm(z, ai_c, af_c, ai_r, af_r, conv_w, conv_b, b_i, b_f, head_norm, heads=2):
    b, t, _ = z.shape
    hk = A_HEADS * A_DK
    wk, wv = heads * A_DK, heads * A_DV
    smem = pl.BlockSpec(memory_space=pltpu.SMEM)
    col = lambda width, off: (lambda bi, g: (bi, 0, off // width + g))
    gate_c = pl.BlockSpec((1, heads, t, 1), lambda bi, g: (bi, g, 0, 0))
    gate_r = pl.BlockSpec((1, heads) + ai_r.shape[2:], lambda bi, g: (bi, g, 0, 0))
    return pl.pallas_call(
        functools.partial(_mlstm_kernel, heads=heads),
        grid=(b, A_HEADS // heads),
        in_specs=[
            smem, smem,
            pl.BlockSpec((1, t, wk), col(wk, 0)),
            pl.BlockSpec((1, t, wk), col(wk, hk)),
            pl.BlockSpec((1, t, wv), col(wv, 2 * hk)),
            pl.BlockSpec((1, t, wv), col(wv, 2 * hk + A_HEADS * A_DV)),
            gate_c, gate_c, gate_r, gate_r,
            pl.BlockSpec((CONV_K, wk), lambda bi, g: (0, g)),
            pl.BlockSpec((CONV_K, wk), lambda bi, g: (0, A_HEADS // heads + g)),
            pl.BlockSpec((1, wk), lambda bi, g: (0, g)),
            pl.BlockSpec((1, wk), lambda bi, g: (0, A_HEADS // heads + g)),
            pl.BlockSpec((1, wv), lambda bi, g: (0, g)),
        ],
        out_specs=pl.BlockSpec((1, t, wv), lambda bi, g: (bi, 0, g)),
        out_shape=jax.ShapeDtypeStruct((b, t, A_HEADS * A_DV), BF16),
        scratch_shapes=[pltpu.VMEM((heads, A_DK, A_DV), F32), pltpu.VMEM((heads, 1, A_DK), F32),
                        pltpu.VMEM((heads, 1, 1), F32)],
        compiler_params=_params("parallel", "parallel"),
        name="mlstm",
    )(b_i, b_f, z, z, z, z, ai_c, af_c, ai_r, af_r, conv_w, conv_w, conv_b.reshape(1, -1), conv_b.reshape(1, -1),
      head_norm.reshape(1, -1))


def _gla_kernel(q_ref, k_ref, v_ref, og_ref, g_ref, p0_ref, p1_ref, p2_ref, hn_ref, o_ref, st_ref, *, mode, heads):
    t_total = q_ref.shape[1]
    n_chunks = (t_total - N_META) // CHUNK
    dv, dk = st_ref.shape[1:]
    st_ref[...] = jnp.zeros_like(st_ref)

    hs = range(heads)
    kcol = [slice(hh * dk, (hh + 1) * dk) for hh in hs]
    vcol = [slice(hh * dv, (hh + 1) * dv) for hh in hs]

    def gate_inputs(hh, o, length):
        q = q_ref[0, pl.ds(o, length), kcol[hh]]
        if mode == "gla":
            pre = _dot(g_ref[0, pl.ds(o, length), :], p0_ref[hh], precision=HIGHEST) + p1_ref[:, kcol[hh]]
            return q * (dk ** -0.5), k_ref[0, pl.ds(o, length), kcol[hh]], _log_sigmoid(pre) / GATE_TAU
        fpre = g_ref[0, pl.ds(o, length), kcol[hh]]
        a = p0_ref[:, kcol[hh]]
        bb = p1_ref[:, kcol[hh]] + _log_sigmoid(fpre)
        lg = jnp.maximum(a, bb) + jnp.log1p(jnp.exp(-jnp.abs(a - bb)))
        return q, p2_ref[:, kcol[hh]] * _sigmoid(-fpre), lg

    def cumsum_time(tri, lg):
        hi = _bf(lg)
        r1 = lg - hi.astype(F32)
        mid = _bf(r1)
        lo = _bf(r1 - mid.astype(F32))
        parts = _dot(tri, jnp.concatenate([hi, mid, lo], axis=1))
        return parts[:, :dk] + (parts[:, dk:2 * dk] + parts[:, 2 * dk:])

    def chunk(o, length):
        row = lax.broadcasted_iota(jnp.int32, (length, length), 0)
        col = lax.broadcasted_iota(jnp.int32, (length, length), 1)
        causal = col <= row
        tri = jnp.where(causal, 1.0, 0.0).astype(BF16)
        qkl = [gate_inputs(hh, o, length) for hh in hs]
        vb = [_bf(v_ref[0, pl.ds(o, length), vcol[hh]]) for hh in hs]
        g = [cumsum_time(tri, qkl[hh][2]) for hh in hs]
        g_end = [g[hh][length - 1:length, :] for hh in hs]
        g_mid = [g[hh][length // 2:length // 2 + 1, :] for hh in hs]
        s = [_dot_nt(_bf(qkl[hh][0] * jnp.exp(g[hh] - g_mid[hh])), _bf(qkl[hh][1] * jnp.exp(g_mid[hh] - g[hh])))
             for hh in hs]
        st_in = [st_ref[hh] for hh in hs]
        inter = [_dot_nt(_bf(qkl[hh][0] * jnp.exp(g[hh])), _bf(st_in[hh])) for hh in hs]
        local = [_dot_tn(vb[hh], _bf(qkl[hh][1] * jnp.exp(g_end[hh] - g[hh]))) for hh in hs]
        for hh in hs:
            st_ref[hh] = st_in[hh] * jnp.exp(g_end[hh]) + local[hh]
        out = [_dot(_bf(jnp.where(causal, s[hh], 0.0)), vb[hh]) + inter[hh] for hh in hs]
        for hh in hs:
            hn = out[hh] * lax.rsqrt(jnp.mean(out[hh] * out[hh], axis=-1, keepdims=True) + EPS) * hn_ref[:, vcol[hh]]
            og = og_ref[0, pl.ds(o, length), vcol[hh]]
            gate = _silu(og) if mode == "gla" else _sigmoid(og)
            o_ref[0, pl.ds(o, length), vcol[hh]] = (gate * hn).astype(o_ref.dtype)

    chunk(0, N_META)

    def body(c, carry):
        chunk(pl.multiple_of(N_META + c * CHUNK, BF16_ROWS), CHUNK)
        return carry

    lax.fori_loop(0, n_chunks, body, 0)


def _gla_call(z, n_heads, heads, dk, dv, blocks, gate_width, params, head_norm, mode):
    b, t, _ = z.shape
    q0, k0, v0, og0, g0 = blocks
    zspec = lambda width, off, grouped=True: pl.BlockSpec(
        (1, t, width), (lambda bi, g: (bi, 0, off // width + (g if grouped else 0))))
    (p0, s0), (p1, s1), (p2, s2) = params
    gate_spec = zspec(gate_width, g0, grouped=False) if mode == "gla" else zspec(heads * dk, g0)
    return pl.pallas_call(
        functools.partial(_gla_kernel, mode=mode, heads=heads),
        grid=(b, n_heads // heads),
        in_specs=[
            zspec(heads * dk, q0), zspec(heads * dk, k0), zspec(heads * dv, v0), zspec(heads * dv, og0),
            gate_spec, s0, s1, s2,
            pl.BlockSpec((1, heads * dv), lambda bi, g: (0, g)),
        ],
        out_specs=pl.BlockSpec((1, t, heads * dv), lambda bi, g: (bi, 0, g)),
        out_shape=jax.ShapeDtypeStruct((b, t, n_heads * dv), BF16),
        scratch_shapes=[pltpu.VMEM((heads, dv, dk), F32)],
        compiler_params=_params("parallel", "parallel"),
        name="gla_" + mode,
    )(z, z, z, z, z, p0, p1, p2, head_norm.reshape(1, -1))


def _mla_kernel(qn_ref, qr_ref, kn_ref, v_ref, kr_ref, tab_ref, gqn_ref, gqr_ref, gkn_ref, gkr_ref,
                o_ref, qf_ref, kf_ref, vf_ref):
    t_total = qn_ref.shape[1]
    n_blocks = (ATT_PAD + t_total) // ATT_BLOCK
    dqk = D_NOPE + D_ROPE
    scale = dqk ** -0.5
    rows = _row_tile(t_total, 768)
    lane = lax.broadcasted_iota(jnp.int32, (rows, LANES), 1)
    first_half = lane < D_ROPE

    qf_ref[0:ATT_PAD, :] = jnp.zeros((ATT_PAD, 2 * LANES), BF16)
    kf_ref[0:ATT_PAD, :] = jnp.zeros((ATT_PAD, 2 * LANES), BF16)
    vf_ref[0:ATT_PAD, :] = jnp.zeros((ATT_PAD, D_V), BF16)

    def rope_pair(x, gains, tab):
        p = x * gains * tab
        return p + pltpu.roll(p, D_ROPE, 1)

    def prep(c):
        r0 = c * rows
        dst = ATT_PAD + c * rows
        tab = tab_ref[pl.ds(r0, rows), :]
        qn = qn_ref[0, pl.ds(r0, rows), :]
        qr = qr_ref[0, pl.ds(r0, rows), :]
        ssq = jnp.sum(qn * qn, axis=-1, keepdims=True) + 0.5 * jnp.sum(qr * qr, axis=-1, keepdims=True)
        rq = lax.rsqrt(ssq / dqk + EPS) * scale
        qf_ref[pl.ds(dst, rows), 0:LANES] = _bf(qn * gqn_ref[...] * rq)
        qf_ref[pl.ds(dst, rows), LANES:2 * LANES] = _bf(rope_pair(qr, gqr_ref[...], tab) * rq)
        kn = kn_ref[0, pl.ds(r0, rows), :]
        kr = kr_ref[0, pl.ds(r0, rows), :]
        ssk = jnp.sum(kn * kn, axis=-1, keepdims=True) + 0.5 * jnp.sum(kr * kr, axis=-1, keepdims=True)
        rk = lax.rsqrt(ssk / dqk + EPS)
        kf_ref[pl.ds(dst, rows), 0:LANES] = _bf(kn * gkn_ref[...] * rk)
        kf_ref[pl.ds(dst, rows), LANES:2 * LANES] = _bf(
            jnp.where(first_half, rope_pair(kr, gkr_ref[...], tab) * rk, 0.0))
        vf_ref[pl.ds(dst, rows), :] = _bf(v_ref[0, pl.ds(r0, rows), :])

    for c in range(t_total // rows):
        prep(c)

    qpos = lax.broadcasted_iota(jnp.int32, (ATT_BLOCK, ATT_BLOCK), 0)
    kpos = lax.broadcasted_iota(jnp.int32, (ATT_BLOCK, ATT_BLOCK), 1)
    neg = -jnp.inf

    for qi in range(n_blocks):
        q = qf_ref[qi * ATT_BLOCK:(qi + 1) * ATT_BLOCK, :]
        s = _dot_nt(q, kf_ref[0:(qi + 1) * ATT_BLOCK, :])
        parts = [s[:, j * ATT_BLOCK:(j + 1) * ATT_BLOCK] for j in range(qi + 1)]
        parts[0] = jnp.where(kpos >= ATT_PAD, parts[0], neg)
        parts[qi] = jnp.where(kpos <= qpos, parts[qi], neg)
        top = functools.reduce(jnp.maximum, parts)
        m = jnp.max(top, axis=-1, keepdims=True)
        if qi == 0:
            m = jnp.where(m == neg, 0.0, m)
        probs = [jnp.exp(part - m) for part in parts]
        l = jnp.sum(functools.reduce(jnp.add, probs), axis=-1, keepdims=True)
        pv = _dot(jnp.concatenate([_bf(pr) for pr in probs], axis=1), vf_ref[0:(qi + 1) * ATT_BLOCK, :])
        if qi == 0:
            out = pv / jnp.where(l == 0.0, 1.0, l)
            o_ref[0, 0:N_META, :] = out[ATT_PAD:, :].astype(o_ref.dtype)
        else:
            dst = qi * ATT_BLOCK - ATT_PAD
            o_ref[0, dst:dst + ATT_BLOCK, :] = (pv / l).astype(o_ref.dtype)


def _mla(qn, kvn, z, kr_col, tab, gqn, gqr, gkn, gkr):
    b, t, _ = qn.shape
    assert (ATT_PAD + t) % ATT_BLOCK == 0 and t % (3 * BF16_ROWS) == 0
    tp = ATT_PAD + t
    hspec = lambda off: pl.BlockSpec((1, t, LANES), lambda bi, h: (bi, 0, off + h))
    gspec = pl.BlockSpec((1, LANES), lambda bi, h: (0, 0))
    return pl.pallas_call(
        _mla_kernel,
        grid=(b, D_HEADS),
        in_specs=[
            hspec(0), hspec(D_HEADS), hspec(0), hspec(D_HEADS),
            pl.BlockSpec((1, t, LANES), lambda bi, h: (bi, 0, kr_col // LANES)),
            pl.BlockSpec((t, LANES), lambda bi, h: (0, 0)),
            gspec, gspec, gspec, gspec,
        ],
        out_specs=pl.BlockSpec((1, t, D_V), lambda bi, h: (bi, 0, h)),
        out_shape=jax.ShapeDtypeStruct((b, t, D_HEADS * D_V), BF16),
        scratch_shapes=[pltpu.VMEM((tp, 2 * LANES), BF16), pltpu.VMEM((tp, 2 * LANES), BF16),
                        pltpu.VMEM((tp, D_V), BF16)],
        compiler_params=_params("parallel", "parallel"),
        name="mla",
    )(qn, qn, kvn, kvn, z, tab, gqn, gqr, gkn, gkr)


def _pack_bf16_pairs(v):
    w = v.shape[1] // 2
    bits = pltpu.bitcast(_bf(v).astype(F32), jnp.uint32)
    return (bits[:, :w] >> 16) | (bits[:, w:] & jnp.uint32(0xFFFF0000))


def _unpack_lo(words):
    return pltpu.bitcast(words << 16, F32)


def _unpack_hi(words):
    return pltpu.bitcast(words & jnp.uint32(0xFFFF0000), F32)


def _router_kernel(x_ref, g_ref, w_ref, b_ref, gate_ref, idx_ref, xg_ref, cnt_ref, carry_ref):
    tm = x_ref.shape[0]

    @pl.when(pl.program_id(0) == 0)
    def _():
        carry_ref[...] = jnp.zeros_like(carry_ref)

    x = x_ref[...]
    ms = jnp.mean(x * x, axis=-1, keepdims=True)
    xn = x * lax.rsqrt(ms + EPS) * g_ref[...]
    logits = _dot(xn, w_ref[...], precision=HIGHEST) + b_ref[...]
    lane = lax.broadcasted_iota(jnp.int32, logits.shape, 1)
    lane_f = lane.astype(F32)
    neg = -jnp.inf
    big = float(LANES)

    is_group = lane < N_GROUPS
    g_max = jnp.max(jnp.where(is_group, logits, neg), axis=-1, keepdims=True)
    g_sum = jnp.sum(jnp.where(is_group, jnp.exp(logits - g_max), 0.0), axis=-1, keepdims=True)
    p_top = 1.0 / g_sum
    grp = jnp.min(jnp.where(is_group & (logits == g_max), lane_f, big), axis=-1, keepdims=True)

    e_lo = N_GROUPS + grp * EXPERTS_PER_GROUP
    in_grp = (lane_f >= e_lo) & (lane_f < e_lo + EXPERTS_PER_GROUP)
    e_max = jnp.max(jnp.where(in_grp, logits, neg), axis=-1, keepdims=True)
    e_sum = jnp.sum(jnp.where(in_grp, jnp.exp(logits - e_max), 0.0), axis=-1, keepdims=True)
    i1 = jnp.min(jnp.where(in_grp & (logits == e_max), lane_f, big), axis=-1, keepdims=True)
    rest = in_grp & (lane_f != i1)
    e_2nd = jnp.max(jnp.where(rest, logits, neg), axis=-1, keepdims=True)
    i2 = jnp.min(jnp.where(rest & (logits == e_2nd), lane_f, big), axis=-1, keepdims=True)
    p1 = 1.0 / e_sum
    p2 = jnp.exp(e_2nd - e_max) / e_sum
    tot = p1 + p2
    gate_ref[...] = jnp.where(lane == 0, p_top * p1 / tot, jnp.where(lane == 1, p_top * p2 / tot, 0.0))

    e1 = i1 - N_GROUPS
    e2 = i2 - N_GROUPS
    hot = jnp.where((lane_f == e1) | (lane_f == e2), 1.0, 0.0)
    row = lax.broadcasted_iota(jnp.int32, (tm, tm), 0)
    col = lax.broadcasted_iota(jnp.int32, (tm, tm), 1)
    before = _dot(jnp.where(col < row, 1.0, 0.0).astype(BF16), _bf(hot)) + carry_ref[...]
    r1 = jnp.sum(jnp.where(lane_f == e1, before, 0.0), axis=-1, keepdims=True)
    r2 = jnp.sum(jnp.where(lane_f == e2, before, 0.0), axis=-1, keepdims=True)
    total = carry_ref[...] + jnp.sum(hot, axis=0, keepdims=True)
    carry_ref[...] = total
    cnt_ref[...] = jnp.broadcast_to(total, cnt_ref.shape).astype(jnp.int32)
    idx_ref[...] = jnp.where(lane == 0, e1, jnp.where(lane == 1, e2, jnp.where(lane == 2, r1, jnp.where(
        lane == 3, r2, 0.0)))).astype(jnp.int32)

    words = _pack_bf16_pairs(xn)
    for s in range(8):
        xg_ref[pl.ds(s, tm, stride=8), :] = words[:, s * LANES:(s + 1) * LANES]


def _router(x2d, gain, w_group, b_group, w_expert, b_expert):
    m, d = x2d.shape
    assert d == 2 * 8 * LANES
    tm = _row_tile(m, 688)
    pad = LANES - N_GROUPS - N_EXPERTS
    w = jnp.concatenate([w_group, w_expert, jnp.zeros((d, pad), F32)], axis=1)
    bias = jnp.concatenate([b_group, b_expert, jnp.zeros((pad,), F32)]).reshape(1, LANES)
    return pl.pallas_call(
        _router_kernel,
        grid=(m // tm,),
        in_specs=[
            pl.BlockSpec((tm, d), lambda i: (i, 0)),
            pl.BlockSpec((1, d), lambda i: (0, 0)),
            pl.BlockSpec((d, LANES), lambda i: (0, 0)),
            pl.BlockSpec((1, LANES), lambda i: (0, 0)),
        ],
        out_specs=[pl.BlockSpec((tm, LANES), lambda i: (i, 0)), pl.BlockSpec((tm, LANES), lambda i: (i, 0)),
                   pl.BlockSpec((tm * 8, LANES), lambda i: (i, 0)), pl.BlockSpec((8, LANES), lambda i: (0, 0))],
        out_shape=[jax.ShapeDtypeStruct((m, LANES), F32), jax.ShapeDtypeStruct((m, LANES), jnp.int32),
                   jax.ShapeDtypeStruct((m * 8, LANES), jnp.uint32), jax.ShapeDtypeStruct((8, LANES), jnp.int32)],
        scratch_shapes=[pltpu.VMEM((1, LANES), F32)],
        compiler_params=_params("arbitrary"),
        name="router",
    )(x2d, gain.reshape(1, d), w, bias)


def _invert_kernel(dest_ref, inv_ref):
    def clear(s, carry):
        inv_ref[s] = -1
        return carry

    lax.fori_loop(0, inv_ref.shape[0], clear, 0, unroll=8)

    def put(f, carry):
        inv_ref[dest_ref[f]] = f
        return carry

    lax.fori_loop(0, dest_ref.shape[0], put, 0, unroll=8)


def _invert(dest, p):
    assert p % 8 == 0 and dest.shape[0] % 8 == 0
    smem = pl.BlockSpec(memory_space=pltpu.SMEM)
    return pl.pallas_call(
        _invert_kernel, in_specs=[smem], out_specs=smem,
        out_shape=jax.ShapeDtypeStruct((p,), jnp.int32), name="moe_invert",
    )(dest)


def _expert_kernel(be_ref, nu_ref, src_ref, dst_ref, xg_hbm, w1_ref, w3_ref, w2_ref, o_hbm,
                   xbuf, ybuf, xs_ref, w1s, w3s, w2s, sem_in, sem_out):
    i = pl.program_id(0)
    n_used = nu_ref[0]
    par = i % 2
    half = D_MODEL // 2

    def issue_gathers(block, slot):
        def body(r, carry):
            src = pl.multiple_of(src_ref[block * MOE_BLOCK + r], 8)
            pltpu.make_async_copy(xg_hbm.at[pl.ds(src, 8), :], xbuf.at[slot, pl.ds(r * 8, 8), :],
                                  sem_in.at[slot]).start(priority=1)
            return carry

        lax.fori_loop(0, MOE_BLOCK, body, 0, unroll=8)

    def wait_gathers(slot):
        pltpu.make_async_copy(xbuf.at[1 - slot], xbuf.at[slot], sem_in.at[slot]).wait()

    def wait_scatters(slot):
        pltpu.make_async_copy(ybuf.at[slot], ybuf.at[1 - slot], sem_out.at[slot]).wait()

    @pl.when(i == 0)
    def _():
        ybuf[1] = jnp.zeros(ybuf.shape[1:], ybuf.dtype)
        n_real = o_hbm.shape[0] - 2 * MOE_BLOCK * 8
        for q in range(2):
            fill = pltpu.make_async_copy(ybuf.at[1], o_hbm.at[pl.ds(n_real + q * MOE_BLOCK * 8, MOE_BLOCK * 8), :],
                                         sem_out.at[1])
            fill.start()
            fill.wait()

    @pl.when((i == 0) & (n_used > 0))
    def _():
        issue_gathers(0, 0)

    @pl.when(i + 1 < n_used)
    def _():
        issue_gathers(i + 1, 1 - par)

    @pl.when(i < n_used)
    def _():
        @pl.when((i == 0) | (be_ref[i] != be_ref[jnp.maximum(i - 1, 0)]))
        def _():
            w1s[...] = _bf(w1_ref[0, 0])
            w3s[...] = _bf(w3_ref[0, 0])
            w2s[...] = _bf(w2_ref[0, 0])

        wait_gathers(par)

        for s in range(8):
            words = xbuf[par, pl.ds(s, MOE_BLOCK, stride=8), :]
            xs_ref[:, s * LANES:(s + 1) * LANES] = _bf(_unpack_lo(words))
            xs_ref[:, half + s * LANES:half + (s + 1) * LANES] = _bf(_unpack_hi(words))
        xb = xs_ref[...]
        h1 = _dot(xb, w1s[...])
        h3 = _dot(xb, w3s[...])
        y = _dot(_bf(_silu(h1) * h3), w2s[...])

        @pl.when(i >= 2)
        def _():
            wait_scatters(par)

        words = _pack_bf16_pairs(y)
        for s in range(8):
            ybuf[par, pl.ds(s, MOE_BLOCK, stride=8), :] = words[:, s * LANES:(s + 1) * LANES]

        def issue_scatter(r, carry):
            dst = pl.multiple_of(dst_ref[i * MOE_BLOCK + r], 8)
            pltpu.make_async_copy(ybuf.at[par, pl.ds(r * 8, 8), :], o_hbm.at[pl.ds(dst, 8), :],
                                  sem_out.at[par]).start()
            return carry

        lax.fori_loop(0, MOE_BLOCK, issue_scatter, 0, unroll=8)

        @pl.when(i == n_used - 1)
        def _():
            wait_scatters(par)

            @pl.when(i >= 1)
            def _():
                wait_scatters(1 - par)


def _moe(x2d, gain, w_group, b_group, w_expert, b_expert, w1, w3, w2, layer):
    n, d = x2d.shape
    gates_l, idx_l, xg, cnt = _router(x2d, gain, w_group, b_group, w_expert, b_expert)

    a = n * TOP_K
    n_blocks = -(-a // MOE_BLOCK) + N_EXPERTS
    p = n_blocks * MOE_BLOCK
    counts = cnt[0, :N_EXPERTS]
    padded = (counts + MOE_BLOCK - 1) // MOE_BLOCK * MOE_BLOCK
    pad_end = jnp.cumsum(padded)
    pad_start = pad_end - padded
    dest = (pad_start[idx_l[:, :TOP_K]] + idx_l[:, TOP_K:2 * TOP_K]).reshape(-1).astype(jnp.int32)
    blk0 = jnp.arange(n_blocks, dtype=jnp.int32) * MOE_BLOCK
    block_expert = jnp.minimum(jnp.searchsorted(pad_end, blk0, side="right"), N_EXPERTS - 1).astype(jnp.int32)
    n_used = (pad_end[-1] // MOE_BLOCK).astype(jnp.int32).reshape(1)
    codes = _invert(dest, p)
    slot = jnp.arange(p, dtype=jnp.int32)
    spare = TOP_K * n + (slot // MOE_BLOCK % 2) * MOE_BLOCK + slot % MOE_BLOCK
    src_tok = (jnp.maximum(codes, 0) >> 1) * 8
    dst_row = jnp.where(codes >= 0, (codes & 1) * n + (codes >> 1), spare) * 8

    wspec = lambda shape: pl.BlockSpec((1, 1) + shape, lambda i, be, nu, src, dst: (layer, be[i], 0, 0))
    out_rows = TOP_K * n + 2 * MOE_BLOCK
    out2 = pl.pallas_call(
        _expert_kernel,
        grid_spec=pltpu.PrefetchScalarGridSpec(
            num_scalar_prefetch=4,
            grid=(n_blocks,),
            in_specs=[
                pl.BlockSpec(memory_space=pl.ANY),
                wspec((d, D_EXPERT)), wspec((d, D_EXPERT)), wspec((D_EXPERT, d)),
            ],
            out_specs=pl.BlockSpec(memory_space=pl.ANY),
            scratch_shapes=[
                pltpu.VMEM((2, MOE_BLOCK * 8, LANES), jnp.uint32), pltpu.VMEM((2, MOE_BLOCK * 8, LANES), jnp.uint32),
                pltpu.VMEM((MOE_BLOCK, d), BF16),
                pltpu.VMEM((d, D_EXPERT), BF16), pltpu.VMEM((d, D_EXPERT), BF16), pltpu.VMEM((D_EXPERT, d), BF16),
                pltpu.SemaphoreType.DMA((2,)), pltpu.SemaphoreType.DMA((2,)),
            ],
        ),
        out_shape=jax.ShapeDtypeStruct((out_rows * 8, LANES), jnp.uint32),
        compiler_params=_params("arbitrary"),
        name="moe_experts",
    )(block_expert, n_used, src_tok, dst_row, xg, w1, w3, w2)
    return _combine(x2d, gates_l, out2)


def _combine_kernel(x_ref, gate_ref, a_ref, b_ref, o_ref):
    tm = x_ref.shape[0]
    half = x_ref.shape[1] // 2
    g0 = gate_ref[:, 0:1]
    g1 = gate_ref[:, 1:2]
    for s in range(8):
        wa = a_ref[pl.ds(s, tm, stride=8), :]
        wb = b_ref[pl.ds(s, tm, stride=8), :]
        lo = slice(s * LANES, (s + 1) * LANES)
        hi = slice(half + s * LANES, half + (s + 1) * LANES)
        o_ref[:, lo] = x_ref[:, lo] + (g0 * _unpack_lo(wa) + g1 * _unpack_lo(wb))
        o_ref[:, hi] = x_ref[:, hi] + (g0 * _unpack_hi(wa) + g1 * _unpack_hi(wb))


def _combine(x2d, gates, out2):
    n, d = x2d.shape
    tm = _row_tile(n, 688)
    return pl.pallas_call(
        _combine_kernel,
        grid=(n // tm,),
        in_specs=[
            pl.BlockSpec((tm, d), lambda i: (i, 0)),
            pl.BlockSpec((tm, LANES), lambda i: (i, 0)),
            pl.BlockSpec((tm * 8, LANES), lambda i: (i, 0)),
            pl.BlockSpec((tm * 8, LANES), lambda i: (n // tm + i, 0)),
        ],
        out_specs=pl.BlockSpec((tm, d), lambda i: (i, 0)),
        out_shape=jax.ShapeDtypeStruct((n, d), F32),
        compiler_params=_params("parallel"),
        name="moe_combine",
    )(x2d, gates, out2, out2)


def _gate_layouts(cols, heads, t):
    b = cols.shape[0]
    rows = cols.transpose(0, 2, 1)
    meta = jnp.pad(rows[:, :, :N_META], ((0, 0), (0, 0), (0, CHUNK - N_META)))
    real = rows[:, :, N_META:].reshape(b, heads, -1, CHUNK)
    return rows[..., None], jnp.concatenate([meta[:, :, None, :], real], axis=2)


def _even_layer(x, norm_g, w_in, conv_w, conv_b, b_i, b_f, a_norm, w_gate2, b_gate, b_norm, w_out):
    b, t, d = x.shape
    n = b * t
    a_w = 2 * A_HEADS * A_DK + 2 * A_HEADS * A_DV
    g_w = 2 * A_HEADS
    b_w = 2 * B_HEADS * B_DK + 2 * B_HEADS * B_DV
    main = a_w + b_w
    gate_cols = g_w + GATE_RANK
    w = _even_weight(w_in, a_w, g_w, b_w, GATE_RANK)
    z = _normproj(x.reshape(n, d), norm_g, w).reshape(b, t, main + MXU_DIM)

    gates = z[:, :, main:main + g_w]
    ai_c, ai_r = _gate_layouts(gates[..., :A_HEADS], A_HEADS, t)
    af_c, af_r = _gate_layouts(gates[..., A_HEADS:], A_HEADS, t)
    ya = _mlstm(z, ai_c, af_c, ai_r, af_r, conv_w, conv_b, b_i, b_f, a_norm)

    wg = jnp.zeros((B_HEADS, MXU_DIM, B_DK), F32).at[:, g_w:g_w + GATE_RANK, :].set(
        w_gate2.reshape(GATE_RANK, B_HEADS, B_DK).transpose(1, 0, 2))
    dummy = jnp.zeros((1, B_HEADS * B_DK), F32)
    hp = 2
    hspec = pl.BlockSpec((1, hp * B_DK), lambda bi, g: (0, g))
    yb = _gla_call(
        z, B_HEADS, hp, B_DK, B_DV,
        (a_w, a_w + B_HEADS * B_DK, a_w + 2 * B_HEADS * B_DK, a_w + 2 * B_HEADS * B_DK + B_HEADS * B_DV, main),
        MXU_DIM,
        ((wg, pl.BlockSpec((hp, MXU_DIM, B_DK), lambda bi, g: (g, 0, 0))),
         (b_gate.reshape(1, -1), hspec), (dummy, hspec)),
        b_norm, "gla")
    return _outproj(ya.reshape(n, -1), yb.reshape(n, -1), w_out.astype(BF16), x.reshape(n, d)).reshape(b, t, d)


def _odd_layer(x, lb, norm_g, w_in, c_norm, q_a_norm, w_q_up, kv_a_norm, w_kv_up, q_norm, k_norm, w_out):
    b, t, d = x.shape
    n = b * t
    c_w = 2 * C_HEADS * C_DK + 2 * C_HEADS * C_DV
    swap = (jnp.arange(D_ROPE) + D_ROPE // 2) % D_ROPE
    kr0 = c_w + Q_LORA + KV_LORA
    used = kr0 + 2 * D_ROPE
    total = -(-used // MXU_DIM) * MXU_DIM
    z2 = _normproj(x.reshape(n, d), norm_g, _odd_weight(w_in, kr0))
    z = z2.reshape(b, t, total)

    hp = 4
    hspec = pl.BlockSpec((1, hp * C_DK), lambda bi, g: (0, g))
    yc = _gla_call(
        z, C_HEADS, hp, C_DK, C_DV,
        (0, C_HEADS * C_DK, 2 * C_HEADS * C_DK, 2 * C_HEADS * C_DK + C_HEADS * C_DV, C_HEADS * C_DK),
        C_DK,
        ((jnp.log(lb).reshape(1, -1), hspec), (jnp.log1p(-lb).reshape(1, -1), hspec), ((1.0 - lb).reshape(1, -1), hspec)),
        c_norm, "hgrn")

    dq = D_NOPE + D_ROPE
    wq = w_q_up.reshape(Q_LORA, D_HEADS, dq)
    wq_rope = wq[:, :, D_NOPE:]
    wq_p = jnp.concatenate([wq[:, :, :D_NOPE].reshape(Q_LORA, -1),
                            jnp.concatenate([wq_rope, wq_rope[:, :, swap]], axis=-1).reshape(Q_LORA, -1)],
                           axis=1).astype(BF16)
    wkv = w_kv_up.reshape(KV_LORA, D_HEADS, D_NOPE + D_V)
    wkv_p = jnp.concatenate([wkv[:, :, :D_NOPE].reshape(KV_LORA, -1), wkv[:, :, D_NOPE:].reshape(KV_LORA, -1)],
                            axis=1).astype(BF16)
    qn = _normproj(z2, q_a_norm, wq_p, x_col_block=c_w // Q_LORA).reshape(b, t, -1)
    kvn = _normproj(z2, kv_a_norm, wkv_p, x_col_block=(c_w + Q_LORA) // KV_LORA).reshape(b, t, -1)

    pos = jnp.arange(t, dtype=F32)
    half = D_ROPE // 2
    inv = ROPE_THETA ** (-jnp.arange(half, dtype=F32) / half)
    ang = pos[:, None] * inv[None, :]
    cos, sin = jnp.cos(ang), jnp.sin(ang)
    tab = jnp.concatenate([cos, cos, -sin, sin], axis=1)
    pair = lambda g: jnp.concatenate([g[D_NOPE:], g[D_NOPE:][swap]]).reshape(1, LANES)
    yd = _mla(qn, kvn, z, kr0, tab, q_norm[:D_NOPE].reshape(1, LANES), pair(q_norm),
              k_norm[:D_NOPE].reshape(1, LANES), pair(k_norm))
    return _outproj(yc.reshape(n, -1), yd.reshape(n, -1), w_out.astype(BF16), x.reshape(n, d)).reshape(b, t, d)


def kernel(x, meta_tokens, ab_norm, ab_w_in, a_conv_w, a_conv_b, a_b_i, a_b_f, a_head_norm, b_w_gate2, b_b_gate, b_head_norm, ab_w_out, cd_norm, cd_w_in, c_lower_bound, c_head_norm, d_q_a_norm, d_w_q_up, d_kv_a_norm, d_w_kv_up, d_q_norm, d_k_norm, cd_w_out, moe_norm, moe_w_group, moe_b_group, moe_w_expert, moe_b_expert, moe_w1, moe_w3, moe_w2):
    b = x.shape[0]
    depth = moe_norm.shape[0]
    h = jnp.concatenate([jnp.broadcast_to(meta_tokens.astype(x.dtype)[None], (b, N_META, D_MODEL)), x], axis=1)
    t = h.shape[1]
    lb_cum = jnp.cumsum(jax.nn.softmax(c_lower_bound.astype(F32), axis=0), axis=0)
    lower_bounds = lb_cum - lb_cum[0]
    for layer in range(depth):
        j = layer // 2
        if layer % 2 == 0:
            h = _even_layer(h, ab_norm[j], ab_w_in[j], a_conv_w[j], a_conv_b[j], a_b_i[j], a_b_f[j], a_head_norm[j],
                            b_w_gate2[j], b_b_gate[j], b_head_norm[j], ab_w_out[j])
        else:
            h = _odd_layer(h, lower_bounds[layer], cd_norm[j], cd_w_in[j], c_head_norm[j], d_q_a_norm[j],
                           d_w_q_up[j], d_kv_a_norm[j], d_w_kv_up[j], d_q_norm[j], d_k_norm[j], cd_w_out[j])
        h = _moe(h.reshape(b * t, D_MODEL), moe_norm[layer], moe_w_group[layer], moe_b_group[layer],
                 moe_w_expert[layer], moe_b_expert[layer], moe_w1, moe_w3, moe_w2, layer).reshape(b, t, D_MODEL)
    return h[:, N_META:]
```

```python
import functools
import math

import jax
import jax.numpy as jnp
from jax import lax
from jax.experimental import pallas as pl
from jax.experimental.pallas import tpu as pltpu

F32 = jnp.float32
BF16 = jnp.bfloat16
HIGHEST = lax.Precision.HIGHEST

D_MODEL = 2048
N_META = 16
CHUNK = 64
CONV_K = 4
EPS = 1e-6
A_HEADS, A_DK, A_DV = 4, 128, 256
B_HEADS, B_DK, B_DV = 4, 128, 256
GATE_RANK = 16
GATE_TAU = 16.0
C_HEADS, C_DK, C_DV = 8, 128, 128
D_HEADS, D_NOPE, D_ROPE, D_V = 8, 128, 64, 128
Q_LORA, KV_LORA = 512, 256
ROPE_THETA = 10000.0
N_GROUPS, EXPERTS_PER_GROUP = 4, 8
N_EXPERTS = N_GROUPS * EXPERTS_PER_GROUP
TOP_K = 2
D_EXPERT = 512

LANES = 128
MXU_DIM = 256
BF16_ROWS = 16
VMEM_LIMIT = 56 * 1024 * 1024
MOE_BLOCK = MXU_DIM
ATT_BLOCK = 256
ATT_PAD = ATT_BLOCK - N_META

_NT = (((1,), (1,)), ((), ()))
_TN = (((0,), (0,)), ((), ()))


def _dot(a, b, precision=None):
    return jnp.dot(a, b, preferred_element_type=F32, precision=precision)


def _dot_nt(a, b):
    return lax.dot_general(a, b, _NT, preferred_element_type=F32)


def _dot_tn(a, b):
    return lax.dot_general(a, b, _TN, preferred_element_type=F32)


def _bf(x):
    return x.astype(BF16)


def _log_sigmoid(x):
    return jnp.minimum(x, 0.0) - jnp.log1p(jnp.exp(-jnp.abs(x)))


def _sigmoid(x):
    return 1.0 / (1.0 + jnp.exp(-x))


def _silu(x):
    return x * _sigmoid(x)


def _row_tile(m, cap):
    best = None
    for t in range(BF16_ROWS, min(m, cap) + 1, BF16_ROWS):
        if m % t == 0:
            best = t
    assert best is not None, m
    return best


def _col_tile(n, cap):
    best = None
    for t in range(MXU_DIM, min(n, cap) + 1, MXU_DIM):
        if n % t == 0:
            best = t
    assert best is not None, n
    return best


def _params(*sem):
    return pltpu.CompilerParams(dimension_semantics=sem, vmem_limit_bytes=VMEM_LIMIT)


_RELAYOUT_ROWS = 256


def _even_weight_kernel(wa_ref, wb_ref, wc_ref, o_ref, *, n_plain, n_shift, shift, gate_cols):
    ob = pl.program_id(0)
    rows = o_ref.shape[0]
    chunks = rows // _RELAYOUT_ROWS

    @pl.when(ob < n_plain)
    def _():
        def body(c, carry):
            r = pl.ds(pl.multiple_of(c * _RELAYOUT_ROWS, _RELAYOUT_ROWS), _RELAYOUT_ROWS)
            o_ref[r, :] = _bf(wa_ref[r, :])
            return carry

        lax.fori_loop(0, chunks, body, 0)

    @pl.when((ob >= n_plain) & (ob < n_plain + n_shift))
    def _():
        def body(c, carry):
            r = pl.ds(pl.multiple_of(c * _RELAYOUT_ROWS, _RELAYOUT_ROWS), _RELAYOUT_ROWS)
            wide = jnp.concatenate([wa_ref[r, :], wb_ref[r, :]], axis=1)
            o_ref[r, :] = _bf(wide[:, shift:shift + MXU_DIM])
            return carry

        lax.fori_loop(0, chunks, body, 0)

    @pl.when(ob == n_plain + n_shift)
    def _():
        lane = lax.broadcasted_iota(jnp.int32, (_RELAYOUT_ROWS, LANES), 1)

        def body(c, carry):
            r = pl.ds(pl.multiple_of(c * _RELAYOUT_ROWS, _RELAYOUT_ROWS), _RELAYOUT_ROWS)
            first = jnp.where(lane < shift, wc_ref[r, :], jnp.where(lane < gate_cols, wb_ref[r, :], 0.0))
            o_ref[r, :] = _bf(jnp.concatenate([first, jnp.zeros_like(first)], axis=1))
            return carry

        lax.fori_loop(0, chunks, body, 0)


def _even_weight(w_in, a_w, g_w, b_w, rank):
    d = w_in.shape[0]
    assert a_w % MXU_DIM == 0 and b_w % MXU_DIM == 0 and g_w + rank <= LANES and d % _RELAYOUT_ROWS == 0
    n_plain, n_shift = a_w // MXU_DIM, b_w // MXU_DIM
    n_out = n_plain + n_shift + 1
    last = n_out - 1

    def b_index(ob):
        return (0, jnp.where(ob < n_plain, 0, jnp.where(ob < last, 2 * (ob + 1), (a_w + g_w + b_w) // LANES)))

    return pl.pallas_call(
        functools.partial(_even_weight_kernel, n_plain=n_plain, n_shift=n_shift, shift=g_w, gate_cols=g_w + rank),
        grid=(n_out,),
        in_specs=[
            pl.BlockSpec((d, MXU_DIM), lambda ob: (0, jnp.minimum(ob, last - 1))),
            pl.BlockSpec((d, LANES), b_index),
            pl.BlockSpec((d, LANES), lambda ob: (0, a_w // LANES)),
        ],
        out_specs=pl.BlockSpec((d, MXU_DIM), lambda ob: (0, ob)),
        out_shape=jax.ShapeDtypeStruct((d, n_out * MXU_DIM), BF16),
        compiler_params=_params("parallel"),
        name="even_weight",
    )(w_in, w_in, w_in)


def _odd_weight_kernel(w_ref, o_ref, *, n_plain):
    ob = pl.program_id(0)
    chunks = o_ref.shape[0] // _RELAYOUT_ROWS
    half = D_ROPE // 2

    def body(c, carry):
        r = pl.ds(pl.multiple_of(c * _RELAYOUT_ROWS, _RELAYOUT_ROWS), _RELAYOUT_ROWS)
        w = w_ref[r, :]

        @pl.when(ob < n_plain)
        def _():
            o_ref[r, :] = _bf(w)

        @pl.when(ob == n_plain)
        def _():
            pair = jnp.concatenate([w[:, :D_ROPE], w[:, half:D_ROPE], w[:, :half]], axis=1)
            o_ref[r, :] = _bf(jnp.concatenate([pair, jnp.zeros_like(pair)], axis=1))

        return carry

    lax.fori_loop(0, chunks, body, 0)


def _odd_weight(w_in, kr0):
    d = w_in.shape[0]
    assert kr0 % MXU_DIM == 0 and w_in.shape[1] == kr0 + D_ROPE and d % _RELAYOUT_ROWS == 0
    n_plain = kr0 // MXU_DIM
    return pl.pallas_call(
        functools.partial(_odd_weight_kernel, n_plain=n_plain),
        grid=(n_plain + 1,),
        in_specs=[pl.BlockSpec((d, MXU_DIM), lambda ob: (0, ob))],
        out_specs=pl.BlockSpec((d, MXU_DIM), lambda ob: (0, ob)),
        out_shape=jax.ShapeDtypeStruct((d, kr0 + MXU_DIM), BF16),
        compiler_params=_params("parallel"),
        name="odd_weight",
    )(w_in)


def _normproj_kernel(x_ref, g_ref, w_ref, o_ref, xs_ref):
    tm = xs_ref.shape[0]

    @pl.when(pl.program_id(1) == 0)
    def _():
        def body(c, carry):
            r0 = pl.multiple_of(c * BF16_ROWS, BF16_ROWS)
            x = x_ref[pl.ds(r0, BF16_ROWS), :]
            ms = jnp.mean(x * x, axis=-1, keepdims=True)
            xs_ref[pl.ds(r0, BF16_ROWS), :] = _bf(x * lax.rsqrt(ms + EPS) * g_ref[...])
            return carry

        lax.fori_loop(0, tm // BF16_ROWS, body, 0)

    o_ref[...] = _dot(xs_ref[...], w_ref[...]).astype(o_ref.dtype)


def _normproj(x2d, gain, w, *, x_col_block=0, out_dtype=F32):
    m = x2d.shape[0]
    k, n = w.shape
    tm = _row_tile(m, 688)
    tn = _col_tile(n, 1280)
    return pl.pallas_call(
        _normproj_kernel,
        grid=(m // tm, n // tn),
        in_specs=[
            pl.BlockSpec((tm, k), lambda i, j: (i, x_col_block)),
            pl.BlockSpec((1, k), lambda i, j: (0, 0)),
            pl.BlockSpec((k, tn), lambda i, j: (0, j)),
        ],
        out_specs=pl.BlockSpec((tm, tn), lambda i, j: (i, j)),
        out_shape=jax.ShapeDtypeStruct((m, n), out_dtype),
        scratch_shapes=[pltpu.VMEM((tm, k), BF16)],
        compiler_params=_params("parallel", "arbitrary"),
        name="normproj",
    )(x2d, gain.reshape(1, k).astype(F32), w)


def _outproj_kernel(ya_ref, yb_ref, w_ref, r_ref, o_ref):
    ka = ya_ref.shape[1]
    acc = _dot(ya_ref[...], w_ref[:ka, :]) + _dot(yb_ref[...], w_ref[ka:, :])
    o_ref[...] = r_ref[...] + acc


def _outproj(ya, yb, w, res):
    m, ka = ya.shape
    kb = yb.shape[1]
    n = w.shape[1]
    tm = _row_tile(m, 1376)
    tn = _col_tile(n, 1024)
    return pl.pallas_call(
        _outproj_kernel,
        grid=(m // tm, n // tn),
        in_specs=[
            pl.BlockSpec((tm, ka), lambda i, j: (i, 0)),
            pl.BlockSpec((tm, kb), lambda i, j: (i, 0)),
            pl.BlockSpec((ka + kb, tn), lambda i, j: (0, j)),
            pl.BlockSpec((tm, tn), lambda i, j: (i, j)),
        ],
        out_specs=pl.BlockSpec((tm, tn), lambda i, j: (i, j)),
        out_shape=jax.ShapeDtypeStruct((m, n), F32),
        compiler_params=_params("parallel", "arbitrary"),
        name="outproj",
    )(ya, yb, w, res)


def _mlstm_kernel(bi_ref, bf_ref, q_ref, k_ref, v_ref, og_ref, aic_ref, afc_ref, air_ref, afr_ref,
                  cwq_ref, cwk_ref, cbq_ref, cbk_ref, hn_ref, o_ref, c_ref, n_ref, m_ref, *, heads):
    t_total = q_ref.shape[1]
    n_chunks = (t_total - N_META) // CHUNK
    head0 = pl.program_id(1) * heads

    c_ref[...] = jnp.zeros_like(c_ref)
    n_ref[...] = jnp.zeros_like(n_ref)
    m_ref[...] = jnp.zeros_like(m_ref)

    def conv(win, cw, cb, length):
        y = cb
        for j in range(CONV_K):
            y = y + win[8 - (CONV_K - 1) + j:8 - (CONV_K - 1) + j + length, :] * cw[j:j + 1, :]
        return _silu(y)

    hs = range(heads)
    kcol = [slice(hh * A_DK, (hh + 1) * A_DK) for hh in hs]
    vcol = [slice(hh * A_DV, (hh + 1) * A_DV) for hh in hs]

    def gates(hh, o, ci, length, causal, upper):
        b_i = bi_ref[head0 + hh]
        b_f = bf_ref[head0 + hh]
        ig_c = aic_ref[0, hh, pl.ds(o, length), :] + b_i
        lf_c = _log_sigmoid(afc_ref[0, hh, pl.ds(o, length), :] + b_f)
        ig_r = air_ref[0, hh, pl.ds(ci, 1), :][:, :length] + b_i
        lf_r = _log_sigmoid(afr_ref[0, hh, pl.ds(ci, 1), :][:, :length] + b_f)
        b_c = jnp.sum(jnp.where(causal, lf_r, 0.0), axis=1, keepdims=True)
        b_r = jnp.sum(jnp.where(upper, lf_c, 0.0), axis=0, keepdims=True)
        b_end = b_c[length - 1:length, :]
        w_end = b_end - b_c + ig_c
        m_loc = jnp.max(w_end, axis=0, keepdims=True)
        d = jnp.where(causal, b_c - b_r + ig_r, -jnp.inf)
        return b_c, b_end, jnp.exp(w_end - m_loc), m_loc, d, jnp.max(d, axis=1, keepdims=True)

    def chunk(o, ci, length, qwin, kwin):
        row = lax.broadcasted_iota(jnp.int32, (length, length), 0)
        col = lax.broadcasted_iota(jnp.int32, (length, length), 1)
        causal = col <= row
        gt = [gates(hh, o, ci, length, causal, row <= col) for hh in hs]
        q = [conv(qwin[hh], cwq_ref[:, kcol[hh]], cbq_ref[:, kcol[hh]], length) for hh in hs]
        k = [conv(kwin[hh], cwk_ref[:, kcol[hh]], cbk_ref[:, kcol[hh]], length) * (A_DK ** -0.5) for hh in hs]
        vb = [_bf(v_ref[0, pl.ds(o, length), vcol[hh]]) for hh in hs]
        qb = [_bf(q[hh]) for hh in hs]
        k_w = [k[hh] * gt[hh][2] for hh in hs]
        qk = [_dot_nt(qb[hh], _bf(k[hh])) for hh in hs]
        c_in = [c_ref[hh] for hh in hs]
        q_c = [_dot(qb[hh], _bf(c_in[hh])) for hh in hs]
        c_loc = [_dot_tn(_bf(k_w[hh]), vb[hh]) for hh in hs]
        s, a_t, m_t, q_n = [], [], [], []
        for hh in hs:
            b_c, b_end, _, m_loc, d, d_max = gt[hh]
            m_in = m_ref[hh]
            inter = b_c + m_in
            m_t.append(jnp.maximum(inter, d_max))
            s.append(qk[hh] * jnp.exp(d - m_t[hh]))
            a_t.append(jnp.exp(inter - m_t[hh]))
            m_new = jnp.maximum(b_end + m_in, m_loc)
            a = jnp.exp(b_end + m_in - m_new)
            c = jnp.exp(m_loc - m_new)
            n_in = n_ref[hh]
            c_ref[hh] = a * c_in[hh] + c * c_loc[hh]
            n_ref[hh] = a * n_in + c * jnp.sum(k_w[hh], axis=0, keepdims=True)
            m_ref[hh] = m_new
            q_n.append(jnp.sum(q[hh] * n_in, axis=1, keepdims=True))
        num = [_dot(_bf(s[hh]), vb[hh]) + a_t[hh] * q_c[hh] for hh in hs]
        for hh in hs:
            den = jnp.sum(s[hh], axis=1, keepdims=True) + a_t[hh] * q_n[hh]
            h = num[hh] / jnp.maximum(jnp.abs(den), jnp.exp(-m_t[hh]))
            hn = h * lax.rsqrt(jnp.mean(h * h, axis=-1, keepdims=True) + EPS) * hn_ref[:, vcol[hh]]
            y = _sigmoid(og_ref[0, pl.ds(o, length), vcol[hh]]) * hn
            o_ref[0, pl.ds(o, length), vcol[hh]] = y.astype(o_ref.dtype)

    zeros8 = jnp.zeros((8, A_DK), F32)
    chunk(0, 0, N_META,
          [jnp.concatenate([zeros8, q_ref[0, 0:N_META, kcol[hh]]], axis=0) for hh in hs],
          [jnp.concatenate([zeros8, k_ref[0, 0:N_META, kcol[hh]]], axis=0) for hh in hs])

    def body(c, carry):
        o = pl.multiple_of(N_META + c * CHUNK, BF16_ROWS)
        w0 = pl.multiple_of(N_META - 8 + c * CHUNK, 8)
        chunk(o, c + 1, CHUNK, [q_ref[0, pl.ds(w0, CHUNK + 8), kcol[hh]] for hh in hs],
              [k_ref[0, pl.ds(w0, CHUNK + 8), kcol[hh]] for hh in hs])
        return carry

    lax.fori_loop(0, n_chunks, body, 0)


def _mlstm(z, ai_c, af_c, ai_r, af_r, conv_w, conv_b, b_i, b_f, head_norm, heads=2):
    b, t, _ = z.shape
    hk = A_HEADS * A_DK
    wk, wv = heads * A_DK, heads * A_DV
    smem = pl.BlockSpec(memory_space=pltpu.SMEM)
    col = lambda width, off: (lambda bi, g: (bi, 0, off // width + g))
    gate_c = pl.BlockSpec((1, heads, t, 1), lambda bi, g: (bi, g, 0, 0))
    gate_r = pl.BlockSpec((1, heads) + ai_r.shape[2:], lambda bi, g: (bi, g, 0, 0))
    return pl.pallas_call(
        functools.partial(_mlstm_kernel, heads=heads),
        grid=(b, A_HEADS // heads),
        in_specs=[
            smem, smem,
            pl.BlockSpec((1, t, wk), col(wk, 0)),
            pl.BlockSpec((1, t, wk), col(wk, hk)),
            pl.BlockSpec((1, t, wv), col(wv, 2 * hk)),
            pl.BlockSpec((1, t, wv), col(wv, 2 * hk + A_HEADS * A_DV)),
            gate_c, gate_c, gate_r, gate_r,
            pl.BlockSpec((CONV_K, wk), lambda bi, g: (0, g)),
            pl.BlockSpec((CONV_K, wk), lambda bi, g: (0, A_HEADS // heads + g)),
            pl.BlockSpec((1, wk), lambda bi, g: (0, g)),
            pl.BlockSpec((1, wk), lambda bi, g: (0, A_HEADS // heads + g)),
            pl.BlockSpec((1, wv), lambda bi, g: (0, g)),
        ],
        out_specs=pl.BlockSpec((1, t, wv), lambda bi, g: (bi, 0, g)),
        out_shape=jax.ShapeDtypeStruct((b, t, A_HEADS * A_DV), BF16),
        scratch_shapes=[pltpu.VMEM((heads, A_DK, A_DV), F32), pltpu.VMEM((heads, 1, A_DK), F32),
                        pltpu.VMEM((heads, 1, 1), F32)],
        compiler_params=_params("parallel", "parallel"),
        name="mlstm",
    )(b_i, b_f, z, z, z, z, ai_c, af_c, ai_r, af_r, conv_w, conv_w, conv_b.reshape(1, -1), conv_b.reshape(1, -1),
      head_norm.reshape(1, -1))


def _gla_kernel(q_ref, k_ref, v_ref, og_ref, g_ref, p0_ref, p1_ref, p2_ref, hn_ref, o_ref, st_ref, *, mode, heads):
    t_total = q_ref.shape[1]
    n_chunks = (t_total - N_META) // CHUNK
    dv, dk = st_ref.shape[1:]
    st_ref[...] = jnp.zeros_like(st_ref)

    hs = range(heads)
    kcol = [slice(hh * dk, (hh + 1) * dk) for hh in hs]
    vcol = [slice(hh * dv, (hh + 1) * dv) for hh in hs]

    def gate_inputs(hh, o, length):
        q = q_ref[0, pl.ds(o, length), kcol[hh]]
        if mode == "gla":
            pre = _dot(g_ref[0, pl.ds(o, length), :], p0_ref[hh], precision=HIGHEST) + p1_ref[:, kcol[hh]]
            return q * (dk ** -0.5), k_ref[0, pl.ds(o, length), kcol[hh]], _log_sigmoid(pre) / GATE_TAU
        fpre = g_ref[0, pl.ds(o, length), kcol[hh]]
        a = p0_ref[:, kcol[hh]]
        bb = p1_ref[:, kcol[hh]] + _log_sigmoid(fpre)
        lg = jnp.maximum(a, bb) + jnp.log1p(jnp.exp(-jnp.abs(a - bb)))
        return q, p2_ref[:, kcol[hh]] * _sigmoid(-fpre), lg

    def cumsum_time(tri, lg):
        hi = _bf(lg)
        r1 = lg - hi.astype(F32)
        mid = _bf(r1)
        lo = _bf(r1 - mid.astype(F32))
        parts = _dot(tri, jnp.concatenate([hi, mid, lo], axis=1))
        return parts[:, :dk] + (parts[:, dk:2 * dk] + parts[:, 2 * dk:])

    def chunk(o, length):
        row = lax.broadcasted_iota(jnp.int32, (length, length), 0)
        col = lax.broadcasted_iota(jnp.int32, (length, length), 1)
        causal = col <= row
        tri = jnp.where(causal, 1.0, 0.0).astype(BF16)
        qkl = [gate_inputs(hh, o, length) for hh in hs]
        vb = [_bf(v_ref[0, pl.ds(o, length), vcol[hh]]) for hh in hs]
        g = [cumsum_time(tri, qkl[hh][2]) for hh in hs]
        g_end = [g[hh][length - 1:length, :] for hh in hs]
        g_mid = [g[hh][length // 2:length // 2 + 1, :] for hh in hs]
        s = [_dot_nt(_bf(qkl[hh][0] * jnp.exp(g[hh] - g_mid[hh])), _bf(qkl[hh][1] * jnp.exp(g_mid[hh] - g[hh])))
             for hh in hs]
        st_in = [st_ref[hh] for hh in hs]
        inter = [_dot_nt(_bf(qkl[hh][0] * jnp.exp(g[hh])), _bf(st_in[hh])) for hh in hs]
        local = [_dot_tn(vb[hh], _bf(qkl[hh][1] * jnp.exp(g_end[hh] - g[hh]))) for hh in hs]
        for hh in hs:
            st_ref[hh] = st_in[hh] * jnp.exp(g_end[hh]) + local[hh]
        out = [_dot(_bf(jnp.where(causal, s[hh], 0.0)), vb[hh]) + inter[hh] for hh in hs]
        for hh in hs:
            hn = out[hh] * lax.rsqrt(jnp.mean(out[hh] * out[hh], axis=-1, keepdims=True) + EPS) * hn_ref[:, vcol[hh]]
            og = og_ref[0, pl.ds(o, length), vcol[hh]]
            gate = _silu(og) if mode == "gla" else _sigmoid(og)
            o_ref[0, pl.ds(o, length), vcol[hh]] = (gate * hn).astype(o_ref.dtype)

    chunk(0, N_META)

    def body(c, carry):
        chunk(pl.multiple_of(N_META + c * CHUNK, BF16_ROWS), CHUNK)
        return carry

    lax.fori_loop(0, n_chunks, body, 0)


def _gla_call(z, n_heads, heads, dk, dv, blocks, gate_width, params, head_norm, mode):
    b, t, _ = z.shape
    q0, k0, v0, og0, g0 = blocks
    zspec = lambda width, off, grouped=True: pl.BlockSpec(
        (1, t, width), (lambda bi, g: (bi, 0, off // width + (g if grouped else 0))))
    (p0, s0), (p1, s1), (p2, s2) = params
    gate_spec = zspec(gate_width, g0, grouped=False) if mode == "gla" else zspec(heads * dk, g0)
    return pl.pallas_call(
        functools.partial(_gla_kernel, mode=mode, heads=heads),
        grid=(b, n_heads // heads),
        in_specs=[
            zspec(heads * dk, q0), zspec(heads * dk, k0), zspec(heads * dv, v0), zspec(heads * dv, og0),
            gate_spec, s0, s1, s2,
            pl.BlockSpec((1, heads * dv), lambda bi, g: (0, g)),
        ],
        out_specs=pl.BlockSpec((1, t, heads * dv), lambda bi, g: (bi, 0, g)),
        out_shape=jax.ShapeDtypeStruct((b, t, n_heads * dv), BF16),
        scratch_shapes=[pltpu.VMEM((heads, dv, dk), F32)],
        compiler_params=_params("parallel", "parallel"),
        name="gla_" + mode,
    )(z, z, z, z, z, p0, p1, p2, head_norm.reshape(1, -1))


def _mla_kernel(qn_ref, qr_ref, kn_ref, v_ref, kr_ref, tab_ref, gqn_ref, gqr_ref, gkn_ref, gkr_ref,
                o_ref, qf_ref, kf_ref, vf_ref):
    t_total = qn_ref.shape[1]
    n_blocks = (ATT_PAD + t_total) // ATT_BLOCK
    dqk = D_NOPE + D_ROPE
    scale = dqk ** -0.5
    rows = _row_tile(t_total, 768)
    lane = lax.broadcasted_iota(jnp.int32, (rows, LANES), 1)
    first_half = lane < D_ROPE

    qf_ref[0:ATT_PAD, :] = jnp.zeros((ATT_PAD, 2 * LANES), BF16)
    kf_ref[0:ATT_PAD, :] = jnp.zeros((ATT_PAD, 2 * LANES), BF16)
    vf_ref[0:ATT_PAD, :] = jnp.zeros((ATT_PAD, D_V), BF16)

    def rope_pair(x, gains, tab):
        p = x * gains * tab
        return p + pltpu.roll(p, D_ROPE, 1)

    def prep(c):
        r0 = c * rows
        dst = ATT_PAD + c * rows
        tab = tab_ref[pl.ds(r0, rows), :]
        qn = qn_ref[0, pl.ds(r0, rows), :]
        qr = qr_ref[0, pl.ds(r0, rows), :]
        ssq = jnp.sum(qn * qn, axis=-1, keepdims=True) + 0.5 * jnp.sum(qr * qr, axis=-1, keepdims=True)
        rq = lax.rsqrt(ssq / dqk + EPS) * scale
        qf_ref[pl.ds(dst, rows), 0:LANES] = _bf(qn * gqn_ref[...] * rq)
        qf_ref[pl.ds(dst, rows), LANES:2 * LANES] = _bf(rope_pair(qr, gqr_ref[...], tab) * rq)
        kn = kn_ref[0, pl.ds(r0, rows), :]
        kr = kr_ref[0, pl.ds(r0, rows), :]
        ssk = jnp.sum(kn * kn, axis=-1, keepdims=True) + 0.5 * jnp.sum(kr * kr, axis=-1, keepdims=True)
        rk = lax.rsqrt(ssk / dqk + EPS)
        kf_ref[pl.ds(dst, rows), 0:LANES] = _bf(kn * gkn_ref[...] * rk)
        kf_ref[pl.ds(dst, rows), LANES:2 * LANES] = _bf(
            jnp.where(first_half, rope_pair(kr, gkr_ref[...], tab) * rk, 0.0))
        vf_ref[pl.ds(dst, rows), :] = _bf(v_ref[0, pl.ds(r0, rows), :])

    for c in range(t_total // rows):
        prep(c)

    qpos = lax.broadcasted_iota(jnp.int32, (ATT_BLOCK, ATT_BLOCK), 0)
    kpos = lax.broadcasted_iota(jnp.int32, (ATT_BLOCK, ATT_BLOCK), 1)
    neg = -jnp.inf

    for qi in range(n_blocks):
        q = qf_ref[qi * ATT_BLOCK:(qi + 1) * ATT_BLOCK, :]
        s = _dot_nt(q, kf_ref[0:(qi + 1) * ATT_BLOCK, :])
        parts = [s[:, j * ATT_BLOCK:(j + 1) * ATT_BLOCK] for j in range(qi + 1)]
        parts[0] = jnp.where(kpos >= ATT_PAD, parts[0], neg)
        parts[qi] = jnp.where(kpos <= qpos, parts[qi], neg)
        top = functools.reduce(jnp.maximum, parts)
        m = jnp.max(top, axis=-1, keepdims=True)
        if qi == 0:
            m = jnp.where(m == neg, 0.0, m)
        probs = [jnp.exp(part - m) for part in parts]
        l = jnp.sum(functools.reduce(jnp.add, probs), axis=-1, keepdims=True)
        pv = _dot(jnp.concatenate([_bf(pr) for pr in probs], axis=1), vf_ref[0:(qi + 1) * ATT_BLOCK, :])
        if qi == 0:
            out = pv / jnp.where(l == 0.0, 1.0, l)
            o_ref[0, 0:N_META, :] = out[ATT_PAD:, :].astype(o_ref.dtype)
        else:
            dst = qi * ATT_BLOCK - ATT_PAD
            o_ref[0, dst:dst + ATT_BLOCK, :] = (pv / l).astype(o_ref.dtype)


def _mla(qn, kvn, z, kr_col, tab, gqn, gqr, gkn, gkr):
    b, t, _ = qn.shape
    assert (ATT_PAD + t) % ATT_BLOCK == 0 and t % (3 * BF16_ROWS) == 0
    tp = ATT_PAD + t
    hspec = lambda off: pl.BlockSpec((1, t, LANES), lambda bi, h: (bi, 0, off + h))
    gspec = pl.BlockSpec((1, LANES), lambda bi, h: (0, 0))
    return pl.pallas_call(
        _mla_kernel,
        grid=(b, D_HEADS),
        in_specs=[
            hspec(0), hspec(D_HEADS), hspec(0), hspec(D_HEADS),
            pl.BlockSpec((1, t, LANES), lambda bi, h: (bi, 0, kr_col // LANES)),
            pl.BlockSpec((t, LANES), lambda bi, h: (0, 0)),
            gspec, gspec, gspec, gspec,
        ],
        out_specs=pl.BlockSpec((1, t, D_V), lambda bi, h: (bi, 0, h)),
        out_shape=jax.ShapeDtypeStruct((b, t, D_HEADS * D_V), BF16),
        scratch_shapes=[pltpu.VMEM((tp, 2 * LANES), BF16), pltpu.VMEM((tp, 2 * LANES), BF16),
                        pltpu.VMEM((tp, D_V), BF16)],
        compiler_params=_params("parallel", "parallel"),
        name="mla",
    )(qn, qn, kvn, kvn, z, tab, gqn, gqr, gkn, gkr)


def _pack_bf16_pairs(v):
    w = v.shape[1] // 2
    bits = pltpu.bitcast(_bf(v).astype(F32), jnp.uint32)
    return (bits[:, :w] >> 16) | (bits[:, w:] & jnp.uint32(0xFFFF0000))


def _unpack_lo(words):
    return pltpu.bitcast(words << 16, F32)


def _unpack_hi(words):
    return pltpu.bitcast(words & jnp.uint32(0xFFFF0000), F32)


def _router_kernel(x_ref, g_ref, w_ref, b_ref, gate_ref, idx_ref, xg_ref, cnt_ref, carry_ref):
    tm = x_ref.shape[0]

    @pl.when(pl.program_id(0) == 0)
    def _():
        carry_ref[...] = jnp.zeros_like(carry_ref)

    x = x_ref[...]
    ms = jnp.mean(x * x, axis=-1, keepdims=True)
    xn = x * lax.rsqrt(ms + EPS) * g_ref[...]
    logits = _dot(xn, w_ref[...], precision=HIGHEST) + b_ref[...]
    lane = lax.broadcasted_iota(jnp.int32, logits.shape, 1)
    lane_f = lane.astype(F32)
    neg = -jnp.inf
    big = float(LANES)

    is_group = lane < N_GROUPS
    g_max = jnp.max(jnp.where(is_group, logits, neg), axis=-1, keepdims=True)
    g_sum = jnp.sum(jnp.where(is_group, jnp.exp(logits - g_max), 0.0), axis=-1, keepdims=True)
    p_top = 1.0 / g_sum
    grp = jnp.min(jnp.where(is_group & (logits == g_max), lane_f, big), axis=-1, keepdims=True)

    e_lo = N_GROUPS + grp * EXPERTS_PER_GROUP
    in_grp = (lane_f >= e_lo) & (lane_f < e_lo + EXPERTS_PER_GROUP)
    e_max = jnp.max(jnp.where(in_grp, logits, neg), axis=-1, keepdims=True)
    e_sum = jnp.sum(jnp.where(in_grp, jnp.exp(logits - e_max), 0.0), axis=-1, keepdims=True)
    i1 = jnp.min(jnp.where(in_grp & (logits == e_max), lane_f, big), axis=-1, keepdims=True)
    rest = in_grp & (lane_f != i1)
    e_2nd = jnp.max(jnp.where(rest, logits, neg), axis=-1, keepdims=True)
    i2 = jnp.min(jnp.where(rest & (logits == e_2nd), lane_f, big), axis=-1, keepdims=True)
    p1 = 1.0 / e_sum
    p2 = jnp.exp(e_2nd - e_max) / e_sum
    tot = p1 + p2
    gate_ref[...] = jnp.where(lane == 0, p_top * p1 / tot, jnp.where(lane == 1, p_top * p2 / tot, 0.0))

    e1 = i1 - N_GROUPS
    e2 = i2 - N_GROUPS
    hot = jnp.where((lane_f == e1) | (lane_f == e2), 1.0, 0.0)
    row = lax.broadcasted_iota(jnp.int32, (tm, tm), 0)
    col = lax.broadcasted_iota(jnp.int32, (tm, tm), 1)
    before = _dot(jnp.where(col < row, 1.0, 0.0).astype(BF16), _bf(hot)) + carry_ref[...]
    r1 = jnp.sum(jnp.where(lane_f == e1, before, 0.0), axis=-1, keepdims=True)
    r2 = jnp.sum(jnp.where(lane_f == e2, before, 0.0), axis=-1, keepdims=True)
    total = carry_ref[...] + jnp.sum(hot, axis=0, keepdims=True)
    carry_ref[...] = total
    cnt_ref[...] = jnp.broadcast_to(total, cnt_ref.shape).astype(jnp.int32)
    idx_ref[...] = jnp.where(lane == 0, e1, jnp.where(lane == 1, e2, jnp.where(lane == 2, r1, jnp.where(
        lane == 3, r2, 0.0)))).astype(jnp.int32)

    words = _pack_bf16_pairs(xn)
    for s in range(8):
        xg_ref[pl.ds(s, tm, stride=8), :] = words[:, s * LANES:(s + 1) * LANES]


def _router(x2d, gain, w_group, b_group, w_expert, b_expert):
    m, d = x2d.shape
    assert d == 2 * 8 * LANES
    tm = _row_tile(m, 688)
    pad = LANES - N_GROUPS - N_EXPERTS
    w = jnp.concatenate([w_group, w_expert, jnp.zeros((d, pad), F32)], axis=1)
    bias = jnp.concatenate([b_group, b_expert, jnp.zeros((pad,), F32)]).reshape(1, LANES)
    return pl.pallas_call(
        _router_kernel,
        grid=(m // tm,),
        in_specs=[
            pl.BlockSpec((tm, d), lambda i: (i, 0)),
            pl.BlockSpec((1, d), lambda i: (0, 0)),
            pl.BlockSpec((d, LANES), lambda i: (0, 0)),
            pl.BlockSpec((1, LANES), lambda i: (0, 0)),
        ],
        out_specs=[pl.BlockSpec((tm, LANES), lambda i: (i, 0)), pl.BlockSpec((tm, LANES), lambda i: (i, 0)),
                   pl.BlockSpec((tm * 8, LANES), lambda i: (i, 0)), pl.BlockSpec((8, LANES), lambda i: (0, 0))],
        out_shape=[jax.ShapeDtypeStruct((m, LANES), F32), jax.ShapeDtypeStruct((m, LANES), jnp.int32),
                   jax.ShapeDtypeStruct((m * 8, LANES), jnp.uint32), jax.ShapeDtypeStruct((8, LANES), jnp.int32)],
        scratch_shapes=[pltpu.VMEM((1, LANES), F32)],
        compiler_params=_params("arbitrary"),
        name="router",
    )(x2d, gain.reshape(1, d), w, bias)


def _invert_kernel(dest_ref, inv_ref):
    def clear(s, carry):
        inv_ref[s] = -1
        return carry

    lax.fori_loop(0, inv_ref.shape[0], clear, 0, unroll=8)

    def put(f, carry):
        inv_ref[dest_ref[f]] = f
        return carry

    lax.fori_loop(0, dest_ref.shape[0], put, 0, unroll=8)


def _invert(dest, p):
    assert p % 8 == 0 and dest.shape[0] % 8 == 0
    smem = pl.BlockSpec(memory_space=pltpu.SMEM)
    return pl.pallas_call(
        _invert_kernel, in_specs=[smem], out_specs=smem,
        out_shape=jax.ShapeDtypeStruct((p,), jnp.int32), name="moe_invert",
    )(dest)


def _expert_kernel(be_ref, nu_ref, src_ref, dst_ref, xg_hbm, w1_ref, w3_ref, w2_ref, o_hbm,
                   xbuf, ybuf, xs_ref, w1s, w3s, w2s, sem_in, sem_out):
    i = pl.program_id(0)
    n_used = nu_ref[0]
    par = i % 2
    half = D_MODEL // 2
    half_e = D_EXPERT // 2

    def gather(block, r, slot):
        src = pl.multiple_of(src_ref[block * MOE_BLOCK + r], 8)
        return pltpu.make_async_copy(xg_hbm.at[pl.ds(src, 8), :], xbuf.at[slot, pl.ds(r * 8, 8), :], sem_in.at[slot])

    def scatter(block, r, slot):
        dst = pl.multiple_of(dst_ref[(block + 1) * MOE_BLOCK + r], 8)
        return pltpu.make_async_copy(ybuf.at[slot, pl.ds(r * 8, 8), :], o_hbm.at[pl.ds(dst, 8), :], sem_out.at[slot])

    def wait_gathers(slot):
        pltpu.make_async_copy(xbuf.at[1 - slot], xbuf.at[slot], sem_in.at[slot]).wait()

    def wait_scatters(slot):
        pltpu.make_async_copy(ybuf.at[slot], ybuf.at[1 - slot], sem_out.at[slot]).wait()

    @pl.when(i == 0)
    def _():
        ybuf[1] = jnp.zeros(ybuf.shape[1:], ybuf.dtype)
        n_real = o_hbm.shape[0] - 2 * MOE_BLOCK * 8
        fill1 = pltpu.make_async_copy(ybuf.at[1], o_hbm.at[pl.ds(n_real + MOE_BLOCK * 8, MOE_BLOCK * 8), :],
                                      sem_out.at[1])
        fill1.start()
        fill1.wait()
        pltpu.make_async_copy(ybuf.at[1], o_hbm.at[pl.ds(n_real, MOE_BLOCK * 8), :], sem_out.at[0]).start()

        def first(r, carry):
            gather(0, r, 0).start()
            return carry

        lax.fori_loop(0, MOE_BLOCK, first, 0, unroll=8)

    @pl.when(i < n_used)
    def _():
        @pl.when((i == 0) | (be_ref[i] != be_ref[jnp.maximum(i - 1, 0)]))
        def _():
            w1s[...] = _bf(w1_ref[0, 0])
            w3s[...] = _bf(w3_ref[0, 0])
            w2s[...] = _bf(w2_ref[0, 0])

        wait_gathers(par)

        def move_rows(group, n_groups=6):
            lo, hi = group * MOE_BLOCK // n_groups, (group + 1) * MOE_BLOCK // n_groups
            for r in range(lo, hi):
                gather(i + 1, r, 1 - par).start(priority=r % 2)
                scatter(i - 1, r, 1 - par).start(priority=(r + 1) % 2)

        for s in range(8):
            words = xbuf[par, pl.ds(s, MOE_BLOCK, stride=8), :]
            xs_ref[:, s * LANES:(s + 1) * LANES] = _bf(_unpack_lo(words))
            xs_ref[:, half + s * LANES:half + (s + 1) * LANES] = _bf(_unpack_hi(words))
        move_rows(0)
        xb = xs_ref[...]
        h1a = _dot(xb, w1s[:, :half_e])
        move_rows(1)
        h3a = _dot(xb, w3s[:, :half_e])
        move_rows(2)
        act_a = _bf(_silu(h1a) * h3a)
        h1b = _dot(xb, w1s[:, half_e:])
        move_rows(3)
        h3b = _dot(xb, w3s[:, half_e:])
        move_rows(4)
        act_b = _bf(_silu(h1b) * h3b)
        y = _dot(act_a, w2s[:half_e, :])
        move_rows(5)
        y = y + _dot(act_b, w2s[half_e:, :])

        wait_scatters(par)
        words = _pack_bf16_pairs(y)
        for s in range(8):
            ybuf[par, pl.ds(s, MOE_BLOCK, stride=8), :] = words[:, s * LANES:(s + 1) * LANES]

        @pl.when(i == n_used - 1)
        def _():
            def last(r, carry):
                scatter(i, r, par).start()
                return carry

            lax.fori_loop(0, MOE_BLOCK, last, 0, unroll=8)
            wait_scatters(par)
            wait_scatters(1 - par)
            wait_gathers(1 - par)


def _moe(x2d, gain, w_group, b_group, w_expert, b_expert, w1, w3, w2, layer):
    n, d = x2d.shape
    gates_l, idx_l, xg, cnt = _router(x2d, gain, w_group, b_group, w_expert, b_expert)

    a = n * TOP_K
    n_blocks = -(-a // MOE_BLOCK) + N_EXPERTS
    p = n_blocks * MOE_BLOCK
    counts = cnt[0, :N_EXPERTS]
    padded = (counts + MOE_BLOCK - 1) // MOE_BLOCK * MOE_BLOCK
    pad_end = jnp.cumsum(padded)
    pad_start = pad_end - padded
    dest = (pad_start[idx_l[:, :TOP_K]] + idx_l[:, TOP_K:2 * TOP_K]).reshape(-1).astype(jnp.int32)
    blk0 = jnp.arange(n_blocks, dtype=jnp.int32) * MOE_BLOCK
    block_expert = jnp.minimum(jnp.searchsorted(pad_end, blk0, side="right"), N_EXPERTS - 1).astype(jnp.int32)
    n_used = (pad_end[-1] // MOE_BLOCK).astype(jnp.int32).reshape(1)
    codes = _invert(dest, p)
    slot = jnp.arange(p, dtype=jnp.int32)
    spare = TOP_K * n + (slot // MOE_BLOCK % 2) * MOE_BLOCK + slot % MOE_BLOCK
    src_tok = (jnp.maximum(codes, 0) >> 1) * 8
    dst_row = jnp.where(codes >= 0, (codes & 1) * n + (codes >> 1), spare) * 8
    lead = (TOP_K * n + MOE_BLOCK + jnp.arange(MOE_BLOCK, dtype=jnp.int32)) * 8
    dst_row = jnp.concatenate([lead, dst_row])

    wspec = lambda shape: pl.BlockSpec((1, 1) + shape, lambda i, be, nu, src, dst: (layer, be[i], 0, 0))
    out_rows = TOP_K * n + 2 * MOE_BLOCK
    out2 = pl.pallas_call(
        _expert_kernel,
        grid_spec=pltpu.PrefetchScalarGridSpec(
            num_scalar_prefetch=4,
            grid=(n_blocks,),
            in_specs=[
                pl.BlockSpec(memory_space=pl.ANY),
                wspec((d, D_EXPERT)), wspec((d, D_EXPERT)), wspec((D_EXPERT, d)),
            ],
            out_specs=pl.BlockSpec(memory_space=pl.ANY),
            scratch_shapes=[
                pltpu.VMEM((2, MOE_BLOCK * 8, LANES), jnp.uint32), pltpu.VMEM((2, MOE_BLOCK * 8, LANES), jnp.uint32),
                pltpu.VMEM((MOE_BLOCK, d), BF16),
                pltpu.VMEM((d, D_EXPERT), BF16), pltpu.VMEM((d, D_EXPERT), BF16), pltpu.VMEM((D_EXPERT, d), BF16),
                pltpu.SemaphoreType.DMA((2,)), pltpu.SemaphoreType.DMA((2,)),
            ],
        ),
        out_shape=jax.ShapeDtypeStruct((out_rows * 8, LANES), jnp.uint32),
        compiler_params=_params("arbitrary"),
        name="moe_experts",
    )(block_expert, n_used, src_tok, dst_row, xg, w1, w3, w2)
    return _combine(x2d, gates_l, out2)


def _combine_kernel(x_ref, gate_ref, a_ref, b_ref, o_ref):
    tm = x_ref.shape[0]
    half = x_ref.shape[1] // 2
    g0 = gate_ref[:, 0:1]
    g1 = gate_ref[:, 1:2]
    for s in range(8):
        wa = a_ref[pl.ds(s, tm, stride=8), :]
        wb = b_ref[pl.ds(s, tm, stride=8), :]
        lo = slice(s * LANES, (s + 1) * LANES)
        hi = slice(half + s * LANES, half + (s + 1) * LANES)
        o_ref[:, lo] = x_ref[:, lo] + (g0 * _unpack_lo(wa) + g1 * _unpack_lo(wb))
        o_ref[:, hi] = x_ref[:, hi] + (g0 * _unpack_hi(wa) + g1 * _unpack_hi(wb))


def _combine(x2d, gates, out2):
    n, d = x2d.shape
    tm = _row_tile(n, 688)
    return pl.pallas_call(
        _combine_kernel,
        grid=(n // tm,),
        in_specs=[
            pl.BlockSpec((tm, d), lambda i: (i, 0)),
            pl.BlockSpec((tm, LANES), lambda i: (i, 0)),
            pl.BlockSpec((tm * 8, LANES), lambda i: (i, 0)),
            pl.BlockSpec((tm * 8, LANES), lambda i: (n // tm + i, 0)),
        ],
        out_specs=pl.BlockSpec((tm, d), lambda i: (i, 0)),
        out_shape=jax.ShapeDtypeStruct((n, d), F32),
        compiler_params=_params("parallel"),
        name="moe_combine",
    )(x2d, gates, out2, out2)


def _gate_layouts(cols, heads, t):
    b = cols.shape[0]
    rows = cols.transpose(0, 2, 1)
    meta = jnp.pad(rows[:, :, :N_META], ((0, 0), (0, 0), (0, CHUNK - N_META)))
    real = rows[:, :, N_META:].reshape(b, heads, -1, CHUNK)
    return rows[..., None], jnp.concatenate([meta[:, :, None, :], real], axis=2)


def _even_layer(x, norm_g, w_in, conv_w, conv_b, b_i, b_f, a_norm, w_gate2, b_gate, b_norm, w_out):
    b, t, d = x.shape
    n = b * t
    a_w = 2 * A_HEADS * A_DK + 2 * A_HEADS * A_DV
    g_w = 2 * A_HEADS
    b_w = 2 * B_HEADS * B_DK + 2 * B_HEADS * B_DV
    main = a_w + b_w
    gate_cols = g_w + GATE_RANK
    w = _even_weight(w_in, a_w, g_w, b_w, GATE_RANK)
    z = _normproj(x.reshape(n, d), norm_g, w).reshape(b, t, main + MXU_DIM)

    gates = z[:, :, main:main + g_w]
    ai_c, ai_r = _gate_layouts(gates[..., :A_HEADS], A_HEADS, t)
    af_c, af_r = _gate_layouts(gates[..., A_HEADS:], A_HEADS, t)
    ya = _mlstm(z, ai_c, af_c, ai_r, af_r, conv_w, conv_b, b_i, b_f, a_norm)

    wg = jnp.zeros((B_HEADS, MXU_DIM, B_DK), F32).at[:, g_w:g_w + GATE_RANK, :].set(
        w_gate2.reshape(GATE_RANK, B_HEADS, B_DK).transpose(1, 0, 2))
    dummy = jnp.zeros((1, B_HEADS * B_DK), F32)
    hp = 2
    hspec = pl.BlockSpec((1, hp * B_DK), lambda bi, g: (0, g))
    yb = _gla_call(
        z, B_HEADS, hp, B_DK, B_DV,
        (a_w, a_w + B_HEADS * B_DK, a_w + 2 * B_HEADS * B_DK, a_w + 2 * B_HEADS * B_DK + B_HEADS * B_DV, main),
        MXU_DIM,
        ((wg, pl.BlockSpec((hp, MXU_DIM, B_DK), lambda bi, g: (g, 0, 0))),
         (b_gate.reshape(1, -1), hspec), (dummy, hspec)),
        b_norm, "gla")
    return _outproj(ya.reshape(n, -1), yb.reshape(n, -1), w_out.astype(BF16), x.reshape(n, d)).reshape(b, t, d)


def _odd_layer(x, lb, norm_g, w_in, c_norm, q_a_norm, w_q_up, kv_a_norm, w_kv_up, q_norm, k_norm, w_out):
    b, t, d = x.shape
    n = b * t
    c_w = 2 * C_HEADS * C_DK + 2 * C_HEADS * C_DV
    swap = (jnp.arange(D_ROPE) + D_ROPE // 2) % D_ROPE
    kr0 = c_w + Q_LORA + KV_LORA
    used = kr0 + 2 * D_ROPE
    total = -(-used // MXU_DIM) * MXU_DIM
    z2 = _normproj(x.reshape(n, d), norm_g, _odd_weight(w_in, kr0))
    z = z2.reshape(b, t, total)

    hp = 4
    hspec = pl.BlockSpec((1, hp * C_DK), lambda bi, g: (0, g))
    yc = _gla_call(
        z, C_HEADS, hp, C_DK, C_DV,
        (0, C_HEADS * C_DK, 2 * C_HEADS * C_DK, 2 * C_HEADS * C_DK + C_HEADS * C_DV, C_HEADS * C_DK),
        C_DK,
        ((jnp.log(lb).reshape(1, -1), hspec), (jnp.log1p(-lb).reshape(1, -1), hspec), ((1.0 - lb).reshape(1, -1), hspec)),
        c_norm, "hgrn")

    dq = D_NOPE + D_ROPE
    wq = w_q_up.reshape(Q_LORA, D_HEADS, dq)
    wq_rope = wq[:, :, D_NOPE:]
    wq_p = jnp.concatenate([wq[:, :, :D_NOPE].reshape(Q_LORA, -1),
                            jnp.concatenate([wq_rope, wq_rope[:, :, swap]], axis=-1).reshape(Q_LORA, -1)],
                           axis=1).astype(BF16)
    wkv = w_kv_up.reshape(KV_LORA, D_HEADS, D_NOPE + D_V)
    wkv_p = jnp.concatenate([wkv[:, :, :D_NOPE].reshape(KV_LORA, -1), wkv[:, :, D_NOPE:].reshape(KV_LORA, -1)],
                            axis=1).astype(BF16)
    qn = _normproj(z2, q_a_norm, wq_p, x_col_block=c_w // Q_LORA).reshape(b, t, -1)
    kvn = _normproj(z2, kv_a_norm, wkv_p, x_col_block=(c_w + Q_LORA) // KV_LORA).reshape(b, t, -1)

    pos = jnp.arange(t, dtype=F32)
    half = D_ROPE // 2
    inv = ROPE_THETA ** (-jnp.arange(half, dtype=F32) / half)
    ang = pos[:, None] * inv[None, :]
    cos, sin = jnp.cos(ang), jnp.sin(ang)
    tab = jnp.concatenate([cos, cos, -sin, sin], axis=1)
    pair = lambda g: jnp.concatenate([g[D_NOPE:], g[D_NOPE:][swap]]).reshape(1, LANES)
    yd = _mla(qn, kvn, z, kr0, tab, q_norm[:D_NOPE].reshape(1, LANES), pair(q_norm),
              k_norm[:D_NOPE].reshape(1, LANES), pair(k_norm))
    return _outproj(yc.reshape(n, -1), yd.reshape(n, -1), w_out.astype(BF16), x.reshape(n, d)).reshape(b, t, d)


def kernel(x, meta_tokens, ab_norm, ab_w_in, a_conv_w, a_conv_b, a_b_i, a_b_f, a_head_norm, b_w_gate2, b_b_gate, b_head_norm, ab_w_out, cd_norm, cd_w_in, c_lower_bound, c_head_norm, d_q_a_norm, d_w_q_up, d_kv_a_norm, d_w_kv_up, d_q_norm, d_k_norm, cd_w_out, moe_norm, moe_w_group, moe_b_group, moe_w_expert, moe_b_expert, moe_w1, moe_w3, moe_w2):
    b = x.shape[0]
    depth = moe_norm.shape[0]
    h = jnp.concatenate([jnp.broadcast_to(meta_tokens.astype(x.dtype)[None], (b, N_META, D_MODEL)), x], axis=1)
    t = h.shape[1]
    lb_cum = jnp.cumsum(jax.nn.softmax(c_lower_bound.astype(F32), axis=0), axis=0)
    lower_bounds = lb_cum - lb_cum[0]
    for layer in range(depth):
        j = layer // 2
        if layer % 2 == 0:
            h = _even_layer(h, ab_norm[j], ab_w_in[j], a_conv_w[j], a_conv_b[j], a_b_i[j], a_b_f[j], a_head_norm[j],
                            b_w_gate2[j], b_b_gate[j], b_head_norm[j], ab_w_out[j])
        else:
            h = _odd_layer(h, lower_bounds[layer], cd_norm[j], cd_w_in[j], c_head_norm[j], d_q_a_norm[j],
                           d_w_q_up[j], d_kv_a_norm[j], d_w_kv_up[j], d_q_norm[j], d_k_norm[j], cd_w_out[j])
        h = _moe(h.reshape(b * t, D_MODEL), moe_norm[layer], moe_w_group[layer], moe_b_group[layer],
                 moe_w_expert[layer], moe_b_expert[layer], moe_w1, moe_w3, moe_w2, layer).reshape(b, t, D_MODEL)
    return h[:, N_META:]
```

```python
import functools
import math

import jax
import jax.numpy as jnp
from jax import lax
from jax.experimental import pallas as pl
from jax.experimental.pallas import tpu as pltpu

F32 = jnp.float32
BF16 = jnp.bfloat16
HIGHEST = lax.Precision.HIGHEST

D_MODEL = 2048
N_META = 16
CHUNK = 64
CONV_K = 4
EPS = 1e-6
A_HEADS, A_DK, A_DV = 4, 128, 256
B_HEADS, B_DK, B_DV = 4, 128, 256
GATE_RANK = 16
GATE_TAU = 16.0
C_HEADS, C_DK, C_DV = 8, 128, 128
D_HEADS, D_NOPE, D_ROPE, D_V = 8, 128, 64, 128
Q_LORA, KV_LORA = 512, 256
ROPE_THETA = 10000.0
N_GROUPS, EXPERTS_PER_GROUP = 4, 8
N_EXPERTS = N_GROUPS * EXPERTS_PER_GROUP
TOP_K = 2
D_EXPERT = 512

LANES = 128
MXU_DIM = 256
BF16_ROWS = 16
VMEM_LIMIT = 56 * 1024 * 1024
MOE_BLOCK = MXU_DIM
ATT_BLOCK = 256
ATT_PAD = ATT_BLOCK - N_META

_NT = (((1,), (1,)), ((), ()))
_TN = (((0,), (0,)), ((), ()))


def _dot(a, b, precision=None):
    return jnp.dot(a, b, preferred_element_type=F32, precision=precision)


def _dot_nt(a, b):
    return lax.dot_general(a, b, _NT, preferred_element_type=F32)


def _dot_tn(a, b):
    return lax.dot_general(a, b, _TN, preferred_element_type=F32)


def _bf(x):
    return x.astype(BF16)


def _split3(x):
    hi = _bf(x)
    rest = x - hi.astype(F32)
    mid = _bf(rest)
    return hi, mid, _bf(rest - mid.astype(F32))


def _log_sigmoid(x):
    return jnp.minimum(x, 0.0) - jnp.log1p(jnp.exp(-jnp.abs(x)))


def _sigmoid(x):
    return 1.0 / (1.0 + jnp.exp(-x))


def _silu(x):
    return x * _sigmoid(x)


def _row_tile(m, cap):
    best = None
    for t in range(BF16_ROWS, min(m, cap) + 1, BF16_ROWS):
        if m % t == 0:
            best = t
    assert best is not None, m
    return best


def _col_tile(n, cap):
    best = None
    for t in range(MXU_DIM, min(n, cap) + 1, MXU_DIM):
        if n % t == 0:
            best = t
    assert best is not None, n
    return best


def _params(*sem):
    return pltpu.CompilerParams(dimension_semantics=sem, vmem_limit_bytes=VMEM_LIMIT)


_RELAYOUT_ROWS = 256


def _even_weight_kernel(wa_ref, wb_ref, wc_ref, o_ref, *, n_plain, n_shift, shift, gate_cols):
    ob = pl.program_id(0)
    rows = o_ref.shape[0]
    chunks = rows // _RELAYOUT_ROWS

    @pl.when(ob < n_plain)
    def _():
        def body(c, carry):
            r = pl.ds(pl.multiple_of(c * _RELAYOUT_ROWS, _RELAYOUT_ROWS), _RELAYOUT_ROWS)
            o_ref[r, :] = _bf(wa_ref[r, :])
            return carry

        lax.fori_loop(0, chunks, body, 0)

    @pl.when((ob >= n_plain) & (ob < n_plain + n_shift))
    def _():
        def body(c, carry):
            r = pl.ds(pl.multiple_of(c * _RELAYOUT_ROWS, _RELAYOUT_ROWS), _RELAYOUT_ROWS)
            wide = jnp.concatenate([wa_ref[r, :], wb_ref[r, :]], axis=1)
            o_ref[r, :] = _bf(wide[:, shift:shift + MXU_DIM])
            return carry

        lax.fori_loop(0, chunks, body, 0)

    @pl.when(ob == n_plain + n_shift)
    def _():
        lane = lax.broadcasted_iota(jnp.int32, (_RELAYOUT_ROWS, LANES), 1)

        def body(c, carry):
            r = pl.ds(pl.multiple_of(c * _RELAYOUT_ROWS, _RELAYOUT_ROWS), _RELAYOUT_ROWS)
            first = jnp.where(lane < shift, wc_ref[r, :], jnp.where(lane < gate_cols, wb_ref[r, :], 0.0))
            o_ref[r, :] = _bf(jnp.concatenate([first, jnp.zeros_like(first)], axis=1))
            return carry

        lax.fori_loop(0, chunks, body, 0)


def _even_weight(w_in, a_w, g_w, b_w, rank):
    d = w_in.shape[0]
    assert a_w % MXU_DIM == 0 and b_w % MXU_DIM == 0 and g_w + rank <= LANES and d % _RELAYOUT_ROWS == 0
    n_plain, n_shift = a_w // MXU_DIM, b_w // MXU_DIM
    n_out = n_plain + n_shift + 1
    last = n_out - 1

    def b_index(ob):
        return (0, jnp.where(ob < n_plain, 0, jnp.where(ob < last, 2 * (ob + 1), (a_w + g_w + b_w) // LANES)))

    return pl.pallas_call(
        functools.partial(_even_weight_kernel, n_plain=n_plain, n_shift=n_shift, shift=g_w, gate_cols=g_w + rank),
        grid=(n_out,),
        in_specs=[
            pl.BlockSpec((d, MXU_DIM), lambda ob: (0, jnp.minimum(ob, last - 1))),
            pl.BlockSpec((d, LANES), b_index),
            pl.BlockSpec((d, LANES), lambda ob: (0, a_w // LANES)),
        ],
        out_specs=pl.BlockSpec((d, MXU_DIM), lambda ob: (0, ob)),
        out_shape=jax.ShapeDtypeStruct((d, n_out * MXU_DIM), BF16),
        compiler_params=_params("parallel"),
        name="even_weight",
    )(w_in, w_in, w_in)


def _odd_weight_kernel(w_ref, o_ref, *, n_plain):
    ob = pl.program_id(0)
    chunks = o_ref.shape[0] // _RELAYOUT_ROWS
    half = D_ROPE // 2

    def body(c, carry):
        r = pl.ds(pl.multiple_of(c * _RELAYOUT_ROWS, _RELAYOUT_ROWS), _RELAYOUT_ROWS)
        w = w_ref[r, :]

        @pl.when(ob < n_plain)
        def _():
            o_ref[r, :] = _bf(w)

        @pl.when(ob == n_plain)
        def _():
            pair = jnp.concatenate([w[:, :D_ROPE], w[:, half:D_ROPE], w[:, :half]], axis=1)
            o_ref[r, :] = _bf(jnp.concatenate([pair, jnp.zeros_like(pair)], axis=1))

        return carry

    lax.fori_loop(0, chunks, body, 0)


def _odd_weight(w_in, kr0):
    d = w_in.shape[0]
    assert kr0 % MXU_DIM == 0 and w_in.shape[1] == kr0 + D_ROPE and d % _RELAYOUT_ROWS == 0
    n_plain = kr0 // MXU_DIM
    return pl.pallas_call(
        functools.partial(_odd_weight_kernel, n_plain=n_plain),
        grid=(n_plain + 1,),
        in_specs=[pl.BlockSpec((d, MXU_DIM), lambda ob: (0, ob))],
        out_specs=pl.BlockSpec((d, MXU_DIM), lambda ob: (0, ob)),
        out_shape=jax.ShapeDtypeStruct((d, kr0 + MXU_DIM), BF16),
        compiler_params=_params("parallel"),
        name="odd_weight",
    )(w_in)


def _normproj_kernel(x_ref, g_ref, w_ref, o_ref, xs_ref):
    tm = xs_ref.shape[0]

    @pl.when(pl.program_id(1) == 0)
    def _():
        def body(c, carry):
            r0 = pl.multiple_of(c * BF16_ROWS, BF16_ROWS)
            x = x_ref[pl.ds(r0, BF16_ROWS), :]
            ms = jnp.mean(x * x, axis=-1, keepdims=True)
            xs_ref[pl.ds(r0, BF16_ROWS), :] = _bf(x * lax.rsqrt(ms + EPS) * g_ref[...])
            return carry

        lax.fori_loop(0, tm // BF16_ROWS, body, 0)

    o_ref[...] = _dot(xs_ref[...], w_ref[...]).astype(o_ref.dtype)


def _normproj(x2d, gain, w, *, x_col_block=0, out_dtype=F32):
    m = x2d.shape[0]
    k, n = w.shape
    tm = _row_tile(m, 688)
    tn = _col_tile(n, 1280)
    return pl.pallas_call(
        _normproj_kernel,
        grid=(m // tm, n // tn),
        in_specs=[
            pl.BlockSpec((tm, k), lambda i, j: (i, x_col_block)),
            pl.BlockSpec((1, k), lambda i, j: (0, 0)),
            pl.BlockSpec((k, tn), lambda i, j: (0, j)),
        ],
        out_specs=pl.BlockSpec((tm, tn), lambda i, j: (i, j)),
        out_shape=jax.ShapeDtypeStruct((m, n), out_dtype),
        scratch_shapes=[pltpu.VMEM((tm, k), BF16)],
        compiler_params=_params("parallel", "arbitrary"),
        name="normproj",
    )(x2d, gain.reshape(1, k).astype(F32), w)


def _outproj_kernel(ya_ref, yb_ref, w_ref, r_ref, o_ref):
    ka = ya_ref.shape[1]
    acc = _dot(ya_ref[...], w_ref[:ka, :]) + _dot(yb_ref[...], w_ref[ka:, :])
    o_ref[...] = r_ref[...] + acc


def _outproj(ya, yb, w, res):
    m, ka = ya.shape
    kb = yb.shape[1]
    n = w.shape[1]
    tm = _row_tile(m, 1376)
    tn = _col_tile(n, 1024)
    return pl.pallas_call(
        _outproj_kernel,
        grid=(m // tm, n // tn),
        in_specs=[
            pl.BlockSpec((tm, ka), lambda i, j: (i, 0)),
            pl.BlockSpec((tm, kb), lambda i, j: (i, 0)),
            pl.BlockSpec((ka + kb, tn), lambda i, j: (0, j)),
            pl.BlockSpec((tm, tn), lambda i, j: (i, j)),
        ],
        out_specs=pl.BlockSpec((tm, tn), lambda i, j: (i, j)),
        out_shape=jax.ShapeDtypeStruct((m, n), F32),
        compiler_params=_params("parallel", "arbitrary"),
        name="outproj",
    )(ya, yb, w, res)


def _mlstm_kernel(bi_ref, bf_ref, q_ref, k_ref, v_ref, og_ref, gt_ref,
                  cwq_ref, cwk_ref, cbq_ref, cbk_ref, hn_ref, o_ref, c_ref, n_ref, m_ref, *, heads):
    t_total = q_ref.shape[1]
    n_chunks = (t_total - N_META) // CHUNK
    head0 = pl.program_id(1) * heads

    c_ref[...] = jnp.zeros_like(c_ref)
    n_ref[...] = jnp.zeros_like(n_ref)
    m_ref[...] = jnp.zeros_like(m_ref)

    def conv(win, cw, cb, length):
        y = cb
        for j in range(CONV_K):
            y = y + win[8 - (CONV_K - 1) + j:8 - (CONV_K - 1) + j + length, :] * cw[j:j + 1, :]
        return _silu(y)

    hs = range(heads)
    kcol = [slice(hh * A_DK, (hh + 1) * A_DK) for hh in hs]
    vcol = [slice(hh * A_DV, (hh + 1) * A_DV) for hh in hs]

    def gates(hh, blk, blk_parts, length, causal, upper):
        head = head0 + hh
        b_i = bi_ref[head]
        b_f = bf_ref[head]
        lane = lax.broadcasted_iota(jnp.int32, (length, LANES), 1)
        ig_c = jnp.sum(jnp.where(lane == head, blk, 0.0), axis=1, keepdims=True) + b_i
        lf_c = _log_sigmoid(jnp.sum(jnp.where(lane == A_HEADS + head, blk, 0.0), axis=1, keepdims=True) + b_f)
        sel_r = lax.broadcasted_iota(jnp.int32, (8, LANES), 0)
        sel_l = lax.broadcasted_iota(jnp.int32, (8, LANES), 1)
        sel = jnp.where(sel_l == head + A_HEADS * sel_r, 1.0, 0.0).astype(BF16)
        rows = _dot_nt(sel, blk_parts[0]) + (_dot_nt(sel, blk_parts[1]) + _dot_nt(sel, blk_parts[2]))
        ig_r = rows[0:1, :] + b_i
        lf_r = _log_sigmoid(rows[1:2, :] + b_f)
        b_c = jnp.sum(jnp.where(causal, lf_r, 0.0), axis=1, keepdims=True)
        b_r = jnp.sum(jnp.where(upper, lf_c, 0.0), axis=0, keepdims=True)
        b_end = b_c[length - 1:length, :]
        w_end = b_end - b_c + ig_c
        m_loc = jnp.max(w_end, axis=0, keepdims=True)
        d = jnp.where(causal, b_c - b_r + ig_r, -jnp.inf)
        return b_c, b_end, jnp.exp(w_end - m_loc), m_loc, d, jnp.max(d, axis=1, keepdims=True)

    def chunk(o, length, qwin, kwin):
        row = lax.broadcasted_iota(jnp.int32, (length, length), 0)
        col = lax.broadcasted_iota(jnp.int32, (length, length), 1)
        causal = col <= row
        blk = gt_ref[0, pl.ds(o, length), 0:LANES]
        blk_parts = _split3(blk)
        gt = [gates(hh, blk, blk_parts, length, causal, row <= col) for hh in hs]
        q = [conv(qwin[hh], cwq_ref[:, kcol[hh]], cbq_ref[:, kcol[hh]], length) for hh in hs]
        k = [conv(kwin[hh], cwk_ref[:, kcol[hh]], cbk_ref[:, kcol[hh]], length) * (A_DK ** -0.5) for hh in hs]
        vb = [_bf(v_ref[0, pl.ds(o, length), vcol[hh]]) for hh in hs]
        qb = [_bf(q[hh]) for hh in hs]
        k_w = [k[hh] * gt[hh][2] for hh in hs]
        qk = [_dot_nt(qb[hh], _bf(k[hh])) for hh in hs]
        c_in = [c_ref[hh] for hh in hs]
        q_c = [_dot(qb[hh], _bf(c_in[hh])) for hh in hs]
        c_loc = [_dot_tn(_bf(k_w[hh]), vb[hh]) for hh in hs]
        s, a_t, m_t, q_n = [], [], [], []
        for hh in hs:
            b_c, b_end, _, m_loc, d, d_max = gt[hh]
            m_in = m_ref[hh]
            inter = b_c + m_in
            m_t.append(jnp.maximum(inter, d_max))
            s.append(qk[hh] * jnp.exp(d - m_t[hh]))
            a_t.append(jnp.exp(inter - m_t[hh]))
            m_new = jnp.maximum(b_end + m_in, m_loc)
            a = jnp.exp(b_end + m_in - m_new)
            c = jnp.exp(m_loc - m_new)
            n_in = n_ref[hh]
            c_ref[hh] = a * c_in[hh] + c * c_loc[hh]
            n_ref[hh] = a * n_in + c * jnp.sum(k_w[hh], axis=0, keepdims=True)
            m_ref[hh] = m_new
            q_n.append(jnp.sum(q[hh] * n_in, axis=1, keepdims=True))
        num = [_dot(_bf(s[hh]), vb[hh]) + a_t[hh] * q_c[hh] for hh in hs]
        for hh in hs:
            den = jnp.sum(s[hh], axis=1, keepdims=True) + a_t[hh] * q_n[hh]
            h = num[hh] / jnp.maximum(jnp.abs(den), jnp.exp(-m_t[hh]))
            hn = h * lax.rsqrt(jnp.mean(h * h, axis=-1, keepdims=True) + EPS) * hn_ref[:, vcol[hh]]
            y = _sigmoid(og_ref[0, pl.ds(o, length), vcol[hh]]) * hn
            o_ref[0, pl.ds(o, length), vcol[hh]] = y.astype(o_ref.dtype)

    zeros8 = jnp.zeros((8, A_DK), F32)
    chunk(0, N_META,
          [jnp.concatenate([zeros8, q_ref[0, 0:N_META, kcol[hh]]], axis=0) for hh in hs],
          [jnp.concatenate([zeros8, k_ref[0, 0:N_META, kcol[hh]]], axis=0) for hh in hs])

    def body(c, carry):
        o = pl.multiple_of(N_META + c * CHUNK, BF16_ROWS)
        w0 = pl.multiple_of(N_META - 8 + c * CHUNK, 8)
        chunk(o, CHUNK, [q_ref[0, pl.ds(w0, CHUNK + 8), kcol[hh]] for hh in hs],
              [k_ref[0, pl.ds(w0, CHUNK + 8), kcol[hh]] for hh in hs])
        return carry

    lax.fori_loop(0, n_chunks, body, 0)


def _mlstm(z, gate_col, conv_w, conv_b, b_i, b_f, head_norm, heads=2):
    b, t, _ = z.shape
    hk = A_HEADS * A_DK
    wk, wv = heads * A_DK, heads * A_DV
    smem = pl.BlockSpec(memory_space=pltpu.SMEM)
    col = lambda width, off: (lambda bi, g: (bi, 0, off // width + g))
    return pl.pallas_call(
        functools.partial(_mlstm_kernel, heads=heads),
        grid=(b, A_HEADS // heads),
        in_specs=[
            smem, smem,
            pl.BlockSpec((1, t, wk), col(wk, 0)),
            pl.BlockSpec((1, t, wk), col(wk, hk)),
            pl.BlockSpec((1, t, wv), col(wv, 2 * hk)),
            pl.BlockSpec((1, t, wv), col(wv, 2 * hk + A_HEADS * A_DV)),
            pl.BlockSpec((1, t, MXU_DIM), lambda bi, g: (bi, 0, gate_col // MXU_DIM)),
            pl.BlockSpec((CONV_K, wk), lambda bi, g: (0, g)),
            pl.BlockSpec((CONV_K, wk), lambda bi, g: (0, A_HEADS // heads + g)),
            pl.BlockSpec((1, wk), lambda bi, g: (0, g)),
            pl.BlockSpec((1, wk), lambda bi, g: (0, A_HEADS // heads + g)),
            pl.BlockSpec((1, wv), lambda bi, g: (0, g)),
        ],
        out_specs=pl.BlockSpec((1, t, wv), lambda bi, g: (bi, 0, g)),
        out_shape=jax.ShapeDtypeStruct((b, t, A_HEADS * A_DV), BF16),
        scratch_shapes=[pltpu.VMEM((heads, A_DK, A_DV), F32), pltpu.VMEM((heads, 1, A_DK), F32),
                        pltpu.VMEM((heads, 1, 1), F32)],
        compiler_params=_params("parallel", "parallel"),
        name="mlstm",
    )(b_i, b_f, z, z, z, z, z, conv_w, conv_w, conv_b.reshape(1, -1), conv_b.reshape(1, -1),
      head_norm.reshape(1, -1))


def _gla_kernel(q_ref, k_ref, v_ref, og_ref, g_ref, p0_ref, p1_ref, p2_ref, hn_ref, o_ref, st_ref, *, mode, heads):
    t_total = q_ref.shape[1]
    n_chunks = (t_total - N_META) // CHUNK
    dv, dk = st_ref.shape[1:]
    st_ref[...] = jnp.zeros_like(st_ref)

    hs = range(heads)
    kcol = [slice(hh * dk, (hh + 1) * dk) for hh in hs]
    vcol = [slice(hh * dv, (hh + 1) * dv) for hh in hs]

    def gate_inputs(hh, o, length):
        q = q_ref[0, pl.ds(o, length), kcol[hh]]
        if mode == "gla":
            pre = _dot(g_ref[0, pl.ds(o, length), :], p0_ref[hh], precision=HIGHEST) + p1_ref[:, kcol[hh]]
            return q * (dk ** -0.5), k_ref[0, pl.ds(o, length), kcol[hh]], _log_sigmoid(pre) / GATE_TAU
        fpre = g_ref[0, pl.ds(o, length), kcol[hh]]
        a = p0_ref[:, kcol[hh]]
        bb = p1_ref[:, kcol[hh]] + _log_sigmoid(fpre)
        lg = jnp.maximum(a, bb) + jnp.log1p(jnp.exp(-jnp.abs(a - bb)))
        return q, p2_ref[:, kcol[hh]] * _sigmoid(-fpre), lg

    def cumsum_time(tri, lg):
        parts = _dot(tri, jnp.concatenate(_split3(lg), axis=1))
        return parts[:, :dk] + (parts[:, dk:2 * dk] + parts[:, 2 * dk:])

    def chunk(o, length):
        row = lax.broadcasted_iota(jnp.int32, (length, length), 0)
        col = lax.broadcasted_iota(jnp.int32, (length, length), 1)
        causal = col <= row
        tri = jnp.where(causal, 1.0, 0.0).astype(BF16)
        qkl = [gate_inputs(hh, o, length) for hh in hs]
        vb = [_bf(v_ref[0, pl.ds(o, length), vcol[hh]]) for hh in hs]
        g = [cumsum_time(tri, qkl[hh][2]) for hh in hs]
        g_end = [g[hh][length - 1:length, :] for hh in hs]
        g_mid = [g[hh][length // 2:length // 2 + 1, :] for hh in hs]
        s = [_dot_nt(_bf(qkl[hh][0] * jnp.exp(g[hh] - g_mid[hh])), _bf(qkl[hh][1] * jnp.exp(g_mid[hh] - g[hh])))
             for hh in hs]
        st_in = [st_ref[hh] for hh in hs]
        inter = [_dot_nt(_bf(qkl[hh][0] * jnp.exp(g[hh])), _bf(st_in[hh])) for hh in hs]
        local = [_dot_tn(vb[hh], _bf(qkl[hh][1] * jnp.exp(g_end[hh] - g[hh]))) for hh in hs]
        for hh in hs:
            st_ref[hh] = st_in[hh] * jnp.exp(g_end[hh]) + local[hh]
        out = [_dot(_bf(jnp.where(causal, s[hh], 0.0)), vb[hh]) + inter[hh] for hh in hs]
        for hh in hs:
            hn = out[hh] * lax.rsqrt(jnp.mean(out[hh] * out[hh], axis=-1, keepdims=True) + EPS) * hn_ref[:, vcol[hh]]
            og = og_ref[0, pl.ds(o, length), vcol[hh]]
            gate = _silu(og) if mode == "gla" else _sigmoid(og)
            o_ref[0, pl.ds(o, length), vcol[hh]] = (gate * hn).astype(o_ref.dtype)

    chunk(0, N_META)

    def body(c, carry):
        chunk(pl.multiple_of(N_META + c * CHUNK, BF16_ROWS), CHUNK)
        return carry

    lax.fori_loop(0, n_chunks, body, 0)


def _gla_call(z, n_heads, heads, dk, dv, blocks, gate_width, params, head_norm, mode):
    b, t, _ = z.shape
    q0, k0, v0, og0, g0 = blocks
    zspec = lambda width, off, grouped=True: pl.BlockSpec(
        (1, t, width), (lambda bi, g: (bi, 0, off // width + (g if grouped else 0))))
    (p0, s0), (p1, s1), (p2, s2) = params
    gate_spec = zspec(gate_width, g0, grouped=False) if mode == "gla" else zspec(heads * dk, g0)
    return pl.pallas_call(
        functools.partial(_gla_kernel, mode=mode, heads=heads),
        grid=(b, n_heads // heads),
        in_specs=[
            zspec(heads * dk, q0), zspec(heads * dk, k0), zspec(heads * dv, v0), zspec(heads * dv, og0),
            gate_spec, s0, s1, s2,
            pl.BlockSpec((1, heads * dv), lambda bi, g: (0, g)),
        ],
        out_specs=pl.BlockSpec((1, t, heads * dv), lambda bi, g: (bi, 0, g)),
        out_shape=jax.ShapeDtypeStruct((b, t, n_heads * dv), BF16),
        scratch_shapes=[pltpu.VMEM((heads, dv, dk), F32)],
        compiler_params=_params("parallel", "parallel"),
        name="gla_" + mode,
    )(z, z, z, z, z, p0, p1, p2, head_norm.reshape(1, -1))


def _mla_kernel(qn_ref, qr_ref, kn_ref, v_ref, kr_ref, tab_ref, gqn_ref, gqr_ref, gkn_ref, gkr_ref,
                o_ref, qf_ref, kf_ref, vf_ref):
    t_total = qn_ref.shape[1]
    n_blocks = (ATT_PAD + t_total) // ATT_BLOCK
    dqk = D_NOPE + D_ROPE
    scale = dqk ** -0.5
    rows = _row_tile(t_total, 768)
    lane = lax.broadcasted_iota(jnp.int32, (rows, LANES), 1)
    first_half = lane < D_ROPE

    qf_ref[0:ATT_PAD, :] = jnp.zeros((ATT_PAD, 2 * LANES), BF16)
    kf_ref[0:ATT_PAD, :] = jnp.zeros((ATT_PAD, 2 * LANES), BF16)
    vf_ref[0:ATT_PAD, :] = jnp.zeros((ATT_PAD, D_V), BF16)

    def rope_pair(x, gains, tab):
        p = x * gains * tab
        return p + pltpu.roll(p, D_ROPE, 1)

    def prep(c):
        r0 = c * rows
        dst = ATT_PAD + c * rows
        tab = tab_ref[pl.ds(r0, rows), :]
        qn = qn_ref[0, pl.ds(r0, rows), :]
        qr = qr_ref[0, pl.ds(r0, rows), :]
        ssq = jnp.sum(qn * qn, axis=-1, keepdims=True) + 0.5 * jnp.sum(qr * qr, axis=-1, keepdims=True)
        rq = lax.rsqrt(ssq / dqk + EPS) * scale
        qf_ref[pl.ds(dst, rows), 0:LANES] = _bf(qn * gqn_ref[...] * rq)
        qf_ref[pl.ds(dst, rows), LANES:2 * LANES] = _bf(rope_pair(qr, gqr_ref[...], tab) * rq)
        kn = kn_ref[0, pl.ds(r0, rows), :]
        kr = kr_ref[0, pl.ds(r0, rows), :]
        ssk = jnp.sum(kn * kn, axis=-1, keepdims=True) + 0.5 * jnp.sum(kr * kr, axis=-1, keepdims=True)
        rk = lax.rsqrt(ssk / dqk + EPS)
        kf_ref[pl.ds(dst, rows), 0:LANES] = _bf(kn * gkn_ref[...] * rk)
        kf_ref[pl.ds(dst, rows), LANES:2 * LANES] = _bf(
            jnp.where(first_half, rope_pair(kr, gkr_ref[...], tab) * rk, 0.0))
        vf_ref[pl.ds(dst, rows), :] = _bf(v_ref[0, pl.ds(r0, rows), :])

    for c in range(t_total // rows):
        prep(c)

    qpos = lax.broadcasted_iota(jnp.int32, (ATT_BLOCK, ATT_BLOCK), 0)
    kpos = lax.broadcasted_iota(jnp.int32, (ATT_BLOCK, ATT_BLOCK), 1)
    neg = -jnp.inf

    for qi in range(n_blocks):
        q = qf_ref[qi * ATT_BLOCK:(qi + 1) * ATT_BLOCK, :]
        s = _dot_nt(q, kf_ref[0:(qi + 1) * ATT_BLOCK, :])
        parts = [s[:, j * ATT_BLOCK:(j + 1) * ATT_BLOCK] for j in range(qi + 1)]
        parts[0] = jnp.where(kpos >= ATT_PAD, parts[0], neg)
        parts[qi] = jnp.where(kpos <= qpos, parts[qi], neg)
        top = functools.reduce(jnp.maximum, parts)
        m = jnp.max(top, axis=-1, keepdims=True)
        if qi == 0:
            m = jnp.where(m == neg, 0.0, m)
        probs = [jnp.exp(part - m) for part in parts]
        l = jnp.sum(functools.reduce(jnp.add, probs), axis=-1, keepdims=True)
        pv = _dot(jnp.concatenate([_bf(pr) for pr in probs], axis=1), vf_ref[0:(qi + 1) * ATT_BLOCK, :])
        if qi == 0:
            out = pv / jnp.where(l == 0.0, 1.0, l)
            o_ref[0, 0:N_META, :] = out[ATT_PAD:, :].astype(o_ref.dtype)
        else:
            dst = qi * ATT_BLOCK - ATT_PAD
            o_ref[0, dst:dst + ATT_BLOCK, :] = (pv / l).astype(o_ref.dtype)


def _mla(qn, kvn, z, kr_col, tab, gqn, gqr, gkn, gkr):
    b, t, _ = qn.shape
    assert (ATT_PAD + t) % ATT_BLOCK == 0 and t % (3 * BF16_ROWS) == 0
    tp = ATT_PAD + t
    hspec = lambda off: pl.BlockSpec((1, t, LANES), lambda bi, h: (bi, 0, off + h))
    gspec = pl.BlockSpec((1, LANES), lambda bi, h: (0, 0))
    return pl.pallas_call(
        _mla_kernel,
        grid=(b, D_HEADS),
        in_specs=[
            hspec(0), hspec(D_HEADS), hspec(0), hspec(D_HEADS),
            pl.BlockSpec((1, t, LANES), lambda bi, h: (bi, 0, kr_col // LANES)),
            pl.BlockSpec((t, LANES), lambda bi, h: (0, 0)),
            gspec, gspec, gspec, gspec,
        ],
        out_specs=pl.BlockSpec((1, t, D_V), lambda bi, h: (bi, 0, h)),
        out_shape=jax.ShapeDtypeStruct((b, t, D_HEADS * D_V), BF16),
        scratch_shapes=[pltpu.VMEM((tp, 2 * LANES), BF16), pltpu.VMEM((tp, 2 * LANES), BF16),
                        pltpu.VMEM((tp, D_V), BF16)],
        compiler_params=_params("parallel", "parallel"),
        name="mla",
    )(qn, qn, kvn, kvn, z, tab, gqn, gqr, gkn, gkr)


def _pack_bf16_pairs(v):
    w = v.shape[1] // 2
    bits = pltpu.bitcast(_bf(v).astype(F32), jnp.uint32)
    return (bits[:, :w] >> 16) | (bits[:, w:] & jnp.uint32(0xFFFF0000))


def _unpack_lo(words):
    return pltpu.bitcast(words << 16, F32)


def _unpack_hi(words):
    return pltpu.bitcast(words & jnp.uint32(0xFFFF0000), F32)


def _router_kernel(x_ref, g_ref, w_ref, b_ref, gate_ref, idx_ref, xg_ref, cnt_ref, carry_ref):
    tm = x_ref.shape[0]

    @pl.when(pl.program_id(0) == 0)
    def _():
        carry_ref[...] = jnp.zeros_like(carry_ref)

    x = x_ref[...]
    ms = jnp.mean(x * x, axis=-1, keepdims=True)
    xn = x * lax.rsqrt(ms + EPS) * g_ref[...]
    logits = _dot(xn, w_ref[...], precision=HIGHEST) + b_ref[...]
    lane = lax.broadcasted_iota(jnp.int32, logits.shape, 1)
    lane_f = lane.astype(F32)
    neg = -jnp.inf
    big = float(LANES)

    is_group = lane < N_GROUPS
    g_max = jnp.max(jnp.where(is_group, logits, neg), axis=-1, keepdims=True)
    g_sum = jnp.sum(jnp.where(is_group, jnp.exp(logits - g_max), 0.0), axis=-1, keepdims=True)
    p_top = 1.0 / g_sum
    grp = jnp.min(jnp.where(is_group & (logits == g_max), lane_f, big), axis=-1, keepdims=True)

    e_lo = N_GROUPS + grp * EXPERTS_PER_GROUP
    in_grp = (lane_f >= e_lo) & (lane_f < e_lo + EXPERTS_PER_GROUP)
    e_max = jnp.max(jnp.where(in_grp, logits, neg), axis=-1, keepdims=True)
    e_sum = jnp.sum(jnp.where(in_grp, jnp.exp(logits - e_max), 0.0), axis=-1, keepdims=True)
    i1 = jnp.min(jnp.where(in_grp & (logits == e_max), lane_f, big), axis=-1, keepdims=True)
    rest = in_grp & (lane_f != i1)
    e_2nd = jnp.max(jnp.where(rest, logits, neg), axis=-1, keepdims=True)
    i2 = jnp.min(jnp.where(rest & (logits == e_2nd), lane_f, big), axis=-1, keepdims=True)
    p1 = 1.0 / e_sum
    p2 = jnp.exp(e_2nd - e_max) / e_sum
    tot = p1 + p2
    gate_ref[...] = jnp.where(lane == 0, p_top * p1 / tot, jnp.where(lane == 1, p_top * p2 / tot, 0.0))

    e1 = i1 - N_GROUPS
    e2 = i2 - N_GROUPS
    hot = jnp.where((lane_f == e1) | (lane_f == e2), 1.0, 0.0)
    row = lax.broadcasted_iota(jnp.int32, (tm, tm), 0)
    col = lax.broadcasted_iota(jnp.int32, (tm, tm), 1)
    before = _dot(jnp.where(col < row, 1.0, 0.0).astype(BF16), _bf(hot)) + carry_ref[...]
    r1 = jnp.sum(jnp.where(lane_f == e1, before, 0.0), axis=-1, keepdims=True)
    r2 = jnp.sum(jnp.where(lane_f == e2, before, 0.0), axis=-1, keepdims=True)
    total = carry_ref[...] + jnp.sum(hot, axis=0, keepdims=True)
    carry_ref[...] = total
    cnt_ref[...] = jnp.broadcast_to(total, cnt_ref.shape).astype(jnp.int32)
    idx_ref[...] = jnp.where(lane == 0, e1, jnp.where(lane == 1, e2, jnp.where(lane == 2, r1, jnp.where(
        lane == 3, r2, 0.0)))).astype(jnp.int32)

    words = _pack_bf16_pairs(xn)
    for s in range(8):
        xg_ref[pl.ds(s, tm, stride=8), :] = words[:, s * LANES:(s + 1) * LANES]


def _router(x2d, gain, w_group, b_group, w_expert, b_expert):
    m, d = x2d.shape
    assert d == 2 * 8 * LANES
    tm = _row_tile(m, 688)
    pad = LANES - N_GROUPS - N_EXPERTS
    w = jnp.concatenate([w_group, w_expert, jnp.zeros((d, pad), F32)], axis=1)
    bias = jnp.concatenate([b_group, b_expert, jnp.zeros((pad,), F32)]).reshape(1, LANES)
    return pl.pallas_call(
        _router_kernel,
        grid=(m // tm,),
        in_specs=[
            pl.BlockSpec((tm, d), lambda i: (i, 0)),
            pl.BlockSpec((1, d), lambda i: (0, 0)),
            pl.BlockSpec((d, LANES), lambda i: (0, 0)),
            pl.BlockSpec((1, LANES), lambda i: (0, 0)),
        ],
        out_specs=[pl.BlockSpec((tm, LANES), lambda i: (i, 0)), pl.BlockSpec((tm, LANES), lambda i: (i, 0)),
                   pl.BlockSpec((tm * 8, LANES), lambda i: (i, 0)), pl.BlockSpec((8, LANES), lambda i: (0, 0))],
        out_shape=[jax.ShapeDtypeStruct((m, LANES), F32), jax.ShapeDtypeStruct((m, LANES), jnp.int32),
                   jax.ShapeDtypeStruct((m * 8, LANES), jnp.uint32), jax.ShapeDtypeStruct((8, LANES), jnp.int32)],
        scratch_shapes=[pltpu.VMEM((1, LANES), F32)],
        compiler_params=_params("arbitrary"),
        name="router",
    )(x2d, gain.reshape(1, d), w, bias)


def _invert_kernel(dest_ref, inv_ref):
    def clear(s, carry):
        inv_ref[s] = -1
        return carry

    lax.fori_loop(0, inv_ref.shape[0], clear, 0, unroll=8)

    def put(f, carry):
        inv_ref[dest_ref[f]] = f
        return carry

    lax.fori_loop(0, dest_ref.shape[0], put, 0, unroll=8)


def _invert(dest, p):
    assert p % 8 == 0 and dest.shape[0] % 8 == 0
    smem = pl.BlockSpec(memory_space=pltpu.SMEM)
    return pl.pallas_call(
        _invert_kernel, in_specs=[smem], out_specs=smem,
        out_shape=jax.ShapeDtypeStruct((p,), jnp.int32), name="moe_invert",
    )(dest)


def _expert_kernel(be_ref, nu_ref, src_ref, dst_ref, xg_hbm, w1_ref, w3_ref, w2_ref, o_hbm,
                   xbuf, ybuf, xs_ref, w1s, w3s, w2s, sem_in, sem_out):
    i = pl.program_id(0)
    n_used = nu_ref[0]
    par = i % 2
    half = D_MODEL // 2
    half_e = D_EXPERT // 2

    def gather(block, r, slot):
        src = pl.multiple_of(src_ref[block * MOE_BLOCK + r], 8)
        return pltpu.make_async_copy(xg_hbm.at[pl.ds(src, 8), :], xbuf.at[slot, pl.ds(r * 8, 8), :], sem_in.at[slot])

    def scatter(block, r, slot):
        dst = pl.multiple_of(dst_ref[(block + 1) * MOE_BLOCK + r], 8)
        return pltpu.make_async_copy(ybuf.at[slot, pl.ds(r * 8, 8), :], o_hbm.at[pl.ds(dst, 8), :], sem_out.at[slot])

    def wait_gathers(slot):
        pltpu.make_async_copy(xbuf.at[1 - slot], xbuf.at[slot], sem_in.at[slot]).wait()

    def wait_scatters(slot):
        pltpu.make_async_copy(ybuf.at[slot], ybuf.at[1 - slot], sem_out.at[slot]).wait()

    @pl.when(i == 0)
    def _():
        ybuf[1] = jnp.zeros(ybuf.shape[1:], ybuf.dtype)
        n_real = o_hbm.shape[0] - 2 * MOE_BLOCK * 8
        fill1 = pltpu.make_async_copy(ybuf.at[1], o_hbm.at[pl.ds(n_real + MOE_BLOCK * 8, MOE_BLOCK * 8), :],
                                      sem_out.at[1])
        fill1.start()
        fill1.wait()
        pltpu.make_async_copy(ybuf.at[1], o_hbm.at[pl.ds(n_real, MOE_BLOCK * 8), :], sem_out.at[0]).start()

        def first(r, carry):
            gather(0, r, 0).start()
            return carry

        lax.fori_loop(0, MOE_BLOCK, first, 0, unroll=8)

    @pl.when(i < n_used)
    def _():
        @pl.when((i == 0) | (be_ref[i] != be_ref[jnp.maximum(i - 1, 0)]))
        def _():
            w1s[...] = _bf(w1_ref[0, 0])
            w3s[...] = _bf(w3_ref[0, 0])
            w2s[...] = _bf(w2_ref[0, 0])

        wait_gathers(par)

        def move_rows(group, n_groups=6):
            lo, hi = group * MOE_BLOCK // n_groups, (group + 1) * MOE_BLOCK // n_groups
            for r in range(lo, hi):
                gather(i + 1, r, 1 - par).start(priority=r % 2)
                scatter(i - 1, r, 1 - par).start(priority=(r + 1) % 2)

        for s in range(8):
            words = xbuf[par, pl.ds(s, MOE_BLOCK, stride=8), :]
            xs_ref[:, s * LANES:(s + 1) * LANES] = _bf(_unpack_lo(words))
            xs_ref[:, half + s * LANES:half + (s + 1) * LANES] = _bf(_unpack_hi(words))
        move_rows(0)
        xb = xs_ref[...]
        h1a = _dot(xb, w1s[:, :half_e])
        move_rows(1)
        h3a = _dot(xb, w3s[:, :half_e])
        move_rows(2)
        act_a = _bf(_silu(h1a) * h3a)
        h1b = _dot(xb, w1s[:, half_e:])
        move_rows(3)
        h3b = _dot(xb, w3s[:, half_e:])
        move_rows(4)
        act_b = _bf(_silu(h1b) * h3b)
        y = _dot(act_a, w2s[:half_e, :])
        move_rows(5)
        y = y + _dot(act_b, w2s[half_e:, :])

        wait_scatters(par)
        words = _pack_bf16_pairs(y)
        for s in range(8):
            ybuf[par, pl.ds(s, MOE_BLOCK, stride=8), :] = words[:, s * LANES:(s + 1) * LANES]

        @pl.when(i == n_used - 1)
        def _():
            def last(r, carry):
                scatter(i, r, par).start()
                return carry

            lax.fori_loop(0, MOE_BLOCK, last, 0, unroll=8)
            wait_scatters(par)
            wait_scatters(1 - par)
            wait_gathers(1 - par)


def _moe(x2d, gain, w_group, b_group, w_expert, b_expert, w1, w3, w2, layer):
    n, d = x2d.shape
    gates_l, idx_l, xg, cnt = _router(x2d, gain, w_group, b_group, w_expert, b_expert)

    a = n * TOP_K
    n_blocks = -(-a // MOE_BLOCK) + N_EXPERTS
    p = n_blocks * MOE_BLOCK
    counts = cnt[0, :N_EXPERTS]
    padded = (counts + MOE_BLOCK - 1) // MOE_BLOCK * MOE_BLOCK
    pad_end = jnp.cumsum(padded)
    pad_start = pad_end - padded
    e_hot = idx_l[:, :TOP_K, None] == jnp.arange(N_EXPERTS, dtype=jnp.int32)
    dest = jnp.sum(jnp.where(e_hot, pad_start, 0), axis=-1) + idx_l[:, TOP_K:2 * TOP_K]
    dest = jnp.clip(dest.reshape(-1), 0, p - 1).astype(jnp.int32)
    blk0 = jnp.arange(n_blocks, dtype=jnp.int32) * MOE_BLOCK
    block_expert = jnp.minimum(jnp.searchsorted(pad_end, blk0, side="right"), N_EXPERTS - 1).astype(jnp.int32)
    n_used = (pad_end[-1] // MOE_BLOCK).astype(jnp.int32).reshape(1)
    codes = _invert(dest, p)
    slot = jnp.arange(p, dtype=jnp.int32)
    spare = TOP_K * n + (slot // MOE_BLOCK % 2) * MOE_BLOCK + slot % MOE_BLOCK
    src_tok = (jnp.maximum(codes, 0) >> 1) * 8
    dst_row = jnp.where(codes >= 0, (codes & 1) * n + (codes >> 1), spare) * 8
    lead = (TOP_K * n + MOE_BLOCK + jnp.arange(MOE_BLOCK, dtype=jnp.int32)) * 8
    dst_row = jnp.concatenate([lead, dst_row])

    wspec = lambda shape: pl.BlockSpec((1, 1) + shape, lambda i, be, nu, src, dst: (layer, be[i], 0, 0))
    out_rows = TOP_K * n + 2 * MOE_BLOCK
    out2 = pl.pallas_call(
        _expert_kernel,
        grid_spec=pltpu.PrefetchScalarGridSpec(
            num_scalar_prefetch=4,
            grid=(n_blocks,),
            in_specs=[
                pl.BlockSpec(memory_space=pl.ANY),
                wspec((d, D_EXPERT)), wspec((d, D_EXPERT)), wspec((D_EXPERT, d)),
            ],
            out_specs=pl.BlockSpec(memory_space=pl.ANY),
            scratch_shapes=[
                pltpu.VMEM((2, MOE_BLOCK * 8, LANES), jnp.uint32), pltpu.VMEM((2, MOE_BLOCK * 8, LANES), jnp.uint32),
                pltpu.VMEM((MOE_BLOCK, d), BF16),
                pltpu.VMEM((d, D_EXPERT), BF16), pltpu.VMEM((d, D_EXPERT), BF16), pltpu.VMEM((D_EXPERT, d), BF16),
                pltpu.SemaphoreType.DMA((2,)), pltpu.SemaphoreType.DMA((2,)),
            ],
        ),
        out_shape=jax.ShapeDtypeStruct((out_rows * 8, LANES), jnp.uint32),
        compiler_params=_params("arbitrary"),
        name="moe_experts",
    )(block_expert, n_used, src_tok, dst_row, xg, w1, w3, w2)
    return _combine(x2d, gates_l, out2)


def _combine_kernel(x_ref, gate_ref, a_ref, b_ref, o_ref):
    tm = x_ref.shape[0]
    half = x_ref.shape[1] // 2
    g0 = gate_ref[:, 0:1]
    g1 = gate_ref[:, 1:2]
    for s in range(8):
        wa = a_ref[pl.ds(s, tm, stride=8), :]
        wb = b_ref[pl.ds(s, tm, stride=8), :]
        lo = slice(s * LANES, (s + 1) * LANES)
        hi = slice(half + s * LANES, half + (s + 1) * LANES)
        o_ref[:, lo] = x_ref[:, lo] + (g0 * _unpack_lo(wa) + g1 * _unpack_lo(wb))
        o_ref[:, hi] = x_ref[:, hi] + (g0 * _unpack_hi(wa) + g1 * _unpack_hi(wb))


def _combine(x2d, gates, out2):
    n, d = x2d.shape
    tm = _row_tile(n, 688)
    return pl.pallas_call(
        _combine_kernel,
        grid=(n // tm,),
        in_specs=[
            pl.BlockSpec((tm, d), lambda i: (i, 0)),
            pl.BlockSpec((tm, LANES), lambda i: (i, 0)),
            pl.BlockSpec((tm * 8, LANES), lambda i: (i, 0)),
            pl.BlockSpec((tm * 8, LANES), lambda i: (n // tm + i, 0)),
        ],
        out_specs=pl.BlockSpec((tm, d), lambda i: (i, 0)),
        out_shape=jax.ShapeDtypeStruct((n, d), F32),
        compiler_params=_params("parallel"),
        name="moe_combine",
    )(x2d, gates, out2, out2)


def _even_layer(x, norm_g, w_in, conv_w, conv_b, b_i, b_f, a_norm, w_gate2, b_gate, b_norm, w_out):
    b, t, d = x.shape
    n = b * t
    a_w = 2 * A_HEADS * A_DK + 2 * A_HEADS * A_DV
    g_w = 2 * A_HEADS
    b_w = 2 * B_HEADS * B_DK + 2 * B_HEADS * B_DV
    main = a_w + b_w
    gate_cols = g_w + GATE_RANK
    w = _even_weight(w_in, a_w, g_w, b_w, GATE_RANK)
    z = _normproj(x.reshape(n, d), norm_g, w).reshape(b, t, main + MXU_DIM)

    ya = _mlstm(z, main, conv_w, conv_b, b_i, b_f, a_norm)

    wg = jnp.zeros((B_HEADS, MXU_DIM, B_DK), F32).at[:, g_w:g_w + GATE_RANK, :].set(
        w_gate2.reshape(GATE_RANK, B_HEADS, B_DK).transpose(1, 0, 2))
    dummy = jnp.zeros((1, B_HEADS * B_DK), F32)
    hp = 2
    hspec = pl.BlockSpec((1, hp * B_DK), lambda bi, g: (0, g))
    yb = _gla_call(
        z, B_HEADS, hp, B_DK, B_DV,
        (a_w, a_w + B_HEADS * B_DK, a_w + 2 * B_HEADS * B_DK, a_w + 2 * B_HEADS * B_DK + B_HEADS * B_DV, main),
        MXU_DIM,
        ((wg, pl.BlockSpec((hp, MXU_DIM, B_DK), lambda bi, g: (g, 0, 0))),
         (b_gate.reshape(1, -1), hspec), (dummy, hspec)),
        b_norm, "gla")
    return _outproj(ya.reshape(n, -1), yb.reshape(n, -1), w_out.astype(BF16), x.reshape(n, d)).reshape(b, t, d)


def _odd_layer(x, lb, norm_g, w_in, c_norm, q_a_norm, w_q_up, kv_a_norm, w_kv_up, q_norm, k_norm, w_out):
    b, t, d = x.shape
    n = b * t
    c_w = 2 * C_HEADS * C_DK + 2 * C_HEADS * C_DV
    swap = (jnp.arange(D_ROPE) + D_ROPE // 2) % D_ROPE
    kr0 = c_w + Q_LORA + KV_LORA
    used = kr0 + 2 * D_ROPE
    total = -(-used // MXU_DIM) * MXU_DIM
    z2 = _normproj(x.reshape(n, d), norm_g, _odd_weight(w_in, kr0))
    z = z2.reshape(b, t, total)

    hp = 4
    hspec = pl.BlockSpec((1, hp * C_DK), lambda bi, g: (0, g))
    yc = _gla_call(
        z, C_HEADS, hp, C_DK, C_DV,
        (0, C_HEADS * C_DK, 2 * C_HEADS * C_DK, 2 * C_HEADS * C_DK + C_HEADS * C_DV, C_HEADS * C_DK),
        C_DK,
        ((jnp.log(lb).reshape(1, -1), hspec), (jnp.log1p(-lb).reshape(1, -1), hspec), ((1.0 - lb).reshape(1, -1), hspec)),
        c_norm, "hgrn")

    dq = D_NOPE + D_ROPE
    wq = w_q_up.reshape(Q_LORA, D_HEADS, dq)
    wq_rope = wq[:, :, D_NOPE:]
    wq_p = jnp.concatenate([wq[:, :, :D_NOPE].reshape(Q_LORA, -1),
                            jnp.concatenate([wq_rope, wq_rope[:, :, swap]], axis=-1).reshape(Q_LORA, -1)],
                           axis=1).astype(BF16)
    wkv = w_kv_up.reshape(KV_LORA, D_HEADS, D_NOPE + D_V)
    wkv_p = jnp.concatenate([wkv[:, :, :D_NOPE].reshape(KV_LORA, -1), wkv[:, :, D_NOPE:].reshape(KV_LORA, -1)],
                            axis=1).astype(BF16)
    qn = _normproj(z2, q_a_norm, wq_p, x_col_block=c_w // Q_LORA).reshape(b, t, -1)
    kvn = _normproj(z2, kv_a_norm, wkv_p, x_col_block=(c_w + Q_LORA) // KV_LORA).reshape(b, t, -1)

    pos = jnp.arange(t, dtype=F32)
    half = D_ROPE // 2
    inv = ROPE_THETA ** (-jnp.arange(half, dtype=F32) / half)
    ang = pos[:, None] * inv[None, :]
    cos, sin = jnp.cos(ang), jnp.sin(ang)
    tab = jnp.concatenate([cos, cos, -sin, sin], axis=1)
    pair = lambda g: jnp.concatenate([g[D_NOPE:], g[D_NOPE:][swap]]).reshape(1, LANES)
    yd = _mla(qn, kvn, z, kr0, tab, q_norm[:D_NOPE].reshape(1, LANES), pair(q_norm),
              k_norm[:D_NOPE].reshape(1, LANES), pair(k_norm))
    return _outproj(yc.reshape(n, -1), yd.reshape(n, -1), w_out.astype(BF16), x.reshape(n, d)).reshape(b, t, d)


def kernel(x, meta_tokens, ab_norm, ab_w_in, a_conv_w, a_conv_b, a_b_i, a_b_f, a_head_norm, b_w_gate2, b_b_gate, b_head_norm, ab_w_out, cd_norm, cd_w_in, c_lower_bound, c_head_norm, d_q_a_norm, d_w_q_up, d_kv_a_norm, d_w_kv_up, d_q_norm, d_k_norm, cd_w_out, moe_norm, moe_w_group, moe_b_group, moe_w_expert, moe_b_expert, moe_w1, moe_w3, moe_w2):
    b = x.shape[0]
    depth = moe_norm.shape[0]
    h = jnp.concatenate([jnp.broadcast_to(meta_tokens.astype(x.dtype)[None], (b, N_META, D_MODEL)), x], axis=1)
    t = h.shape[1]
    lb_cum = jnp.cumsum(jax.nn.softmax(c_lower_bound.astype(F32), axis=0), axis=0)
    lower_bounds = lb_cum - lb_cum[0]
    for layer in range(depth):
        j = layer // 2
        if layer % 2 == 0:
            h = _even_layer(h, ab_norm[j], ab_w_in[j], a_conv_w[j], a_conv_b[j], a_b_i[j], a_b_f[j], a_head_norm[j],
                            b_w_gate2[j], b_b_gate[j], b_head_norm[j], ab_w_out[j])
        else:
            h = _odd_layer(h, lower_bounds[layer], cd_norm[j], cd_w_in[j], c_head_norm[j], d_q_a_norm[j],
                           d_w_q_up[j], d_kv_a_norm[j], d_w_kv_up[j], d_q_norm[j], d_k_norm[j], cd_w_out[j])
        h = _moe(h.reshape(b * t, D_MODEL), moe_norm[layer], moe_w_group[layer], moe_b_group[layer],
                 moe_w_expert[layer], moe_b_expert[layer], moe_w1, moe_w3, moe_w2, layer).reshape(b, t, D_MODEL)
    return h[:, N_META:]
```

```python
import functools
import math

import jax
import jax.numpy as jnp
from jax import lax
from jax.experimental import pallas as pl
from jax.experimental.pallas import tpu as pltpu

F32 = jnp.float32
BF16 = jnp.bfloat16
HIGHEST = lax.Precision.HIGHEST

D_MODEL = 2048
N_META = 16
CHUNK = 64
CONV_K = 4
EPS = 1e-6
A_HEADS, A_DK, A_DV = 4, 128, 256
B_HEADS, B_DK, B_DV = 4, 128, 256
GATE_RANK = 16
GATE_TAU = 16.0
C_HEADS, C_DK, C_DV = 8, 128, 128
D_HEADS, D_NOPE, D_ROPE, D_V = 8, 128, 64, 128
Q_LORA, KV_LORA = 512, 256
ROPE_THETA = 10000.0
N_GROUPS, EXPERTS_PER_GROUP = 4, 8
N_EXPERTS = N_GROUPS * EXPERTS_PER_GROUP
TOP_K = 2
D_EXPERT = 512

LANES = 128
MXU_DIM = 256
BF16_ROWS = 16
VMEM_LIMIT = 56 * 1024 * 1024
MOE_BLOCK = MXU_DIM
ATT_BLOCK = 256
ATT_PAD = ATT_BLOCK - N_META

_NT = (((1,), (1,)), ((), ()))
_TN = (((0,), (0,)), ((), ()))


def _dot(a, b, precision=None):
    return jnp.dot(a, b, preferred_element_type=F32, precision=precision)


def _dot_nt(a, b):
    return lax.dot_general(a, b, _NT, preferred_element_type=F32)


def _dot_tn(a, b):
    return lax.dot_general(a, b, _TN, preferred_element_type=F32)


def _bf(x):
    return x.astype(BF16)


def _split3(x):
    hi = _bf(x)
    rest = x - hi.astype(F32)
    mid = _bf(rest)
    return hi, mid, _bf(rest - mid.astype(F32))


def _log_sigmoid(x):
    return jnp.minimum(x, 0.0) - jnp.log1p(jnp.exp(-jnp.abs(x)))


def _sigmoid(x):
    return 1.0 / (1.0 + jnp.exp(-x))


def _silu(x):
    return x * _sigmoid(x)


def _row_tile(m, cap):
    best = None
    for t in range(BF16_ROWS, min(m, cap) + 1, BF16_ROWS):
        if m % t == 0:
            best = t
    assert best is not None, m
    return best


def _col_tile(n, cap):
    best = None
    for t in range(MXU_DIM, min(n, cap) + 1, MXU_DIM):
        if n % t == 0:
            best = t
    assert best is not None, n
    return best


def _params(*sem):
    return pltpu.CompilerParams(dimension_semantics=sem, vmem_limit_bytes=VMEM_LIMIT)


_RELAYOUT_ROWS = 256


def _even_weight_kernel(wa_ref, wb_ref, wc_ref, o_ref, *, n_plain, n_shift, shift, gate_cols):
    ob = pl.program_id(0)
    rows = o_ref.shape[0]
    chunks = rows // _RELAYOUT_ROWS

    @pl.when(ob < n_plain)
    def _():
        def body(c, carry):
            r = pl.ds(pl.multiple_of(c * _RELAYOUT_ROWS, _RELAYOUT_ROWS), _RELAYOUT_ROWS)
            o_ref[r, :] = _bf(wa_ref[r, :])
            return carry

        lax.fori_loop(0, chunks, body, 0)

    @pl.when((ob >= n_plain) & (ob < n_plain + n_shift))
    def _():
        def body(c, carry):
            r = pl.ds(pl.multiple_of(c * _RELAYOUT_ROWS, _RELAYOUT_ROWS), _RELAYOUT_ROWS)
            wide = jnp.concatenate([wa_ref[r, :], wb_ref[r, :]], axis=1)
            o_ref[r, :] = _bf(wide[:, shift:shift + MXU_DIM])
            return carry

        lax.fori_loop(0, chunks, body, 0)

    @pl.when(ob == n_plain + n_shift)
    def _():
        lane = lax.broadcasted_iota(jnp.int32, (_RELAYOUT_ROWS, LANES), 1)

        def body(c, carry):
            r = pl.ds(pl.multiple_of(c * _RELAYOUT_ROWS, _RELAYOUT_ROWS), _RELAYOUT_ROWS)
            first = jnp.where(lane < shift, wc_ref[r, :], jnp.where(lane < gate_cols, wb_ref[r, :], 0.0))
            o_ref[r, :] = _bf(jnp.concatenate([first, jnp.zeros_like(first)], axis=1))
            return carry

        lax.fori_loop(0, chunks, body, 0)


def _even_weight(w_in, a_w, g_w, b_w, rank):
    d = w_in.shape[0]
    assert a_w % MXU_DIM == 0 and b_w % MXU_DIM == 0 and g_w + rank <= LANES and d % _RELAYOUT_ROWS == 0
    n_plain, n_shift = a_w // MXU_DIM, b_w // MXU_DIM
    n_out = n_plain + n_shift + 1
    last = n_out - 1

    def b_index(ob):
        return (0, jnp.where(ob < n_plain, 0, jnp.where(ob < last, 2 * (ob + 1), (a_w + g_w + b_w) // LANES)))

    return pl.pallas_call(
        functools.partial(_even_weight_kernel, n_plain=n_plain, n_shift=n_shift, shift=g_w, gate_cols=g_w + rank),
        grid=(n_out,),
        in_specs=[
            pl.BlockSpec((d, MXU_DIM), lambda ob: (0, jnp.minimum(ob, last - 1))),
            pl.BlockSpec((d, LANES), b_index),
            pl.BlockSpec((d, LANES), lambda ob: (0, a_w // LANES)),
        ],
        out_specs=pl.BlockSpec((d, MXU_DIM), lambda ob: (0, ob)),
        out_shape=jax.ShapeDtypeStruct((d, n_out * MXU_DIM), BF16),
        compiler_params=_params("parallel"),
        name="even_weight",
    )(w_in, w_in, w_in)


def _odd_weight_kernel(w_ref, o_ref, *, n_plain):
    ob = pl.program_id(0)
    chunks = o_ref.shape[0] // _RELAYOUT_ROWS
    half = D_ROPE // 2

    def body(c, carry):
        r = pl.ds(pl.multiple_of(c * _RELAYOUT_ROWS, _RELAYOUT_ROWS), _RELAYOUT_ROWS)
        w = w_ref[r, :]

        @pl.when(ob < n_plain)
        def _():
            o_ref[r, :] = _bf(w)

        @pl.when(ob == n_plain)
        def _():
            pair = jnp.concatenate([w[:, :D_ROPE], w[:, half:D_ROPE], w[:, :half]], axis=1)
            o_ref[r, :] = _bf(jnp.concatenate([pair, jnp.zeros_like(pair)], axis=1))

        return carry

    lax.fori_loop(0, chunks, body, 0)


def _odd_weight(w_in, kr0):
    d = w_in.shape[0]
    assert kr0 % MXU_DIM == 0 and w_in.shape[1] == kr0 + D_ROPE and d % _RELAYOUT_ROWS == 0
    n_plain = kr0 // MXU_DIM
    return pl.pallas_call(
        functools.partial(_odd_weight_kernel, n_plain=n_plain),
        grid=(n_plain + 1,),
        in_specs=[pl.BlockSpec((d, MXU_DIM), lambda ob: (0, ob))],
        out_specs=pl.BlockSpec((d, MXU_DIM), lambda ob: (0, ob)),
        out_shape=jax.ShapeDtypeStruct((d, kr0 + MXU_DIM), BF16),
        compiler_params=_params("parallel"),
        name="odd_weight",
    )(w_in)


def _normproj_kernel(x_ref, g_ref, w_ref, o_ref, xs_ref):
    tm = xs_ref.shape[0]

    @pl.when(pl.program_id(1) == 0)
    def _():
        def body(c, carry):
            r0 = pl.multiple_of(c * BF16_ROWS, BF16_ROWS)
            x = x_ref[pl.ds(r0, BF16_ROWS), :]
            ms = jnp.mean(x * x, axis=-1, keepdims=True)
            xs_ref[pl.ds(r0, BF16_ROWS), :] = _bf(x * lax.rsqrt(ms + EPS) * g_ref[...])
            return carry

        lax.fori_loop(0, tm // BF16_ROWS, body, 0)

    o_ref[...] = _dot(xs_ref[...], w_ref[...]).astype(o_ref.dtype)


def _normproj(x2d, gain, w, *, x_col_block=0, out_dtype=F32):
    m = x2d.shape[0]
    k, n = w.shape
    tm = _row_tile(m, 688)
    tn = _col_tile(n, 1280)
    return pl.pallas_call(
        _normproj_kernel,
        grid=(m // tm, n // tn),
        in_specs=[
            pl.BlockSpec((tm, k), lambda i, j: (i, x_col_block)),
            pl.BlockSpec((1, k), lambda i, j: (0, 0)),
            pl.BlockSpec((k, tn), lambda i, j: (0, j)),
        ],
        out_specs=pl.BlockSpec((tm, tn), lambda i, j: (i, j)),
        out_shape=jax.ShapeDtypeStruct((m, n), out_dtype),
        scratch_shapes=[pltpu.VMEM((tm, k), BF16)],
        compiler_params=_params("parallel", "arbitrary"),
        name="normproj",
    )(x2d, gain.reshape(1, k).astype(F32), w)


def _outproj_kernel(ya_ref, yb_ref, w_ref, r_ref, o_ref):
    ka = ya_ref.shape[1]
    acc = _dot(ya_ref[...], w_ref[:ka, :]) + _dot(yb_ref[...], w_ref[ka:, :])
    o_ref[...] = r_ref[...] + acc


def _outproj(ya, yb, w, res):
    m, ka = ya.shape
    kb = yb.shape[1]
    n = w.shape[1]
    tm = _row_tile(m, 1376)
    tn = _col_tile(n, 1024)
    return pl.pallas_call(
        _outproj_kernel,
        grid=(m // tm, n // tn),
        in_specs=[
            pl.BlockSpec((tm, ka), lambda i, j: (i, 0)),
            pl.BlockSpec((tm, kb), lambda i, j: (i, 0)),
            pl.BlockSpec((ka + kb, tn), lambda i, j: (0, j)),
            pl.BlockSpec((tm, tn), lambda i, j: (i, j)),
        ],
        out_specs=pl.BlockSpec((tm, tn), lambda i, j: (i, j)),
        out_shape=jax.ShapeDtypeStruct((m, n), F32),
        compiler_params=_params("parallel", "arbitrary"),
        name="outproj",
    )(ya, yb, w, res)


def _mlstm_kernel(bi_ref, bf_ref, q_ref, k_ref, v_ref, og_ref, gt_ref,
                  cwq_ref, cwk_ref, cbq_ref, cbk_ref, hn_ref, o_ref, c_ref, n_ref, m_ref, *, heads):
    t_total = q_ref.shape[1]
    n_chunks = (t_total - N_META) // CHUNK
    head0 = pl.program_id(1) * heads

    c_ref[...] = jnp.zeros_like(c_ref)
    n_ref[...] = jnp.zeros_like(n_ref)
    m_ref[...] = jnp.zeros_like(m_ref)

    def conv(win, cw, cb, length):
        y = cb
        for j in range(CONV_K):
            y = y + win[8 - (CONV_K - 1) + j:8 - (CONV_K - 1) + j + length, :] * cw[j:j + 1, :]
        return _silu(y)

    hs = range(heads)
    kcol = [slice(hh * A_DK, (hh + 1) * A_DK) for hh in hs]
    vcol = [slice(hh * A_DV, (hh + 1) * A_DV) for hh in hs]

    def gates(hh, blk, blk_parts, length, causal, upper):
        head = head0 + hh
        b_i = bi_ref[head]
        b_f = bf_ref[head]
        lane = lax.broadcasted_iota(jnp.int32, (length, LANES), 1)
        ig_c = jnp.sum(jnp.where(lane == head, blk, 0.0), axis=1, keepdims=True) + b_i
        lf_c = _log_sigmoid(jnp.sum(jnp.where(lane == A_HEADS + head, blk, 0.0), axis=1, keepdims=True) + b_f)
        sel_r = lax.broadcasted_iota(jnp.int32, (8, LANES), 0)
        sel_l = lax.broadcasted_iota(jnp.int32, (8, LANES), 1)
        sel = jnp.where(sel_l == head + A_HEADS * sel_r, 1.0, 0.0).astype(BF16)
        rows = _dot_nt(sel, blk_parts[0]) + (_dot_nt(sel, blk_parts[1]) + _dot_nt(sel, blk_parts[2]))
        ig_r = rows[0:1, :] + b_i
        lf_r = _log_sigmoid(rows[1:2, :] + b_f)
        b_c = jnp.sum(jnp.where(causal, lf_r, 0.0), axis=1, keepdims=True)
        b_r = jnp.sum(jnp.where(upper, lf_c, 0.0), axis=0, keepdims=True)
        b_end = b_c[length - 1:length, :]
        w_end = b_end - b_c + ig_c
        m_loc = jnp.max(w_end, axis=0, keepdims=True)
        d = jnp.where(causal, b_c - b_r + ig_r, -jnp.inf)
        return b_c, b_end, jnp.exp(w_end - m_loc), m_loc, d, jnp.max(d, axis=1, keepdims=True)

    def chunk(o, length, qwin, kwin):
        row = lax.broadcasted_iota(jnp.int32, (length, length), 0)
        col = lax.broadcasted_iota(jnp.int32, (length, length), 1)
        causal = col <= row
        blk = gt_ref[0, pl.ds(o, length), 0:LANES]
        blk_parts = _split3(blk)
        gt = [gates(hh, blk, blk_parts, length, causal, row <= col) for hh in hs]
        q = [conv(qwin[hh], cwq_ref[:, kcol[hh]], cbq_ref[:, kcol[hh]], length) for hh in hs]
        k = [conv(kwin[hh], cwk_ref[:, kcol[hh]], cbk_ref[:, kcol[hh]], length) * (A_DK ** -0.5) for hh in hs]
        vb = [_bf(v_ref[0, pl.ds(o, length), vcol[hh]]) for hh in hs]
        qb = [_bf(q[hh]) for hh in hs]
        k_w = [k[hh] * gt[hh][2] for hh in hs]
        qk = [_dot_nt(qb[hh], _bf(k[hh])) for hh in hs]
        c_in = [c_ref[hh] for hh in hs]
        q_c = [_dot(qb[hh], _bf(c_in[hh])) for hh in hs]
        c_loc = [_dot_tn(_bf(k_w[hh]), vb[hh]) for hh in hs]
        s, a_t, m_t, q_n = [], [], [], []
        for hh in hs:
            b_c, b_end, _, m_loc, d, d_max = gt[hh]
            m_in = m_ref[hh]
            inter = b_c + m_in
            m_t.append(jnp.maximum(inter, d_max))
            s.append(qk[hh] * jnp.exp(d - m_t[hh]))
            a_t.append(jnp.exp(inter - m_t[hh]))
            m_new = jnp.maximum(b_end + m_in, m_loc)
            a = jnp.exp(b_end + m_in - m_new)
            c = jnp.exp(m_loc - m_new)
            n_in = n_ref[hh]
            c_ref[hh] = a * c_in[hh] + c * c_loc[hh]
            n_ref[hh] = a * n_in + c * jnp.sum(k_w[hh], axis=0, keepdims=True)
            m_ref[hh] = m_new
            q_n.append(jnp.sum(q[hh] * n_in, axis=1, keepdims=True))
        num = [_dot(_bf(s[hh]), vb[hh]) + a_t[hh] * q_c[hh] for hh in hs]
        for hh in hs:
            den = jnp.sum(s[hh], axis=1, keepdims=True) + a_t[hh] * q_n[hh]
            h = num[hh] / jnp.maximum(jnp.abs(den), jnp.exp(-m_t[hh]))
            hn = h * lax.rsqrt(jnp.mean(h * h, axis=-1, keepdims=True) + EPS) * hn_ref[:, vcol[hh]]
            y = _sigmoid(og_ref[0, pl.ds(o, length), vcol[hh]]) * hn
            o_ref[0, pl.ds(o, length), vcol[hh]] = y.astype(o_ref.dtype)

    zeros8 = jnp.zeros((8, A_DK), F32)
    chunk(0, N_META,
          [jnp.concatenate([zeros8, q_ref[0, 0:N_META, kcol[hh]]], axis=0) for hh in hs],
          [jnp.concatenate([zeros8, k_ref[0, 0:N_META, kcol[hh]]], axis=0) for hh in hs])

    def body(c, carry):
        o = pl.multiple_of(N_META + c * CHUNK, BF16_ROWS)
        w0 = pl.multiple_of(N_META - 8 + c * CHUNK, 8)
        chunk(o, CHUNK, [q_ref[0, pl.ds(w0, CHUNK + 8), kcol[hh]] for hh in hs],
              [k_ref[0, pl.ds(w0, CHUNK + 8), kcol[hh]] for hh in hs])
        return carry

    lax.fori_loop(0, n_chunks, body, 0)


def _mlstm(z, gate_col, conv_w, conv_b, b_i, b_f, head_norm, heads=2):
    b, t, _ = z.shape
    hk = A_HEADS * A_DK
    wk, wv = heads * A_DK, heads * A_DV
    smem = pl.BlockSpec(memory_space=pltpu.SMEM)
    col = lambda width, off: (lambda bi, g: (bi, 0, off // width + g))
    return pl.pallas_call(
        functools.partial(_mlstm_kernel, heads=heads),
        grid=(b, A_HEADS // heads),
        in_specs=[
            smem, smem,
            pl.BlockSpec((1, t, wk), col(wk, 0)),
            pl.BlockSpec((1, t, wk), col(wk, hk)),
            pl.BlockSpec((1, t, wv), col(wv, 2 * hk)),
            pl.BlockSpec((1, t, wv), col(wv, 2 * hk + A_HEADS * A_DV)),
            pl.BlockSpec((1, t, MXU_DIM), lambda bi, g: (bi, 0, gate_col // MXU_DIM)),
            pl.BlockSpec((CONV_K, wk), lambda bi, g: (0, g)),
            pl.BlockSpec((CONV_K, wk), lambda bi, g: (0, A_HEADS // heads + g)),
            pl.BlockSpec((1, wk), lambda bi, g: (0, g)),
            pl.BlockSpec((1, wk), lambda bi, g: (0, A_HEADS // heads + g)),
            pl.BlockSpec((1, wv), lambda bi, g: (0, g)),
        ],
        out_specs=pl.BlockSpec((1, t, wv), lambda bi, g: (bi, 0, g)),
        out_shape=jax.ShapeDtypeStruct((b, t, A_HEADS * A_DV), BF16),
        scratch_shapes=[pltpu.VMEM((heads, A_DK, A_DV), F32), pltpu.VMEM((heads, 1, A_DK), F32),
                        pltpu.VMEM((heads, 1, 1), F32)],
        compiler_params=_params("parallel", "parallel"),
        name="mlstm",
    )(b_i, b_f, z, z, z, z, z, conv_w, conv_w, conv_b.reshape(1, -1), conv_b.reshape(1, -1),
      head_norm.reshape(1, -1))


def _gla_kernel(q_ref, k_ref, v_ref, og_ref, g_ref, p0_ref, p1_ref, p2_ref, hn_ref, o_ref, st_ref, *, mode, heads):
    t_total = q_ref.shape[1]
    n_chunks = (t_total - N_META) // CHUNK
    dv, dk = st_ref.shape[1:]
    st_ref[...] = jnp.zeros_like(st_ref)

    hs = range(heads)
    kcol = [slice(hh * dk, (hh + 1) * dk) for hh in hs]
    vcol = [slice(hh * dv, (hh + 1) * dv) for hh in hs]

    def gate_inputs(hh, o, length):
        q = q_ref[0, pl.ds(o, length), kcol[hh]]
        if mode == "gla":
            pre = _dot(g_ref[0, pl.ds(o, length), :], p0_ref[hh], precision=HIGHEST) + p1_ref[:, kcol[hh]]
            return q * (dk ** -0.5), k_ref[0, pl.ds(o, length), kcol[hh]], _log_sigmoid(pre) / GATE_TAU
        fpre = g_ref[0, pl.ds(o, length), kcol[hh]]
        a = p0_ref[:, kcol[hh]]
        bb = p1_ref[:, kcol[hh]] + _log_sigmoid(fpre)
        lg = jnp.maximum(a, bb) + jnp.log1p(jnp.exp(-jnp.abs(a - bb)))
        return q, p2_ref[:, kcol[hh]] * _sigmoid(-fpre), lg

    def cumsum_time(tri, lg):
        parts = _dot(tri, jnp.concatenate(_split3(lg), axis=1))
        return parts[:, :dk] + (parts[:, dk:2 * dk] + parts[:, 2 * dk:])

    def chunk(o, length):
        row = lax.broadcasted_iota(jnp.int32, (length, length), 0)
        col = lax.broadcasted_iota(jnp.int32, (length, length), 1)
        causal = col <= row
        tri = jnp.where(causal, 1.0, 0.0).astype(BF16)
        qkl = [gate_inputs(hh, o, length) for hh in hs]
        vb = [_bf(v_ref[0, pl.ds(o, length), vcol[hh]]) for hh in hs]
        g = [cumsum_time(tri, qkl[hh][2]) for hh in hs]
        g_end = [g[hh][length - 1:length, :] for hh in hs]
        g_mid = [g[hh][length // 2:length // 2 + 1, :] for hh in hs]
        s = [_dot_nt(_bf(qkl[hh][0] * jnp.exp(g[hh] - g_mid[hh])), _bf(qkl[hh][1] * jnp.exp(g_mid[hh] - g[hh])))
             for hh in hs]
        st_in = [st_ref[hh] for hh in hs]
        inter = [_dot_nt(_bf(qkl[hh][0] * jnp.exp(g[hh])), _bf(st_in[hh])) for hh in hs]
        local = [_dot_tn(vb[hh], _bf(qkl[hh][1] * jnp.exp(g_end[hh] - g[hh]))) for hh in hs]
        for hh in hs:
            st_ref[hh] = st_in[hh] * jnp.exp(g_end[hh]) + local[hh]
        out = [_dot(_bf(jnp.where(causal, s[hh], 0.0)), vb[hh]) + inter[hh] for hh in hs]
        for hh in hs:
            hn = out[hh] * lax.rsqrt(jnp.mean(out[hh] * out[hh], axis=-1, keepdims=True) + EPS) * hn_ref[:, vcol[hh]]
            og = og_ref[0, pl.ds(o, length), vcol[hh]]
            gate = _silu(og) if mode == "gla" else _sigmoid(og)
            o_ref[0, pl.ds(o, length), vcol[hh]] = (gate * hn).astype(o_ref.dtype)

    chunk(0, N_META)

    def body(c, carry):
        chunk(pl.multiple_of(N_META + c * CHUNK, BF16_ROWS), CHUNK)
        return carry

    lax.fori_loop(0, n_chunks, body, 0)


def _gla_call(z, n_heads, heads, dk, dv, blocks, gate_width, params, head_norm, mode):
    b, t, _ = z.shape
    q0, k0, v0, og0, g0 = blocks
    zspec = lambda width, off, grouped=True: pl.BlockSpec(
        (1, t, width), (lambda bi, g: (bi, 0, off // width + (g if grouped else 0))))
    (p0, s0), (p1, s1), (p2, s2) = params
    gate_spec = zspec(gate_width, g0, grouped=False) if mode == "gla" else zspec(heads * dk, g0)
    return pl.pallas_call(
        functools.partial(_gla_kernel, mode=mode, heads=heads),
        grid=(b, n_heads // heads),
        in_specs=[
            zspec(heads * dk, q0), zspec(heads * dk, k0), zspec(heads * dv, v0), zspec(heads * dv, og0),
            gate_spec, s0, s1, s2,
            pl.BlockSpec((1, heads * dv), lambda bi, g: (0, g)),
        ],
        out_specs=pl.BlockSpec((1, t, heads * dv), lambda bi, g: (bi, 0, g)),
        out_shape=jax.ShapeDtypeStruct((b, t, n_heads * dv), BF16),
        scratch_shapes=[pltpu.VMEM((heads, dv, dk), F32)],
        compiler_params=_params("parallel", "parallel"),
        name="gla_" + mode,
    )(z, z, z, z, z, p0, p1, p2, head_norm.reshape(1, -1))


def _mla_kernel(qn_ref, qr_ref, kn_ref, v_ref, kr_ref, tab_ref, gqn_ref, gqr_ref, gkn_ref, gkr_ref,
                o_ref, qf_ref, kf_ref, vf_ref):
    t_total = qn_ref.shape[1]
    n_blocks = (ATT_PAD + t_total) // ATT_BLOCK
    dqk = D_NOPE + D_ROPE
    scale = dqk ** -0.5
    rows = _row_tile(t_total, 768)
    lane = lax.broadcasted_iota(jnp.int32, (rows, LANES), 1)
    first_half = lane < D_ROPE

    qf_ref[0:ATT_PAD, :] = jnp.zeros((ATT_PAD, 2 * LANES), BF16)
    kf_ref[0:ATT_PAD, :] = jnp.zeros((ATT_PAD, 2 * LANES), BF16)
    vf_ref[0:ATT_PAD, :] = jnp.zeros((ATT_PAD, D_V), BF16)

    def rope_pair(x, gains, tab):
        p = x * gains * tab
        return p + pltpu.roll(p, D_ROPE, 1)

    def prep(c):
        r0 = c * rows
        dst = ATT_PAD + c * rows
        tab = tab_ref[pl.ds(r0, rows), :]
        qn = qn_ref[0, pl.ds(r0, rows), :]
        qr = qr_ref[0, pl.ds(r0, rows), :]
        ssq = jnp.sum(qn * qn, axis=-1, keepdims=True) + 0.5 * jnp.sum(qr * qr, axis=-1, keepdims=True)
        rq = lax.rsqrt(ssq / dqk + EPS) * scale
        qf_ref[pl.ds(dst, rows), 0:LANES] = _bf(qn * gqn_ref[...] * rq)
        qf_ref[pl.ds(dst, rows), LANES:2 * LANES] = _bf(rope_pair(qr, gqr_ref[...], tab) * rq)
        kn = kn_ref[0, pl.ds(r0, rows), :]
        kr = kr_ref[0, pl.ds(r0, rows), :]
        ssk = jnp.sum(kn * kn, axis=-1, keepdims=True) + 0.5 * jnp.sum(kr * kr, axis=-1, keepdims=True)
        rk = lax.rsqrt(ssk / dqk + EPS)
        kf_ref[pl.ds(dst, rows), 0:LANES] = _bf(kn * gkn_ref[...] * rk)
        kf_ref[pl.ds(dst, rows), LANES:2 * LANES] = _bf(
            jnp.where(first_half, rope_pair(kr, gkr_ref[...], tab) * rk, 0.0))
        vf_ref[pl.ds(dst, rows), :] = _bf(v_ref[0, pl.ds(r0, rows), :])

    for c in range(t_total // rows):
        prep(c)

    qpos = lax.broadcasted_iota(jnp.int32, (ATT_BLOCK, ATT_BLOCK), 0)
    kpos = lax.broadcasted_iota(jnp.int32, (ATT_BLOCK, ATT_BLOCK), 1)
    neg = -jnp.inf

    for qi in range(n_blocks):
        q = qf_ref[qi * ATT_BLOCK:(qi + 1) * ATT_BLOCK, :]
        s = _dot_nt(q, kf_ref[0:(qi + 1) * ATT_BLOCK, :])
        parts = [s[:, j * ATT_BLOCK:(j + 1) * ATT_BLOCK] for j in range(qi + 1)]
        parts[0] = jnp.where(kpos >= ATT_PAD, parts[0], neg)
        parts[qi] = jnp.where(kpos <= qpos, parts[qi], neg)
        top = functools.reduce(jnp.maximum, parts)
        m = jnp.max(top, axis=-1, keepdims=True)
        if qi == 0:
            m = jnp.where(m == neg, 0.0, m)
        probs = [jnp.exp(part - m) for part in parts]
        l = jnp.sum(functools.reduce(jnp.add, probs), axis=-1, keepdims=True)
        pv = _dot(jnp.concatenate([_bf(pr) for pr in probs], axis=1), vf_ref[0:(qi + 1) * ATT_BLOCK, :])
        if qi == 0:
            out = pv / jnp.where(l == 0.0, 1.0, l)
            o_ref[0, 0:N_META, :] = out[ATT_PAD:, :].astype(o_ref.dtype)
        else:
            dst = qi * ATT_BLOCK - ATT_PAD
            o_ref[0, dst:dst + ATT_BLOCK, :] = (pv / l).astype(o_ref.dtype)


def _mla(qn, kvn, z, kr_col, tab, gqn, gqr, gkn, gkr):
    b, t, _ = qn.shape
    assert (ATT_PAD + t) % ATT_BLOCK == 0 and t % (3 * BF16_ROWS) == 0
    tp = ATT_PAD + t
    hspec = lambda off: pl.BlockSpec((1, t, LANES), lambda bi, h: (bi, 0, off + h))
    gspec = pl.BlockSpec((1, LANES), lambda bi, h: (0, 0))
    return pl.pallas_call(
        _mla_kernel,
        grid=(b, D_HEADS),
        in_specs=[
            hspec(0), hspec(D_HEADS), hspec(0), hspec(D_HEADS),
            pl.BlockSpec((1, t, LANES), lambda bi, h: (bi, 0, kr_col // LANES)),
            pl.BlockSpec((t, LANES), lambda bi, h: (0, 0)),
            gspec, gspec, gspec, gspec,
        ],
        out_specs=pl.BlockSpec((1, t, D_V), lambda bi, h: (bi, 0, h)),
        out_shape=jax.ShapeDtypeStruct((b, t, D_HEADS * D_V), BF16),
        scratch_shapes=[pltpu.VMEM((tp, 2 * LANES), BF16), pltpu.VMEM((tp, 2 * LANES), BF16),
                        pltpu.VMEM((tp, D_V), BF16)],
        compiler_params=_params("parallel", "parallel"),
        name="mla",
    )(qn, qn, kvn, kvn, z, tab, gqn, gqr, gkn, gkr)


def _pack_bf16_pairs(v):
    w = v.shape[1] // 2
    bits = pltpu.bitcast(_bf(v).astype(F32), jnp.uint32)
    return (bits[:, :w] >> 16) | (bits[:, w:] & jnp.uint32(0xFFFF0000))


def _unpack_lo(words):
    return pltpu.bitcast(words << 16, F32)


def _unpack_hi(words):
    return pltpu.bitcast(words & jnp.uint32(0xFFFF0000), F32)


def _router_kernel(x_ref, g_ref, w_ref, b_ref, gate_ref, idx_ref, xg_ref, cnt_ref, carry_ref):
    tm = x_ref.shape[0]

    @pl.when(pl.program_id(0) == 0)
    def _():
        carry_ref[...] = jnp.zeros_like(carry_ref)

    x = x_ref[...]
    ms = jnp.mean(x * x, axis=-1, keepdims=True)
    xn = x * lax.rsqrt(ms + EPS) * g_ref[...]
    logits = _dot(xn, w_ref[...], precision=HIGHEST) + b_ref[...]
    lane = lax.broadcasted_iota(jnp.int32, logits.shape, 1)
    lane_f = lane.astype(F32)
    neg = -jnp.inf
    big = float(LANES)

    is_group = lane < N_GROUPS
    g_max = jnp.max(jnp.where(is_group, logits, neg), axis=-1, keepdims=True)
    g_sum = jnp.sum(jnp.where(is_group, jnp.exp(logits - g_max), 0.0), axis=-1, keepdims=True)
    p_top = 1.0 / g_sum
    grp = jnp.min(jnp.where(is_group & (logits == g_max), lane_f, big), axis=-1, keepdims=True)

    e_lo = N_GROUPS + grp * EXPERTS_PER_GROUP
    in_grp = (lane_f >= e_lo) & (lane_f < e_lo + EXPERTS_PER_GROUP)
    e_max = jnp.max(jnp.where(in_grp, logits, neg), axis=-1, keepdims=True)
    e_sum = jnp.sum(jnp.where(in_grp, jnp.exp(logits - e_max), 0.0), axis=-1, keepdims=True)
    i1 = jnp.min(jnp.where(in_grp & (logits == e_max), lane_f, big), axis=-1, keepdims=True)
    rest = in_grp & (lane_f != i1)
    e_2nd = jnp.max(jnp.where(rest, logits, neg), axis=-1, keepdims=True)
    i2 = jnp.min(jnp.where(rest & (logits == e_2nd), lane_f, big), axis=-1, keepdims=True)
    p1 = 1.0 / e_sum
    p2 = jnp.exp(e_2nd - e_max) / e_sum
    tot = p1 + p2
    gate_ref[...] = jnp.where(lane == 0, p_top * p1 / tot, jnp.where(lane == 1, p_top * p2 / tot, 0.0))

    e1 = i1 - N_GROUPS
    e2 = i2 - N_GROUPS
    hot = jnp.where((lane_f == e1) | (lane_f == e2), 1.0, 0.0)
    row = lax.broadcasted_iota(jnp.int32, (tm, tm), 0)
    col = lax.broadcasted_iota(jnp.int32, (tm, tm), 1)
    before = _dot(jnp.where(col < row, 1.0, 0.0).astype(BF16), _bf(hot)) + carry_ref[...]
    r1 = jnp.sum(jnp.where(lane_f == e1, before, 0.0), axis=-1, keepdims=True)
    r2 = jnp.sum(jnp.where(lane_f == e2, before, 0.0), axis=-1, keepdims=True)
    total = carry_ref[...] + jnp.sum(hot, axis=0, keepdims=True)
    carry_ref[...] = total
    cnt_ref[...] = jnp.broadcast_to(total, cnt_ref.shape).astype(jnp.int32)
    idx_ref[...] = jnp.where(lane == 0, e1, jnp.where(lane == 1, e2, jnp.where(lane == 2, r1, jnp.where(
        lane == 3, r2, 0.0)))).astype(jnp.int32)

    words = _pack_bf16_pairs(xn)
    for s in range(8):
        xg_ref[pl.ds(s, tm, stride=8), :] = words[:, s * LANES:(s + 1) * LANES]


def _router(x2d, gain, w_group, b_group, w_expert, b_expert):
    m, d = x2d.shape
    assert d == 2 * 8 * LANES
    tm = _row_tile(m, 688)
    pad = LANES - N_GROUPS - N_EXPERTS
    w = jnp.concatenate([w_group, w_expert, jnp.zeros((d, pad), F32)], axis=1)
    bias = jnp.concatenate([b_group, b_expert, jnp.zeros((pad,), F32)]).reshape(1, LANES)
    return pl.pallas_call(
        _router_kernel,
        grid=(m // tm,),
        in_specs=[
            pl.BlockSpec((tm, d), lambda i: (i, 0)),
            pl.BlockSpec((1, d), lambda i: (0, 0)),
            pl.BlockSpec((d, LANES), lambda i: (0, 0)),
            pl.BlockSpec((1, LANES), lambda i: (0, 0)),
        ],
        out_specs=[pl.BlockSpec((tm, LANES), lambda i: (i, 0)), pl.BlockSpec((tm, LANES), lambda i: (i, 0)),
                   pl.BlockSpec((tm * 8, LANES), lambda i: (i, 0)), pl.BlockSpec((8, LANES), lambda i: (0, 0))],
        out_shape=[jax.ShapeDtypeStruct((m, LANES), F32), jax.ShapeDtypeStruct((m, LANES), jnp.int32),
                   jax.ShapeDtypeStruct((m * 8, LANES), jnp.uint32), jax.ShapeDtypeStruct((8, LANES), jnp.int32)],
        scratch_shapes=[pltpu.VMEM((1, LANES), F32)],
        compiler_params=_params("arbitrary"),
        name="router",
    )(x2d, gain.reshape(1, d), w, bias)


def _invert_kernel(dest_ref, inv_ref):
    def clear(s, carry):
        inv_ref[s] = -1
        return carry

    lax.fori_loop(0, inv_ref.shape[0], clear, 0, unroll=8)

    def put(f, carry):
        inv_ref[dest_ref[f]] = f
        return carry

    lax.fori_loop(0, dest_ref.shape[0], put, 0, unroll=8)


def _invert(dest, p):
    assert p % 8 == 0 and dest.shape[0] % 8 == 0
    smem = pl.BlockSpec(memory_space=pltpu.SMEM)
    return pl.pallas_call(
        _invert_kernel, in_specs=[smem], out_specs=smem,
        out_shape=jax.ShapeDtypeStruct((p,), jnp.int32), name="moe_invert",
    )(dest)


def _expert_kernel(be_ref, nxt_ref, run_ref, nu_ref, src_ref, dst_ref, xg_hbm, w1_hbm, w3_hbm, w2_hbm, o_hbm,
                   xbuf, ybuf, xs_ref, w1f, w3f, w2f, w1s, w3s, w2s, sem_in, sem_out, sem_w, *, layer):
    i = pl.program_id(0)
    n_used = nu_ref[0]
    par = i % 2
    half = D_MODEL // 2
    half_e = D_EXPERT // 2

    def weight_copies(expert, slot):
        return [pltpu.make_async_copy(w_hbm.at[layer, expert], w_f.at[slot], sem_w.at[slot, j])
                for j, (w_hbm, w_f) in enumerate(((w1_hbm, w1f), (w3_hbm, w3f), (w2_hbm, w2f)))]

    def gather(block, r, slot):
        src = pl.multiple_of(src_ref[block * MOE_BLOCK + r], 8)
        return pltpu.make_async_copy(xg_hbm.at[pl.ds(src, 8), :], xbuf.at[slot, pl.ds(r * 8, 8), :], sem_in.at[slot])

    def scatter(block, r, slot):
        dst = pl.multiple_of(dst_ref[(block + 1) * MOE_BLOCK + r], 8)
        return pltpu.make_async_copy(ybuf.at[slot, pl.ds(r * 8, 8), :], o_hbm.at[pl.ds(dst, 8), :], sem_out.at[slot])

    def wait_gathers(slot):
        pltpu.make_async_copy(xbuf.at[1 - slot], xbuf.at[slot], sem_in.at[slot]).wait()

    def wait_scatters(slot):
        pltpu.make_async_copy(ybuf.at[slot], ybuf.at[1 - slot], sem_out.at[slot]).wait()

    @pl.when(i == 0)
    def _():
        ybuf[1] = jnp.zeros(ybuf.shape[1:], ybuf.dtype)
        n_real = o_hbm.shape[0] - 2 * MOE_BLOCK * 8
        fill1 = pltpu.make_async_copy(ybuf.at[1], o_hbm.at[pl.ds(n_real + MOE_BLOCK * 8, MOE_BLOCK * 8), :],
                                      sem_out.at[1])
        fill1.start()
        fill1.wait()
        pltpu.make_async_copy(ybuf.at[1], o_hbm.at[pl.ds(n_real, MOE_BLOCK * 8), :], sem_out.at[0]).start()

        def first(r, carry):
            gather(0, r, 0).start()
            return carry

        lax.fori_loop(0, MOE_BLOCK, first, 0, unroll=8)

    @pl.when(i < n_used)
    def _():
        expert = be_ref[i]
        wslot = run_ref[i] % 2

        @pl.when(i == 0)
        def _():
            for cp in weight_copies(expert, 0):
                cp.start()

        @pl.when((i == 0) | (expert != be_ref[jnp.maximum(i - 1, 0)]))
        def _():
            for cp in weight_copies(expert, wslot):
                cp.wait()

            @pl.when(nxt_ref[i] != expert)
            def _():
                for cp in weight_copies(nxt_ref[i], 1 - wslot):
                    cp.start()

            w1s[...] = _bf(w1f[wslot])
            w3s[...] = _bf(w3f[wslot])
            w2s[...] = _bf(w2f[wslot])

        wait_gathers(par)

        def move_rows(group, n_groups=6):
            lo, hi = group * MOE_BLOCK // n_groups, (group + 1) * MOE_BLOCK // n_groups
            for r in range(lo, hi):
                gather(i + 1, r, 1 - par).start(priority=r % 2)
                scatter(i - 1, r, 1 - par).start(priority=(r + 1) % 2)

        for s in range(8):
            words = xbuf[par, pl.ds(s, MOE_BLOCK, stride=8), :]
            xs_ref[:, s * LANES:(s + 1) * LANES] = _bf(_unpack_lo(words))
            xs_ref[:, half + s * LANES:half + (s + 1) * LANES] = _bf(_unpack_hi(words))
        move_rows(0)
        xb = xs_ref[...]
        h1a = _dot(xb, w1s[:, :half_e])
        move_rows(1)
        h3a = _dot(xb, w3s[:, :half_e])
        move_rows(2)
        act_a = _bf(_silu(h1a) * h3a)
        h1b = _dot(xb, w1s[:, half_e:])
        move_rows(3)
        h3b = _dot(xb, w3s[:, half_e:])
        move_rows(4)
        act_b = _bf(_silu(h1b) * h3b)
        y = _dot(act_a, w2s[:half_e, :])
        move_rows(5)
        y = y + _dot(act_b, w2s[half_e:, :])

        wait_scatters(par)
        words = _pack_bf16_pairs(y)
        for s in range(8):
            ybuf[par, pl.ds(s, MOE_BLOCK, stride=8), :] = words[:, s * LANES:(s + 1) * LANES]

        @pl.when(i == n_used - 1)
        def _():
            def last(r, carry):
                scatter(i, r, par).start()
                return carry

            lax.fori_loop(0, MOE_BLOCK, last, 0, unroll=8)
            wait_scatters(par)
            wait_scatters(1 - par)
            wait_gathers(1 - par)


def _moe(x2d, gain, w_group, b_group, w_expert, b_expert, w1, w3, w2, layer):
    n, d = x2d.shape
    gates_l, idx_l, xg, cnt = _router(x2d, gain, w_group, b_group, w_expert, b_expert)

    a = n * TOP_K
    n_blocks = -(-a // MOE_BLOCK) + N_EXPERTS
    p = n_blocks * MOE_BLOCK
    counts = cnt[0, :N_EXPERTS]
    padded = (counts + MOE_BLOCK - 1) // MOE_BLOCK * MOE_BLOCK
    pad_end = jnp.cumsum(padded)
    pad_start = pad_end - padded
    e_hot = idx_l[:, :TOP_K, None] == jnp.arange(N_EXPERTS, dtype=jnp.int32)
    dest = jnp.sum(jnp.where(e_hot, pad_start, 0), axis=-1) + idx_l[:, TOP_K:2 * TOP_K]
    dest = jnp.clip(dest.reshape(-1), 0, p - 1).astype(jnp.int32)
    blk0 = jnp.arange(n_blocks, dtype=jnp.int32) * MOE_BLOCK
    block_expert = jnp.minimum(jnp.searchsorted(pad_end, blk0, side="right"), N_EXPERTS - 1).astype(jnp.int32)
    n_used = (pad_end[-1] // MOE_BLOCK).astype(jnp.int32).reshape(1)
    experts = jnp.arange(N_EXPERTS, dtype=jnp.int32)
    later = jnp.where((experts[None, :] > experts[:, None]) & (counts[None, :] > 0), experts[None, :], N_EXPERTS)
    next_owner = jnp.min(later, axis=1)
    next_expert = jnp.where(next_owner < N_EXPERTS, next_owner, experts)[block_expert].astype(jnp.int32)
    run_index = (jnp.cumsum(jnp.concatenate([jnp.ones((1,), jnp.int32),
                                             (block_expert[1:] != block_expert[:-1]).astype(jnp.int32)])) - 1
                 ).astype(jnp.int32)
    codes = _invert(dest, p)
    slot = jnp.arange(p, dtype=jnp.int32)
    spare = TOP_K * n + (slot // MOE_BLOCK % 2) * MOE_BLOCK + slot % MOE_BLOCK
    src_tok = (jnp.maximum(codes, 0) >> 1) * 8
    dst_row = jnp.where(codes >= 0, (codes & 1) * n + (codes >> 1), spare) * 8
    lead = (TOP_K * n + MOE_BLOCK + jnp.arange(MOE_BLOCK, dtype=jnp.int32)) * 8
    dst_row = jnp.concatenate([lead, dst_row])

    hbm = pl.BlockSpec(memory_space=pl.ANY)
    out_rows = TOP_K * n + 2 * MOE_BLOCK
    out2 = pl.pallas_call(
        functools.partial(_expert_kernel, layer=layer),
        grid_spec=pltpu.PrefetchScalarGridSpec(
            num_scalar_prefetch=6,
            grid=(n_blocks,),
            in_specs=[hbm, hbm, hbm, hbm],
            out_specs=hbm,
            scratch_shapes=[
                pltpu.VMEM((2, MOE_BLOCK * 8, LANES), jnp.uint32), pltpu.VMEM((2, MOE_BLOCK * 8, LANES), jnp.uint32),
                pltpu.VMEM((MOE_BLOCK, d), BF16),
                pltpu.VMEM((2, d, D_EXPERT), F32), pltpu.VMEM((2, d, D_EXPERT), F32), pltpu.VMEM((2, D_EXPERT, d), F32),
                pltpu.VMEM((d, D_EXPERT), BF16), pltpu.VMEM((d, D_EXPERT), BF16), pltpu.VMEM((D_EXPERT, d), BF16),
                pltpu.SemaphoreType.DMA((2,)), pltpu.SemaphoreType.DMA((2,)), pltpu.SemaphoreType.DMA((2, 3)),
            ],
        ),
        out_shape=jax.ShapeDtypeStruct((out_rows * 8, LANES), jnp.uint32),
        compiler_params=_params("arbitrary"),
        name="moe_experts",
    )(block_expert, next_expert, run_index, n_used, src_tok, dst_row, xg, w1, w3, w2)
    return _combine(x2d, gates_l, out2)


def _combine_kernel(x_ref, gate_ref, a_ref, b_ref, o_ref):
    tm = x_ref.shape[0]
    half = x_ref.shape[1] // 2
    g0 = gate_ref[:, 0:1]
    g1 = gate_ref[:, 1:2]
    for s in range(8):
        wa = a_ref[pl.ds(s, tm, stride=8), :]
        wb = b_ref[pl.ds(s, tm, stride=8), :]
        lo = slice(s * LANES, (s + 1) * LANES)
        hi = slice(half + s * LANES, half + (s + 1) * LANES)
        o_ref[:, lo] = x_ref[:, lo] + (g0 * _unpack_lo(wa) + g1 * _unpack_lo(wb))
        o_ref[:, hi] = x_ref[:, hi] + (g0 * _unpack_hi(wa) + g1 * _unpack_hi(wb))


def _combine(x2d, gates, out2):
    n, d = x2d.shape
    tm = _row_tile(n, 688)
    return pl.pallas_call(
        _combine_kernel,
        grid=(n // tm,),
        in_specs=[
            pl.BlockSpec((tm, d), lambda i: (i, 0)),
            pl.BlockSpec((tm, LANES), lambda i: (i, 0)),
            pl.BlockSpec((tm * 8, LANES), lambda i: (i, 0)),
            pl.BlockSpec((tm * 8, LANES), lambda i: (n // tm + i, 0)),
        ],
        out_specs=pl.BlockSpec((tm, d), lambda i: (i, 0)),
        out_shape=jax.ShapeDtypeStruct((n, d), F32),
        compiler_params=_params("parallel"),
        name="moe_combine",
    )(x2d, gates, out2, out2)


def _even_layer(x, norm_g, w_in, conv_w, conv_b, b_i, b_f, a_norm, w_gate2, b_gate, b_norm, w_out):
    b, t, d = x.shape
    n = b * t
    a_w = 2 * A_HEADS * A_DK + 2 * A_HEADS * A_DV
    g_w = 2 * A_HEADS
    b_w = 2 * B_HEADS * B_DK + 2 * B_HEADS * B_DV
    main = a_w + b_w
    gate_cols = g_w + GATE_RANK
    w = _even_weight(w_in, a_w, g_w, b_w, GATE_RANK)
    z = _normproj(x.reshape(n, d), norm_g, w).reshape(b, t, main + MXU_DIM)

    ya = _mlstm(z, main, conv_w, conv_b, b_i, b_f, a_norm)

    wg = jnp.zeros((B_HEADS, MXU_DIM, B_DK), F32).at[:, g_w:g_w + GATE_RANK, :].set(
        w_gate2.reshape(GATE_RANK, B_HEADS, B_DK).transpose(1, 0, 2))
    dummy = jnp.zeros((1, B_HEADS * B_DK), F32)
    hp = 2
    hspec = pl.BlockSpec((1, hp * B_DK), lambda bi, g: (0, g))
    yb = _gla_call(
        z, B_HEADS, hp, B_DK, B_DV,
        (a_w, a_w + B_HEADS * B_DK, a_w + 2 * B_HEADS * B_DK, a_w + 2 * B_HEADS * B_DK + B_HEADS * B_DV, main),
        MXU_DIM,
        ((wg, pl.BlockSpec((hp, MXU_DIM, B_DK), lambda bi, g: (g, 0, 0))),
         (b_gate.reshape(1, -1), hspec), (dummy, hspec)),
        b_norm, "gla")
    return _outproj(ya.reshape(n, -1), yb.reshape(n, -1), w_out.astype(BF16), x.reshape(n, d)).reshape(b, t, d)


def _odd_layer(x, lb, norm_g, w_in, c_norm, q_a_norm, w_q_up, kv_a_norm, w_kv_up, q_norm, k_norm, w_out):
    b, t, d = x.shape
    n = b * t
    c_w = 2 * C_HEADS * C_DK + 2 * C_HEADS * C_DV
    swap = (jnp.arange(D_ROPE) + D_ROPE // 2) % D_ROPE
    kr0 = c_w + Q_LORA + KV_LORA
    used = kr0 + 2 * D_ROPE
    total = -(-used // MXU_DIM) * MXU_DIM
    z2 = _normproj(x.reshape(n, d), norm_g, _odd_weight(w_in, kr0))
    z = z2.reshape(b, t, total)

    hp = 4
    hspec = pl.BlockSpec((1, hp * C_DK), lambda bi, g: (0, g))
    yc = _gla_call(
        z, C_HEADS, hp, C_DK, C_DV,
        (0, C_HEADS * C_DK, 2 * C_HEADS * C_DK, 2 * C_HEADS * C_DK + C_HEADS * C_DV, C_HEADS * C_DK),
        C_DK,
        ((jnp.log(lb).reshape(1, -1), hspec), (jnp.log1p(-lb).reshape(1, -1), hspec), ((1.0 - lb).reshape(1, -1), hspec)),
        c_norm, "hgrn")

    dq = D_NOPE + D_ROPE
    wq = w_q_up.reshape(Q_LORA, D_HEADS, dq)
    wq_rope = wq[:, :, D_NOPE:]
    wq_p = jnp.concatenate([wq[:, :, :D_NOPE].reshape(Q_LORA, -1),
                            jnp.concatenate([wq_rope, wq_rope[:, :, swap]], axis=-1).reshape(Q_LORA, -1)],
                           axis=1).astype(BF16)
    wkv = w_kv_up.reshape(KV_LORA, D_HEADS, D_NOPE + D_V)
    wkv_p = jnp.concatenate([wkv[:, :, :D_NOPE].reshape(KV_LORA, -1), wkv[:, :, D_NOPE:].reshape(KV_LORA, -1)],
                            axis=1).astype(BF16)
    qn = _normproj(z2, q_a_norm, wq_p, x_col_block=c_w // Q_LORA).reshape(b, t, -1)
    kvn = _normproj(z2, kv_a_norm, wkv_p, x_col_block=(c_w + Q_LORA) // KV_LORA).reshape(b, t, -1)

    pos = jnp.arange(t, dtype=F32)
    half = D_ROPE // 2
    inv = ROPE_THETA ** (-jnp.arange(half, dtype=F32) / half)
    ang = pos[:, None] * inv[None, :]
    cos, sin = jnp.cos(ang), jnp.sin(ang)
    tab = jnp.concatenate([cos, cos, -sin, sin], axis=1)
    pair = lambda g: jnp.concatenate([g[D_NOPE:], g[D_NOPE:][swap]]).reshape(1, LANES)
    yd = _mla(qn, kvn, z, kr0, tab, q_norm[:D_NOPE].reshape(1, LANES), pair(q_norm),
              k_norm[:D_NOPE].reshape(1, LANES), pair(k_norm))
    return _outproj(yc.reshape(n, -1), yd.reshape(n, -1), w_out.astype(BF16), x.reshape(n, d)).reshape(b, t, d)


def kernel(x, meta_tokens, ab_norm, ab_w_in, a_conv_w, a_conv_b, a_b_i, a_b_f, a_head_norm, b_w_gate2, b_b_gate, b_head_norm, ab_w_out, cd_norm, cd_w_in, c_lower_bound, c_head_norm, d_q_a_norm, d_w_q_up, d_kv_a_norm, d_w_kv_up, d_q_norm, d_k_norm, cd_w_out, moe_norm, moe_w_group, moe_b_group, moe_w_expert, moe_b_expert, moe_w1, moe_w3, moe_w2):
    b = x.shape[0]
    depth = moe_norm.shape[0]
    h = jnp.concatenate([jnp.broadcast_to(meta_tokens.astype(x.dtype)[None], (b, N_META, D_MODEL)), x], axis=1)
    t = h.shape[1]
    lb_cum = jnp.cumsum(jax.nn.softmax(c_lower_bound.astype(F32), axis=0), axis=0)
    lower_bounds = lb_cum - lb_cum[0]
    for layer in range(depth):
        j = layer // 2
        if layer % 2 == 0:
            h = _even_layer(h, ab_norm[j], ab_w_in[j], a_conv_w[j], a_conv_b[j], a_b_i[j], a_b_f[j], a_head_norm[j],
                            b_w_gate2[j], b_b_gate[j], b_head_norm[j], ab_w_out[j])
        else:
            h = _odd_layer(h, lower_bounds[layer], cd_norm[j], cd_w_in[j], c_head_norm[j], d_q_a_norm[j],
                           d_w_q_up[j], d_kv_a_norm[j], d_w_kv_up[j], d_q_norm[j], d_k_norm[j], cd_w_out[j])
        h = _moe(h.reshape(b * t, D_MODEL), moe_norm[layer], moe_w_group[layer], moe_b_group[layer],
                 moe_w_expert[layer], moe_b_expert[layer], moe_w1, moe_w3, moe_w2, layer).reshape(b, t, D_MODEL)
    return h[:, N_META:]
```

```python
import functools
import math

import jax
import jax.numpy as jnp
from jax import lax
from jax.experimental import pallas as pl
from jax.experimental.pallas import tpu as pltpu

F32 = jnp.float32
BF16 = jnp.bfloat16
HIGHEST = lax.Precision.HIGHEST

D_MODEL = 2048
N_META = 16
CHUNK = 64
CONV_K = 4
EPS = 1e-6
A_HEADS, A_DK, A_DV = 4, 128, 256
B_HEADS, B_DK, B_DV = 4, 128, 256
GATE_RANK = 16
GATE_TAU = 16.0
C_HEADS, C_DK, C_DV = 8, 128, 128
D_HEADS, D_NOPE, D_ROPE, D_V = 8, 128, 64, 128
Q_LORA, KV_LORA = 512, 256
ROPE_THETA = 10000.0
N_GROUPS, EXPERTS_PER_GROUP = 4, 8
N_EXPERTS = N_GROUPS * EXPERTS_PER_GROUP
TOP_K = 2
D_EXPERT = 512

LANES = 128
MXU_DIM = 256
BF16_ROWS = 16
VMEM_LIMIT = 56 * 1024 * 1024
MOE_BLOCK = MXU_DIM
ATT_BLOCK = 256
ATT_PAD = ATT_BLOCK - N_META

_NT = (((1,), (1,)), ((), ()))
_TN = (((0,), (0,)), ((), ()))


def _dot(a, b, precision=None):
    return jnp.dot(a, b, preferred_element_type=F32, precision=precision)


def _dot_nt(a, b):
    return lax.dot_general(a, b, _NT, preferred_element_type=F32)


def _dot_tn(a, b):
    return lax.dot_general(a, b, _TN, preferred_element_type=F32)


def _bf(x):
    return x.astype(BF16)


def _split3(x):
    hi = _bf(x)
    rest = x - hi.astype(F32)
    mid = _bf(rest)
    return hi, mid, _bf(rest - mid.astype(F32))


def _log_sigmoid(x):
    return jnp.minimum(x, 0.0) - jnp.log1p(jnp.exp(-jnp.abs(x)))


def _sigmoid(x):
    return 1.0 / (1.0 + jnp.exp(-x))


def _silu(x):
    return x * _sigmoid(x)


def _row_tile(m, cap):
    best = None
    for t in range(BF16_ROWS, min(m, cap) + 1, BF16_ROWS):
        if m % t == 0:
            best = t
    assert best is not None, m
    return best


def _col_tile(n, cap):
    best = None
    for t in range(MXU_DIM, min(n, cap) + 1, MXU_DIM):
        if n % t == 0:
            best = t
    assert best is not None, n
    return best


def _params(*sem):
    return pltpu.CompilerParams(dimension_semantics=sem, vmem_limit_bytes=VMEM_LIMIT)


_RELAYOUT_ROWS = 256


def _even_weight_kernel(wa_ref, wb_ref, wc_ref, o_ref, *, n_plain, n_shift, shift, gate_cols):
    ob = pl.program_id(0)
    rows = o_ref.shape[0]
    chunks = rows // _RELAYOUT_ROWS

    @pl.when(ob < n_plain)
    def _():
        def body(c, carry):
            r = pl.ds(pl.multiple_of(c * _RELAYOUT_ROWS, _RELAYOUT_ROWS), _RELAYOUT_ROWS)
            o_ref[r, :] = _bf(wa_ref[r, :])
            return carry

        lax.fori_loop(0, chunks, body, 0)

    @pl.when((ob >= n_plain) & (ob < n_plain + n_shift))
    def _():
        def body(c, carry):
            r = pl.ds(pl.multiple_of(c * _RELAYOUT_ROWS, _RELAYOUT_ROWS), _RELAYOUT_ROWS)
            wide = jnp.concatenate([wa_ref[r, :], wb_ref[r, :]], axis=1)
            o_ref[r, :] = _bf(wide[:, shift:shift + MXU_DIM])
            return carry

        lax.fori_loop(0, chunks, body, 0)

    @pl.when(ob == n_plain + n_shift)
    def _():
        lane = lax.broadcasted_iota(jnp.int32, (_RELAYOUT_ROWS, LANES), 1)

        def body(c, carry):
            r = pl.ds(pl.multiple_of(c * _RELAYOUT_ROWS, _RELAYOUT_ROWS), _RELAYOUT_ROWS)
            first = jnp.where(lane < shift, wc_ref[r, :], jnp.where(lane < gate_cols, wb_ref[r, :], 0.0))
            o_ref[r, :] = _bf(jnp.concatenate([first, jnp.zeros_like(first)], axis=1))
            return carry

        lax.fori_loop(0, chunks, body, 0)


def _even_weight(w_in, a_w, g_w, b_w, rank):
    d = w_in.shape[0]
    assert a_w % MXU_DIM == 0 and b_w % MXU_DIM == 0 and g_w + rank <= LANES and d % _RELAYOUT_ROWS == 0
    n_plain, n_shift = a_w // MXU_DIM, b_w // MXU_DIM
    n_out = n_plain + n_shift + 1
    last = n_out - 1

    def b_index(ob):
        return (0, jnp.where(ob < n_plain, 0, jnp.where(ob < last, 2 * (ob + 1), (a_w + g_w + b_w) // LANES)))

    return pl.pallas_call(
        functools.partial(_even_weight_kernel, n_plain=n_plain, n_shift=n_shift, shift=g_w, gate_cols=g_w + rank),
        grid=(n_out,),
        in_specs=[
            pl.BlockSpec((d, MXU_DIM), lambda ob: (0, jnp.minimum(ob, last - 1))),
            pl.BlockSpec((d, LANES), b_index),
            pl.BlockSpec((d, LANES), lambda ob: (0, a_w // LANES)),
        ],
        out_specs=pl.BlockSpec((d, MXU_DIM), lambda ob: (0, ob)),
        out_shape=jax.ShapeDtypeStruct((d, n_out * MXU_DIM), BF16),
        compiler_params=_params("parallel"),
        name="even_weight",
    )(w_in, w_in, w_in)


def _odd_weight_kernel(w_ref, o_ref, *, n_plain):
    ob = pl.program_id(0)
    chunks = o_ref.shape[0] // _RELAYOUT_ROWS
    half = D_ROPE // 2

    def body(c, carry):
        r = pl.ds(pl.multiple_of(c * _RELAYOUT_ROWS, _RELAYOUT_ROWS), _RELAYOUT_ROWS)
        w = w_ref[r, :]

        @pl.when(ob < n_plain)
        def _():
            o_ref[r, :] = _bf(w)

        @pl.when(ob == n_plain)
        def _():
            pair = jnp.concatenate([w[:, :D_ROPE], w[:, half:D_ROPE], w[:, :half]], axis=1)
            o_ref[r, :] = _bf(jnp.concatenate([pair, jnp.zeros_like(pair)], axis=1))

        return carry

    lax.fori_loop(0, chunks, body, 0)


def _odd_weight(w_in, kr0):
    d = w_in.shape[0]
    assert kr0 % MXU_DIM == 0 and w_in.shape[1] == kr0 + D_ROPE and d % _RELAYOUT_ROWS == 0
    n_plain = kr0 // MXU_DIM
    return pl.pallas_call(
        functools.partial(_odd_weight_kernel, n_plain=n_plain),
        grid=(n_plain + 1,),
        in_specs=[pl.BlockSpec((d, MXU_DIM), lambda ob: (0, ob))],
        out_specs=pl.BlockSpec((d, MXU_DIM), lambda ob: (0, ob)),
        out_shape=jax.ShapeDtypeStruct((d, kr0 + MXU_DIM), BF16),
        compiler_params=_params("parallel"),
        name="odd_weight",
    )(w_in)


def _normproj_kernel(x_ref, g_ref, w_ref, o_ref, xs_ref):
    tm = xs_ref.shape[0]

    @pl.when(pl.program_id(1) == 0)
    def _():
        def body(c, carry):
            r0 = pl.multiple_of(c * BF16_ROWS, BF16_ROWS)
            x = x_ref[pl.ds(r0, BF16_ROWS), :]
            ms = jnp.mean(x * x, axis=-1, keepdims=True)
            xs_ref[pl.ds(r0, BF16_ROWS), :] = _bf(x * lax.rsqrt(ms + EPS) * g_ref[...])
            return carry

        lax.fori_loop(0, tm // BF16_ROWS, body, 0, unroll=8)

    o_ref[...] = _dot(xs_ref[...], w_ref[...]).astype(o_ref.dtype)


def _normproj(x2d, gain, w, *, x_col_block=0, out_dtype=F32):
    m = x2d.shape[0]
    k, n = w.shape
    tm = _row_tile(m, 688)
    tn = _col_tile(n, 1280)
    return pl.pallas_call(
        _normproj_kernel,
        grid=(m // tm, n // tn),
        in_specs=[
            pl.BlockSpec((tm, k), lambda i, j: (i, x_col_block)),
            pl.BlockSpec((1, k), lambda i, j: (0, 0)),
            pl.BlockSpec((k, tn), lambda i, j: (0, j)),
        ],
        out_specs=pl.BlockSpec((tm, tn), lambda i, j: (i, j)),
        out_shape=jax.ShapeDtypeStruct((m, n), out_dtype),
        scratch_shapes=[pltpu.VMEM((tm, k), BF16)],
        compiler_params=_params("parallel", "arbitrary"),
        name="normproj",
    )(x2d, gain.reshape(1, k).astype(F32), w)


def _outproj_kernel(ya_ref, yb_ref, w_ref, r_ref, o_ref):
    ka = ya_ref.shape[1]
    acc = _dot(ya_ref[...], w_ref[:ka, :]) + _dot(yb_ref[...], w_ref[ka:, :])
    o_ref[...] = r_ref[...] + acc


def _outproj(ya, yb, w, res):
    m, ka = ya.shape
    kb = yb.shape[1]
    n = w.shape[1]
    tm = _row_tile(m, 1376)
    tn = _col_tile(n, 1024)
    return pl.pallas_call(
        _outproj_kernel,
        grid=(m // tm, n // tn),
        in_specs=[
            pl.BlockSpec((tm, ka), lambda i, j: (i, 0)),
            pl.BlockSpec((tm, kb), lambda i, j: (i, 0)),
            pl.BlockSpec((ka + kb, tn), lambda i, j: (0, j)),
            pl.BlockSpec((tm, tn), lambda i, j: (i, j)),
        ],
        out_specs=pl.BlockSpec((tm, tn), lambda i, j: (i, j)),
        out_shape=jax.ShapeDtypeStruct((m, n), F32),
        compiler_params=_params("parallel", "arbitrary"),
        name="outproj",
    )(ya, yb, w, res)


def _mlstm_kernel(bi_ref, bf_ref, q_ref, k_ref, v_ref, og_ref, gt_ref,
                  cwq_ref, cwk_ref, cbq_ref, cbk_ref, hn_ref, o_ref, c_ref, n_ref, m_ref, *, heads):
    t_total = q_ref.shape[1]
    n_chunks = (t_total - N_META) // CHUNK
    head0 = pl.program_id(1) * heads

    c_ref[...] = jnp.zeros_like(c_ref)
    n_ref[...] = jnp.zeros_like(n_ref)
    m_ref[...] = jnp.zeros_like(m_ref)

    def conv(win, cw, cb, length):
        y = cb
        for j in range(CONV_K):
            y = y + win[8 - (CONV_K - 1) + j:8 - (CONV_K - 1) + j + length, :] * cw[j:j + 1, :]
        return _silu(y)

    hs = range(heads)
    kcol = [slice(hh * A_DK, (hh + 1) * A_DK) for hh in hs]
    vcol = [slice(hh * A_DV, (hh + 1) * A_DV) for hh in hs]

    def gates(hh, blk, blk_parts, length, causal, upper):
        head = head0 + hh
        b_i = bi_ref[head]
        b_f = bf_ref[head]
        lane = lax.broadcasted_iota(jnp.int32, (length, LANES), 1)
        ig_c = jnp.sum(jnp.where(lane == head, blk, 0.0), axis=1, keepdims=True) + b_i
        lf_c = _log_sigmoid(jnp.sum(jnp.where(lane == A_HEADS + head, blk, 0.0), axis=1, keepdims=True) + b_f)
        sel_r = lax.broadcasted_iota(jnp.int32, (8, LANES), 0)
        sel_l = lax.broadcasted_iota(jnp.int32, (8, LANES), 1)
        sel = jnp.where(sel_l == head + A_HEADS * sel_r, 1.0, 0.0).astype(BF16)
        rows = _dot_nt(sel, blk_parts[0]) + (_dot_nt(sel, blk_parts[1]) + _dot_nt(sel, blk_parts[2]))
        ig_r = rows[0:1, :] + b_i
        lf_r = _log_sigmoid(rows[1:2, :] + b_f)
        b_c = jnp.sum(jnp.where(causal, lf_r, 0.0), axis=1, keepdims=True)
        b_r = jnp.sum(jnp.where(upper, lf_c, 0.0), axis=0, keepdims=True)
        b_end = b_c[length - 1:length, :]
        w_end = b_end - b_c + ig_c
        m_loc = jnp.max(w_end, axis=0, keepdims=True)
        d = jnp.where(causal, b_c - b_r + ig_r, -jnp.inf)
        return b_c, b_end, jnp.exp(w_end - m_loc), m_loc, d, jnp.max(d, axis=1, keepdims=True)

    def chunk(o, length, qwin, kwin):
        row = lax.broadcasted_iota(jnp.int32, (length, length), 0)
        col = lax.broadcasted_iota(jnp.int32, (length, length), 1)
        causal = col <= row
        blk = gt_ref[0, pl.ds(o, length), 0:LANES]
        blk_parts = _split3(blk)
        gt = [gates(hh, blk, blk_parts, length, causal, row <= col) for hh in hs]
        q = [conv(qwin[hh], cwq_ref[:, kcol[hh]], cbq_ref[:, kcol[hh]], length) for hh in hs]
        k = [conv(kwin[hh], cwk_ref[:, kcol[hh]], cbk_ref[:, kcol[hh]], length) * (A_DK ** -0.5) for hh in hs]
        vb = [_bf(v_ref[0, pl.ds(o, length), vcol[hh]]) for hh in hs]
        qb = [_bf(q[hh]) for hh in hs]
        k_w = [k[hh] * gt[hh][2] for hh in hs]
        qk = [_dot_nt(qb[hh], _bf(k[hh])) for hh in hs]
        c_in = [c_ref[hh] for hh in hs]
        q_c = [_dot(qb[hh], _bf(c_in[hh])) for hh in hs]
        c_loc = [_dot_tn(_bf(k_w[hh]), vb[hh]) for hh in hs]
        s, a_t, m_t, q_n = [], [], [], []
        for hh in hs:
            b_c, b_end, _, m_loc, d, d_max = gt[hh]
            m_in = m_ref[hh]
            inter = b_c + m_in
            m_t.append(jnp.maximum(inter, d_max))
            s.append(qk[hh] * jnp.exp(d - m_t[hh]))
            a_t.append(jnp.exp(inter - m_t[hh]))
            m_new = jnp.maximum(b_end + m_in, m_loc)
            a = jnp.exp(b_end + m_in - m_new)
            c = jnp.exp(m_loc - m_new)
            n_in = n_ref[hh]
            c_ref[hh] = a * c_in[hh] + c * c_loc[hh]
            n_ref[hh] = a * n_in + c * jnp.sum(k_w[hh], axis=0, keepdims=True)
            m_ref[hh] = m_new
            q_n.append(jnp.sum(q[hh] * n_in, axis=1, keepdims=True))
        num = [_dot(_bf(s[hh]), vb[hh]) + a_t[hh] * q_c[hh] for hh in hs]
        for hh in hs:
            den = jnp.sum(s[hh], axis=1, keepdims=True) + a_t[hh] * q_n[hh]
            h = num[hh] / jnp.maximum(jnp.abs(den), jnp.exp(-m_t[hh]))
            hn = h * lax.rsqrt(jnp.mean(h * h, axis=-1, keepdims=True) + EPS) * hn_ref[:, vcol[hh]]
            y = _sigmoid(og_ref[0, pl.ds(o, length), vcol[hh]]) * hn
            o_ref[0, pl.ds(o, length), vcol[hh]] = y.astype(o_ref.dtype)

    zeros8 = jnp.zeros((8, A_DK), F32)
    chunk(0, N_META,
          [jnp.concatenate([zeros8, q_ref[0, 0:N_META, kcol[hh]]], axis=0) for hh in hs],
          [jnp.concatenate([zeros8, k_ref[0, 0:N_META, kcol[hh]]], axis=0) for hh in hs])

    def body(c, carry):
        o = pl.multiple_of(N_META + c * CHUNK, BF16_ROWS)
        w0 = pl.multiple_of(N_META - 8 + c * CHUNK, 8)
        chunk(o, CHUNK, [q_ref[0, pl.ds(w0, CHUNK + 8), kcol[hh]] for hh in hs],
              [k_ref[0, pl.ds(w0, CHUNK + 8), kcol[hh]] for hh in hs])
        return carry

    lax.fori_loop(0, n_chunks, body, 0)


def _mlstm(z, gate_col, conv_w, conv_b, b_i, b_f, head_norm, heads=2):
    b, t, _ = z.shape
    hk = A_HEADS * A_DK
    wk, wv = heads * A_DK, heads * A_DV
    smem = pl.BlockSpec(memory_space=pltpu.SMEM)
    col = lambda width, off: (lambda bi, g: (bi, 0, off // width + g))
    return pl.pallas_call(
        functools.partial(_mlstm_kernel, heads=heads),
        grid=(b, A_HEADS // heads),
        in_specs=[
            smem, smem,
            pl.BlockSpec((1, t, wk), col(wk, 0)),
            pl.BlockSpec((1, t, wk), col(wk, hk)),
            pl.BlockSpec((1, t, wv), col(wv, 2 * hk)),
            pl.BlockSpec((1, t, wv), col(wv, 2 * hk + A_HEADS * A_DV)),
            pl.BlockSpec((1, t, MXU_DIM), lambda bi, g: (bi, 0, gate_col // MXU_DIM)),
            pl.BlockSpec((CONV_K, wk), lambda bi, g: (0, g)),
            pl.BlockSpec((CONV_K, wk), lambda bi, g: (0, A_HEADS // heads + g)),
            pl.BlockSpec((1, wk), lambda bi, g: (0, g)),
            pl.BlockSpec((1, wk), lambda bi, g: (0, A_HEADS // heads + g)),
            pl.BlockSpec((1, wv), lambda bi, g: (0, g)),
        ],
        out_specs=pl.BlockSpec((1, t, wv), lambda bi, g: (bi, 0, g)),
        out_shape=jax.ShapeDtypeStruct((b, t, A_HEADS * A_DV), BF16),
        scratch_shapes=[pltpu.VMEM((heads, A_DK, A_DV), F32), pltpu.VMEM((heads, 1, A_DK), F32),
                        pltpu.VMEM((heads, 1, 1), F32)],
        compiler_params=_params("parallel", "parallel"),
        name="mlstm",
    )(b_i, b_f, z, z, z, z, z, conv_w, conv_w, conv_b.reshape(1, -1), conv_b.reshape(1, -1),
      head_norm.reshape(1, -1))


def _gla_kernel(q_ref, k_ref, v_ref, og_ref, g_ref, p0_ref, p1_ref, p2_ref, hn_ref, o_ref, st_ref, *, mode, heads):
    t_total = q_ref.shape[1]
    n_chunks = (t_total - N_META) // CHUNK
    dv, dk = st_ref.shape[1:]
    st_ref[...] = jnp.zeros_like(st_ref)

    hs = range(heads)
    kcol = [slice(hh * dk, (hh + 1) * dk) for hh in hs]
    vcol = [slice(hh * dv, (hh + 1) * dv) for hh in hs]

    def gate_inputs(hh, o, length):
        q = q_ref[0, pl.ds(o, length), kcol[hh]]
        if mode == "gla":
            pre = _dot(g_ref[0, pl.ds(o, length), :], p0_ref[hh], precision=HIGHEST) + p1_ref[:, kcol[hh]]
            return q * (dk ** -0.5), k_ref[0, pl.ds(o, length), kcol[hh]], _log_sigmoid(pre) / GATE_TAU
        fpre = g_ref[0, pl.ds(o, length), kcol[hh]]
        a = p0_ref[:, kcol[hh]]
        bb = p1_ref[:, kcol[hh]] + _log_sigmoid(fpre)
        lg = jnp.maximum(a, bb) + jnp.log1p(jnp.exp(-jnp.abs(a - bb)))
        return q, p2_ref[:, kcol[hh]] * _sigmoid(-fpre), lg

    def cumsum_time(tri, lg):
        parts = _dot(tri, jnp.concatenate(_split3(lg), axis=1))
        return parts[:, :dk] + (parts[:, dk:2 * dk] + parts[:, 2 * dk:])

    def chunk(o, length):
        row = lax.broadcasted_iota(jnp.int32, (length, length), 0)
        col = lax.broadcasted_iota(jnp.int32, (length, length), 1)
        causal = col <= row
        tri = jnp.where(causal, 1.0, 0.0).astype(BF16)
        qkl = [gate_inputs(hh, o, length) for hh in hs]
        vb = [_bf(v_ref[0, pl.ds(o, length), vcol[hh]]) for hh in hs]
        g = [cumsum_time(tri, qkl[hh][2]) for hh in hs]
        g_end = [g[hh][length - 1:length, :] for hh in hs]
        g_mid = [g[hh][length // 2:length // 2 + 1, :] for hh in hs]
        s = [_dot_nt(_bf(qkl[hh][0] * jnp.exp(g[hh] - g_mid[hh])), _bf(qkl[hh][1] * jnp.exp(g_mid[hh] - g[hh])))
             for hh in hs]
        st_in = [st_ref[hh] for hh in hs]
        inter = [_dot_nt(_bf(qkl[hh][0] * jnp.exp(g[hh])), _bf(st_in[hh])) for hh in hs]
        local = [_dot_tn(vb[hh], _bf(qkl[hh][1] * jnp.exp(g_end[hh] - g[hh]))) for hh in hs]
        for hh in hs:
            st_ref[hh] = st_in[hh] * jnp.exp(g_end[hh]) + local[hh]
        out = [_dot(_bf(jnp.where(causal, s[hh], 0.0)), vb[hh]) + inter[hh] for hh in hs]
        for hh in hs:
            hn = out[hh] * lax.rsqrt(jnp.mean(out[hh] * out[hh], axis=-1, keepdims=True) + EPS) * hn_ref[:, vcol[hh]]
            og = og_ref[0, pl.ds(o, length), vcol[hh]]
            gate = _silu(og) if mode == "gla" else _sigmoid(og)
            o_ref[0, pl.ds(o, length), vcol[hh]] = (gate * hn).astype(o_ref.dtype)

    chunk(0, N_META)

    def body(c, carry):
        chunk(pl.multiple_of(N_META + c * CHUNK, BF16_ROWS), CHUNK)
        return carry

    lax.fori_loop(0, n_chunks, body, 0)


def _gla_call(z, n_heads, heads, dk, dv, blocks, gate_width, params, head_norm, mode):
    b, t, _ = z.shape
    q0, k0, v0, og0, g0 = blocks
    zspec = lambda width, off, grouped=True: pl.BlockSpec(
        (1, t, width), (lambda bi, g: (bi, 0, off // width + (g if grouped else 0))))
    (p0, s0), (p1, s1), (p2, s2) = params
    gate_spec = zspec(gate_width, g0, grouped=False) if mode == "gla" else zspec(heads * dk, g0)
    return pl.pallas_call(
        functools.partial(_gla_kernel, mode=mode, heads=heads),
        grid=(b, n_heads // heads),
        in_specs=[
            zspec(heads * dk, q0), zspec(heads * dk, k0), zspec(heads * dv, v0), zspec(heads * dv, og0),
            gate_spec, s0, s1, s2,
            pl.BlockSpec((1, heads * dv), lambda bi, g: (0, g)),
        ],
        out_specs=pl.BlockSpec((1, t, heads * dv), lambda bi, g: (bi, 0, g)),
        out_shape=jax.ShapeDtypeStruct((b, t, n_heads * dv), BF16),
        scratch_shapes=[pltpu.VMEM((heads, dv, dk), F32)],
        compiler_params=_params("parallel", "parallel"),
        name="gla_" + mode,
    )(z, z, z, z, z, p0, p1, p2, head_norm.reshape(1, -1))


def _mla_kernel(qn_ref, qr_ref, kn_ref, v_ref, kr_ref, tab_ref, gqn_ref, gqr_ref, gkn_ref, gkr_ref,
                o_ref, qf_ref, kf_ref, vf_ref):
    t_total = qn_ref.shape[1]
    n_blocks = (ATT_PAD + t_total) // ATT_BLOCK
    dqk = D_NOPE + D_ROPE
    scale = dqk ** -0.5
    rows = _row_tile(t_total, 768)
    lane = lax.broadcasted_iota(jnp.int32, (rows, LANES), 1)
    first_half = lane < D_ROPE

    qf_ref[0:ATT_PAD, :] = jnp.zeros((ATT_PAD, 2 * LANES), BF16)
    kf_ref[0:ATT_PAD, :] = jnp.zeros((ATT_PAD, 2 * LANES), BF16)
    vf_ref[0:ATT_PAD, :] = jnp.zeros((ATT_PAD, D_V), BF16)

    def rope_pair(x, gains, tab):
        p = x * gains * tab
        return p + pltpu.roll(p, D_ROPE, 1)

    def prep(c):
        r0 = c * rows
        dst = ATT_PAD + c * rows
        tab = tab_ref[pl.ds(r0, rows), :]
        qn = qn_ref[0, pl.ds(r0, rows), :]
        qr = qr_ref[0, pl.ds(r0, rows), :]
        ssq = jnp.sum(qn * qn, axis=-1, keepdims=True) + 0.5 * jnp.sum(qr * qr, axis=-1, keepdims=True)
        rq = lax.rsqrt(ssq / dqk + EPS) * scale
        qf_ref[pl.ds(dst, rows), 0:LANES] = _bf(qn * gqn_ref[...] * rq)
        qf_ref[pl.ds(dst, rows), LANES:2 * LANES] = _bf(rope_pair(qr, gqr_ref[...], tab) * rq)
        kn = kn_ref[0, pl.ds(r0, rows), :]
        kr = kr_ref[0, pl.ds(r0, rows), :]
        ssk = jnp.sum(kn * kn, axis=-1, keepdims=True) + 0.5 * jnp.sum(kr * kr, axis=-1, keepdims=True)
        rk = lax.rsqrt(ssk / dqk + EPS)
        kf_ref[pl.ds(dst, rows), 0:LANES] = _bf(kn * gkn_ref[...] * rk)
        kf_ref[pl.ds(dst, rows), LANES:2 * LANES] = _bf(
            jnp.where(first_half, rope_pair(kr, gkr_ref[...], tab) * rk, 0.0))
        vf_ref[pl.ds(dst, rows), :] = _bf(v_ref[0, pl.ds(r0, rows), :])

    for c in range(t_total // rows):
        prep(c)

    qpos = lax.broadcasted_iota(jnp.int32, (ATT_BLOCK, ATT_BLOCK), 0)
    kpos = lax.broadcasted_iota(jnp.int32, (ATT_BLOCK, ATT_BLOCK), 1)
    neg = -jnp.inf

    for qi in range(n_blocks):
        q = qf_ref[qi * ATT_BLOCK:(qi + 1) * ATT_BLOCK, :]
        s = _dot_nt(q, kf_ref[0:(qi + 1) * ATT_BLOCK, :])
        parts = [s[:, j * ATT_BLOCK:(j + 1) * ATT_BLOCK] for j in range(qi + 1)]
        parts[0] = jnp.where(kpos >= ATT_PAD, parts[0], neg)
        parts[qi] = jnp.where(kpos <= qpos, parts[qi], neg)
        top = functools.reduce(jnp.maximum, parts)
        m = jnp.max(top, axis=-1, keepdims=True)
        if qi == 0:
            m = jnp.where(m == neg, 0.0, m)
        probs = [jnp.exp(part - m) for part in parts]
        l = jnp.sum(functools.reduce(jnp.add, probs), axis=-1, keepdims=True)
        pv = _dot(jnp.concatenate([_bf(pr) for pr in probs], axis=1), vf_ref[0:(qi + 1) * ATT_BLOCK, :])
        if qi == 0:
            out = pv / jnp.where(l == 0.0, 1.0, l)
            o_ref[0, 0:N_META, :] = out[ATT_PAD:, :].astype(o_ref.dtype)
        else:
            dst = qi * ATT_BLOCK - ATT_PAD
            o_ref[0, dst:dst + ATT_BLOCK, :] = (pv / l).astype(o_ref.dtype)


def _mla(qn, kvn, z, kr_col, tab, gqn, gqr, gkn, gkr):
    b, t, _ = qn.shape
    assert (ATT_PAD + t) % ATT_BLOCK == 0 and t % (3 * BF16_ROWS) == 0
    tp = ATT_PAD + t
    hspec = lambda off: pl.BlockSpec((1, t, LANES), lambda bi, h: (bi, 0, off + h))
    gspec = pl.BlockSpec((1, LANES), lambda bi, h: (0, 0))
    return pl.pallas_call(
        _mla_kernel,
        grid=(b, D_HEADS),
        in_specs=[
            hspec(0), hspec(D_HEADS), hspec(0), hspec(D_HEADS),
            pl.BlockSpec((1, t, LANES), lambda bi, h: (bi, 0, kr_col // LANES)),
            pl.BlockSpec((t, LANES), lambda bi, h: (0, 0)),
            gspec, gspec, gspec, gspec,
        ],
        out_specs=pl.BlockSpec((1, t, D_V), lambda bi, h: (bi, 0, h)),
        out_shape=jax.ShapeDtypeStruct((b, t, D_HEADS * D_V), BF16),
        scratch_shapes=[pltpu.VMEM((tp, 2 * LANES), BF16), pltpu.VMEM((tp, 2 * LANES), BF16),
                        pltpu.VMEM((tp, D_V), BF16)],
        compiler_params=_params("parallel", "parallel"),
        name="mla",
    )(qn, qn, kvn, kvn, z, tab, gqn, gqr, gkn, gkr)


def _pack_bf16_pairs(v):
    w = v.shape[1] // 2
    bits = pltpu.bitcast(_bf(v).astype(F32), jnp.uint32)
    return (bits[:, :w] >> 16) | (bits[:, w:] & jnp.uint32(0xFFFF0000))


def _unpack_lo(words):
    return pltpu.bitcast(words << 16, F32)


def _unpack_hi(words):
    return pltpu.bitcast(words & jnp.uint32(0xFFFF0000), F32)


def _router_kernel(x_ref, g_ref, w_ref, b_ref, gate_ref, idx_ref, xg_ref, cnt_ref, carry_ref):
    tm = x_ref.shape[0]

    @pl.when(pl.program_id(0) == 0)
    def _():
        carry_ref[...] = jnp.zeros_like(carry_ref)

    x = x_ref[...]
    ms = jnp.mean(x * x, axis=-1, keepdims=True)
    xn = x * lax.rsqrt(ms + EPS) * g_ref[...]
    logits = _dot(xn, w_ref[...], precision=HIGHEST) + b_ref[...]
    lane = lax.broadcasted_iota(jnp.int32, logits.shape, 1)
    lane_f = lane.astype(F32)
    neg = -jnp.inf
    big = float(LANES)

    is_group = lane < N_GROUPS
    g_max = jnp.max(jnp.where(is_group, logits, neg), axis=-1, keepdims=True)
    g_sum = jnp.sum(jnp.where(is_group, jnp.exp(logits - g_max), 0.0), axis=-1, keepdims=True)
    p_top = 1.0 / g_sum
    grp = jnp.min(jnp.where(is_group & (logits == g_max), lane_f, big), axis=-1, keepdims=True)

    e_lo = N_GROUPS + grp * EXPERTS_PER_GROUP
    in_grp = (lane_f >= e_lo) & (lane_f < e_lo + EXPERTS_PER_GROUP)
    e_max = jnp.max(jnp.where(in_grp, logits, neg), axis=-1, keepdims=True)
    e_sum = jnp.sum(jnp.where(in_grp, jnp.exp(logits - e_max), 0.0), axis=-1, keepdims=True)
    i1 = jnp.min(jnp.where(in_grp & (logits == e_max), lane_f, big), axis=-1, keepdims=True)
    rest = in_grp & (lane_f != i1)
    e_2nd = jnp.max(jnp.where(rest, logits, neg), axis=-1, keepdims=True)
    i2 = jnp.min(jnp.where(rest & (logits == e_2nd), lane_f, big), axis=-1, keepdims=True)
    p1 = 1.0 / e_sum
    p2 = jnp.exp(e_2nd - e_max) / e_sum
    tot = p1 + p2
    gate_ref[...] = jnp.where(lane == 0, p_top * p1 / tot, jnp.where(lane == 1, p_top * p2 / tot, 0.0))

    e1 = i1 - N_GROUPS
    e2 = i2 - N_GROUPS
    hot = jnp.where((lane_f == e1) | (lane_f == e2), 1.0, 0.0)
    row = lax.broadcasted_iota(jnp.int32, (tm, tm), 0)
    col = lax.broadcasted_iota(jnp.int32, (tm, tm), 1)
    before = _dot(jnp.where(col < row, 1.0, 0.0).astype(BF16), _bf(hot)) + carry_ref[...]
    r1 = jnp.sum(jnp.where(lane_f == e1, before, 0.0), axis=-1, keepdims=True)
    r2 = jnp.sum(jnp.where(lane_f == e2, before, 0.0), axis=-1, keepdims=True)
    total = carry_ref[...] + jnp.sum(hot, axis=0, keepdims=True)
    carry_ref[...] = total
    cnt_ref[...] = jnp.broadcast_to(total, cnt_ref.shape).astype(jnp.int32)
    idx_ref[...] = jnp.where(lane == 0, e1, jnp.where(lane == 1, e2, jnp.where(lane == 2, r1, jnp.where(
        lane == 3, r2, 0.0)))).astype(jnp.int32)

    words = _pack_bf16_pairs(xn)
    for s in range(8):
        xg_ref[pl.ds(s, tm, stride=8), :] = words[:, s * LANES:(s + 1) * LANES]


def _router(x2d, gain, w_group, b_group, w_expert, b_expert):
    m, d = x2d.shape
    assert d == 2 * 8 * LANES
    tm = _row_tile(m, 688)
    pad = LANES - N_GROUPS - N_EXPERTS
    w = jnp.concatenate([w_group, w_expert, jnp.zeros((d, pad), F32)], axis=1)
    bias = jnp.concatenate([b_group, b_expert, jnp.zeros((pad,), F32)]).reshape(1, LANES)
    return pl.pallas_call(
        _router_kernel,
        grid=(m // tm,),
        in_specs=[
            pl.BlockSpec((tm, d), lambda i: (i, 0)),
            pl.BlockSpec((1, d), lambda i: (0, 0)),
            pl.BlockSpec((d, LANES), lambda i: (0, 0)),
            pl.BlockSpec((1, LANES), lambda i: (0, 0)),
        ],
        out_specs=[pl.BlockSpec((tm, LANES), lambda i: (i, 0)), pl.BlockSpec((tm, LANES), lambda i: (i, 0)),
                   pl.BlockSpec((tm * 8, LANES), lambda i: (i, 0)), pl.BlockSpec((8, LANES), lambda i: (0, 0))],
        out_shape=[jax.ShapeDtypeStruct((m, LANES), F32), jax.ShapeDtypeStruct((m, LANES), jnp.int32),
                   jax.ShapeDtypeStruct((m * 8, LANES), jnp.uint32), jax.ShapeDtypeStruct((8, LANES), jnp.int32)],
        scratch_shapes=[pltpu.VMEM((1, LANES), F32)],
        compiler_params=_params("arbitrary"),
        name="router",
    )(x2d, gain.reshape(1, d), w, bias)


def _invert_kernel(dest_ref, inv_ref):
    def clear(s, carry):
        inv_ref[s] = -1
        return carry

    lax.fori_loop(0, inv_ref.shape[0], clear, 0, unroll=8)

    def put(f, carry):
        inv_ref[dest_ref[f]] = f
        return carry

    lax.fori_loop(0, dest_ref.shape[0], put, 0, unroll=8)


def _invert(dest, p):
    assert p % 8 == 0 and dest.shape[0] % 8 == 0
    smem = pl.BlockSpec(memory_space=pltpu.SMEM)
    return pl.pallas_call(
        _invert_kernel, in_specs=[smem], out_specs=smem,
        out_shape=jax.ShapeDtypeStruct((p,), jnp.int32), name="moe_invert",
    )(dest)


def _expert_kernel(be_ref, nxt_ref, run_ref, nu_ref, src_ref, dst_ref, xg_hbm, w1_hbm, w3_hbm, w2_hbm, o_hbm,
                   xbuf, ybuf, xs_ref, w1f, w3f, w2f, w1s, w3s, w2s, sem_in, sem_out, sem_w, *, layer):
    i = pl.program_id(0)
    n_used = nu_ref[0]
    par = i % 2
    half = D_MODEL // 2
    half_e = D_EXPERT // 2

    def weight_copies(expert, slot):
        return [pltpu.make_async_copy(w_hbm.at[layer, expert], w_f.at[slot], sem_w.at[slot, j])
                for j, (w_hbm, w_f) in enumerate(((w1_hbm, w1f), (w3_hbm, w3f), (w2_hbm, w2f)))]

    def gather(block, r, slot):
        src = pl.multiple_of(src_ref[block * MOE_BLOCK + r], 8)
        return pltpu.make_async_copy(xg_hbm.at[pl.ds(src, 8), :], xbuf.at[slot, pl.ds(r * 8, 8), :], sem_in.at[slot])

    def scatter(block, r, slot):
        dst = pl.multiple_of(dst_ref[(block + 1) * MOE_BLOCK + r], 8)
        return pltpu.make_async_copy(ybuf.at[slot, pl.ds(r * 8, 8), :], o_hbm.at[pl.ds(dst, 8), :], sem_out.at[slot])

    def wait_gathers(slot):
        pltpu.make_async_copy(xbuf.at[1 - slot], xbuf.at[slot], sem_in.at[slot]).wait()

    def wait_scatters(slot):
        pltpu.make_async_copy(ybuf.at[slot], ybuf.at[1 - slot], sem_out.at[slot]).wait()

    @pl.when(i == 0)
    def _():
        ybuf[1] = jnp.zeros(ybuf.shape[1:], ybuf.dtype)
        n_real = o_hbm.shape[0] - 2 * MOE_BLOCK * 8
        fill1 = pltpu.make_async_copy(ybuf.at[1], o_hbm.at[pl.ds(n_real + MOE_BLOCK * 8, MOE_BLOCK * 8), :],
                                      sem_out.at[1])
        fill1.start()
        fill1.wait()
        pltpu.make_async_copy(ybuf.at[1], o_hbm.at[pl.ds(n_real, MOE_BLOCK * 8), :], sem_out.at[0]).start()

        def first(r, carry):
            gather(0, r, 0).start()
            return carry

        lax.fori_loop(0, MOE_BLOCK, first, 0, unroll=8)

    @pl.when(i < n_used)
    def _():
        expert = be_ref[i]
        wslot = run_ref[i] % 2

        @pl.when(i == 0)
        def _():
            for cp in weight_copies(expert, 0):
                cp.start()

        @pl.when((i == 0) | (expert != be_ref[jnp.maximum(i - 1, 0)]))
        def _():
            for cp in weight_copies(expert, wslot):
                cp.wait()

            @pl.when(nxt_ref[i] != expert)
            def _():
                for cp in weight_copies(nxt_ref[i], 1 - wslot):
                    cp.start()

            w1s[...] = _bf(w1f[wslot])
            w3s[...] = _bf(w3f[wslot])
            w2s[...] = _bf(w2f[wslot])

        wait_gathers(par)

        def move_rows(group, n_groups=6):
            lo, hi = group * MOE_BLOCK // n_groups, (group + 1) * MOE_BLOCK // n_groups
            for r in range(lo, hi):
                gather(i + 1, r, 1 - par).start(priority=r % 2)
                scatter(i - 1, r, 1 - par).start(priority=(r + 1) % 2)

        for s in range(8):
            words = xbuf[par, pl.ds(s, MOE_BLOCK, stride=8), :]
            xs_ref[:, s * LANES:(s + 1) * LANES] = _bf(_unpack_lo(words))
            xs_ref[:, half + s * LANES:half + (s + 1) * LANES] = _bf(_unpack_hi(words))
        move_rows(0)
        xb = xs_ref[...]
        h1a = _dot(xb, w1s[:, :half_e])
        move_rows(1)
        h3a = _dot(xb, w3s[:, :half_e])
        move_rows(2)
        act_a = _bf(_silu(h1a) * h3a)
        h1b = _dot(xb, w1s[:, half_e:])
        move_rows(3)
        h3b = _dot(xb, w3s[:, half_e:])
        move_rows(4)
        act_b = _bf(_silu(h1b) * h3b)
        y = _dot(act_a, w2s[:half_e, :])
        move_rows(5)
        y = y + _dot(act_b, w2s[half_e:, :])

        wait_scatters(par)
        words = _pack_bf16_pairs(y)
        for s in range(8):
            ybuf[par, pl.ds(s, MOE_BLOCK, stride=8), :] = words[:, s * LANES:(s + 1) * LANES]

        @pl.when(i == n_used - 1)
        def _():
            def last(r, carry):
                scatter(i, r, par).start()
                return carry

            lax.fori_loop(0, MOE_BLOCK, last, 0, unroll=8)
            wait_scatters(par)
            wait_scatters(1 - par)
            wait_gathers(1 - par)


def _moe(x2d, gain, w_group, b_group, w_expert, b_expert, w1, w3, w2, layer):
    n, d = x2d.shape
    gates_l, idx_l, xg, cnt = _router(x2d, gain, w_group, b_group, w_expert, b_expert)

    a = n * TOP_K
    n_blocks = -(-a // MOE_BLOCK) + N_EXPERTS
    p = n_blocks * MOE_BLOCK
    counts = cnt[0, :N_EXPERTS]
    padded = (counts + MOE_BLOCK - 1) // MOE_BLOCK * MOE_BLOCK
    pad_end = jnp.cumsum(padded)
    pad_start = pad_end - padded
    e_hot = idx_l[:, :TOP_K, None] == jnp.arange(N_EXPERTS, dtype=jnp.int32)
    dest = jnp.sum(jnp.where(e_hot, pad_start, 0), axis=-1) + idx_l[:, TOP_K:2 * TOP_K]
    dest = jnp.clip(dest.reshape(-1), 0, p - 1).astype(jnp.int32)
    blk0 = jnp.arange(n_blocks, dtype=jnp.int32) * MOE_BLOCK
    block_expert = jnp.minimum(jnp.searchsorted(pad_end, blk0, side="right"), N_EXPERTS - 1).astype(jnp.int32)
    n_used = (pad_end[-1] // MOE_BLOCK).astype(jnp.int32).reshape(1)
    experts = jnp.arange(N_EXPERTS, dtype=jnp.int32)
    later = jnp.where((experts[None, :] > experts[:, None]) & (counts[None, :] > 0), experts[None, :], N_EXPERTS)
    next_owner = jnp.min(later, axis=1)
    next_expert = jnp.where(next_owner < N_EXPERTS, next_owner, experts)[block_expert].astype(jnp.int32)
    run_index = (jnp.cumsum(jnp.concatenate([jnp.ones((1,), jnp.int32),
                                             (block_expert[1:] != block_expert[:-1]).astype(jnp.int32)])) - 1
                 ).astype(jnp.int32)
    codes = _invert(dest, p)
    slot = jnp.arange(p, dtype=jnp.int32)
    spare = TOP_K * n + (slot // MOE_BLOCK % 2) * MOE_BLOCK + slot % MOE_BLOCK
    src_tok = (jnp.maximum(codes, 0) >> 1) * 8
    dst_row = jnp.where(codes >= 0, (codes & 1) * n + (codes >> 1), spare) * 8
    lead = (TOP_K * n + MOE_BLOCK + jnp.arange(MOE_BLOCK, dtype=jnp.int32)) * 8
    dst_row = jnp.concatenate([lead, dst_row])

    hbm = pl.BlockSpec(memory_space=pl.ANY)
    out_rows = TOP_K * n + 2 * MOE_BLOCK
    out2 = pl.pallas_call(
        functools.partial(_expert_kernel, layer=layer),
        grid_spec=pltpu.PrefetchScalarGridSpec(
            num_scalar_prefetch=6,
            grid=(n_blocks,),
            in_specs=[hbm, hbm, hbm, hbm],
            out_specs=hbm,
            scratch_shapes=[
                pltpu.VMEM((2, MOE_BLOCK * 8, LANES), jnp.uint32), pltpu.VMEM((2, MOE_BLOCK * 8, LANES), jnp.uint32),
                pltpu.VMEM((MOE_BLOCK, d), BF16),
                pltpu.VMEM((2, d, D_EXPERT), F32), pltpu.VMEM((2, d, D_EXPERT), F32), pltpu.VMEM((2, D_EXPERT, d), F32),
                pltpu.VMEM((d, D_EXPERT), BF16), pltpu.VMEM((d, D_EXPERT), BF16), pltpu.VMEM((D_EXPERT, d), BF16),
                pltpu.SemaphoreType.DMA((2,)), pltpu.SemaphoreType.DMA((2,)), pltpu.SemaphoreType.DMA((2, 3)),
            ],
        ),
        out_shape=jax.ShapeDtypeStruct((out_rows * 8, LANES), jnp.uint32),
        compiler_params=_params("arbitrary"),
        name="moe_experts",
    )(block_expert, next_expert, run_index, n_used, src_tok, dst_row, xg, w1, w3, w2)
    return _combine(x2d, gates_l, out2)


def _combine_kernel(x_ref, gate_ref, a_ref, b_ref, o_ref):
    tm = x_ref.shape[0]
    half = x_ref.shape[1] // 2
    g0 = gate_ref[:, 0:1]
    g1 = gate_ref[:, 1:2]
    for s in range(8):
        wa = a_ref[pl.ds(s, tm, stride=8), :]
        wb = b_ref[pl.ds(s, tm, stride=8), :]
        lo = slice(s * LANES, (s + 1) * LANES)
        hi = slice(half + s * LANES, half + (s + 1) * LANES)
        o_ref[:, lo] = x_ref[:, lo] + (g0 * _unpack_lo(wa) + g1 * _unpack_lo(wb))
        o_ref[:, hi] = x_ref[:, hi] + (g0 * _unpack_hi(wa) + g1 * _unpack_hi(wb))


def _combine(x2d, gates, out2):
    n, d = x2d.shape
    tm = _row_tile(n, 688)
    return pl.pallas_call(
        _combine_kernel,
        grid=(n // tm,),
        in_specs=[
            pl.BlockSpec((tm, d), lambda i: (i, 0)),
            pl.BlockSpec((tm, LANES), lambda i: (i, 0)),
            pl.BlockSpec((tm * 8, LANES), lambda i: (i, 0)),
            pl.BlockSpec((tm * 8, LANES), lambda i: (n // tm + i, 0)),
        ],
        out_specs=pl.BlockSpec((tm, d), lambda i: (i, 0)),
        out_shape=jax.ShapeDtypeStruct((n, d), F32),
        compiler_params=_params("parallel"),
        name="moe_combine",
    )(x2d, gates, out2, out2)


def _even_layer(x, norm_g, w_in, conv_w, conv_b, b_i, b_f, a_norm, w_gate2, b_gate, b_norm, w_out):
    b, t, d = x.shape
    n = b * t
    a_w = 2 * A_HEADS * A_DK + 2 * A_HEADS * A_DV
    g_w = 2 * A_HEADS
    b_w = 2 * B_HEADS * B_DK + 2 * B_HEADS * B_DV
    main = a_w + b_w
    gate_cols = g_w + GATE_RANK
    w = _even_weight(w_in, a_w, g_w, b_w, GATE_RANK)
    z = _normproj(x.reshape(n, d), norm_g, w).reshape(b, t, main + MXU_DIM)

    ya = _mlstm(z, main, conv_w, conv_b, b_i, b_f, a_norm)

    wg = jnp.zeros((B_HEADS, MXU_DIM, B_DK), F32).at[:, g_w:g_w + GATE_RANK, :].set(
        w_gate2.reshape(GATE_RANK, B_HEADS, B_DK).transpose(1, 0, 2))
    dummy = jnp.zeros((1, B_HEADS * B_DK), F32)
    hp = 2
    hspec = pl.BlockSpec((1, hp * B_DK), lambda bi, g: (0, g))
    yb = _gla_call(
        z, B_HEADS, hp, B_DK, B_DV,
        (a_w, a_w + B_HEADS * B_DK, a_w + 2 * B_HEADS * B_DK, a_w + 2 * B_HEADS * B_DK + B_HEADS * B_DV, main),
        MXU_DIM,
        ((wg, pl.BlockSpec((hp, MXU_DIM, B_DK), lambda bi, g: (g, 0, 0))),
         (b_gate.reshape(1, -1), hspec), (dummy, hspec)),
        b_norm, "gla")
    return _outproj(ya.reshape(n, -1), yb.reshape(n, -1), w_out.astype(BF16), x.reshape(n, d)).reshape(b, t, d)


def _odd_layer(x, lb, norm_g, w_in, c_norm, q_a_norm, w_q_up, kv_a_norm, w_kv_up, q_norm, k_norm, w_out):
    b, t, d = x.shape
    n = b * t
    c_w = 2 * C_HEADS * C_DK + 2 * C_HEADS * C_DV
    swap = (jnp.arange(D_ROPE) + D_ROPE // 2) % D_ROPE
    kr0 = c_w + Q_LORA + KV_LORA
    used = kr0 + 2 * D_ROPE
    total = -(-used // MXU_DIM) * MXU_DIM
    z2 = _normproj(x.reshape(n, d), norm_g, _odd_weight(w_in, kr0))
    z = z2.reshape(b, t, total)

    hp = 4
    hspec = pl.BlockSpec((1, hp * C_DK), lambda bi, g: (0, g))
    yc = _gla_call(
        z, C_HEADS, hp, C_DK, C_DV,
        (0, C_HEADS * C_DK, 2 * C_HEADS * C_DK, 2 * C_HEADS * C_DK + C_HEADS * C_DV, C_HEADS * C_DK),
        C_DK,
        ((jnp.log(lb).reshape(1, -1), hspec), (jnp.log1p(-lb).reshape(1, -1), hspec), ((1.0 - lb).reshape(1, -1), hspec)),
        c_norm, "hgrn")

    dq = D_NOPE + D_ROPE
    wq = w_q_up.reshape(Q_LORA, D_HEADS, dq)
    wq_rope = wq[:, :, D_NOPE:]
    wq_p = jnp.concatenate([wq[:, :, :D_NOPE].reshape(Q_LORA, -1),
                            jnp.concatenate([wq_rope, wq_rope[:, :, swap]], axis=-1).reshape(Q_LORA, -1)],
                           axis=1).astype(BF16)
    wkv = w_kv_up.reshape(KV_LORA, D_HEADS, D_NOPE + D_V)
    wkv_p = jnp.concatenate([wkv[:, :, :D_NOPE].reshape(KV_LORA, -1), wkv[:, :, D_NOPE:].reshape(KV_LORA, -1)],
                            axis=1).astype(BF16)
    qn = _normproj(z2, q_a_norm, wq_p, x_col_block=c_w // Q_LORA).reshape(b, t, -1)
    kvn = _normproj(z2, kv_a_norm, wkv_p, x_col_block=(c_w + Q_LORA) // KV_LORA).reshape(b, t, -1)

    pos = jnp.arange(t, dtype=F32)
    half = D_ROPE // 2
    inv = ROPE_THETA ** (-jnp.arange(half, dtype=F32) / half)
    ang = pos[:, None] * inv[None, :]
    cos, sin = jnp.cos(ang), jnp.sin(ang)
    tab = jnp.concatenate([cos, cos, -sin, sin], axis=1)
    pair = lambda g: jnp.concatenate([g[D_NOPE:], g[D_NOPE:][swap]]).reshape(1, LANES)
    yd = _mla(qn, kvn, z, kr0, tab, q_norm[:D_NOPE].reshape(1, LANES), pair(q_norm),
              k_norm[:D_NOPE].reshape(1, LANES), pair(k_norm))
    return _outproj(yc.reshape(n, -1), yd.reshape(n, -1), w_out.astype(BF16), x.reshape(n, d)).reshape(b, t, d)


def kernel(x, meta_tokens, ab_norm, ab_w_in, a_conv_w, a_conv_b, a_b_i, a_b_f, a_head_norm, b_w_gate2, b_b_gate, b_head_norm, ab_w_out, cd_norm, cd_w_in, c_lower_bound, c_head_norm, d_q_a_norm, d_w_q_up, d_kv_a_norm, d_w_kv_up, d_q_norm, d_k_norm, cd_w_out, moe_norm, moe_w_group, moe_b_group, moe_w_expert, moe_b_expert, moe_w1, moe_w3, moe_w2):
    b = x.shape[0]
    depth = moe_norm.shape[0]
    h = jnp.concatenate([jnp.broadcast_to(meta_tokens.astype(x.dtype)[None], (b, N_META, D_MODEL)), x], axis=1)
    t = h.shape[1]
    lb_cum = jnp.cumsum(jax.nn.softmax(c_lower_bound.astype(F32), axis=0), axis=0)
    lower_bounds = lb_cum - lb_cum[0]
    for layer in range(depth):
        j = layer // 2
        if layer % 2 == 0:
            h = _even_layer(h, ab_norm[j], ab_w_in[j], a_conv_w[j], a_conv_b[j], a_b_i[j], a_b_f[j], a_head_norm[j],
                            b_w_gate2[j], b_b_gate[j], b_head_norm[j], ab_w_out[j])
        else:
            h = _odd_layer(h, lower_bounds[layer], cd_norm[j], cd_w_in[j], c_head_norm[j], d_q_a_norm[j],
                           d_w_q_up[j], d_kv_a_norm[j], d_w_kv_up[j], d_q_norm[j], d_k_norm[j], cd_w_out[j])
        h = _moe(h.reshape(b * t, D_MODEL), moe_norm[layer], moe_w_group[layer], moe_b_group[layer],
                 moe_w_expert[layer], moe_b_expert[layer], moe_w1, moe_w3, moe_w2, layer).reshape(b, t, D_MODEL)
    return h[:, N_META:]
```

```python
import functools
import math

import jax
import jax.numpy as jnp
from jax import lax
from jax.experimental import pallas as pl
from jax.experimental.pallas import tpu as pltpu

F32 = jnp.float32
BF16 = jnp.bfloat16
HIGHEST = lax.Precision.HIGHEST

D_MODEL = 2048
N_META = 16
CHUNK = 64
CONV_K = 4
EPS = 1e-6
A_HEADS, A_DK, A_DV = 4, 128, 256
B_HEADS, B_DK, B_DV = 4, 128, 256
GATE_RANK = 16
GATE_TAU = 16.0
C_HEADS, C_DK, C_DV = 8, 128, 128
D_HEADS, D_NOPE, D_ROPE, D_V = 8, 128, 64, 128
Q_LORA, KV_LORA = 512, 256
ROPE_THETA = 10000.0
N_GROUPS, EXPERTS_PER_GROUP = 4, 8
N_EXPERTS = N_GROUPS * EXPERTS_PER_GROUP
TOP_K = 2
D_EXPERT = 512

LANES = 128
MXU_DIM = 256
BF16_ROWS = 16
VMEM_LIMIT = 56 * 1024 * 1024
MOE_BLOCK = MXU_DIM
ATT_BLOCK = 256
ATT_PAD = ATT_BLOCK - N_META

_NT = (((1,), (1,)), ((), ()))
_TN = (((0,), (0,)), ((), ()))


def _dot(a, b, precision=None):
    return jnp.dot(a, b, preferred_element_type=F32, precision=precision)


def _dot_nt(a, b):
    return lax.dot_general(a, b, _NT, preferred_element_type=F32)


def _dot_tn(a, b):
    return lax.dot_general(a, b, _TN, preferred_element_type=F32)


def _bf(x):
    return x.astype(BF16)


def _split3(x):
    hi = _bf(x)
    rest = x - hi.astype(F32)
    mid = _bf(rest)
    return hi, mid, _bf(rest - mid.astype(F32))


def _log_sigmoid(x):
    return jnp.minimum(x, 0.0) - jnp.log1p(jnp.exp(-jnp.abs(x)))


def _sigmoid(x):
    return 1.0 / (1.0 + jnp.exp(-x))


def _silu(x):
    return x * _sigmoid(x)


def _row_tile(m, cap):
    best = None
    for t in range(BF16_ROWS, min(m, cap) + 1, BF16_ROWS):
        if m % t == 0:
            best = t
    assert best is not None, m
    return best


def _col_tile(n, cap):
    best = None
    for t in range(MXU_DIM, min(n, cap) + 1, MXU_DIM):
        if n % t == 0:
            best = t
    assert best is not None, n
    return best


def _params(*sem):
    return pltpu.CompilerParams(dimension_semantics=sem, vmem_limit_bytes=VMEM_LIMIT)


_RELAYOUT_ROWS = 256


def _even_weight_kernel(wa_ref, wb_ref, wc_ref, o_ref, *, n_plain, n_shift, shift, gate_cols):
    ob = pl.program_id(0)
    rows = o_ref.shape[0]
    chunks = rows // _RELAYOUT_ROWS

    @pl.when(ob < n_plain)
    def _():
        def body(c, carry):
            r = pl.ds(pl.multiple_of(c * _RELAYOUT_ROWS, _RELAYOUT_ROWS), _RELAYOUT_ROWS)
            o_ref[r, :] = _bf(wa_ref[r, :])
            return carry

        lax.fori_loop(0, chunks, body, 0)

    @pl.when((ob >= n_plain) & (ob < n_plain + n_shift))
    def _():
        def body(c, carry):
            r = pl.ds(pl.multiple_of(c * _RELAYOUT_ROWS, _RELAYOUT_ROWS), _RELAYOUT_ROWS)
            wide = jnp.concatenate([wa_ref[r, :], wb_ref[r, :]], axis=1)
            o_ref[r, :] = _bf(wide[:, shift:shift + MXU_DIM])
            return carry

        lax.fori_loop(0, chunks, body, 0)

    @pl.when(ob == n_plain + n_shift)
    def _():
        lane = lax.broadcasted_iota(jnp.int32, (_RELAYOUT_ROWS, LANES), 1)

        def body(c, carry):
            r = pl.ds(pl.multiple_of(c * _RELAYOUT_ROWS, _RELAYOUT_ROWS), _RELAYOUT_ROWS)
            first = jnp.where(lane < shift, wc_ref[r, :], jnp.where(lane < gate_cols, wb_ref[r, :], 0.0))
            o_ref[r, :] = _bf(jnp.concatenate([first, jnp.zeros_like(first)], axis=1))
            return carry

        lax.fori_loop(0, chunks, body, 0)


def _even_weight(w_in, a_w, g_w, b_w, rank):
    d = w_in.shape[0]
    assert a_w % MXU_DIM == 0 and b_w % MXU_DIM == 0 and g_w + rank <= LANES and d % _RELAYOUT_ROWS == 0
    n_plain, n_shift = a_w // MXU_DIM, b_w // MXU_DIM
    n_out = n_plain + n_shift + 1
    last = n_out - 1

    def b_index(ob):
        return (0, jnp.where(ob < n_plain, 0, jnp.where(ob < last, 2 * (ob + 1), (a_w + g_w + b_w) // LANES)))

    return pl.pallas_call(
        functools.partial(_even_weight_kernel, n_plain=n_plain, n_shift=n_shift, shift=g_w, gate_cols=g_w + rank),
        grid=(n_out,),
        in_specs=[
            pl.BlockSpec((d, MXU_DIM), lambda ob: (0, jnp.minimum(ob, last - 1))),
            pl.BlockSpec((d, LANES), b_index),
            pl.BlockSpec((d, LANES), lambda ob: (0, a_w // LANES)),
        ],
        out_specs=pl.BlockSpec((d, MXU_DIM), lambda ob: (0, ob)),
        out_shape=jax.ShapeDtypeStruct((d, n_out * MXU_DIM), BF16),
        compiler_params=_params("parallel"),
        name="even_weight",
    )(w_in, w_in, w_in)


def _odd_weight_kernel(w_ref, o_ref, *, n_plain):
    ob = pl.program_id(0)
    chunks = o_ref.shape[0] // _RELAYOUT_ROWS
    half = D_ROPE // 2

    def body(c, carry):
        r = pl.ds(pl.multiple_of(c * _RELAYOUT_ROWS, _RELAYOUT_ROWS), _RELAYOUT_ROWS)
        w = w_ref[r, :]

        @pl.when(ob < n_plain)
        def _():
            o_ref[r, :] = _bf(w)

        @pl.when(ob == n_plain)
        def _():
            pair = jnp.concatenate([w[:, :D_ROPE], w[:, half:D_ROPE], w[:, :half]], axis=1)
            o_ref[r, :] = _bf(jnp.concatenate([pair, jnp.zeros_like(pair)], axis=1))

        return carry

    lax.fori_loop(0, chunks, body, 0)


def _odd_weight(w_in, kr0):
    d = w_in.shape[0]
    assert kr0 % MXU_DIM == 0 and w_in.shape[1] == kr0 + D_ROPE and d % _RELAYOUT_ROWS == 0
    n_plain = kr0 // MXU_DIM
    return pl.pallas_call(
        functools.partial(_odd_weight_kernel, n_plain=n_plain),
        grid=(n_plain + 1,),
        in_specs=[pl.BlockSpec((d, MXU_DIM), lambda ob: (0, ob))],
        out_specs=pl.BlockSpec((d, MXU_DIM), lambda ob: (0, ob)),
        out_shape=jax.ShapeDtypeStruct((d, kr0 + MXU_DIM), BF16),
        compiler_params=_params("parallel"),
        name="odd_weight",
    )(w_in)


def _normproj_kernel(x_ref, g_ref, w_ref, o_ref, xs_ref):
    tm = xs_ref.shape[0]

    @pl.when(pl.program_id(1) == 0)
    def _():
        def body(c, carry):
            r0 = pl.multiple_of(c * BF16_ROWS, BF16_ROWS)
            x = x_ref[pl.ds(r0, BF16_ROWS), :]
            ms = jnp.mean(x * x, axis=-1, keepdims=True)
            xs_ref[pl.ds(r0, BF16_ROWS), :] = _bf(x * lax.rsqrt(ms + EPS) * g_ref[...])
            return carry

        lax.fori_loop(0, tm // BF16_ROWS, body, 0, unroll=8)

    o_ref[...] = _dot(xs_ref[...], w_ref[...]).astype(o_ref.dtype)


def _normproj(x2d, gain, w, *, x_col_block=0, out_dtype=F32):
    m = x2d.shape[0]
    k, n = w.shape
    tm = _row_tile(m, 688)
    tn = _col_tile(n, 1280)
    return pl.pallas_call(
        _normproj_kernel,
        grid=(m // tm, n // tn),
        in_specs=[
            pl.BlockSpec((tm, k), lambda i, j: (i, x_col_block)),
            pl.BlockSpec((1, k), lambda i, j: (0, 0)),
            pl.BlockSpec((k, tn), lambda i, j: (0, j)),
        ],
        out_specs=pl.BlockSpec((tm, tn), lambda i, j: (i, j)),
        out_shape=jax.ShapeDtypeStruct((m, n), out_dtype),
        scratch_shapes=[pltpu.VMEM((tm, k), BF16)],
        compiler_params=_params("parallel", "arbitrary"),
        name="normproj",
    )(x2d, gain.reshape(1, k).astype(F32), w)


def _outproj_kernel(ya_ref, yb_ref, w_ref, r_ref, o_ref):
    ka = ya_ref.shape[1]
    acc = _dot(ya_ref[...], w_ref[:ka, :]) + _dot(yb_ref[...], w_ref[ka:, :])
    o_ref[...] = r_ref[...] + acc


def _outproj(ya, yb, w, res):
    m, ka = ya.shape
    kb = yb.shape[1]
    n = w.shape[1]
    tm = _row_tile(m, 1376)
    tn = _col_tile(n, 1024)
    return pl.pallas_call(
        _outproj_kernel,
        grid=(m // tm, n // tn),
        in_specs=[
            pl.BlockSpec((tm, ka), lambda i, j: (i, 0)),
            pl.BlockSpec((tm, kb), lambda i, j: (i, 0)),
            pl.BlockSpec((ka + kb, tn), lambda i, j: (0, j)),
            pl.BlockSpec((tm, tn), lambda i, j: (i, j)),
        ],
        out_specs=pl.BlockSpec((tm, tn), lambda i, j: (i, j)),
        out_shape=jax.ShapeDtypeStruct((m, n), F32),
        compiler_params=_params("parallel", "arbitrary"),
        name="outproj",
    )(ya, yb, w, res)


def _mlstm_kernel(bi_ref, bf_ref, q_ref, k_ref, v_ref, og_ref, gt_ref,
                  cwq_ref, cwk_ref, cbq_ref, cbk_ref, hn_ref, o_ref, c_ref, n_ref, m_ref, *, heads):
    t_total = q_ref.shape[1]
    n_chunks = (t_total - N_META) // CHUNK
    head0 = pl.program_id(1) * heads

    c_ref[...] = jnp.zeros_like(c_ref)
    n_ref[...] = jnp.zeros_like(n_ref)
    m_ref[...] = jnp.zeros_like(m_ref)

    def conv(win, cw, cb, length):
        y = cb
        for j in range(CONV_K):
            y = y + win[8 - (CONV_K - 1) + j:8 - (CONV_K - 1) + j + length, :] * cw[j:j + 1, :]
        return _silu(y)

    hs = range(heads)
    kcol = [slice(hh * A_DK, (hh + 1) * A_DK) for hh in hs]
    vcol = [slice(hh * A_DV, (hh + 1) * A_DV) for hh in hs]

    def gates(hh, blk, blk_parts, length, causal, upper):
        head = head0 + hh
        b_i = bi_ref[head]
        b_f = bf_ref[head]
        lane = lax.broadcasted_iota(jnp.int32, (length, LANES), 1)
        ig_c = jnp.sum(jnp.where(lane == head, blk, 0.0), axis=1, keepdims=True) + b_i
        lf_c = _log_sigmoid(jnp.sum(jnp.where(lane == A_HEADS + head, blk, 0.0), axis=1, keepdims=True) + b_f)
        sel_r = lax.broadcasted_iota(jnp.int32, (8, LANES), 0)
        sel_l = lax.broadcasted_iota(jnp.int32, (8, LANES), 1)
        sel = jnp.where(sel_l == head + A_HEADS * sel_r, 1.0, 0.0).astype(BF16)
        rows = _dot_nt(sel, blk_parts[0]) + (_dot_nt(sel, blk_parts[1]) + _dot_nt(sel, blk_parts[2]))
        ig_r = rows[0:1, :] + b_i
        lf_r = _log_sigmoid(rows[1:2, :] + b_f)
        b_c = jnp.sum(jnp.where(causal, lf_r, 0.0), axis=1, keepdims=True)
        b_r = jnp.sum(jnp.where(upper, lf_c, 0.0), axis=0, keepdims=True)
        b_end = b_c[length - 1:length, :]
        w_end = b_end - b_c + ig_c
        m_loc = jnp.max(w_end, axis=0, keepdims=True)
        d = jnp.where(causal, b_c - b_r + ig_r, -jnp.inf)
        return b_c, b_end, jnp.exp(w_end - m_loc), m_loc, d, jnp.max(d, axis=1, keepdims=True)

    def chunk(o, length, qwin, kwin):
        row = lax.broadcasted_iota(jnp.int32, (length, length), 0)
        col = lax.broadcasted_iota(jnp.int32, (length, length), 1)
        causal = col <= row
        blk = gt_ref[0, pl.ds(o, length), 0:LANES]
        blk_parts = _split3(blk)
        gt = [gates(hh, blk, blk_parts, length, causal, row <= col) for hh in hs]
        q = [conv(qwin[hh], cwq_ref[:, kcol[hh]], cbq_ref[:, kcol[hh]], length) for hh in hs]
        k = [conv(kwin[hh], cwk_ref[:, kcol[hh]], cbk_ref[:, kcol[hh]], length) * (A_DK ** -0.5) for hh in hs]
        vb = [_bf(v_ref[0, pl.ds(o, length), vcol[hh]]) for hh in hs]
        qb = [_bf(q[hh]) for hh in hs]
        k_w = [k[hh] * gt[hh][2] for hh in hs]
        qk = [_dot_nt(qb[hh], _bf(k[hh])) for hh in hs]
        c_in = [c_ref[hh] for hh in hs]
        q_c = [_dot(qb[hh], _bf(c_in[hh])) for hh in hs]
        c_loc = [_dot_tn(_bf(k_w[hh]), vb[hh]) for hh in hs]
        s, a_t, m_t, q_n = [], [], [], []
        for hh in hs:
            b_c, b_end, _, m_loc, d, d_max = gt[hh]
            m_in = m_ref[hh]
            inter = b_c + m_in
            m_t.append(jnp.maximum(inter, d_max))
            s.append(qk[hh] * jnp.exp(d - m_t[hh]))
            a_t.append(jnp.exp(inter - m_t[hh]))
            m_new = jnp.maximum(b_end + m_in, m_loc)
            a = jnp.exp(b_end + m_in - m_new)
            c = jnp.exp(m_loc - m_new)
            n_in = n_ref[hh]
            c_ref[hh] = a * c_in[hh] + c * c_loc[hh]
            n_ref[hh] = a * n_in + c * jnp.sum(k_w[hh], axis=0, keepdims=True)
            m_ref[hh] = m_new
            q_n.append(jnp.sum(q[hh] * n_in, axis=1, keepdims=True))
        num = [_dot(_bf(s[hh]), vb[hh]) + a_t[hh] * q_c[hh] for hh in hs]
        for hh in hs:
            den = jnp.sum(s[hh], axis=1, keepdims=True) + a_t[hh] * q_n[hh]
            h = num[hh] / jnp.maximum(jnp.abs(den), jnp.exp(-m_t[hh]))
            hn = h * lax.rsqrt(jnp.mean(h * h, axis=-1, keepdims=True) + EPS) * hn_ref[:, vcol[hh]]
            y = _sigmoid(og_ref[0, pl.ds(o, length), vcol[hh]]) * hn
            o_ref[0, pl.ds(o, length), vcol[hh]] = y.astype(o_ref.dtype)

    zeros8 = jnp.zeros((8, A_DK), F32)
    chunk(0, N_META,
          [jnp.concatenate([zeros8, q_ref[0, 0:N_META, kcol[hh]]], axis=0) for hh in hs],
          [jnp.concatenate([zeros8, k_ref[0, 0:N_META, kcol[hh]]], axis=0) for hh in hs])

    def body(c, carry):
        o = pl.multiple_of(N_META + c * CHUNK, BF16_ROWS)
        w0 = pl.multiple_of(N_META - 8 + c * CHUNK, 8)
        chunk(o, CHUNK, [q_ref[0, pl.ds(w0, CHUNK + 8), kcol[hh]] for hh in hs],
              [k_ref[0, pl.ds(w0, CHUNK + 8), kcol[hh]] for hh in hs])
        return carry

    lax.fori_loop(0, n_chunks, body, 0)


def _mlstm(z, gate_col, conv_w, conv_b, b_i, b_f, head_norm, heads=2):
    b, t, _ = z.shape
    hk = A_HEADS * A_DK
    wk, wv = heads * A_DK, heads * A_DV
    smem = pl.BlockSpec(memory_space=pltpu.SMEM)
    col = lambda width, off: (lambda bi, g: (bi, 0, off // width + g))
    return pl.pallas_call(
        functools.partial(_mlstm_kernel, heads=heads),
        grid=(b, A_HEADS // heads),
        in_specs=[
            smem, smem,
            pl.BlockSpec((1, t, wk), col(wk, 0)),
            pl.BlockSpec((1, t, wk), col(wk, hk)),
            pl.BlockSpec((1, t, wv), col(wv, 2 * hk)),
            pl.BlockSpec((1, t, wv), col(wv, 2 * hk + A_HEADS * A_DV)),
            pl.BlockSpec((1, t, MXU_DIM), lambda bi, g: (bi, 0, gate_col // MXU_DIM)),
            pl.BlockSpec((CONV_K, wk), lambda bi, g: (0, g)),
            pl.BlockSpec((CONV_K, wk), lambda bi, g: (0, A_HEADS // heads + g)),
            pl.BlockSpec((1, wk), lambda bi, g: (0, g)),
            pl.BlockSpec((1, wk), lambda bi, g: (0, A_HEADS // heads + g)),
            pl.BlockSpec((1, wv), lambda bi, g: (0, g)),
        ],
        out_specs=pl.BlockSpec((1, t, wv), lambda bi, g: (bi, 0, g)),
        out_shape=jax.ShapeDtypeStruct((b, t, A_HEADS * A_DV), BF16),
        scratch_shapes=[pltpu.VMEM((heads, A_DK, A_DV), F32), pltpu.VMEM((heads, 1, A_DK), F32),
                        pltpu.VMEM((heads, 1, 1), F32)],
        compiler_params=_params("parallel", "parallel"),
        name="mlstm",
    )(b_i, b_f, z, z, z, z, z, conv_w, conv_w, conv_b.reshape(1, -1), conv_b.reshape(1, -1),
      head_norm.reshape(1, -1))


def _gla_kernel(q_ref, k_ref, v_ref, og_ref, g_ref, p0_ref, p1_ref, p2_ref, hn_ref, o_ref, st_ref, *, mode, heads):
    t_total = q_ref.shape[1]
    n_chunks = (t_total - N_META) // CHUNK
    dv, dk = st_ref.shape[1:]
    st_ref[...] = jnp.zeros_like(st_ref)

    hs = range(heads)
    kcol = [slice(hh * dk, (hh + 1) * dk) for hh in hs]
    vcol = [slice(hh * dv, (hh + 1) * dv) for hh in hs]

    def gate_inputs(hh, o, length):
        q = q_ref[0, pl.ds(o, length), kcol[hh]]
        if mode == "gla":
            pre = _dot(g_ref[0, pl.ds(o, length), :], p0_ref[hh], precision=HIGHEST) + p1_ref[:, kcol[hh]]
            return q * (dk ** -0.5), k_ref[0, pl.ds(o, length), kcol[hh]], _log_sigmoid(pre) / GATE_TAU
        fpre = g_ref[0, pl.ds(o, length), kcol[hh]]
        a = p0_ref[:, kcol[hh]]
        bb = p1_ref[:, kcol[hh]] + _log_sigmoid(fpre)
        lg = jnp.maximum(a, bb) + jnp.log1p(jnp.exp(-jnp.abs(a - bb)))
        return q, p2_ref[:, kcol[hh]] * _sigmoid(-fpre), lg

    def cumsum_time(tri, lg):
        parts = _dot(tri, jnp.concatenate(_split3(lg), axis=1))
        return parts[:, :dk] + (parts[:, dk:2 * dk] + parts[:, 2 * dk:])

    def chunk(o, length):
        row = lax.broadcasted_iota(jnp.int32, (length, length), 0)
        col = lax.broadcasted_iota(jnp.int32, (length, length), 1)
        causal = col <= row
        tri = jnp.where(causal, 1.0, 0.0).astype(BF16)
        qkl = [gate_inputs(hh, o, length) for hh in hs]
        vb = [_bf(v_ref[0, pl.ds(o, length), vcol[hh]]) for hh in hs]
        g = [cumsum_time(tri, qkl[hh][2]) for hh in hs]
        g_end = [g[hh][length - 1:length, :] for hh in hs]
        g_mid = [g[hh][length // 2:length // 2 + 1, :] for hh in hs]
        s = [_dot_nt(_bf(qkl[hh][0] * jnp.exp(g[hh] - g_mid[hh])), _bf(qkl[hh][1] * jnp.exp(g_mid[hh] - g[hh])))
             for hh in hs]
        st_in = [st_ref[hh] for hh in hs]
        inter = [_dot_nt(_bf(qkl[hh][0] * jnp.exp(g[hh])), _bf(st_in[hh])) for hh in hs]
        local = [_dot_tn(vb[hh], _bf(qkl[hh][1] * jnp.exp(g_end[hh] - g[hh]))) for hh in hs]
        for hh in hs:
            st_ref[hh] = st_in[hh] * jnp.exp(g_end[hh]) + local[hh]
        out = [_dot(_bf(jnp.where(causal, s[hh], 0.0)), vb[hh]) + inter[hh] for hh in hs]
        for hh in hs:
            hn = out[hh] * lax.rsqrt(jnp.mean(out[hh] * out[hh], axis=-1, keepdims=True) + EPS) * hn_ref[:, vcol[hh]]
            og = og_ref[0, pl.ds(o, length), vcol[hh]]
            gate = _silu(og) if mode == "gla" else _sigmoid(og)
            o_ref[0, pl.ds(o, length), vcol[hh]] = (gate * hn).astype(o_ref.dtype)

    chunk(0, N_META)

    def body(c, carry):
        chunk(pl.multiple_of(N_META + c * CHUNK, BF16_ROWS), CHUNK)
        return carry

    lax.fori_loop(0, n_chunks, body, 0)


def _gla_call(z, n_heads, heads, dk, dv, blocks, gate_width, params, head_norm, mode):
    b, t, _ = z.shape
    q0, k0, v0, og0, g0 = blocks
    zspec = lambda width, off, grouped=True: pl.BlockSpec(
        (1, t, width), (lambda bi, g: (bi, 0, off // width + (g if grouped else 0))))
    (p0, s0), (p1, s1), (p2, s2) = params
    gate_spec = zspec(gate_width, g0, grouped=False) if mode == "gla" else zspec(heads * dk, g0)
    return pl.pallas_call(
        functools.partial(_gla_kernel, mode=mode, heads=heads),
        grid=(b, n_heads // heads),
        in_specs=[
            zspec(heads * dk, q0), zspec(heads * dk, k0), zspec(heads * dv, v0), zspec(heads * dv, og0),
            gate_spec, s0, s1, s2,
            pl.BlockSpec((1, heads * dv), lambda bi, g: (0, g)),
        ],
        out_specs=pl.BlockSpec((1, t, heads * dv), lambda bi, g: (bi, 0, g)),
        out_shape=jax.ShapeDtypeStruct((b, t, n_heads * dv), BF16),
        scratch_shapes=[pltpu.VMEM((heads, dv, dk), F32)],
        compiler_params=_params("parallel", "parallel"),
        name="gla_" + mode,
    )(z, z, z, z, z, p0, p1, p2, head_norm.reshape(1, -1))


def _mla_kernel(qn_ref, qr_ref, kn_ref, v_ref, kr_ref, tab_ref, gqn_ref, gqr_ref, gkn_ref, gkr_ref,
                o_ref, qf_ref, kf_ref, vf_ref):
    t_total = qn_ref.shape[1]
    n_blocks = (ATT_PAD + t_total) // ATT_BLOCK
    dqk = D_NOPE + D_ROPE
    scale = dqk ** -0.5
    rows = _row_tile(t_total, 768)

    qf_ref[0:ATT_PAD, :] = jnp.zeros((ATT_PAD, 2 * LANES), BF16)
    kf_ref[0:ATT_PAD, :] = jnp.zeros((ATT_PAD, 2 * LANES), BF16)
    vf_ref[0:ATT_PAD, :] = jnp.zeros((ATT_PAD, D_V), BF16)

    def rope_pair(x, gains, tab):
        p = x * gains * tab
        return p + pltpu.roll(p, D_ROPE, 1)

    def prep(c):
        r0 = c * rows
        dst = ATT_PAD + c * rows
        tab = tab_ref[pl.ds(r0, rows), :]
        qn = qn_ref[0, pl.ds(r0, rows), :]
        qr = qr_ref[0, pl.ds(r0, rows), :]
        ssq = jnp.sum(qn * qn + 0.5 * (qr * qr), axis=-1, keepdims=True)
        rq = lax.rsqrt(ssq / dqk + EPS) * scale
        qf_ref[pl.ds(dst, rows), 0:LANES] = _bf(qn * gqn_ref[...] * rq)
        qf_ref[pl.ds(dst, rows), LANES:2 * LANES] = _bf(qr * gqr_ref[...] * tab * rq)
        kn = kn_ref[0, pl.ds(r0, rows), :]
        kr = kr_ref[0, pl.ds(r0, rows), :]
        ssk = jnp.sum(kn * kn + 0.5 * (kr * kr), axis=-1, keepdims=True)
        rk = lax.rsqrt(ssk / dqk + EPS)
        kf_ref[pl.ds(dst, rows), 0:LANES] = _bf(kn * gkn_ref[...] * rk)
        kf_ref[pl.ds(dst, rows), LANES:2 * LANES] = _bf(rope_pair(kr, gkr_ref[...], tab) * rk)
        vf_ref[pl.ds(dst, rows), :] = _bf(v_ref[0, pl.ds(r0, rows), :])

    for c in range(t_total // rows):
        prep(c)

    qpos = lax.broadcasted_iota(jnp.int32, (ATT_BLOCK, ATT_BLOCK), 0)
    kpos = lax.broadcasted_iota(jnp.int32, (ATT_BLOCK, ATT_BLOCK), 1)
    neg = -jnp.inf

    def scores(qi):
        q = qf_ref[qi * ATT_BLOCK:(qi + 1) * ATT_BLOCK, :]
        return _dot_nt(q, kf_ref[0:(qi + 1) * ATT_BLOCK, :])

    s_next = scores(0)
    for qi in range(n_blocks):
        s = s_next
        if qi + 1 < n_blocks:
            s_next = scores(qi + 1)
        parts = [s[:, j * ATT_BLOCK:(j + 1) * ATT_BLOCK] for j in range(qi + 1)]
        parts[0] = jnp.where(kpos >= ATT_PAD, parts[0], neg)
        parts[qi] = jnp.where(kpos <= qpos, parts[qi], neg)
        top = functools.reduce(jnp.maximum, parts)
        m = jnp.max(top, axis=-1, keepdims=True)
        if qi == 0:
            m = jnp.where(m == neg, 0.0, m)
        probs = [jnp.exp(part - m) for part in parts]
        l = jnp.sum(functools.reduce(jnp.add, probs), axis=-1, keepdims=True)
        pv = _dot(jnp.concatenate([_bf(pr) for pr in probs], axis=1), vf_ref[0:(qi + 1) * ATT_BLOCK, :])
        if qi == 0:
            out = pv / jnp.where(l == 0.0, 1.0, l)
            o_ref[0, 0:N_META, :] = out[ATT_PAD:, :].astype(o_ref.dtype)
        else:
            dst = qi * ATT_BLOCK - ATT_PAD
            o_ref[0, dst:dst + ATT_BLOCK, :] = (pv / l).astype(o_ref.dtype)


def _mla(qn, kvn, z, kr_col, tab, gqn, gqr, gkn, gkr):
    b, t, _ = qn.shape
    assert (ATT_PAD + t) % ATT_BLOCK == 0 and t % (3 * BF16_ROWS) == 0
    tp = ATT_PAD + t
    hspec = lambda off: pl.BlockSpec((1, t, LANES), lambda bi, h: (bi, 0, off + h))
    gspec = pl.BlockSpec((1, LANES), lambda bi, h: (0, 0))
    return pl.pallas_call(
        _mla_kernel,
        grid=(b, D_HEADS),
        in_specs=[
            hspec(0), hspec(D_HEADS), hspec(0), hspec(D_HEADS),
            pl.BlockSpec((1, t, LANES), lambda bi, h: (bi, 0, kr_col // LANES)),
            pl.BlockSpec((t, LANES), lambda bi, h: (0, 0)),
            gspec, gspec, gspec, gspec,
        ],
        out_specs=pl.BlockSpec((1, t, D_V), lambda bi, h: (bi, 0, h)),
        out_shape=jax.ShapeDtypeStruct((b, t, D_HEADS * D_V), BF16),
        scratch_shapes=[pltpu.VMEM((tp, 2 * LANES), BF16), pltpu.VMEM((tp, 2 * LANES), BF16),
                        pltpu.VMEM((tp, D_V), BF16)],
        compiler_params=_params("parallel", "parallel"),
        name="mla",
    )(qn, qn, kvn, kvn, z, tab, gqn, gqr, gkn, gkr)


def _pack_bf16_pairs(v):
    w = v.shape[1] // 2
    bits = pltpu.bitcast(_bf(v).astype(F32), jnp.uint32)
    return (bits[:, :w] >> 16) | (bits[:, w:] & jnp.uint32(0xFFFF0000))


def _unpack_lo(words):
    return pltpu.bitcast(words << 16, F32)


def _unpack_hi(words):
    return pltpu.bitcast(words & jnp.uint32(0xFFFF0000), F32)


def _router_kernel(x_ref, g_ref, w_ref, b_ref, gate_ref, idx_ref, xg_ref, cnt_ref, carry_ref):
    tm = x_ref.shape[0]

    @pl.when(pl.program_id(0) == 0)
    def _():
        carry_ref[...] = jnp.zeros_like(carry_ref)

    x = x_ref[...]
    ms = jnp.mean(x * x, axis=-1, keepdims=True)
    xn = x * lax.rsqrt(ms + EPS) * g_ref[...]
    xh = _bf(xn)
    xl = _bf(xn - xh.astype(F32))
    w = w_ref[...]
    wh = _bf(w)
    wl = _bf(w - wh.astype(F32))
    both = _dot(xh, jnp.concatenate([wh, wl], axis=1))
    logits = (both[:, :LANES] + both[:, LANES:]) + _dot(xl, wh) + b_ref[...]
    lane = lax.broadcasted_iota(jnp.int32, logits.shape, 1)
    lane_f = lane.astype(F32)
    neg = -jnp.inf
    big = float(LANES)

    is_group = lane < N_GROUPS
    g_max = jnp.max(jnp.where(is_group, logits, neg), axis=-1, keepdims=True)
    g_sum = jnp.sum(jnp.where(is_group, jnp.exp(logits - g_max), 0.0), axis=-1, keepdims=True)
    p_top = 1.0 / g_sum
    grp = jnp.min(jnp.where(is_group & (logits == g_max), lane_f, big), axis=-1, keepdims=True)

    e_lo = N_GROUPS + grp * EXPERTS_PER_GROUP
    in_grp = (lane_f >= e_lo) & (lane_f < e_lo + EXPERTS_PER_GROUP)
    e_max = jnp.max(jnp.where(in_grp, logits, neg), axis=-1, keepdims=True)
    e_sum = jnp.sum(jnp.where(in_grp, jnp.exp(logits - e_max), 0.0), axis=-1, keepdims=True)
    i1 = jnp.min(jnp.where(in_grp & (logits == e_max), lane_f, big), axis=-1, keepdims=True)
    rest = in_grp & (lane_f != i1)
    e_2nd = jnp.max(jnp.where(rest, logits, neg), axis=-1, keepdims=True)
    i2 = jnp.min(jnp.where(rest & (logits == e_2nd), lane_f, big), axis=-1, keepdims=True)
    p1 = 1.0 / e_sum
    p2 = jnp.exp(e_2nd - e_max) / e_sum
    tot = p1 + p2
    gate_ref[...] = jnp.where(lane == 0, p_top * p1 / tot, jnp.where(lane == 1, p_top * p2 / tot, 0.0))

    e1 = i1 - N_GROUPS
    e2 = i2 - N_GROUPS
    hot = jnp.where((lane_f == e1) | (lane_f == e2), 1.0, 0.0)
    row = lax.broadcasted_iota(jnp.int32, (tm, tm), 0)
    col = lax.broadcasted_iota(jnp.int32, (tm, tm), 1)
    before = _dot(jnp.where(col < row, 1.0, 0.0).astype(BF16), _bf(hot)) + carry_ref[...]
    r1 = jnp.sum(jnp.where(lane_f == e1, before, 0.0), axis=-1, keepdims=True)
    r2 = jnp.sum(jnp.where(lane_f == e2, before, 0.0), axis=-1, keepdims=True)
    total = carry_ref[...] + jnp.sum(hot, axis=0, keepdims=True)
    carry_ref[...] = total
    cnt_ref[...] = jnp.broadcast_to(total, cnt_ref.shape).astype(jnp.int32)
    idx_ref[...] = jnp.where(lane == 0, e1, jnp.where(lane == 1, e2, jnp.where(lane == 2, r1, jnp.where(
        lane == 3, r2, 0.0)))).astype(jnp.int32)

    words = _pack_bf16_pairs(xn)
    for s in range(8):
        xg_ref[pl.ds(s, tm, stride=8), :] = words[:, s * LANES:(s + 1) * LANES]


def _router(x2d, gain, w_group, b_group, w_expert, b_expert):
    m, d = x2d.shape
    assert d == 2 * 8 * LANES
    tm = _row_tile(m, 688)
    pad = LANES - N_GROUPS - N_EXPERTS
    w = jnp.concatenate([w_group, w_expert, jnp.zeros((d, pad), F32)], axis=1)
    bias = jnp.concatenate([b_group, b_expert, jnp.zeros((pad,), F32)]).reshape(1, LANES)
    return pl.pallas_call(
        _router_kernel,
        grid=(m // tm,),
        in_specs=[
            pl.BlockSpec((tm, d), lambda i: (i, 0)),
            pl.BlockSpec((1, d), lambda i: (0, 0)),
            pl.BlockSpec((d, LANES), lambda i: (0, 0)),
            pl.BlockSpec((1, LANES), lambda i: (0, 0)),
        ],
        out_specs=[pl.BlockSpec((tm, LANES), lambda i: (i, 0)), pl.BlockSpec((tm, LANES), lambda i: (i, 0)),
                   pl.BlockSpec((tm * 8, LANES), lambda i: (i, 0)), pl.BlockSpec((8, LANES), lambda i: (0, 0))],
        out_shape=[jax.ShapeDtypeStruct((m, LANES), F32), jax.ShapeDtypeStruct((m, LANES), jnp.int32),
                   jax.ShapeDtypeStruct((m * 8, LANES), jnp.uint32), jax.ShapeDtypeStruct((8, LANES), jnp.int32)],
        scratch_shapes=[pltpu.VMEM((1, LANES), F32)],
        compiler_params=_params("arbitrary"),
        name="router",
    )(x2d, gain.reshape(1, d), w, bias)


def _invert_kernel(dest_ref, inv_ref):
    def clear(s, carry):
        inv_ref[s] = -1
        return carry

    lax.fori_loop(0, inv_ref.shape[0], clear, 0, unroll=8)

    def put(f, carry):
        inv_ref[dest_ref[f]] = f
        return carry

    lax.fori_loop(0, dest_ref.shape[0], put, 0, unroll=8)


def _invert(dest, p):
    assert p % 8 == 0 and dest.shape[0] % 8 == 0
    smem = pl.BlockSpec(memory_space=pltpu.SMEM)
    return pl.pallas_call(
        _invert_kernel, in_specs=[smem], out_specs=smem,
        out_shape=jax.ShapeDtypeStruct((p,), jnp.int32), name="moe_invert",
    )(dest)


def _expert_kernel(be_ref, nxt_ref, run_ref, nu_ref, src_ref, dst_ref, xg_hbm, w1_hbm, w3_hbm, w2_hbm, o_hbm,
                   xbuf, ybuf, xs_ref, w1f, w3f, w2f, w1s, w3s, w2s, sem_in, sem_out, sem_w, *, layer):
    i = pl.program_id(0)
    n_used = nu_ref[0]
    par = i % 2
    half = D_MODEL // 2
    half_e = D_EXPERT // 2

    def weight_copies(expert, slot):
        return [pltpu.make_async_copy(w_hbm.at[layer, expert], w_f.at[slot], sem_w.at[slot, j])
                for j, (w_hbm, w_f) in enumerate(((w1_hbm, w1f), (w3_hbm, w3f), (w2_hbm, w2f)))]

    def gather(block, r, slot):
        src = pl.multiple_of(src_ref[block * MOE_BLOCK + r], 8)
        return pltpu.make_async_copy(xg_hbm.at[pl.ds(src, 8), :], xbuf.at[slot, pl.ds(r * 8, 8), :], sem_in.at[slot])

    def scatter(block, r, slot):
        dst = pl.multiple_of(dst_ref[(block + 1) * MOE_BLOCK + r], 8)
        return pltpu.make_async_copy(ybuf.at[slot, pl.ds(r * 8, 8), :], o_hbm.at[pl.ds(dst, 8), :], sem_out.at[slot])

    def wait_gathers(slot):
        pltpu.make_async_copy(xbuf.at[1 - slot], xbuf.at[slot], sem_in.at[slot]).wait()

    def wait_scatters(slot):
        pltpu.make_async_copy(ybuf.at[slot], ybuf.at[1 - slot], sem_out.at[slot]).wait()

    @pl.when(i == 0)
    def _():
        ybuf[1] = jnp.zeros(ybuf.shape[1:], ybuf.dtype)
        n_real = o_hbm.shape[0] - 2 * MOE_BLOCK * 8
        fill1 = pltpu.make_async_copy(ybuf.at[1], o_hbm.at[pl.ds(n_real + MOE_BLOCK * 8, MOE_BLOCK * 8), :],
                                      sem_out.at[1])
        fill1.start()
        fill1.wait()
        pltpu.make_async_copy(ybuf.at[1], o_hbm.at[pl.ds(n_real, MOE_BLOCK * 8), :], sem_out.at[0]).start()

        def first(r, carry):
            gather(0, r, 0).start()
            return carry

        lax.fori_loop(0, MOE_BLOCK, first, 0, unroll=8)

    @pl.when(i < n_used)
    def _():
        expert = be_ref[i]
        wslot = run_ref[i] % 2

        @pl.when(i == 0)
        def _():
            for cp in weight_copies(expert, 0):
                cp.start()

        @pl.when((i == 0) | (expert != be_ref[jnp.maximum(i - 1, 0)]))
        def _():
            for cp in weight_copies(expert, wslot):
                cp.wait()

            @pl.when(nxt_ref[i] != expert)
            def _():
                for cp in weight_copies(nxt_ref[i], 1 - wslot):
                    cp.start()

            w1s[...] = _bf(w1f[wslot])
            w3s[...] = _bf(w3f[wslot])
            w2s[...] = _bf(w2f[wslot])

        wait_gathers(par)

        def move_rows(group, n_groups=6):
            lo, hi = group * MOE_BLOCK // n_groups, (group + 1) * MOE_BLOCK // n_groups
            for r in range(lo, hi):
                gather(i + 1, r, 1 - par).start(priority=r % 2)
                scatter(i - 1, r, 1 - par).start(priority=(r + 1) % 2)

        for s in range(8):
            words = xbuf[par, pl.ds(s, MOE_BLOCK, stride=8), :]
            xs_ref[:, s * LANES:(s + 1) * LANES] = _bf(_unpack_lo(words))
            xs_ref[:, half + s * LANES:half + (s + 1) * LANES] = _bf(_unpack_hi(words))
        move_rows(0)
        xb = xs_ref[...]
        h1a = _dot(xb, w1s[:, :half_e])
        move_rows(1)
        h3a = _dot(xb, w3s[:, :half_e])
        move_rows(2)
        act_a = _bf(_silu(h1a) * h3a)
        h1b = _dot(xb, w1s[:, half_e:])
        move_rows(3)
        h3b = _dot(xb, w3s[:, half_e:])
        move_rows(4)
        act_b = _bf(_silu(h1b) * h3b)
        y = _dot(act_a, w2s[:half_e, :])
        move_rows(5)
        y = y + _dot(act_b, w2s[half_e:, :])

        wait_scatters(par)
        words = _pack_bf16_pairs(y)
        for s in range(8):
            ybuf[par, pl.ds(s, MOE_BLOCK, stride=8), :] = words[:, s * LANES:(s + 1) * LANES]

        @pl.when(i == n_used - 1)
        def _():
            def last(r, carry):
                scatter(i, r, par).start()
                return carry

            lax.fori_loop(0, MOE_BLOCK, last, 0, unroll=8)
            wait_scatters(par)
            wait_scatters(1 - par)
            wait_gathers(1 - par)


def _moe(x2d, gain, w_group, b_group, w_expert, b_expert, w1, w3, w2, layer):
    n, d = x2d.shape
    gates_l, idx_l, xg, cnt = _router(x2d, gain, w_group, b_group, w_expert, b_expert)

    a = n * TOP_K
    n_blocks = -(-a // MOE_BLOCK) + N_EXPERTS
    p = n_blocks * MOE_BLOCK
    counts = cnt[0, :N_EXPERTS]
    padded = (counts + MOE_BLOCK - 1) // MOE_BLOCK * MOE_BLOCK
    pad_end = jnp.cumsum(padded)
    pad_start = pad_end - padded
    e_hot = idx_l[:, :TOP_K, None] == jnp.arange(N_EXPERTS, dtype=jnp.int32)
    dest = jnp.sum(jnp.where(e_hot, pad_start, 0), axis=-1) + idx_l[:, TOP_K:2 * TOP_K]
    dest = jnp.clip(dest.reshape(-1), 0, p - 1).astype(jnp.int32)
    blk0 = jnp.arange(n_blocks, dtype=jnp.int32) * MOE_BLOCK
    block_expert = jnp.minimum(jnp.searchsorted(pad_end, blk0, side="right"), N_EXPERTS - 1).astype(jnp.int32)
    n_used = (pad_end[-1] // MOE_BLOCK).astype(jnp.int32).reshape(1)
    experts = jnp.arange(N_EXPERTS, dtype=jnp.int32)
    later = jnp.where((experts[None, :] > experts[:, None]) & (counts[None, :] > 0), experts[None, :], N_EXPERTS)
    next_owner = jnp.min(later, axis=1)
    next_expert = jnp.where(next_owner < N_EXPERTS, next_owner, experts)[block_expert].astype(jnp.int32)
    run_index = (jnp.cumsum(jnp.concatenate([jnp.ones((1,), jnp.int32),
                                             (block_expert[1:] != block_expert[:-1]).astype(jnp.int32)])) - 1
                 ).astype(jnp.int32)
    codes = _invert(dest, p)
    slot = jnp.arange(p, dtype=jnp.int32)
    spare = TOP_K * n + (slot // MOE_BLOCK % 2) * MOE_BLOCK + slot % MOE_BLOCK
    src_tok = (jnp.maximum(codes, 0) >> 1) * 8
    dst_row = jnp.where(codes >= 0, (codes & 1) * n + (codes >> 1), spare) * 8
    lead = (TOP_K * n + MOE_BLOCK + jnp.arange(MOE_BLOCK, dtype=jnp.int32)) * 8
    dst_row = jnp.concatenate([lead, dst_row])

    hbm = pl.BlockSpec(memory_space=pl.ANY)
    out_rows = TOP_K * n + 2 * MOE_BLOCK
    out2 = pl.pallas_call(
        functools.partial(_expert_kernel, layer=layer),
        grid_spec=pltpu.PrefetchScalarGridSpec(
            num_scalar_prefetch=6,
            grid=(n_blocks,),
            in_specs=[hbm, hbm, hbm, hbm],
            out_specs=hbm,
            scratch_shapes=[
                pltpu.VMEM((2, MOE_BLOCK * 8, LANES), jnp.uint32), pltpu.VMEM((2, MOE_BLOCK * 8, LANES), jnp.uint32),
                pltpu.VMEM((MOE_BLOCK, d), BF16),
                pltpu.VMEM((2, d, D_EXPERT), F32), pltpu.VMEM((2, d, D_EXPERT), F32), pltpu.VMEM((2, D_EXPERT, d), F32),
                pltpu.VMEM((d, D_EXPERT), BF16), pltpu.VMEM((d, D_EXPERT), BF16), pltpu.VMEM((D_EXPERT, d), BF16),
                pltpu.SemaphoreType.DMA((2,)), pltpu.SemaphoreType.DMA((2,)), pltpu.SemaphoreType.DMA((2, 3)),
            ],
        ),
        out_shape=jax.ShapeDtypeStruct((out_rows * 8, LANES), jnp.uint32),
        compiler_params=_params("arbitrary"),
        name="moe_experts",
    )(block_expert, next_expert, run_index, n_used, src_tok, dst_row, xg, w1, w3, w2)
    return _combine(x2d, gates_l, out2)


def _combine_kernel(x_ref, gate_ref, a_ref, b_ref, o_ref):
    tm = x_ref.shape[0]
    half = x_ref.shape[1] // 2
    g0 = gate_ref[:, 0:1]
    g1 = gate_ref[:, 1:2]
    for s in range(8):
        wa = a_ref[pl.ds(s, tm, stride=8), :]
        wb = b_ref[pl.ds(s, tm, stride=8), :]
        lo = slice(s * LANES, (s + 1) * LANES)
        hi = slice(half + s * LANES, half + (s + 1) * LANES)
        o_ref[:, lo] = x_ref[:, lo] + (g0 * _unpack_lo(wa) + g1 * _unpack_lo(wb))
        o_ref[:, hi] = x_ref[:, hi] + (g0 * _unpack_hi(wa) + g1 * _unpack_hi(wb))


def _combine(x2d, gates, out2):
    n, d = x2d.shape
    tm = _row_tile(n, 688)
    return pl.pallas_call(
        _combine_kernel,
        grid=(n // tm,),
        in_specs=[
            pl.BlockSpec((tm, d), lambda i: (i, 0)),
            pl.BlockSpec((tm, LANES), lambda i: (i, 0)),
            pl.BlockSpec((tm * 8, LANES), lambda i: (i, 0)),
            pl.BlockSpec((tm * 8, LANES), lambda i: (n // tm + i, 0)),
        ],
        out_specs=pl.BlockSpec((tm, d), lambda i: (i, 0)),
        out_shape=jax.ShapeDtypeStruct((n, d), F32),
        compiler_params=_params("parallel"),
        name="moe_combine",
    )(x2d, gates, out2, out2)


def _even_layer(x, norm_g, w_in, conv_w, conv_b, b_i, b_f, a_norm, w_gate2, b_gate, b_norm, w_out):
    b, t, d = x.shape
    n = b * t
    a_w = 2 * A_HEADS * A_DK + 2 * A_HEADS * A_DV
    g_w = 2 * A_HEADS
    b_w = 2 * B_HEADS * B_DK + 2 * B_HEADS * B_DV
    main = a_w + b_w
    gate_cols = g_w + GATE_RANK
    w = _even_weight(w_in, a_w, g_w, b_w, GATE_RANK)
    z = _normproj(x.reshape(n, d), norm_g, w).reshape(b, t, main + MXU_DIM)

    ya = _mlstm(z, main, conv_w, conv_b, b_i, b_f, a_norm)

    wg = jnp.zeros((B_HEADS, MXU_DIM, B_DK), F32).at[:, g_w:g_w + GATE_RANK, :].set(
        w_gate2.reshape(GATE_RANK, B_HEADS, B_DK).transpose(1, 0, 2))
    dummy = jnp.zeros((1, B_HEADS * B_DK), F32)
    hp = 2
    hspec = pl.BlockSpec((1, hp * B_DK), lambda bi, g: (0, g))
    yb = _gla_call(
        z, B_HEADS, hp, B_DK, B_DV,
        (a_w, a_w + B_HEADS * B_DK, a_w + 2 * B_HEADS * B_DK, a_w + 2 * B_HEADS * B_DK + B_HEADS * B_DV, main),
        MXU_DIM,
        ((wg, pl.BlockSpec((hp, MXU_DIM, B_DK), lambda bi, g: (g, 0, 0))),
         (b_gate.reshape(1, -1), hspec), (dummy, hspec)),
        b_norm, "gla")
    return _outproj(ya.reshape(n, -1), yb.reshape(n, -1), w_out.astype(BF16), x.reshape(n, d)).reshape(b, t, d)


def _odd_layer(x, lb, norm_g, w_in, c_norm, q_a_norm, w_q_up, kv_a_norm, w_kv_up, q_norm, k_norm, w_out):
    b, t, d = x.shape
    n = b * t
    c_w = 2 * C_HEADS * C_DK + 2 * C_HEADS * C_DV
    swap = (jnp.arange(D_ROPE) + D_ROPE // 2) % D_ROPE
    kr0 = c_w + Q_LORA + KV_LORA
    used = kr0 + 2 * D_ROPE
    total = -(-used // MXU_DIM) * MXU_DIM
    z2 = _normproj(x.reshape(n, d), norm_g, _odd_weight(w_in, kr0))
    z = z2.reshape(b, t, total)

    hp = 4
    hspec = pl.BlockSpec((1, hp * C_DK), lambda bi, g: (0, g))
    yc = _gla_call(
        z, C_HEADS, hp, C_DK, C_DV,
        (0, C_HEADS * C_DK, 2 * C_HEADS * C_DK, 2 * C_HEADS * C_DK + C_HEADS * C_DV, C_HEADS * C_DK),
        C_DK,
        ((jnp.log(lb).reshape(1, -1), hspec), (jnp.log1p(-lb).reshape(1, -1), hspec), ((1.0 - lb).reshape(1, -1), hspec)),
        c_norm, "hgrn")

    dq = D_NOPE + D_ROPE
    wq = w_q_up.reshape(Q_LORA, D_HEADS, dq)
    wq_rope = wq[:, :, D_NOPE:]
    wq_p = jnp.concatenate([wq[:, :, :D_NOPE].reshape(Q_LORA, -1),
                            jnp.concatenate([wq_rope, wq_rope[:, :, swap]], axis=-1).reshape(Q_LORA, -1)],
                           axis=1).astype(BF16)
    wkv = w_kv_up.reshape(KV_LORA, D_HEADS, D_NOPE + D_V)
    wkv_p = jnp.concatenate([wkv[:, :, :D_NOPE].reshape(KV_LORA, -1), wkv[:, :, D_NOPE:].reshape(KV_LORA, -1)],
                            axis=1).astype(BF16)
    qn = _normproj(z2, q_a_norm, wq_p, x_col_block=c_w // Q_LORA).reshape(b, t, -1)
    kvn = _normproj(z2, kv_a_norm, wkv_p, x_col_block=(c_w + Q_LORA) // KV_LORA).reshape(b, t, -1)

    pos = jnp.arange(t, dtype=F32)
    half = D_ROPE // 2
    inv = ROPE_THETA ** (-jnp.arange(half, dtype=F32) / half)
    ang = pos[:, None] * inv[None, :]
    cos, sin = jnp.cos(ang), jnp.sin(ang)
    tab = jnp.concatenate([cos, cos, -sin, sin], axis=1)
    pair = lambda g: jnp.concatenate([g[D_NOPE:], g[D_NOPE:][swap]]).reshape(1, LANES)
    yd = _mla(qn, kvn, z, kr0, tab, q_norm[:D_NOPE].reshape(1, LANES), pair(q_norm),
              k_norm[:D_NOPE].reshape(1, LANES), pair(k_norm))
    return _outproj(yc.reshape(n, -1), yd.reshape(n, -1), w_out.astype(BF16), x.reshape(n, d)).reshape(b, t, d)


def kernel(x, meta_tokens, ab_norm, ab_w_in, a_conv_w, a_conv_b, a_b_i, a_b_f, a_head_norm, b_w_gate2, b_b_gate, b_head_norm, ab_w_out, cd_norm, cd_w_in, c_lower_bound, c_head_norm, d_q_a_norm, d_w_q_up, d_kv_a_norm, d_w_kv_up, d_q_norm, d_k_norm, cd_w_out, moe_norm, moe_w_group, moe_b_group, moe_w_expert, moe_b_expert, moe_w1, moe_w3, moe_w2):
    b = x.shape[0]
    depth = moe_norm.shape[0]
    h = jnp.concatenate([jnp.broadcast_to(meta_tokens.astype(x.dtype)[None], (b, N_META, D_MODEL)), x], axis=1)
    t = h.shape[1]
    lb_cum = jnp.cumsum(jax.nn.softmax(c_lower_bound.astype(F32), axis=0), axis=0)
    lower_bounds = lb_cum - lb_cum[0]
    for layer in range(depth):
        j = layer // 2
        if layer % 2 == 0:
            h = _even_layer(h, ab_norm[j], ab_w_in[j], a_conv_w[j], a_conv_b[j], a_b_i[j], a_b_f[j], a_head_norm[j],
                            b_w_gate2[j], b_b_gate[j], b_head_norm[j], ab_w_out[j])
        else:
            h = _odd_layer(h, lower_bounds[layer], cd_norm[j], cd_w_in[j], c_head_norm[j], d_q_a_norm[j],
                           d_w_q_up[j], d_kv_a_norm[j], d_w_kv_up[j], d_q_norm[j], d_k_norm[j], cd_w_out[j])
        h = _moe(h.reshape(b * t, D_MODEL), moe_norm[layer], moe_w_group[layer], moe_b_group[layer],
                 moe_w_expert[layer], moe_b_expert[layer], moe_w1, moe_w3, moe_w2, layer).reshape(b, t, D_MODEL)
    return h[:, N_META:]
```

```python
import functools
import math

import jax
import jax.numpy as jnp
from jax import lax
from jax.experimental import pallas as pl
from jax.experimental.pallas import tpu as pltpu

F32 = jnp.float32
BF16 = jnp.bfloat16
HIGHEST = lax.Precision.HIGHEST

D_MODEL = 2048
N_META = 16
CHUNK = 64
CONV_K = 4
EPS = 1e-6
A_HEADS, A_DK, A_DV = 4, 128, 256
B_HEADS, B_DK, B_DV = 4, 128, 256
GATE_RANK = 16
GATE_TAU = 16.0
C_HEADS, C_DK, C_DV = 8, 128, 128
D_HEADS, D_NOPE, D_ROPE, D_V = 8, 128, 64, 128
Q_LORA, KV_LORA = 512, 256
ROPE_THETA = 10000.0
N_GROUPS, EXPERTS_PER_GROUP = 4, 8
N_EXPERTS = N_GROUPS * EXPERTS_PER_GROUP
TOP_K = 2
D_EXPERT = 512

LANES = 128
MXU_DIM = 256
BF16_ROWS = 16
VMEM_LIMIT = 56 * 1024 * 1024
MOE_BLOCK = MXU_DIM
ATT_BLOCK = 256
ATT_PAD = ATT_BLOCK - N_META

_NT = (((1,), (1,)), ((), ()))
_TN = (((0,), (0,)), ((), ()))


def _dot(a, b, precision=None):
    return jnp.dot(a, b, preferred_element_type=F32, precision=precision)


def _dot_nt(a, b):
    return lax.dot_general(a, b, _NT, preferred_element_type=F32)


def _dot_tn(a, b):
    return lax.dot_general(a, b, _TN, preferred_element_type=F32)


def _bf(x):
    return x.astype(BF16)


def _split3(x):
    hi = _bf(x)
    rest = x - hi.astype(F32)
    mid = _bf(rest)
    return hi, mid, _bf(rest - mid.astype(F32))


def _log_sigmoid(x):
    return jnp.minimum(x, 0.0) - jnp.log1p(jnp.exp(-jnp.abs(x)))


def _sigmoid(x):
    return 1.0 / (1.0 + jnp.exp(-x))


def _silu(x):
    return x * _sigmoid(x)


def _row_tile(m, cap):
    best = None
    for t in range(BF16_ROWS, min(m, cap) + 1, BF16_ROWS):
        if m % t == 0:
            best = t
    assert best is not None, m
    return best


def _col_tile(n, cap):
    best = None
    for t in range(MXU_DIM, min(n, cap) + 1, MXU_DIM):
        if n % t == 0:
            best = t
    assert best is not None, n
    return best


def _params(*sem):
    return pltpu.CompilerParams(dimension_semantics=sem, vmem_limit_bytes=VMEM_LIMIT)


_RELAYOUT_ROWS = 256


def _even_weight_kernel(wa_ref, wb_ref, wc_ref, o_ref, *, n_plain, n_shift, shift, gate_cols):
    ob = pl.program_id(0)
    rows = o_ref.shape[0]
    chunks = rows // _RELAYOUT_ROWS

    @pl.when(ob < n_plain)
    def _():
        def body(c, carry):
            r = pl.ds(pl.multiple_of(c * _RELAYOUT_ROWS, _RELAYOUT_ROWS), _RELAYOUT_ROWS)
            o_ref[r, :] = _bf(wa_ref[r, :])
            return carry

        lax.fori_loop(0, chunks, body, 0)

    @pl.when((ob >= n_plain) & (ob < n_plain + n_shift))
    def _():
        def body(c, carry):
            r = pl.ds(pl.multiple_of(c * _RELAYOUT_ROWS, _RELAYOUT_ROWS), _RELAYOUT_ROWS)
            wide = jnp.concatenate([wa_ref[r, :], wb_ref[r, :]], axis=1)
            o_ref[r, :] = _bf(wide[:, shift:shift + MXU_DIM])
            return carry

        lax.fori_loop(0, chunks, body, 0)

    @pl.when(ob == n_plain + n_shift)
    def _():
        lane = lax.broadcasted_iota(jnp.int32, (_RELAYOUT_ROWS, LANES), 1)

        def body(c, carry):
            r = pl.ds(pl.multiple_of(c * _RELAYOUT_ROWS, _RELAYOUT_ROWS), _RELAYOUT_ROWS)
            first = jnp.where(lane < shift, wc_ref[r, :], jnp.where(lane < gate_cols, wb_ref[r, :], 0.0))
            o_ref[r, :] = _bf(jnp.concatenate([first, jnp.zeros_like(first)], axis=1))
            return carry

        lax.fori_loop(0, chunks, body, 0)


def _even_weight(w_in, a_w, g_w, b_w, rank):
    d = w_in.shape[0]
    assert a_w % MXU_DIM == 0 and b_w % MXU_DIM == 0 and g_w + rank <= LANES and d % _RELAYOUT_ROWS == 0
    n_plain, n_shift = a_w // MXU_DIM, b_w // MXU_DIM
    n_out = n_plain + n_shift + 1
    last = n_out - 1

    def b_index(ob):
        return (0, jnp.where(ob < n_plain, 0, jnp.where(ob < last, 2 * (ob + 1), (a_w + g_w + b_w) // LANES)))

    return pl.pallas_call(
        functools.partial(_even_weight_kernel, n_plain=n_plain, n_shift=n_shift, shift=g_w, gate_cols=g_w + rank),
        grid=(n_out,),
        in_specs=[
            pl.BlockSpec((d, MXU_DIM), lambda ob: (0, jnp.minimum(ob, last - 1))),
            pl.BlockSpec((d, LANES), b_index),
            pl.BlockSpec((d, LANES), lambda ob: (0, a_w // LANES)),
        ],
        out_specs=pl.BlockSpec((d, MXU_DIM), lambda ob: (0, ob)),
        out_shape=jax.ShapeDtypeStruct((d, n_out * MXU_DIM), BF16),
        compiler_params=_params("parallel"),
        name="even_weight",
    )(w_in, w_in, w_in)


def _odd_weight_kernel(w_ref, o_ref, *, n_plain):
    ob = pl.program_id(0)
    chunks = o_ref.shape[0] // _RELAYOUT_ROWS
    half = D_ROPE // 2

    def body(c, carry):
        r = pl.ds(pl.multiple_of(c * _RELAYOUT_ROWS, _RELAYOUT_ROWS), _RELAYOUT_ROWS)
        w = w_ref[r, :]

        @pl.when(ob < n_plain)
        def _():
            o_ref[r, :] = _bf(w)

        @pl.when(ob == n_plain)
        def _():
            pair = jnp.concatenate([w[:, :D_ROPE], w[:, half:D_ROPE], w[:, :half]], axis=1)
            o_ref[r, :] = _bf(jnp.concatenate([pair, jnp.zeros_like(pair)], axis=1))

        return carry

    lax.fori_loop(0, chunks, body, 0)


def _odd_weight(w_in, kr0):
    d = w_in.shape[0]
    assert kr0 % MXU_DIM == 0 and w_in.shape[1] == kr0 + D_ROPE and d % _RELAYOUT_ROWS == 0
    n_plain = kr0 // MXU_DIM
    return pl.pallas_call(
        functools.partial(_odd_weight_kernel, n_plain=n_plain),
        grid=(n_plain + 1,),
        in_specs=[pl.BlockSpec((d, MXU_DIM), lambda ob: (0, ob))],
        out_specs=pl.BlockSpec((d, MXU_DIM), lambda ob: (0, ob)),
        out_shape=jax.ShapeDtypeStruct((d, kr0 + MXU_DIM), BF16),
        compiler_params=_params("parallel"),
        name="odd_weight",
    )(w_in)


def _normproj_kernel(x_ref, g_ref, w_ref, o_ref, xs_ref):
    tm = xs_ref.shape[0]

    @pl.when(pl.program_id(1) == 0)
    def _():
        def body(c, carry):
            r0 = pl.multiple_of(c * BF16_ROWS, BF16_ROWS)
            x = x_ref[pl.ds(r0, BF16_ROWS), :]
            ms = jnp.mean(x * x, axis=-1, keepdims=True)
            xs_ref[pl.ds(r0, BF16_ROWS), :] = _bf(x * lax.rsqrt(ms + EPS) * g_ref[...])
            return carry

        lax.fori_loop(0, tm // BF16_ROWS, body, 0, unroll=8)

    o_ref[...] = _dot(xs_ref[...], w_ref[...]).astype(o_ref.dtype)


def _normproj(x2d, gain, w, *, x_col_block=0, out_dtype=F32):
    m = x2d.shape[0]
    k, n = w.shape
    tm = _row_tile(m, 688)
    tn = _col_tile(n, 1280)
    return pl.pallas_call(
        _normproj_kernel,
        grid=(m // tm, n // tn),
        in_specs=[
            pl.BlockSpec((tm, k), lambda i, j: (i, x_col_block)),
            pl.BlockSpec((1, k), lambda i, j: (0, 0)),
            pl.BlockSpec((k, tn), lambda i, j: (0, j)),
        ],
        out_specs=pl.BlockSpec((tm, tn), lambda i, j: (i, j)),
        out_shape=jax.ShapeDtypeStruct((m, n), out_dtype),
        scratch_shapes=[pltpu.VMEM((tm, k), BF16)],
        compiler_params=_params("parallel", "arbitrary"),
        name="normproj",
    )(x2d, gain.reshape(1, k).astype(F32), w)


def _outproj_kernel(ya_ref, yb_ref, w_ref, r_ref, o_ref):
    ka = ya_ref.shape[1]
    acc = _dot(ya_ref[...], w_ref[:ka, :]) + _dot(yb_ref[...], w_ref[ka:, :])
    o_ref[...] = r_ref[...] + acc


def _outproj(ya, yb, w, res):
    m, ka = ya.shape
    kb = yb.shape[1]
    n = w.shape[1]
    tm = _row_tile(m, 1376)
    tn = _col_tile(n, 1024)
    return pl.pallas_call(
        _outproj_kernel,
        grid=(m // tm, n // tn),
        in_specs=[
            pl.BlockSpec((tm, ka), lambda i, j: (i, 0)),
            pl.BlockSpec((tm, kb), lambda i, j: (i, 0)),
            pl.BlockSpec((ka + kb, tn), lambda i, j: (0, j)),
            pl.BlockSpec((tm, tn), lambda i, j: (i, j)),
        ],
        out_specs=pl.BlockSpec((tm, tn), lambda i, j: (i, j)),
        out_shape=jax.ShapeDtypeStruct((m, n), F32),
        compiler_params=_params("parallel", "arbitrary"),
        name="outproj",
    )(ya, yb, w, res)


def _mlstm_kernel(bi_ref, bf_ref, q_ref, k_ref, v_ref, og_ref, gt_ref,
                  cwq_ref, cwk_ref, cbq_ref, cbk_ref, hn_ref, o_ref, c_ref, n_ref, m_ref, *, heads, group):
    t_total = q_ref.shape[1]
    n_chunks = (t_total - N_META) // CHUNK
    head0 = pl.program_id(1) * heads

    c_ref[...] = jnp.zeros_like(c_ref)
    n_ref[...] = jnp.zeros_like(n_ref)
    m_ref[...] = jnp.zeros_like(m_ref)

    def conv(win, cw, cb, length):
        y = cb
        for j in range(CONV_K):
            y = y + win[8 - (CONV_K - 1) + j:8 - (CONV_K - 1) + j + length, :] * cw[j:j + 1, :]
        return _silu(y)

    hs = range(heads)
    kcol = [slice(hh * A_DK, (hh + 1) * A_DK) for hh in hs]
    vcol = [slice(hh * A_DV, (hh + 1) * A_DV) for hh in hs]

    def gates(hh, blk, blk_parts, length, causal, upper):
        head = head0 + hh
        b_i = bi_ref[head]
        b_f = bf_ref[head]
        lane = lax.broadcasted_iota(jnp.int32, (length, LANES), 1)
        ig_c = jnp.sum(jnp.where(lane == head, blk, 0.0), axis=1, keepdims=True) + b_i
        lf_c = _log_sigmoid(jnp.sum(jnp.where(lane == A_HEADS + head, blk, 0.0), axis=1, keepdims=True) + b_f)
        sel_r = lax.broadcasted_iota(jnp.int32, (8, LANES), 0)
        sel_l = lax.broadcasted_iota(jnp.int32, (8, LANES), 1)
        sel = jnp.where(sel_l == head + A_HEADS * sel_r, 1.0, 0.0).astype(BF16)
        rows = _dot_nt(sel, blk_parts[0]) + (_dot_nt(sel, blk_parts[1]) + _dot_nt(sel, blk_parts[2]))
        ig_r = rows[0:1, :] + b_i
        lf_r = _log_sigmoid(rows[1:2, :] + b_f)
        b_c = jnp.sum(jnp.where(causal, lf_r, 0.0), axis=1, keepdims=True)
        b_r = jnp.sum(jnp.where(upper, lf_c, 0.0), axis=0, keepdims=True)
        b_end = b_c[length - 1:length, :]
        w_end = b_end - b_c + ig_c
        m_loc = jnp.max(w_end, axis=0, keepdims=True)
        d = jnp.where(causal, b_c - b_r + ig_r, -jnp.inf)
        return b_c, b_end, jnp.exp(w_end - m_loc), m_loc, d, jnp.max(d, axis=1, keepdims=True)

    def local_stage(o, length, qwin, kwin):
        row = lax.broadcasted_iota(jnp.int32, (length, length), 0)
        col = lax.broadcasted_iota(jnp.int32, (length, length), 1)
        causal = col <= row
        blk = gt_ref[0, pl.ds(o, length), 0:LANES]
        blk_parts = _split3(blk)
        gt = [gates(hh, blk, blk_parts, length, causal, row <= col) for hh in hs]
        q = [conv(qwin[hh], cwq_ref[:, kcol[hh]], cbq_ref[:, kcol[hh]], length) for hh in hs]
        k = [conv(kwin[hh], cwk_ref[:, kcol[hh]], cbk_ref[:, kcol[hh]], length) * (A_DK ** -0.5) for hh in hs]
        vb = [_bf(v_ref[0, pl.ds(o, length), vcol[hh]]) for hh in hs]
        qb = [_bf(q[hh]) for hh in hs]
        k_w = [k[hh] * gt[hh][2] for hh in hs]
        qk = [_dot_nt(qb[hh], _bf(k[hh])) for hh in hs]
        c_loc = [_dot_tn(_bf(k_w[hh]), vb[hh]) for hh in hs]
        n_loc = [jnp.sum(k_w[hh], axis=0, keepdims=True) for hh in hs]
        return gt, q, qb, vb, qk, c_loc, n_loc

    def state_stage(o, length, staged, state):
        gt, q, qb, vb, qk, c_loc, n_loc = staged
        c_in, n_in, m_in = state
        q_c = [_dot(qb[hh], _bf(c_in[hh])) for hh in hs]
        s, a_t, m_t, q_n, c_out, n_out, m_out = [], [], [], [], [], [], []
        for hh in hs:
            b_c, b_end, _, m_loc, d, d_max = gt[hh]
            inter = b_c + m_in[hh]
            m_t.append(jnp.maximum(inter, d_max))
            s.append(qk[hh] * jnp.exp(d - m_t[hh]))
            a_t.append(jnp.exp(inter - m_t[hh]))
            m_new = jnp.maximum(b_end + m_in[hh], m_loc)
            a = jnp.exp(b_end + m_in[hh] - m_new)
            c = jnp.exp(m_loc - m_new)
            c_out.append(a * c_in[hh] + c * c_loc[hh])
            n_out.append(a * n_in[hh] + c * n_loc[hh])
            m_out.append(m_new)
            q_n.append(jnp.sum(q[hh] * n_in[hh], axis=1, keepdims=True))
        num = [_dot(_bf(s[hh]), vb[hh]) + a_t[hh] * q_c[hh] for hh in hs]
        for hh in hs:
            den = jnp.sum(s[hh], axis=1, keepdims=True) + a_t[hh] * q_n[hh]
            h = num[hh] / jnp.maximum(jnp.abs(den), jnp.exp(-m_t[hh]))
            hn = h * lax.rsqrt(jnp.mean(h * h, axis=-1, keepdims=True) + EPS) * hn_ref[:, vcol[hh]]
            y = _sigmoid(og_ref[0, pl.ds(o, length), vcol[hh]]) * hn
            o_ref[0, pl.ds(o, length), vcol[hh]] = y.astype(o_ref.dtype)
        return c_out, n_out, m_out

    def sweep(offsets, length, qwins, kwins):
        staged = [local_stage(o, length, qw, kw) for o, qw, kw in zip(offsets, qwins, kwins)]
        state = ([c_ref[hh] for hh in hs], [n_ref[hh] for hh in hs], [m_ref[hh] for hh in hs])
        for o, stg in zip(offsets, staged):
            state = state_stage(o, length, stg, state)
        for hh in hs:
            c_ref[hh] = state[0][hh]
            n_ref[hh] = state[1][hh]
            m_ref[hh] = state[2][hh]

    zeros8 = jnp.zeros((8, A_DK), F32)
    sweep([0], N_META,
          [[jnp.concatenate([zeros8, q_ref[0, 0:N_META, kcol[hh]]], axis=0) for hh in hs]],
          [[jnp.concatenate([zeros8, k_ref[0, 0:N_META, kcol[hh]]], axis=0) for hh in hs]])
    assert n_chunks % group == 0

    def body(c, carry):
        offsets = [pl.multiple_of(N_META + (c * group + j) * CHUNK, BF16_ROWS) for j in range(group)]
        starts = [pl.multiple_of(N_META - 8 + (c * group + j) * CHUNK, 8) for j in range(group)]
        sweep(offsets, CHUNK,
              [[q_ref[0, pl.ds(w0, CHUNK + 8), kcol[hh]] for hh in hs] for w0 in starts],
              [[k_ref[0, pl.ds(w0, CHUNK + 8), kcol[hh]] for hh in hs] for w0 in starts])
        return carry

    lax.fori_loop(0, n_chunks // group, body, 0)


def _mlstm(z, gate_col, conv_w, conv_b, b_i, b_f, head_norm, heads=2):
    b, t, _ = z.shape
    hk = A_HEADS * A_DK
    wk, wv = heads * A_DK, heads * A_DV
    smem = pl.BlockSpec(memory_space=pltpu.SMEM)
    col = lambda width, off: (lambda bi, g: (bi, 0, off // width + g))
    return pl.pallas_call(
        functools.partial(_mlstm_kernel, heads=heads, group=4),
        grid=(b, A_HEADS // heads),
        in_specs=[
            smem, smem,
            pl.BlockSpec((1, t, wk), col(wk, 0)),
            pl.BlockSpec((1, t, wk), col(wk, hk)),
            pl.BlockSpec((1, t, wv), col(wv, 2 * hk)),
            pl.BlockSpec((1, t, wv), col(wv, 2 * hk + A_HEADS * A_DV)),
            pl.BlockSpec((1, t, MXU_DIM), lambda bi, g: (bi, 0, gate_col // MXU_DIM)),
            pl.BlockSpec((CONV_K, wk), lambda bi, g: (0, g)),
            pl.BlockSpec((CONV_K, wk), lambda bi, g: (0, A_HEADS // heads + g)),
            pl.BlockSpec((1, wk), lambda bi, g: (0, g)),
            pl.BlockSpec((1, wk), lambda bi, g: (0, A_HEADS // heads + g)),
            pl.BlockSpec((1, wv), lambda bi, g: (0, g)),
        ],
        out_specs=pl.BlockSpec((1, t, wv), lambda bi, g: (bi, 0, g)),
        out_shape=jax.ShapeDtypeStruct((b, t, A_HEADS * A_DV), BF16),
        scratch_shapes=[pltpu.VMEM((heads, A_DK, A_DV), F32), pltpu.VMEM((heads, 1, A_DK), F32),
                        pltpu.VMEM((heads, 1, 1), F32)],
        compiler_params=_params("parallel", "parallel"),
        name="mlstm",
    )(b_i, b_f, z, z, z, z, z, conv_w, conv_w, conv_b.reshape(1, -1), conv_b.reshape(1, -1),
      head_norm.reshape(1, -1))


def _gla_kernel(q_ref, k_ref, v_ref, og_ref, g_ref, p0_ref, p1_ref, p2_ref, hn_ref, o_ref, st_ref, *pre_ref,
                mode, heads, group):
    t_total = q_ref.shape[1]
    n_chunks = (t_total - N_META) // CHUNK
    dv, dk = st_ref.shape[1:]
    st_ref[...] = jnp.zeros_like(st_ref)

    hs = range(heads)
    kcol = [slice(hh * dk, (hh + 1) * dk) for hh in hs]
    vcol = [slice(hh * dv, (hh + 1) * dv) for hh in hs]

    if mode == "gla":
        gate = g_ref[0]
        gate_hi = _bf(gate)
        gate_lo = _bf(gate - gate_hi.astype(F32))
        for hh in hs:
            w = p0_ref[hh]
            w_hi = _bf(w)
            w_lo = _bf(w - w_hi.astype(F32))
            both = _dot(gate_hi, jnp.concatenate([w_hi, w_lo], axis=1))
            pre_ref[0][hh] = (both[:, :dk] + both[:, dk:]) + _dot(gate_lo, w_hi) + p1_ref[:, kcol[hh]]

    def gate_inputs(hh, o, length):
        q = q_ref[0, pl.ds(o, length), kcol[hh]]
        if mode == "gla":
            pre = pre_ref[0][hh, pl.ds(o, length), :]
            return q * (dk ** -0.5), k_ref[0, pl.ds(o, length), kcol[hh]], _log_sigmoid(pre) / GATE_TAU
        fpre = g_ref[0, pl.ds(o, length), kcol[hh]]
        a = p0_ref[:, kcol[hh]]
        bb = p1_ref[:, kcol[hh]] + _log_sigmoid(fpre)
        lg = jnp.maximum(a, bb) + jnp.log1p(jnp.exp(-jnp.abs(a - bb)))
        return q, p2_ref[:, kcol[hh]] * _sigmoid(-fpre), lg

    def cumsum_time(tri, lg):
        parts = _dot(tri, jnp.concatenate(_split3(lg), axis=1))
        return parts[:, :dk] + (parts[:, dk:2 * dk] + parts[:, 2 * dk:])

    def local_stage(o, length):
        row = lax.broadcasted_iota(jnp.int32, (length, length), 0)
        col = lax.broadcasted_iota(jnp.int32, (length, length), 1)
        causal = col <= row
        tri = jnp.where(causal, 1.0, 0.0).astype(BF16)
        qkl = [gate_inputs(hh, o, length) for hh in hs]
        vb = [_bf(v_ref[0, pl.ds(o, length), vcol[hh]]) for hh in hs]
        g = [cumsum_time(tri, qkl[hh][2]) for hh in hs]
        g_end = [g[hh][length - 1:length, :] for hh in hs]
        g_mid = [g[hh][length // 2:length // 2 + 1, :] for hh in hs]
        s = [_dot_nt(_bf(qkl[hh][0] * jnp.exp(g[hh] - g_mid[hh])), _bf(qkl[hh][1] * jnp.exp(g_mid[hh] - g[hh])))
             for hh in hs]
        q_dec = [_bf(qkl[hh][0] * jnp.exp(g[hh])) for hh in hs]
        local = [_dot_tn(vb[hh], _bf(qkl[hh][1] * jnp.exp(g_end[hh] - g[hh]))) for hh in hs]
        intra = [_dot(_bf(jnp.where(causal, s[hh], 0.0)), vb[hh]) for hh in hs]
        return q_dec, [jnp.exp(ge) for ge in g_end], local, intra

    def state_stage(o, length, staged, st_in):
        q_dec, decay, local, intra = staged
        inter = [_dot_nt(q_dec[hh], _bf(st_in[hh])) for hh in hs]
        st_out = [st_in[hh] * decay[hh] + local[hh] for hh in hs]
        for hh in hs:
            out = intra[hh] + inter[hh]
            hn = out * lax.rsqrt(jnp.mean(out * out, axis=-1, keepdims=True) + EPS) * hn_ref[:, vcol[hh]]
            og = og_ref[0, pl.ds(o, length), vcol[hh]]
            gate = _silu(og) if mode == "gla" else _sigmoid(og)
            o_ref[0, pl.ds(o, length), vcol[hh]] = (gate * hn).astype(o_ref.dtype)
        return st_out

    def sweep(offsets, length):
        staged = [local_stage(o, length) for o in offsets]
        st = [st_ref[hh] for hh in hs]
        for o, stg in zip(offsets, staged):
            st = state_stage(o, length, stg, st)
        for hh in hs:
            st_ref[hh] = st[hh]

    sweep([0], N_META)
    assert n_chunks % group == 0

    def body(c, carry):
        o = pl.multiple_of(N_META + c * (group * CHUNK), BF16_ROWS)
        sweep([pl.multiple_of(o + j * CHUNK, BF16_ROWS) for j in range(group)], CHUNK)
        return carry

    lax.fori_loop(0, n_chunks // group, body, 0)


def _gla_call(z, n_heads, heads, dk, dv, blocks, gate_width, params, head_norm, mode):
    b, t, _ = z.shape
    q0, k0, v0, og0, g0 = blocks
    zspec = lambda width, off, grouped=True: pl.BlockSpec(
        (1, t, width), (lambda bi, g: (bi, 0, off // width + (g if grouped else 0))))
    (p0, s0), (p1, s1), (p2, s2) = params
    gate_spec = zspec(gate_width, g0, grouped=False) if mode == "gla" else zspec(heads * dk, g0)
    return pl.pallas_call(
        functools.partial(_gla_kernel, mode=mode, heads=heads, group=4),
        grid=(b, n_heads // heads),
        in_specs=[
            zspec(heads * dk, q0), zspec(heads * dk, k0), zspec(heads * dv, v0), zspec(heads * dv, og0),
            gate_spec, s0, s1, s2,
            pl.BlockSpec((1, heads * dv), lambda bi, g: (0, g)),
        ],
        out_specs=pl.BlockSpec((1, t, heads * dv), lambda bi, g: (bi, 0, g)),
        out_shape=jax.ShapeDtypeStruct((b, t, n_heads * dv), BF16),
        scratch_shapes=[pltpu.VMEM((heads, dv, dk), F32)] + (
            [pltpu.VMEM((heads, t, dk), F32)] if mode == "gla" else []),
        compiler_params=_params("parallel", "parallel"),
        name="gla_" + mode,
    )(z, z, z, z, z, p0, p1, p2, head_norm.reshape(1, -1))


def _mla_kernel(qn_ref, qr_ref, kn_ref, v_ref, kr_ref, tab_ref, gqn_ref, gqr_ref, gkn_ref, gkr_ref,
                o_ref, qf_ref, kf_ref, vf_ref):
    t_total = qn_ref.shape[1]
    n_blocks = (ATT_PAD + t_total) // ATT_BLOCK
    dqk = D_NOPE + D_ROPE
    scale = dqk ** -0.5
    rows = _row_tile(t_total, 768)

    qf_ref[0:ATT_PAD, :] = jnp.zeros((ATT_PAD, 2 * LANES), BF16)
    kf_ref[0:ATT_PAD, :] = jnp.zeros((ATT_PAD, 2 * LANES), BF16)
    vf_ref[0:ATT_PAD, :] = jnp.zeros((ATT_PAD, D_V), BF16)

    def rope_pair(x, gains, tab):
        p = x * gains * tab
        return p + pltpu.roll(p, D_ROPE, 1)

    def prep(c):
        r0 = c * rows
        dst = ATT_PAD + c * rows
        tab = tab_ref[pl.ds(r0, rows), :]
        qn = qn_ref[0, pl.ds(r0, rows), :]
        qr = qr_ref[0, pl.ds(r0, rows), :]
        ssq = jnp.sum(qn * qn + 0.5 * (qr * qr), axis=-1, keepdims=True)
        rq = lax.rsqrt(ssq / dqk + EPS) * scale
        qf_ref[pl.ds(dst, rows), 0:LANES] = _bf(qn * gqn_ref[...] * rq)
        qf_ref[pl.ds(dst, rows), LANES:2 * LANES] = _bf(qr * gqr_ref[...] * tab * rq)
        kn = kn_ref[0, pl.ds(r0, rows), :]
        kr = kr_ref[0, pl.ds(r0, rows), :]
        ssk = jnp.sum(kn * kn + 0.5 * (kr * kr), axis=-1, keepdims=True)
        rk = lax.rsqrt(ssk / dqk + EPS)
        kf_ref[pl.ds(dst, rows), 0:LANES] = _bf(kn * gkn_ref[...] * rk)
        kf_ref[pl.ds(dst, rows), LANES:2 * LANES] = _bf(rope_pair(kr, gkr_ref[...], tab) * rk)
        vf_ref[pl.ds(dst, rows), :] = _bf(v_ref[0, pl.ds(r0, rows), :])

    for c in range(t_total // rows):
        prep(c)

    qpos = lax.broadcasted_iota(jnp.int32, (ATT_BLOCK, ATT_BLOCK), 0)
    kpos = lax.broadcasted_iota(jnp.int32, (ATT_BLOCK, ATT_BLOCK), 1)
    neg = -jnp.inf

    def scores(qi):
        q = qf_ref[qi * ATT_BLOCK:(qi + 1) * ATT_BLOCK, :]
        return _dot_nt(q, kf_ref[0:(qi + 1) * ATT_BLOCK, :])

    s_next = scores(0)
    for qi in range(n_blocks):
        s = s_next
        if qi + 1 < n_blocks:
            s_next = scores(qi + 1)
        parts = [s[:, j * ATT_BLOCK:(j + 1) * ATT_BLOCK] for j in range(qi + 1)]
        parts[0] = jnp.where(kpos >= ATT_PAD, parts[0], neg)
        parts[qi] = jnp.where(kpos <= qpos, parts[qi], neg)
        top = functools.reduce(jnp.maximum, parts)
        m = jnp.max(top, axis=-1, keepdims=True)
        if qi == 0:
            m = jnp.where(m == neg, 0.0, m)
        probs = [jnp.exp(part - m) for part in parts]
        l = jnp.sum(functools.reduce(jnp.add, probs), axis=-1, keepdims=True)
        pv = _dot(jnp.concatenate([_bf(pr) for pr in probs], axis=1), vf_ref[0:(qi + 1) * ATT_BLOCK, :])
        if qi == 0:
            out = pv / jnp.where(l == 0.0, 1.0, l)
            o_ref[0, 0:N_META, :] = out[ATT_PAD:, :].astype(o_ref.dtype)
        else:
            dst = qi * ATT_BLOCK - ATT_PAD
            o_ref[0, dst:dst + ATT_BLOCK, :] = (pv / l).astype(o_ref.dtype)


def _mla(qn, kvn, z, kr_col, tab, gqn, gqr, gkn, gkr):
    b, t, _ = qn.shape
    assert (ATT_PAD + t) % ATT_BLOCK == 0 and t % (3 * BF16_ROWS) == 0
    tp = ATT_PAD + t
    hspec = lambda off: pl.BlockSpec((1, t, LANES), lambda bi, h: (bi, 0, off + h))
    gspec = pl.BlockSpec((1, LANES), lambda bi, h: (0, 0))
    return pl.pallas_call(
        _mla_kernel,
        grid=(b, D_HEADS),
        in_specs=[
            hspec(0), hspec(D_HEADS), hspec(0), hspec(D_HEADS),
            pl.BlockSpec((1, t, LANES), lambda bi, h: (bi, 0, kr_col // LANES)),
            pl.BlockSpec((t, LANES), lambda bi, h: (0, 0)),
            gspec, gspec, gspec, gspec,
        ],
        out_specs=pl.BlockSpec((1, t, D_V), lambda bi, h: (bi, 0, h)),
        out_shape=jax.ShapeDtypeStruct((b, t, D_HEADS * D_V), BF16),
        scratch_shapes=[pltpu.VMEM((tp, 2 * LANES), BF16), pltpu.VMEM((tp, 2 * LANES), BF16),
                        pltpu.VMEM((tp, D_V), BF16)],
        compiler_params=_params("parallel", "parallel"),
        name="mla",
    )(qn, qn, kvn, kvn, z, tab, gqn, gqr, gkn, gkr)


def _pack_bf16_pairs(v):
    w = v.shape[1] // 2
    bits = pltpu.bitcast(_bf(v).astype(F32), jnp.uint32)
    return (bits[:, :w] >> 16) | (bits[:, w:] & jnp.uint32(0xFFFF0000))


def _unpack_lo(words):
    return pltpu.bitcast(words << 16, F32)


def _unpack_hi(words):
    return pltpu.bitcast(words & jnp.uint32(0xFFFF0000), F32)


def _router_kernel(x_ref, g_ref, w_ref, b_ref, gate_ref, idx_ref, xg_ref, cnt_ref, carry_ref):
    tm = x_ref.shape[0]

    @pl.when(pl.program_id(0) == 0)
    def _():
        carry_ref[...] = jnp.zeros_like(carry_ref)

    x = x_ref[...]
    ms = jnp.mean(x * x, axis=-1, keepdims=True)
    xn = x * lax.rsqrt(ms + EPS) * g_ref[...]
    xh = _bf(xn)
    xl = _bf(xn - xh.astype(F32))
    w = w_ref[...]
    wh = _bf(w)
    wl = _bf(w - wh.astype(F32))
    both = _dot(xh, jnp.concatenate([wh, wl], axis=1))
    logits = (both[:, :LANES] + both[:, LANES:]) + _dot(xl, wh) + b_ref[...]
    lane = lax.broadcasted_iota(jnp.int32, logits.shape, 1)
    lane_f = lane.astype(F32)
    neg = -jnp.inf
    big = float(LANES)

    is_group = lane < N_GROUPS
    g_max = jnp.max(jnp.where(is_group, logits, neg), axis=-1, keepdims=True)
    g_sum = jnp.sum(jnp.where(is_group, jnp.exp(logits - g_max), 0.0), axis=-1, keepdims=True)
    p_top = 1.0 / g_sum
    grp = jnp.min(jnp.where(is_group & (logits == g_max), lane_f, big), axis=-1, keepdims=True)

    e_lo = N_GROUPS + grp * EXPERTS_PER_GROUP
    in_grp = (lane_f >= e_lo) & (lane_f < e_lo + EXPERTS_PER_GROUP)
    e_max = jnp.max(jnp.where(in_grp, logits, neg), axis=-1, keepdims=True)
    e_sum = jnp.sum(jnp.where(in_grp, jnp.exp(logits - e_max), 0.0), axis=-1, keepdims=True)
    i1 = jnp.min(jnp.where(in_grp & (logits == e_max), lane_f, big), axis=-1, keepdims=True)
    rest = in_grp & (lane_f != i1)
    e_2nd = jnp.max(jnp.where(rest, logits, neg), axis=-1, keepdims=True)
    i2 = jnp.min(jnp.where(rest & (logits == e_2nd), lane_f, big), axis=-1, keepdims=True)
    p1 = 1.0 / e_sum
    p2 = jnp.exp(e_2nd - e_max) / e_sum
    tot = p1 + p2
    gate_ref[...] = jnp.where(lane == 0, p_top * p1 / tot, jnp.where(lane == 1, p_top * p2 / tot, 0.0))

    e1 = i1 - N_GROUPS
    e2 = i2 - N_GROUPS
    hot = jnp.where((lane_f == e1) | (lane_f == e2), 1.0, 0.0)
    row = lax.broadcasted_iota(jnp.int32, (tm, tm), 0)
    col = lax.broadcasted_iota(jnp.int32, (tm, tm), 1)
    before = _dot(jnp.where(col < row, 1.0, 0.0).astype(BF16), _bf(hot)) + carry_ref[...]
    r1 = jnp.sum(jnp.where(lane_f == e1, before, 0.0), axis=-1, keepdims=True)
    r2 = jnp.sum(jnp.where(lane_f == e2, before, 0.0), axis=-1, keepdims=True)
    total = carry_ref[...] + jnp.sum(hot, axis=0, keepdims=True)
    carry_ref[...] = total
    cnt_ref[...] = jnp.broadcast_to(total, cnt_ref.shape).astype(jnp.int32)
    idx_ref[...] = jnp.where(lane == 0, e1, jnp.where(lane == 1, e2, jnp.where(lane == 2, r1, jnp.where(
        lane == 3, r2, 0.0)))).astype(jnp.int32)

    words = _pack_bf16_pairs(xn)
    for s in range(8):
        xg_ref[pl.ds(s, tm, stride=8), :] = words[:, s * LANES:(s + 1) * LANES]


def _router(x2d, gain, w_group, b_group, w_expert, b_expert):
    m, d = x2d.shape
    assert d == 2 * 8 * LANES
    tm = _row_tile(m, 688)
    pad = LANES - N_GROUPS - N_EXPERTS
    w = jnp.concatenate([w_group, w_expert, jnp.zeros((d, pad), F32)], axis=1)
    bias = jnp.concatenate([b_group, b_expert, jnp.zeros((pad,), F32)]).reshape(1, LANES)
    return pl.pallas_call(
        _router_kernel,
        grid=(m // tm,),
        in_specs=[
            pl.BlockSpec((tm, d), lambda i: (i, 0)),
            pl.BlockSpec((1, d), lambda i: (0, 0)),
            pl.BlockSpec((d, LANES), lambda i: (0, 0)),
            pl.BlockSpec((1, LANES), lambda i: (0, 0)),
        ],
        out_specs=[pl.BlockSpec((tm, LANES), lambda i: (i, 0)), pl.BlockSpec((tm, LANES), lambda i: (i, 0)),
                   pl.BlockSpec((tm * 8, LANES), lambda i: (i, 0)), pl.BlockSpec((8, LANES), lambda i: (0, 0))],
        out_shape=[jax.ShapeDtypeStruct((m, LANES), F32), jax.ShapeDtypeStruct((m, LANES), jnp.int32),
                   jax.ShapeDtypeStruct((m * 8, LANES), jnp.uint32), jax.ShapeDtypeStruct((8, LANES), jnp.int32)],
        scratch_shapes=[pltpu.VMEM((1, LANES), F32)],
        compiler_params=_params("arbitrary"),
        name="router",
    )(x2d, gain.reshape(1, d), w, bias)


def _invert_kernel(dest_ref, inv_ref):
    def clear(s, carry):
        inv_ref[s] = -1
        return carry

    lax.fori_loop(0, inv_ref.shape[0], clear, 0, unroll=8)

    def put(f, carry):
        inv_ref[dest_ref[f]] = f
        return carry

    lax.fori_loop(0, dest_ref.shape[0], put, 0, unroll=8)


def _invert(dest, p):
    assert p % 8 == 0 and dest.shape[0] % 8 == 0
    smem = pl.BlockSpec(memory_space=pltpu.SMEM)
    return pl.pallas_call(
        _invert_kernel, in_specs=[smem], out_specs=smem,
        out_shape=jax.ShapeDtypeStruct((p,), jnp.int32), name="moe_invert",
    )(dest)


def _expert_kernel(be_ref, nxt_ref, run_ref, nu_ref, src_ref, dst_ref, xg_hbm, w1_hbm, w3_hbm, w2_hbm, o_hbm,
                   xbuf, ybuf, xs_ref, w1f, w3f, w2f, w1s, w3s, w2s, sem_in, sem_out, sem_w, *, layer):
    i = pl.program_id(0)
    n_used = nu_ref[0]
    par = i % 2
    half = D_MODEL // 2
    half_e = D_EXPERT // 2

    def weight_copies(expert, slot):
        return [pltpu.make_async_copy(w_hbm.at[layer, expert], w_f.at[slot], sem_w.at[slot, j])
                for j, (w_hbm, w_f) in enumerate(((w1_hbm, w1f), (w3_hbm, w3f), (w2_hbm, w2f)))]

    def gather(block, r, slot):
        src = pl.multiple_of(src_ref[block * MOE_BLOCK + r], 8)
        return pltpu.make_async_copy(xg_hbm.at[pl.ds(src, 8), :], xbuf.at[slot, pl.ds(r * 8, 8), :], sem_in.at[slot])

    def scatter(block, r, slot):
        dst = pl.multiple_of(dst_ref[(block + 1) * MOE_BLOCK + r], 8)
        return pltpu.make_async_copy(ybuf.at[slot, pl.ds(r * 8, 8), :], o_hbm.at[pl.ds(dst, 8), :], sem_out.at[slot])

    def wait_gathers(slot):
        pltpu.make_async_copy(xbuf.at[1 - slot], xbuf.at[slot], sem_in.at[slot]).wait()

    def wait_scatters(slot):
        pltpu.make_async_copy(ybuf.at[slot], ybuf.at[1 - slot], sem_out.at[slot]).wait()

    @pl.when(i == 0)
    def _():
        ybuf[1] = jnp.zeros(ybuf.shape[1:], ybuf.dtype)
        n_real = o_hbm.shape[0] - 2 * MOE_BLOCK * 8
        fill1 = pltpu.make_async_copy(ybuf.at[1], o_hbm.at[pl.ds(n_real + MOE_BLOCK * 8, MOE_BLOCK * 8), :],
                                      sem_out.at[1])
        fill1.start()
        fill1.wait()
        pltpu.make_async_copy(ybuf.at[1], o_hbm.at[pl.ds(n_real, MOE_BLOCK * 8), :], sem_out.at[0]).start()

        def first(r, carry):
            gather(0, r, 0).start()
            return carry

        lax.fori_loop(0, MOE_BLOCK, first, 0, unroll=8)

    @pl.when(i < n_used)
    def _():
        expert = be_ref[i]
        wslot = run_ref[i] % 2

        @pl.when(i == 0)
        def _():
            for cp in weight_copies(expert, 0):
                cp.start()

        @pl.when((i == 0) | (expert != be_ref[jnp.maximum(i - 1, 0)]))
        def _():
            for cp in weight_copies(expert, wslot):
                cp.wait()

            @pl.when(nxt_ref[i] != expert)
            def _():
                for cp in weight_copies(nxt_ref[i], 1 - wslot):
                    cp.start()

            w1s[...] = _bf(w1f[wslot])
            w3s[...] = _bf(w3f[wslot])
            w2s[...] = _bf(w2f[wslot])

        wait_gathers(par)

        def move_rows(group, n_groups=6):
            lo, hi = group * MOE_BLOCK // n_groups, (group + 1) * MOE_BLOCK // n_groups
            for r in range(lo, hi):
                gather(i + 1, r, 1 - par).start(priority=r % 2)
                scatter(i - 1, r, 1 - par).start(priority=(r + 1) % 2)

        for s in range(8):
            words = xbuf[par, pl.ds(s, MOE_BLOCK, stride=8), :]
            xs_ref[:, s * LANES:(s + 1) * LANES] = _bf(_unpack_lo(words))
            xs_ref[:, half + s * LANES:half + (s + 1) * LANES] = _bf(_unpack_hi(words))
        move_rows(0)
        xb = xs_ref[...]
        h1a = _dot(xb, w1s[:, :half_e])
        move_rows(1)
        h3a = _dot(xb, w3s[:, :half_e])
        move_rows(2)
        act_a = _bf(_silu(h1a) * h3a)
        h1b = _dot(xb, w1s[:, half_e:])
        move_rows(3)
        h3b = _dot(xb, w3s[:, half_e:])
        move_rows(4)
        act_b = _bf(_silu(h1b) * h3b)
        y = _dot(act_a, w2s[:half_e, :])
        move_rows(5)
        y = y + _dot(act_b, w2s[half_e:, :])

        wait_scatters(par)
        words = _pack_bf16_pairs(y)
        for s in range(8):
            ybuf[par, pl.ds(s, MOE_BLOCK, stride=8), :] = words[:, s * LANES:(s + 1) * LANES]

        @pl.when(i == n_used - 1)
        def _():
            def last(r, carry):
                scatter(i, r, par).start()
                return carry

            lax.fori_loop(0, MOE_BLOCK, last, 0, unroll=8)
            wait_scatters(par)
            wait_scatters(1 - par)
            wait_gathers(1 - par)


def _moe(x2d, gain, w_group, b_group, w_expert, b_expert, w1, w3, w2, layer):
    n, d = x2d.shape
    gates_l, idx_l, xg, cnt = _router(x2d, gain, w_group, b_group, w_expert, b_expert)

    a = n * TOP_K
    n_blocks = -(-a // MOE_BLOCK) + N_EXPERTS
    p = n_blocks * MOE_BLOCK
    counts = cnt[0, :N_EXPERTS]
    padded = (counts + MOE_BLOCK - 1) // MOE_BLOCK * MOE_BLOCK
    pad_end = jnp.cumsum(padded)
    pad_start = pad_end - padded
    e_hot = idx_l[:, :TOP_K, None] == jnp.arange(N_EXPERTS, dtype=jnp.int32)
    dest = jnp.sum(jnp.where(e_hot, pad_start, 0), axis=-1) + idx_l[:, TOP_K:2 * TOP_K]
    dest = jnp.clip(dest.reshape(-1), 0, p - 1).astype(jnp.int32)
    blk0 = jnp.arange(n_blocks, dtype=jnp.int32) * MOE_BLOCK
    block_expert = jnp.minimum(jnp.searchsorted(pad_end, blk0, side="right"), N_EXPERTS - 1).astype(jnp.int32)
    n_used = (pad_end[-1] // MOE_BLOCK).astype(jnp.int32).reshape(1)
    experts = jnp.arange(N_EXPERTS, dtype=jnp.int32)
    later = jnp.where((experts[None, :] > experts[:, None]) & (counts[None, :] > 0), experts[None, :], N_EXPERTS)
    next_owner = jnp.min(later, axis=1)
    next_expert = jnp.where(next_owner < N_EXPERTS, next_owner, experts)[block_expert].astype(jnp.int32)
    run_index = (jnp.cumsum(jnp.concatenate([jnp.ones((1,), jnp.int32),
                                             (block_expert[1:] != block_expert[:-1]).astype(jnp.int32)])) - 1
                 ).astype(jnp.int32)
    codes = _invert(dest, p)
    slot = jnp.arange(p, dtype=jnp.int32)
    spare = TOP_K * n + (slot // MOE_BLOCK % 2) * MOE_BLOCK + slot % MOE_BLOCK
    src_tok = (jnp.maximum(codes, 0) >> 1) * 8
    dst_row = jnp.where(codes >= 0, (codes & 1) * n + (codes >> 1), spare) * 8
    lead = (TOP_K * n + MOE_BLOCK + jnp.arange(MOE_BLOCK, dtype=jnp.int32)) * 8
    dst_row = jnp.concatenate([lead, dst_row])

    hbm = pl.BlockSpec(memory_space=pl.ANY)
    out_rows = TOP_K * n + 2 * MOE_BLOCK
    out2 = pl.pallas_call(
        functools.partial(_expert_kernel, layer=layer),
        grid_spec=pltpu.PrefetchScalarGridSpec(
            num_scalar_prefetch=6,
            grid=(n_blocks,),
            in_specs=[hbm, hbm, hbm, hbm],
            out_specs=hbm,
            scratch_shapes=[
                pltpu.VMEM((2, MOE_BLOCK * 8, LANES), jnp.uint32), pltpu.VMEM((2, MOE_BLOCK * 8, LANES), jnp.uint32),
                pltpu.VMEM((MOE_BLOCK, d), BF16),
                pltpu.VMEM((2, d, D_EXPERT), F32), pltpu.VMEM((2, d, D_EXPERT), F32), pltpu.VMEM((2, D_EXPERT, d), F32),
                pltpu.VMEM((d, D_EXPERT), BF16), pltpu.VMEM((d, D_EXPERT), BF16), pltpu.VMEM((D_EXPERT, d), BF16),
                pltpu.SemaphoreType.DMA((2,)), pltpu.SemaphoreType.DMA((2,)), pltpu.SemaphoreType.DMA((2, 3)),
            ],
        ),
        out_shape=jax.ShapeDtypeStruct((out_rows * 8, LANES), jnp.uint32),
        compiler_params=_params("arbitrary"),
        name="moe_experts",
    )(block_expert, next_expert, run_index, n_used, src_tok, dst_row, xg, w1, w3, w2)
    return _combine(x2d, gates_l, out2)


def _combine_kernel(x_ref, gate_ref, a_ref, b_ref, o_ref):
    tm = x_ref.shape[0]
    half = x_ref.shape[1] // 2
    g0 = gate_ref[:, 0:1]
    g1 = gate_ref[:, 1:2]
    for s in range(8):
        wa = a_ref[pl.ds(s, tm, stride=8), :]
        wb = b_ref[pl.ds(s, tm, stride=8), :]
        lo = slice(s * LANES, (s + 1) * LANES)
        hi = slice(half + s * LANES, half + (s + 1) * LANES)
        o_ref[:, lo] = x_ref[:, lo] + (g0 * _unpack_lo(wa) + g1 * _unpack_lo(wb))
        o_ref[:, hi] = x_ref[:, hi] + (g0 * _unpack_hi(wa) + g1 * _unpack_hi(wb))


def _combine(x2d, gates, out2):
    n, d = x2d.shape
    tm = _row_tile(n, 688)
    return pl.pallas_call(
        _combine_kernel,
        grid=(n // tm,),
        in_specs=[
            pl.BlockSpec((tm, d), lambda i: (i, 0)),
            pl.BlockSpec((tm, LANES), lambda i: (i, 0)),
            pl.BlockSpec((tm * 8, LANES), lambda i: (i, 0)),
            pl.BlockSpec((tm * 8, LANES), lambda i: (n // tm + i, 0)),
        ],
        out_specs=pl.BlockSpec((tm, d), lambda i: (i, 0)),
        out_shape=jax.ShapeDtypeStruct((n, d), F32),
        compiler_params=_params("parallel"),
        name="moe_combine",
    )(x2d, gates, out2, out2)


def _even_layer(x, norm_g, w_in, conv_w, conv_b, b_i, b_f, a_norm, w_gate2, b_gate, b_norm, w_out):
    b, t, d = x.shape
    n = b * t
    a_w = 2 * A_HEADS * A_DK + 2 * A_HEADS * A_DV
    g_w = 2 * A_HEADS
    b_w = 2 * B_HEADS * B_DK + 2 * B_HEADS * B_DV
    main = a_w + b_w
    gate_cols = g_w + GATE_RANK
    w = _even_weight(w_in, a_w, g_w, b_w, GATE_RANK)
    z = _normproj(x.reshape(n, d), norm_g, w).reshape(b, t, main + MXU_DIM)

    ya = _mlstm(z, main, conv_w, conv_b, b_i, b_f, a_norm)

    wg = jnp.zeros((B_HEADS, MXU_DIM, B_DK), F32).at[:, g_w:g_w + GATE_RANK, :].set(
        w_gate2.reshape(GATE_RANK, B_HEADS, B_DK).transpose(1, 0, 2))
    dummy = jnp.zeros((1, B_HEADS * B_DK), F32)
    hp = 2
    hspec = pl.BlockSpec((1, hp * B_DK), lambda bi, g: (0, g))
    yb = _gla_call(
        z, B_HEADS, hp, B_DK, B_DV,
        (a_w, a_w + B_HEADS * B_DK, a_w + 2 * B_HEADS * B_DK, a_w + 2 * B_HEADS * B_DK + B_HEADS * B_DV, main),
        MXU_DIM,
        ((wg, pl.BlockSpec((hp, MXU_DIM, B_DK), lambda bi, g: (g, 0, 0))),
         (b_gate.reshape(1, -1), hspec), (dummy, hspec)),
        b_norm, "gla")
    return _outproj(ya.reshape(n, -1), yb.reshape(n, -1), w_out.astype(BF16), x.reshape(n, d)).reshape(b, t, d)


def _odd_layer(x, lb, norm_g, w_in, c_norm, q_a_norm, w_q_up, kv_a_norm, w_kv_up, q_norm, k_norm, w_out):
    b, t, d = x.shape
    n = b * t
    c_w = 2 * C_HEADS * C_DK + 2 * C_HEADS * C_DV
    swap = (jnp.arange(D_ROPE) + D_ROPE // 2) % D_ROPE
    kr0 = c_w + Q_LORA + KV_LORA
    used = kr0 + 2 * D_ROPE
    total = -(-used // MXU_DIM) * MXU_DIM
    z2 = _normproj(x.reshape(n, d), norm_g, _odd_weight(w_in, kr0))
    z = z2.reshape(b, t, total)

    hp = 4
    hspec = pl.BlockSpec((1, hp * C_DK), lambda bi, g: (0, g))
    yc = _gla_call(
        z, C_HEADS, hp, C_DK, C_DV,
        (0, C_HEADS * C_DK, 2 * C_HEADS * C_DK, 2 * C_HEADS * C_DK + C_HEADS * C_DV, C_HEADS * C_DK),
        C_DK,
        ((jnp.log(lb).reshape(1, -1), hspec), (jnp.log1p(-lb).reshape(1, -1), hspec), ((1.0 - lb).reshape(1, -1), hspec)),
        c_norm, "hgrn")

    dq = D_NOPE + D_ROPE
    wq = w_q_up.reshape(Q_LORA, D_HEADS, dq)
    wq_rope = wq[:, :, D_NOPE:]
    wq_p = jnp.concatenate([wq[:, :, :D_NOPE].reshape(Q_LORA, -1),
                            jnp.concatenate([wq_rope, wq_rope[:, :, swap]], axis=-1).reshape(Q_LORA, -1)],
                           axis=1).astype(BF16)
    wkv = w_kv_up.reshape(KV_LORA, D_HEADS, D_NOPE + D_V)
    wkv_p = jnp.concatenate([wkv[:, :, :D_NOPE].reshape(KV_LORA, -1), wkv[:, :, D_NOPE:].reshape(KV_LORA, -1)],
                            axis=1).astype(BF16)
    qn = _normproj(z2, q_a_norm, wq_p, x_col_block=c_w // Q_LORA).reshape(b, t, -1)
    kvn = _normproj(z2, kv_a_norm, wkv_p, x_col_block=(c_w + Q_LORA) // KV_LORA).reshape(b, t, -1)

    pos = jnp.arange(t, dtype=F32)
    half = D_ROPE // 2
    inv = ROPE_THETA ** (-jnp.arange(half, dtype=F32) / half)
    ang = pos[:, None] * inv[None, :]
    cos, sin = jnp.cos(ang), jnp.sin(ang)
    tab = jnp.concatenate([cos, cos, -sin, sin], axis=1)
    pair = lambda g: jnp.concatenate([g[D_NOPE:], g[D_NOPE:][swap]]).reshape(1, LANES)
    yd = _mla(qn, kvn, z, kr0, tab, q_norm[:D_NOPE].reshape(1, LANES), pair(q_norm),
              k_norm[:D_NOPE].reshape(1, LANES), pair(k_norm))
    return _outproj(yc.reshape(n, -1), yd.reshape(n, -1), w_out.astype(BF16), x.reshape(n, d)).reshape(b, t, d)


def kernel(x, meta_tokens, ab_norm, ab_w_in, a_conv_w, a_conv_b, a_b_i, a_b_f, a_head_norm, b_w_gate2, b_b_gate, b_head_norm, ab_w_out, cd_norm, cd_w_in, c_lower_bound, c_head_norm, d_q_a_norm, d_w_q_up, d_kv_a_norm, d_w_kv_up, d_q_norm, d_k_norm, cd_w_out, moe_norm, moe_w_group, moe_b_group, moe_w_expert, moe_b_expert, moe_w1, moe_w3, moe_w2):
    b = x.shape[0]
    depth = moe_norm.shape[0]
    h = jnp.concatenate([jnp.broadcast_to(meta_tokens.astype(x.dtype)[None], (b, N_META, D_MODEL)), x], axis=1)
    t = h.shape[1]
    lb_cum = jnp.cumsum(jax.nn.softmax(c_lower_bound.astype(F32), axis=0), axis=0)
    lower_bounds = lb_cum - lb_cum[0]
    for layer in range(depth):
        j = layer // 2
        if layer % 2 == 0:
            h = _even_layer(h, ab_norm[j], ab_w_in[j], a_conv_w[j], a_conv_b[j], a_b_i[j], a_b_f[j], a_head_norm[j],
                            b_w_gate2[j], b_b_gate[j], b_head_norm[j], ab_w_out[j])
        else:
            h = _odd_layer(h, lower_bounds[layer], cd_norm[j], cd_w_in[j], c_head_norm[j], d_q_a_norm[j],
                           d_w_q_up[j], d_kv_a_norm[j], d_w_kv_up[j], d_q_norm[j], d_k_norm[j], cd_w_out[j])
        h = _moe(h.reshape(b * t, D_MODEL), moe_norm[layer], moe_w_group[layer], moe_b_group[layer],
                 moe_w_expert[layer], moe_b_expert[layer], moe_w1, moe_w3, moe_w2, layer).reshape(b, t, D_MODEL)
    return h[:, N_META:]
```

```python
import functools
import math

import jax
import jax.numpy as jnp
from jax import lax
from jax.experimental import pallas as pl
from jax.experimental.pallas import tpu as pltpu

F32 = jnp.float32
BF16 = jnp.bfloat16
HIGHEST = lax.Precision.HIGHEST

D_MODEL = 2048
N_META = 16
CHUNK = 64
CONV_K = 4
EPS = 1e-6
A_HEADS, A_DK, A_DV = 4, 128, 256
B_HEADS, B_DK, B_DV = 4, 128, 256
GATE_RANK = 16
GATE_TAU = 16.0
C_HEADS, C_DK, C_DV = 8, 128, 128
D_HEADS, D_NOPE, D_ROPE, D_V = 8, 128, 64, 128
Q_LORA, KV_LORA = 512, 256
ROPE_THETA = 10000.0
N_GROUPS, EXPERTS_PER_GROUP = 4, 8
N_EXPERTS = N_GROUPS * EXPERTS_PER_GROUP
TOP_K = 2
D_EXPERT = 512

LANES = 128
MXU_DIM = 256
BF16_ROWS = 16
VMEM_LIMIT = 56 * 1024 * 1024
MOE_BLOCK = MXU_DIM
ATT_BLOCK = 256
ATT_PAD = ATT_BLOCK - N_META

_NT = (((1,), (1,)), ((), ()))
_TN = (((0,), (0,)), ((), ()))


def _dot(a, b, precision=None):
    return jnp.dot(a, b, preferred_element_type=F32, precision=precision)


def _dot_nt(a, b):
    return lax.dot_general(a, b, _NT, preferred_element_type=F32)


def _dot_tn(a, b):
    return lax.dot_general(a, b, _TN, preferred_element_type=F32)


def _bf(x):
    return x.astype(BF16)


def _split3(x):
    hi = _bf(x)
    rest = x - hi.astype(F32)
    mid = _bf(rest)
    return hi, mid, _bf(rest - mid.astype(F32))


def _log_sigmoid(x):
    return jnp.minimum(x, 0.0) - jnp.log1p(jnp.exp(-jnp.abs(x)))


def _sigmoid(x):
    return 1.0 / (1.0 + jnp.exp(-x))


def _silu(x):
    return x * _sigmoid(x)


def _row_tile(m, cap):
    best = None
    for t in range(BF16_ROWS, min(m, cap) + 1, BF16_ROWS):
        if m % t == 0:
            best = t
    assert best is not None, m
    return best


def _col_tile(n, cap):
    best = None
    for t in range(MXU_DIM, min(n, cap) + 1, MXU_DIM):
        if n % t == 0:
            best = t
    assert best is not None, n
    return best


def _params(*sem):
    return pltpu.CompilerParams(dimension_semantics=sem, vmem_limit_bytes=VMEM_LIMIT)


_RELAYOUT_ROWS = 256


def _even_weight_kernel(wa_ref, wb_ref, wc_ref, o_ref, *, n_plain, n_shift, shift, gate_cols):
    ob = pl.program_id(0)
    rows = o_ref.shape[0]
    chunks = rows // _RELAYOUT_ROWS

    @pl.when(ob < n_plain)
    def _():
        def body(c, carry):
            r = pl.ds(pl.multiple_of(c * _RELAYOUT_ROWS, _RELAYOUT_ROWS), _RELAYOUT_ROWS)
            o_ref[r, :] = _bf(wa_ref[r, :])
            return carry

        lax.fori_loop(0, chunks, body, 0)

    @pl.when((ob >= n_plain) & (ob < n_plain + n_shift))
    def _():
        def body(c, carry):
            r = pl.ds(pl.multiple_of(c * _RELAYOUT_ROWS, _RELAYOUT_ROWS), _RELAYOUT_ROWS)
            wide = jnp.concatenate([wa_ref[r, :], wb_ref[r, :]], axis=1)
            o_ref[r, :] = _bf(wide[:, shift:shift + MXU_DIM])
            return carry

        lax.fori_loop(0, chunks, body, 0)

    @pl.when(ob == n_plain + n_shift)
    def _():
        lane = lax.broadcasted_iota(jnp.int32, (_RELAYOUT_ROWS, LANES), 1)

        def body(c, carry):
            r = pl.ds(pl.multiple_of(c * _RELAYOUT_ROWS, _RELAYOUT_ROWS), _RELAYOUT_ROWS)
            first = jnp.where(lane < shift, wc_ref[r, :], jnp.where(lane < gate_cols, wb_ref[r, :], 0.0))
            o_ref[r, :] = _bf(jnp.concatenate([first, jnp.zeros_like(first)], axis=1))
            return carry

        lax.fori_loop(0, chunks, body, 0)


def _even_weight(w_in, a_w, g_w, b_w, rank):
    d = w_in.shape[0]
    assert a_w % MXU_DIM == 0 and b_w % MXU_DIM == 0 and g_w + rank <= LANES and d % _RELAYOUT_ROWS == 0
    n_plain, n_shift = a_w // MXU_DIM, b_w // MXU_DIM
    n_out = n_plain + n_shift + 1
    last = n_out - 1

    def b_index(ob):
        return (0, jnp.where(ob < n_plain, 0, jnp.where(ob < last, 2 * (ob + 1), (a_w + g_w + b_w) // LANES)))

    return pl.pallas_call(
        functools.partial(_even_weight_kernel, n_plain=n_plain, n_shift=n_shift, shift=g_w, gate_cols=g_w + rank),
        grid=(n_out,),
        in_specs=[
            pl.BlockSpec((d, MXU_DIM), lambda ob: (0, jnp.minimum(ob, last - 1))),
            pl.BlockSpec((d, LANES), b_index),
            pl.BlockSpec((d, LANES), lambda ob: (0, a_w // LANES)),
        ],
        out_specs=pl.BlockSpec((d, MXU_DIM), lambda ob: (0, ob)),
        out_shape=jax.ShapeDtypeStruct((d, n_out * MXU_DIM), BF16),
        compiler_params=_params("parallel"),
        name="even_weight",
    )(w_in, w_in, w_in)


def _odd_weight_kernel(w_ref, o_ref, *, n_plain):
    ob = pl.program_id(0)
    chunks = o_ref.shape[0] // _RELAYOUT_ROWS
    half = D_ROPE // 2

    def body(c, carry):
        r = pl.ds(pl.multiple_of(c * _RELAYOUT_ROWS, _RELAYOUT_ROWS), _RELAYOUT_ROWS)
        w = w_ref[r, :]

        @pl.when(ob < n_plain)
        def _():
            o_ref[r, :] = _bf(w)

        @pl.when(ob == n_plain)
        def _():
            pair = jnp.concatenate([w[:, :D_ROPE], w[:, half:D_ROPE], w[:, :half]], axis=1)
            o_ref[r, :] = _bf(jnp.concatenate([pair, jnp.zeros_like(pair)], axis=1))

        return carry

    lax.fori_loop(0, chunks, body, 0)


def _odd_weight(w_in, kr0):
    d = w_in.shape[0]
    assert kr0 % MXU_DIM == 0 and w_in.shape[1] == kr0 + D_ROPE and d % _RELAYOUT_ROWS == 0
    n_plain = kr0 // MXU_DIM
    return pl.pallas_call(
        functools.partial(_odd_weight_kernel, n_plain=n_plain),
        grid=(n_plain + 1,),
        in_specs=[pl.BlockSpec((d, MXU_DIM), lambda ob: (0, ob))],
        out_specs=pl.BlockSpec((d, MXU_DIM), lambda ob: (0, ob)),
        out_shape=jax.ShapeDtypeStruct((d, kr0 + MXU_DIM), BF16),
        compiler_params=_params("parallel"),
        name="odd_weight",
    )(w_in)


def _normproj_kernel(x_ref, g_ref, w_ref, o_ref, xs_ref):
    tm = xs_ref.shape[0]

    @pl.when(pl.program_id(1) == 0)
    def _():
        def body(c, carry):
            r0 = pl.multiple_of(c * BF16_ROWS, BF16_ROWS)
            x = x_ref[pl.ds(r0, BF16_ROWS), :]
            ms = jnp.mean(x * x, axis=-1, keepdims=True)
            xs_ref[pl.ds(r0, BF16_ROWS), :] = _bf(x * lax.rsqrt(ms + EPS) * g_ref[...])
            return carry

        lax.fori_loop(0, tm // BF16_ROWS, body, 0, unroll=8)

    o_ref[...] = _dot(xs_ref[...], w_ref[...]).astype(o_ref.dtype)


def _normproj(x2d, gain, w, *, x_col_block=0, out_dtype=F32):
    m = x2d.shape[0]
    k, n = w.shape
    tm = _row_tile(m, 688)
    tn = _col_tile(n, 1280)
    return pl.pallas_call(
        _normproj_kernel,
        grid=(m // tm, n // tn),
        in_specs=[
            pl.BlockSpec((tm, k), lambda i, j: (i, x_col_block)),
            pl.BlockSpec((1, k), lambda i, j: (0, 0)),
            pl.BlockSpec((k, tn), lambda i, j: (0, j)),
        ],
        out_specs=pl.BlockSpec((tm, tn), lambda i, j: (i, j)),
        out_shape=jax.ShapeDtypeStruct((m, n), out_dtype),
        scratch_shapes=[pltpu.VMEM((tm, k), BF16)],
        compiler_params=_params("parallel", "arbitrary"),
        name="normproj",
    )(x2d, gain.reshape(1, k).astype(F32), w)


def _outproj_kernel(ya_ref, yb_ref, w_ref, r_ref, o_ref):
    ka = ya_ref.shape[1]
    acc = _dot(ya_ref[...], w_ref[:ka, :]) + _dot(yb_ref[...], w_ref[ka:, :])
    o_ref[...] = r_ref[...] + acc


def _outproj(ya, yb, w, res):
    m, ka = ya.shape
    kb = yb.shape[1]
    n = w.shape[1]
    tm = _row_tile(m, 1376)
    tn = _col_tile(n, 1024)
    return pl.pallas_call(
        _outproj_kernel,
        grid=(m // tm, n // tn),
        in_specs=[
            pl.BlockSpec((tm, ka), lambda i, j: (i, 0)),
            pl.BlockSpec((tm, kb), lambda i, j: (i, 0)),
            pl.BlockSpec((ka + kb, tn), lambda i, j: (0, j)),
            pl.BlockSpec((tm, tn), lambda i, j: (i, j)),
        ],
        out_specs=pl.BlockSpec((tm, tn), lambda i, j: (i, j)),
        out_shape=jax.ShapeDtypeStruct((m, n), F32),
        compiler_params=_params("parallel", "arbitrary"),
        name="outproj",
    )(ya, yb, w, res)


def _mlstm_kernel(bi_ref, bf_ref, q_ref, k_ref, v_ref, og_ref, gt_ref,
                  cwq_ref, cwk_ref, cbq_ref, cbk_ref, hn_ref, o_ref, c_ref, n_ref, m_ref, *, heads, group):
    t_total = q_ref.shape[1]
    n_chunks = (t_total - N_META) // CHUNK
    head0 = pl.program_id(1) * heads

    c_ref[...] = jnp.zeros_like(c_ref)
    n_ref[...] = jnp.zeros_like(n_ref)
    m_ref[...] = jnp.zeros_like(m_ref)

    def conv(win, cw, cb, length):
        y = cb
        for j in range(CONV_K):
            y = y + win[8 - (CONV_K - 1) + j:8 - (CONV_K - 1) + j + length, :] * cw[j:j + 1, :]
        return _silu(y)

    hs = range(heads)
    kcol = [slice(hh * A_DK, (hh + 1) * A_DK) for hh in hs]
    vcol = [slice(hh * A_DV, (hh + 1) * A_DV) for hh in hs]

    def gates(hh, blk, blk_parts, length, causal, upper):
        head = head0 + hh
        b_i = bi_ref[head]
        b_f = bf_ref[head]
        lane = lax.broadcasted_iota(jnp.int32, (length, LANES), 1)
        ig_c = jnp.sum(jnp.where(lane == head, blk, 0.0), axis=1, keepdims=True) + b_i
        lf_c = _log_sigmoid(jnp.sum(jnp.where(lane == A_HEADS + head, blk, 0.0), axis=1, keepdims=True) + b_f)
        sel_r = lax.broadcasted_iota(jnp.int32, (8, LANES), 0)
        sel_l = lax.broadcasted_iota(jnp.int32, (8, LANES), 1)
        sel = jnp.where(sel_l == head + A_HEADS * sel_r, 1.0, 0.0).astype(BF16)
        rows = _dot_nt(sel, blk_parts[0]) + (_dot_nt(sel, blk_parts[1]) + _dot_nt(sel, blk_parts[2]))
        ig_r = rows[0:1, :] + b_i
        lf_r = _log_sigmoid(rows[1:2, :] + b_f)
        b_c = jnp.sum(jnp.where(causal, lf_r, 0.0), axis=1, keepdims=True)
        b_r = jnp.sum(jnp.where(upper, lf_c, 0.0), axis=0, keepdims=True)
        b_end = b_c[length - 1:length, :]
        w_end = b_end - b_c + ig_c
        m_loc = jnp.max(w_end, axis=0, keepdims=True)
        d = jnp.where(causal, b_c - b_r + ig_r, -jnp.inf)
        return b_c, b_end, jnp.exp(w_end - m_loc), m_loc, d, jnp.max(d, axis=1, keepdims=True)

    def local_stage(o, length, qwin, kwin):
        row = lax.broadcasted_iota(jnp.int32, (length, length), 0)
        col = lax.broadcasted_iota(jnp.int32, (length, length), 1)
        causal = col <= row
        blk = gt_ref[0, pl.ds(o, length), 0:LANES]
        blk_parts = _split3(blk)
        gt = [gates(hh, blk, blk_parts, length, causal, row <= col) for hh in hs]
        q = [conv(qwin[hh], cwq_ref[:, kcol[hh]], cbq_ref[:, kcol[hh]], length) for hh in hs]
        k = [conv(kwin[hh], cwk_ref[:, kcol[hh]], cbk_ref[:, kcol[hh]], length) * (A_DK ** -0.5) for hh in hs]
        vb = [_bf(v_ref[0, pl.ds(o, length), vcol[hh]]) for hh in hs]
        qb = [_bf(q[hh]) for hh in hs]
        k_w = [k[hh] * gt[hh][2] for hh in hs]
        qk = [_dot_nt(qb[hh], _bf(k[hh])) for hh in hs]
        c_loc = [_dot_tn(_bf(k_w[hh]), vb[hh]) for hh in hs]
        n_loc = [jnp.sum(k_w[hh], axis=0, keepdims=True) for hh in hs]
        return gt, q, qb, vb, qk, c_loc, n_loc

    def state_stage(o, length, staged, state):
        gt, q, qb, vb, qk, c_loc, n_loc = staged
        c_in, n_in, m_in = state
        q_c = [_dot(qb[hh], _bf(c_in[hh])) for hh in hs]
        s, a_t, m_t, q_n, c_out, n_out, m_out = [], [], [], [], [], [], []
        for hh in hs:
            b_c, b_end, _, m_loc, d, d_max = gt[hh]
            inter = b_c + m_in[hh]
            m_t.append(jnp.maximum(inter, d_max))
            s.append(qk[hh] * jnp.exp(d - m_t[hh]))
            a_t.append(jnp.exp(inter - m_t[hh]))
            m_new = jnp.maximum(b_end + m_in[hh], m_loc)
            a = jnp.exp(b_end + m_in[hh] - m_new)
            c = jnp.exp(m_loc - m_new)
            c_out.append(a * c_in[hh] + c * c_loc[hh])
            n_out.append(a * n_in[hh] + c * n_loc[hh])
            m_out.append(m_new)
            q_n.append(jnp.sum(q[hh] * n_in[hh], axis=1, keepdims=True))
        num = [_dot(_bf(s[hh]), vb[hh]) + a_t[hh] * q_c[hh] for hh in hs]
        for hh in hs:
            den = jnp.sum(s[hh], axis=1, keepdims=True) + a_t[hh] * q_n[hh]
            h = num[hh] / jnp.maximum(jnp.abs(den), jnp.exp(-m_t[hh]))
            hn = h * lax.rsqrt(jnp.mean(h * h, axis=-1, keepdims=True) + EPS) * hn_ref[:, vcol[hh]]
            y = _sigmoid(og_ref[0, pl.ds(o, length), vcol[hh]]) * hn
            o_ref[0, pl.ds(o, length), vcol[hh]] = y.astype(o_ref.dtype)
        return c_out, n_out, m_out

    def sweep(offsets, length, qwins, kwins):
        staged = [local_stage(o, length, qw, kw) for o, qw, kw in zip(offsets, qwins, kwins)]
        state = ([c_ref[hh] for hh in hs], [n_ref[hh] for hh in hs], [m_ref[hh] for hh in hs])
        for o, stg in zip(offsets, staged):
            state = state_stage(o, length, stg, state)
        for hh in hs:
            c_ref[hh] = state[0][hh]
            n_ref[hh] = state[1][hh]
            m_ref[hh] = state[2][hh]

    zeros8 = jnp.zeros((8, A_DK), F32)
    sweep([0], N_META,
          [[jnp.concatenate([zeros8, q_ref[0, 0:N_META, kcol[hh]]], axis=0) for hh in hs]],
          [[jnp.concatenate([zeros8, k_ref[0, 0:N_META, kcol[hh]]], axis=0) for hh in hs]])
    assert n_chunks % group == 0

    def body(c, carry):
        offsets = [pl.multiple_of(N_META + (c * group + j) * CHUNK, BF16_ROWS) for j in range(group)]
        starts = [pl.multiple_of(N_META - 8 + (c * group + j) * CHUNK, 8) for j in range(group)]
        sweep(offsets, CHUNK,
              [[q_ref[0, pl.ds(w0, CHUNK + 8), kcol[hh]] for hh in hs] for w0 in starts],
              [[k_ref[0, pl.ds(w0, CHUNK + 8), kcol[hh]] for hh in hs] for w0 in starts])
        return carry

    lax.fori_loop(0, n_chunks // group, body, 0)


def _mlstm(z, gate_col, conv_w, conv_b, b_i, b_f, head_norm, heads=2):
    b, t, _ = z.shape
    hk = A_HEADS * A_DK
    wk, wv = heads * A_DK, heads * A_DV
    smem = pl.BlockSpec(memory_space=pltpu.SMEM)
    col = lambda width, off: (lambda bi, g: (bi, 0, off // width + g))
    return pl.pallas_call(
        functools.partial(_mlstm_kernel, heads=heads, group=4),
        grid=(b, A_HEADS // heads),
        in_specs=[
            smem, smem,
            pl.BlockSpec((1, t, wk), col(wk, 0)),
            pl.BlockSpec((1, t, wk), col(wk, hk)),
            pl.BlockSpec((1, t, wv), col(wv, 2 * hk)),
            pl.BlockSpec((1, t, wv), col(wv, 2 * hk + A_HEADS * A_DV)),
            pl.BlockSpec((1, t, MXU_DIM), lambda bi, g: (bi, 0, gate_col // MXU_DIM)),
            pl.BlockSpec((CONV_K, wk), lambda bi, g: (0, g)),
            pl.BlockSpec((CONV_K, wk), lambda bi, g: (0, A_HEADS // heads + g)),
            pl.BlockSpec((1, wk), lambda bi, g: (0, g)),
            pl.BlockSpec((1, wk), lambda bi, g: (0, A_HEADS // heads + g)),
            pl.BlockSpec((1, wv), lambda bi, g: (0, g)),
        ],
        out_specs=pl.BlockSpec((1, t, wv), lambda bi, g: (bi, 0, g)),
        out_shape=jax.ShapeDtypeStruct((b, t, A_HEADS * A_DV), BF16),
        scratch_shapes=[pltpu.VMEM((heads, A_DK, A_DV), F32), pltpu.VMEM((heads, 1, A_DK), F32),
                        pltpu.VMEM((heads, 1, 1), F32)],
        compiler_params=_params("parallel", "parallel"),
        name="mlstm",
    )(b_i, b_f, z, z, z, z, z, conv_w, conv_w, conv_b.reshape(1, -1), conv_b.reshape(1, -1),
      head_norm.reshape(1, -1))


def _gla_kernel(q_ref, k_ref, v_ref, og_ref, g_ref, p0_ref, p1_ref, p2_ref, hn_ref, o_ref, st_ref, *pre_ref,
                mode, heads, group):
    t_total = q_ref.shape[1]
    n_chunks = (t_total - N_META) // CHUNK
    dv, dk = st_ref.shape[1:]
    st_ref[...] = jnp.zeros_like(st_ref)

    hs = range(heads)
    kcol = [slice(hh * dk, (hh + 1) * dk) for hh in hs]
    vcol = [slice(hh * dv, (hh + 1) * dv) for hh in hs]

    if mode == "gla":
        gate = g_ref[0]
        gate_hi = _bf(gate)
        gate_lo = _bf(gate - gate_hi.astype(F32))
        for hh in hs:
            w = p0_ref[hh]
            w_hi = _bf(w)
            w_lo = _bf(w - w_hi.astype(F32))
            both = _dot(gate_hi, jnp.concatenate([w_hi, w_lo], axis=1))
            pre_ref[0][hh] = (both[:, :dk] + both[:, dk:]) + _dot(gate_lo, w_hi) + p1_ref[:, kcol[hh]]

    def gate_inputs(hh, o, length):
        q = q_ref[0, pl.ds(o, length), kcol[hh]]
        if mode == "gla":
            pre = pre_ref[0][hh, pl.ds(o, length), :]
            return q * (dk ** -0.5), k_ref[0, pl.ds(o, length), kcol[hh]], _log_sigmoid(pre) / GATE_TAU
        fpre = g_ref[0, pl.ds(o, length), kcol[hh]]
        a = p0_ref[:, kcol[hh]]
        bb = p1_ref[:, kcol[hh]] + _log_sigmoid(fpre)
        lg = jnp.maximum(a, bb) + jnp.log1p(jnp.exp(-jnp.abs(a - bb)))
        return q, p2_ref[:, kcol[hh]] * _sigmoid(-fpre), lg

    def cumsum_time(tri, lg):
        parts = _dot(tri, jnp.concatenate(_split3(lg), axis=1))
        return parts[:, :dk] + (parts[:, dk:2 * dk] + parts[:, 2 * dk:])

    def local_stage(o, length):
        row = lax.broadcasted_iota(jnp.int32, (length, length), 0)
        col = lax.broadcasted_iota(jnp.int32, (length, length), 1)
        causal = col <= row
        tri = jnp.where(causal, 1.0, 0.0).astype(BF16)
        qkl = [gate_inputs(hh, o, length) for hh in hs]
        vb = [_bf(v_ref[0, pl.ds(o, length), vcol[hh]]) for hh in hs]
        g = [cumsum_time(tri, qkl[hh][2]) for hh in hs]
        g_end = [g[hh][length - 1:length, :] for hh in hs]
        g_mid = [g[hh][length // 2:length // 2 + 1, :] for hh in hs]
        s = [_dot_nt(_bf(qkl[hh][0] * jnp.exp(g[hh] - g_mid[hh])), _bf(qkl[hh][1] * jnp.exp(g_mid[hh] - g[hh])))
             for hh in hs]
        q_dec = [_bf(qkl[hh][0] * jnp.exp(g[hh])) for hh in hs]
        local = [_dot_tn(vb[hh], _bf(qkl[hh][1] * jnp.exp(g_end[hh] - g[hh]))) for hh in hs]
        intra = [_dot(_bf(jnp.where(causal, s[hh], 0.0)), vb[hh]) for hh in hs]
        return q_dec, [jnp.exp(ge) for ge in g_end], local, intra

    def state_stage(o, length, staged, st_in):
        q_dec, decay, local, intra = staged
        inter = [_dot_nt(q_dec[hh], _bf(st_in[hh])) for hh in hs]
        st_out = [st_in[hh] * decay[hh] + local[hh] for hh in hs]
        for hh in hs:
            out = intra[hh] + inter[hh]
            hn = out * lax.rsqrt(jnp.mean(out * out, axis=-1, keepdims=True) + EPS) * hn_ref[:, vcol[hh]]
            og = og_ref[0, pl.ds(o, length), vcol[hh]]
            gate = _silu(og) if mode == "gla" else _sigmoid(og)
            o_ref[0, pl.ds(o, length), vcol[hh]] = (gate * hn).astype(o_ref.dtype)
        return st_out

    def sweep(offsets, length):
        staged = [local_stage(o, length) for o in offsets]
        st = [st_ref[hh] for hh in hs]
        for o, stg in zip(offsets, staged):
            st = state_stage(o, length, stg, st)
        for hh in hs:
            st_ref[hh] = st[hh]

    sweep([0], N_META)
    assert n_chunks % group == 0

    def body(c, carry):
        o = pl.multiple_of(N_META + c * (group * CHUNK), BF16_ROWS)
        sweep([pl.multiple_of(o + j * CHUNK, BF16_ROWS) for j in range(group)], CHUNK)
        return carry

    lax.fori_loop(0, n_chunks // group, body, 0)


def _gla_call(z, n_heads, heads, dk, dv, blocks, gate_width, params, head_norm, mode):
    b, t, _ = z.shape
    q0, k0, v0, og0, g0 = blocks
    zspec = lambda width, off, grouped=True: pl.BlockSpec(
        (1, t, width), (lambda bi, g: (bi, 0, off // width + (g if grouped else 0))))
    (p0, s0), (p1, s1), (p2, s2) = params
    gate_spec = zspec(gate_width, g0, grouped=False) if mode == "gla" else zspec(heads * dk, g0)
    return pl.pallas_call(
        functools.partial(_gla_kernel, mode=mode, heads=heads, group=4),
        grid=(b, n_heads // heads),
        in_specs=[
            zspec(heads * dk, q0), zspec(heads * dk, k0), zspec(heads * dv, v0), zspec(heads * dv, og0),
            gate_spec, s0, s1, s2,
            pl.BlockSpec((1, heads * dv), lambda bi, g: (0, g)),
        ],
        out_specs=pl.BlockSpec((1, t, heads * dv), lambda bi, g: (bi, 0, g)),
        out_shape=jax.ShapeDtypeStruct((b, t, n_heads * dv), BF16),
        scratch_shapes=[pltpu.VMEM((heads, dv, dk), F32)] + (
            [pltpu.VMEM((heads, t, dk), F32)] if mode == "gla" else []),
        compiler_params=_params("parallel", "parallel"),
        name="gla_" + mode,
    )(z, z, z, z, z, p0, p1, p2, head_norm.reshape(1, -1))


def _mla_kernel(qn_ref, qr_ref, kn_ref, v_ref, kr_ref, tab_ref, gqn_ref, gqr_ref, gkn_ref, gkr_ref,
                o_ref, qf_ref, kf_ref, vf_ref):
    t_total = qn_ref.shape[1]
    n_blocks = (ATT_PAD + t_total) // ATT_BLOCK
    dqk = D_NOPE + D_ROPE
    scale = dqk ** -0.5
    rows = _row_tile(t_total, 768)

    qf_ref[0:ATT_PAD, :] = jnp.zeros((ATT_PAD, 2 * LANES), BF16)
    kf_ref[0:ATT_PAD, :] = jnp.zeros((ATT_PAD, 2 * LANES), BF16)
    vf_ref[0:ATT_PAD, :] = jnp.zeros((ATT_PAD, D_V), BF16)

    def rope_pair(x, gains, tab):
        p = x * gains * tab
        return p + pltpu.roll(p, D_ROPE, 1)

    def prep(c):
        r0 = c * rows
        dst = ATT_PAD + c * rows
        tab = tab_ref[pl.ds(r0, rows), :]
        qn = qn_ref[0, pl.ds(r0, rows), :]
        qr = qr_ref[0, pl.ds(r0, rows), :]
        ssq = jnp.sum(qn * qn + 0.5 * (qr * qr), axis=-1, keepdims=True)
        rq = lax.rsqrt(ssq / dqk + EPS) * scale
        qf_ref[pl.ds(dst, rows), 0:LANES] = _bf(qn * gqn_ref[...] * rq)
        qf_ref[pl.ds(dst, rows), LANES:2 * LANES] = _bf(qr * gqr_ref[...] * tab * rq)
        kn = kn_ref[0, pl.ds(r0, rows), :]
        kr = kr_ref[0, pl.ds(r0, rows), :]
        ssk = jnp.sum(kn * kn + 0.5 * (kr * kr), axis=-1, keepdims=True)
        rk = lax.rsqrt(ssk / dqk + EPS)
        kf_ref[pl.ds(dst, rows), 0:LANES] = _bf(kn * gkn_ref[...] * rk)
        kf_ref[pl.ds(dst, rows), LANES:2 * LANES] = _bf(rope_pair(kr, gkr_ref[...], tab) * rk)
        vf_ref[pl.ds(dst, rows), :] = _bf(v_ref[0, pl.ds(r0, rows), :])

    for c in range(t_total // rows):
        prep(c)

    qpos = lax.broadcasted_iota(jnp.int32, (ATT_BLOCK, ATT_BLOCK), 0)
    kpos = lax.broadcasted_iota(jnp.int32, (ATT_BLOCK, ATT_BLOCK), 1)
    neg = -jnp.inf

    def scores(qi):
        q = qf_ref[qi * ATT_BLOCK:(qi + 1) * ATT_BLOCK, :]
        return _dot_nt(q, kf_ref[0:(qi + 1) * ATT_BLOCK, :])

    s_next = scores(0)
    for qi in range(n_blocks):
        s = s_next
        if qi + 1 < n_blocks:
            s_next = scores(qi + 1)
        parts = [s[:, j * ATT_BLOCK:(j + 1) * ATT_BLOCK] for j in range(qi + 1)]
        parts[0] = jnp.where(kpos >= ATT_PAD, parts[0], neg)
        parts[qi] = jnp.where(kpos <= qpos, parts[qi], neg)
        top = functools.reduce(jnp.maximum, parts)
        m = jnp.max(top, axis=-1, keepdims=True)
        if qi == 0:
            m = jnp.where(m == neg, 0.0, m)
        probs = [jnp.exp(part - m) for part in parts]
        l = jnp.sum(functools.reduce(jnp.add, probs), axis=-1, keepdims=True)
        pv = _dot(jnp.concatenate([_bf(pr) for pr in probs], axis=1), vf_ref[0:(qi + 1) * ATT_BLOCK, :])
        if qi == 0:
            out = pv / jnp.where(l == 0.0, 1.0, l)
            o_ref[0, 0:N_META, :] = out[ATT_PAD:, :].astype(o_ref.dtype)
        else:
            dst = qi * ATT_BLOCK - ATT_PAD
            o_ref[0, dst:dst + ATT_BLOCK, :] = (pv / l).astype(o_ref.dtype)


def _mla(qn, kvn, z, kr_col, tab, gqn, gqr, gkn, gkr):
    b, t, _ = qn.shape
    assert (ATT_PAD + t) % ATT_BLOCK == 0 and t % (3 * BF16_ROWS) == 0
    tp = ATT_PAD + t
    hspec = lambda off: pl.BlockSpec((1, t, LANES), lambda bi, h: (bi, 0, off + h))
    gspec = pl.BlockSpec((1, LANES), lambda bi, h: (0, 0))
    return pl.pallas_call(
        _mla_kernel,
        grid=(b, D_HEADS),
        in_specs=[
            hspec(0), hspec(D_HEADS), hspec(0), hspec(D_HEADS),
            pl.BlockSpec((1, t, LANES), lambda bi, h: (bi, 0, kr_col // LANES)),
            pl.BlockSpec((t, LANES), lambda bi, h: (0, 0)),
            gspec, gspec, gspec, gspec,
        ],
        out_specs=pl.BlockSpec((1, t, D_V), lambda bi, h: (bi, 0, h)),
        out_shape=jax.ShapeDtypeStruct((b, t, D_HEADS * D_V), BF16),
        scratch_shapes=[pltpu.VMEM((tp, 2 * LANES), BF16), pltpu.VMEM((tp, 2 * LANES), BF16),
                        pltpu.VMEM((tp, D_V), BF16)],
        compiler_params=_params("parallel", "parallel"),
        name="mla",
    )(qn, qn, kvn, kvn, z, tab, gqn, gqr, gkn, gkr)


def _pack_bf16_pairs(v):
    w = v.shape[1] // 2
    bits = pltpu.bitcast(_bf(v).astype(F32), jnp.uint32)
    return (bits[:, :w] >> 16) | (bits[:, w:] & jnp.uint32(0xFFFF0000))


def _unpack_lo(words):
    return pltpu.bitcast(words << 16, F32)


def _unpack_hi(words):
    return pltpu.bitcast(words & jnp.uint32(0xFFFF0000), F32)


def _router_kernel(x_ref, g_ref, w_ref, b_ref, gate_ref, idx_ref, xg_ref, cnt_ref, carry_ref):
    tm = x_ref.shape[0]

    @pl.when(pl.program_id(0) == 0)
    def _():
        carry_ref[...] = jnp.zeros_like(carry_ref)

    x = x_ref[...]
    ms = jnp.mean(x * x, axis=-1, keepdims=True)
    xn = x * lax.rsqrt(ms + EPS) * g_ref[...]
    xh = _bf(xn)
    xl = _bf(xn - xh.astype(F32))
    w = w_ref[...]
    wh = _bf(w)
    wl = _bf(w - wh.astype(F32))
    both = _dot(xh, jnp.concatenate([wh, wl], axis=1))
    logits = (both[:, :LANES] + both[:, LANES:]) + _dot(xl, wh) + b_ref[...]
    lane = lax.broadcasted_iota(jnp.int32, logits.shape, 1)
    lane_f = lane.astype(F32)
    neg = -jnp.inf
    big = float(LANES)

    is_group = lane < N_GROUPS
    g_max = jnp.max(jnp.where(is_group, logits, neg), axis=-1, keepdims=True)
    g_sum = jnp.sum(jnp.where(is_group, jnp.exp(logits - g_max), 0.0), axis=-1, keepdims=True)
    p_top = 1.0 / g_sum
    grp = jnp.min(jnp.where(is_group & (logits == g_max), lane_f, big), axis=-1, keepdims=True)

    e_lo = N_GROUPS + grp * EXPERTS_PER_GROUP
    in_grp = (lane_f >= e_lo) & (lane_f < e_lo + EXPERTS_PER_GROUP)
    e_max = jnp.max(jnp.where(in_grp, logits, neg), axis=-1, keepdims=True)
    e_sum = jnp.sum(jnp.where(in_grp, jnp.exp(logits - e_max), 0.0), axis=-1, keepdims=True)
    i1 = jnp.min(jnp.where(in_grp & (logits == e_max), lane_f, big), axis=-1, keepdims=True)
    rest = in_grp & (lane_f != i1)
    e_2nd = jnp.max(jnp.where(rest, logits, neg), axis=-1, keepdims=True)
    i2 = jnp.min(jnp.where(rest & (logits == e_2nd), lane_f, big), axis=-1, keepdims=True)
    p1 = 1.0 / e_sum
    p2 = jnp.exp(e_2nd - e_max) / e_sum
    tot = p1 + p2
    gate_ref[...] = jnp.where(lane == 0, p_top * p1 / tot, jnp.where(lane == 1, p_top * p2 / tot, 0.0))

    e1 = i1 - N_GROUPS
    e2 = i2 - N_GROUPS
    hot = jnp.where((lane_f == e1) | (lane_f == e2), 1.0, 0.0)
    row = lax.broadcasted_iota(jnp.int32, (tm, tm), 0)
    col = lax.broadcasted_iota(jnp.int32, (tm, tm), 1)
    before = _dot(jnp.where(col < row, 1.0, 0.0).astype(BF16), _bf(hot)) + carry_ref[...]
    r1 = jnp.sum(jnp.where(lane_f == e1, before, 0.0), axis=-1, keepdims=True)
    r2 = jnp.sum(jnp.where(lane_f == e2, before, 0.0), axis=-1, keepdims=True)
    total = carry_ref[...] + jnp.sum(hot, axis=0, keepdims=True)
    carry_ref[...] = total
    cnt_ref[...] = jnp.broadcast_to(total, cnt_ref.shape).astype(jnp.int32)
    idx_ref[...] = jnp.where(lane == 0, e1, jnp.where(lane == 1, e2, jnp.where(lane == 2, r1, jnp.where(
        lane == 3, r2, 0.0)))).astype(jnp.int32)

    words = _pack_bf16_pairs(xn)
    for s in range(8):
        xg_ref[pl.ds(s, tm, stride=8), :] = words[:, s * LANES:(s + 1) * LANES]


def _router(x2d, gain, w_group, b_group, w_expert, b_expert):
    m, d = x2d.shape
    assert d == 2 * 8 * LANES
    tm = _row_tile(m, 688)
    pad = LANES - N_GROUPS - N_EXPERTS
    w = jnp.concatenate([w_group, w_expert, jnp.zeros((d, pad), F32)], axis=1)
    bias = jnp.concatenate([b_group, b_expert, jnp.zeros((pad,), F32)]).reshape(1, LANES)
    return pl.pallas_call(
        _router_kernel,
        grid=(m // tm,),
        in_specs=[
            pl.BlockSpec((tm, d), lambda i: (i, 0)),
            pl.BlockSpec((1, d), lambda i: (0, 0)),
            pl.BlockSpec((d, LANES), lambda i: (0, 0)),
            pl.BlockSpec((1, LANES), lambda i: (0, 0)),
        ],
        out_specs=[pl.BlockSpec((tm, LANES), lambda i: (i, 0)), pl.BlockSpec((tm, LANES), lambda i: (i, 0)),
                   pl.BlockSpec((tm * 8, LANES), lambda i: (i, 0)), pl.BlockSpec((8, LANES), lambda i: (0, 0))],
        out_shape=[jax.ShapeDtypeStruct((m, LANES), F32), jax.ShapeDtypeStruct((m, LANES), jnp.int32),
                   jax.ShapeDtypeStruct((m * 8, LANES), jnp.uint32), jax.ShapeDtypeStruct((8, LANES), jnp.int32)],
        scratch_shapes=[pltpu.VMEM((1, LANES), F32)],
        compiler_params=_params("arbitrary"),
        name="router",
    )(x2d, gain.reshape(1, d), w, bias)


def _invert_kernel(dest_ref, inv_ref):
    def clear(s, carry):
        inv_ref[s] = -1
        return carry

    lax.fori_loop(0, inv_ref.shape[0], clear, 0, unroll=8)

    def put(f, carry):
        inv_ref[dest_ref[f]] = f
        return carry

    lax.fori_loop(0, dest_ref.shape[0], put, 0, unroll=8)


def _invert(dest, p):
    assert p % 8 == 0 and dest.shape[0] % 8 == 0
    smem = pl.BlockSpec(memory_space=pltpu.SMEM)
    return pl.pallas_call(
        _invert_kernel, in_specs=[smem], out_specs=smem,
        out_shape=jax.ShapeDtypeStruct((p,), jnp.int32), name="moe_invert",
    )(dest)


def _expert_kernel(be_ref, nxt_ref, run_ref, nu_ref, src_ref, dst_ref, xg_hbm, w1_hbm, w3_hbm, w2_hbm, o_hbm,
                   xbuf, ybuf, xs_ref, w1f, w3f, w2f, w1s, w3s, w2s, sem_in, sem_out, sem_w, *, layer):
    i = pl.program_id(0)
    n_used = nu_ref[0]
    par = i % 2
    half = D_MODEL // 2
    half_e = D_EXPERT // 2

    def weight_copies(expert, slot):
        return [pltpu.make_async_copy(w_hbm.at[layer, expert], w_f.at[slot], sem_w.at[slot, j])
                for j, (w_hbm, w_f) in enumerate(((w1_hbm, w1f), (w3_hbm, w3f), (w2_hbm, w2f)))]

    def gather(block, r, slot):
        src = pl.multiple_of(src_ref[block * MOE_BLOCK + r], 8)
        return pltpu.make_async_copy(xg_hbm.at[pl.ds(src, 8), :], xbuf.at[slot, pl.ds(r * 8, 8), :], sem_in.at[slot])

    def scatter(block, r, slot):
        dst = pl.multiple_of(dst_ref[(block + 1) * MOE_BLOCK + r], 8)
        return pltpu.make_async_copy(ybuf.at[slot, pl.ds(r * 8, 8), :], o_hbm.at[pl.ds(dst, 8), :], sem_out.at[slot])

    def wait_gathers(slot):
        pltpu.make_async_copy(xbuf.at[1 - slot], xbuf.at[slot], sem_in.at[slot]).wait()

    def wait_scatters(slot):
        pltpu.make_async_copy(ybuf.at[slot], ybuf.at[1 - slot], sem_out.at[slot]).wait()

    @pl.when(i == 0)
    def _():
        ybuf[1] = jnp.zeros(ybuf.shape[1:], ybuf.dtype)
        n_real = o_hbm.shape[0] - 2 * MOE_BLOCK * 8
        fill1 = pltpu.make_async_copy(ybuf.at[1], o_hbm.at[pl.ds(n_real + MOE_BLOCK * 8, MOE_BLOCK * 8), :],
                                      sem_out.at[1])
        fill1.start()
        fill1.wait()
        pltpu.make_async_copy(ybuf.at[1], o_hbm.at[pl.ds(n_real, MOE_BLOCK * 8), :], sem_out.at[0]).start()

        def first(r, carry):
            gather(0, r, 0).start()
            return carry

        lax.fori_loop(0, MOE_BLOCK, first, 0, unroll=8)

    @pl.when(i < n_used)
    def _():
        expert = be_ref[i]
        wslot = run_ref[i] % 2

        @pl.when(i == 0)
        def _():
            for cp in weight_copies(expert, 0):
                cp.start()

        @pl.when((i == 0) | (expert != be_ref[jnp.maximum(i - 1, 0)]))
        def _():
            for cp in weight_copies(expert, wslot):
                cp.wait()

            @pl.when(nxt_ref[i] != expert)
            def _():
                for cp in weight_copies(nxt_ref[i], 1 - wslot):
                    cp.start()

            w1s[...] = _bf(w1f[wslot])
            w3s[...] = _bf(w3f[wslot])
            w2s[...] = _bf(w2f[wslot])

        wait_gathers(par)

        def move_rows(group, n_groups=6):
            lo, hi = group * MOE_BLOCK // n_groups, (group + 1) * MOE_BLOCK // n_groups
            for r in range(lo, hi):
                gather(i + 1, r, 1 - par).start(priority=r % 2)
                scatter(i - 1, r, 1 - par).start(priority=(r + 1) % 2)

        for s in range(8):
            words = xbuf[par, pl.ds(s, MOE_BLOCK, stride=8), :]
            xs_ref[:, s * LANES:(s + 1) * LANES] = _bf(_unpack_lo(words))
            xs_ref[:, half + s * LANES:half + (s + 1) * LANES] = _bf(_unpack_hi(words))
        move_rows(0)
        xb = xs_ref[...]
        h1a = _dot(xb, w1s[:, :half_e])
        move_rows(1)
        h3a = _dot(xb, w3s[:, :half_e])
        move_rows(2)
        act_a = _bf(_silu(h1a) * h3a)
        h1b = _dot(xb, w1s[:, half_e:])
        move_rows(3)
        h3b = _dot(xb, w3s[:, half_e:])
        move_rows(4)
        act_b = _bf(_silu(h1b) * h3b)
        y = _dot(act_a, w2s[:half_e, :])
        move_rows(5)
        y = y + _dot(act_b, w2s[half_e:, :])

        wait_scatters(par)
        words = _pack_bf16_pairs(y)
        for s in range(8):
            ybuf[par, pl.ds(s, MOE_BLOCK, stride=8), :] = words[:, s * LANES:(s + 1) * LANES]

        @pl.when(i == n_used - 1)
        def _():
            def last(r, carry):
                scatter(i, r, par).start()
                return carry

            lax.fori_loop(0, MOE_BLOCK, last, 0, unroll=8)
            wait_scatters(par)
            wait_scatters(1 - par)
            wait_gathers(1 - par)


def _moe(x2d, gain, w_group, b_group, w_expert, b_expert, w1, w3, w2, layer, final_shape=None):
    n, d = x2d.shape
    gates_l, idx_l, xg, cnt = _router(x2d, gain, w_group, b_group, w_expert, b_expert)

    a = n * TOP_K
    n_blocks = -(-a // MOE_BLOCK) + N_EXPERTS
    p = n_blocks * MOE_BLOCK
    counts = cnt[0, :N_EXPERTS]
    padded = (counts + MOE_BLOCK - 1) // MOE_BLOCK * MOE_BLOCK
    pad_end = jnp.cumsum(padded)
    pad_start = pad_end - padded
    e_hot = idx_l[:, :TOP_K, None] == jnp.arange(N_EXPERTS, dtype=jnp.int32)
    dest = jnp.sum(jnp.where(e_hot, pad_start, 0), axis=-1) + idx_l[:, TOP_K:2 * TOP_K]
    dest = jnp.clip(dest.reshape(-1), 0, p - 1).astype(jnp.int32)
    blk0 = jnp.arange(n_blocks, dtype=jnp.int32) * MOE_BLOCK
    block_expert = jnp.minimum(jnp.searchsorted(pad_end, blk0, side="right"), N_EXPERTS - 1).astype(jnp.int32)
    n_used = (pad_end[-1] // MOE_BLOCK).astype(jnp.int32).reshape(1)
    experts = jnp.arange(N_EXPERTS, dtype=jnp.int32)
    later = jnp.where((experts[None, :] > experts[:, None]) & (counts[None, :] > 0), experts[None, :], N_EXPERTS)
    next_owner = jnp.min(later, axis=1)
    next_expert = jnp.where(next_owner < N_EXPERTS, next_owner, experts)[block_expert].astype(jnp.int32)
    run_index = (jnp.cumsum(jnp.concatenate([jnp.ones((1,), jnp.int32),
                                             (block_expert[1:] != block_expert[:-1]).astype(jnp.int32)])) - 1
                 ).astype(jnp.int32)
    codes = _invert(dest, p)
    slot = jnp.arange(p, dtype=jnp.int32)
    spare = TOP_K * n + (slot // MOE_BLOCK % 2) * MOE_BLOCK + slot % MOE_BLOCK
    src_tok = (jnp.maximum(codes, 0) >> 1) * 8
    dst_row = jnp.where(codes >= 0, (codes & 1) * n + (codes >> 1), spare) * 8
    lead = (TOP_K * n + MOE_BLOCK + jnp.arange(MOE_BLOCK, dtype=jnp.int32)) * 8
    dst_row = jnp.concatenate([lead, dst_row])

    hbm = pl.BlockSpec(memory_space=pl.ANY)
    out_rows = TOP_K * n + 2 * MOE_BLOCK
    out2 = pl.pallas_call(
        functools.partial(_expert_kernel, layer=layer),
        grid_spec=pltpu.PrefetchScalarGridSpec(
            num_scalar_prefetch=6,
            grid=(n_blocks,),
            in_specs=[hbm, hbm, hbm, hbm],
            out_specs=hbm,
            scratch_shapes=[
                pltpu.VMEM((2, MOE_BLOCK * 8, LANES), jnp.uint32), pltpu.VMEM((2, MOE_BLOCK * 8, LANES), jnp.uint32),
                pltpu.VMEM((MOE_BLOCK, d), BF16),
                pltpu.VMEM((2, d, D_EXPERT), F32), pltpu.VMEM((2, d, D_EXPERT), F32), pltpu.VMEM((2, D_EXPERT, d), F32),
                pltpu.VMEM((d, D_EXPERT), BF16), pltpu.VMEM((d, D_EXPERT), BF16), pltpu.VMEM((D_EXPERT, d), BF16),
                pltpu.SemaphoreType.DMA((2,)), pltpu.SemaphoreType.DMA((2,)), pltpu.SemaphoreType.DMA((2, 3)),
            ],
        ),
        out_shape=jax.ShapeDtypeStruct((out_rows * 8, LANES), jnp.uint32),
        compiler_params=_params("arbitrary"),
        name="moe_experts",
    )(block_expert, next_expert, run_index, n_used, src_tok, dst_row, xg, w1, w3, w2)
    return _combine(x2d, gates_l, out2, final_shape)


def _combine_tile(x_ref, gate_ref, a_ref, b_ref, o_ref):
    tm = x_ref.shape[0]
    half = x_ref.shape[1] // 2
    g0 = gate_ref[:, 0:1]
    g1 = gate_ref[:, 1:2]
    for s in range(8):
        wa = a_ref[pl.ds(s, tm, stride=8), :]
        wb = b_ref[pl.ds(s, tm, stride=8), :]
        lo = slice(s * LANES, (s + 1) * LANES)
        hi = slice(half + s * LANES, half + (s + 1) * LANES)
        o_ref[:, lo] = x_ref[:, lo] + (g0 * _unpack_lo(wa) + g1 * _unpack_lo(wb))
        o_ref[:, hi] = x_ref[:, hi] + (g0 * _unpack_hi(wa) + g1 * _unpack_hi(wb))


def _combine_kernel(x_ref, gate_ref, a_ref, b_ref, o_ref):
    _combine_tile(x_ref, gate_ref, a_ref, b_ref, o_ref)


def _combine_final_kernel(x_ref, gate_ref, a_ref, b_ref, o_hbm, obuf, sem, *, tiles_per_batch):
    i = pl.program_id(0)
    n_steps = pl.num_programs(0)
    tm = x_ref.shape[0]

    def copies(step, slot):
        batch, j = step // tiles_per_batch, step % tiles_per_batch
        row = pl.multiple_of(j * tm, 8)
        main = pltpu.make_async_copy(obuf.at[slot, pl.ds(N_META, tm - N_META), :],
                                     o_hbm.at[batch, pl.ds(row, tm - N_META), :], sem.at[slot, 0])
        head = pltpu.make_async_copy(obuf.at[slot, pl.ds(0, N_META), :],
                                     o_hbm.at[batch, pl.ds(pl.multiple_of(jnp.maximum(row - N_META, 0), 8), N_META), :],
                                     sem.at[slot, 1])
        return main, head, j > 0

    def wait_step(step, slot):
        main, head, has_head = copies(step, slot)
        main.wait()

        @pl.when(has_head)
        def _():
            head.wait()

    slot = i % 2

    @pl.when(i >= 2)
    def _():
        wait_step(i - 2, slot)

    _combine_tile(x_ref, gate_ref, a_ref, b_ref, obuf.at[slot])
    main, head, has_head = copies(i, slot)
    main.start()

    @pl.when(has_head)
    def _():
        head.start()

    @pl.when(i == n_steps - 1)
    def _():
        wait_step(i, slot)

        @pl.when(i >= 1)
        def _():
            wait_step(i - 1, 1 - slot)


def _combine(x2d, gates, out2, final_shape=None):
    n, d = x2d.shape
    tm = _row_tile(n, 688)
    in_specs = [
        pl.BlockSpec((tm, d), lambda i: (i, 0)),
        pl.BlockSpec((tm, LANES), lambda i: (i, 0)),
        pl.BlockSpec((tm * 8, LANES), lambda i: (i, 0)),
        pl.BlockSpec((tm * 8, LANES), lambda i: (n // tm + i, 0)),
    ]
    if final_shape is None:
        return pl.pallas_call(
            _combine_kernel,
            grid=(n // tm,),
            in_specs=in_specs,
            out_specs=pl.BlockSpec((tm, d), lambda i: (i, 0)),
            out_shape=jax.ShapeDtypeStruct((n, d), F32),
            compiler_params=_params("parallel"),
            name="moe_combine",
        )(x2d, gates, out2, out2)
    b, t = final_shape
    assert t % tm == 0 and tm > N_META
    return pl.pallas_call(
        functools.partial(_combine_final_kernel, tiles_per_batch=t // tm),
        grid=(n // tm,),
        in_specs=in_specs,
        out_specs=pl.BlockSpec(memory_space=pl.ANY),
        out_shape=jax.ShapeDtypeStruct((b, t - N_META, d), F32),
        scratch_shapes=[pltpu.VMEM((2, tm, d), F32), pltpu.SemaphoreType.DMA((2, 2))],
        compiler_params=_params("arbitrary"),
        name="moe_combine_final",
    )(x2d, gates, out2, out2)


def _even_layer(x, norm_g, w_in, conv_w, conv_b, b_i, b_f, a_norm, w_gate2, b_gate, b_norm, w_out):
    b, t, d = x.shape
    n = b * t
    a_w = 2 * A_HEADS * A_DK + 2 * A_HEADS * A_DV
    g_w = 2 * A_HEADS
    b_w = 2 * B_HEADS * B_DK + 2 * B_HEADS * B_DV
    main = a_w + b_w
    gate_cols = g_w + GATE_RANK
    w = _even_weight(w_in, a_w, g_w, b_w, GATE_RANK)
    z = _normproj(x.reshape(n, d), norm_g, w).reshape(b, t, main + MXU_DIM)

    ya = _mlstm(z, main, conv_w, conv_b, b_i, b_f, a_norm)

    wg = jnp.zeros((B_HEADS, MXU_DIM, B_DK), F32).at[:, g_w:g_w + GATE_RANK, :].set(
        w_gate2.reshape(GATE_RANK, B_HEADS, B_DK).transpose(1, 0, 2))
    dummy = jnp.zeros((1, B_HEADS * B_DK), F32)
    hp = 2
    hspec = pl.BlockSpec((1, hp * B_DK), lambda bi, g: (0, g))
    yb = _gla_call(
        z, B_HEADS, hp, B_DK, B_DV,
        (a_w, a_w + B_HEADS * B_DK, a_w + 2 * B_HEADS * B_DK, a_w + 2 * B_HEADS * B_DK + B_HEADS * B_DV, main),
        MXU_DIM,
        ((wg, pl.BlockSpec((hp, MXU_DIM, B_DK), lambda bi, g: (g, 0, 0))),
         (b_gate.reshape(1, -1), hspec), (dummy, hspec)),
        b_norm, "gla")
    return _outproj(ya.reshape(n, -1), yb.reshape(n, -1), w_out.astype(BF16), x.reshape(n, d)).reshape(b, t, d)


def _odd_layer(x, lb, norm_g, w_in, c_norm, q_a_norm, w_q_up, kv_a_norm, w_kv_up, q_norm, k_norm, w_out):
    b, t, d = x.shape
    n = b * t
    c_w = 2 * C_HEADS * C_DK + 2 * C_HEADS * C_DV
    swap = (jnp.arange(D_ROPE) + D_ROPE // 2) % D_ROPE
    kr0 = c_w + Q_LORA + KV_LORA
    used = kr0 + 2 * D_ROPE
    total = -(-used // MXU_DIM) * MXU_DIM
    z2 = _normproj(x.reshape(n, d), norm_g, _odd_weight(w_in, kr0))
    z = z2.reshape(b, t, total)

    hp = 4
    hspec = pl.BlockSpec((1, hp * C_DK), lambda bi, g: (0, g))
    yc = _gla_call(
        z, C_HEADS, hp, C_DK, C_DV,
        (0, C_HEADS * C_DK, 2 * C_HEADS * C_DK, 2 * C_HEADS * C_DK + C_HEADS * C_DV, C_HEADS * C_DK),
        C_DK,
        ((jnp.log(lb).reshape(1, -1), hspec), (jnp.log1p(-lb).reshape(1, -1), hspec), ((1.0 - lb).reshape(1, -1), hspec)),
        c_norm, "hgrn")

    dq = D_NOPE + D_ROPE
    wq = w_q_up.reshape(Q_LORA, D_HEADS, dq)
    wq_rope = wq[:, :, D_NOPE:]
    wq_p = jnp.concatenate([wq[:, :, :D_NOPE].reshape(Q_LORA, -1),
                            jnp.concatenate([wq_rope, wq_rope[:, :, swap]], axis=-1).reshape(Q_LORA, -1)],
                           axis=1).astype(BF16)
    wkv = w_kv_up.reshape(KV_LORA, D_HEADS, D_NOPE + D_V)
    wkv_p = jnp.concatenate([wkv[:, :, :D_NOPE].reshape(KV_LORA, -1), wkv[:, :, D_NOPE:].reshape(KV_LORA, -1)],
                            axis=1).astype(BF16)
    qn = _normproj(z2, q_a_norm, wq_p, x_col_block=c_w // Q_LORA).reshape(b, t, -1)
    kvn = _normproj(z2, kv_a_norm, wkv_p, x_col_block=(c_w + Q_LORA) // KV_LORA).reshape(b, t, -1)

    pos = jnp.arange(t, dtype=F32)
    half = D_ROPE // 2
    inv = ROPE_THETA ** (-jnp.arange(half, dtype=F32) / half)
    ang = pos[:, None] * inv[None, :]
    cos, sin = jnp.cos(ang), jnp.sin(ang)
    tab = jnp.concatenate([cos, cos, -sin, sin], axis=1)
    pair = lambda g: jnp.concatenate([g[D_NOPE:], g[D_NOPE:][swap]]).reshape(1, LANES)
    yd = _mla(qn, kvn, z, kr0, tab, q_norm[:D_NOPE].reshape(1, LANES), pair(q_norm),
              k_norm[:D_NOPE].reshape(1, LANES), pair(k_norm))
    return _outproj(yc.reshape(n, -1), yd.reshape(n, -1), w_out.astype(BF16), x.reshape(n, d)).reshape(b, t, d)


def kernel(x, meta_tokens, ab_norm, ab_w_in, a_conv_w, a_conv_b, a_b_i, a_b_f, a_head_norm, b_w_gate2, b_b_gate, b_head_norm, ab_w_out, cd_norm, cd_w_in, c_lower_bound, c_head_norm, d_q_a_norm, d_w_q_up, d_kv_a_norm, d_w_kv_up, d_q_norm, d_k_norm, cd_w_out, moe_norm, moe_w_group, moe_b_group, moe_w_expert, moe_b_expert, moe_w1, moe_w3, moe_w2):
    b = x.shape[0]
    depth = moe_norm.shape[0]
    h = jnp.concatenate([jnp.broadcast_to(meta_tokens.astype(x.dtype)[None], (b, N_META, D_MODEL)), x], axis=1)
    t = h.shape[1]
    lb_cum = jnp.cumsum(jax.nn.softmax(c_lower_bound.astype(F32), axis=0), axis=0)
    lower_bounds = lb_cum - lb_cum[0]
    for layer in range(depth):
        j = layer // 2
        if layer % 2 == 0:
            h = _even_layer(h, ab_norm[j], ab_w_in[j], a_conv_w[j], a_conv_b[j], a_b_i[j], a_b_f[j], a_head_norm[j],
                            b_w_gate2[j], b_b_gate[j], b_head_norm[j], ab_w_out[j])
        else:
            h = _odd_layer(h, lower_bounds[layer], cd_norm[j], cd_w_in[j], c_head_norm[j], d_q_a_norm[j],
                           d_w_q_up[j], d_kv_a_norm[j], d_w_kv_up[j], d_q_norm[j], d_k_norm[j], cd_w_out[j])
        last = layer == depth - 1
        h = _moe(h.reshape(b * t, D_MODEL), moe_norm[layer], moe_w_group[layer], moe_b_group[layer],
                 moe_w_expert[layer], moe_b_expert[layer], moe_w1, moe_w3, moe_w2, layer,
                 final_shape=(b, t) if last else None)
        if not last:
            h = h.reshape(b, t, D_MODEL)
    return h
```

```python
import functools
import math

import jax
import jax.numpy as jnp
from jax import lax
from jax.experimental import pallas as pl
from jax.experimental.pallas import tpu as pltpu

F32 = jnp.float32
BF16 = jnp.bfloat16
HIGHEST = lax.Precision.HIGHEST

D_MODEL = 2048
N_META = 16
CHUNK = 64
CONV_K = 4
EPS = 1e-6
A_HEADS, A_DK, A_DV = 4, 128, 256
B_HEADS, B_DK, B_DV = 4, 128, 256
GATE_RANK = 16
GATE_TAU = 16.0
C_HEADS, C_DK, C_DV = 8, 128, 128
D_HEADS, D_NOPE, D_ROPE, D_V = 8, 128, 64, 128
Q_LORA, KV_LORA = 512, 256
ROPE_THETA = 10000.0
N_GROUPS, EXPERTS_PER_GROUP = 4, 8
N_EXPERTS = N_GROUPS * EXPERTS_PER_GROUP
TOP_K = 2
D_EXPERT = 512

LANES = 128
MXU_DIM = 256
BF16_ROWS = 16
VMEM_LIMIT = 56 * 1024 * 1024
MOE_BLOCK = MXU_DIM
ROW_TILE = 688
ROW_TILE_BF16 = 2 * ROW_TILE
COL_TILE = 1280
COL_TILE_OUT = 1024
PREP_ROWS = 768
ATT_BLOCK = 256
ATT_PAD = ATT_BLOCK - N_META

_NT = (((1,), (1,)), ((), ()))
_TN = (((0,), (0,)), ((), ()))


def _dot(a, b, precision=None):
    return jnp.dot(a, b, preferred_element_type=F32, precision=precision)


def _dot_nt(a, b):
    return lax.dot_general(a, b, _NT, preferred_element_type=F32)


def _dot_tn(a, b):
    return lax.dot_general(a, b, _TN, preferred_element_type=F32)


def _bf(x):
    return x.astype(BF16)


def _split3(x):
    hi = _bf(x)
    rest = x - hi.astype(F32)
    mid = _bf(rest)
    return hi, mid, _bf(rest - mid.astype(F32))


def _log_sigmoid(x):
    return jnp.minimum(x, 0.0) - jnp.log1p(jnp.exp(-jnp.abs(x)))


def _sigmoid(x):
    return 1.0 / (1.0 + jnp.exp(-x))


def _silu(x):
    return x * _sigmoid(x)


def _row_tile(m, cap):
    best = None
    for t in range(BF16_ROWS, min(m, cap) + 1, BF16_ROWS):
        if m % t == 0:
            best = t
    assert best is not None, m
    return best


def _col_tile(n, cap):
    best = None
    for t in range(MXU_DIM, min(n, cap) + 1, MXU_DIM):
        if n % t == 0:
            best = t
    assert best is not None, n
    return best


def _params(*sem):
    return pltpu.CompilerParams(dimension_semantics=sem, vmem_limit_bytes=VMEM_LIMIT)


_RELAYOUT_ROWS = 256


def _even_weight_kernel(wa_ref, wb_ref, wc_ref, o_ref, *, n_plain, n_shift, shift, gate_cols):
    ob = pl.program_id(0)
    rows = o_ref.shape[0]
    chunks = rows // _RELAYOUT_ROWS

    @pl.when(ob < n_plain)
    def _():
        def body(c, carry):
            r = pl.ds(pl.multiple_of(c * _RELAYOUT_ROWS, _RELAYOUT_ROWS), _RELAYOUT_ROWS)
            o_ref[r, :] = _bf(wa_ref[r, :])
            return carry

        lax.fori_loop(0, chunks, body, 0)

    @pl.when((ob >= n_plain) & (ob < n_plain + n_shift))
    def _():
        def body(c, carry):
            r = pl.ds(pl.multiple_of(c * _RELAYOUT_ROWS, _RELAYOUT_ROWS), _RELAYOUT_ROWS)
            wide = jnp.concatenate([wa_ref[r, :], wb_ref[r, :]], axis=1)
            o_ref[r, :] = _bf(wide[:, shift:shift + MXU_DIM])
            return carry

        lax.fori_loop(0, chunks, body, 0)

    @pl.when(ob == n_plain + n_shift)
    def _():
        lane = lax.broadcasted_iota(jnp.int32, (_RELAYOUT_ROWS, LANES), 1)

        def body(c, carry):
            r = pl.ds(pl.multiple_of(c * _RELAYOUT_ROWS, _RELAYOUT_ROWS), _RELAYOUT_ROWS)
            first = jnp.where(lane < shift, wc_ref[r, :], jnp.where(lane < gate_cols, wb_ref[r, :], 0.0))
            o_ref[r, :] = _bf(jnp.concatenate([first, jnp.zeros_like(first)], axis=1))
            return carry

        lax.fori_loop(0, chunks, body, 0)


def _even_weight(w_in, a_w, g_w, b_w, rank):
    d = w_in.shape[0]
    assert a_w % MXU_DIM == 0 and b_w % MXU_DIM == 0 and g_w + rank <= LANES and d % _RELAYOUT_ROWS == 0
    n_plain, n_shift = a_w // MXU_DIM, b_w // MXU_DIM
    n_out = n_plain + n_shift + 1
    last = n_out - 1

    def b_index(ob):
        return (0, jnp.where(ob < n_plain, 0, jnp.where(ob < last, 2 * (ob + 1), (a_w + g_w + b_w) // LANES)))

    return pl.pallas_call(
        functools.partial(_even_weight_kernel, n_plain=n_plain, n_shift=n_shift, shift=g_w, gate_cols=g_w + rank),
        grid=(n_out,),
        in_specs=[
            pl.BlockSpec((d, MXU_DIM), lambda ob: (0, jnp.minimum(ob, last - 1))),
            pl.BlockSpec((d, LANES), b_index),
            pl.BlockSpec((d, LANES), lambda ob: (0, a_w // LANES)),
        ],
        out_specs=pl.BlockSpec((d, MXU_DIM), lambda ob: (0, ob)),
        out_shape=jax.ShapeDtypeStruct((d, n_out * MXU_DIM), BF16),
        compiler_params=_params("parallel"),
        name="even_weight",
    )(w_in, w_in, w_in)


def _odd_weight_kernel(w_ref, o_ref, *, n_plain):
    ob = pl.program_id(0)
    chunks = o_ref.shape[0] // _RELAYOUT_ROWS
    half = D_ROPE // 2

    def body(c, carry):
        r = pl.ds(pl.multiple_of(c * _RELAYOUT_ROWS, _RELAYOUT_ROWS), _RELAYOUT_ROWS)
        w = w_ref[r, :]

        @pl.when(ob < n_plain)
        def _():
            o_ref[r, :] = _bf(w)

        @pl.when(ob == n_plain)
        def _():
            pair = jnp.concatenate([w[:, :D_ROPE], w[:, half:D_ROPE], w[:, :half]], axis=1)
            o_ref[r, :] = _bf(jnp.concatenate([pair, jnp.zeros_like(pair)], axis=1))

        return carry

    lax.fori_loop(0, chunks, body, 0)


def _odd_weight(w_in, kr0):
    d = w_in.shape[0]
    assert kr0 % MXU_DIM == 0 and w_in.shape[1] == kr0 + D_ROPE and d % _RELAYOUT_ROWS == 0
    n_plain = kr0 // MXU_DIM
    return pl.pallas_call(
        functools.partial(_odd_weight_kernel, n_plain=n_plain),
        grid=(n_plain + 1,),
        in_specs=[pl.BlockSpec((d, MXU_DIM), lambda ob: (0, ob))],
        out_specs=pl.BlockSpec((d, MXU_DIM), lambda ob: (0, ob)),
        out_shape=jax.ShapeDtypeStruct((d, kr0 + MXU_DIM), BF16),
        compiler_params=_params("parallel"),
        name="odd_weight",
    )(w_in)


def _normproj_kernel(x_ref, g_ref, w_ref, o_ref, xs_ref):
    tm = xs_ref.shape[0]

    @pl.when(pl.program_id(1) == 0)
    def _():
        def body(c, carry):
            r0 = pl.multiple_of(c * BF16_ROWS, BF16_ROWS)
            x = x_ref[pl.ds(r0, BF16_ROWS), :]
            ms = jnp.mean(x * x, axis=-1, keepdims=True)
            xs_ref[pl.ds(r0, BF16_ROWS), :] = _bf(x * lax.rsqrt(ms + EPS) * g_ref[...])
            return carry

        lax.fori_loop(0, tm // BF16_ROWS, body, 0, unroll=8)

    o_ref[...] = _dot(xs_ref[...], w_ref[...]).astype(o_ref.dtype)


def _normproj(x2d, gain, w, *, x_col_block=0, out_dtype=F32):
    m = x2d.shape[0]
    k, n = w.shape
    tm = _row_tile(m, ROW_TILE)
    tn = _col_tile(n, COL_TILE)
    return pl.pallas_call(
        _normproj_kernel,
        grid=(m // tm, n // tn),
        in_specs=[
            pl.BlockSpec((tm, k), lambda i, j: (i, x_col_block)),
            pl.BlockSpec((1, k), lambda i, j: (0, 0)),
            pl.BlockSpec((k, tn), lambda i, j: (0, j)),
        ],
        out_specs=pl.BlockSpec((tm, tn), lambda i, j: (i, j)),
        out_shape=jax.ShapeDtypeStruct((m, n), out_dtype),
        scratch_shapes=[pltpu.VMEM((tm, k), BF16)],
        compiler_params=_params("parallel", "arbitrary"),
        name="normproj",
    )(x2d, gain.reshape(1, k).astype(F32), w)


def _outproj_kernel(ya_ref, yb_ref, w_ref, r_ref, o_ref):
    ka = ya_ref.shape[1]
    acc = _dot(ya_ref[...], w_ref[:ka, :]) + _dot(yb_ref[...], w_ref[ka:, :])
    o_ref[...] = r_ref[...] + acc


def _outproj(ya, yb, w, res):
    m, ka = ya.shape
    kb = yb.shape[1]
    n = w.shape[1]
    tm = _row_tile(m, ROW_TILE_BF16)
    tn = _col_tile(n, COL_TILE_OUT)
    return pl.pallas_call(
        _outproj_kernel,
        grid=(m // tm, n // tn),
        in_specs=[
            pl.BlockSpec((tm, ka), lambda i, j: (i, 0)),
            pl.BlockSpec((tm, kb), lambda i, j: (i, 0)),
            pl.BlockSpec((ka + kb, tn), lambda i, j: (0, j)),
            pl.BlockSpec((tm, tn), lambda i, j: (i, j)),
        ],
        out_specs=pl.BlockSpec((tm, tn), lambda i, j: (i, j)),
        out_shape=jax.ShapeDtypeStruct((m, n), F32),
        compiler_params=_params("parallel", "arbitrary"),
        name="outproj",
    )(ya, yb, w, res)


def _mlstm_kernel(bi_ref, bf_ref, q_ref, k_ref, v_ref, og_ref, gt_ref,
                  cwq_ref, cwk_ref, cbq_ref, cbk_ref, hn_ref, o_ref, c_ref, n_ref, m_ref, *, heads, group):
    t_total = q_ref.shape[1]
    n_chunks = (t_total - N_META) // CHUNK
    head0 = pl.program_id(1) * heads

    c_ref[...] = jnp.zeros_like(c_ref)
    n_ref[...] = jnp.zeros_like(n_ref)
    m_ref[...] = jnp.zeros_like(m_ref)

    def conv(taps, cw, cb, length):
        y = cb
        for j in range(CONV_K):
            y = y + taps[j] * cw[j:j + 1, :]
        return _silu(y)

    def window_taps(win, length):
        return [win[8 - (CONV_K - 1) + j:8 - (CONV_K - 1) + j + length, :] for j in range(CONV_K)]

    hs = range(heads)
    kcol = [slice(hh * A_DK, (hh + 1) * A_DK) for hh in hs]
    vcol = [slice(hh * A_DV, (hh + 1) * A_DV) for hh in hs]

    def gates(hh, blk, blk_parts, length, causal, upper):
        head = head0 + hh
        b_i = bi_ref[head]
        b_f = bf_ref[head]
        lane = lax.broadcasted_iota(jnp.int32, (length, LANES), 1)
        ig_c = jnp.sum(jnp.where(lane == head, blk, 0.0), axis=1, keepdims=True) + b_i
        lf_c = _log_sigmoid(jnp.sum(jnp.where(lane == A_HEADS + head, blk, 0.0), axis=1, keepdims=True) + b_f)
        sel_r = lax.broadcasted_iota(jnp.int32, (8, LANES), 0)
        sel_l = lax.broadcasted_iota(jnp.int32, (8, LANES), 1)
        sel = jnp.where(sel_l == head + A_HEADS * sel_r, 1.0, 0.0).astype(BF16)
        rows = _dot_nt(sel, blk_parts[0]) + (_dot_nt(sel, blk_parts[1]) + _dot_nt(sel, blk_parts[2]))
        ig_r = rows[0:1, :] + b_i
        lf_r = _log_sigmoid(rows[1:2, :] + b_f)
        b_c = jnp.sum(jnp.where(causal, lf_r, 0.0), axis=1, keepdims=True)
        b_r = jnp.sum(jnp.where(upper, lf_c, 0.0), axis=0, keepdims=True)
        b_end = b_c[length - 1:length, :]
        w_end = b_end - b_c + ig_c
        m_loc = jnp.max(w_end, axis=0, keepdims=True)
        d = jnp.where(causal, b_c - b_r + ig_r, -jnp.inf)
        return b_c, b_end, jnp.exp(w_end - m_loc), m_loc, d, jnp.max(d, axis=1, keepdims=True)

    def local_stage(o, length, qwin, kwin):
        row = lax.broadcasted_iota(jnp.int32, (length, length), 0)
        col = lax.broadcasted_iota(jnp.int32, (length, length), 1)
        causal = col <= row
        blk = gt_ref[0, pl.ds(o, length), 0:LANES]
        blk_parts = _split3(blk)
        gt = [gates(hh, blk, blk_parts, length, causal, row <= col) for hh in hs]
        q = [conv(qwin[hh], cwq_ref[:, kcol[hh]], cbq_ref[:, kcol[hh]], length) for hh in hs]
        k = [conv(kwin[hh], cwk_ref[:, kcol[hh]], cbk_ref[:, kcol[hh]], length) * (A_DK ** -0.5) for hh in hs]
        vb = [_bf(v_ref[0, pl.ds(o, length), vcol[hh]]) for hh in hs]
        qb = [_bf(q[hh]) for hh in hs]
        k_w = [k[hh] * gt[hh][2] for hh in hs]
        qk = [_dot_nt(qb[hh], _bf(k[hh])) for hh in hs]
        c_loc = [_dot_tn(_bf(k_w[hh]), vb[hh]) for hh in hs]
        n_loc = [jnp.sum(k_w[hh], axis=0, keepdims=True) for hh in hs]
        return gt, q, qb, vb, qk, c_loc, n_loc

    def state_stage(o, length, staged, state):
        gt, q, qb, vb, qk, c_loc, n_loc = staged
        c_in, n_in, m_in = state
        q_c = [_dot(qb[hh], _bf(c_in[hh])) for hh in hs]
        s, a_t, m_t, q_n, c_out, n_out, m_out = [], [], [], [], [], [], []
        for hh in hs:
            b_c, b_end, _, m_loc, d, d_max = gt[hh]
            inter = b_c + m_in[hh]
            m_t.append(jnp.maximum(inter, d_max))
            s.append(qk[hh] * jnp.exp(d - m_t[hh]))
            a_t.append(jnp.exp(inter - m_t[hh]))
            m_new = jnp.maximum(b_end + m_in[hh], m_loc)
            a = jnp.exp(b_end + m_in[hh] - m_new)
            c = jnp.exp(m_loc - m_new)
            c_out.append(a * c_in[hh] + c * c_loc[hh])
            n_out.append(a * n_in[hh] + c * n_loc[hh])
            m_out.append(m_new)
            q_n.append(jnp.sum(q[hh] * n_in[hh], axis=1, keepdims=True))
        num = [_dot(_bf(s[hh]), vb[hh]) + a_t[hh] * q_c[hh] for hh in hs]
        for hh in hs:
            den = jnp.sum(s[hh], axis=1, keepdims=True) + a_t[hh] * q_n[hh]
            h = num[hh] / jnp.maximum(jnp.abs(den), jnp.exp(-m_t[hh]))
            hn = h * lax.rsqrt(jnp.mean(h * h, axis=-1, keepdims=True) + EPS) * hn_ref[:, vcol[hh]]
            y = _sigmoid(og_ref[0, pl.ds(o, length), vcol[hh]]) * hn
            o_ref[0, pl.ds(o, length), vcol[hh]] = y.astype(o_ref.dtype)
        return c_out, n_out, m_out

    def sweep(offsets, length, qwins, kwins):
        staged = [local_stage(o, length, qw, kw) for o, qw, kw in zip(offsets, qwins, kwins)]
        state = ([c_ref[hh] for hh in hs], [n_ref[hh] for hh in hs], [m_ref[hh] for hh in hs])
        for o, stg in zip(offsets, staged):
            state = state_stage(o, length, stg, state)
        for hh in hs:
            c_ref[hh] = state[0][hh]
            n_ref[hh] = state[1][hh]
            m_ref[hh] = state[2][hh]

    zeros8 = jnp.zeros((8, A_DK), F32)
    sweep([0], N_META,
          [[window_taps(jnp.concatenate([zeros8, q_ref[0, 0:N_META, kcol[hh]]], axis=0), N_META) for hh in hs]],
          [[window_taps(jnp.concatenate([zeros8, k_ref[0, 0:N_META, kcol[hh]]], axis=0), N_META) for hh in hs]])
    assert n_chunks % group == 0

    def body(c, carry):
        offsets = [pl.multiple_of(N_META + (c * group + j) * CHUNK, BF16_ROWS) for j in range(group)]

        starts = [pl.multiple_of(N_META - 8 + (c * group + j) * CHUNK, 8) for j in range(group)]

        def shifted(ref, w0, hh):
            return window_taps(ref[0, pl.ds(w0, CHUNK + 8), kcol[hh]], CHUNK)

        sweep(offsets, CHUNK,
              [[shifted(q_ref, w0, hh) for hh in hs] for w0 in starts],
              [[shifted(k_ref, w0, hh) for hh in hs] for w0 in starts])
        return carry

    lax.fori_loop(0, n_chunks // group, body, 0)


def _mlstm(z, gate_col, conv_w, conv_b, b_i, b_f, head_norm, heads=2):
    b, t, _ = z.shape
    hk = A_HEADS * A_DK
    wk, wv = heads * A_DK, heads * A_DV
    smem = pl.BlockSpec(memory_space=pltpu.SMEM)
    col = lambda width, off: (lambda bi, g: (bi, 0, off // width + g))
    return pl.pallas_call(
        functools.partial(_mlstm_kernel, heads=heads, group=4),
        grid=(b, A_HEADS // heads),
        in_specs=[
            smem, smem,
            pl.BlockSpec((1, t, wk), col(wk, 0)),
            pl.BlockSpec((1, t, wk), col(wk, hk)),
            pl.BlockSpec((1, t, wv), col(wv, 2 * hk)),
            pl.BlockSpec((1, t, wv), col(wv, 2 * hk + A_HEADS * A_DV)),
            pl.BlockSpec((1, t, MXU_DIM), lambda bi, g: (bi, 0, gate_col // MXU_DIM)),
            pl.BlockSpec((CONV_K, wk), lambda bi, g: (0, g)),
            pl.BlockSpec((CONV_K, wk), lambda bi, g: (0, A_HEADS // heads + g)),
            pl.BlockSpec((1, wk), lambda bi, g: (0, g)),
            pl.BlockSpec((1, wk), lambda bi, g: (0, A_HEADS // heads + g)),
            pl.BlockSpec((1, wv), lambda bi, g: (0, g)),
        ],
        out_specs=pl.BlockSpec((1, t, wv), lambda bi, g: (bi, 0, g)),
        out_shape=jax.ShapeDtypeStruct((b, t, A_HEADS * A_DV), BF16),
        scratch_shapes=[pltpu.VMEM((heads, A_DK, A_DV), F32), pltpu.VMEM((heads, 1, A_DK), F32),
                        pltpu.VMEM((heads, 1, 1), F32)],
        compiler_params=_params("parallel", "parallel"),
        name="mlstm",
    )(b_i, b_f, z, z, z, z, z, conv_w, conv_w, conv_b.reshape(1, -1), conv_b.reshape(1, -1),
      head_norm.reshape(1, -1))


def _gla_kernel(q_ref, k_ref, v_ref, og_ref, g_ref, p0_ref, p1_ref, p2_ref, hn_ref, o_ref, st_ref, *pre_ref,
                mode, heads, group):
    t_total = q_ref.shape[1]
    n_chunks = (t_total - N_META) // CHUNK
    dv, dk = st_ref.shape[1:]
    st_ref[...] = jnp.zeros_like(st_ref)

    hs = range(heads)
    kcol = [slice(hh * dk, (hh + 1) * dk) for hh in hs]
    vcol = [slice(hh * dv, (hh + 1) * dv) for hh in hs]

    if mode == "gla":
        gate = g_ref[0]
        gate_hi = _bf(gate)
        gate_lo = _bf(gate - gate_hi.astype(F32))
        for hh in hs:
            w = p0_ref[hh]
            w_hi = _bf(w)
            w_lo = _bf(w - w_hi.astype(F32))
            both = _dot(gate_hi, jnp.concatenate([w_hi, w_lo], axis=1))
            pre_ref[0][hh] = (both[:, :dk] + both[:, dk:]) + _dot(gate_lo, w_hi) + p1_ref[:, kcol[hh]]

    def gate_inputs(hh, o, length):
        q = q_ref[0, pl.ds(o, length), kcol[hh]]
        if mode == "gla":
            pre = pre_ref[0][hh, pl.ds(o, length), :]
            return q * (dk ** -0.5), k_ref[0, pl.ds(o, length), kcol[hh]], _log_sigmoid(pre) / GATE_TAU
        fpre = g_ref[0, pl.ds(o, length), kcol[hh]]
        a = p0_ref[:, kcol[hh]]
        bb = p1_ref[:, kcol[hh]] + _log_sigmoid(fpre)
        lg = jnp.maximum(a, bb) + jnp.log1p(jnp.exp(-jnp.abs(a - bb)))
        return q, p2_ref[:, kcol[hh]] * _sigmoid(-fpre), lg

    def cumsum_time(tri, lg):
        parts = _dot(tri, jnp.concatenate(_split3(lg), axis=1))
        return parts[:, :dk] + (parts[:, dk:2 * dk] + parts[:, 2 * dk:])

    def local_stage(o, length):
        row = lax.broadcasted_iota(jnp.int32, (length, length), 0)
        col = lax.broadcasted_iota(jnp.int32, (length, length), 1)
        causal = col <= row
        tri = jnp.where(causal, 1.0, 0.0).astype(BF16)
        qkl = [gate_inputs(hh, o, length) for hh in hs]
        vb = [_bf(v_ref[0, pl.ds(o, length), vcol[hh]]) for hh in hs]
        g = [cumsum_time(tri, qkl[hh][2]) for hh in hs]
        g_end = [g[hh][length - 1:length, :] for hh in hs]
        g_mid = [g[hh][length // 2:length // 2 + 1, :] for hh in hs]
        s = [_dot_nt(_bf(qkl[hh][0] * jnp.exp(g[hh] - g_mid[hh])), _bf(qkl[hh][1] * jnp.exp(g_mid[hh] - g[hh])))
             for hh in hs]
        q_dec = [_bf(qkl[hh][0] * jnp.exp(g[hh])) for hh in hs]
        local = [_dot_tn(vb[hh], _bf(qkl[hh][1] * jnp.exp(g_end[hh] - g[hh]))) for hh in hs]
        intra = [_dot(_bf(jnp.where(causal, s[hh], 0.0)), vb[hh]) for hh in hs]
        return q_dec, [jnp.exp(ge) for ge in g_end], local, intra

    def state_stage(o, length, staged, st_in):
        q_dec, decay, local, intra = staged
        inter = [_dot_nt(q_dec[hh], _bf(st_in[hh])) for hh in hs]
        st_out = [st_in[hh] * decay[hh] + local[hh] for hh in hs]
        for hh in hs:
            out = intra[hh] + inter[hh]
            hn = out * lax.rsqrt(jnp.mean(out * out, axis=-1, keepdims=True) + EPS) * hn_ref[:, vcol[hh]]
            og = og_ref[0, pl.ds(o, length), vcol[hh]]
            gate = _silu(og) if mode == "gla" else _sigmoid(og)
            o_ref[0, pl.ds(o, length), vcol[hh]] = (gate * hn).astype(o_ref.dtype)
        return st_out

    def sweep(offsets, length):
        staged = [local_stage(o, length) for o in offsets]
        st = [st_ref[hh] for hh in hs]
        for o, stg in zip(offsets, staged):
            st = state_stage(o, length, stg, st)
        for hh in hs:
            st_ref[hh] = st[hh]

    sweep([0], N_META)
    assert n_chunks % group == 0

    def body(c, carry):
        o = pl.multiple_of(N_META + c * (group * CHUNK), BF16_ROWS)
        sweep([pl.multiple_of(o + j * CHUNK, BF16_ROWS) for j in range(group)], CHUNK)
        return carry

    lax.fori_loop(0, n_chunks // group, body, 0)


def _gla_call(z, n_heads, heads, dk, dv, blocks, gate_width, params, head_norm, mode):
    b, t, _ = z.shape
    q0, k0, v0, og0, g0 = blocks
    zspec = lambda width, off, grouped=True: pl.BlockSpec(
        (1, t, width), (lambda bi, g: (bi, 0, off // width + (g if grouped else 0))))
    (p0, s0), (p1, s1), (p2, s2) = params
    gate_spec = zspec(gate_width, g0, grouped=False) if mode == "gla" else zspec(heads * dk, g0)
    return pl.pallas_call(
        functools.partial(_gla_kernel, mode=mode, heads=heads, group=8),
        grid=(b, n_heads // heads),
        in_specs=[
            zspec(heads * dk, q0), zspec(heads * dk, k0), zspec(heads * dv, v0), zspec(heads * dv, og0),
            gate_spec, s0, s1, s2,
            pl.BlockSpec((1, heads * dv), lambda bi, g: (0, g)),
        ],
        out_specs=pl.BlockSpec((1, t, heads * dv), lambda bi, g: (bi, 0, g)),
        out_shape=jax.ShapeDtypeStruct((b, t, n_heads * dv), BF16),
        scratch_shapes=[pltpu.VMEM((heads, dv, dk), F32)] + (
            [pltpu.VMEM((heads, t, dk), F32)] if mode == "gla" else []),
        compiler_params=_params("parallel", "parallel"),
        name="gla_" + mode,
    )(z, z, z, z, z, p0, p1, p2, head_norm.reshape(1, -1))


def _mla_kernel(qn_ref, qr_ref, kn_ref, v_ref, kr_ref, tab_ref, gqn_ref, gqr_ref, gkn_ref, gkr_ref,
                o_ref, qf_ref, kf_ref, vf_ref, *, heads):
    t_total = qn_ref.shape[1]
    n_blocks = (ATT_PAD + t_total) // ATT_BLOCK
    dqk = D_NOPE + D_ROPE
    scale = dqk ** -0.5
    rows = _row_tile(t_total, PREP_ROWS)

    hs = range(heads)
    col = [slice(hh * LANES, (hh + 1) * LANES) for hh in hs]
    for hh in hs:
        qf_ref[hh, 0:ATT_PAD, :] = jnp.zeros((ATT_PAD, 2 * LANES), BF16)
        kf_ref[hh, 0:ATT_PAD, :] = jnp.zeros((ATT_PAD, 2 * LANES), BF16)
        vf_ref[hh, 0:ATT_PAD, :] = jnp.zeros((ATT_PAD, D_V), BF16)

    def rope_pair(x, gains, tab):
        p = x * gains * tab
        return p + pltpu.roll(p, D_ROPE, 1)

    def prep(c, hh):
        r0 = c * rows
        dst = ATT_PAD + c * rows
        tab = tab_ref[pl.ds(r0, rows), :]
        qn = qn_ref[0, pl.ds(r0, rows), col[hh]]
        qr = qr_ref[0, pl.ds(r0, rows), col[hh]]
        ssq = jnp.sum(qn * qn + 0.5 * (qr * qr), axis=-1, keepdims=True)
        rq = lax.rsqrt(ssq / dqk + EPS) * scale
        qf_ref[hh, pl.ds(dst, rows), 0:LANES] = _bf(qn * gqn_ref[...] * rq)
        qf_ref[hh, pl.ds(dst, rows), LANES:2 * LANES] = _bf(qr * gqr_ref[...] * tab * rq)
        kn = kn_ref[0, pl.ds(r0, rows), col[hh]]
        kr = kr_ref[0, pl.ds(r0, rows), :]
        ssk = jnp.sum(kn * kn + 0.5 * (kr * kr), axis=-1, keepdims=True)
        rk = lax.rsqrt(ssk / dqk + EPS)
        kf_ref[hh, pl.ds(dst, rows), 0:LANES] = _bf(kn * gkn_ref[...] * rk)
        kf_ref[hh, pl.ds(dst, rows), LANES:2 * LANES] = _bf(rope_pair(kr, gkr_ref[...], tab) * rk)
        vf_ref[hh, pl.ds(dst, rows), :] = _bf(v_ref[0, pl.ds(r0, rows), col[hh]])

    for c in range(t_total // rows):
        for hh in hs:
            prep(c, hh)

    qpos = lax.broadcasted_iota(jnp.int32, (ATT_BLOCK, ATT_BLOCK), 0)
    kpos = lax.broadcasted_iota(jnp.int32, (ATT_BLOCK, ATT_BLOCK), 1)
    neg = -jnp.inf

    def scores(item):
        qi, hh = item
        q = qf_ref[hh, qi * ATT_BLOCK:(qi + 1) * ATT_BLOCK, :]
        return _dot_nt(q, kf_ref[hh, 0:(qi + 1) * ATT_BLOCK, :])

    items = [(qi, hh) for qi in range(n_blocks) for hh in hs]
    s_next = scores(items[0])
    for idx, (qi, hh) in enumerate(items):
        s = s_next
        if idx + 1 < len(items):
            s_next = scores(items[idx + 1])
        parts = [s[:, j * ATT_BLOCK:(j + 1) * ATT_BLOCK] for j in range(qi + 1)]
        parts[0] = jnp.where(kpos >= ATT_PAD, parts[0], neg)
        parts[qi] = jnp.where(kpos <= qpos, parts[qi], neg)
        top = functools.reduce(jnp.maximum, parts)
        m = jnp.max(top, axis=-1, keepdims=True)
        if qi == 0:
            m = jnp.where(m == neg, 0.0, m)
        probs = [jnp.exp(part - m) for part in parts]
        l = jnp.sum(functools.reduce(jnp.add, probs), axis=-1, keepdims=True)
        pv = _dot(jnp.concatenate([_bf(pr) for pr in probs], axis=1), vf_ref[hh, 0:(qi + 1) * ATT_BLOCK, :])
        if qi == 0:
            out = pv / jnp.where(l == 0.0, 1.0, l)
            o_ref[0, 0:N_META, col[hh]] = out[ATT_PAD:, :].astype(o_ref.dtype)
        else:
            dst = qi * ATT_BLOCK - ATT_PAD
            o_ref[0, dst:dst + ATT_BLOCK, col[hh]] = (pv / l).astype(o_ref.dtype)


def _mla(qn, kvn, z, kr_col, tab, gqn, gqr, gkn, gkr):
    b, t, _ = qn.shape
    assert (ATT_PAD + t) % ATT_BLOCK == 0 and t % BF16_ROWS == 0
    tp = ATT_PAD + t
    heads = 2
    groups = D_HEADS // heads
    hspec = lambda off: pl.BlockSpec((1, t, heads * LANES), lambda bi, g: (bi, 0, off + g))
    gspec = pl.BlockSpec((1, LANES), lambda bi, g: (0, 0))
    return pl.pallas_call(
        functools.partial(_mla_kernel, heads=heads),
        grid=(b, groups),
        in_specs=[
            hspec(0), hspec(groups), hspec(0), hspec(groups),
            pl.BlockSpec((1, t, LANES), lambda bi, g: (bi, 0, kr_col // LANES)),
            pl.BlockSpec((t, LANES), lambda bi, g: (0, 0)),
            gspec, gspec, gspec, gspec,
        ],
        out_specs=pl.BlockSpec((1, t, heads * D_V), lambda bi, g: (bi, 0, g)),
        out_shape=jax.ShapeDtypeStruct((b, t, D_HEADS * D_V), BF16),
        scratch_shapes=[pltpu.VMEM((heads, tp, 2 * LANES), BF16), pltpu.VMEM((heads, tp, 2 * LANES), BF16),
                        pltpu.VMEM((heads, tp, D_V), BF16)],
        compiler_params=_params("parallel", "parallel"),
        name="mla",
    )(qn, qn, kvn, kvn, z, tab, gqn, gqr, gkn, gkr)


def _pack_bf16_pairs(v):
    w = v.shape[1] // 2
    bits = pltpu.bitcast(_bf(v).astype(F32), jnp.uint32)
    return (bits[:, :w] >> 16) | (bits[:, w:] & jnp.uint32(0xFFFF0000))


def _unpack_lo(words):
    return pltpu.bitcast(words << 16, F32)


def _unpack_hi(words):
    return pltpu.bitcast(words & jnp.uint32(0xFFFF0000), F32)


def _router_kernel(x_ref, g_ref, w_ref, b_ref, gate_ref, idx_ref, xg_ref, cnt_ref, carry_ref):
    tm = x_ref.shape[0]

    @pl.when(pl.program_id(0) == 0)
    def _():
        carry_ref[...] = jnp.zeros_like(carry_ref)

    x = x_ref[...]
    ms = jnp.mean(x * x, axis=-1, keepdims=True)
    xn = x * lax.rsqrt(ms + EPS) * g_ref[...]
    xh = _bf(xn)
    xl = _bf(xn - xh.astype(F32))
    w = w_ref[...]
    wh = _bf(w)
    wl = _bf(w - wh.astype(F32))
    both = _dot(xh, jnp.concatenate([wh, wl], axis=1))
    logits = (both[:, :LANES] + both[:, LANES:]) + _dot(xl, wh) + b_ref[...]
    lane = lax.broadcasted_iota(jnp.int32, logits.shape, 1)
    lane_f = lane.astype(F32)
    neg = -jnp.inf
    big = float(LANES)

    is_group = lane < N_GROUPS
    g_max = jnp.max(jnp.where(is_group, logits, neg), axis=-1, keepdims=True)
    g_sum = jnp.sum(jnp.where(is_group, jnp.exp(logits - g_max), 0.0), axis=-1, keepdims=True)
    p_top = 1.0 / g_sum
    grp = jnp.min(jnp.where(is_group & (logits == g_max), lane_f, big), axis=-1, keepdims=True)

    e_lo = N_GROUPS + grp * EXPERTS_PER_GROUP
    in_grp = (lane_f >= e_lo) & (lane_f < e_lo + EXPERTS_PER_GROUP)
    e_max = jnp.max(jnp.where(in_grp, logits, neg), axis=-1, keepdims=True)
    e_sum = jnp.sum(jnp.where(in_grp, jnp.exp(logits - e_max), 0.0), axis=-1, keepdims=True)
    i1 = jnp.min(jnp.where(in_grp & (logits == e_max), lane_f, big), axis=-1, keepdims=True)
    rest = in_grp & (lane_f != i1)
    e_2nd = jnp.max(jnp.where(rest, logits, neg), axis=-1, keepdims=True)
    i2 = jnp.min(jnp.where(rest & (logits == e_2nd), lane_f, big), axis=-1, keepdims=True)
    p1 = 1.0 / e_sum
    p2 = jnp.exp(e_2nd - e_max) / e_sum
    tot = p1 + p2
    gate_ref[...] = jnp.where(lane == 0, p_top * p1 / tot, jnp.where(lane == 1, p_top * p2 / tot, 0.0))

    e1 = i1 - N_GROUPS
    e2 = i2 - N_GROUPS
    hot = jnp.where((lane_f == e1) | (lane_f == e2), 1.0, 0.0)
    row = lax.broadcasted_iota(jnp.int32, (tm, tm), 0)
    col = lax.broadcasted_iota(jnp.int32, (tm, tm), 1)
    before = _dot(jnp.where(col < row, 1.0, 0.0).astype(BF16), _bf(hot)) + carry_ref[...]
    r1 = jnp.sum(jnp.where(lane_f == e1, before, 0.0), axis=-1, keepdims=True)
    r2 = jnp.sum(jnp.where(lane_f == e2, before, 0.0), axis=-1, keepdims=True)
    total = carry_ref[...] + jnp.sum(hot, axis=0, keepdims=True)
    carry_ref[...] = total
    cnt_ref[...] = jnp.broadcast_to(total, cnt_ref.shape).astype(jnp.int32)
    idx_ref[...] = jnp.where(lane == 0, e1, jnp.where(lane == 1, e2, jnp.where(lane == 2, r1, jnp.where(
        lane == 3, r2, 0.0)))).astype(jnp.int32)

    words = _pack_bf16_pairs(xn)
    for s in range(8):
        xg_ref[pl.ds(s, tm, stride=8), :] = words[:, s * LANES:(s + 1) * LANES]


def _router(x2d, gain, w_group, b_group, w_expert, b_expert):
    m, d = x2d.shape
    assert d == 2 * 8 * LANES
    tm = _row_tile(m, ROW_TILE)
    pad = LANES - N_GROUPS - N_EXPERTS
    w = jnp.concatenate([w_group, w_expert, jnp.zeros((d, pad), F32)], axis=1)
    bias = jnp.concatenate([b_group, b_expert, jnp.zeros((pad,), F32)]).reshape(1, LANES)
    return pl.pallas_call(
        _router_kernel,
        grid=(m // tm,),
        in_specs=[
            pl.BlockSpec((tm, d), lambda i: (i, 0)),
            pl.BlockSpec((1, d), lambda i: (0, 0)),
            pl.BlockSpec((d, LANES), lambda i: (0, 0)),
            pl.BlockSpec((1, LANES), lambda i: (0, 0)),
        ],
        out_specs=[pl.BlockSpec((tm, LANES), lambda i: (i, 0)), pl.BlockSpec((tm, LANES), lambda i: (i, 0)),
                   pl.BlockSpec((tm * 8, LANES), lambda i: (i, 0)), pl.BlockSpec((8, LANES), lambda i: (0, 0))],
        out_shape=[jax.ShapeDtypeStruct((m, LANES), F32), jax.ShapeDtypeStruct((m, LANES), jnp.int32),
                   jax.ShapeDtypeStruct((m * 8, LANES), jnp.uint32), jax.ShapeDtypeStruct((8, LANES), jnp.int32)],
        scratch_shapes=[pltpu.VMEM((1, LANES), F32)],
        compiler_params=_params("arbitrary"),
        name="router",
    )(x2d, gain.reshape(1, d), w, bias)


def _invert_kernel(dest_ref, inv_ref):
    def clear(s, carry):
        inv_ref[s] = -1
        return carry

    lax.fori_loop(0, inv_ref.shape[0], clear, 0, unroll=8)

    def put(f, carry):
        inv_ref[dest_ref[f]] = f
        return carry

    lax.fori_loop(0, dest_ref.shape[0], put, 0, unroll=8)


def _invert(dest, p):
    assert p % 8 == 0 and dest.shape[0] % 8 == 0
    smem = pl.BlockSpec(memory_space=pltpu.SMEM)
    return pl.pallas_call(
        _invert_kernel, in_specs=[smem], out_specs=smem,
        out_shape=jax.ShapeDtypeStruct((p,), jnp.int32), name="moe_invert",
    )(dest)


def _expert_kernel(be_ref, nxt_ref, run_ref, nu_ref, src_ref, dst_ref, xg_hbm, w1_hbm, w3_hbm, w2_hbm, o_hbm,
                   xbuf, ybuf, w1f, w3f, w2f, w1s, w3s, w2s, sem_in, sem_out, sem_w, *, layer):
    i = pl.program_id(0)
    n_used = nu_ref[0]
    par = i % 2
    half_e = D_EXPERT // 2

    def weight_copies(expert, slot):
        return [pltpu.make_async_copy(w_hbm.at[layer, expert], w_f.at[slot], sem_w.at[slot, j])
                for j, (w_hbm, w_f) in enumerate(((w1_hbm, w1f), (w3_hbm, w3f), (w2_hbm, w2f)))]

    def gather(block, r, slot):
        src = pl.multiple_of(src_ref[block * MOE_BLOCK + r], 8)
        return pltpu.make_async_copy(xg_hbm.at[pl.ds(src, 8), :], xbuf.at[slot, pl.ds(r * 8, 8), :], sem_in.at[slot])

    def scatter(block, r, slot):
        dst = pl.multiple_of(dst_ref[(block + 1) * MOE_BLOCK + r], 8)
        return pltpu.make_async_copy(ybuf.at[slot, pl.ds(r * 8, 8), :], o_hbm.at[pl.ds(dst, 8), :], sem_out.at[slot])

    def wait_gathers(slot):
        pltpu.make_async_copy(xbuf.at[1 - slot], xbuf.at[slot], sem_in.at[slot]).wait()

    def wait_scatters(slot):
        pltpu.make_async_copy(ybuf.at[slot], ybuf.at[1 - slot], sem_out.at[slot]).wait()

    @pl.when(i == 0)
    def _():
        ybuf[1] = jnp.zeros(ybuf.shape[1:], ybuf.dtype)
        n_real = o_hbm.shape[0] - 2 * MOE_BLOCK * 8
        fill1 = pltpu.make_async_copy(ybuf.at[1], o_hbm.at[pl.ds(n_real + MOE_BLOCK * 8, MOE_BLOCK * 8), :],
                                      sem_out.at[1])
        fill1.start()
        fill1.wait()
        pltpu.make_async_copy(ybuf.at[1], o_hbm.at[pl.ds(n_real, MOE_BLOCK * 8), :], sem_out.at[0]).start()

        def first(r, carry):
            gather(0, r, 0).start()
            return carry

        lax.fori_loop(0, MOE_BLOCK, first, 0, unroll=8)

    @pl.when(i < n_used)
    def _():
        expert = be_ref[i]
        wslot = run_ref[i] % 2

        @pl.when(i == 0)
        def _():
            for cp in weight_copies(expert, 0):
                cp.start()

        @pl.when((i == 0) | (expert != be_ref[jnp.maximum(i - 1, 0)]))
        def _():
            for cp in weight_copies(expert, wslot):
                cp.wait()

            @pl.when(nxt_ref[i] != expert)
            def _():
                for cp in weight_copies(nxt_ref[i], 1 - wslot):
                    cp.start()

            w1s[...] = _bf(w1f[wslot])
            w3s[...] = _bf(w3f[wslot])
            w2s[...] = _bf(w2f[wslot])

        wait_gathers(par)

        def move_rows(group, n_groups=6):
            lo, hi = group * MOE_BLOCK // n_groups, (group + 1) * MOE_BLOCK // n_groups
            for r in range(lo, hi):
                gather(i + 1, r, 1 - par).start(priority=r % 2)
                scatter(i - 1, r, 1 - par).start(priority=(r + 1) % 2)

        tiles = [xbuf[par, pl.ds(s, MOE_BLOCK, stride=8), :] for s in range(8)]
        xb = jnp.concatenate([_bf(_unpack_lo(w)) for w in tiles] + [_bf(_unpack_hi(w)) for w in tiles], axis=1)
        move_rows(0)
        h1a = _dot(xb, w1s[:, :half_e])
        move_rows(1)
        h3a = _dot(xb, w3s[:, :half_e])
        move_rows(2)
        act_a = _bf(_silu(h1a) * h3a)
        h1b = _dot(xb, w1s[:, half_e:])
        move_rows(3)
        h3b = _dot(xb, w3s[:, half_e:])
        move_rows(4)
        act_b = _bf(_silu(h1b) * h3b)
        y = _dot(act_a, w2s[:half_e, :])
        move_rows(5)
        y = y + _dot(act_b, w2s[half_e:, :])

        wait_scatters(par)
        words = _pack_bf16_pairs(y)
        for s in range(8):
            ybuf[par, pl.ds(s, MOE_BLOCK, stride=8), :] = words[:, s * LANES:(s + 1) * LANES]

        @pl.when(i == n_used - 1)
        def _():
            def last(r, carry):
                scatter(i, r, par).start()
                return carry

            lax.fori_loop(0, MOE_BLOCK, last, 0, unroll=8)
            wait_scatters(par)
            wait_scatters(1 - par)
            wait_gathers(1 - par)


def _moe(x2d, gain, w_group, b_group, w_expert, b_expert, w1, w3, w2, layer, final_shape=None):
    n, d = x2d.shape
    gates_l, idx_l, xg, cnt = _router(x2d, gain, w_group, b_group, w_expert, b_expert)

    a = n * TOP_K
    n_blocks = -(-a // MOE_BLOCK) + N_EXPERTS
    p = n_blocks * MOE_BLOCK
    counts = cnt[0, :N_EXPERTS]
    padded = (counts + MOE_BLOCK - 1) // MOE_BLOCK * MOE_BLOCK
    pad_end = jnp.cumsum(padded)
    pad_start = pad_end - padded
    e_hot = idx_l[:, :TOP_K, None] == jnp.arange(N_EXPERTS, dtype=jnp.int32)
    dest = jnp.sum(jnp.where(e_hot, pad_start, 0), axis=-1) + idx_l[:, TOP_K:2 * TOP_K]
    dest = jnp.clip(dest.reshape(-1), 0, p - 1).astype(jnp.int32)
    blk0 = jnp.arange(n_blocks, dtype=jnp.int32) * MOE_BLOCK
    block_expert = jnp.minimum(jnp.searchsorted(pad_end, blk0, side="right"), N_EXPERTS - 1).astype(jnp.int32)
    n_used = (pad_end[-1] // MOE_BLOCK).astype(jnp.int32).reshape(1)
    experts = jnp.arange(N_EXPERTS, dtype=jnp.int32)
    later = jnp.where((experts[None, :] > experts[:, None]) & (counts[None, :] > 0), experts[None, :], N_EXPERTS)
    next_owner = jnp.min(later, axis=1)
    next_expert = jnp.where(next_owner < N_EXPERTS, next_owner, experts)[block_expert].astype(jnp.int32)
    run_index = (jnp.cumsum(jnp.concatenate([jnp.ones((1,), jnp.int32),
                                             (block_expert[1:] != block_expert[:-1]).astype(jnp.int32)])) - 1
                 ).astype(jnp.int32)
    codes = _invert(dest, p)
    slot = jnp.arange(p, dtype=jnp.int32)
    spare = TOP_K * n + (slot // MOE_BLOCK % 2) * MOE_BLOCK + slot % MOE_BLOCK
    src_tok = (jnp.maximum(codes, 0) >> 1) * 8
    dst_row = jnp.where(codes >= 0, (codes & 1) * n + (codes >> 1), spare) * 8
    lead = (TOP_K * n + MOE_BLOCK + jnp.arange(MOE_BLOCK, dtype=jnp.int32)) * 8
    dst_row = jnp.concatenate([lead, dst_row])

    hbm = pl.BlockSpec(memory_space=pl.ANY)
    out_rows = TOP_K * n + 2 * MOE_BLOCK
    out2 = pl.pallas_call(
        functools.partial(_expert_kernel, layer=layer),
        grid_spec=pltpu.PrefetchScalarGridSpec(
            num_scalar_prefetch=6,
            grid=(n_blocks,),
            in_specs=[hbm, hbm, hbm, hbm],
            out_specs=hbm,
            scratch_shapes=[
                pltpu.VMEM((2, MOE_BLOCK * 8, LANES), jnp.uint32), pltpu.VMEM((2, MOE_BLOCK * 8, LANES), jnp.uint32),
                pltpu.VMEM((2, d, D_EXPERT), F32), pltpu.VMEM((2, d, D_EXPERT), F32), pltpu.VMEM((2, D_EXPERT, d), F32),
                pltpu.VMEM((d, D_EXPERT), BF16), pltpu.VMEM((d, D_EXPERT), BF16), pltpu.VMEM((D_EXPERT, d), BF16),
                pltpu.SemaphoreType.DMA((2,)), pltpu.SemaphoreType.DMA((2,)), pltpu.SemaphoreType.DMA((2, 3)),
            ],
        ),
        out_shape=jax.ShapeDtypeStruct((out_rows * 8, LANES), jnp.uint32),
        compiler_params=_params("arbitrary"),
        name="moe_experts",
    )(block_expert, next_expert, run_index, n_used, src_tok, dst_row, xg, w1, w3, w2)
    return _combine(x2d, gates_l, out2, final_shape)


def _combine_tile(x_ref, gate_ref, a_ref, b_ref, o_ref):
    tm = x_ref.shape[0]
    half = x_ref.shape[1] // 2
    g0 = gate_ref[:, 0:1]
    g1 = gate_ref[:, 1:2]
    for s in range(8):
        wa = a_ref[pl.ds(s, tm, stride=8), :]
        wb = b_ref[pl.ds(s, tm, stride=8), :]
        lo = slice(s * LANES, (s + 1) * LANES)
        hi = slice(half + s * LANES, half + (s + 1) * LANES)
        o_ref[:, lo] = x_ref[:, lo] + (g0 * _unpack_lo(wa) + g1 * _unpack_lo(wb))
        o_ref[:, hi] = x_ref[:, hi] + (g0 * _unpack_hi(wa) + g1 * _unpack_hi(wb))


def _combine_kernel(x_ref, gate_ref, a_ref, b_ref, o_ref):
    _combine_tile(x_ref, gate_ref, a_ref, b_ref, o_ref)


def _combine_final_kernel(x_ref, gate_ref, a_ref, b_ref, o_hbm, obuf, sem, *, tiles_per_batch):
    i = pl.program_id(0)
    n_steps = pl.num_programs(0)
    tm = x_ref.shape[0]

    def copies(step, slot):
        batch, j = step // tiles_per_batch, step % tiles_per_batch
        row = pl.multiple_of(j * tm, 8)
        main = pltpu.make_async_copy(obuf.at[slot, pl.ds(N_META, tm - N_META), :],
                                     o_hbm.at[batch, pl.ds(row, tm - N_META), :], sem.at[slot, 0])
        head = pltpu.make_async_copy(obuf.at[slot, pl.ds(0, N_META), :],
                                     o_hbm.at[batch, pl.ds(pl.multiple_of(jnp.maximum(row - N_META, 0), 8), N_META), :],
                                     sem.at[slot, 1])
        return main, head, j > 0

    def wait_step(step, slot):
        main, head, has_head = copies(step, slot)
        main.wait()

        @pl.when(has_head)
        def _():
            head.wait()

    slot = i % 2

    @pl.when(i >= 2)
    def _():
        wait_step(i - 2, slot)

    _combine_tile(x_ref, gate_ref, a_ref, b_ref, obuf.at[slot])
    main, head, has_head = copies(i, slot)
    main.start()

    @pl.when(has_head)
    def _():
        head.start()

    @pl.when(i == n_steps - 1)
    def _():
        wait_step(i, slot)

        @pl.when(i >= 1)
        def _():
            wait_step(i - 1, 1 - slot)


def _combine(x2d, gates, out2, final_shape=None):
    n, d = x2d.shape
    tm = _row_tile(n, ROW_TILE)
    in_specs = [
        pl.BlockSpec((tm, d), lambda i: (i, 0)),
        pl.BlockSpec((tm, LANES), lambda i: (i, 0)),
        pl.BlockSpec((tm * 8, LANES), lambda i: (i, 0)),
        pl.BlockSpec((tm * 8, LANES), lambda i: (n // tm + i, 0)),
    ]
    if final_shape is None:
        return pl.pallas_call(
            _combine_kernel,
            grid=(n // tm,),
            in_specs=in_specs,
            out_specs=pl.BlockSpec((tm, d), lambda i: (i, 0)),
            out_shape=jax.ShapeDtypeStruct((n, d), F32),
            compiler_params=_params("parallel"),
            name="moe_combine",
        )(x2d, gates, out2, out2)
    b, t = final_shape
    assert t % tm == 0 and tm > N_META
    return pl.pallas_call(
        functools.partial(_combine_final_kernel, tiles_per_batch=t // tm),
        grid=(n // tm,),
        in_specs=in_specs,
        out_specs=pl.BlockSpec(memory_space=pl.ANY),
        out_shape=jax.ShapeDtypeStruct((b, t - N_META, d), F32),
        scratch_shapes=[pltpu.VMEM((2, tm, d), F32), pltpu.SemaphoreType.DMA((2, 2))],
        compiler_params=_params("arbitrary"),
        name="moe_combine_final",
    )(x2d, gates, out2, out2)


def _even_layer(x, norm_g, w_in, conv_w, conv_b, b_i, b_f, a_norm, w_gate2, b_gate, b_norm, w_out):
    b, t, d = x.shape
    n = b * t
    a_w = 2 * A_HEADS * A_DK + 2 * A_HEADS * A_DV
    g_w = 2 * A_HEADS
    b_w = 2 * B_HEADS * B_DK + 2 * B_HEADS * B_DV
    main = a_w + b_w
    w = _even_weight(w_in, a_w, g_w, b_w, GATE_RANK)
    z = _normproj(x.reshape(n, d), norm_g, w).reshape(b, t, main + MXU_DIM)

    ya = _mlstm(z, main, conv_w, conv_b, b_i, b_f, a_norm)

    wg = jnp.zeros((B_HEADS, MXU_DIM, B_DK), F32).at[:, g_w:g_w + GATE_RANK, :].set(
        w_gate2.reshape(GATE_RANK, B_HEADS, B_DK).transpose(1, 0, 2))
    dummy = jnp.zeros((1, B_HEADS * B_DK), F32)
    hp = 2
    hspec = pl.BlockSpec((1, hp * B_DK), lambda bi, g: (0, g))
    yb = _gla_call(
        z, B_HEADS, hp, B_DK, B_DV,
        (a_w, a_w + B_HEADS * B_DK, a_w + 2 * B_HEADS * B_DK, a_w + 2 * B_HEADS * B_DK + B_HEADS * B_DV, main),
        MXU_DIM,
        ((wg, pl.BlockSpec((hp, MXU_DIM, B_DK), lambda bi, g: (g, 0, 0))),
         (b_gate.reshape(1, -1), hspec), (dummy, hspec)),
        b_norm, "gla")
    return _outproj(ya.reshape(n, -1), yb.reshape(n, -1), w_out.astype(BF16), x.reshape(n, d)).reshape(b, t, d)


def _odd_layer(x, lb, norm_g, w_in, c_norm, q_a_norm, w_q_up, kv_a_norm, w_kv_up, q_norm, k_norm, w_out):
    b, t, d = x.shape
    n = b * t
    c_w = 2 * C_HEADS * C_DK + 2 * C_HEADS * C_DV
    swap = (jnp.arange(D_ROPE) + D_ROPE // 2) % D_ROPE
    kr0 = c_w + Q_LORA + KV_LORA
    used = kr0 + 2 * D_ROPE
    total = -(-used // MXU_DIM) * MXU_DIM
    z2 = _normproj(x.reshape(n, d), norm_g, _odd_weight(w_in, kr0))
    z = z2.reshape(b, t, total)

    hp = 4
    hspec = pl.BlockSpec((1, hp * C_DK), lambda bi, g: (0, g))
    yc = _gla_call(
        z, C_HEADS, hp, C_DK, C_DV,
        (0, C_HEADS * C_DK, 2 * C_HEADS * C_DK, 2 * C_HEADS * C_DK + C_HEADS * C_DV, C_HEADS * C_DK),
        C_DK,
        ((jnp.log(lb).reshape(1, -1), hspec), (jnp.log1p(-lb).reshape(1, -1), hspec), ((1.0 - lb).reshape(1, -1), hspec)),
        c_norm, "hgrn")

    dq = D_NOPE + D_ROPE
    wq = w_q_up.reshape(Q_LORA, D_HEADS, dq)
    wq_rope = wq[:, :, D_NOPE:]
    wq_p = jnp.concatenate([wq[:, :, :D_NOPE].reshape(Q_LORA, -1),
                            jnp.concatenate([wq_rope, wq_rope[:, :, swap]], axis=-1).reshape(Q_LORA, -1)],
                           axis=1).astype(BF16)
    wkv = w_kv_up.reshape(KV_LORA, D_HEADS, D_NOPE + D_V)
    wkv_p = jnp.concatenate([wkv[:, :, :D_NOPE].reshape(KV_LORA, -1), wkv[:, :, D_NOPE:].reshape(KV_LORA, -1)],
                            axis=1).astype(BF16)
    qn = _normproj(z2, q_a_norm, wq_p, x_col_block=c_w // Q_LORA).reshape(b, t, -1)
    kvn = _normproj(z2, kv_a_norm, wkv_p, x_col_block=(c_w + Q_LORA) // KV_LORA).reshape(b, t, -1)

    pos = jnp.arange(t, dtype=F32)
    half = D_ROPE // 2
    inv = ROPE_THETA ** (-jnp.arange(half, dtype=F32) / half)
    ang = pos[:, None] * inv[None, :]
    cos, sin = jnp.cos(ang), jnp.sin(ang)
    tab = jnp.concatenate([cos, cos, -sin, sin], axis=1)
    pair = lambda g: jnp.concatenate([g[D_NOPE:], g[D_NOPE:][swap]]).reshape(1, LANES)
    yd = _mla(qn, kvn, z, kr0, tab, q_norm[:D_NOPE].reshape(1, LANES), pair(q_norm),
              k_norm[:D_NOPE].reshape(1, LANES), pair(k_norm))
    return _outproj(yc.reshape(n, -1), yd.reshape(n, -1), w_out.astype(BF16), x.reshape(n, d)).reshape(b, t, d)


def kernel(x, meta_tokens, ab_norm, ab_w_in, a_conv_w, a_conv_b, a_b_i, a_b_f, a_head_norm, b_w_gate2, b_b_gate, b_head_norm, ab_w_out, cd_norm, cd_w_in, c_lower_bound, c_head_norm, d_q_a_norm, d_w_q_up, d_kv_a_norm, d_w_kv_up, d_q_norm, d_k_norm, cd_w_out, moe_norm, moe_w_group, moe_b_group, moe_w_expert, moe_b_expert, moe_w1, moe_w3, moe_w2):
    b = x.shape[0]
    depth = moe_norm.shape[0]
    h = jnp.concatenate([jnp.broadcast_to(meta_tokens.astype(x.dtype)[None], (b, N_META, D_MODEL)), x], axis=1)
    t = h.shape[1]
    lb_cum = jnp.cumsum(jax.nn.softmax(c_lower_bound.astype(F32), axis=0), axis=0)
    lower_bounds = lb_cum - lb_cum[0]
    for layer in range(depth):
        j = layer // 2
        if layer % 2 == 0:
            h = _even_layer(h, ab_norm[j], ab_w_in[j], a_conv_w[j], a_conv_b[j], a_b_i[j], a_b_f[j], a_head_norm[j],
                            b_w_gate2[j], b_b_gate[j], b_head_norm[j], ab_w_out[j])
        else:
            h = _odd_layer(h, lower_bounds[layer], cd_norm[j], cd_w_in[j], c_head_norm[j], d_q_a_norm[j],
                           d_w_q_up[j], d_kv_a_norm[j], d_w_kv_up[j], d_q_norm[j], d_k_norm[j], cd_w_out[j])
        last = layer == depth - 1
        h = _moe(h.reshape(b * t, D_MODEL), moe_norm[layer], moe_w_group[layer], moe_b_group[layer],
                 moe_w_expert[layer], moe_b_expert[layer], moe_w1, moe_w3, moe_w2, layer,
                 final_shape=(b, t) if last else None)
        if not last:
            h = h.reshape(b, t, D_MODEL)
    return h
```

```python
import functools
import math

import jax
import jax.numpy as jnp
from jax import lax
from jax.experimental import pallas as pl
from jax.experimental.pallas import tpu as pltpu

F32 = jnp.float32
BF16 = jnp.bfloat16
HIGHEST = lax.Precision.HIGHEST

D_MODEL = 2048
N_META = 16
CHUNK = 64
CONV_K = 4
EPS = 1e-6
A_HEADS, A_DK, A_DV = 4, 128, 256
B_HEADS, B_DK, B_DV = 4, 128, 256
GATE_RANK = 16
GATE_TAU = 16.0
C_HEADS, C_DK, C_DV = 8, 128, 128
D_HEADS, D_NOPE, D_ROPE, D_V = 8, 128, 64, 128
Q_LORA, KV_LORA = 512, 256
ROPE_THETA = 10000.0
N_GROUPS, EXPERTS_PER_GROUP = 4, 8
N_EXPERTS = N_GROUPS * EXPERTS_PER_GROUP
TOP_K = 2
D_EXPERT = 512

LANES = 128
MXU_DIM = 256
BF16_ROWS = 16
VMEM_LIMIT = 56 * 1024 * 1024
MOE_BLOCK = MXU_DIM
ROW_TILE = 688
ROW_TILE_BF16 = 2 * ROW_TILE
COL_TILE = 1280
COL_TILE_OUT = 1024
PREP_ROWS = 768
ATT_BLOCK = 256
ATT_PAD = ATT_BLOCK - N_META

_NT = (((1,), (1,)), ((), ()))
_TN = (((0,), (0,)), ((), ()))


def _dot(a, b, precision=None):
    return jnp.dot(a, b, preferred_element_type=F32, precision=precision)


def _dot_nt(a, b):
    return lax.dot_general(a, b, _NT, preferred_element_type=F32)


def _dot_tn(a, b):
    return lax.dot_general(a, b, _TN, preferred_element_type=F32)


def _bf(x):
    return x.astype(BF16)


def _split3(x):
    hi = _bf(x)
    rest = x - hi.astype(F32)
    mid = _bf(rest)
    return hi, mid, _bf(rest - mid.astype(F32))


def _log_sigmoid(x):
    return jnp.minimum(x, 0.0) - jnp.log1p(jnp.exp(-jnp.abs(x)))


def _sigmoid(x):
    return 1.0 / (1.0 + jnp.exp(-x))


def _silu(x):
    return x * _sigmoid(x)


def _row_tile(m, cap):
    best = None
    for t in range(BF16_ROWS, min(m, cap) + 1, BF16_ROWS):
        if m % t == 0:
            best = t
    assert best is not None, m
    return best


def _col_tile(n, cap):
    best = None
    for t in range(MXU_DIM, min(n, cap) + 1, MXU_DIM):
        if n % t == 0:
            best = t
    assert best is not None, n
    return best


def _params(*sem):
    return pltpu.CompilerParams(dimension_semantics=sem, vmem_limit_bytes=VMEM_LIMIT)


_RELAYOUT_ROWS = 256


def _even_weight_kernel(wa_ref, wb_ref, wc_ref, o_ref, *, n_plain, n_shift, shift, gate_cols):
    ob = pl.program_id(0)
    rows = o_ref.shape[0]
    chunks = rows // _RELAYOUT_ROWS

    @pl.when(ob < n_plain)
    def _():
        def body(c, carry):
            r = pl.ds(pl.multiple_of(c * _RELAYOUT_ROWS, _RELAYOUT_ROWS), _RELAYOUT_ROWS)
            o_ref[r, :] = _bf(wa_ref[r, :])
            return carry

        lax.fori_loop(0, chunks, body, 0)

    @pl.when((ob >= n_plain) & (ob < n_plain + n_shift))
    def _():
        def body(c, carry):
            r = pl.ds(pl.multiple_of(c * _RELAYOUT_ROWS, _RELAYOUT_ROWS), _RELAYOUT_ROWS)
            wide = jnp.concatenate([wa_ref[r, :], wb_ref[r, :]], axis=1)
            o_ref[r, :] = _bf(wide[:, shift:shift + MXU_DIM])
            return carry

        lax.fori_loop(0, chunks, body, 0)

    @pl.when(ob == n_plain + n_shift)
    def _():
        lane = lax.broadcasted_iota(jnp.int32, (_RELAYOUT_ROWS, LANES), 1)

        def body(c, carry):
            r = pl.ds(pl.multiple_of(c * _RELAYOUT_ROWS, _RELAYOUT_ROWS), _RELAYOUT_ROWS)
            first = jnp.where(lane < shift, wc_ref[r, :], jnp.where(lane < gate_cols, wb_ref[r, :], 0.0))
            o_ref[r, :] = _bf(jnp.concatenate([first, jnp.zeros_like(first)], axis=1))
            return carry

        lax.fori_loop(0, chunks, body, 0)


def _even_weight(w_in, a_w, g_w, b_w, rank):
    d = w_in.shape[0]
    assert a_w % MXU_DIM == 0 and b_w % MXU_DIM == 0 and g_w + rank <= LANES and d % _RELAYOUT_ROWS == 0
    n_plain, n_shift = a_w // MXU_DIM, b_w // MXU_DIM
    n_out = n_plain + n_shift + 1
    last = n_out - 1

    def b_index(ob):
        return (0, jnp.where(ob < n_plain, 0, jnp.where(ob < last, 2 * (ob + 1), (a_w + g_w + b_w) // LANES)))

    return pl.pallas_call(
        functools.partial(_even_weight_kernel, n_plain=n_plain, n_shift=n_shift, shift=g_w, gate_cols=g_w + rank),
        grid=(n_out,),
        in_specs=[
            pl.BlockSpec((d, MXU_DIM), lambda ob: (0, jnp.minimum(ob, last - 1))),
            pl.BlockSpec((d, LANES), b_index),
            pl.BlockSpec((d, LANES), lambda ob: (0, a_w // LANES)),
        ],
        out_specs=pl.BlockSpec((d, MXU_DIM), lambda ob: (0, ob)),
        out_shape=jax.ShapeDtypeStruct((d, n_out * MXU_DIM), BF16),
        compiler_params=_params("parallel"),
        name="even_weight",
    )(w_in, w_in, w_in)


def _odd_weight_kernel(w_ref, o_ref, *, n_plain):
    ob = pl.program_id(0)
    chunks = o_ref.shape[0] // _RELAYOUT_ROWS
    half = D_ROPE // 2

    def body(c, carry):
        r = pl.ds(pl.multiple_of(c * _RELAYOUT_ROWS, _RELAYOUT_ROWS), _RELAYOUT_ROWS)
        w = w_ref[r, :]

        @pl.when(ob < n_plain)
        def _():
            o_ref[r, :] = _bf(w)

        @pl.when(ob == n_plain)
        def _():
            pair = jnp.concatenate([w[:, :D_ROPE], w[:, half:D_ROPE], w[:, :half]], axis=1)
            o_ref[r, :] = _bf(jnp.concatenate([pair, jnp.zeros_like(pair)], axis=1))

        return carry

    lax.fori_loop(0, chunks, body, 0)


def _odd_weight(w_in, kr0):
    d = w_in.shape[0]
    assert kr0 % MXU_DIM == 0 and w_in.shape[1] == kr0 + D_ROPE and d % _RELAYOUT_ROWS == 0
    n_plain = kr0 // MXU_DIM
    return pl.pallas_call(
        functools.partial(_odd_weight_kernel, n_plain=n_plain),
        grid=(n_plain + 1,),
        in_specs=[pl.BlockSpec((d, MXU_DIM), lambda ob: (0, ob))],
        out_specs=pl.BlockSpec((d, MXU_DIM), lambda ob: (0, ob)),
        out_shape=jax.ShapeDtypeStruct((d, kr0 + MXU_DIM), BF16),
        compiler_params=_params("parallel"),
        name="odd_weight",
    )(w_in)


def _normproj_kernel(x_ref, g_ref, w_ref, o_ref, xs_ref):
    tm = xs_ref.shape[0]

    @pl.when(pl.program_id(1) == 0)
    def _():
        def body(c, carry):
            r0 = pl.multiple_of(c * BF16_ROWS, BF16_ROWS)
            x = x_ref[pl.ds(r0, BF16_ROWS), :]
            ms = jnp.mean(x * x, axis=-1, keepdims=True)
            xs_ref[pl.ds(r0, BF16_ROWS), :] = _bf(x * lax.rsqrt(ms + EPS) * g_ref[...])
            return carry

        lax.fori_loop(0, tm // BF16_ROWS, body, 0, unroll=8)

    o_ref[...] = _dot(xs_ref[...], w_ref[...]).astype(o_ref.dtype)


def _normproj(x2d, gain, w, *, x_col_block=0, out_dtype=F32):
    m = x2d.shape[0]
    k, n = w.shape
    tm = _row_tile(m, ROW_TILE)
    tn = _col_tile(n, COL_TILE)
    return pl.pallas_call(
        _normproj_kernel,
        grid=(m // tm, n // tn),
        in_specs=[
            pl.BlockSpec((tm, k), lambda i, j: (i, x_col_block)),
            pl.BlockSpec((1, k), lambda i, j: (0, 0)),
            pl.BlockSpec((k, tn), lambda i, j: (0, j)),
        ],
        out_specs=pl.BlockSpec((tm, tn), lambda i, j: (i, j)),
        out_shape=jax.ShapeDtypeStruct((m, n), out_dtype),
        scratch_shapes=[pltpu.VMEM((tm, k), BF16)],
        compiler_params=_params("parallel", "arbitrary"),
        name="normproj",
    )(x2d, gain.reshape(1, k).astype(F32), w)


def _outproj_kernel(ya_ref, yb_ref, w_ref, r_ref, o_ref):
    ka = ya_ref.shape[1]
    acc = _dot(ya_ref[...], w_ref[:ka, :]) + _dot(yb_ref[...], w_ref[ka:, :])
    o_ref[...] = r_ref[...] + acc


def _outproj(ya, yb, w, res):
    m, ka = ya.shape
    kb = yb.shape[1]
    n = w.shape[1]
    tm = _row_tile(m, ROW_TILE_BF16)
    tn = _col_tile(n, COL_TILE_OUT)
    return pl.pallas_call(
        _outproj_kernel,
        grid=(m // tm, n // tn),
        in_specs=[
            pl.BlockSpec((tm, ka), lambda i, j: (i, 0)),
            pl.BlockSpec((tm, kb), lambda i, j: (i, 0)),
            pl.BlockSpec((ka + kb, tn), lambda i, j: (0, j)),
            pl.BlockSpec((tm, tn), lambda i, j: (i, j)),
        ],
        out_specs=pl.BlockSpec((tm, tn), lambda i, j: (i, j)),
        out_shape=jax.ShapeDtypeStruct((m, n), F32),
        compiler_params=_params("parallel", "arbitrary"),
        name="outproj",
    )(ya, yb, w, res)


def _mlstm_kernel(bi_ref, bf_ref, q_ref, k_ref, v_ref, og_ref, gt_ref,
                  cwq_ref, cwk_ref, cbq_ref, cbk_ref, hn_ref, o_ref, c_ref, n_ref, m_ref, *, heads, group):
    t_total = q_ref.shape[1]
    n_chunks = (t_total - N_META) // CHUNK
    head0 = pl.program_id(1) * heads

    c_ref[...] = jnp.zeros_like(c_ref)
    n_ref[...] = jnp.zeros_like(n_ref)
    m_ref[...] = jnp.zeros_like(m_ref)

    def conv(taps, cw, cb, length):
        y = cb
        for j in range(CONV_K):
            y = y + taps[j] * cw[j:j + 1, :]
        return _silu(y)

    def window_taps(win, length):
        return [win[8 - (CONV_K - 1) + j:8 - (CONV_K - 1) + j + length, :] for j in range(CONV_K)]

    hs = range(heads)
    kcol = [slice(hh * A_DK, (hh + 1) * A_DK) for hh in hs]
    vcol = [slice(hh * A_DV, (hh + 1) * A_DV) for hh in hs]

    def gates(hh, blk, blk_parts, length, causal, upper):
        head = head0 + hh
        b_i = bi_ref[head]
        b_f = bf_ref[head]
        lane = lax.broadcasted_iota(jnp.int32, (length, LANES), 1)
        ig_c = jnp.sum(jnp.where(lane == head, blk, 0.0), axis=1, keepdims=True) + b_i
        lf_c = _log_sigmoid(jnp.sum(jnp.where(lane == A_HEADS + head, blk, 0.0), axis=1, keepdims=True) + b_f)
        sel_r = lax.broadcasted_iota(jnp.int32, (8, LANES), 0)
        sel_l = lax.broadcasted_iota(jnp.int32, (8, LANES), 1)
        sel = jnp.where(sel_l == head + A_HEADS * sel_r, 1.0, 0.0).astype(BF16)
        rows = _dot_nt(sel, blk_parts[0]) + (_dot_nt(sel, blk_parts[1]) + _dot_nt(sel, blk_parts[2]))
        ig_r = rows[0:1, :] + b_i
        lf_r = _log_sigmoid(rows[1:2, :] + b_f)
        b_c = jnp.sum(jnp.where(causal, lf_r, 0.0), axis=1, keepdims=True)
        b_r = jnp.sum(jnp.where(upper, lf_c, 0.0), axis=0, keepdims=True)
        b_end = b_c[length - 1:length, :]
        w_end = b_end - b_c + ig_c
        m_loc = jnp.max(w_end, axis=0, keepdims=True)
        d = jnp.where(causal, b_c - b_r + ig_r, -jnp.inf)
        return b_c, b_end, jnp.exp(w_end - m_loc), m_loc, d, jnp.max(d, axis=1, keepdims=True)

    def local_stage(o, length, qwin, kwin):
        row = lax.broadcasted_iota(jnp.int32, (length, length), 0)
        col = lax.broadcasted_iota(jnp.int32, (length, length), 1)
        causal = col <= row
        blk = gt_ref[0, pl.ds(o, length), 0:LANES]
        blk_parts = _split3(blk)
        gt = [gates(hh, blk, blk_parts, length, causal, row <= col) for hh in hs]
        q = [conv(qwin[hh], cwq_ref[:, kcol[hh]], cbq_ref[:, kcol[hh]], length) for hh in hs]
        k = [conv(kwin[hh], cwk_ref[:, kcol[hh]], cbk_ref[:, kcol[hh]], length) * (A_DK ** -0.5) for hh in hs]
        vb = [_bf(v_ref[0, pl.ds(o, length), vcol[hh]]) for hh in hs]
        qb = [_bf(q[hh]) for hh in hs]
        k_w = [k[hh] * gt[hh][2] for hh in hs]
        qk = [_dot_nt(qb[hh], _bf(k[hh])) for hh in hs]
        c_loc = [_dot_tn(_bf(k_w[hh]), vb[hh]) for hh in hs]
        n_loc = [jnp.sum(k_w[hh], axis=0, keepdims=True) for hh in hs]
        return gt, q, qb, vb, qk, c_loc, n_loc

    def state_stage(o, length, staged, state):
        gt, q, qb, vb, qk, c_loc, n_loc = staged
        c_in, n_in, m_in = state
        q_c = [_dot(qb[hh], _bf(c_in[hh])) for hh in hs]
        s, a_t, m_t, q_n, c_out, n_out, m_out = [], [], [], [], [], [], []
        for hh in hs:
            b_c, b_end, _, m_loc, d, d_max = gt[hh]
            inter = b_c + m_in[hh]
            m_t.append(jnp.maximum(inter, d_max))
            s.append(qk[hh] * jnp.exp(d - m_t[hh]))
            a_t.append(jnp.exp(inter - m_t[hh]))
            m_new = jnp.maximum(b_end + m_in[hh], m_loc)
            a = jnp.exp(b_end + m_in[hh] - m_new)
            c = jnp.exp(m_loc - m_new)
            c_out.append(a * c_in[hh] + c * c_loc[hh])
            n_out.append(a * n_in[hh] + c * n_loc[hh])
            m_out.append(m_new)
            q_n.append(jnp.sum(q[hh] * n_in[hh], axis=1, keepdims=True))
        num = [_dot(_bf(s[hh]), vb[hh]) + a_t[hh] * q_c[hh] for hh in hs]
        for hh in hs:
            den = jnp.sum(s[hh], axis=1, keepdims=True) + a_t[hh] * q_n[hh]
            h = num[hh] / jnp.maximum(jnp.abs(den), jnp.exp(-m_t[hh]))
            hn = h * lax.rsqrt(jnp.mean(h * h, axis=-1, keepdims=True) + EPS) * hn_ref[:, vcol[hh]]
            y = _sigmoid(og_ref[0, pl.ds(o, length), vcol[hh]]) * hn
            o_ref[0, pl.ds(o, length), vcol[hh]] = y.astype(o_ref.dtype)
        return c_out, n_out, m_out

    def sweep(offsets, length, qwins, kwins):
        staged = [local_stage(o, length, qw, kw) for o, qw, kw in zip(offsets, qwins, kwins)]
        state = ([c_ref[hh] for hh in hs], [n_ref[hh] for hh in hs], [m_ref[hh] for hh in hs])
        for o, stg in zip(offsets, staged):
            state = state_stage(o, length, stg, state)
        for hh in hs:
            c_ref[hh] = state[0][hh]
            n_ref[hh] = state[1][hh]
            m_ref[hh] = state[2][hh]

    zeros8 = jnp.zeros((8, A_DK), F32)
    sweep([0], N_META,
          [[window_taps(jnp.concatenate([zeros8, q_ref[0, 0:N_META, kcol[hh]]], axis=0), N_META) for hh in hs]],
          [[window_taps(jnp.concatenate([zeros8, k_ref[0, 0:N_META, kcol[hh]]], axis=0), N_META) for hh in hs]])
    assert n_chunks % group == 0

    def body(c, carry):
        offsets = [pl.multiple_of(N_META + (c * group + j) * CHUNK, BF16_ROWS) for j in range(group)]

        starts = [pl.multiple_of(N_META - 8 + (c * group + j) * CHUNK, 8) for j in range(group)]

        def shifted(ref, w0, hh):
            return window_taps(ref[0, pl.ds(w0, CHUNK + 8), kcol[hh]], CHUNK)

        sweep(offsets, CHUNK,
              [[shifted(q_ref, w0, hh) for hh in hs] for w0 in starts],
              [[shifted(k_ref, w0, hh) for hh in hs] for w0 in starts])
        return carry

    lax.fori_loop(0, n_chunks // group, body, 0)


def _mlstm(z, gate_col, conv_w, conv_b, b_i, b_f, head_norm, heads=2):
    b, t, _ = z.shape
    hk = A_HEADS * A_DK
    wk, wv = heads * A_DK, heads * A_DV
    smem = pl.BlockSpec(memory_space=pltpu.SMEM)
    col = lambda width, off: (lambda bi, g: (bi, 0, off // width + g))
    return pl.pallas_call(
        functools.partial(_mlstm_kernel, heads=heads, group=4),
        grid=(b, A_HEADS // heads),
        in_specs=[
            smem, smem,
            pl.BlockSpec((1, t, wk), col(wk, 0)),
            pl.BlockSpec((1, t, wk), col(wk, hk)),
            pl.BlockSpec((1, t, wv), col(wv, 2 * hk)),
            pl.BlockSpec((1, t, wv), col(wv, 2 * hk + A_HEADS * A_DV)),
            pl.BlockSpec((1, t, MXU_DIM), lambda bi, g: (bi, 0, gate_col // MXU_DIM)),
            pl.BlockSpec((CONV_K, wk), lambda bi, g: (0, g)),
            pl.BlockSpec((CONV_K, wk), lambda bi, g: (0, A_HEADS // heads + g)),
            pl.BlockSpec((1, wk), lambda bi, g: (0, g)),
            pl.BlockSpec((1, wk), lambda bi, g: (0, A_HEADS // heads + g)),
            pl.BlockSpec((1, wv), lambda bi, g: (0, g)),
        ],
        out_specs=pl.BlockSpec((1, t, wv), lambda bi, g: (bi, 0, g)),
        out_shape=jax.ShapeDtypeStruct((b, t, A_HEADS * A_DV), BF16),
        scratch_shapes=[pltpu.VMEM((heads, A_DK, A_DV), F32), pltpu.VMEM((heads, 1, A_DK), F32),
                        pltpu.VMEM((heads, 1, 1), F32)],
        compiler_params=_params("parallel", "parallel"),
        name="mlstm",
    )(b_i, b_f, z, z, z, z, z, conv_w, conv_w, conv_b.reshape(1, -1), conv_b.reshape(1, -1),
      head_norm.reshape(1, -1))


def _gla_kernel(q_ref, k_ref, v_ref, og_ref, g_ref, p0_ref, p1_ref, p2_ref, hn_ref, o_ref, st_ref, *pre_ref,
                mode, heads, group):
    t_total = q_ref.shape[1]
    n_chunks = (t_total - N_META) // CHUNK
    dv, dk = st_ref.shape[1:]
    st_ref[...] = jnp.zeros_like(st_ref)

    hs = range(heads)
    kcol = [slice(hh * dk, (hh + 1) * dk) for hh in hs]
    vcol = [slice(hh * dv, (hh + 1) * dv) for hh in hs]

    if mode == "gla":
        gate = g_ref[0]
        gate_hi = _bf(gate)
        gate_lo = _bf(gate - gate_hi.astype(F32))
        for hh in hs:
            w = p0_ref[hh]
            w_hi = _bf(w)
            w_lo = _bf(w - w_hi.astype(F32))
            both = _dot(gate_hi, jnp.concatenate([w_hi, w_lo], axis=1))
            pre_ref[0][hh] = (both[:, :dk] + both[:, dk:]) + _dot(gate_lo, w_hi) + p1_ref[:, kcol[hh]]

    def gate_inputs(hh, o, length):
        q = q_ref[0, pl.ds(o, length), kcol[hh]]
        if mode == "gla":
            pre = pre_ref[0][hh, pl.ds(o, length), :]
            return q * (dk ** -0.5), k_ref[0, pl.ds(o, length), kcol[hh]], _log_sigmoid(pre) / GATE_TAU
        fpre = g_ref[0, pl.ds(o, length), kcol[hh]]
        a = p0_ref[:, kcol[hh]]
        bb = p1_ref[:, kcol[hh]] + _log_sigmoid(fpre)
        lg = jnp.maximum(a, bb) + jnp.log1p(jnp.exp(-jnp.abs(a - bb)))
        return q, p2_ref[:, kcol[hh]] * _sigmoid(-fpre), lg

    def cumsum_time(tri, lg):
        parts = _dot(tri, jnp.concatenate(_split3(lg), axis=1))
        return parts[:, :dk] + (parts[:, dk:2 * dk] + parts[:, 2 * dk:])

    def local_stage(o, length):
        row = lax.broadcasted_iota(jnp.int32, (length, length), 0)
        col = lax.broadcasted_iota(jnp.int32, (length, length), 1)
        causal = col <= row
        tri = jnp.where(causal, 1.0, 0.0).astype(BF16)
        qkl = [gate_inputs(hh, o, length) for hh in hs]
        vb = [_bf(v_ref[0, pl.ds(o, length), vcol[hh]]) for hh in hs]
        g = [cumsum_time(tri, qkl[hh][2]) for hh in hs]
        g_end = [g[hh][length - 1:length, :] for hh in hs]
        g_mid = [g[hh][length // 2:length // 2 + 1, :] for hh in hs]
        s = [_dot_nt(_bf(qkl[hh][0] * jnp.exp(g[hh] - g_mid[hh])), _bf(qkl[hh][1] * jnp.exp(g_mid[hh] - g[hh])))
             for hh in hs]
        q_dec = [_bf(qkl[hh][0] * jnp.exp(g[hh])) for hh in hs]
        local = [_dot_tn(vb[hh], _bf(qkl[hh][1] * jnp.exp(g_end[hh] - g[hh]))) for hh in hs]
        intra = [_dot(_bf(jnp.where(causal, s[hh], 0.0)), vb[hh]) for hh in hs]
        return q_dec, [jnp.exp(ge) for ge in g_end], local, intra

    def state_stage(o, length, staged, st_in):
        q_dec, decay, local, intra = staged
        inter = [_dot_nt(q_dec[hh], _bf(st_in[hh])) for hh in hs]
        st_out = [st_in[hh] * decay[hh] + local[hh] for hh in hs]
        for hh in hs:
            out = intra[hh] + inter[hh]
            hn = out * lax.rsqrt(jnp.mean(out * out, axis=-1, keepdims=True) + EPS) * hn_ref[:, vcol[hh]]
            og = og_ref[0, pl.ds(o, length), vcol[hh]]
            gate = _silu(og) if mode == "gla" else _sigmoid(og)
            o_ref[0, pl.ds(o, length), vcol[hh]] = (gate * hn).astype(o_ref.dtype)
        return st_out

    def sweep(offsets, length):
        staged = [local_stage(o, length) for o in offsets]
        st = [st_ref[hh] for hh in hs]
        for o, stg in zip(offsets, staged):
            st = state_stage(o, length, stg, st)
        for hh in hs:
            st_ref[hh] = st[hh]

    sweep([0], N_META)
    assert n_chunks % group == 0

    def body(c, carry):
        o = pl.multiple_of(N_META + c * (group * CHUNK), BF16_ROWS)
        sweep([pl.multiple_of(o + j * CHUNK, BF16_ROWS) for j in range(group)], CHUNK)
        return carry

    lax.fori_loop(0, n_chunks // group, body, 0)


def _gla_call(z, n_heads, heads, dk, dv, blocks, gate_width, params, head_norm, mode):
    b, t, _ = z.shape
    q0, k0, v0, og0, g0 = blocks
    zspec = lambda width, off, grouped=True: pl.BlockSpec(
        (1, t, width), (lambda bi, g: (bi, 0, off // width + (g if grouped else 0))))
    (p0, s0), (p1, s1), (p2, s2) = params
    gate_spec = zspec(gate_width, g0, grouped=False) if mode == "gla" else zspec(heads * dk, g0)
    return pl.pallas_call(
        functools.partial(_gla_kernel, mode=mode, heads=heads, group=8),
        grid=(b, n_heads // heads),
        in_specs=[
            zspec(heads * dk, q0), zspec(heads * dk, k0), zspec(heads * dv, v0), zspec(heads * dv, og0),
            gate_spec, s0, s1, s2,
            pl.BlockSpec((1, heads * dv), lambda bi, g: (0, g)),
        ],
        out_specs=pl.BlockSpec((1, t, heads * dv), lambda bi, g: (bi, 0, g)),
        out_shape=jax.ShapeDtypeStruct((b, t, n_heads * dv), BF16),
        scratch_shapes=[pltpu.VMEM((heads, dv, dk), F32)] + (
            [pltpu.VMEM((heads, t, dk), F32)] if mode == "gla" else []),
        compiler_params=_params("parallel", "parallel"),
        name="gla_" + mode,
    )(z, z, z, z, z, p0, p1, p2, head_norm.reshape(1, -1))


def _mla_kernel(cq_ref, ckv_ref, kr_ref, tab_ref, gqa_ref, gkva_ref, wqn_ref, wqr_ref, wkn_ref, wv_ref,
                gqn_ref, gqr_ref, gkn_ref, gkr_ref, o_ref, qf_ref, kf_ref, vf_ref, *, heads):
    t_total = cq_ref.shape[1]
    n_blocks = (ATT_PAD + t_total) // ATT_BLOCK
    dqk = D_NOPE + D_ROPE
    scale = dqk ** -0.5
    rows = _row_tile(t_total, PREP_ROWS)

    hs = range(heads)
    col = [slice(hh * LANES, (hh + 1) * LANES) for hh in hs]
    for hh in hs:
        qf_ref[hh, 0:ATT_PAD, :] = jnp.zeros((ATT_PAD, 2 * LANES), BF16)
        kf_ref[hh, 0:ATT_PAD, :] = jnp.zeros((ATT_PAD, 2 * LANES), BF16)
        vf_ref[hh, 0:ATT_PAD, :] = jnp.zeros((ATT_PAD, D_V), BF16)

    def rope_pair(x, gains, tab):
        p = x * gains * tab
        return p + pltpu.roll(p, D_ROPE, 1)

    def latent_norm(ref, gain_ref, r0):
        u = ref[0, pl.ds(r0, rows), :]
        return _bf(u * lax.rsqrt(jnp.mean(u * u, axis=-1, keepdims=True) + EPS) * gain_ref[...])

    def up_project(c):
        r0 = c * rows
        cq = latent_norm(cq_ref, gqa_ref, r0)
        ckv = latent_norm(ckv_ref, gkva_ref, r0)
        return _dot(cq, wqn_ref[...]), _dot(cq, wqr_ref[...]), _dot(ckv, wkn_ref[...]), _dot(ckv, wv_ref[...])

    def prep(c, hh, projected):
        r0 = c * rows
        dst = ATT_PAD + c * rows
        tab = tab_ref[pl.ds(r0, rows), :]
        qn = projected[0][:, col[hh]]
        qr = projected[1][:, col[hh]]
        ssq = jnp.sum(qn * qn + 0.5 * (qr * qr), axis=-1, keepdims=True)
        rq = lax.rsqrt(ssq / dqk + EPS) * scale
        qf_ref[hh, pl.ds(dst, rows), 0:LANES] = _bf(qn * gqn_ref[...] * rq)
        qf_ref[hh, pl.ds(dst, rows), LANES:2 * LANES] = _bf(qr * gqr_ref[...] * tab * rq)
        kn = projected[2][:, col[hh]]
        kr = kr_ref[0, pl.ds(r0, rows), :]
        ssk = jnp.sum(kn * kn + 0.5 * (kr * kr), axis=-1, keepdims=True)
        rk = lax.rsqrt(ssk / dqk + EPS)
        kf_ref[hh, pl.ds(dst, rows), 0:LANES] = _bf(kn * gkn_ref[...] * rk)
        kf_ref[hh, pl.ds(dst, rows), LANES:2 * LANES] = _bf(rope_pair(kr, gkr_ref[...], tab) * rk)
        vf_ref[hh, pl.ds(dst, rows), :] = _bf(projected[3][:, col[hh]])

    for c in range(t_total // rows):
        projected = up_project(c)
        for hh in hs:
            prep(c, hh, projected)

    qpos = lax.broadcasted_iota(jnp.int32, (ATT_BLOCK, ATT_BLOCK), 0)
    kpos = lax.broadcasted_iota(jnp.int32, (ATT_BLOCK, ATT_BLOCK), 1)
    neg = -jnp.inf

    def scores(item):
        qi, hh = item
        q = qf_ref[hh, qi * ATT_BLOCK:(qi + 1) * ATT_BLOCK, :]
        return _dot_nt(q, kf_ref[hh, 0:(qi + 1) * ATT_BLOCK, :])

    items = [(qi, hh) for qi in range(n_blocks) for hh in hs]
    s_next = scores(items[0])
    for idx, (qi, hh) in enumerate(items):
        s = s_next
        if idx + 1 < len(items):
            s_next = scores(items[idx + 1])
        parts = [s[:, j * ATT_BLOCK:(j + 1) * ATT_BLOCK] for j in range(qi + 1)]
        parts[0] = jnp.where(kpos >= ATT_PAD, parts[0], neg)
        parts[qi] = jnp.where(kpos <= qpos, parts[qi], neg)
        top = functools.reduce(jnp.maximum, parts)
        m = jnp.max(top, axis=-1, keepdims=True)
        if qi == 0:
            m = jnp.where(m == neg, 0.0, m)
        probs = [jnp.exp(part - m) for part in parts]
        l = jnp.sum(functools.reduce(jnp.add, probs), axis=-1, keepdims=True)
        pv = _dot(jnp.concatenate([_bf(pr) for pr in probs], axis=1), vf_ref[hh, 0:(qi + 1) * ATT_BLOCK, :])
        if qi == 0:
            out = pv / jnp.where(l == 0.0, 1.0, l)
            o_ref[0, 0:N_META, col[hh]] = out[ATT_PAD:, :].astype(o_ref.dtype)
        else:
            dst = qi * ATT_BLOCK - ATT_PAD
            o_ref[0, dst:dst + ATT_BLOCK, col[hh]] = (pv / l).astype(o_ref.dtype)


def _mla(z, cq_col, tab, q_a_norm, kv_a_norm, wq, wkv, gqn, gqr, gkn, gkr):
    b, t, _ = z.shape
    assert (ATT_PAD + t) % ATT_BLOCK == 0 and t % BF16_ROWS == 0
    tp = ATT_PAD + t
    heads = 2
    groups = D_HEADS // heads
    ckv_col = cq_col + Q_LORA
    kr_col = ckv_col + KV_LORA
    wspec = lambda k, off: pl.BlockSpec((k, heads * LANES), lambda bi, g: (0, off + g))
    gspec = lambda width: pl.BlockSpec((1, width), lambda bi, g: (0, 0))
    return pl.pallas_call(
        functools.partial(_mla_kernel, heads=heads),
        grid=(b, groups),
        in_specs=[
            pl.BlockSpec((1, t, Q_LORA), lambda bi, g: (bi, 0, cq_col // Q_LORA)),
            pl.BlockSpec((1, t, KV_LORA), lambda bi, g: (bi, 0, ckv_col // KV_LORA)),
            pl.BlockSpec((1, t, LANES), lambda bi, g: (bi, 0, kr_col // LANES)),
            pl.BlockSpec((t, LANES), lambda bi, g: (0, 0)),
            gspec(Q_LORA), gspec(KV_LORA),
            wspec(Q_LORA, 0), wspec(Q_LORA, groups), wspec(KV_LORA, 0), wspec(KV_LORA, groups),
            gspec(LANES), gspec(LANES), gspec(LANES), gspec(LANES),
        ],
        out_specs=pl.BlockSpec((1, t, heads * D_V), lambda bi, g: (bi, 0, g)),
        out_shape=jax.ShapeDtypeStruct((b, t, D_HEADS * D_V), BF16),
        scratch_shapes=[pltpu.VMEM((heads, tp, 2 * LANES), BF16), pltpu.VMEM((heads, tp, 2 * LANES), BF16),
                        pltpu.VMEM((heads, tp, D_V), BF16)],
        compiler_params=_params("parallel", "parallel"),
        name="mla",
    )(z, z, z, tab, q_a_norm.reshape(1, -1), kv_a_norm.reshape(1, -1), wq, wq, wkv, wkv, gqn, gqr, gkn, gkr)


def _pack_bf16_pairs(v):
    w = v.shape[1] // 2
    bits = pltpu.bitcast(_bf(v).astype(F32), jnp.uint32)
    return (bits[:, :w] >> 16) | (bits[:, w:] & jnp.uint32(0xFFFF0000))


def _unpack_lo(words):
    return pltpu.bitcast(words << 16, F32)


def _unpack_hi(words):
    return pltpu.bitcast(words & jnp.uint32(0xFFFF0000), F32)


def _router_kernel(x_ref, g_ref, w_ref, b_ref, gate_ref, idx_ref, xg_ref, cnt_ref, carry_ref):
    tm = x_ref.shape[0]

    @pl.when(pl.program_id(0) == 0)
    def _():
        carry_ref[...] = jnp.zeros_like(carry_ref)

    x = x_ref[...]
    ms = jnp.mean(x * x, axis=-1, keepdims=True)
    xn = x * lax.rsqrt(ms + EPS) * g_ref[...]
    xh = _bf(xn)
    xl = _bf(xn - xh.astype(F32))
    w = w_ref[...]
    wh = _bf(w)
    wl = _bf(w - wh.astype(F32))
    both = _dot(xh, jnp.concatenate([wh, wl], axis=1))
    logits = (both[:, :LANES] + both[:, LANES:]) + _dot(xl, wh) + b_ref[...]
    lane = lax.broadcasted_iota(jnp.int32, logits.shape, 1)
    lane_f = lane.astype(F32)
    neg = -jnp.inf
    big = float(LANES)

    is_group = lane < N_GROUPS
    g_max = jnp.max(jnp.where(is_group, logits, neg), axis=-1, keepdims=True)
    g_sum = jnp.sum(jnp.where(is_group, jnp.exp(logits - g_max), 0.0), axis=-1, keepdims=True)
    p_top = 1.0 / g_sum
    grp = jnp.min(jnp.where(is_group & (logits == g_max), lane_f, big), axis=-1, keepdims=True)

    e_lo = N_GROUPS + grp * EXPERTS_PER_GROUP
    in_grp = (lane_f >= e_lo) & (lane_f < e_lo + EXPERTS_PER_GROUP)
    e_max = jnp.max(jnp.where(in_grp, logits, neg), axis=-1, keepdims=True)
    e_sum = jnp.sum(jnp.where(in_grp, jnp.exp(logits - e_max), 0.0), axis=-1, keepdims=True)
    i1 = jnp.min(jnp.where(in_grp & (logits == e_max), lane_f, big), axis=-1, keepdims=True)
    rest = in_grp & (lane_f != i1)
    e_2nd = jnp.max(jnp.where(rest, logits, neg), axis=-1, keepdims=True)
    i2 = jnp.min(jnp.where(rest & (logits == e_2nd), lane_f, big), axis=-1, keepdims=True)
    p1 = 1.0 / e_sum
    p2 = jnp.exp(e_2nd - e_max) / e_sum
    tot = p1 + p2
    gate_ref[...] = jnp.where(lane == 0, p_top * p1 / tot, jnp.where(lane == 1, p_top * p2 / tot, 0.0))

    e1 = i1 - N_GROUPS
    e2 = i2 - N_GROUPS
    hot = jnp.where((lane_f == e1) | (lane_f == e2), 1.0, 0.0)
    row = lax.broadcasted_iota(jnp.int32, (tm, tm), 0)
    col = lax.broadcasted_iota(jnp.int32, (tm, tm), 1)
    before = _dot(jnp.where(col < row, 1.0, 0.0).astype(BF16), _bf(hot)) + carry_ref[...]
    r1 = jnp.sum(jnp.where(lane_f == e1, before, 0.0), axis=-1, keepdims=True)
    r2 = jnp.sum(jnp.where(lane_f == e2, before, 0.0), axis=-1, keepdims=True)
    total = carry_ref[...] + jnp.sum(hot, axis=0, keepdims=True)
    carry_ref[...] = total
    cnt_ref[...] = jnp.broadcast_to(total, cnt_ref.shape).astype(jnp.int32)
    idx_ref[...] = jnp.where(lane == 0, e1, jnp.where(lane == 1, e2, jnp.where(lane == 2, r1, jnp.where(
        lane == 3, r2, 0.0)))).astype(jnp.int32)

    words = _pack_bf16_pairs(xn)
    for s in range(8):
        xg_ref[pl.ds(s, tm, stride=8), :] = words[:, s * LANES:(s + 1) * LANES]


def _router(x2d, gain, w_group, b_group, w_expert, b_expert):
    m, d = x2d.shape
    assert d == 2 * 8 * LANES
    tm = _row_tile(m, ROW_TILE)
    pad = LANES - N_GROUPS - N_EXPERTS
    w = jnp.concatenate([w_group, w_expert, jnp.zeros((d, pad), F32)], axis=1)
    bias = jnp.concatenate([b_group, b_expert, jnp.zeros((pad,), F32)]).reshape(1, LANES)
    return pl.pallas_call(
        _router_kernel,
        grid=(m // tm,),
        in_specs=[
            pl.BlockSpec((tm, d), lambda i: (i, 0)),
            pl.BlockSpec((1, d), lambda i: (0, 0)),
            pl.BlockSpec((d, LANES), lambda i: (0, 0)),
            pl.BlockSpec((1, LANES), lambda i: (0, 0)),
        ],
        out_specs=[pl.BlockSpec((tm, LANES), lambda i: (i, 0)), pl.BlockSpec((tm, LANES), lambda i: (i, 0)),
                   pl.BlockSpec((tm * 8, LANES), lambda i: (i, 0)), pl.BlockSpec((8, LANES), lambda i: (0, 0))],
        out_shape=[jax.ShapeDtypeStruct((m, LANES), F32), jax.ShapeDtypeStruct((m, LANES), jnp.int32),
                   jax.ShapeDtypeStruct((m * 8, LANES), jnp.uint32), jax.ShapeDtypeStruct((8, LANES), jnp.int32)],
        scratch_shapes=[pltpu.VMEM((1, LANES), F32)],
        compiler_params=_params("arbitrary"),
        name="router",
    )(x2d, gain.reshape(1, d), w, bias)


def _invert_kernel(dest_ref, inv_ref):
    def clear(s, carry):
        inv_ref[s] = -1
        return carry

    lax.fori_loop(0, inv_ref.shape[0], clear, 0, unroll=8)

    def put(f, carry):
        inv_ref[dest_ref[f]] = f
        return carry

    lax.fori_loop(0, dest_ref.shape[0], put, 0, unroll=8)


def _invert(dest, p):
    assert p % 8 == 0 and dest.shape[0] % 8 == 0
    smem = pl.BlockSpec(memory_space=pltpu.SMEM)
    return pl.pallas_call(
        _invert_kernel, in_specs=[smem], out_specs=smem,
        out_shape=jax.ShapeDtypeStruct((p,), jnp.int32), name="moe_invert",
    )(dest)


def _expert_kernel(be_ref, nxt_ref, run_ref, nu_ref, src_ref, dst_ref, xg_hbm, w1_hbm, w3_hbm, w2_hbm, o_hbm,
                   xbuf, ybuf, w1f, w3f, w2f, w1s, w3s, w2s, sem_in, sem_out, sem_w, *, layer):
    i = pl.program_id(0)
    n_used = nu_ref[0]
    par = i % 2
    half_e = D_EXPERT // 2

    def weight_copies(expert, slot):
        return [pltpu.make_async_copy(w_hbm.at[layer, expert], w_f.at[slot], sem_w.at[slot, j])
                for j, (w_hbm, w_f) in enumerate(((w1_hbm, w1f), (w3_hbm, w3f), (w2_hbm, w2f)))]

    def gather(block, r, slot):
        src = pl.multiple_of(src_ref[block * MOE_BLOCK + r], 8)
        return pltpu.make_async_copy(xg_hbm.at[pl.ds(src, 8), :], xbuf.at[slot, pl.ds(r * 8, 8), :], sem_in.at[slot])

    def scatter(block, r, slot):
        dst = pl.multiple_of(dst_ref[(block + 1) * MOE_BLOCK + r], 8)
        return pltpu.make_async_copy(ybuf.at[slot, pl.ds(r * 8, 8), :], o_hbm.at[pl.ds(dst, 8), :], sem_out.at[slot])

    def wait_gathers(slot):
        pltpu.make_async_copy(xbuf.at[1 - slot], xbuf.at[slot], sem_in.at[slot]).wait()

    def wait_scatters(slot):
        pltpu.make_async_copy(ybuf.at[slot], ybuf.at[1 - slot], sem_out.at[slot]).wait()

    @pl.when(i == 0)
    def _():
        ybuf[1] = jnp.zeros(ybuf.shape[1:], ybuf.dtype)
        n_real = o_hbm.shape[0] - 2 * MOE_BLOCK * 8
        fill1 = pltpu.make_async_copy(ybuf.at[1], o_hbm.at[pl.ds(n_real + MOE_BLOCK * 8, MOE_BLOCK * 8), :],
                                      sem_out.at[1])
        fill1.start()
        fill1.wait()
        pltpu.make_async_copy(ybuf.at[1], o_hbm.at[pl.ds(n_real, MOE_BLOCK * 8), :], sem_out.at[0]).start()

        def first(r, carry):
            gather(0, r, 0).start()
            return carry

        lax.fori_loop(0, MOE_BLOCK, first, 0, unroll=8)

    @pl.when(i < n_used)
    def _():
        expert = be_ref[i]
        wslot = run_ref[i] % 2

        @pl.when(i == 0)
        def _():
            for cp in weight_copies(expert, 0):
                cp.start()

        @pl.when((i == 0) | (expert != be_ref[jnp.maximum(i - 1, 0)]))
        def _():
            for cp in weight_copies(expert, wslot):
                cp.wait()

            @pl.when(nxt_ref[i] != expert)
            def _():
                for cp in weight_copies(nxt_ref[i], 1 - wslot):
                    cp.start()

            w1s[...] = _bf(w1f[wslot])
            w3s[...] = _bf(w3f[wslot])
            w2s[...] = _bf(w2f[wslot])

        wait_gathers(par)

        def move_rows(group, n_groups=6):
            lo, hi = group * MOE_BLOCK // n_groups, (group + 1) * MOE_BLOCK // n_groups
            for r in range(lo, hi):
                gather(i + 1, r, 1 - par).start(priority=r % 2)
                scatter(i - 1, r, 1 - par).start(priority=(r + 1) % 2)

        tiles = [xbuf[par, pl.ds(s, MOE_BLOCK, stride=8), :] for s in range(8)]
        xb = jnp.concatenate([_bf(_unpack_lo(w)) for w in tiles] + [_bf(_unpack_hi(w)) for w in tiles], axis=1)
        move_rows(0)
        h1a = _dot(xb, w1s[:, :half_e])
        move_rows(1)
        h3a = _dot(xb, w3s[:, :half_e])
        move_rows(2)
        act_a = _bf(_silu(h1a) * h3a)
        h1b = _dot(xb, w1s[:, half_e:])
        move_rows(3)
        h3b = _dot(xb, w3s[:, half_e:])
        move_rows(4)
        act_b = _bf(_silu(h1b) * h3b)
        y = _dot(act_a, w2s[:half_e, :])
        move_rows(5)
        y = y + _dot(act_b, w2s[half_e:, :])

        wait_scatters(par)
        words = _pack_bf16_pairs(y)
        for s in range(8):
            ybuf[par, pl.ds(s, MOE_BLOCK, stride=8), :] = words[:, s * LANES:(s + 1) * LANES]

        @pl.when(i == n_used - 1)
        def _():
            def last(r, carry):
                scatter(i, r, par).start()
                return carry

            lax.fori_loop(0, MOE_BLOCK, last, 0, unroll=8)
            wait_scatters(par)
            wait_scatters(1 - par)
            wait_gathers(1 - par)


def _moe(x2d, gain, w_group, b_group, w_expert, b_expert, w1, w3, w2, layer, final_shape=None):
    n, d = x2d.shape
    gates_l, idx_l, xg, cnt = _router(x2d, gain, w_group, b_group, w_expert, b_expert)

    a = n * TOP_K
    n_blocks = -(-a // MOE_BLOCK) + N_EXPERTS
    p = n_blocks * MOE_BLOCK
    counts = cnt[0, :N_EXPERTS]
    padded = (counts + MOE_BLOCK - 1) // MOE_BLOCK * MOE_BLOCK
    pad_end = jnp.cumsum(padded)
    pad_start = pad_end - padded
    e_hot = idx_l[:, :TOP_K, None] == jnp.arange(N_EXPERTS, dtype=jnp.int32)
    dest = jnp.sum(jnp.where(e_hot, pad_start, 0), axis=-1) + idx_l[:, TOP_K:2 * TOP_K]
    dest = jnp.clip(dest.reshape(-1), 0, p - 1).astype(jnp.int32)
    blk0 = jnp.arange(n_blocks, dtype=jnp.int32) * MOE_BLOCK
    block_expert = jnp.minimum(jnp.searchsorted(pad_end, blk0, side="right"), N_EXPERTS - 1).astype(jnp.int32)
    n_used = (pad_end[-1] // MOE_BLOCK).astype(jnp.int32).reshape(1)
    experts = jnp.arange(N_EXPERTS, dtype=jnp.int32)
    later = jnp.where((experts[None, :] > experts[:, None]) & (counts[None, :] > 0), experts[None, :], N_EXPERTS)
    next_owner = jnp.min(later, axis=1)
    next_expert = jnp.where(next_owner < N_EXPERTS, next_owner, experts)[block_expert].astype(jnp.int32)
    run_index = (jnp.cumsum(jnp.concatenate([jnp.ones((1,), jnp.int32),
                                             (block_expert[1:] != block_expert[:-1]).astype(jnp.int32)])) - 1
                 ).astype(jnp.int32)
    codes = _invert(dest, p)
    slot = jnp.arange(p, dtype=jnp.int32)
    spare = TOP_K * n + (slot // MOE_BLOCK % 2) * MOE_BLOCK + slot % MOE_BLOCK
    src_tok = (jnp.maximum(codes, 0) >> 1) * 8
    dst_row = jnp.where(codes >= 0, (codes & 1) * n + (codes >> 1), spare) * 8
    lead = (TOP_K * n + MOE_BLOCK + jnp.arange(MOE_BLOCK, dtype=jnp.int32)) * 8
    dst_row = jnp.concatenate([lead, dst_row])

    hbm = pl.BlockSpec(memory_space=pl.ANY)
    out_rows = TOP_K * n + 2 * MOE_BLOCK
    out2 = pl.pallas_call(
        functools.partial(_expert_kernel, layer=layer),
        grid_spec=pltpu.PrefetchScalarGridSpec(
            num_scalar_prefetch=6,
            grid=(n_blocks,),
            in_specs=[hbm, hbm, hbm, hbm],
            out_specs=hbm,
            scratch_shapes=[
                pltpu.VMEM((2, MOE_BLOCK * 8, LANES), jnp.uint32), pltpu.VMEM((2, MOE_BLOCK * 8, LANES), jnp.uint32),
                pltpu.VMEM((2, d, D_EXPERT), F32), pltpu.VMEM((2, d, D_EXPERT), F32), pltpu.VMEM((2, D_EXPERT, d), F32),
                pltpu.VMEM((d, D_EXPERT), BF16), pltpu.VMEM((d, D_EXPERT), BF16), pltpu.VMEM((D_EXPERT, d), BF16),
                pltpu.SemaphoreType.DMA((2,)), pltpu.SemaphoreType.DMA((2,)), pltpu.SemaphoreType.DMA((2, 3)),
            ],
        ),
        out_shape=jax.ShapeDtypeStruct((out_rows * 8, LANES), jnp.uint32),
        compiler_params=_params("arbitrary"),
        name="moe_experts",
    )(block_expert, next_expert, run_index, n_used, src_tok, dst_row, xg, w1, w3, w2)
    return _combine(x2d, gates_l, out2, final_shape)


def _combine_tile(x_ref, gate_ref, a_ref, b_ref, o_ref):
    tm = x_ref.shape[0]
    half = x_ref.shape[1] // 2
    g0 = gate_ref[:, 0:1]
    g1 = gate_ref[:, 1:2]
    for s in range(8):
        wa = a_ref[pl.ds(s, tm, stride=8), :]
        wb = b_ref[pl.ds(s, tm, stride=8), :]
        lo = slice(s * LANES, (s + 1) * LANES)
        hi = slice(half + s * LANES, half + (s + 1) * LANES)
        o_ref[:, lo] = x_ref[:, lo] + (g0 * _unpack_lo(wa) + g1 * _unpack_lo(wb))
        o_ref[:, hi] = x_ref[:, hi] + (g0 * _unpack_hi(wa) + g1 * _unpack_hi(wb))


def _combine_kernel(x_ref, gate_ref, a_ref, b_ref, o_ref):
    _combine_tile(x_ref, gate_ref, a_ref, b_ref, o_ref)


def _combine_final_kernel(x_ref, gate_ref, a_ref, b_ref, o_hbm, obuf, sem, *, tiles_per_batch):
    i = pl.program_id(0)
    n_steps = pl.num_programs(0)
    tm = x_ref.shape[0]

    def copies(step, slot):
        batch, j = step // tiles_per_batch, step % tiles_per_batch
        row = pl.multiple_of(j * tm, 8)
        main = pltpu.make_async_copy(obuf.at[slot, pl.ds(N_META, tm - N_META), :],
                                     o_hbm.at[batch, pl.ds(row, tm - N_META), :], sem.at[slot, 0])
        head = pltpu.make_async_copy(obuf.at[slot, pl.ds(0, N_META), :],
                                     o_hbm.at[batch, pl.ds(pl.multiple_of(jnp.maximum(row - N_META, 0), 8), N_META), :],
                                     sem.at[slot, 1])
        return main, head, j > 0

    def wait_step(step, slot):
        main, head, has_head = copies(step, slot)
        main.wait()

        @pl.when(has_head)
        def _():
            head.wait()

    slot = i % 2

    @pl.when(i >= 2)
    def _():
        wait_step(i - 2, slot)

    _combine_tile(x_ref, gate_ref, a_ref, b_ref, obuf.at[slot])
    main, head, has_head = copies(i, slot)
    main.start()

    @pl.when(has_head)
    def _():
        head.start()

    @pl.when(i == n_steps - 1)
    def _():
        wait_step(i, slot)

        @pl.when(i >= 1)
        def _():
            wait_step(i - 1, 1 - slot)


def _combine(x2d, gates, out2, final_shape=None):
    n, d = x2d.shape
    tm = _row_tile(n, ROW_TILE)
    in_specs = [
        pl.BlockSpec((tm, d), lambda i: (i, 0)),
        pl.BlockSpec((tm, LANES), lambda i: (i, 0)),
        pl.BlockSpec((tm * 8, LANES), lambda i: (i, 0)),
        pl.BlockSpec((tm * 8, LANES), lambda i: (n // tm + i, 0)),
    ]
    if final_shape is None:
        return pl.pallas_call(
            _combine_kernel,
            grid=(n // tm,),
            in_specs=in_specs,
            out_specs=pl.BlockSpec((tm, d), lambda i: (i, 0)),
            out_shape=jax.ShapeDtypeStruct((n, d), F32),
            compiler_params=_params("parallel"),
            name="moe_combine",
        )(x2d, gates, out2, out2)
    b, t = final_shape
    assert t % tm == 0 and tm > N_META
    return pl.pallas_call(
        functools.partial(_combine_final_kernel, tiles_per_batch=t // tm),
        grid=(n // tm,),
        in_specs=in_specs,
        out_specs=pl.BlockSpec(memory_space=pl.ANY),
        out_shape=jax.ShapeDtypeStruct((b, t - N_META, d), F32),
        scratch_shapes=[pltpu.VMEM((2, tm, d), F32), pltpu.SemaphoreType.DMA((2, 2))],
        compiler_params=_params("arbitrary"),
        name="moe_combine_final",
    )(x2d, gates, out2, out2)


def _even_layer(x, norm_g, w_in, conv_w, conv_b, b_i, b_f, a_norm, w_gate2, b_gate, b_norm, w_out):
    b, t, d = x.shape
    n = b * t
    a_w = 2 * A_HEADS * A_DK + 2 * A_HEADS * A_DV
    g_w = 2 * A_HEADS
    b_w = 2 * B_HEADS * B_DK + 2 * B_HEADS * B_DV
    main = a_w + b_w
    w = _even_weight(w_in, a_w, g_w, b_w, GATE_RANK)
    z = _normproj(x.reshape(n, d), norm_g, w).reshape(b, t, main + MXU_DIM)

    ya = _mlstm(z, main, conv_w, conv_b, b_i, b_f, a_norm)

    wg = jnp.zeros((B_HEADS, MXU_DIM, B_DK), F32).at[:, g_w:g_w + GATE_RANK, :].set(
        w_gate2.reshape(GATE_RANK, B_HEADS, B_DK).transpose(1, 0, 2))
    dummy = jnp.zeros((1, B_HEADS * B_DK), F32)
    hp = 2
    hspec = pl.BlockSpec((1, hp * B_DK), lambda bi, g: (0, g))
    yb = _gla_call(
        z, B_HEADS, hp, B_DK, B_DV,
        (a_w, a_w + B_HEADS * B_DK, a_w + 2 * B_HEADS * B_DK, a_w + 2 * B_HEADS * B_DK + B_HEADS * B_DV, main),
        MXU_DIM,
        ((wg, pl.BlockSpec((hp, MXU_DIM, B_DK), lambda bi, g: (g, 0, 0))),
         (b_gate.reshape(1, -1), hspec), (dummy, hspec)),
        b_norm, "gla")
    return _outproj(ya.reshape(n, -1), yb.reshape(n, -1), w_out.astype(BF16), x.reshape(n, d)).reshape(b, t, d)


def _odd_layer(x, lb, norm_g, w_in, c_norm, q_a_norm, w_q_up, kv_a_norm, w_kv_up, q_norm, k_norm, w_out):
    b, t, d = x.shape
    n = b * t
    c_w = 2 * C_HEADS * C_DK + 2 * C_HEADS * C_DV
    swap = (jnp.arange(D_ROPE) + D_ROPE // 2) % D_ROPE
    kr0 = c_w + Q_LORA + KV_LORA
    used = kr0 + 2 * D_ROPE
    total = -(-used // MXU_DIM) * MXU_DIM
    z2 = _normproj(x.reshape(n, d), norm_g, _odd_weight(w_in, kr0))
    z = z2.reshape(b, t, total)

    hp = 4
    hspec = pl.BlockSpec((1, hp * C_DK), lambda bi, g: (0, g))
    yc = _gla_call(
        z, C_HEADS, hp, C_DK, C_DV,
        (0, C_HEADS * C_DK, 2 * C_HEADS * C_DK, 2 * C_HEADS * C_DK + C_HEADS * C_DV, C_HEADS * C_DK),
        C_DK,
        ((jnp.log(lb).reshape(1, -1), hspec), (jnp.log1p(-lb).reshape(1, -1), hspec), ((1.0 - lb).reshape(1, -1), hspec)),
        c_norm, "hgrn")

    dq = D_NOPE + D_ROPE
    wq = w_q_up.reshape(Q_LORA, D_HEADS, dq)
    wq_rope = wq[:, :, D_NOPE:]
    wq_p = jnp.concatenate([wq[:, :, :D_NOPE].reshape(Q_LORA, -1),
                            jnp.concatenate([wq_rope, wq_rope[:, :, swap]], axis=-1).reshape(Q_LORA, -1)],
                           axis=1).astype(BF16)
    wkv = w_kv_up.reshape(KV_LORA, D_HEADS, D_NOPE + D_V)
    wkv_p = jnp.concatenate([wkv[:, :, :D_NOPE].reshape(KV_LORA, -1), wkv[:, :, D_NOPE:].reshape(KV_LORA, -1)],
                            axis=1).astype(BF16)
    pos = jnp.arange(t, dtype=F32)
    half = D_ROPE // 2
    inv = ROPE_THETA ** (-jnp.arange(half, dtype=F32) / half)
    ang = pos[:, None] * inv[None, :]
    cos, sin = jnp.cos(ang), jnp.sin(ang)
    tab = jnp.concatenate([cos, cos, -sin, sin], axis=1)
    pair = lambda g: jnp.concatenate([g[D_NOPE:], g[D_NOPE:][swap]]).reshape(1, LANES)
    yd = _mla(z, c_w, tab, q_a_norm, kv_a_norm, wq_p, wkv_p, q_norm[:D_NOPE].reshape(1, LANES), pair(q_norm),
              k_norm[:D_NOPE].reshape(1, LANES), pair(k_norm))
    return _outproj(yc.reshape(n, -1), yd.reshape(n, -1), w_out.astype(BF16), x.reshape(n, d)).reshape(b, t, d)


def kernel(x, meta_tokens, ab_norm, ab_w_in, a_conv_w, a_conv_b, a_b_i, a_b_f, a_head_norm, b_w_gate2, b_b_gate, b_head_norm, ab_w_out, cd_norm, cd_w_in, c_lower_bound, c_head_norm, d_q_a_norm, d_w_q_up, d_kv_a_norm, d_w_kv_up, d_q_norm, d_k_norm, cd_w_out, moe_norm, moe_w_group, moe_b_group, moe_w_expert, moe_b_expert, moe_w1, moe_w3, moe_w2):
    b = x.shape[0]
    depth = moe_norm.shape[0]
    h = jnp.concatenate([jnp.broadcast_to(meta_tokens.astype(x.dtype)[None], (b, N_META, D_MODEL)), x], axis=1)
    t = h.shape[1]
    lb_cum = jnp.cumsum(jax.nn.softmax(c_lower_bound.astype(F32), axis=0), axis=0)
    lower_bounds = lb_cum - lb_cum[0]
    for layer in range(depth):
        j = layer // 2
        if layer % 2 == 0:
            h = _even_layer(h, ab_norm[j], ab_w_in[j], a_conv_w[j], a_conv_b[j], a_b_i[j], a_b_f[j], a_head_norm[j],
                            b_w_gate2[j], b_b_gate[j], b_head_norm[j], ab_w_out[j])
        else:
            h = _odd_layer(h, lower_bounds[layer], cd_norm[j], cd_w_in[j], c_head_norm[j], d_q_a_norm[j],
                           d_w_q_up[j], d_kv_a_norm[j], d_w_kv_up[j], d_q_norm[j], d_k_norm[j], cd_w_out[j])
        last = layer == depth - 1
        h = _moe(h.reshape(b * t, D_MODEL), moe_norm[layer], moe_w_group[layer], moe_b_group[layer],
                 moe_w_expert[layer], moe_b_expert[layer], moe_w1, moe_w3, moe_w2, layer,
                 final_shape=(b, t) if last else None)
        if not last:
            h = h.reshape(b, t, D_MODEL)
    return h
```

```python
import functools
import math

import jax
import jax.numpy as jnp
from jax import lax
from jax.experimental import pallas as pl
from jax.experimental.pallas import tpu as pltpu

F32 = jnp.float32
BF16 = jnp.bfloat16
HIGHEST = lax.Precision.HIGHEST

D_MODEL = 2048
N_META = 16
CHUNK = 64
CONV_K = 4
EPS = 1e-6
A_HEADS, A_DK, A_DV = 4, 128, 256
B_HEADS, B_DK, B_DV = 4, 128, 256
GATE_RANK = 16
GATE_TAU = 16.0
C_HEADS, C_DK, C_DV = 8, 128, 128
D_HEADS, D_NOPE, D_ROPE, D_V = 8, 128, 64, 128
Q_LORA, KV_LORA = 512, 256
ROPE_THETA = 10000.0
N_GROUPS, EXPERTS_PER_GROUP = 4, 8
N_EXPERTS = N_GROUPS * EXPERTS_PER_GROUP
TOP_K = 2
D_EXPERT = 512

LANES = 128
MXU_DIM = 256
BF16_ROWS = 16
VMEM_LIMIT = 56 * 1024 * 1024
MOE_BLOCK = MXU_DIM
ROW_TILE = 688
ROW_TILE_BF16 = 2 * ROW_TILE
COL_TILE = 1280
COL_TILE_OUT = 1024
PREP_ROWS = 768
ATT_BLOCK = 256
ATT_PAD = ATT_BLOCK - N_META

_NT = (((1,), (1,)), ((), ()))
_TN = (((0,), (0,)), ((), ()))


def _dot(a, b, precision=None):
    return jnp.dot(a, b, preferred_element_type=F32, precision=precision)


def _dot_nt(a, b):
    return lax.dot_general(a, b, _NT, preferred_element_type=F32)


def _dot_tn(a, b):
    return lax.dot_general(a, b, _TN, preferred_element_type=F32)


def _bf(x):
    return x.astype(BF16)


def _split3(x):
    hi = _bf(x)
    rest = x - hi.astype(F32)
    mid = _bf(rest)
    return hi, mid, _bf(rest - mid.astype(F32))


def _log_sigmoid(x):
    return jnp.minimum(x, 0.0) - jnp.log1p(jnp.exp(-jnp.abs(x)))


def _sigmoid(x):
    return 1.0 / (1.0 + jnp.exp(-x))


def _silu(x):
    return x * _sigmoid(x)


def _row_tile(m, cap):
    best = None
    for t in range(BF16_ROWS, min(m, cap) + 1, BF16_ROWS):
        if m % t == 0:
            best = t
    assert best is not None, m
    return best


def _col_tile(n, cap):
    best = None
    for t in range(MXU_DIM, min(n, cap) + 1, MXU_DIM):
        if n % t == 0:
            best = t
    assert best is not None, n
    return best


def _params(*sem):
    return pltpu.CompilerParams(dimension_semantics=sem, vmem_limit_bytes=VMEM_LIMIT)


_RELAYOUT_ROWS = 256


def _even_weight_kernel(wa_ref, wb_ref, wc_ref, o_ref, *, n_plain, n_shift, shift, gate_cols):
    ob = pl.program_id(0)
    rows = o_ref.shape[0]
    chunks = rows // _RELAYOUT_ROWS

    @pl.when(ob < n_plain)
    def _():
        def body(c, carry):
            r = pl.ds(pl.multiple_of(c * _RELAYOUT_ROWS, _RELAYOUT_ROWS), _RELAYOUT_ROWS)
            o_ref[r, :] = _bf(wa_ref[r, :])
            return carry

        lax.fori_loop(0, chunks, body, 0)

    @pl.when((ob >= n_plain) & (ob < n_plain + n_shift))
    def _():
        def body(c, carry):
            r = pl.ds(pl.multiple_of(c * _RELAYOUT_ROWS, _RELAYOUT_ROWS), _RELAYOUT_ROWS)
            wide = jnp.concatenate([wa_ref[r, :], wb_ref[r, :]], axis=1)
            o_ref[r, :] = _bf(wide[:, shift:shift + MXU_DIM])
            return carry

        lax.fori_loop(0, chunks, body, 0)

    @pl.when(ob == n_plain + n_shift)
    def _():
        lane = lax.broadcasted_iota(jnp.int32, (_RELAYOUT_ROWS, LANES), 1)

        def body(c, carry):
            r = pl.ds(pl.multiple_of(c * _RELAYOUT_ROWS, _RELAYOUT_ROWS), _RELAYOUT_ROWS)
            first = jnp.where(lane < shift, wc_ref[r, :], jnp.where(lane < gate_cols, wb_ref[r, :], 0.0))
            o_ref[r, :] = _bf(jnp.concatenate([first, jnp.zeros_like(first)], axis=1))
            return carry

        lax.fori_loop(0, chunks, body, 0)


def _even_weight(w_in, a_w, g_w, b_w, rank):
    d = w_in.shape[0]
    assert a_w % MXU_DIM == 0 and b_w % MXU_DIM == 0 and g_w + rank <= LANES and d % _RELAYOUT_ROWS == 0
    n_plain, n_shift = a_w // MXU_DIM, b_w // MXU_DIM
    n_out = n_plain + n_shift + 1
    last = n_out - 1

    def b_index(ob):
        return (0, jnp.where(ob < n_plain, 0, jnp.where(ob < last, 2 * (ob + 1), (a_w + g_w + b_w) // LANES)))

    return pl.pallas_call(
        functools.partial(_even_weight_kernel, n_plain=n_plain, n_shift=n_shift, shift=g_w, gate_cols=g_w + rank),
        grid=(n_out,),
        in_specs=[
            pl.BlockSpec((d, MXU_DIM), lambda ob: (0, jnp.minimum(ob, last - 1))),
            pl.BlockSpec((d, LANES), b_index),
            pl.BlockSpec((d, LANES), lambda ob: (0, a_w // LANES)),
        ],
        out_specs=pl.BlockSpec((d, MXU_DIM), lambda ob: (0, ob)),
        out_shape=jax.ShapeDtypeStruct((d, n_out * MXU_DIM), BF16),
        compiler_params=_params("parallel"),
        name="even_weight",
    )(w_in, w_in, w_in)


def _odd_weight_kernel(w_ref, o_ref, *, n_plain):
    ob = pl.program_id(0)
    chunks = o_ref.shape[0] // _RELAYOUT_ROWS
    half = D_ROPE // 2

    def body(c, carry):
        r = pl.ds(pl.multiple_of(c * _RELAYOUT_ROWS, _RELAYOUT_ROWS), _RELAYOUT_ROWS)
        w = w_ref[r, :]

        @pl.when(ob < n_plain)
        def _():
            o_ref[r, :] = _bf(w)

        @pl.when(ob == n_plain)
        def _():
            pair = jnp.concatenate([w[:, :D_ROPE], w[:, half:D_ROPE], w[:, :half]], axis=1)
            o_ref[r, :] = _bf(jnp.concatenate([pair, jnp.zeros_like(pair)], axis=1))

        return carry

    lax.fori_loop(0, chunks, body, 0)


def _odd_weight(w_in, kr0):
    d = w_in.shape[0]
    assert kr0 % MXU_DIM == 0 and w_in.shape[1] == kr0 + D_ROPE and d % _RELAYOUT_ROWS == 0
    n_plain = kr0 // MXU_DIM
    return pl.pallas_call(
        functools.partial(_odd_weight_kernel, n_plain=n_plain),
        grid=(n_plain + 1,),
        in_specs=[pl.BlockSpec((d, MXU_DIM), lambda ob: (0, ob))],
        out_specs=pl.BlockSpec((d, MXU_DIM), lambda ob: (0, ob)),
        out_shape=jax.ShapeDtypeStruct((d, kr0 + MXU_DIM), BF16),
        compiler_params=_params("parallel"),
        name="odd_weight",
    )(w_in)


def _normproj_kernel(x_ref, g_ref, w_ref, o_ref, xs_ref):
    tm = xs_ref.shape[0]

    @pl.when(pl.program_id(1) == 0)
    def _():
        def body(c, carry):
            r0 = pl.multiple_of(c * BF16_ROWS, BF16_ROWS)
            x = x_ref[pl.ds(r0, BF16_ROWS), :]
            ms = jnp.mean(x * x, axis=-1, keepdims=True)
            xs_ref[pl.ds(r0, BF16_ROWS), :] = _bf(x * lax.rsqrt(ms + EPS) * g_ref[...])
            return carry

        lax.fori_loop(0, tm // BF16_ROWS, body, 0, unroll=8)

    o_ref[...] = _dot(xs_ref[...], w_ref[...]).astype(o_ref.dtype)


def _normproj(x2d, gain, w, *, x_col_block=0, out_dtype=F32):
    m = x2d.shape[0]
    k, n = w.shape
    tm = _row_tile(m, ROW_TILE)
    tn = _col_tile(n, COL_TILE)
    return pl.pallas_call(
        _normproj_kernel,
        grid=(m // tm, n // tn),
        in_specs=[
            pl.BlockSpec((tm, k), lambda i, j: (i, x_col_block)),
            pl.BlockSpec((1, k), lambda i, j: (0, 0)),
            pl.BlockSpec((k, tn), lambda i, j: (0, j)),
        ],
        out_specs=pl.BlockSpec((tm, tn), lambda i, j: (i, j)),
        out_shape=jax.ShapeDtypeStruct((m, n), out_dtype),
        scratch_shapes=[pltpu.VMEM((tm, k), BF16)],
        compiler_params=_params("parallel", "arbitrary"),
        name="normproj",
    )(x2d, gain.reshape(1, k).astype(F32), w)


def _outproj_kernel(ya_ref, yb_ref, w_ref, r_ref, o_ref):
    ka = ya_ref.shape[1]
    acc = _dot(ya_ref[...], w_ref[:ka, :]) + _dot(yb_ref[...], w_ref[ka:, :])
    o_ref[...] = r_ref[...] + acc


def _outproj(ya, yb, w, res):
    m, ka = ya.shape
    kb = yb.shape[1]
    n = w.shape[1]
    tm = _row_tile(m, ROW_TILE_BF16)
    tn = _col_tile(n, COL_TILE_OUT)
    return pl.pallas_call(
        _outproj_kernel,
        grid=(m // tm, n // tn),
        in_specs=[
            pl.BlockSpec((tm, ka), lambda i, j: (i, 0)),
            pl.BlockSpec((tm, kb), lambda i, j: (i, 0)),
            pl.BlockSpec((ka + kb, tn), lambda i, j: (0, j)),
            pl.BlockSpec((tm, tn), lambda i, j: (i, j)),
        ],
        out_specs=pl.BlockSpec((tm, tn), lambda i, j: (i, j)),
        out_shape=jax.ShapeDtypeStruct((m, n), F32),
        compiler_params=_params("parallel", "arbitrary"),
        name="outproj",
    )(ya, yb, w, res)


def _mlstm_kernel(bi_ref, bf_ref, q_ref, k_ref, v_ref, og_ref, gt_ref,
                  cwq_ref, cwk_ref, cbq_ref, cbk_ref, hn_ref, o_ref, c_ref, n_ref, m_ref, *, heads, group):
    t_total = q_ref.shape[1]
    n_chunks = (t_total - N_META) // CHUNK
    head0 = pl.program_id(1) * heads

    c_ref[...] = jnp.zeros_like(c_ref)
    n_ref[...] = jnp.zeros_like(n_ref)
    m_ref[...] = jnp.zeros_like(m_ref)

    def conv(taps, cw, cb, length):
        y = cb
        for j in range(CONV_K):
            y = y + taps[j] * cw[j:j + 1, :]
        return _silu(y)

    def window_taps(win, length):
        return [win[8 - (CONV_K - 1) + j:8 - (CONV_K - 1) + j + length, :] for j in range(CONV_K)]

    hs = range(heads)
    kcol = [slice(hh * A_DK, (hh + 1) * A_DK) for hh in hs]
    vcol = [slice(hh * A_DV, (hh + 1) * A_DV) for hh in hs]

    def gates(hh, blk, blk_parts, length, causal, upper):
        head = head0 + hh
        b_i = bi_ref[head]
        b_f = bf_ref[head]
        lane = lax.broadcasted_iota(jnp.int32, (length, LANES), 1)
        ig_c = jnp.sum(jnp.where(lane == head, blk, 0.0), axis=1, keepdims=True) + b_i
        lf_c = _log_sigmoid(jnp.sum(jnp.where(lane == A_HEADS + head, blk, 0.0), axis=1, keepdims=True) + b_f)
        sel_r = lax.broadcasted_iota(jnp.int32, (8, LANES), 0)
        sel_l = lax.broadcasted_iota(jnp.int32, (8, LANES), 1)
        sel = jnp.where(sel_l == head + A_HEADS * sel_r, 1.0, 0.0).astype(BF16)
        rows = _dot_nt(sel, blk_parts[0]) + (_dot_nt(sel, blk_parts[1]) + _dot_nt(sel, blk_parts[2]))
        ig_r = rows[0:1, :] + b_i
        lf_r = _log_sigmoid(rows[1:2, :] + b_f)
        b_c = jnp.sum(jnp.where(causal, lf_r, 0.0), axis=1, keepdims=True)
        b_r = jnp.sum(jnp.where(upper, lf_c, 0.0), axis=0, keepdims=True)
        b_end = b_c[length - 1:length, :]
        w_end = b_end - b_c + ig_c
        m_loc = jnp.max(w_end, axis=0, keepdims=True)
        d = jnp.where(causal, b_c - b_r + ig_r, -jnp.inf)
        return b_c, b_end, jnp.exp(w_end - m_loc), m_loc, d, jnp.max(d, axis=1, keepdims=True)

    def local_stage(o, length, qwin, kwin):
        row = lax.broadcasted_iota(jnp.int32, (length, length), 0)
        col = lax.broadcasted_iota(jnp.int32, (length, length), 1)
        causal = col <= row
        blk = gt_ref[0, pl.ds(o, length), 0:LANES]
        blk_parts = _split3(blk)
        gt = [gates(hh, blk, blk_parts, length, causal, row <= col) for hh in hs]
        q = [conv(qwin[hh], cwq_ref[:, kcol[hh]], cbq_ref[:, kcol[hh]], length) for hh in hs]
        k = [conv(kwin[hh], cwk_ref[:, kcol[hh]], cbk_ref[:, kcol[hh]], length) * (A_DK ** -0.5) for hh in hs]
        vb = [_bf(v_ref[0, pl.ds(o, length), vcol[hh]]) for hh in hs]
        qb = [_bf(q[hh]) for hh in hs]
        k_w = [k[hh] * gt[hh][2] for hh in hs]
        qk = [_dot_nt(qb[hh], _bf(k[hh])) for hh in hs]
        c_loc = [_dot_tn(_bf(k_w[hh]), vb[hh]) for hh in hs]
        n_loc = [jnp.sum(k_w[hh], axis=0, keepdims=True) for hh in hs]
        return gt, q, qb, vb, qk, c_loc, n_loc

    def state_stage(o, length, staged, state):
        gt, q, qb, vb, qk, c_loc, n_loc = staged
        c_in, n_in, m_in = state
        q_c = [_dot(qb[hh], _bf(c_in[hh])) for hh in hs]
        s, a_t, m_t, q_n, c_out, n_out, m_out = [], [], [], [], [], [], []
        for hh in hs:
            b_c, b_end, _, m_loc, d, d_max = gt[hh]
            inter = b_c + m_in[hh]
            m_t.append(jnp.maximum(inter, d_max))
            s.append(qk[hh] * jnp.exp(d - m_t[hh]))
            a_t.append(jnp.exp(inter - m_t[hh]))
            m_new = jnp.maximum(b_end + m_in[hh], m_loc)
            a = jnp.exp(b_end + m_in[hh] - m_new)
            c = jnp.exp(m_loc - m_new)
            c_out.append(a * c_in[hh] + c * c_loc[hh])
            n_out.append(a * n_in[hh] + c * n_loc[hh])
            m_out.append(m_new)
            q_n.append(jnp.sum(q[hh] * n_in[hh], axis=1, keepdims=True))
        num = [_dot(_bf(s[hh]), vb[hh]) + a_t[hh] * q_c[hh] for hh in hs]
        for hh in hs:
            den = jnp.sum(s[hh], axis=1, keepdims=True) + a_t[hh] * q_n[hh]
            h = num[hh] / jnp.maximum(jnp.abs(den), jnp.exp(-m_t[hh]))
            hn = h * lax.rsqrt(jnp.mean(h * h, axis=-1, keepdims=True) + EPS) * hn_ref[:, vcol[hh]]
            y = _sigmoid(og_ref[0, pl.ds(o, length), vcol[hh]]) * hn
            o_ref[0, pl.ds(o, length), vcol[hh]] = y.astype(o_ref.dtype)
        return c_out, n_out, m_out

    def sweep(offsets, length, qwins, kwins):
        staged = [local_stage(o, length, qw, kw) for o, qw, kw in zip(offsets, qwins, kwins)]
        state = ([c_ref[hh] for hh in hs], [n_ref[hh] for hh in hs], [m_ref[hh] for hh in hs])
        for o, stg in zip(offsets, staged):
            state = state_stage(o, length, stg, state)
        for hh in hs:
            c_ref[hh] = state[0][hh]
            n_ref[hh] = state[1][hh]
            m_ref[hh] = state[2][hh]

    zeros8 = jnp.zeros((8, A_DK), F32)
    sweep([0], N_META,
          [[window_taps(jnp.concatenate([zeros8, q_ref[0, 0:N_META, kcol[hh]]], axis=0), N_META) for hh in hs]],
          [[window_taps(jnp.concatenate([zeros8, k_ref[0, 0:N_META, kcol[hh]]], axis=0), N_META) for hh in hs]])
    assert n_chunks % group == 0

    def body(c, carry):
        offsets = [pl.multiple_of(N_META + (c * group + j) * CHUNK, BF16_ROWS) for j in range(group)]

        starts = [pl.multiple_of(N_META - 8 + (c * group + j) * CHUNK, 8) for j in range(group)]

        def shifted(ref, w0, hh):
            return window_taps(ref[0, pl.ds(w0, CHUNK + 8), kcol[hh]], CHUNK)

        sweep(offsets, CHUNK,
              [[shifted(q_ref, w0, hh) for hh in hs] for w0 in starts],
              [[shifted(k_ref, w0, hh) for hh in hs] for w0 in starts])
        return carry

    lax.fori_loop(0, n_chunks // group, body, 0)


def _mlstm(z, gate_col, conv_w, conv_b, b_i, b_f, head_norm, heads=2):
    b, t, _ = z.shape
    hk = A_HEADS * A_DK
    wk, wv = heads * A_DK, heads * A_DV
    smem = pl.BlockSpec(memory_space=pltpu.SMEM)
    col = lambda width, off: (lambda bi, g: (bi, 0, off // width + g))
    return pl.pallas_call(
        functools.partial(_mlstm_kernel, heads=heads, group=4),
        grid=(b, A_HEADS // heads),
        in_specs=[
            smem, smem,
            pl.BlockSpec((1, t, wk), col(wk, 0)),
            pl.BlockSpec((1, t, wk), col(wk, hk)),
            pl.BlockSpec((1, t, wv), col(wv, 2 * hk)),
            pl.BlockSpec((1, t, wv), col(wv, 2 * hk + A_HEADS * A_DV)),
            pl.BlockSpec((1, t, MXU_DIM), lambda bi, g: (bi, 0, gate_col // MXU_DIM)),
            pl.BlockSpec((CONV_K, wk), lambda bi, g: (0, g)),
            pl.BlockSpec((CONV_K, wk), lambda bi, g: (0, A_HEADS // heads + g)),
            pl.BlockSpec((1, wk), lambda bi, g: (0, g)),
            pl.BlockSpec((1, wk), lambda bi, g: (0, A_HEADS // heads + g)),
            pl.BlockSpec((1, wv), lambda bi, g: (0, g)),
        ],
        out_specs=pl.BlockSpec((1, t, wv), lambda bi, g: (bi, 0, g)),
        out_shape=jax.ShapeDtypeStruct((b, t, A_HEADS * A_DV), BF16),
        scratch_shapes=[pltpu.VMEM((heads, A_DK, A_DV), F32), pltpu.VMEM((heads, 1, A_DK), F32),
                        pltpu.VMEM((heads, 1, 1), F32)],
        compiler_params=_params("parallel", "parallel"),
        name="mlstm",
    )(b_i, b_f, z, z, z, z, z, conv_w, conv_w, conv_b.reshape(1, -1), conv_b.reshape(1, -1),
      head_norm.reshape(1, -1))


def _gla_kernel(q_ref, k_ref, v_ref, og_ref, g_ref, p0_ref, p1_ref, p2_ref, hn_ref, o_ref, st_ref, *pre_ref,
                mode, heads, group):
    t_total = q_ref.shape[1]
    n_chunks = (t_total - N_META) // CHUNK
    dv, dk = st_ref.shape[1:]
    st_ref[...] = jnp.zeros_like(st_ref)

    hs = range(heads)
    kcol = [slice(hh * dk, (hh + 1) * dk) for hh in hs]
    vcol = [slice(hh * dv, (hh + 1) * dv) for hh in hs]

    if mode == "gla":
        gate = g_ref[0]
        gate_hi = _bf(gate)
        gate_lo = _bf(gate - gate_hi.astype(F32))
        for hh in hs:
            w = p0_ref[hh]
            w_hi = _bf(w)
            w_lo = _bf(w - w_hi.astype(F32))
            both = _dot(gate_hi, jnp.concatenate([w_hi, w_lo], axis=1))
            pre_ref[0][hh] = (both[:, :dk] + both[:, dk:]) + _dot(gate_lo, w_hi) + p1_ref[:, kcol[hh]]

    def gate_inputs(hh, o, length):
        q = q_ref[0, pl.ds(o, length), kcol[hh]]
        if mode == "gla":
            pre = pre_ref[0][hh, pl.ds(o, length), :]
            return q * (dk ** -0.5), k_ref[0, pl.ds(o, length), kcol[hh]], _log_sigmoid(pre) / GATE_TAU
        fpre = g_ref[0, pl.ds(o, length), kcol[hh]]
        a = p0_ref[:, kcol[hh]]
        bb = p1_ref[:, kcol[hh]] + _log_sigmoid(fpre)
        lg = jnp.maximum(a, bb) + jnp.log1p(jnp.exp(-jnp.abs(a - bb)))
        return q, p2_ref[:, kcol[hh]] * _sigmoid(-fpre), lg

    def cumsum_time(tri, lg):
        parts = _dot(tri, jnp.concatenate(_split3(lg), axis=1))
        return parts[:, :dk] + (parts[:, dk:2 * dk] + parts[:, 2 * dk:])

    def local_stage(o, length):
        row = lax.broadcasted_iota(jnp.int32, (length, length), 0)
        col = lax.broadcasted_iota(jnp.int32, (length, length), 1)
        causal = col <= row
        tri = jnp.where(causal, 1.0, 0.0).astype(BF16)
        qkl = [gate_inputs(hh, o, length) for hh in hs]
        vb = [_bf(v_ref[0, pl.ds(o, length), vcol[hh]]) for hh in hs]
        g = [cumsum_time(tri, qkl[hh][2]) for hh in hs]
        g_end = [g[hh][length - 1:length, :] for hh in hs]
        g_mid = [g[hh][length // 2:length // 2 + 1, :] for hh in hs]
        s = [_dot_nt(_bf(qkl[hh][0] * jnp.exp(g[hh] - g_mid[hh])), _bf(qkl[hh][1] * jnp.exp(g_mid[hh] - g[hh])))
             for hh in hs]
        q_dec = [_bf(qkl[hh][0] * jnp.exp(g[hh])) for hh in hs]
        local = [_dot_tn(vb[hh], _bf(qkl[hh][1] * jnp.exp(g_end[hh] - g[hh]))) for hh in hs]
        intra = [_dot(_bf(jnp.where(causal, s[hh], 0.0)), vb[hh]) for hh in hs]
        return q_dec, [jnp.exp(ge) for ge in g_end], local, intra

    def state_stage(o, length, staged, st_in):
        q_dec, decay, local, intra = staged
        inter = [_dot_nt(q_dec[hh], _bf(st_in[hh])) for hh in hs]
        st_out = [st_in[hh] * decay[hh] + local[hh] for hh in hs]
        for hh in hs:
            out = intra[hh] + inter[hh]
            hn = out * lax.rsqrt(jnp.mean(out * out, axis=-1, keepdims=True) + EPS) * hn_ref[:, vcol[hh]]
            og = og_ref[0, pl.ds(o, length), vcol[hh]]
            gate = _silu(og) if mode == "gla" else _sigmoid(og)
            o_ref[0, pl.ds(o, length), vcol[hh]] = (gate * hn).astype(o_ref.dtype)
        return st_out

    def sweep(offsets, length):
        staged = [local_stage(o, length) for o in offsets]
        st = [st_ref[hh] for hh in hs]
        for o, stg in zip(offsets, staged):
            st = state_stage(o, length, stg, st)
        for hh in hs:
            st_ref[hh] = st[hh]

    sweep([0], N_META)
    assert n_chunks % group == 0

    def body(c, carry):
        o = pl.multiple_of(N_META + c * (group * CHUNK), BF16_ROWS)
        sweep([pl.multiple_of(o + j * CHUNK, BF16_ROWS) for j in range(group)], CHUNK)
        return carry

    lax.fori_loop(0, n_chunks // group, body, 0)


def _gla_call(z, n_heads, heads, dk, dv, blocks, gate_width, params, head_norm, mode):
    b, t, _ = z.shape
    q0, k0, v0, og0, g0 = blocks
    zspec = lambda width, off, grouped=True: pl.BlockSpec(
        (1, t, width), (lambda bi, g: (bi, 0, off // width + (g if grouped else 0))))
    (p0, s0), (p1, s1), (p2, s2) = params
    gate_spec = zspec(gate_width, g0, grouped=False) if mode == "gla" else zspec(heads * dk, g0)
    return pl.pallas_call(
        functools.partial(_gla_kernel, mode=mode, heads=heads, group=8),
        grid=(b, n_heads // heads),
        in_specs=[
            zspec(heads * dk, q0), zspec(heads * dk, k0), zspec(heads * dv, v0), zspec(heads * dv, og0),
            gate_spec, s0, s1, s2,
            pl.BlockSpec((1, heads * dv), lambda bi, g: (0, g)),
        ],
        out_specs=pl.BlockSpec((1, t, heads * dv), lambda bi, g: (bi, 0, g)),
        out_shape=jax.ShapeDtypeStruct((b, t, n_heads * dv), BF16),
        scratch_shapes=[pltpu.VMEM((heads, dv, dk), F32)] + (
            [pltpu.VMEM((heads, t, dk), F32)] if mode == "gla" else []),
        compiler_params=_params("parallel", "parallel"),
        name="gla_" + mode,
    )(z, z, z, z, z, p0, p1, p2, head_norm.reshape(1, -1))


def _mla_kernel(cq_ref, ckv_ref, kr_ref, tab_ref, gqa_ref, gkva_ref, wqn_ref, wqr_ref, wkn_ref, wv_ref,
                gqn_ref, gqr_ref, gkn_ref, gkr_ref, o_ref, qf_ref, kf_ref, vf_ref, *, heads):
    t_total = cq_ref.shape[1]
    n_blocks = (ATT_PAD + t_total) // ATT_BLOCK
    dqk = D_NOPE + D_ROPE
    scale = dqk ** -0.5
    rows = _row_tile(t_total, PREP_ROWS)

    hs = range(heads)
    col = [slice(hh * LANES, (hh + 1) * LANES) for hh in hs]
    for hh in hs:
        qf_ref[hh, 0:ATT_PAD, :] = jnp.zeros((ATT_PAD, 2 * LANES), BF16)
        kf_ref[hh, 0:ATT_PAD, :] = jnp.zeros((ATT_PAD, 2 * LANES), BF16)
        vf_ref[hh, 0:ATT_PAD, :] = jnp.zeros((ATT_PAD, D_V), BF16)

    def rope_pair(x, gains, tab):
        p = x * gains * tab
        return p + pltpu.roll(p, D_ROPE, 1)

    def latent_norm(ref, gain_ref, r0):
        u = ref[0, pl.ds(r0, rows), :]
        return _bf(u * lax.rsqrt(jnp.mean(u * u, axis=-1, keepdims=True) + EPS) * gain_ref[...])

    def up_project(c):
        r0 = c * rows
        cq = latent_norm(cq_ref, gqa_ref, r0)
        ckv = latent_norm(ckv_ref, gkva_ref, r0)
        return _dot(cq, wqn_ref[...]), _dot(cq, wqr_ref[...]), _dot(ckv, wkn_ref[...]), _dot(ckv, wv_ref[...])

    def prep(c, hh, projected):
        r0 = c * rows
        dst = ATT_PAD + c * rows
        tab = tab_ref[pl.ds(r0, rows), :]
        qn = projected[0][:, col[hh]]
        qr = projected[1][:, col[hh]]
        ssq = jnp.sum(qn * qn + 0.5 * (qr * qr), axis=-1, keepdims=True)
        rq = lax.rsqrt(ssq / dqk + EPS) * scale
        qf_ref[hh, pl.ds(dst, rows), 0:LANES] = _bf(qn * gqn_ref[...] * rq)
        qf_ref[hh, pl.ds(dst, rows), LANES:2 * LANES] = _bf(qr * gqr_ref[...] * tab * rq)
        kn = projected[2][:, col[hh]]
        kr = kr_ref[0, pl.ds(r0, rows), :]
        ssk = jnp.sum(kn * kn + 0.5 * (kr * kr), axis=-1, keepdims=True)
        rk = lax.rsqrt(ssk / dqk + EPS)
        kf_ref[hh, pl.ds(dst, rows), 0:LANES] = _bf(kn * gkn_ref[...] * rk)
        kf_ref[hh, pl.ds(dst, rows), LANES:2 * LANES] = _bf(rope_pair(kr, gkr_ref[...], tab) * rk)
        vf_ref[hh, pl.ds(dst, rows), :] = _bf(projected[3][:, col[hh]])

    for c in range(t_total // rows):
        projected = up_project(c)
        for hh in hs:
            prep(c, hh, projected)

    qpos = lax.broadcasted_iota(jnp.int32, (ATT_BLOCK, ATT_BLOCK), 0)
    kpos = lax.broadcasted_iota(jnp.int32, (ATT_BLOCK, ATT_BLOCK), 1)
    neg = -jnp.inf

    def scores(item):
        qi, hh = item
        q = qf_ref[hh, qi * ATT_BLOCK:(qi + 1) * ATT_BLOCK, :]
        return _dot_nt(q, kf_ref[hh, 0:(qi + 1) * ATT_BLOCK, :])

    items = [(qi, hh) for qi in range(n_blocks) for hh in hs]
    s_next = scores(items[0])
    for idx, (qi, hh) in enumerate(items):
        s = s_next
        if idx + 1 < len(items):
            s_next = scores(items[idx + 1])
        parts = [s[:, j * ATT_BLOCK:(j + 1) * ATT_BLOCK] for j in range(qi + 1)]
        parts[0] = jnp.where(kpos >= ATT_PAD, parts[0], neg)
        parts[qi] = jnp.where(kpos <= qpos, parts[qi], neg)
        top = functools.reduce(jnp.maximum, parts)
        m = jnp.max(top, axis=-1, keepdims=True)
        if qi == 0:
            m = jnp.where(m == neg, 0.0, m)
        probs = [jnp.exp(part - m) for part in parts]
        l = jnp.sum(functools.reduce(jnp.add, probs), axis=-1, keepdims=True)
        pv = _dot(jnp.concatenate([_bf(pr) for pr in probs], axis=1), vf_ref[hh, 0:(qi + 1) * ATT_BLOCK, :])
        if qi == 0:
            out = pv / jnp.where(l == 0.0, 1.0, l)
            o_ref[0, 0:N_META, col[hh]] = out[ATT_PAD:, :].astype(o_ref.dtype)
        else:
            dst = qi * ATT_BLOCK - ATT_PAD
            o_ref[0, dst:dst + ATT_BLOCK, col[hh]] = (pv / l).astype(o_ref.dtype)


def _mla(z, cq_col, tab, q_a_norm, kv_a_norm, wq, wkv, gqn, gqr, gkn, gkr):
    b, t, _ = z.shape
    assert (ATT_PAD + t) % ATT_BLOCK == 0 and t % BF16_ROWS == 0
    tp = ATT_PAD + t
    heads = 2
    groups = D_HEADS // heads
    ckv_col = cq_col + Q_LORA
    kr_col = ckv_col + KV_LORA
    wspec = lambda k, off: pl.BlockSpec((k, heads * LANES), lambda bi, g: (0, off + g))
    gspec = lambda width: pl.BlockSpec((1, width), lambda bi, g: (0, 0))
    return pl.pallas_call(
        functools.partial(_mla_kernel, heads=heads),
        grid=(b, groups),
        in_specs=[
            pl.BlockSpec((1, t, Q_LORA), lambda bi, g: (bi, 0, cq_col // Q_LORA)),
            pl.BlockSpec((1, t, KV_LORA), lambda bi, g: (bi, 0, ckv_col // KV_LORA)),
            pl.BlockSpec((1, t, LANES), lambda bi, g: (bi, 0, kr_col // LANES)),
            pl.BlockSpec((t, LANES), lambda bi, g: (0, 0)),
            gspec(Q_LORA), gspec(KV_LORA),
            wspec(Q_LORA, 0), wspec(Q_LORA, groups), wspec(KV_LORA, 0), wspec(KV_LORA, groups),
            gspec(LANES), gspec(LANES), gspec(LANES), gspec(LANES),
        ],
        out_specs=pl.BlockSpec((1, t, heads * D_V), lambda bi, g: (bi, 0, g)),
        out_shape=jax.ShapeDtypeStruct((b, t, D_HEADS * D_V), BF16),
        scratch_shapes=[pltpu.VMEM((heads, tp, 2 * LANES), BF16), pltpu.VMEM((heads, tp, 2 * LANES), BF16),
                        pltpu.VMEM((heads, tp, D_V), BF16)],
        compiler_params=_params("parallel", "parallel"),
        name="mla",
    )(z, z, z, tab, q_a_norm.reshape(1, -1), kv_a_norm.reshape(1, -1), wq, wq, wkv, wkv, gqn, gqr, gkn, gkr)


def _pack_bf16_pairs(v):
    w = v.shape[1] // 2
    bits = pltpu.bitcast(_bf(v).astype(F32), jnp.uint32)
    return (bits[:, :w] >> 16) | (bits[:, w:] & jnp.uint32(0xFFFF0000))


def _unpack_lo(words):
    return pltpu.bitcast(words << 16, F32)


def _unpack_hi(words):
    return pltpu.bitcast(words & jnp.uint32(0xFFFF0000), F32)


def _router_kernel(x_ref, g_ref, w_ref, b_ref, gate_ref, idx_ref, xg_ref, cnt_ref, carry_ref):
    tm = x_ref.shape[0]

    @pl.when(pl.program_id(0) == 0)
    def _():
        carry_ref[...] = jnp.zeros_like(carry_ref)

    x = x_ref[...]
    ms = jnp.mean(x * x, axis=-1, keepdims=True)
    xn = x * lax.rsqrt(ms + EPS) * g_ref[...]
    xh = _bf(xn)
    xl = _bf(xn - xh.astype(F32))
    w = w_ref[...]
    wh = _bf(w)
    wl = _bf(w - wh.astype(F32))
    both = _dot(xh, jnp.concatenate([wh, wl], axis=1))
    logits = (both[:, :LANES] + both[:, LANES:]) + _dot(xl, wh) + b_ref[...]
    lane = lax.broadcasted_iota(jnp.int32, logits.shape, 1)
    lane_f = lane.astype(F32)
    neg = -jnp.inf
    big = float(LANES)

    is_group = lane < N_GROUPS
    g_max = jnp.max(jnp.where(is_group, logits, neg), axis=-1, keepdims=True)
    g_sum = jnp.sum(jnp.where(is_group, jnp.exp(logits - g_max), 0.0), axis=-1, keepdims=True)
    p_top = 1.0 / g_sum
    grp = jnp.min(jnp.where(is_group & (logits == g_max), lane_f, big), axis=-1, keepdims=True)

    e_lo = N_GROUPS + grp * EXPERTS_PER_GROUP
    in_grp = (lane_f >= e_lo) & (lane_f < e_lo + EXPERTS_PER_GROUP)
    e_max = jnp.max(jnp.where(in_grp, logits, neg), axis=-1, keepdims=True)
    e_sum = jnp.sum(jnp.where(in_grp, jnp.exp(logits - e_max), 0.0), axis=-1, keepdims=True)
    i1 = jnp.min(jnp.where(in_grp & (logits == e_max), lane_f, big), axis=-1, keepdims=True)
    rest = in_grp & (lane_f != i1)
    e_2nd = jnp.max(jnp.where(rest, logits, neg), axis=-1, keepdims=True)
    i2 = jnp.min(jnp.where(rest & (logits == e_2nd), lane_f, big), axis=-1, keepdims=True)
    p1 = 1.0 / e_sum
    p2 = jnp.exp(e_2nd - e_max) / e_sum
    tot = p1 + p2
    gate_ref[...] = jnp.where(lane == 0, p_top * p1 / tot, jnp.where(lane == 1, p_top * p2 / tot, 0.0))

    e1 = i1 - N_GROUPS
    e2 = i2 - N_GROUPS
    hot = jnp.where((lane_f == e1) | (lane_f == e2), 1.0, 0.0)
    row = lax.broadcasted_iota(jnp.int32, (tm, tm), 0)
    col = lax.broadcasted_iota(jnp.int32, (tm, tm), 1)
    before = _dot(jnp.where(col < row, 1.0, 0.0).astype(BF16), _bf(hot)) + carry_ref[...]
    r1 = jnp.sum(jnp.where(lane_f == e1, before, 0.0), axis=-1, keepdims=True)
    r2 = jnp.sum(jnp.where(lane_f == e2, before, 0.0), axis=-1, keepdims=True)
    total = carry_ref[...] + jnp.sum(hot, axis=0, keepdims=True)
    carry_ref[...] = total
    cnt_ref[...] = jnp.broadcast_to(total, cnt_ref.shape).astype(jnp.int32)
    idx_ref[...] = jnp.where(lane == 0, e1, jnp.where(lane == 1, e2, jnp.where(lane == 2, r1, jnp.where(
        lane == 3, r2, 0.0)))).astype(jnp.int32)

    words = _pack_bf16_pairs(xn)
    for s in range(8):
        xg_ref[pl.ds(s, tm, stride=8), :] = words[:, s * LANES:(s + 1) * LANES]


def _router(x2d, gain, w_group, b_group, w_expert, b_expert):
    m, d = x2d.shape
    assert d == 2 * 8 * LANES
    tm = _row_tile(m, ROW_TILE)
    pad = LANES - N_GROUPS - N_EXPERTS
    w = jnp.concatenate([w_group, w_expert, jnp.zeros((d, pad), F32)], axis=1)
    bias = jnp.concatenate([b_group, b_expert, jnp.zeros((pad,), F32)]).reshape(1, LANES)
    return pl.pallas_call(
        _router_kernel,
        grid=(m // tm,),
        in_specs=[
            pl.BlockSpec((tm, d), lambda i: (i, 0)),
            pl.BlockSpec((1, d), lambda i: (0, 0)),
            pl.BlockSpec((d, LANES), lambda i: (0, 0)),
            pl.BlockSpec((1, LANES), lambda i: (0, 0)),
        ],
        out_specs=[pl.BlockSpec((tm, LANES), lambda i: (i, 0)), pl.BlockSpec((tm, LANES), lambda i: (i, 0)),
                   pl.BlockSpec((tm * 8, LANES), lambda i: (i, 0)), pl.BlockSpec((8, LANES), lambda i: (0, 0))],
        out_shape=[jax.ShapeDtypeStruct((m, LANES), F32), jax.ShapeDtypeStruct((m, LANES), jnp.int32),
                   jax.ShapeDtypeStruct((m * 8, LANES), jnp.uint32), jax.ShapeDtypeStruct((8, LANES), jnp.int32)],
        scratch_shapes=[pltpu.VMEM((1, LANES), F32)],
        compiler_params=_params("arbitrary"),
        name="router",
    )(x2d, gain.reshape(1, d), w, bias)


def _invert_kernel(dest_ref, inv_ref):
    def clear(s, carry):
        inv_ref[s] = -1
        return carry

    lax.fori_loop(0, inv_ref.shape[0], clear, 0, unroll=8)

    def put(f, carry):
        inv_ref[dest_ref[f]] = f
        return carry

    lax.fori_loop(0, dest_ref.shape[0], put, 0, unroll=8)


def _invert(dest, p):
    assert p % 8 == 0 and dest.shape[0] % 8 == 0
    smem = pl.BlockSpec(memory_space=pltpu.SMEM)
    return pl.pallas_call(
        _invert_kernel, in_specs=[smem], out_specs=smem,
        out_shape=jax.ShapeDtypeStruct((p,), jnp.int32), name="moe_invert",
    )(dest)


def _expert_kernel(be_ref, nxt_ref, run_ref, nu_ref, src_ref, dst_ref, xg_hbm, w1_hbm, w3_hbm, w2_hbm, o_hbm,
                   xbuf, ybuf, w1f, w3f, w2f, w1s, w3s, w2s, sem_in, sem_out, sem_w, *, layer):
    i = pl.program_id(0)
    n_used = nu_ref[0]
    half_e = D_EXPERT // 2

    def weight_copies(expert, slot):
        return [pltpu.make_async_copy(w_hbm.at[layer, expert], w_f.at[slot], sem_w.at[slot, j])
                for j, (w_hbm, w_f) in enumerate(((w1_hbm, w1f), (w3_hbm, w3f), (w2_hbm, w2f)))]

    def gather(block, r, slot):
        src = pl.multiple_of(src_ref[block * MOE_BLOCK + r], 8)
        return pltpu.make_async_copy(xg_hbm.at[pl.ds(src, 8), :], xbuf.at[slot, pl.ds(r * 8, 8), :], sem_in.at[slot])

    def scatter(block, r, slot):
        dst = pl.multiple_of(dst_ref[(block + 1) * MOE_BLOCK + r], 8)
        return pltpu.make_async_copy(ybuf.at[slot, pl.ds(r * 8, 8), :], o_hbm.at[pl.ds(dst, 8), :], sem_out.at[slot])

    def wait_gathers(slot):
        pltpu.make_async_copy(xbuf.at[1 - slot], xbuf.at[slot], sem_in.at[slot]).wait()

    def wait_scatters(slot):
        pltpu.make_async_copy(ybuf.at[slot], ybuf.at[1 - slot], sem_out.at[slot]).wait()

    @pl.when(i == 0)
    def _():
        ybuf[1] = jnp.zeros(ybuf.shape[1:], ybuf.dtype)
        n_real = o_hbm.shape[0] - 2 * MOE_BLOCK * 8
        fill1 = pltpu.make_async_copy(ybuf.at[1], o_hbm.at[pl.ds(n_real + MOE_BLOCK * 8, MOE_BLOCK * 8), :],
                                      sem_out.at[1])
        fill1.start()
        fill1.wait()
        pltpu.make_async_copy(ybuf.at[1], o_hbm.at[pl.ds(n_real, MOE_BLOCK * 8), :], sem_out.at[0]).start()

        def first(r, carry):
            gather(0, r, 0).start()
            return carry

        lax.fori_loop(0, MOE_BLOCK, first, 0, unroll=8)

    @pl.when(i < n_used)
    def _():
        expert = be_ref[i]
        wslot = run_ref[i] % 2

        @pl.when(i == 0)
        def _():
            for cp in weight_copies(expert, 0):
                cp.start()

        @pl.when((i == 0) | (expert != be_ref[jnp.maximum(i - 1, 0)]))
        def _():
            for cp in weight_copies(expert, wslot):
                cp.wait()

            @pl.when(nxt_ref[i] != expert)
            def _():
                for cp in weight_copies(nxt_ref[i], 1 - wslot):
                    cp.start()

            w1s[...] = _bf(w1f[wslot])
            w3s[...] = _bf(w3f[wslot])
            w2s[...] = _bf(w2f[wslot])

        def block(par):
            wait_gathers(par)

            def move_rows(group, n_groups=6):
                lo, hi = group * MOE_BLOCK // n_groups, (group + 1) * MOE_BLOCK // n_groups
                for r in range(lo, hi):
                    gather(i + 1, r, 1 - par).start(priority=r % 2)
                    scatter(i - 1, r, 1 - par).start(priority=(r + 1) % 2)

            tiles = [xbuf[par, pl.ds(s, MOE_BLOCK, stride=8), :] for s in range(8)]
            xb = jnp.concatenate([_bf(_unpack_lo(w)) for w in tiles] + [_bf(_unpack_hi(w)) for w in tiles], axis=1)
            move_rows(0)
            h1a = _dot(xb, w1s[:, :half_e])
            move_rows(1)
            h3a = _dot(xb, w3s[:, :half_e])
            move_rows(2)
            act_a = _bf(_silu(h1a) * h3a)
            h1b = _dot(xb, w1s[:, half_e:])
            move_rows(3)
            h3b = _dot(xb, w3s[:, half_e:])
            move_rows(4)
            act_b = _bf(_silu(h1b) * h3b)
            y = _dot(act_a, w2s[:half_e, :])
            move_rows(5)
            y = y + _dot(act_b, w2s[half_e:, :])

            wait_scatters(par)
            words = _pack_bf16_pairs(y)
            for s in range(8):
                ybuf[par, pl.ds(s, MOE_BLOCK, stride=8), :] = words[:, s * LANES:(s + 1) * LANES]

            @pl.when(i == n_used - 1)
            def _():
                def last(r, carry):
                    scatter(i, r, par).start()
                    return carry

                lax.fori_loop(0, MOE_BLOCK, last, 0, unroll=8)
                wait_scatters(par)
                wait_scatters(1 - par)
                wait_gathers(1 - par)

        for parity in range(2):
            pl.when(i % 2 == parity)(functools.partial(block, parity))


def _moe(x2d, gain, w_group, b_group, w_expert, b_expert, w1, w3, w2, layer, final_shape=None):
    n, d = x2d.shape
    gates_l, idx_l, xg, cnt = _router(x2d, gain, w_group, b_group, w_expert, b_expert)

    a = n * TOP_K
    n_blocks = -(-a // MOE_BLOCK) + N_EXPERTS
    p = n_blocks * MOE_BLOCK
    counts = cnt[0, :N_EXPERTS]
    padded = (counts + MOE_BLOCK - 1) // MOE_BLOCK * MOE_BLOCK
    pad_end = jnp.cumsum(padded)
    pad_start = pad_end - padded
    e_hot = idx_l[:, :TOP_K, None] == jnp.arange(N_EXPERTS, dtype=jnp.int32)
    dest = jnp.sum(jnp.where(e_hot, pad_start, 0), axis=-1) + idx_l[:, TOP_K:2 * TOP_K]
    dest = jnp.clip(dest.reshape(-1), 0, p - 1).astype(jnp.int32)
    blk0 = jnp.arange(n_blocks, dtype=jnp.int32) * MOE_BLOCK
    block_expert = jnp.minimum(jnp.searchsorted(pad_end, blk0, side="right"), N_EXPERTS - 1).astype(jnp.int32)
    n_used = (pad_end[-1] // MOE_BLOCK).astype(jnp.int32).reshape(1)
    experts = jnp.arange(N_EXPERTS, dtype=jnp.int32)
    later = jnp.where((experts[None, :] > experts[:, None]) & (counts[None, :] > 0), experts[None, :], N_EXPERTS)
    next_owner = jnp.min(later, axis=1)
    next_expert = jnp.where(next_owner < N_EXPERTS, next_owner, experts)[block_expert].astype(jnp.int32)
    run_index = (jnp.cumsum(jnp.concatenate([jnp.ones((1,), jnp.int32),
                                             (block_expert[1:] != block_expert[:-1]).astype(jnp.int32)])) - 1
                 ).astype(jnp.int32)
    codes = _invert(dest, p)
    slot = jnp.arange(p, dtype=jnp.int32)
    spare = TOP_K * n + (slot // MOE_BLOCK % 2) * MOE_BLOCK + slot % MOE_BLOCK
    src_tok = (jnp.maximum(codes, 0) >> 1) * 8
    dst_row = jnp.where(codes >= 0, (codes & 1) * n + (codes >> 1), spare) * 8
    lead = (TOP_K * n + MOE_BLOCK + jnp.arange(MOE_BLOCK, dtype=jnp.int32)) * 8
    dst_row = jnp.concatenate([lead, dst_row])

    hbm = pl.BlockSpec(memory_space=pl.ANY)
    out_rows = TOP_K * n + 2 * MOE_BLOCK
    out2 = pl.pallas_call(
        functools.partial(_expert_kernel, layer=layer),
        grid_spec=pltpu.PrefetchScalarGridSpec(
            num_scalar_prefetch=6,
            grid=(n_blocks,),
            in_specs=[hbm, hbm, hbm, hbm],
            out_specs=hbm,
            scratch_shapes=[
                pltpu.VMEM((2, MOE_BLOCK * 8, LANES), jnp.uint32), pltpu.VMEM((2, MOE_BLOCK * 8, LANES), jnp.uint32),
                pltpu.VMEM((2, d, D_EXPERT), F32), pltpu.VMEM((2, d, D_EXPERT), F32), pltpu.VMEM((2, D_EXPERT, d), F32),
                pltpu.VMEM((d, D_EXPERT), BF16), pltpu.VMEM((d, D_EXPERT), BF16), pltpu.VMEM((D_EXPERT, d), BF16),
                pltpu.SemaphoreType.DMA((2,)), pltpu.SemaphoreType.DMA((2,)), pltpu.SemaphoreType.DMA((2, 3)),
            ],
        ),
        out_shape=jax.ShapeDtypeStruct((out_rows * 8, LANES), jnp.uint32),
        compiler_params=_params("arbitrary"),
        name="moe_experts",
    )(block_expert, next_expert, run_index, n_used, src_tok, dst_row, xg, w1, w3, w2)
    return _combine(x2d, gates_l, out2, final_shape)


def _combine_tile(x_ref, gate_ref, a_ref, b_ref, o_ref):
    tm = x_ref.shape[0]
    half = x_ref.shape[1] // 2
    g0 = gate_ref[:, 0:1]
    g1 = gate_ref[:, 1:2]
    for s in range(8):
        wa = a_ref[pl.ds(s, tm, stride=8), :]
        wb = b_ref[pl.ds(s, tm, stride=8), :]
        lo = slice(s * LANES, (s + 1) * LANES)
        hi = slice(half + s * LANES, half + (s + 1) * LANES)
        o_ref[:, lo] = x_ref[:, lo] + (g0 * _unpack_lo(wa) + g1 * _unpack_lo(wb))
        o_ref[:, hi] = x_ref[:, hi] + (g0 * _unpack_hi(wa) + g1 * _unpack_hi(wb))


def _combine_kernel(x_ref, gate_ref, a_ref, b_ref, o_ref):
    _combine_tile(x_ref, gate_ref, a_ref, b_ref, o_ref)


def _combine_final_kernel(x_ref, gate_ref, a_ref, b_ref, o_hbm, obuf, sem, *, tiles_per_batch):
    i = pl.program_id(0)
    n_steps = pl.num_programs(0)
    tm = x_ref.shape[0]

    def copies(step, slot):
        batch, j = step // tiles_per_batch, step % tiles_per_batch
        row = pl.multiple_of(j * tm, 8)
        main = pltpu.make_async_copy(obuf.at[slot, pl.ds(N_META, tm - N_META), :],
                                     o_hbm.at[batch, pl.ds(row, tm - N_META), :], sem.at[slot, 0])
        head = pltpu.make_async_copy(obuf.at[slot, pl.ds(0, N_META), :],
                                     o_hbm.at[batch, pl.ds(pl.multiple_of(jnp.maximum(row - N_META, 0), 8), N_META), :],
                                     sem.at[slot, 1])
        return main, head, j > 0

    def wait_step(step, slot):
        main, head, has_head = copies(step, slot)
        main.wait()

        @pl.when(has_head)
        def _():
            head.wait()

    slot = i % 2

    @pl.when(i >= 2)
    def _():
        wait_step(i - 2, slot)

    _combine_tile(x_ref, gate_ref, a_ref, b_ref, obuf.at[slot])
    main, head, has_head = copies(i, slot)
    main.start()

    @pl.when(has_head)
    def _():
        head.start()

    @pl.when(i == n_steps - 1)
    def _():
        wait_step(i, slot)

        @pl.when(i >= 1)
        def _():
            wait_step(i - 1, 1 - slot)


def _combine(x2d, gates, out2, final_shape=None):
    n, d = x2d.shape
    tm = _row_tile(n, ROW_TILE)
    in_specs = [
        pl.BlockSpec((tm, d), lambda i: (i, 0)),
        pl.BlockSpec((tm, LANES), lambda i: (i, 0)),
        pl.BlockSpec((tm * 8, LANES), lambda i: (i, 0)),
        pl.BlockSpec((tm * 8, LANES), lambda i: (n // tm + i, 0)),
    ]
    if final_shape is None:
        return pl.pallas_call(
            _combine_kernel,
            grid=(n // tm,),
            in_specs=in_specs,
            out_specs=pl.BlockSpec((tm, d), lambda i: (i, 0)),
            out_shape=jax.ShapeDtypeStruct((n, d), F32),
            compiler_params=_params("parallel"),
            name="moe_combine",
        )(x2d, gates, out2, out2)
    b, t = final_shape
    assert t % tm == 0 and tm > N_META
    return pl.pallas_call(
        functools.partial(_combine_final_kernel, tiles_per_batch=t // tm),
        grid=(n // tm,),
        in_specs=in_specs,
        out_specs=pl.BlockSpec(memory_space=pl.ANY),
        out_shape=jax.ShapeDtypeStruct((b, t - N_META, d), F32),
        scratch_shapes=[pltpu.VMEM((2, tm, d), F32), pltpu.SemaphoreType.DMA((2, 2))],
        compiler_params=_params("arbitrary"),
        name="moe_combine_final",
    )(x2d, gates, out2, out2)


def _even_layer(x, norm_g, w_in, conv_w, conv_b, b_i, b_f, a_norm, w_gate2, b_gate, b_norm, w_out):
    b, t, d = x.shape
    n = b * t
    a_w = 2 * A_HEADS * A_DK + 2 * A_HEADS * A_DV
    g_w = 2 * A_HEADS
    b_w = 2 * B_HEADS * B_DK + 2 * B_HEADS * B_DV
    main = a_w + b_w
    w = _even_weight(w_in, a_w, g_w, b_w, GATE_RANK)
    z = _normproj(x.reshape(n, d), norm_g, w).reshape(b, t, main + MXU_DIM)

    ya = _mlstm(z, main, conv_w, conv_b, b_i, b_f, a_norm)

    wg = jnp.zeros((B_HEADS, MXU_DIM, B_DK), F32).at[:, g_w:g_w + GATE_RANK, :].set(
        w_gate2.reshape(GATE_RANK, B_HEADS, B_DK).transpose(1, 0, 2))
    dummy = jnp.zeros((1, B_HEADS * B_DK), F32)
    hp = 2
    hspec = pl.BlockSpec((1, hp * B_DK), lambda bi, g: (0, g))
    yb = _gla_call(
        z, B_HEADS, hp, B_DK, B_DV,
        (a_w, a_w + B_HEADS * B_DK, a_w + 2 * B_HEADS * B_DK, a_w + 2 * B_HEADS * B_DK + B_HEADS * B_DV, main),
        MXU_DIM,
        ((wg, pl.BlockSpec((hp, MXU_DIM, B_DK), lambda bi, g: (g, 0, 0))),
         (b_gate.reshape(1, -1), hspec), (dummy, hspec)),
        b_norm, "gla")
    return _outproj(ya.reshape(n, -1), yb.reshape(n, -1), w_out.astype(BF16), x.reshape(n, d)).reshape(b, t, d)


def _odd_layer(x, lb, norm_g, w_in, c_norm, q_a_norm, w_q_up, kv_a_norm, w_kv_up, q_norm, k_norm, w_out):
    b, t, d = x.shape
    n = b * t
    c_w = 2 * C_HEADS * C_DK + 2 * C_HEADS * C_DV
    swap = (jnp.arange(D_ROPE) + D_ROPE // 2) % D_ROPE
    kr0 = c_w + Q_LORA + KV_LORA
    used = kr0 + 2 * D_ROPE
    total = -(-used // MXU_DIM) * MXU_DIM
    z2 = _normproj(x.reshape(n, d), norm_g, _odd_weight(w_in, kr0))
    z = z2.reshape(b, t, total)

    hp = 4
    hspec = pl.BlockSpec((1, hp * C_DK), lambda bi, g: (0, g))
    yc = _gla_call(
        z, C_HEADS, hp, C_DK, C_DV,
        (0, C_HEADS * C_DK, 2 * C_HEADS * C_DK, 2 * C_HEADS * C_DK + C_HEADS * C_DV, C_HEADS * C_DK),
        C_DK,
        ((jnp.log(lb).reshape(1, -1), hspec), (jnp.log1p(-lb).reshape(1, -1), hspec), ((1.0 - lb).reshape(1, -1), hspec)),
        c_norm, "hgrn")

    dq = D_NOPE + D_ROPE
    wq = w_q_up.reshape(Q_LORA, D_HEADS, dq)
    wq_rope = wq[:, :, D_NOPE:]
    wq_p = jnp.concatenate([wq[:, :, :D_NOPE].reshape(Q_LORA, -1),
                            jnp.concatenate([wq_rope, wq_rope[:, :, swap]], axis=-1).reshape(Q_LORA, -1)],
                           axis=1).astype(BF16)
    wkv = w_kv_up.reshape(KV_LORA, D_HEADS, D_NOPE + D_V)
    wkv_p = jnp.concatenate([wkv[:, :, :D_NOPE].reshape(KV_LORA, -1), wkv[:, :, D_NOPE:].reshape(KV_LORA, -1)],
                            axis=1).astype(BF16)
    pos = jnp.arange(t, dtype=F32)
    half = D_ROPE // 2
    inv = ROPE_THETA ** (-jnp.arange(half, dtype=F32) / half)
    ang = pos[:, None] * inv[None, :]
    cos, sin = jnp.cos(ang), jnp.sin(ang)
    tab = jnp.concatenate([cos, cos, -sin, sin], axis=1)
    pair = lambda g: jnp.concatenate([g[D_NOPE:], g[D_NOPE:][swap]]).reshape(1, LANES)
    yd = _mla(z, c_w, tab, q_a_norm, kv_a_norm, wq_p, wkv_p, q_norm[:D_NOPE].reshape(1, LANES), pair(q_norm),
              k_norm[:D_NOPE].reshape(1, LANES), pair(k_norm))
    return _outproj(yc.reshape(n, -1), yd.reshape(n, -1), w_out.astype(BF16), x.reshape(n, d)).reshape(b, t, d)


def kernel(x, meta_tokens, ab_norm, ab_w_in, a_conv_w, a_conv_b, a_b_i, a_b_f, a_head_norm, b_w_gate2, b_b_gate, b_head_norm, ab_w_out, cd_norm, cd_w_in, c_lower_bound, c_head_norm, d_q_a_norm, d_w_q_up, d_kv_a_norm, d_w_kv_up, d_q_norm, d_k_norm, cd_w_out, moe_norm, moe_w_group, moe_b_group, moe_w_expert, moe_b_expert, moe_w1, moe_w3, moe_w2):
    b = x.shape[0]
    depth = moe_norm.shape[0]
    h = jnp.concatenate([jnp.broadcast_to(meta_tokens.astype(x.dtype)[None], (b, N_META, D_MODEL)), x], axis=1)
    t = h.shape[1]
    lb_cum = jnp.cumsum(jax.nn.softmax(c_lower_bound.astype(F32), axis=0), axis=0)
    lower_bounds = lb_cum - lb_cum[0]
    for layer in range(depth):
        j = layer // 2
        if layer % 2 == 0:
            h = _even_layer(h, ab_norm[j], ab_w_in[j], a_conv_w[j], a_conv_b[j], a_b_i[j], a_b_f[j], a_head_norm[j],
                            b_w_gate2[j], b_b_gate[j], b_head_norm[j], ab_w_out[j])
        else:
            h = _odd_layer(h, lower_bounds[layer], cd_norm[j], cd_w_in[j], c_head_norm[j], d_q_a_norm[j],
                           d_w_q_up[j], d_kv_a_norm[j], d_w_kv_up[j], d_q_norm[j], d_k_norm[j], cd_w_out[j])
        last = layer == depth - 1
        h = _moe(h.reshape(b * t, D_MODEL), moe_norm[layer], moe_w_group[layer], moe_b_group[layer],
                 moe_w_expert[layer], moe_b_expert[layer], moe_w1, moe_w3, moe_w2, layer,
                 final_shape=(b, t) if last else None)
        if not last:
            h = h.reshape(b, t, D_MODEL)
    return h
```

```python
import functools

import jax
import jax.numpy as jnp
from jax import lax
from jax.experimental import pallas as pl
from jax.experimental.pallas import tpu as pltpu

F32 = jnp.float32
BF16 = jnp.bfloat16

D_MODEL = 2048
N_META = 16
CHUNK = 64
CONV_K = 4
EPS = 1e-6
A_HEADS, A_DK, A_DV = 4, 128, 256
B_HEADS, B_DK, B_DV = 4, 128, 256
GATE_RANK = 16
GATE_TAU = 16.0
C_HEADS, C_DK, C_DV = 8, 128, 128
D_HEADS, D_NOPE, D_ROPE, D_V = 8, 128, 64, 128
Q_LORA, KV_LORA = 512, 256
ROPE_THETA = 10000.0
N_GROUPS, EXPERTS_PER_GROUP = 4, 8
N_EXPERTS = N_GROUPS * EXPERTS_PER_GROUP
TOP_K = 2
D_EXPERT = 512

LANES = 128
MXU_DIM = 256
BF16_ROWS = 16
VMEM_LIMIT = 56 * 1024 * 1024
MOE_BLOCK = MXU_DIM
ROW_TILE = 688
ROW_TILE_BF16 = 2 * ROW_TILE
COL_TILE = 1280
COL_TILE_OUT = 1024
PREP_ROWS = 768
ATT_BLOCK = 256
ATT_PAD = ATT_BLOCK - N_META

_NT = (((1,), (1,)), ((), ()))
_TN = (((0,), (0,)), ((), ()))


def _dot(a, b):
    return jnp.dot(a, b, preferred_element_type=F32)


def _dot_nt(a, b):
    return lax.dot_general(a, b, _NT, preferred_element_type=F32)


def _dot_tn(a, b):
    return lax.dot_general(a, b, _TN, preferred_element_type=F32)


def _bf(x):
    return x.astype(BF16)


def _split3(x):
    hi = _bf(x)
    rest = x - hi.astype(F32)
    mid = _bf(rest)
    return hi, mid, _bf(rest - mid.astype(F32))


def _log_sigmoid(x):
    return jnp.minimum(x, 0.0) - jnp.log1p(jnp.exp(-jnp.abs(x)))


def _sigmoid(x):
    return 1.0 / (1.0 + jnp.exp(-x))


def _silu(x):
    return x * _sigmoid(x)


def _row_tile(m, cap):
    best = None
    for t in range(BF16_ROWS, min(m, cap) + 1, BF16_ROWS):
        if m % t == 0:
            best = t
    assert best is not None, m
    return best


def _col_tile(n, cap):
    best = None
    for t in range(MXU_DIM, min(n, cap) + 1, MXU_DIM):
        if n % t == 0:
            best = t
    assert best is not None, n
    return best


def _params(*sem):
    return pltpu.CompilerParams(dimension_semantics=sem, vmem_limit_bytes=VMEM_LIMIT)


_RELAYOUT_ROWS = 256


def _even_weight_kernel(wa_ref, wb_ref, wc_ref, o_ref, *, n_plain, n_shift, shift, gate_cols):
    ob = pl.program_id(0)
    rows = o_ref.shape[0]
    chunks = rows // _RELAYOUT_ROWS

    @pl.when(ob < n_plain)
    def _():
        def body(c, carry):
            r = pl.ds(pl.multiple_of(c * _RELAYOUT_ROWS, _RELAYOUT_ROWS), _RELAYOUT_ROWS)
            o_ref[r, :] = _bf(wa_ref[r, :])
            return carry

        lax.fori_loop(0, chunks, body, 0)

    @pl.when((ob >= n_plain) & (ob < n_plain + n_shift))
    def _():
        def body(c, carry):
            r = pl.ds(pl.multiple_of(c * _RELAYOUT_ROWS, _RELAYOUT_ROWS), _RELAYOUT_ROWS)
            wide = jnp.concatenate([wa_ref[r, :], wb_ref[r, :]], axis=1)
            o_ref[r, :] = _bf(wide[:, shift:shift + MXU_DIM])
            return carry

        lax.fori_loop(0, chunks, body, 0)

    @pl.when(ob == n_plain + n_shift)
    def _():
        lane = lax.broadcasted_iota(jnp.int32, (_RELAYOUT_ROWS, LANES), 1)

        def body(c, carry):
            r = pl.ds(pl.multiple_of(c * _RELAYOUT_ROWS, _RELAYOUT_ROWS), _RELAYOUT_ROWS)
            first = jnp.where(lane < shift, wc_ref[r, :], jnp.where(lane < gate_cols, wb_ref[r, :], 0.0))
            o_ref[r, :] = _bf(jnp.concatenate([first, jnp.zeros_like(first)], axis=1))
            return carry

        lax.fori_loop(0, chunks, body, 0)


def _even_weight(w_in, a_w, g_w, b_w, rank):
    d = w_in.shape[0]
    assert a_w % MXU_DIM == 0 and b_w % MXU_DIM == 0 and g_w + rank <= LANES and d % _RELAYOUT_ROWS == 0
    n_plain, n_shift = a_w // MXU_DIM, b_w // MXU_DIM
    n_out = n_plain + n_shift + 1
    last = n_out - 1

    def b_index(ob):
        return (0, jnp.where(ob < n_plain, 0, jnp.where(ob < last, 2 * (ob + 1), (a_w + g_w + b_w) // LANES)))

    return pl.pallas_call(
        functools.partial(_even_weight_kernel, n_plain=n_plain, n_shift=n_shift, shift=g_w, gate_cols=g_w + rank),
        grid=(n_out,),
        in_specs=[
            pl.BlockSpec((d, MXU_DIM), lambda ob: (0, jnp.minimum(ob, last - 1))),
            pl.BlockSpec((d, LANES), b_index),
            pl.BlockSpec((d, LANES), lambda ob: (0, a_w // LANES)),
        ],
        out_specs=pl.BlockSpec((d, MXU_DIM), lambda ob: (0, ob)),
        out_shape=jax.ShapeDtypeStruct((d, n_out * MXU_DIM), BF16),
        compiler_params=_params("parallel"),
        name="even_weight",
    )(w_in, w_in, w_in)


def _odd_weight_kernel(w_ref, o_ref, *, n_plain):
    ob = pl.program_id(0)
    chunks = o_ref.shape[0] // _RELAYOUT_ROWS
    half = D_ROPE // 2

    def body(c, carry):
        r = pl.ds(pl.multiple_of(c * _RELAYOUT_ROWS, _RELAYOUT_ROWS), _RELAYOUT_ROWS)
        w = w_ref[r, :]

        @pl.when(ob < n_plain)
        def _():
            o_ref[r, :] = _bf(w)

        @pl.when(ob == n_plain)
        def _():
            pair = jnp.concatenate([w[:, :D_ROPE], w[:, half:D_ROPE], w[:, :half]], axis=1)
            o_ref[r, :] = _bf(jnp.concatenate([pair, jnp.zeros_like(pair)], axis=1))

        return carry

    lax.fori_loop(0, chunks, body, 0)


def _odd_weight(w_in, kr0):
    d = w_in.shape[0]
    assert kr0 % MXU_DIM == 0 and w_in.shape[1] == kr0 + D_ROPE and d % _RELAYOUT_ROWS == 0
    n_plain = kr0 // MXU_DIM
    return pl.pallas_call(
        functools.partial(_odd_weight_kernel, n_plain=n_plain),
        grid=(n_plain + 1,),
        in_specs=[pl.BlockSpec((d, MXU_DIM), lambda ob: (0, ob))],
        out_specs=pl.BlockSpec((d, MXU_DIM), lambda ob: (0, ob)),
        out_shape=jax.ShapeDtypeStruct((d, kr0 + MXU_DIM), BF16),
        compiler_params=_params("parallel"),
        name="odd_weight",
    )(w_in)


def _normproj_kernel(x_ref, g_ref, w_ref, o_ref, xs_ref):
    tm = xs_ref.shape[0]

    @pl.when(pl.program_id(1) == 0)
    def _():
        def body(c, carry):
            r0 = pl.multiple_of(c * BF16_ROWS, BF16_ROWS)
            x = x_ref[pl.ds(r0, BF16_ROWS), :]
            ms = jnp.mean(x * x, axis=-1, keepdims=True)
            xs_ref[pl.ds(r0, BF16_ROWS), :] = _bf(x * lax.rsqrt(ms + EPS) * g_ref[...])
            return carry

        lax.fori_loop(0, tm // BF16_ROWS, body, 0, unroll=8)

    o_ref[...] = _dot(xs_ref[...], w_ref[...]).astype(o_ref.dtype)


def _normproj(x2d, gain, w):
    m, k = x2d.shape
    n = w.shape[1]
    tm = _row_tile(m, ROW_TILE)
    tn = _col_tile(n, COL_TILE)
    return pl.pallas_call(
        _normproj_kernel,
        grid=(m // tm, n // tn),
        in_specs=[
            pl.BlockSpec((tm, k), lambda i, j: (i, 0)),
            pl.BlockSpec((1, k), lambda i, j: (0, 0)),
            pl.BlockSpec((k, tn), lambda i, j: (0, j)),
        ],
        out_specs=pl.BlockSpec((tm, tn), lambda i, j: (i, j)),
        out_shape=jax.ShapeDtypeStruct((m, n), F32),
        scratch_shapes=[pltpu.VMEM((tm, k), BF16)],
        compiler_params=_params("parallel", "arbitrary"),
        name="normproj",
    )(x2d, gain.reshape(1, k).astype(F32), w)


def _outproj_kernel(ya_ref, yb_ref, w_ref, r_ref, o_ref):
    ka = ya_ref.shape[1]
    acc = _dot(ya_ref[...], w_ref[:ka, :]) + _dot(yb_ref[...], w_ref[ka:, :])
    o_ref[...] = r_ref[...] + acc


def _outproj(ya, yb, w, res):
    m, ka = ya.shape
    kb = yb.shape[1]
    n = w.shape[1]
    tm = _row_tile(m, ROW_TILE_BF16)
    tn = _col_tile(n, COL_TILE_OUT)
    return pl.pallas_call(
        _outproj_kernel,
        grid=(m // tm, n // tn),
        in_specs=[
            pl.BlockSpec((tm, ka), lambda i, j: (i, 0)),
            pl.BlockSpec((tm, kb), lambda i, j: (i, 0)),
            pl.BlockSpec((ka + kb, tn), lambda i, j: (0, j)),
            pl.BlockSpec((tm, tn), lambda i, j: (i, j)),
        ],
        out_specs=pl.BlockSpec((tm, tn), lambda i, j: (i, j)),
        out_shape=jax.ShapeDtypeStruct((m, n), F32),
        compiler_params=_params("parallel", "arbitrary"),
        name="outproj",
    )(ya, yb, w, res)


def _mlstm_kernel(bi_ref, bf_ref, q_ref, k_ref, v_ref, og_ref, gt_ref,
                  cwq_ref, cwk_ref, cbq_ref, cbk_ref, hn_ref, o_ref, c_ref, n_ref, m_ref, *, heads, group):
    t_total = q_ref.shape[1]
    n_chunks = (t_total - N_META) // CHUNK
    head0 = pl.program_id(1) * heads

    c_ref[...] = jnp.zeros_like(c_ref)
    n_ref[...] = jnp.zeros_like(n_ref)
    m_ref[...] = jnp.zeros_like(m_ref)

    def conv(taps, cw, cb, length):
        y = cb
        for j in range(CONV_K):
            y = y + taps[j] * cw[j:j + 1, :]
        return _silu(y)

    def window_taps(win, length):
        return [win[8 - (CONV_K - 1) + j:8 - (CONV_K - 1) + j + length, :] for j in range(CONV_K)]

    hs = range(heads)
    kcol = [slice(hh * A_DK, (hh + 1) * A_DK) for hh in hs]
    vcol = [slice(hh * A_DV, (hh + 1) * A_DV) for hh in hs]

    def gates(hh, blk, blk_parts, length, causal, upper):
        head = head0 + hh
        b_i = bi_ref[head]
        b_f = bf_ref[head]
        lane = lax.broadcasted_iota(jnp.int32, (length, LANES), 1)
        ig_c = jnp.sum(jnp.where(lane == head, blk, 0.0), axis=1, keepdims=True) + b_i
        lf_c = _log_sigmoid(jnp.sum(jnp.where(lane == A_HEADS + head, blk, 0.0), axis=1, keepdims=True) + b_f)
        sel_r = lax.broadcasted_iota(jnp.int32, (8, LANES), 0)
        sel_l = lax.broadcasted_iota(jnp.int32, (8, LANES), 1)
        sel = jnp.where(sel_l == head + A_HEADS * sel_r, 1.0, 0.0).astype(BF16)
        rows = _dot_nt(sel, blk_parts[0]) + (_dot_nt(sel, blk_parts[1]) + _dot_nt(sel, blk_parts[2]))
        ig_r = rows[0:1, :] + b_i
        lf_r = _log_sigmoid(rows[1:2, :] + b_f)
        b_c = jnp.sum(jnp.where(causal, lf_r, 0.0), axis=1, keepdims=True)
        b_r = jnp.sum(jnp.where(upper, lf_c, 0.0), axis=0, keepdims=True)
        b_end = b_c[length - 1:length, :]
        w_end = b_end - b_c + ig_c
        m_loc = jnp.max(w_end, axis=0, keepdims=True)
        d = jnp.where(causal, b_c - b_r + ig_r, -jnp.inf)
        return b_c, b_end, jnp.exp(w_end - m_loc), m_loc, d, jnp.max(d, axis=1, keepdims=True)

    def local_stage(o, length, qwin, kwin):
        row = lax.broadcasted_iota(jnp.int32, (length, length), 0)
        col = lax.broadcasted_iota(jnp.int32, (length, length), 1)
        causal = col <= row
        blk = gt_ref[0, pl.ds(o, length), 0:LANES]
        blk_parts = _split3(blk)
        gt = [gates(hh, blk, blk_parts, length, causal, row <= col) for hh in hs]
        q = [conv(qwin[hh], cwq_ref[:, kcol[hh]], cbq_ref[:, kcol[hh]], length) for hh in hs]
        k = [conv(kwin[hh], cwk_ref[:, kcol[hh]], cbk_ref[:, kcol[hh]], length) * (A_DK ** -0.5) for hh in hs]
        vb = [_bf(v_ref[0, pl.ds(o, length), vcol[hh]]) for hh in hs]
        qb = [_bf(q[hh]) for hh in hs]
        k_w = [k[hh] * gt[hh][2] for hh in hs]
        qk = [_dot_nt(qb[hh], _bf(k[hh])) for hh in hs]
        c_loc = [_dot_tn(_bf(k_w[hh]), vb[hh]) for hh in hs]
        n_loc = [jnp.sum(k_w[hh], axis=0, keepdims=True) for hh in hs]
        return gt, q, qb, vb, qk, c_loc, n_loc

    def state_stage(o, length, staged, state):
        gt, q, qb, vb, qk, c_loc, n_loc = staged
        c_in, n_in, m_in = state
        q_c = [_dot(qb[hh], _bf(c_in[hh])) for hh in hs]
        s, a_t, m_t, q_n, c_out, n_out, m_out = [], [], [], [], [], [], []
        for hh in hs:
            b_c, b_end, _, m_loc, d, d_max = gt[hh]
            inter = b_c + m_in[hh]
            m_t.append(jnp.maximum(inter, d_max))
            s.append(qk[hh] * jnp.exp(d - m_t[hh]))
            a_t.append(jnp.exp(inter - m_t[hh]))
            m_new = jnp.maximum(b_end + m_in[hh], m_loc)
            a = jnp.exp(b_end + m_in[hh] - m_new)
            c = jnp.exp(m_loc - m_new)
            c_out.append(a * c_in[hh] + c * c_loc[hh])
            n_out.append(a * n_in[hh] + c * n_loc[hh])
            m_out.append(m_new)
            q_n.append(jnp.sum(q[hh] * n_in[hh], axis=1, keepdims=True))
        num = [_dot(_bf(s[hh]), vb[hh]) + a_t[hh] * q_c[hh] for hh in hs]
        for hh in hs:
            den = jnp.sum(s[hh], axis=1, keepdims=True) + a_t[hh] * q_n[hh]
            h = num[hh] / jnp.maximum(jnp.abs(den), jnp.exp(-m_t[hh]))
            hn = h * lax.rsqrt(jnp.mean(h * h, axis=-1, keepdims=True) + EPS) * hn_ref[:, vcol[hh]]
            y = _sigmoid(og_ref[0, pl.ds(o, length), vcol[hh]]) * hn
            o_ref[0, pl.ds(o, length), vcol[hh]] = y.astype(o_ref.dtype)
        return c_out, n_out, m_out

    def sweep(offsets, length, qwins, kwins):
        staged = [local_stage(o, length, qw, kw) for o, qw, kw in zip(offsets, qwins, kwins)]
        state = ([c_ref[hh] for hh in hs], [n_ref[hh] for hh in hs], [m_ref[hh] for hh in hs])
        for o, stg in zip(offsets, staged):
            state = state_stage(o, length, stg, state)
        for hh in hs:
            c_ref[hh] = state[0][hh]
            n_ref[hh] = state[1][hh]
            m_ref[hh] = state[2][hh]

    zeros8 = jnp.zeros((8, A_DK), F32)
    sweep([0], N_META,
          [[window_taps(jnp.concatenate([zeros8, q_ref[0, 0:N_META, kcol[hh]]], axis=0), N_META) for hh in hs]],
          [[window_taps(jnp.concatenate([zeros8, k_ref[0, 0:N_META, kcol[hh]]], axis=0), N_META) for hh in hs]])
    assert n_chunks % group == 0

    def body(c, carry):
        offsets = [pl.multiple_of(N_META + (c * group + j) * CHUNK, BF16_ROWS) for j in range(group)]

        starts = [pl.multiple_of(N_META - 8 + (c * group + j) * CHUNK, 8) for j in range(group)]

        def shifted(ref, w0, hh):
            return window_taps(ref[0, pl.ds(w0, CHUNK + 8), kcol[hh]], CHUNK)

        sweep(offsets, CHUNK,
              [[shifted(q_ref, w0, hh) for hh in hs] for w0 in starts],
              [[shifted(k_ref, w0, hh) for hh in hs] for w0 in starts])
        return carry

    lax.fori_loop(0, n_chunks // group, body, 0)


def _mlstm(z, gate_col, conv_w, conv_b, b_i, b_f, head_norm, heads=2):
    b, t, _ = z.shape
    hk = A_HEADS * A_DK
    wk, wv = heads * A_DK, heads * A_DV
    smem = pl.BlockSpec(memory_space=pltpu.SMEM)
    col = lambda width, off: (lambda bi, g: (bi, 0, off // width + g))
    return pl.pallas_call(
        functools.partial(_mlstm_kernel, heads=heads, group=4),
        grid=(b, A_HEADS // heads),
        in_specs=[
            smem, smem,
            pl.BlockSpec((1, t, wk), col(wk, 0)),
            pl.BlockSpec((1, t, wk), col(wk, hk)),
            pl.BlockSpec((1, t, wv), col(wv, 2 * hk)),
            pl.BlockSpec((1, t, wv), col(wv, 2 * hk + A_HEADS * A_DV)),
            pl.BlockSpec((1, t, MXU_DIM), lambda bi, g: (bi, 0, gate_col // MXU_DIM)),
            pl.BlockSpec((CONV_K, wk), lambda bi, g: (0, g)),
            pl.BlockSpec((CONV_K, wk), lambda bi, g: (0, A_HEADS // heads + g)),
            pl.BlockSpec((1, wk), lambda bi, g: (0, g)),
            pl.BlockSpec((1, wk), lambda bi, g: (0, A_HEADS // heads + g)),
            pl.BlockSpec((1, wv), lambda bi, g: (0, g)),
        ],
        out_specs=pl.BlockSpec((1, t, wv), lambda bi, g: (bi, 0, g)),
        out_shape=jax.ShapeDtypeStruct((b, t, A_HEADS * A_DV), BF16),
        scratch_shapes=[pltpu.VMEM((heads, A_DK, A_DV), F32), pltpu.VMEM((heads, 1, A_DK), F32),
                        pltpu.VMEM((heads, 1, 1), F32)],
        compiler_params=_params("parallel", "parallel"),
        name="mlstm",
    )(b_i, b_f, z, z, z, z, z, conv_w, conv_w, conv_b.reshape(1, -1), conv_b.reshape(1, -1),
      head_norm.reshape(1, -1))


def _gla_kernel(q_ref, k_ref, v_ref, og_ref, g_ref, p0_ref, p1_ref, p2_ref, hn_ref, o_ref, st_ref, *pre_ref,
                mode, heads, group):
    t_total = q_ref.shape[1]
    n_chunks = (t_total - N_META) // CHUNK
    dv, dk = st_ref.shape[1:]
    st_ref[...] = jnp.zeros_like(st_ref)

    hs = range(heads)
    kcol = [slice(hh * dk, (hh + 1) * dk) for hh in hs]
    vcol = [slice(hh * dv, (hh + 1) * dv) for hh in hs]

    if mode == "gla":
        gate = g_ref[0]
        gate_hi = _bf(gate)
        gate_lo = _bf(gate - gate_hi.astype(F32))
        for hh in hs:
            w = p0_ref[hh]
            w_hi = _bf(w)
            w_lo = _bf(w - w_hi.astype(F32))
            both = _dot(gate_hi, jnp.concatenate([w_hi, w_lo], axis=1))
            pre_ref[0][hh] = (both[:, :dk] + both[:, dk:]) + _dot(gate_lo, w_hi) + p1_ref[:, kcol[hh]]

    def gate_inputs(hh, o, length):
        q = q_ref[0, pl.ds(o, length), kcol[hh]]
        if mode == "gla":
            pre = pre_ref[0][hh, pl.ds(o, length), :]
            return q * (dk ** -0.5), k_ref[0, pl.ds(o, length), kcol[hh]], _log_sigmoid(pre) / GATE_TAU
        fpre = g_ref[0, pl.ds(o, length), kcol[hh]]
        a = p0_ref[:, kcol[hh]]
        bb = p1_ref[:, kcol[hh]] + _log_sigmoid(fpre)
        lg = jnp.maximum(a, bb) + jnp.log1p(jnp.exp(-jnp.abs(a - bb)))
        return q, p2_ref[:, kcol[hh]] * _sigmoid(-fpre), lg

    def cumsum_time(tri, lg):
        parts = _dot(tri, jnp.concatenate(_split3(lg), axis=1))
        return parts[:, :dk] + (parts[:, dk:2 * dk] + parts[:, 2 * dk:])

    def local_stage(o, length):
        row = lax.broadcasted_iota(jnp.int32, (length, length), 0)
        col = lax.broadcasted_iota(jnp.int32, (length, length), 1)
        causal = col <= row
        tri = jnp.where(causal, 1.0, 0.0).astype(BF16)
        qkl = [gate_inputs(hh, o, length) for hh in hs]
        vb = [_bf(v_ref[0, pl.ds(o, length), vcol[hh]]) for hh in hs]
        g = [cumsum_time(tri, qkl[hh][2]) for hh in hs]
        g_end = [g[hh][length - 1:length, :] for hh in hs]
        g_mid = [g[hh][length // 2:length // 2 + 1, :] for hh in hs]
        s = [_dot_nt(_bf(qkl[hh][0] * jnp.exp(g[hh] - g_mid[hh])), _bf(qkl[hh][1] * jnp.exp(g_mid[hh] - g[hh])))
             for hh in hs]
        q_dec = [_bf(qkl[hh][0] * jnp.exp(g[hh])) for hh in hs]
        local = [_dot_tn(vb[hh], _bf(qkl[hh][1] * jnp.exp(g_end[hh] - g[hh]))) for hh in hs]
        intra = [_dot(_bf(jnp.where(causal, s[hh], 0.0)), vb[hh]) for hh in hs]
        return q_dec, [jnp.exp(ge) for ge in g_end], local, intra

    def state_stage(o, length, staged, st_in):
        q_dec, decay, local, intra = staged
        inter = [_dot_nt(q_dec[hh], _bf(st_in[hh])) for hh in hs]
        st_out = [st_in[hh] * decay[hh] + local[hh] for hh in hs]
        for hh in hs:
            out = intra[hh] + inter[hh]
            hn = out * lax.rsqrt(jnp.mean(out * out, axis=-1, keepdims=True) + EPS) * hn_ref[:, vcol[hh]]
            og = og_ref[0, pl.ds(o, length), vcol[hh]]
            gate = _silu(og) if mode == "gla" else _sigmoid(og)
            o_ref[0, pl.ds(o, length), vcol[hh]] = (gate * hn).astype(o_ref.dtype)
        return st_out

    def sweep(offsets, length):
        staged = [local_stage(o, length) for o in offsets]
        st = [st_ref[hh] for hh in hs]
        for o, stg in zip(offsets, staged):
            st = state_stage(o, length, stg, st)
        for hh in hs:
            st_ref[hh] = st[hh]

    sweep([0], N_META)
    assert n_chunks % group == 0

    def body(c, carry):
        o = pl.multiple_of(N_META + c * (group * CHUNK), BF16_ROWS)
        sweep([pl.multiple_of(o + j * CHUNK, BF16_ROWS) for j in range(group)], CHUNK)
        return carry

    lax.fori_loop(0, n_chunks // group, body, 0)


def _gla_call(z, n_heads, heads, dk, dv, blocks, gate_width, params, head_norm, mode):
    b, t, _ = z.shape
    q0, k0, v0, og0, g0 = blocks
    zspec = lambda width, off, grouped=True: pl.BlockSpec(
        (1, t, width), (lambda bi, g: (bi, 0, off // width + (g if grouped else 0))))
    (p0, s0), (p1, s1), (p2, s2) = params
    gate_spec = zspec(gate_width, g0, grouped=False) if mode == "gla" else zspec(heads * dk, g0)
    return pl.pallas_call(
        functools.partial(_gla_kernel, mode=mode, heads=heads, group=8),
        grid=(b, n_heads // heads),
        in_specs=[
            zspec(heads * dk, q0), zspec(heads * dk, k0), zspec(heads * dv, v0), zspec(heads * dv, og0),
            gate_spec, s0, s1, s2,
            pl.BlockSpec((1, heads * dv), lambda bi, g: (0, g)),
        ],
        out_specs=pl.BlockSpec((1, t, heads * dv), lambda bi, g: (bi, 0, g)),
        out_shape=jax.ShapeDtypeStruct((b, t, n_heads * dv), BF16),
        scratch_shapes=[pltpu.VMEM((heads, dv, dk), F32)] + (
            [pltpu.VMEM((heads, t, dk), F32)] if mode == "gla" else []),
        compiler_params=_params("parallel", "parallel"),
        name="gla_" + mode,
    )(z, z, z, z, z, p0, p1, p2, head_norm.reshape(1, -1))


def _mla_kernel(cq_ref, ckv_ref, kr_ref, tab_ref, gqa_ref, gkva_ref, wqn_ref, wqr_ref, wkn_ref, wv_ref,
                gqn_ref, gqr_ref, gkn_ref, gkr_ref, o_ref, qf_ref, kf_ref, vf_ref, *, heads):
    t_total = cq_ref.shape[1]
    n_blocks = (ATT_PAD + t_total) // ATT_BLOCK
    dqk = D_NOPE + D_ROPE
    scale = dqk ** -0.5
    rows = _row_tile(t_total, PREP_ROWS)

    hs = range(heads)
    col = [slice(hh * LANES, (hh + 1) * LANES) for hh in hs]
    for hh in hs:
        qf_ref[hh, 0:ATT_PAD, :] = jnp.zeros((ATT_PAD, 2 * LANES), BF16)
        kf_ref[hh, 0:ATT_PAD, :] = jnp.zeros((ATT_PAD, 2 * LANES), BF16)
        vf_ref[hh, 0:ATT_PAD, :] = jnp.zeros((ATT_PAD, D_V), BF16)

    def rope_pair(x, gains, tab):
        p = x * gains * tab
        return p + pltpu.roll(p, D_ROPE, 1)

    def latent_norm(ref, gain_ref, r0):
        u = ref[0, pl.ds(r0, rows), :]
        return _bf(u * lax.rsqrt(jnp.mean(u * u, axis=-1, keepdims=True) + EPS) * gain_ref[...])

    def up_project(c):
        r0 = c * rows
        cq = latent_norm(cq_ref, gqa_ref, r0)
        ckv = latent_norm(ckv_ref, gkva_ref, r0)
        return _dot(cq, wqn_ref[...]), _dot(cq, wqr_ref[...]), _dot(ckv, wkn_ref[...]), _dot(ckv, wv_ref[...])

    def prep(c, hh, projected):
        r0 = c * rows
        dst = ATT_PAD + c * rows
        tab = tab_ref[pl.ds(r0, rows), :]
        qn = projected[0][:, col[hh]]
        qr = projected[1][:, col[hh]]
        ssq = jnp.sum(qn * qn + 0.5 * (qr * qr), axis=-1, keepdims=True)
        rq = lax.rsqrt(ssq / dqk + EPS) * scale
        qf_ref[hh, pl.ds(dst, rows), 0:LANES] = _bf(qn * gqn_ref[...] * rq)
        qf_ref[hh, pl.ds(dst, rows), LANES:2 * LANES] = _bf(qr * gqr_ref[...] * tab * rq)
        kn = projected[2][:, col[hh]]
        kr = kr_ref[0, pl.ds(r0, rows), :]
        ssk = jnp.sum(kn * kn + 0.5 * (kr * kr), axis=-1, keepdims=True)
        rk = lax.rsqrt(ssk / dqk + EPS)
        kf_ref[hh, pl.ds(dst, rows), 0:LANES] = _bf(kn * gkn_ref[...] * rk)
        kf_ref[hh, pl.ds(dst, rows), LANES:2 * LANES] = _bf(rope_pair(kr, gkr_ref[...], tab) * rk)
        vf_ref[hh, pl.ds(dst, rows), :] = _bf(projected[3][:, col[hh]])

    for c in range(t_total // rows):
        projected = up_project(c)
        for hh in hs:
            prep(c, hh, projected)

    qpos = lax.broadcasted_iota(jnp.int32, (ATT_BLOCK, ATT_BLOCK), 0)
    kpos = lax.broadcasted_iota(jnp.int32, (ATT_BLOCK, ATT_BLOCK), 1)
    neg = -jnp.inf

    def scores(item):
        qi, hh = item
        q = qf_ref[hh, qi * ATT_BLOCK:(qi + 1) * ATT_BLOCK, :]
        return _dot_nt(q, kf_ref[hh, 0:(qi + 1) * ATT_BLOCK, :])

    items = [(qi, hh) for qi in range(n_blocks) for hh in hs]
    s_next = scores(items[0])
    for idx, (qi, hh) in enumerate(items):
        s = s_next
        if idx + 1 < len(items):
            s_next = scores(items[idx + 1])
        parts = [s[:, j * ATT_BLOCK:(j + 1) * ATT_BLOCK] for j in range(qi + 1)]
        parts[0] = jnp.where(kpos >= ATT_PAD, parts[0], neg)
        parts[qi] = jnp.where(kpos <= qpos, parts[qi], neg)
        top = functools.reduce(jnp.maximum, parts)
        m = jnp.max(top, axis=-1, keepdims=True)
        if qi == 0:
            m = jnp.where(m == neg, 0.0, m)
        probs = [jnp.exp(part - m) for part in parts]
        l = jnp.sum(functools.reduce(jnp.add, probs), axis=-1, keepdims=True)
        pv = _dot(jnp.concatenate([_bf(pr) for pr in probs], axis=1), vf_ref[hh, 0:(qi + 1) * ATT_BLOCK, :])
        if qi == 0:
            out = pv / jnp.where(l == 0.0, 1.0, l)
            o_ref[0, 0:N_META, col[hh]] = out[ATT_PAD:, :].astype(o_ref.dtype)
        else:
            dst = qi * ATT_BLOCK - ATT_PAD
            o_ref[0, dst:dst + ATT_BLOCK, col[hh]] = (pv / l).astype(o_ref.dtype)


def _mla(z, cq_col, tab, q_a_norm, kv_a_norm, wq, wkv, gqn, gqr, gkn, gkr):
    b, t, _ = z.shape
    assert (ATT_PAD + t) % ATT_BLOCK == 0 and t % BF16_ROWS == 0
    tp = ATT_PAD + t
    heads = 2
    groups = D_HEADS // heads
    ckv_col = cq_col + Q_LORA
    kr_col = ckv_col + KV_LORA
    wspec = lambda k, off: pl.BlockSpec((k, heads * LANES), lambda bi, g: (0, off + g))
    gspec = lambda width: pl.BlockSpec((1, width), lambda bi, g: (0, 0))
    return pl.pallas_call(
        functools.partial(_mla_kernel, heads=heads),
        grid=(b, groups),
        in_specs=[
            pl.BlockSpec((1, t, Q_LORA), lambda bi, g: (bi, 0, cq_col // Q_LORA)),
            pl.BlockSpec((1, t, KV_LORA), lambda bi, g: (bi, 0, ckv_col // KV_LORA)),
            pl.BlockSpec((1, t, LANES), lambda bi, g: (bi, 0, kr_col // LANES)),
            pl.BlockSpec((t, LANES), lambda bi, g: (0, 0)),
            gspec(Q_LORA), gspec(KV_LORA),
            wspec(Q_LORA, 0), wspec(Q_LORA, groups), wspec(KV_LORA, 0), wspec(KV_LORA, groups),
            gspec(LANES), gspec(LANES), gspec(LANES), gspec(LANES),
        ],
        out_specs=pl.BlockSpec((1, t, heads * D_V), lambda bi, g: (bi, 0, g)),
        out_shape=jax.ShapeDtypeStruct((b, t, D_HEADS * D_V), BF16),
        scratch_shapes=[pltpu.VMEM((heads, tp, 2 * LANES), BF16), pltpu.VMEM((heads, tp, 2 * LANES), BF16),
                        pltpu.VMEM((heads, tp, D_V), BF16)],
        compiler_params=_params("parallel", "parallel"),
        name="mla",
    )(z, z, z, tab, q_a_norm.reshape(1, -1), kv_a_norm.reshape(1, -1), wq, wq, wkv, wkv, gqn, gqr, gkn, gkr)


def _pack_bf16_pairs(v):
    w = v.shape[1] // 2
    bits = pltpu.bitcast(_bf(v).astype(F32), jnp.uint32)
    return (bits[:, :w] >> 16) | (bits[:, w:] & jnp.uint32(0xFFFF0000))


def _unpack_lo(words):
    return pltpu.bitcast(words << 16, F32)


def _unpack_hi(words):
    return pltpu.bitcast(words & jnp.uint32(0xFFFF0000), F32)


def _router_kernel(x_ref, g_ref, w_ref, b_ref, gate_ref, idx_ref, xg_ref, cnt_ref, carry_ref):
    tm = x_ref.shape[0]

    @pl.when(pl.program_id(0) == 0)
    def _():
        carry_ref[...] = jnp.zeros_like(carry_ref)

    x = x_ref[...]
    ms = jnp.mean(x * x, axis=-1, keepdims=True)
    xn = x * lax.rsqrt(ms + EPS) * g_ref[...]
    xh = _bf(xn)
    xl = _bf(xn - xh.astype(F32))
    w = w_ref[...]
    wh = _bf(w)
    wl = _bf(w - wh.astype(F32))
    both = _dot(xh, jnp.concatenate([wh, wl], axis=1))
    logits = (both[:, :LANES] + both[:, LANES:]) + _dot(xl, wh) + b_ref[...]
    lane = lax.broadcasted_iota(jnp.int32, logits.shape, 1)
    lane_f = lane.astype(F32)
    neg = -jnp.inf
    big = float(LANES)

    is_group = lane < N_GROUPS
    g_max = jnp.max(jnp.where(is_group, logits, neg), axis=-1, keepdims=True)
    g_sum = jnp.sum(jnp.where(is_group, jnp.exp(logits - g_max), 0.0), axis=-1, keepdims=True)
    p_top = 1.0 / g_sum
    grp = jnp.min(jnp.where(is_group & (logits == g_max), lane_f, big), axis=-1, keepdims=True)

    e_lo = N_GROUPS + grp * EXPERTS_PER_GROUP
    in_grp = (lane_f >= e_lo) & (lane_f < e_lo + EXPERTS_PER_GROUP)
    e_max = jnp.max(jnp.where(in_grp, logits, neg), axis=-1, keepdims=True)
    e_sum = jnp.sum(jnp.where(in_grp, jnp.exp(logits - e_max), 0.0), axis=-1, keepdims=True)
    i1 = jnp.min(jnp.where(in_grp & (logits == e_max), lane_f, big), axis=-1, keepdims=True)
    rest = in_grp & (lane_f != i1)
    e_2nd = jnp.max(jnp.where(rest, logits, neg), axis=-1, keepdims=True)
    i2 = jnp.min(jnp.where(rest & (logits == e_2nd), lane_f, big), axis=-1, keepdims=True)
    p1 = 1.0 / e_sum
    p2 = jnp.exp(e_2nd - e_max) / e_sum
    tot = p1 + p2
    gate_ref[...] = jnp.where(lane == 0, p_top * p1 / tot, jnp.where(lane == 1, p_top * p2 / tot, 0.0))

    e1 = i1 - N_GROUPS
    e2 = i2 - N_GROUPS
    hot = jnp.where((lane_f == e1) | (lane_f == e2), 1.0, 0.0)
    row = lax.broadcasted_iota(jnp.int32, (tm, tm), 0)
    col = lax.broadcasted_iota(jnp.int32, (tm, tm), 1)
    before = _dot(jnp.where(col < row, 1.0, 0.0).astype(BF16), _bf(hot)) + carry_ref[...]
    r1 = jnp.sum(jnp.where(lane_f == e1, before, 0.0), axis=-1, keepdims=True)
    r2 = jnp.sum(jnp.where(lane_f == e2, before, 0.0), axis=-1, keepdims=True)
    total = carry_ref[...] + jnp.sum(hot, axis=0, keepdims=True)
    carry_ref[...] = total
    cnt_ref[...] = jnp.broadcast_to(total, cnt_ref.shape).astype(jnp.int32)
    idx_ref[...] = jnp.where(lane == 0, e1, jnp.where(lane == 1, e2, jnp.where(lane == 2, r1, jnp.where(
        lane == 3, r2, 0.0)))).astype(jnp.int32)

    words = _pack_bf16_pairs(xn)
    for s in range(8):
        xg_ref[pl.ds(s, tm, stride=8), :] = words[:, s * LANES:(s + 1) * LANES]


def _router(x2d, gain, w_group, b_group, w_expert, b_expert):
    m, d = x2d.shape
    assert d == 2 * 8 * LANES
    tm = _row_tile(m, ROW_TILE)
    pad = LANES - N_GROUPS - N_EXPERTS
    w = jnp.concatenate([w_group, w_expert, jnp.zeros((d, pad), F32)], axis=1)
    bias = jnp.concatenate([b_group, b_expert, jnp.zeros((pad,), F32)]).reshape(1, LANES)
    return pl.pallas_call(
        _router_kernel,
        grid=(m // tm,),
        in_specs=[
            pl.BlockSpec((tm, d), lambda i: (i, 0)),
            pl.BlockSpec((1, d), lambda i: (0, 0)),
            pl.BlockSpec((d, LANES), lambda i: (0, 0)),
            pl.BlockSpec((1, LANES), lambda i: (0, 0)),
        ],
        out_specs=[pl.BlockSpec((tm, LANES), lambda i: (i, 0)), pl.BlockSpec((tm, LANES), lambda i: (i, 0)),
                   pl.BlockSpec((tm * 8, LANES), lambda i: (i, 0)), pl.BlockSpec((8, LANES), lambda i: (0, 0))],
        out_shape=[jax.ShapeDtypeStruct((m, LANES), F32), jax.ShapeDtypeStruct((m, LANES), jnp.int32),
                   jax.ShapeDtypeStruct((m * 8, LANES), jnp.uint32), jax.ShapeDtypeStruct((8, LANES), jnp.int32)],
        scratch_shapes=[pltpu.VMEM((1, LANES), F32)],
        compiler_params=_params("arbitrary"),
        name="router",
    )(x2d, gain.reshape(1, d), w, bias)


def _invert_kernel(dest_ref, inv_ref):
    def clear(s, carry):
        inv_ref[s] = -1
        return carry

    lax.fori_loop(0, inv_ref.shape[0], clear, 0, unroll=8)

    def put(f, carry):
        inv_ref[dest_ref[f]] = f
        return carry

    lax.fori_loop(0, dest_ref.shape[0], put, 0, unroll=8)


def _invert(dest, p):
    assert p % 8 == 0 and dest.shape[0] % 8 == 0
    smem = pl.BlockSpec(memory_space=pltpu.SMEM)
    return pl.pallas_call(
        _invert_kernel, in_specs=[smem], out_specs=smem,
        out_shape=jax.ShapeDtypeStruct((p,), jnp.int32), name="moe_invert",
    )(dest)


def _expert_kernel(be_ref, nxt_ref, run_ref, nu_ref, src_ref, dst_ref, xg_hbm, w1_hbm, w3_hbm, w2_hbm, o_hbm,
                   xbuf, ybuf, w1f, w3f, w2f, w1s, w3s, w2s, sem_in, sem_out, sem_w, *, layer):
    i = pl.program_id(0)
    n_used = nu_ref[0]
    par = i % 2
    half_e = D_EXPERT // 2

    def weight_copies(expert, slot):
        return [pltpu.make_async_copy(w_hbm.at[layer, expert], w_f.at[slot], sem_w.at[slot, j])
                for j, (w_hbm, w_f) in enumerate(((w1_hbm, w1f), (w3_hbm, w3f), (w2_hbm, w2f)))]

    def gather(block, r, slot):
        src = pl.multiple_of(src_ref[block * MOE_BLOCK + r], 8)
        return pltpu.make_async_copy(xg_hbm.at[pl.ds(src, 8), :], xbuf.at[slot, pl.ds(r * 8, 8), :], sem_in.at[slot])

    def scatter(block, r, slot):
        dst = pl.multiple_of(dst_ref[(block + 1) * MOE_BLOCK + r], 8)
        return pltpu.make_async_copy(ybuf.at[slot, pl.ds(r * 8, 8), :], o_hbm.at[pl.ds(dst, 8), :], sem_out.at[slot])

    def wait_gathers(slot):
        pltpu.make_async_copy(xbuf.at[1 - slot], xbuf.at[slot], sem_in.at[slot]).wait()

    def wait_scatters(slot):
        pltpu.make_async_copy(ybuf.at[slot], ybuf.at[1 - slot], sem_out.at[slot]).wait()

    @pl.when(i == 0)
    def _():
        ybuf[1] = jnp.zeros(ybuf.shape[1:], ybuf.dtype)
        n_real = o_hbm.shape[0] - 2 * MOE_BLOCK * 8
        fill1 = pltpu.make_async_copy(ybuf.at[1], o_hbm.at[pl.ds(n_real + MOE_BLOCK * 8, MOE_BLOCK * 8), :],
                                      sem_out.at[1])
        fill1.start()
        fill1.wait()
        pltpu.make_async_copy(ybuf.at[1], o_hbm.at[pl.ds(n_real, MOE_BLOCK * 8), :], sem_out.at[0]).start()

        def first(r, carry):
            gather(0, r, 0).start()
            return carry

        lax.fori_loop(0, MOE_BLOCK, first, 0, unroll=8)

    @pl.when(i < n_used)
    def _():
        expert = be_ref[i]
        wslot = run_ref[i] % 2

        @pl.when(i == 0)
        def _():
            for cp in weight_copies(expert, 0):
                cp.start()

        @pl.when((i == 0) | (expert != be_ref[jnp.maximum(i - 1, 0)]))
        def _():
            for cp in weight_copies(expert, wslot):
                cp.wait()

            @pl.when(nxt_ref[i] != expert)
            def _():
                for cp in weight_copies(nxt_ref[i], 1 - wslot):
                    cp.start()

            w1s[...] = _bf(w1f[wslot])
            w3s[...] = _bf(w3f[wslot])
            w2s[...] = _bf(w2f[wslot])

        wait_gathers(par)

        def move_rows(group, n_groups=6):
            lo, hi = group * MOE_BLOCK // n_groups, (group + 1) * MOE_BLOCK // n_groups
            for r in range(lo, hi):
                gather(i + 1, r, 1 - par).start(priority=r % 2)
                scatter(i - 1, r, 1 - par).start(priority=(r + 1) % 2)

        tiles = [xbuf[par, pl.ds(s, MOE_BLOCK, stride=8), :] for s in range(8)]
        xb = jnp.concatenate([_bf(_unpack_lo(w)) for w in tiles] + [_bf(_unpack_hi(w)) for w in tiles], axis=1)
        move_rows(0)
        h1a = _dot(xb, w1s[:, :half_e])
        move_rows(1)
        h3a = _dot(xb, w3s[:, :half_e])
        move_rows(2)
        act_a = _bf(_silu(h1a) * h3a)
        h1b = _dot(xb, w1s[:, half_e:])
        move_rows(3)
        h3b = _dot(xb, w3s[:, half_e:])
        move_rows(4)
        act_b = _bf(_silu(h1b) * h3b)
        y = _dot(act_a, w2s[:half_e, :])
        move_rows(5)
        y = y + _dot(act_b, w2s[half_e:, :])

        wait_scatters(par)
        words = _pack_bf16_pairs(y)
        for s in range(8):
            ybuf[par, pl.ds(s, MOE_BLOCK, stride=8), :] = words[:, s * LANES:(s + 1) * LANES]

        @pl.when(i == n_used - 1)
        def _():
            def last(r, carry):
                scatter(i, r, par).start()
                return carry

            lax.fori_loop(0, MOE_BLOCK, last, 0, unroll=8)
            wait_scatters(par)
            wait_scatters(1 - par)
            wait_gathers(1 - par)


def _moe(x2d, gain, w_group, b_group, w_expert, b_expert, w1, w3, w2, layer, final_shape=None):
    n, d = x2d.shape
    gates_l, idx_l, xg, cnt = _router(x2d, gain, w_group, b_group, w_expert, b_expert)

    a = n * TOP_K
    n_blocks = -(-a // MOE_BLOCK) + N_EXPERTS
    p = n_blocks * MOE_BLOCK
    counts = cnt[0, :N_EXPERTS]
    padded = (counts + MOE_BLOCK - 1) // MOE_BLOCK * MOE_BLOCK
    pad_end = jnp.cumsum(padded)
    pad_start = pad_end - padded
    e_hot = idx_l[:, :TOP_K, None] == jnp.arange(N_EXPERTS, dtype=jnp.int32)
    dest = jnp.sum(jnp.where(e_hot, pad_start, 0), axis=-1) + idx_l[:, TOP_K:2 * TOP_K]
    dest = jnp.clip(dest.reshape(-1), 0, p - 1).astype(jnp.int32)
    blk0 = jnp.arange(n_blocks, dtype=jnp.int32) * MOE_BLOCK
    block_expert = jnp.minimum(jnp.searchsorted(pad_end, blk0, side="right"), N_EXPERTS - 1).astype(jnp.int32)
    n_used = (pad_end[-1] // MOE_BLOCK).astype(jnp.int32).reshape(1)
    experts = jnp.arange(N_EXPERTS, dtype=jnp.int32)
    later = jnp.where((experts[None, :] > experts[:, None]) & (counts[None, :] > 0), experts[None, :], N_EXPERTS)
    next_owner = jnp.min(later, axis=1)
    next_expert = jnp.where(next_owner < N_EXPERTS, next_owner, experts)[block_expert].astype(jnp.int32)
    run_index = (jnp.cumsum(jnp.concatenate([jnp.ones((1,), jnp.int32),
                                             (block_expert[1:] != block_expert[:-1]).astype(jnp.int32)])) - 1
                 ).astype(jnp.int32)
    codes = _invert(dest, p)
    slot = jnp.arange(p, dtype=jnp.int32)
    spare = TOP_K * n + (slot // MOE_BLOCK % 2) * MOE_BLOCK + slot % MOE_BLOCK
    src_tok = (jnp.maximum(codes, 0) >> 1) * 8
    dst_row = jnp.where(codes >= 0, (codes & 1) * n + (codes >> 1), spare) * 8
    lead = (TOP_K * n + MOE_BLOCK + jnp.arange(MOE_BLOCK, dtype=jnp.int32)) * 8
    dst_row = jnp.concatenate([lead, dst_row])

    hbm = pl.BlockSpec(memory_space=pl.ANY)
    out_rows = TOP_K * n + 2 * MOE_BLOCK
    out2 = pl.pallas_call(
        functools.partial(_expert_kernel, layer=layer),
        grid_spec=pltpu.PrefetchScalarGridSpec(
            num_scalar_prefetch=6,
            grid=(n_blocks,),
            in_specs=[hbm, hbm, hbm, hbm],
            out_specs=hbm,
            scratch_shapes=[
                pltpu.VMEM((2, MOE_BLOCK * 8, LANES), jnp.uint32), pltpu.VMEM((2, MOE_BLOCK * 8, LANES), jnp.uint32),
                pltpu.VMEM((2, d, D_EXPERT), F32), pltpu.VMEM((2, d, D_EXPERT), F32), pltpu.VMEM((2, D_EXPERT, d), F32),
                pltpu.VMEM((d, D_EXPERT), BF16), pltpu.VMEM((d, D_EXPERT), BF16), pltpu.VMEM((D_EXPERT, d), BF16),
                pltpu.SemaphoreType.DMA((2,)), pltpu.SemaphoreType.DMA((2,)), pltpu.SemaphoreType.DMA((2, 3)),
            ],
        ),
        out_shape=jax.ShapeDtypeStruct((out_rows * 8, LANES), jnp.uint32),
        compiler_params=_params("arbitrary"),
        name="moe_experts",
    )(block_expert, next_expert, run_index, n_used, src_tok, dst_row, xg, w1, w3, w2)
    return _combine(x2d, gates_l, out2, final_shape)


def _combine_tile(x_ref, gate_ref, a_ref, b_ref, o_ref):
    tm = x_ref.shape[0]
    half = x_ref.shape[1] // 2
    g0 = gate_ref[:, 0:1]
    g1 = gate_ref[:, 1:2]
    for s in range(8):
        wa = a_ref[pl.ds(s, tm, stride=8), :]
        wb = b_ref[pl.ds(s, tm, stride=8), :]
        lo = slice(s * LANES, (s + 1) * LANES)
        hi = slice(half + s * LANES, half + (s + 1) * LANES)
        o_ref[:, lo] = x_ref[:, lo] + (g0 * _unpack_lo(wa) + g1 * _unpack_lo(wb))
        o_ref[:, hi] = x_ref[:, hi] + (g0 * _unpack_hi(wa) + g1 * _unpack_hi(wb))


def _combine_kernel(x_ref, gate_ref, a_ref, b_ref, o_ref):
    _combine_tile(x_ref, gate_ref, a_ref, b_ref, o_ref)


def _combine_final_kernel(x_ref, gate_ref, a_ref, b_ref, o_hbm, obuf, sem, *, tiles_per_batch):
    i = pl.program_id(0)
    n_steps = pl.num_programs(0)
    tm = x_ref.shape[0]

    def copies(step, slot):
        batch, j = step // tiles_per_batch, step % tiles_per_batch
        row = pl.multiple_of(j * tm, 8)
        main = pltpu.make_async_copy(obuf.at[slot, pl.ds(N_META, tm - N_META), :],
                                     o_hbm.at[batch, pl.ds(row, tm - N_META), :], sem.at[slot, 0])
        head = pltpu.make_async_copy(obuf.at[slot, pl.ds(0, N_META), :],
                                     o_hbm.at[batch, pl.ds(pl.multiple_of(jnp.maximum(row - N_META, 0), 8), N_META), :],
                                     sem.at[slot, 1])
        return main, head, j > 0

    def wait_step(step, slot):
        main, head, has_head = copies(step, slot)
        main.wait()

        @pl.when(has_head)
        def _():
            head.wait()

    slot = i % 2

    @pl.when(i >= 2)
    def _():
        wait_step(i - 2, slot)

    _combine_tile(x_ref, gate_ref, a_ref, b_ref, obuf.at[slot])
    main, head, has_head = copies(i, slot)
    main.start()

    @pl.when(has_head)
    def _():
        head.start()

    @pl.when(i == n_steps - 1)
    def _():
        wait_step(i, slot)

        @pl.when(i >= 1)
        def _():
            wait_step(i - 1, 1 - slot)


def _combine(x2d, gates, out2, final_shape=None):
    n, d = x2d.shape
    tm = _row_tile(n, ROW_TILE)
    in_specs = [
        pl.BlockSpec((tm, d), lambda i: (i, 0)),
        pl.BlockSpec((tm, LANES), lambda i: (i, 0)),
        pl.BlockSpec((tm * 8, LANES), lambda i: (i, 0)),
        pl.BlockSpec((tm * 8, LANES), lambda i: (n // tm + i, 0)),
    ]
    if final_shape is None:
        return pl.pallas_call(
            _combine_kernel,
            grid=(n // tm,),
            in_specs=in_specs,
            out_specs=pl.BlockSpec((tm, d), lambda i: (i, 0)),
            out_shape=jax.ShapeDtypeStruct((n, d), F32),
            compiler_params=_params("parallel"),
            name="moe_combine",
        )(x2d, gates, out2, out2)
    b, t = final_shape
    assert t % tm == 0 and tm > N_META
    return pl.pallas_call(
        functools.partial(_combine_final_kernel, tiles_per_batch=t // tm),
        grid=(n // tm,),
        in_specs=in_specs,
        out_specs=pl.BlockSpec(memory_space=pl.ANY),
        out_shape=jax.ShapeDtypeStruct((b, t - N_META, d), F32),
        scratch_shapes=[pltpu.VMEM((2, tm, d), F32), pltpu.SemaphoreType.DMA((2, 2))],
        compiler_params=_params("arbitrary"),
        name="moe_combine_final",
    )(x2d, gates, out2, out2)


def _even_layer(x, norm_g, w_in, conv_w, conv_b, b_i, b_f, a_norm, w_gate2, b_gate, b_norm, w_out):
    b, t, d = x.shape
    n = b * t
    a_w = 2 * A_HEADS * A_DK + 2 * A_HEADS * A_DV
    g_w = 2 * A_HEADS
    b_w = 2 * B_HEADS * B_DK + 2 * B_HEADS * B_DV
    main = a_w + b_w
    w = _even_weight(w_in, a_w, g_w, b_w, GATE_RANK)
    z = _normproj(x.reshape(n, d), norm_g, w).reshape(b, t, main + MXU_DIM)

    ya = _mlstm(z, main, conv_w, conv_b, b_i, b_f, a_norm)

    wg = jnp.zeros((B_HEADS, MXU_DIM, B_DK), F32).at[:, g_w:g_w + GATE_RANK, :].set(
        w_gate2.reshape(GATE_RANK, B_HEADS, B_DK).transpose(1, 0, 2))
    dummy = jnp.zeros((1, B_HEADS * B_DK), F32)
    hp = 2
    hspec = pl.BlockSpec((1, hp * B_DK), lambda bi, g: (0, g))
    yb = _gla_call(
        z, B_HEADS, hp, B_DK, B_DV,
        (a_w, a_w + B_HEADS * B_DK, a_w + 2 * B_HEADS * B_DK, a_w + 2 * B_HEADS * B_DK + B_HEADS * B_DV, main),
        MXU_DIM,
        ((wg, pl.BlockSpec((hp, MXU_DIM, B_DK), lambda bi, g: (g, 0, 0))),
         (b_gate.reshape(1, -1), hspec), (dummy, hspec)),
        b_norm, "gla")
    return _outproj(ya.reshape(n, -1), yb.reshape(n, -1), w_out.astype(BF16), x.reshape(n, d)).reshape(b, t, d)


def _odd_layer(x, lb, norm_g, w_in, c_norm, q_a_norm, w_q_up, kv_a_norm, w_kv_up, q_norm, k_norm, w_out):
    b, t, d = x.shape
    n = b * t
    c_w = 2 * C_HEADS * C_DK + 2 * C_HEADS * C_DV
    swap = (jnp.arange(D_ROPE) + D_ROPE // 2) % D_ROPE
    kr0 = c_w + Q_LORA + KV_LORA
    used = kr0 + 2 * D_ROPE
    total = -(-used // MXU_DIM) * MXU_DIM
    z2 = _normproj(x.reshape(n, d), norm_g, _odd_weight(w_in, kr0))
    z = z2.reshape(b, t, total)

    hp = 4
    hspec = pl.BlockSpec((1, hp * C_DK), lambda bi, g: (0, g))
    yc = _gla_call(
        z, C_HEADS, hp, C_DK, C_DV,
        (0, C_HEADS * C_DK, 2 * C_HEADS * C_DK, 2 * C_HEADS * C_DK + C_HEADS * C_DV, C_HEADS * C_DK),
        C_DK,
        ((jnp.log(lb).reshape(1, -1), hspec), (jnp.log1p(-lb).reshape(1, -1), hspec), ((1.0 - lb).reshape(1, -1), hspec)),
        c_norm, "hgrn")

    dq = D_NOPE + D_ROPE
    wq = w_q_up.reshape(Q_LORA, D_HEADS, dq)
    wq_rope = wq[:, :, D_NOPE:]
    wq_p = jnp.concatenate([wq[:, :, :D_NOPE].reshape(Q_LORA, -1),
                            jnp.concatenate([wq_rope, wq_rope[:, :, swap]], axis=-1).reshape(Q_LORA, -1)],
                           axis=1).astype(BF16)
    wkv = w_kv_up.reshape(KV_LORA, D_HEADS, D_NOPE + D_V)
    wkv_p = jnp.concatenate([wkv[:, :, :D_NOPE].reshape(KV_LORA, -1), wkv[:, :, D_NOPE:].reshape(KV_LORA, -1)],
                            axis=1).astype(BF16)
    pos = jnp.arange(t, dtype=F32)
    half = D_ROPE // 2
    inv = ROPE_THETA ** (-jnp.arange(half, dtype=F32) / half)
    ang = pos[:, None] * inv[None, :]
    cos, sin = jnp.cos(ang), jnp.sin(ang)
    tab = jnp.concatenate([cos, cos, -sin, sin], axis=1)
    pair = lambda g: jnp.concatenate([g[D_NOPE:], g[D_NOPE:][swap]]).reshape(1, LANES)
    yd = _mla(z, c_w, tab, q_a_norm, kv_a_norm, wq_p, wkv_p, q_norm[:D_NOPE].reshape(1, LANES), pair(q_norm),
              k_norm[:D_NOPE].reshape(1, LANES), pair(k_norm))
    return _outproj(yc.reshape(n, -1), yd.reshape(n, -1), w_out.astype(BF16), x.reshape(n, d)).reshape(b, t, d)


def kernel(x, meta_tokens, ab_norm, ab_w_in, a_conv_w, a_conv_b, a_b_i, a_b_f, a_head_norm, b_w_gate2, b_b_gate, b_head_norm, ab_w_out, cd_norm, cd_w_in, c_lower_bound, c_head_norm, d_q_a_norm, d_w_q_up, d_kv_a_norm, d_w_kv_up, d_q_norm, d_k_norm, cd_w_out, moe_norm, moe_w_group, moe_b_group, moe_w_expert, moe_b_expert, moe_w1, moe_w3, moe_w2):
    b = x.shape[0]
    depth = moe_norm.shape[0]
    h = jnp.concatenate([jnp.broadcast_to(meta_tokens.astype(x.dtype)[None], (b, N_META, D_MODEL)), x], axis=1)
    t = h.shape[1]
    lb_cum = jnp.cumsum(jax.nn.softmax(c_lower_bound.astype(F32), axis=0), axis=0)
    lower_bounds = lb_cum - lb_cum[0]
    for layer in range(depth):
        j = layer // 2
        if layer % 2 == 0:
            h = _even_layer(h, ab_norm[j], ab_w_in[j], a_conv_w[j], a_conv_b[j], a_b_i[j], a_b_f[j], a_head_norm[j],
                            b_w_gate2[j], b_b_gate[j], b_head_norm[j], ab_w_out[j])
        else:
            h = _odd_layer(h, lower_bounds[layer], cd_norm[j], cd_w_in[j], c_head_norm[j], d_q_a_norm[j],
                           d_w_q_up[j], d_kv_a_norm[j], d_w_kv_up[j], d_q_norm[j], d_k_norm[j], cd_w_out[j])
        last = layer == depth - 1
        h = _moe(h.reshape(b * t, D_MODEL), moe_norm[layer], moe_w_group[layer], moe_b_group[layer],
                 moe_w_expert[layer], moe_b_expert[layer], moe_w1, moe_w3, moe_w2, layer,
                 final_shape=(b, t) if last else None)
        if not last:
            h = h.reshape(b, t, D_MODEL)
    return h
```

```python
import functools

import jax
import jax.numpy as jnp
from jax import lax
from jax.experimental import pallas as pl
from jax.experimental.pallas import tpu as pltpu

F32 = jnp.float32
BF16 = jnp.bfloat16

D_MODEL = 2048
N_META = 16
CHUNK = 64
CONV_K = 4
EPS = 1e-6
A_HEADS, A_DK, A_DV = 4, 128, 256
B_HEADS, B_DK, B_DV = 4, 128, 256
GATE_RANK = 16
GATE_TAU = 16.0
C_HEADS, C_DK, C_DV = 8, 128, 128
D_HEADS, D_NOPE, D_ROPE, D_V = 8, 128, 64, 128
Q_LORA, KV_LORA = 512, 256
ROPE_THETA = 10000.0
N_GROUPS, EXPERTS_PER_GROUP = 4, 8
N_EXPERTS = N_GROUPS * EXPERTS_PER_GROUP
TOP_K = 2
D_EXPERT = 512

LANES = 128
MXU_DIM = 256
BF16_ROWS = 16
VMEM_LIMIT = 56 * 1024 * 1024
MOE_BLOCK = MXU_DIM
ROW_TILE = 688
ROW_TILE_BF16 = 2 * ROW_TILE
COL_TILE = 1280
COL_TILE_OUT = 1024
PREP_ROWS = 768
ATT_BLOCK = 256
ATT_PAD = ATT_BLOCK - N_META

_NT = (((1,), (1,)), ((), ()))
_TN = (((0,), (0,)), ((), ()))


def _dot(a, b):
    return jnp.dot(a, b, preferred_element_type=F32)


def _dot_nt(a, b):
    return lax.dot_general(a, b, _NT, preferred_element_type=F32)


def _dot_tn(a, b):
    return lax.dot_general(a, b, _TN, preferred_element_type=F32)


def _bf(x):
    return x.astype(BF16)


def _split3(x):
    hi = _bf(x)
    rest = x - hi.astype(F32)
    mid = _bf(rest)
    return hi, mid, _bf(rest - mid.astype(F32))


def _log_sigmoid(x):
    return jnp.minimum(x, 0.0) - jnp.log1p(jnp.exp(-jnp.abs(x)))


def _sigmoid(x):
    return 1.0 / (1.0 + jnp.exp(-x))


def _silu(x):
    return x * _sigmoid(x)


def _row_tile(m, cap):
    best = None
    for t in range(BF16_ROWS, min(m, cap) + 1, BF16_ROWS):
        if m % t == 0:
            best = t
    assert best is not None, m
    return best


def _col_tile(n, cap):
    best = None
    for t in range(MXU_DIM, min(n, cap) + 1, MXU_DIM):
        if n % t == 0:
            best = t
    assert best is not None, n
    return best


def _params(*sem):
    return pltpu.CompilerParams(dimension_semantics=sem, vmem_limit_bytes=VMEM_LIMIT)


_RELAYOUT_ROWS = 256


def _even_weight_kernel(wa_ref, wb_ref, wc_ref, o_ref, *, n_plain, n_shift, shift, gate_cols):
    ob = pl.program_id(0)
    rows = o_ref.shape[0]
    chunks = rows // _RELAYOUT_ROWS

    @pl.when(ob < n_plain)
    def _():
        def body(c, carry):
            r = pl.ds(pl.multiple_of(c * _RELAYOUT_ROWS, _RELAYOUT_ROWS), _RELAYOUT_ROWS)
            o_ref[r, :] = _bf(wa_ref[r, :])
            return carry

        lax.fori_loop(0, chunks, body, 0)

    @pl.when((ob >= n_plain) & (ob < n_plain + n_shift))
    def _():
        def body(c, carry):
            r = pl.ds(pl.multiple_of(c * _RELAYOUT_ROWS, _RELAYOUT_ROWS), _RELAYOUT_ROWS)
            wide = jnp.concatenate([wa_ref[r, :], wb_ref[r, :]], axis=1)
            o_ref[r, :] = _bf(wide[:, shift:shift + MXU_DIM])
            return carry

        lax.fori_loop(0, chunks, body, 0)

    @pl.when(ob == n_plain + n_shift)
    def _():
        lane = lax.broadcasted_iota(jnp.int32, (_RELAYOUT_ROWS, LANES), 1)

        def body(c, carry):
            r = pl.ds(pl.multiple_of(c * _RELAYOUT_ROWS, _RELAYOUT_ROWS), _RELAYOUT_ROWS)
            first = jnp.where(lane < shift, wc_ref[r, :], jnp.where(lane < gate_cols, wb_ref[r, :], 0.0))
            o_ref[r, :] = _bf(jnp.concatenate([first, jnp.zeros_like(first)], axis=1))
            return carry

        lax.fori_loop(0, chunks, body, 0)


def _even_weight(w_in, a_w, g_w, b_w, rank):
    d = w_in.shape[0]
    assert a_w % MXU_DIM == 0 and b_w % MXU_DIM == 0 and g_w + rank <= LANES and d % _RELAYOUT_ROWS == 0
    n_plain, n_shift = a_w // MXU_DIM, b_w // MXU_DIM
    n_out = n_plain + n_shift + 1
    last = n_out - 1

    def b_index(ob):
        return (0, jnp.where(ob < n_plain, 0, jnp.where(ob < last, 2 * (ob + 1), (a_w + g_w + b_w) // LANES)))

    return pl.pallas_call(
        functools.partial(_even_weight_kernel, n_plain=n_plain, n_shift=n_shift, shift=g_w, gate_cols=g_w + rank),
        grid=(n_out,),
        in_specs=[
            pl.BlockSpec((d, MXU_DIM), lambda ob: (0, jnp.minimum(ob, last - 1))),
            pl.BlockSpec((d, LANES), b_index),
            pl.BlockSpec((d, LANES), lambda ob: (0, a_w // LANES)),
        ],
        out_specs=pl.BlockSpec((d, MXU_DIM), lambda ob: (0, ob)),
        out_shape=jax.ShapeDtypeStruct((d, n_out * MXU_DIM), BF16),
        compiler_params=_params("parallel"),
        name="even_weight",
    )(w_in, w_in, w_in)


def _odd_weight_kernel(w_ref, o_ref, *, n_plain):
    ob = pl.program_id(0)
    chunks = o_ref.shape[0] // _RELAYOUT_ROWS
    half = D_ROPE // 2

    def body(c, carry):
        r = pl.ds(pl.multiple_of(c * _RELAYOUT_ROWS, _RELAYOUT_ROWS), _RELAYOUT_ROWS)
        w = w_ref[r, :]

        @pl.when(ob < n_plain)
        def _():
            o_ref[r, :] = _bf(w)

        @pl.when(ob == n_plain)
        def _():
            pair = jnp.concatenate([w[:, :D_ROPE], w[:, half:D_ROPE], w[:, :half]], axis=1)
            o_ref[r, :] = _bf(jnp.concatenate([pair, jnp.zeros_like(pair)], axis=1))

        return carry

    lax.fori_loop(0, chunks, body, 0)


def _odd_weight(w_in, kr0):
    d = w_in.shape[0]
    assert kr0 % MXU_DIM == 0 and w_in.shape[1] == kr0 + D_ROPE and d % _RELAYOUT_ROWS == 0
    n_plain = kr0 // MXU_DIM
    return pl.pallas_call(
        functools.partial(_odd_weight_kernel, n_plain=n_plain),
        grid=(n_plain + 1,),
        in_specs=[pl.BlockSpec((d, MXU_DIM), lambda ob: (0, ob))],
        out_specs=pl.BlockSpec((d, MXU_DIM), lambda ob: (0, ob)),
        out_shape=jax.ShapeDtypeStruct((d, kr0 + MXU_DIM), BF16),
        compiler_params=_params("parallel"),
        name="odd_weight",
    )(w_in)


def _normproj_kernel(x_ref, g_ref, w_ref, o_ref, xs_ref):
    tm = xs_ref.shape[0]

    @pl.when(pl.program_id(1) == 0)
    def _():
        def body(c, carry):
            r0 = pl.multiple_of(c * BF16_ROWS, BF16_ROWS)
            x = x_ref[pl.ds(r0, BF16_ROWS), :]
            ms = jnp.mean(x * x, axis=-1, keepdims=True)
            xs_ref[pl.ds(r0, BF16_ROWS), :] = _bf(x * lax.rsqrt(ms + EPS) * g_ref[...])
            return carry

        lax.fori_loop(0, tm // BF16_ROWS, body, 0, unroll=8)

    o_ref[...] = _dot(xs_ref[...], w_ref[...]).astype(o_ref.dtype)


def _normproj(x2d, gain, w):
    m, k = x2d.shape
    n = w.shape[1]
    tm = _row_tile(m, ROW_TILE)
    tn = _col_tile(n, COL_TILE)
    return pl.pallas_call(
        _normproj_kernel,
        grid=(m // tm, n // tn),
        in_specs=[
            pl.BlockSpec((tm, k), lambda i, j: (i, 0)),
            pl.BlockSpec((1, k), lambda i, j: (0, 0)),
            pl.BlockSpec((k, tn), lambda i, j: (0, j)),
        ],
        out_specs=pl.BlockSpec((tm, tn), lambda i, j: (i, j)),
        out_shape=jax.ShapeDtypeStruct((m, n), F32),
        scratch_shapes=[pltpu.VMEM((tm, k), BF16)],
        compiler_params=_params("parallel", "arbitrary"),
        name="normproj",
    )(x2d, gain.reshape(1, k).astype(F32), w)


def _outproj_kernel(ya_ref, yb_ref, w_ref, r_ref, o_ref):
    ka = ya_ref.shape[1]
    acc = _dot(ya_ref[...], w_ref[:ka, :]) + _dot(yb_ref[...], w_ref[ka:, :])
    o_ref[...] = r_ref[...] + acc


def _outproj(ya, yb, w, res):
    m, ka = ya.shape
    kb = yb.shape[1]
    n = w.shape[1]
    tm = _row_tile(m, ROW_TILE_BF16)
    tn = _col_tile(n, COL_TILE_OUT)
    return pl.pallas_call(
        _outproj_kernel,
        grid=(m // tm, n // tn),
        in_specs=[
            pl.BlockSpec((tm, ka), lambda i, j: (i, 0)),
            pl.BlockSpec((tm, kb), lambda i, j: (i, 0)),
            pl.BlockSpec((ka + kb, tn), lambda i, j: (0, j)),
            pl.BlockSpec((tm, tn), lambda i, j: (i, j)),
        ],
        out_specs=pl.BlockSpec((tm, tn), lambda i, j: (i, j)),
        out_shape=jax.ShapeDtypeStruct((m, n), F32),
        compiler_params=_params("parallel", "arbitrary"),
        name="outproj",
    )(ya, yb, w, res)


def _mlstm_kernel(bi_ref, bf_ref, q_ref, k_ref, v_ref, og_ref, gt_ref,
                  cwq_ref, cwk_ref, cbq_ref, cbk_ref, hn_ref, o_ref, c_ref, n_ref, m_ref, *, heads, group):
    t_total = q_ref.shape[1]
    n_chunks = (t_total - N_META) // CHUNK
    head0 = pl.program_id(1) * heads

    c_ref[...] = jnp.zeros_like(c_ref)
    n_ref[...] = jnp.zeros_like(n_ref)
    m_ref[...] = jnp.zeros_like(m_ref)

    def conv(taps, cw, cb, length):
        y = cb
        for j in range(CONV_K):
            y = y + taps[j] * cw[j:j + 1, :]
        return _silu(y)

    def window_taps(win, length):
        return [win[8 - (CONV_K - 1) + j:8 - (CONV_K - 1) + j + length, :] for j in range(CONV_K)]

    hs = range(heads)
    kcol = [slice(hh * A_DK, (hh + 1) * A_DK) for hh in hs]
    vcol = [slice(hh * A_DV, (hh + 1) * A_DV) for hh in hs]

    def gates(hh, blk, blk_parts, length, causal, upper):
        head = head0 + hh
        b_i = bi_ref[head]
        b_f = bf_ref[head]
        lane = lax.broadcasted_iota(jnp.int32, (length, LANES), 1)
        ig_c = jnp.sum(jnp.where(lane == head, blk, 0.0), axis=1, keepdims=True) + b_i
        lf_c = _log_sigmoid(jnp.sum(jnp.where(lane == A_HEADS + head, blk, 0.0), axis=1, keepdims=True) + b_f)
        sel_r = lax.broadcasted_iota(jnp.int32, (8, LANES), 0)
        sel_l = lax.broadcasted_iota(jnp.int32, (8, LANES), 1)
        sel = jnp.where(sel_l == head + A_HEADS * sel_r, 1.0, 0.0).astype(BF16)
        rows = _dot_nt(sel, blk_parts[0]) + (_dot_nt(sel, blk_parts[1]) + _dot_nt(sel, blk_parts[2]))
        ig_r = rows[0:1, :] + b_i
        lf_r = _log_sigmoid(rows[1:2, :] + b_f)
        b_c = jnp.sum(jnp.where(causal, lf_r, 0.0), axis=1, keepdims=True)
        b_r = jnp.sum(jnp.where(upper, lf_c, 0.0), axis=0, keepdims=True)
        b_end = b_c[length - 1:length, :]
        w_end = b_end - b_c + ig_c
        m_loc = jnp.max(w_end, axis=0, keepdims=True)
        d = jnp.where(causal, b_c - b_r + ig_r, -jnp.inf)
        return b_c, b_end, jnp.exp(w_end - m_loc), m_loc, d, jnp.max(d, axis=1, keepdims=True)

    def local_stage(o, length, qwin, kwin):
        row = lax.broadcasted_iota(jnp.int32, (length, length), 0)
        col = lax.broadcasted_iota(jnp.int32, (length, length), 1)
        causal = col <= row
        blk = gt_ref[0, pl.ds(o, length), 0:LANES]
        blk_parts = _split3(blk)
        gt = [gates(hh, blk, blk_parts, length, causal, row <= col) for hh in hs]
        q = [conv(qwin[hh], cwq_ref[:, kcol[hh]], cbq_ref[:, kcol[hh]], length) for hh in hs]
        k = [conv(kwin[hh], cwk_ref[:, kcol[hh]], cbk_ref[:, kcol[hh]], length) * (A_DK ** -0.5) for hh in hs]
        vb = [_bf(v_ref[0, pl.ds(o, length), vcol[hh]]) for hh in hs]
        qb = [_bf(q[hh]) for hh in hs]
        k_w = [k[hh] * gt[hh][2] for hh in hs]
        qk = [_dot_nt(qb[hh], _bf(k[hh])) for hh in hs]
        c_loc = [_dot_tn(_bf(k_w[hh]), vb[hh]) for hh in hs]
        n_loc = [jnp.sum(k_w[hh], axis=0, keepdims=True) for hh in hs]
        return gt, q, qb, vb, qk, c_loc, n_loc

    def state_stage(o, length, staged, state):
        gt, q, qb, vb, qk, c_loc, n_loc = staged
        c_in, n_in, m_in = state
        q_c = [_dot(qb[hh], _bf(c_in[hh])) for hh in hs]
        s, a_t, m_t, q_n, c_out, n_out, m_out = [], [], [], [], [], [], []
        for hh in hs:
            b_c, b_end, _, m_loc, d, d_max = gt[hh]
            inter = b_c + m_in[hh]
            m_t.append(jnp.maximum(inter, d_max))
            s.append(qk[hh] * jnp.exp(d - m_t[hh]))
            a_t.append(jnp.exp(inter - m_t[hh]))
            m_new = jnp.maximum(b_end + m_in[hh], m_loc)
            a = jnp.exp(b_end + m_in[hh] - m_new)
            c = jnp.exp(m_loc - m_new)
            c_out.append(a * c_in[hh] + c * c_loc[hh])
            n_out.append(a * n_in[hh] + c * n_loc[hh])
            m_out.append(m_new)
            q_n.append(jnp.sum(q[hh] * n_in[hh], axis=1, keepdims=True))
        num = [_dot(_bf(s[hh]), vb[hh]) + a_t[hh] * q_c[hh] for hh in hs]
        for hh in hs:
            den = jnp.sum(s[hh], axis=1, keepdims=True) + a_t[hh] * q_n[hh]
            h = num[hh] / jnp.maximum(jnp.abs(den), jnp.exp(-m_t[hh]))
            hn = h * lax.rsqrt(jnp.mean(h * h, axis=-1, keepdims=True) + EPS) * hn_ref[:, vcol[hh]]
            y = _sigmoid(og_ref[0, pl.ds(o, length), vcol[hh]]) * hn
            o_ref[0, pl.ds(o, length), vcol[hh]] = y.astype(o_ref.dtype)
        return c_out, n_out, m_out

    def sweep(offsets, length, qwins, kwins):
        staged = [local_stage(o, length, qw, kw) for o, qw, kw in zip(offsets, qwins, kwins)]
        state = ([c_ref[hh] for hh in hs], [n_ref[hh] for hh in hs], [m_ref[hh] for hh in hs])
        for o, stg in zip(offsets, staged):
            state = state_stage(o, length, stg, state)
        for hh in hs:
            c_ref[hh] = state[0][hh]
            n_ref[hh] = state[1][hh]
            m_ref[hh] = state[2][hh]

    zeros8 = jnp.zeros((8, A_DK), F32)
    sweep([0], N_META,
          [[window_taps(jnp.concatenate([zeros8, q_ref[0, 0:N_META, kcol[hh]]], axis=0), N_META) for hh in hs]],
          [[window_taps(jnp.concatenate([zeros8, k_ref[0, 0:N_META, kcol[hh]]], axis=0), N_META) for hh in hs]])
    assert n_chunks % group == 0

    def body(c, carry):
        offsets = [pl.multiple_of(N_META + (c * group + j) * CHUNK, BF16_ROWS) for j in range(group)]

        starts = [pl.multiple_of(N_META - 8 + (c * group + j) * CHUNK, 8) for j in range(group)]

        def shifted(ref, w0, hh):
            return window_taps(ref[0, pl.ds(w0, CHUNK + 8), kcol[hh]], CHUNK)

        sweep(offsets, CHUNK,
              [[shifted(q_ref, w0, hh) for hh in hs] for w0 in starts],
              [[shifted(k_ref, w0, hh) for hh in hs] for w0 in starts])
        return carry

    lax.fori_loop(0, n_chunks // group, body, 0)


def _mlstm(z, gate_col, conv_w, conv_b, b_i, b_f, head_norm, heads=2):
    b, t, _ = z.shape
    hk = A_HEADS * A_DK
    wk, wv = heads * A_DK, heads * A_DV
    smem = pl.BlockSpec(memory_space=pltpu.SMEM)
    col = lambda width, off: (lambda bi, g: (bi, 0, off // width + g))
    return pl.pallas_call(
        functools.partial(_mlstm_kernel, heads=heads, group=4),
        grid=(b, A_HEADS // heads),
        in_specs=[
            smem, smem,
            pl.BlockSpec((1, t, wk), col(wk, 0)),
            pl.BlockSpec((1, t, wk), col(wk, hk)),
            pl.BlockSpec((1, t, wv), col(wv, 2 * hk)),
            pl.BlockSpec((1, t, wv), col(wv, 2 * hk + A_HEADS * A_DV)),
            pl.BlockSpec((1, t, MXU_DIM), lambda bi, g: (bi, 0, gate_col // MXU_DIM)),
            pl.BlockSpec((CONV_K, wk), lambda bi, g: (0, g)),
            pl.BlockSpec((CONV_K, wk), lambda bi, g: (0, A_HEADS // heads + g)),
            pl.BlockSpec((1, wk), lambda bi, g: (0, g)),
            pl.BlockSpec((1, wk), lambda bi, g: (0, A_HEADS // heads + g)),
            pl.BlockSpec((1, wv), lambda bi, g: (0, g)),
        ],
        out_specs=pl.BlockSpec((1, t, wv), lambda bi, g: (bi, 0, g)),
        out_shape=jax.ShapeDtypeStruct((b, t, A_HEADS * A_DV), BF16),
        scratch_shapes=[pltpu.VMEM((heads, A_DK, A_DV), F32), pltpu.VMEM((heads, 1, A_DK), F32),
                        pltpu.VMEM((heads, 1, 1), F32)],
        compiler_params=_params("parallel", "parallel"),
        name="mlstm",
    )(b_i, b_f, z, z, z, z, z, conv_w, conv_w, conv_b.reshape(1, -1), conv_b.reshape(1, -1),
      head_norm.reshape(1, -1))


def _gla_kernel(q_ref, k_ref, v_ref, og_ref, g_ref, p0_ref, p1_ref, p2_ref, hn_ref, o_ref, st_ref, *pre_ref,
                mode, heads, group):
    t_total = q_ref.shape[1]
    n_chunks = (t_total - N_META) // CHUNK
    dv, dk = st_ref.shape[1:]
    st_ref[...] = jnp.zeros_like(st_ref)

    hs = range(heads)
    kcol = [slice(hh * dk, (hh + 1) * dk) for hh in hs]
    vcol = [slice(hh * dv, (hh + 1) * dv) for hh in hs]

    if mode == "gla":
        gate = g_ref[0]
        gate_hi = _bf(gate)
        gate_lo = _bf(gate - gate_hi.astype(F32))
        for hh in hs:
            w = p0_ref[hh]
            w_hi = _bf(w)
            w_lo = _bf(w - w_hi.astype(F32))
            both = _dot(gate_hi, jnp.concatenate([w_hi, w_lo], axis=1))
            pre_ref[0][hh] = (both[:, :dk] + both[:, dk:]) + _dot(gate_lo, w_hi) + p1_ref[:, kcol[hh]]

    def gate_inputs(hh, o, length):
        q = q_ref[0, pl.ds(o, length), kcol[hh]]
        if mode == "gla":
            pre = pre_ref[0][hh, pl.ds(o, length), :]
            return q * (dk ** -0.5), k_ref[0, pl.ds(o, length), kcol[hh]], _log_sigmoid(pre) / GATE_TAU
        fpre = g_ref[0, pl.ds(o, length), kcol[hh]]
        a = p0_ref[:, kcol[hh]]
        bb = p1_ref[:, kcol[hh]] + _log_sigmoid(fpre)
        lg = jnp.maximum(a, bb) + jnp.log1p(jnp.exp(-jnp.abs(a - bb)))
        return q, p2_ref[:, kcol[hh]] * _sigmoid(-fpre), lg

    def cumsum_time(tri, lg):
        parts = _dot(tri, jnp.concatenate(_split3(lg), axis=1))
        return parts[:, :dk] + (parts[:, dk:2 * dk] + parts[:, 2 * dk:])

    def local_stage(o, length):
        row = lax.broadcasted_iota(jnp.int32, (length, length), 0)
        col = lax.broadcasted_iota(jnp.int32, (length, length), 1)
        causal = col <= row
        tri = jnp.where(causal, 1.0, 0.0).astype(BF16)
        qkl = [gate_inputs(hh, o, length) for hh in hs]
        vb = [_bf(v_ref[0, pl.ds(o, length), vcol[hh]]) for hh in hs]
        g = [cumsum_time(tri, qkl[hh][2]) for hh in hs]
        g_end = [g[hh][length - 1:length, :] for hh in hs]
        g_mid = [g[hh][length // 2:length // 2 + 1, :] for hh in hs]
        s = [_dot_nt(_bf(qkl[hh][0] * jnp.exp(g[hh] - g_mid[hh])), _bf(qkl[hh][1] * jnp.exp(g_mid[hh] - g[hh])))
             for hh in hs]
        q_dec = [_bf(qkl[hh][0] * jnp.exp(g[hh])) for hh in hs]
        local = [_dot_tn(vb[hh], _bf(qkl[hh][1] * jnp.exp(g_end[hh] - g[hh]))) for hh in hs]
        intra = [_dot(_bf(jnp.where(causal, s[hh], 0.0)), vb[hh]) for hh in hs]
        return q_dec, [jnp.exp(ge) for ge in g_end], local, intra

    def state_stage(o, length, staged, st_in):
        q_dec, decay, local, intra = staged
        inter = [_dot_nt(q_dec[hh], _bf(st_in[hh])) for hh in hs]
        st_out = [st_in[hh] * decay[hh] + local[hh] for hh in hs]
        for hh in hs:
            out = intra[hh] + inter[hh]
            hn = out * lax.rsqrt(jnp.mean(out * out, axis=-1, keepdims=True) + EPS) * hn_ref[:, vcol[hh]]
            og = og_ref[0, pl.ds(o, length), vcol[hh]]
            gate = _silu(og) if mode == "gla" else _sigmoid(og)
            o_ref[0, pl.ds(o, length), vcol[hh]] = (gate * hn).astype(o_ref.dtype)
        return st_out

    def sweep(offsets, length):
        staged = [local_stage(o, length) for o in offsets]
        st = [st_ref[hh] for hh in hs]
        for o, stg in zip(offsets, staged):
            st = state_stage(o, length, stg, st)
        for hh in hs:
            st_ref[hh] = st[hh]

    sweep([0], N_META)
    assert n_chunks % group == 0

    def body(c, carry):
        o = pl.multiple_of(N_META + c * (group * CHUNK), BF16_ROWS)
        sweep([pl.multiple_of(o + j * CHUNK, BF16_ROWS) for j in range(group)], CHUNK)
        return carry

    lax.fori_loop(0, n_chunks // group, body, 0)


def _gla_call(z, n_heads, heads, dk, dv, blocks, gate_width, params, head_norm, mode):
    b, t, _ = z.shape
    q0, k0, v0, og0, g0 = blocks
    zspec = lambda width, off, grouped=True: pl.BlockSpec(
        (1, t, width), (lambda bi, g: (bi, 0, off // width + (g if grouped else 0))))
    (p0, s0), (p1, s1), (p2, s2) = params
    gate_spec = zspec(gate_width, g0, grouped=False) if mode == "gla" else zspec(heads * dk, g0)
    return pl.pallas_call(
        functools.partial(_gla_kernel, mode=mode, heads=heads, group=8),
        grid=(b, n_heads // heads),
        in_specs=[
            zspec(heads * dk, q0), zspec(heads * dk, k0) if mode == "gla" else s2,
            zspec(heads * dv, v0), zspec(heads * dv, og0),
            gate_spec, s0, s1, s2,
            pl.BlockSpec((1, heads * dv), lambda bi, g: (0, g)),
        ],
        out_specs=pl.BlockSpec((1, t, heads * dv), lambda bi, g: (bi, 0, g)),
        out_shape=jax.ShapeDtypeStruct((b, t, n_heads * dv), BF16),
        scratch_shapes=[pltpu.VMEM((heads, dv, dk), F32)] + (
            [pltpu.VMEM((heads, t, dk), F32)] if mode == "gla" else []),
        compiler_params=_params("parallel", "parallel"),
        name="gla_" + mode,
    )(z, z if mode == "gla" else p2, z, z, z, p0, p1, p2, head_norm.reshape(1, -1))


def _mla_kernel(cq_ref, ckv_ref, kr_ref, tab_ref, gqa_ref, gkva_ref, wqn_ref, wqr_ref, wkn_ref, wv_ref,
                gqn_ref, gqr_ref, gkn_ref, gkr_ref, o_ref, qf_ref, kf_ref, vf_ref, *, heads):
    t_total = cq_ref.shape[1]
    n_blocks = (ATT_PAD + t_total) // ATT_BLOCK
    dqk = D_NOPE + D_ROPE
    scale = dqk ** -0.5
    rows = _row_tile(t_total, PREP_ROWS)

    hs = range(heads)
    col = [slice(hh * LANES, (hh + 1) * LANES) for hh in hs]
    for hh in hs:
        qf_ref[hh, 0:ATT_PAD, :] = jnp.zeros((ATT_PAD, 2 * LANES), BF16)
        kf_ref[hh, 0:ATT_PAD, :] = jnp.zeros((ATT_PAD, 2 * LANES), BF16)
        vf_ref[hh, 0:ATT_PAD, :] = jnp.zeros((ATT_PAD, D_V), BF16)

    def rope_pair(x, gains, tab):
        p = x * gains * tab
        return p + pltpu.roll(p, D_ROPE, 1)

    def latent_norm(ref, gain_ref, r0):
        u = ref[0, pl.ds(r0, rows), :]
        return _bf(u * lax.rsqrt(jnp.mean(u * u, axis=-1, keepdims=True) + EPS) * gain_ref[...])

    def up_project(c):
        r0 = c * rows
        cq = latent_norm(cq_ref, gqa_ref, r0)
        ckv = latent_norm(ckv_ref, gkva_ref, r0)
        return _dot(cq, wqn_ref[...]), _dot(cq, wqr_ref[...]), _dot(ckv, wkn_ref[...]), _dot(ckv, wv_ref[...])

    def prep(c, hh, projected):
        r0 = c * rows
        dst = ATT_PAD + c * rows
        tab = tab_ref[pl.ds(r0, rows), :]
        qn = projected[0][:, col[hh]]
        qr = projected[1][:, col[hh]]
        ssq = jnp.sum(qn * qn + 0.5 * (qr * qr), axis=-1, keepdims=True)
        rq = lax.rsqrt(ssq / dqk + EPS) * scale
        qf_ref[hh, pl.ds(dst, rows), 0:LANES] = _bf(qn * gqn_ref[...] * rq)
        qf_ref[hh, pl.ds(dst, rows), LANES:2 * LANES] = _bf(qr * gqr_ref[...] * tab * rq)
        kn = projected[2][:, col[hh]]
        kr = kr_ref[0, pl.ds(r0, rows), :]
        ssk = jnp.sum(kn * kn + 0.5 * (kr * kr), axis=-1, keepdims=True)
        rk = lax.rsqrt(ssk / dqk + EPS)
        kf_ref[hh, pl.ds(dst, rows), 0:LANES] = _bf(kn * gkn_ref[...] * rk)
        kf_ref[hh, pl.ds(dst, rows), LANES:2 * LANES] = _bf(rope_pair(kr, gkr_ref[...], tab) * rk)
        vf_ref[hh, pl.ds(dst, rows), :] = _bf(projected[3][:, col[hh]])

    for c in range(t_total // rows):
        projected = up_project(c)
        for hh in hs:
            prep(c, hh, projected)

    qpos = lax.broadcasted_iota(jnp.int32, (ATT_BLOCK, ATT_BLOCK), 0)
    kpos = lax.broadcasted_iota(jnp.int32, (ATT_BLOCK, ATT_BLOCK), 1)
    neg = -jnp.inf

    def scores(item):
        qi, hh = item
        q = qf_ref[hh, qi * ATT_BLOCK:(qi + 1) * ATT_BLOCK, :]
        return _dot_nt(q, kf_ref[hh, 0:(qi + 1) * ATT_BLOCK, :])

    items = [(qi, hh) for qi in range(n_blocks) for hh in hs]
    s_next = scores(items[0])
    for idx, (qi, hh) in enumerate(items):
        s = s_next
        if idx + 1 < len(items):
            s_next = scores(items[idx + 1])
        parts = [s[:, j * ATT_BLOCK:(j + 1) * ATT_BLOCK] for j in range(qi + 1)]
        parts[0] = jnp.where(kpos >= ATT_PAD, parts[0], neg)
        parts[qi] = jnp.where(kpos <= qpos, parts[qi], neg)
        top = functools.reduce(jnp.maximum, parts)
        m = jnp.max(top, axis=-1, keepdims=True)
        if qi == 0:
            m = jnp.where(m == neg, 0.0, m)
        probs = [jnp.exp(part - m) for part in parts]
        l = jnp.sum(functools.reduce(jnp.add, probs), axis=-1, keepdims=True)
        pv = _dot(jnp.concatenate([_bf(pr) for pr in probs], axis=1), vf_ref[hh, 0:(qi + 1) * ATT_BLOCK, :])
        if qi == 0:
            out = pv / jnp.where(l == 0.0, 1.0, l)
            o_ref[0, 0:N_META, col[hh]] = out[ATT_PAD:, :].astype(o_ref.dtype)
        else:
            dst = qi * ATT_BLOCK - ATT_PAD
            o_ref[0, dst:dst + ATT_BLOCK, col[hh]] = (pv / l).astype(o_ref.dtype)


def _mla(z, cq_col, tab, q_a_norm, kv_a_norm, wq, wkv, gqn, gqr, gkn, gkr):
    b, t, _ = z.shape
    assert (ATT_PAD + t) % ATT_BLOCK == 0 and t % BF16_ROWS == 0
    tp = ATT_PAD + t
    heads = 2
    groups = D_HEADS // heads
    ckv_col = cq_col + Q_LORA
    kr_col = ckv_col + KV_LORA
    wspec = lambda k, off: pl.BlockSpec((k, heads * LANES), lambda bi, g: (0, off + g))
    gspec = lambda width: pl.BlockSpec((1, width), lambda bi, g: (0, 0))
    return pl.pallas_call(
        functools.partial(_mla_kernel, heads=heads),
        grid=(b, groups),
        in_specs=[
            pl.BlockSpec((1, t, Q_LORA), lambda bi, g: (bi, 0, cq_col // Q_LORA)),
            pl.BlockSpec((1, t, KV_LORA), lambda bi, g: (bi, 0, ckv_col // KV_LORA)),
            pl.BlockSpec((1, t, LANES), lambda bi, g: (bi, 0, kr_col // LANES)),
            pl.BlockSpec((t, LANES), lambda bi, g: (0, 0)),
            gspec(Q_LORA), gspec(KV_LORA),
            wspec(Q_LORA, 0), wspec(Q_LORA, groups), wspec(KV_LORA, 0), wspec(KV_LORA, groups),
            gspec(LANES), gspec(LANES), gspec(LANES), gspec(LANES),
        ],
        out_specs=pl.BlockSpec((1, t, heads * D_V), lambda bi, g: (bi, 0, g)),
        out_shape=jax.ShapeDtypeStruct((b, t, D_HEADS * D_V), BF16),
        scratch_shapes=[pltpu.VMEM((heads, tp, 2 * LANES), BF16), pltpu.VMEM((heads, tp, 2 * LANES), BF16),
                        pltpu.VMEM((heads, tp, D_V), BF16)],
        compiler_params=_params("parallel", "parallel"),
        name="mla",
    )(z, z, z, tab, q_a_norm.reshape(1, -1), kv_a_norm.reshape(1, -1), wq, wq, wkv, wkv, gqn, gqr, gkn, gkr)


def _pack_bf16_pairs(v):
    w = v.shape[1] // 2
    bits = pltpu.bitcast(_bf(v).astype(F32), jnp.uint32)
    return (bits[:, :w] >> 16) | (bits[:, w:] & jnp.uint32(0xFFFF0000))


def _unpack_lo(words):
    return pltpu.bitcast(words << 16, F32)


def _unpack_hi(words):
    return pltpu.bitcast(words & jnp.uint32(0xFFFF0000), F32)


def _router_kernel(x_ref, g_ref, w_ref, b_ref, gate_ref, idx_ref, xg_ref, cnt_ref, carry_ref):
    tm = x_ref.shape[0]

    @pl.when(pl.program_id(0) == 0)
    def _():
        carry_ref[...] = jnp.zeros_like(carry_ref)

    x = x_ref[...]
    ms = jnp.mean(x * x, axis=-1, keepdims=True)
    xn = x * lax.rsqrt(ms + EPS) * g_ref[...]
    xh = _bf(xn)
    xl = _bf(xn - xh.astype(F32))
    w = w_ref[...]
    wh = _bf(w)
    wl = _bf(w - wh.astype(F32))
    both = _dot(xh, jnp.concatenate([wh, wl], axis=1))
    logits = (both[:, :LANES] + both[:, LANES:]) + _dot(xl, wh) + b_ref[...]
    lane = lax.broadcasted_iota(jnp.int32, logits.shape, 1)
    lane_f = lane.astype(F32)
    neg = -jnp.inf
    big = float(LANES)

    is_group = lane < N_GROUPS
    g_max = jnp.max(jnp.where(is_group, logits, neg), axis=-1, keepdims=True)
    g_sum = jnp.sum(jnp.where(is_group, jnp.exp(logits - g_max), 0.0), axis=-1, keepdims=True)
    p_top = 1.0 / g_sum
    grp = jnp.min(jnp.where(is_group & (logits == g_max), lane_f, big), axis=-1, keepdims=True)

    e_lo = N_GROUPS + grp * EXPERTS_PER_GROUP
    in_grp = (lane_f >= e_lo) & (lane_f < e_lo + EXPERTS_PER_GROUP)
    e_max = jnp.max(jnp.where(in_grp, logits, neg), axis=-1, keepdims=True)
    e_sum = jnp.sum(jnp.where(in_grp, jnp.exp(logits - e_max), 0.0), axis=-1, keepdims=True)
    i1 = jnp.min(jnp.where(in_grp & (logits == e_max), lane_f, big), axis=-1, keepdims=True)
    rest = in_grp & (lane_f != i1)
    e_2nd = jnp.max(jnp.where(rest, logits, neg), axis=-1, keepdims=True)
    i2 = jnp.min(jnp.where(rest & (logits == e_2nd), lane_f, big), axis=-1, keepdims=True)
    p1 = 1.0 / e_sum
    p2 = jnp.exp(e_2nd - e_max) / e_sum
    tot = p1 + p2
    gate_ref[...] = jnp.where(lane == 0, p_top * p1 / tot, jnp.where(lane == 1, p_top * p2 / tot, 0.0))

    e1 = i1 - N_GROUPS
    e2 = i2 - N_GROUPS
    hot = jnp.where((lane_f == e1) | (lane_f == e2), 1.0, 0.0)
    row = lax.broadcasted_iota(jnp.int32, (tm, tm), 0)
    col = lax.broadcasted_iota(jnp.int32, (tm, tm), 1)
    before = _dot(jnp.where(col < row, 1.0, 0.0).astype(BF16), _bf(hot)) + carry_ref[...]
    r1 = jnp.sum(jnp.where(lane_f == e1, before, 0.0), axis=-1, keepdims=True)
    r2 = jnp.sum(jnp.where(lane_f == e2, before, 0.0), axis=-1, keepdims=True)
    total = carry_ref[...] + jnp.sum(hot, axis=0, keepdims=True)
    carry_ref[...] = total
    cnt_ref[...] = jnp.broadcast_to(total, cnt_ref.shape).astype(jnp.int32)
    idx_ref[...] = jnp.where(lane == 0, e1, jnp.where(lane == 1, e2, jnp.where(lane == 2, r1, jnp.where(
        lane == 3, r2, 0.0)))).astype(jnp.int32)

    words = _pack_bf16_pairs(xn)
    for s in range(8):
        xg_ref[pl.ds(s, tm, stride=8), :] = words[:, s * LANES:(s + 1) * LANES]


def _router(x2d, gain, w_group, b_group, w_expert, b_expert):
    m, d = x2d.shape
    assert d == 2 * 8 * LANES
    tm = _row_tile(m, ROW_TILE)
    pad = LANES - N_GROUPS - N_EXPERTS
    w = jnp.concatenate([w_group, w_expert, jnp.zeros((d, pad), F32)], axis=1)
    bias = jnp.concatenate([b_group, b_expert, jnp.zeros((pad,), F32)]).reshape(1, LANES)
    return pl.pallas_call(
        _router_kernel,
        grid=(m // tm,),
        in_specs=[
            pl.BlockSpec((tm, d), lambda i: (i, 0)),
            pl.BlockSpec((1, d), lambda i: (0, 0)),
            pl.BlockSpec((d, LANES), lambda i: (0, 0)),
            pl.BlockSpec((1, LANES), lambda i: (0, 0)),
        ],
        out_specs=[pl.BlockSpec((tm, LANES), lambda i: (i, 0)), pl.BlockSpec((tm, LANES), lambda i: (i, 0)),
                   pl.BlockSpec((tm * 8, LANES), lambda i: (i, 0)), pl.BlockSpec((8, LANES), lambda i: (0, 0))],
        out_shape=[jax.ShapeDtypeStruct((m, LANES), F32), jax.ShapeDtypeStruct((m, LANES), jnp.int32),
                   jax.ShapeDtypeStruct((m * 8, LANES), jnp.uint32), jax.ShapeDtypeStruct((8, LANES), jnp.int32)],
        scratch_shapes=[pltpu.VMEM((1, LANES), F32)],
        compiler_params=_params("arbitrary"),
        name="router",
    )(x2d, gain.reshape(1, d), w, bias)


def _invert_kernel(dest_ref, inv_ref):
    def clear(s, carry):
        inv_ref[s] = -1
        return carry

    lax.fori_loop(0, inv_ref.shape[0], clear, 0, unroll=8)

    def put(f, carry):
        inv_ref[dest_ref[f]] = f
        return carry

    lax.fori_loop(0, dest_ref.shape[0], put, 0, unroll=8)


def _invert(dest, p):
    assert p % 8 == 0 and dest.shape[0] % 8 == 0
    smem = pl.BlockSpec(memory_space=pltpu.SMEM)
    return pl.pallas_call(
        _invert_kernel, in_specs=[smem], out_specs=smem,
        out_shape=jax.ShapeDtypeStruct((p,), jnp.int32), name="moe_invert",
    )(dest)


def _expert_kernel(be_ref, nxt_ref, run_ref, nu_ref, src_ref, dst_ref, xg_hbm, w1_hbm, w3_hbm, w2_hbm, o_hbm,
                   xbuf, ybuf, w1f, w3f, w2f, w1s, w3s, w2s, sem_in, sem_out, sem_w, *, layer):
    i = pl.program_id(0)
    n_used = nu_ref[0]
    par = i % 2
    half_e = D_EXPERT // 2

    def weight_copies(expert, slot):
        return [pltpu.make_async_copy(w_hbm.at[layer, expert], w_f.at[slot], sem_w.at[slot, j])
                for j, (w_hbm, w_f) in enumerate(((w1_hbm, w1f), (w3_hbm, w3f), (w2_hbm, w2f)))]

    def gather(block, r, slot):
        src = pl.multiple_of(src_ref[block * MOE_BLOCK + r], 8)
        return pltpu.make_async_copy(xg_hbm.at[pl.ds(src, 8), :], xbuf.at[slot, pl.ds(r * 8, 8), :], sem_in.at[slot])

    def scatter(block, r, slot):
        dst = pl.multiple_of(dst_ref[(block + 1) * MOE_BLOCK + r], 8)
        return pltpu.make_async_copy(ybuf.at[slot, pl.ds(r * 8, 8), :], o_hbm.at[pl.ds(dst, 8), :], sem_out.at[slot])

    def wait_gathers(slot):
        pltpu.make_async_copy(xbuf.at[1 - slot], xbuf.at[slot], sem_in.at[slot]).wait()

    def wait_scatters(slot):
        pltpu.make_async_copy(ybuf.at[slot], ybuf.at[1 - slot], sem_out.at[slot]).wait()

    @pl.when(i == 0)
    def _():
        ybuf[1] = jnp.zeros(ybuf.shape[1:], ybuf.dtype)
        n_real = o_hbm.shape[0] - 2 * MOE_BLOCK * 8
        fill1 = pltpu.make_async_copy(ybuf.at[1], o_hbm.at[pl.ds(n_real + MOE_BLOCK * 8, MOE_BLOCK * 8), :],
                                      sem_out.at[1])
        fill1.start()
        fill1.wait()
        pltpu.make_async_copy(ybuf.at[1], o_hbm.at[pl.ds(n_real, MOE_BLOCK * 8), :], sem_out.at[0]).start()

        def first(r, carry):
            gather(0, r, 0).start()
            return carry

        lax.fori_loop(0, MOE_BLOCK, first, 0, unroll=8)

    @pl.when(i < n_used)
    def _():
        expert = be_ref[i]
        wslot = run_ref[i] % 2

        @pl.when(i == 0)
        def _():
            for cp in weight_copies(expert, 0):
                cp.start()

        @pl.when((i == 0) | (expert != be_ref[jnp.maximum(i - 1, 0)]))
        def _():
            for cp in weight_copies(expert, wslot):
                cp.wait()

            @pl.when(nxt_ref[i] != expert)
            def _():
                for cp in weight_copies(nxt_ref[i], 1 - wslot):
                    cp.start()

            w1s[...] = _bf(w1f[wslot])
            w3s[...] = _bf(w3f[wslot])
            w2s[...] = _bf(w2f[wslot])

        wait_gathers(par)

        def move_rows(group, n_groups=6):
            lo, hi = group * MOE_BLOCK // n_groups, (group + 1) * MOE_BLOCK // n_groups
            for r in range(lo, hi):
                gather(i + 1, r, 1 - par).start(priority=r % 2)
                scatter(i - 1, r, 1 - par).start(priority=(r + 1) % 2)

        tiles = [xbuf[par, pl.ds(s, MOE_BLOCK, stride=8), :] for s in range(8)]
        xb = jnp.concatenate([_bf(_unpack_lo(w)) for w in tiles] + [_bf(_unpack_hi(w)) for w in tiles], axis=1)
        move_rows(0)
        h1a = _dot(xb, w1s[:, :half_e])
        move_rows(1)
        h3a = _dot(xb, w3s[:, :half_e])
        move_rows(2)
        act_a = _bf(_silu(h1a) * h3a)
        h1b = _dot(xb, w1s[:, half_e:])
        move_rows(3)
        h3b = _dot(xb, w3s[:, half_e:])
        move_rows(4)
        act_b = _bf(_silu(h1b) * h3b)
        y = _dot(act_a, w2s[:half_e, :])
        move_rows(5)
        y = y + _dot(act_b, w2s[half_e:, :])

        wait_scatters(par)
        words = _pack_bf16_pairs(y)
        for s in range(8):
            ybuf[par, pl.ds(s, MOE_BLOCK, stride=8), :] = words[:, s * LANES:(s + 1) * LANES]

        @pl.when(i == n_used - 1)
        def _():
            def last(r, carry):
                scatter(i, r, par).start()
                return carry

            lax.fori_loop(0, MOE_BLOCK, last, 0, unroll=8)
            wait_scatters(par)
            wait_scatters(1 - par)
            wait_gathers(1 - par)


def _moe(x2d, gain, w_group, b_group, w_expert, b_expert, w1, w3, w2, layer, final_shape=None):
    n, d = x2d.shape
    gates_l, idx_l, xg, cnt = _router(x2d, gain, w_group, b_group, w_expert, b_expert)

    a = n * TOP_K
    n_blocks = -(-a // MOE_BLOCK) + N_EXPERTS
    p = n_blocks * MOE_BLOCK
    counts = cnt[0, :N_EXPERTS]
    padded = (counts + MOE_BLOCK - 1) // MOE_BLOCK * MOE_BLOCK
    pad_end = jnp.cumsum(padded)
    pad_start = pad_end - padded
    e_hot = idx_l[:, :TOP_K, None] == jnp.arange(N_EXPERTS, dtype=jnp.int32)
    dest = jnp.sum(jnp.where(e_hot, pad_start, 0), axis=-1) + idx_l[:, TOP_K:2 * TOP_K]
    dest = jnp.clip(dest.reshape(-1), 0, p - 1).astype(jnp.int32)
    blk0 = jnp.arange(n_blocks, dtype=jnp.int32) * MOE_BLOCK
    block_expert = jnp.minimum(jnp.searchsorted(pad_end, blk0, side="right"), N_EXPERTS - 1).astype(jnp.int32)
    n_used = (pad_end[-1] // MOE_BLOCK).astype(jnp.int32).reshape(1)
    experts = jnp.arange(N_EXPERTS, dtype=jnp.int32)
    later = jnp.where((experts[None, :] > experts[:, None]) & (counts[None, :] > 0), experts[None, :], N_EXPERTS)
    next_owner = jnp.min(later, axis=1)
    next_expert = jnp.where(next_owner < N_EXPERTS, next_owner, experts)[block_expert].astype(jnp.int32)
    run_index = (jnp.cumsum(jnp.concatenate([jnp.ones((1,), jnp.int32),
                                             (block_expert[1:] != block_expert[:-1]).astype(jnp.int32)])) - 1
                 ).astype(jnp.int32)
    codes = _invert(dest, p)
    slot = jnp.arange(p, dtype=jnp.int32)
    spare = TOP_K * n + (slot // MOE_BLOCK % 2) * MOE_BLOCK + slot % MOE_BLOCK
    src_tok = (jnp.maximum(codes, 0) >> 1) * 8
    dst_row = jnp.where(codes >= 0, (codes & 1) * n + (codes >> 1), spare) * 8
    lead = (TOP_K * n + MOE_BLOCK + jnp.arange(MOE_BLOCK, dtype=jnp.int32)) * 8
    dst_row = jnp.concatenate([lead, dst_row])

    hbm = pl.BlockSpec(memory_space=pl.ANY)
    out_rows = TOP_K * n + 2 * MOE_BLOCK
    out2 = pl.pallas_call(
        functools.partial(_expert_kernel, layer=layer),
        grid_spec=pltpu.PrefetchScalarGridSpec(
            num_scalar_prefetch=6,
            grid=(n_blocks,),
            in_specs=[hbm, hbm, hbm, hbm],
            out_specs=hbm,
            scratch_shapes=[
                pltpu.VMEM((2, MOE_BLOCK * 8, LANES), jnp.uint32), pltpu.VMEM((2, MOE_BLOCK * 8, LANES), jnp.uint32),
                pltpu.VMEM((2, d, D_EXPERT), F32), pltpu.VMEM((2, d, D_EXPERT), F32), pltpu.VMEM((2, D_EXPERT, d), F32),
                pltpu.VMEM((d, D_EXPERT), BF16), pltpu.VMEM((d, D_EXPERT), BF16), pltpu.VMEM((D_EXPERT, d), BF16),
                pltpu.SemaphoreType.DMA((2,)), pltpu.SemaphoreType.DMA((2,)), pltpu.SemaphoreType.DMA((2, 3)),
            ],
        ),
        out_shape=jax.ShapeDtypeStruct((out_rows * 8, LANES), jnp.uint32),
        compiler_params=_params("arbitrary"),
        name="moe_experts",
    )(block_expert, next_expert, run_index, n_used, src_tok, dst_row, xg, w1, w3, w2)
    return _combine(x2d, gates_l, out2, final_shape)


def _combine_tile(x_ref, gate_ref, a_ref, b_ref, o_ref):
    tm = x_ref.shape[0]
    half = x_ref.shape[1] // 2
    g0 = gate_ref[:, 0:1]
    g1 = gate_ref[:, 1:2]
    for s in range(8):
        wa = a_ref[pl.ds(s, tm, stride=8), :]
        wb = b_ref[pl.ds(s, tm, stride=8), :]
        lo = slice(s * LANES, (s + 1) * LANES)
        hi = slice(half + s * LANES, half + (s + 1) * LANES)
        o_ref[:, lo] = x_ref[:, lo] + (g0 * _unpack_lo(wa) + g1 * _unpack_lo(wb))
        o_ref[:, hi] = x_ref[:, hi] + (g0 * _unpack_hi(wa) + g1 * _unpack_hi(wb))


def _combine_kernel(x_ref, gate_ref, a_ref, b_ref, o_ref):
    _combine_tile(x_ref, gate_ref, a_ref, b_ref, o_ref)


def _combine_final_kernel(x_ref, gate_ref, a_ref, b_ref, o_hbm, obuf, sem, *, tiles_per_batch):
    i = pl.program_id(0)
    n_steps = pl.num_programs(0)
    tm = x_ref.shape[0]

    def copies(step, slot):
        batch, j = step // tiles_per_batch, step % tiles_per_batch
        row = pl.multiple_of(j * tm, 8)
        main = pltpu.make_async_copy(obuf.at[slot, pl.ds(N_META, tm - N_META), :],
                                     o_hbm.at[batch, pl.ds(row, tm - N_META), :], sem.at[slot, 0])
        head = pltpu.make_async_copy(obuf.at[slot, pl.ds(0, N_META), :],
                                     o_hbm.at[batch, pl.ds(pl.multiple_of(jnp.maximum(row - N_META, 0), 8), N_META), :],
                                     sem.at[slot, 1])
        return main, head, j > 0

    def wait_step(step, slot):
        main, head, has_head = copies(step, slot)
        main.wait()

        @pl.when(has_head)
        def _():
            head.wait()

    slot = i % 2

    @pl.when(i >= 2)
    def _():
        wait_step(i - 2, slot)

    _combine_tile(x_ref, gate_ref, a_ref, b_ref, obuf.at[slot])
    main, head, has_head = copies(i, slot)
    main.start()

    @pl.when(has_head)
    def _():
        head.start()

    @pl.when(i == n_steps - 1)
    def _():
        wait_step(i, slot)

        @pl.when(i >= 1)
        def _():
            wait_step(i - 1, 1 - slot)


def _combine(x2d, gates, out2, final_shape=None):
    n, d = x2d.shape
    tm = _row_tile(n, ROW_TILE)
    in_specs = [
        pl.BlockSpec((tm, d), lambda i: (i, 0)),
        pl.BlockSpec((tm, LANES), lambda i: (i, 0)),
        pl.BlockSpec((tm * 8, LANES), lambda i: (i, 0)),
        pl.BlockSpec((tm * 8, LANES), lambda i: (n // tm + i, 0)),
    ]
    if final_shape is None:
        return pl.pallas_call(
            _combine_kernel,
            grid=(n // tm,),
            in_specs=in_specs,
            out_specs=pl.BlockSpec((tm, d), lambda i: (i, 0)),
            out_shape=jax.ShapeDtypeStruct((n, d), F32),
            compiler_params=_params("parallel"),
            name="moe_combine",
        )(x2d, gates, out2, out2)
    b, t = final_shape
    assert t % tm == 0 and tm > N_META
    return pl.pallas_call(
        functools.partial(_combine_final_kernel, tiles_per_batch=t // tm),
        grid=(n // tm,),
        in_specs=in_specs,
        out_specs=pl.BlockSpec(memory_space=pl.ANY),
        out_shape=jax.ShapeDtypeStruct((b, t - N_META, d), F32),
        scratch_shapes=[pltpu.VMEM((2, tm, d), F32), pltpu.SemaphoreType.DMA((2, 2))],
        compiler_params=_params("arbitrary"),
        name="moe_combine_final",
    )(x2d, gates, out2, out2)


def _even_layer(x, norm_g, w_in, conv_w, conv_b, b_i, b_f, a_norm, w_gate2, b_gate, b_norm, w_out):
    b, t, d = x.shape
    n = b * t
    a_w = 2 * A_HEADS * A_DK + 2 * A_HEADS * A_DV
    g_w = 2 * A_HEADS
    b_w = 2 * B_HEADS * B_DK + 2 * B_HEADS * B_DV
    main = a_w + b_w
    w = _even_weight(w_in, a_w, g_w, b_w, GATE_RANK)
    z = _normproj(x.reshape(n, d), norm_g, w).reshape(b, t, main + MXU_DIM)

    ya = _mlstm(z, main, conv_w, conv_b, b_i, b_f, a_norm)

    wg = jnp.zeros((B_HEADS, MXU_DIM, B_DK), F32).at[:, g_w:g_w + GATE_RANK, :].set(
        w_gate2.reshape(GATE_RANK, B_HEADS, B_DK).transpose(1, 0, 2))
    dummy = jnp.zeros((1, B_HEADS * B_DK), F32)
    hp = 2
    hspec = pl.BlockSpec((1, hp * B_DK), lambda bi, g: (0, g))
    yb = _gla_call(
        z, B_HEADS, hp, B_DK, B_DV,
        (a_w, a_w + B_HEADS * B_DK, a_w + 2 * B_HEADS * B_DK, a_w + 2 * B_HEADS * B_DK + B_HEADS * B_DV, main),
        MXU_DIM,
        ((wg, pl.BlockSpec((hp, MXU_DIM, B_DK), lambda bi, g: (g, 0, 0))),
         (b_gate.reshape(1, -1), hspec), (dummy, hspec)),
        b_norm, "gla")
    return _outproj(ya.reshape(n, -1), yb.reshape(n, -1), w_out.astype(BF16), x.reshape(n, d)).reshape(b, t, d)


def _odd_layer(x, lb, norm_g, w_in, c_norm, q_a_norm, w_q_up, kv_a_norm, w_kv_up, q_norm, k_norm, w_out):
    b, t, d = x.shape
    n = b * t
    c_w = 2 * C_HEADS * C_DK + 2 * C_HEADS * C_DV
    swap = (jnp.arange(D_ROPE) + D_ROPE // 2) % D_ROPE
    kr0 = c_w + Q_LORA + KV_LORA
    used = kr0 + 2 * D_ROPE
    total = -(-used // MXU_DIM) * MXU_DIM
    z2 = _normproj(x.reshape(n, d), norm_g, _odd_weight(w_in, kr0))
    z = z2.reshape(b, t, total)

    hp = 4
    hspec = pl.BlockSpec((1, hp * C_DK), lambda bi, g: (0, g))
    yc = _gla_call(
        z, C_HEADS, hp, C_DK, C_DV,
        (0, C_HEADS * C_DK, 2 * C_HEADS * C_DK, 2 * C_HEADS * C_DK + C_HEADS * C_DV, C_HEADS * C_DK),
        C_DK,
        ((jnp.log(lb).reshape(1, -1), hspec), (jnp.log1p(-lb).reshape(1, -1), hspec), ((1.0 - lb).reshape(1, -1), hspec)),
        c_norm, "hgrn")

    dq = D_NOPE + D_ROPE
    wq = w_q_up.reshape(Q_LORA, D_HEADS, dq)
    wq_rope = wq[:, :, D_NOPE:]
    wq_p = jnp.concatenate([wq[:, :, :D_NOPE].reshape(Q_LORA, -1),
                            jnp.concatenate([wq_rope, wq_rope[:, :, swap]], axis=-1).reshape(Q_LORA, -1)],
                           axis=1).astype(BF16)
    wkv = w_kv_up.reshape(KV_LORA, D_HEADS, D_NOPE + D_V)
    wkv_p = jnp.concatenate([wkv[:, :, :D_NOPE].reshape(KV_LORA, -1), wkv[:, :, D_NOPE:].reshape(KV_LORA, -1)],
                            axis=1).astype(BF16)
    pos = jnp.arange(t, dtype=F32)
    half = D_ROPE // 2
    inv = ROPE_THETA ** (-jnp.arange(half, dtype=F32) / half)
    ang = pos[:, None] * inv[None, :]
    cos, sin = jnp.cos(ang), jnp.sin(ang)
    tab = jnp.concatenate([cos, cos, -sin, sin], axis=1)
    pair = lambda g: jnp.concatenate([g[D_NOPE:], g[D_NOPE:][swap]]).reshape(1, LANES)
    yd = _mla(z, c_w, tab, q_a_norm, kv_a_norm, wq_p, wkv_p, q_norm[:D_NOPE].reshape(1, LANES), pair(q_norm),
              k_norm[:D_NOPE].reshape(1, LANES), pair(k_norm))
    return _outproj(yc.reshape(n, -1), yd.reshape(n, -1), w_out.astype(BF16), x.reshape(n, d)).reshape(b, t, d)


def kernel(x, meta_tokens, ab_norm, ab_w_in, a_conv_w, a_conv_b, a_b_i, a_b_f, a_head_norm, b_w_gate2, b_b_gate, b_head_norm, ab_w_out, cd_norm, cd_w_in, c_lower_bound, c_head_norm, d_q_a_norm, d_w_q_up, d_kv_a_norm, d_w_kv_up, d_q_norm, d_k_norm, cd_w_out, moe_norm, moe_w_group, moe_b_group, moe_w_expert, moe_b_expert, moe_w1, moe_w3, moe_w2):
    b = x.shape[0]
    depth = moe_norm.shape[0]
    h = jnp.concatenate([jnp.broadcast_to(meta_tokens.astype(x.dtype)[None], (b, N_META, D_MODEL)), x], axis=1)
    t = h.shape[1]
    lb_cum = jnp.cumsum(jax.nn.softmax(c_lower_bound.astype(F32), axis=0), axis=0)
    lower_bounds = lb_cum - lb_cum[0]
    for layer in range(depth):
        j = layer // 2
        if layer % 2 == 0:
            h = _even_layer(h, ab_norm[j], ab_w_in[j], a_conv_w[j], a_conv_b[j], a_b_i[j], a_b_f[j], a_head_norm[j],
                            b_w_gate2[j], b_b_gate[j], b_head_norm[j], ab_w_out[j])
        else:
            h = _odd_layer(h, lower_bounds[layer], cd_norm[j], cd_w_in[j], c_head_norm[j], d_q_a_norm[j],
                           d_w_q_up[j], d_kv_a_norm[j], d_w_kv_up[j], d_q_norm[j], d_k_norm[j], cd_w_out[j])
        last = layer == depth - 1
        h = _moe(h.reshape(b * t, D_MODEL), moe_norm[layer], moe_w_group[layer], moe_b_group[layer],
                 moe_w_expert[layer], moe_b_expert[layer], moe_w1, moe_w3, moe_w2, layer,
                 final_shape=(b, t) if last else None)
        if not last:
            h = h.reshape(b, t, D_MODEL)
    return h
```

```python
import functools

import jax
import jax.numpy as jnp
from jax import lax
from jax.experimental import pallas as pl
from jax.experimental.pallas import tpu as pltpu

F32 = jnp.float32
BF16 = jnp.bfloat16

D_MODEL = 2048
N_META = 16
CHUNK = 64
CONV_K = 4
EPS = 1e-6
A_HEADS, A_DK, A_DV = 4, 128, 256
B_HEADS, B_DK, B_DV = 4, 128, 256
GATE_RANK = 16
GATE_TAU = 16.0
C_HEADS, C_DK, C_DV = 8, 128, 128
D_HEADS, D_NOPE, D_ROPE, D_V = 8, 128, 64, 128
Q_LORA, KV_LORA = 512, 256
ROPE_THETA = 10000.0
N_GROUPS, EXPERTS_PER_GROUP = 4, 8
N_EXPERTS = N_GROUPS * EXPERTS_PER_GROUP
TOP_K = 2
D_EXPERT = 512

LANES = 128
MXU_DIM = 256
BF16_ROWS = 16
VMEM_LIMIT = 56 * 1024 * 1024
MOE_BLOCK = MXU_DIM
ROW_TILE = 688
ROW_TILE_BF16 = 2 * ROW_TILE
COL_TILE = 1280
COL_TILE_OUT = 1024
PREP_ROWS = 768
ATT_BLOCK = 256
ATT_PAD = ATT_BLOCK - N_META

_NT = (((1,), (1,)), ((), ()))
_TN = (((0,), (0,)), ((), ()))


def _dot(a, b):
    return jnp.dot(a, b, preferred_element_type=F32)


def _dot_nt(a, b):
    return lax.dot_general(a, b, _NT, preferred_element_type=F32)


def _dot_tn(a, b):
    return lax.dot_general(a, b, _TN, preferred_element_type=F32)


def _bf(x):
    return x.astype(BF16)


def _split3(x):
    hi = _bf(x)
    rest = x - hi.astype(F32)
    mid = _bf(rest)
    return hi, mid, _bf(rest - mid.astype(F32))


def _log_sigmoid(x):
    return jnp.minimum(x, 0.0) - jnp.log1p(jnp.exp(-jnp.abs(x)))


def _sigmoid(x):
    return 1.0 / (1.0 + jnp.exp(-x))


def _silu(x):
    return x * _sigmoid(x)


def _row_tile(m, cap):
    best = None
    for t in range(BF16_ROWS, min(m, cap) + 1, BF16_ROWS):
        if m % t == 0:
            best = t
    assert best is not None, m
    return best


def _col_tile(n, cap):
    best = None
    for t in range(MXU_DIM, min(n, cap) + 1, MXU_DIM):
        if n % t == 0:
            best = t
    assert best is not None, n
    return best


def _params(*sem):
    return pltpu.CompilerParams(dimension_semantics=sem, vmem_limit_bytes=VMEM_LIMIT)


_RELAYOUT_ROWS = 256


def _even_weight_kernel(wa_ref, wb_ref, wc_ref, o_ref, *, n_plain, n_shift, shift, gate_cols):
    ob = pl.program_id(0)
    rows = o_ref.shape[0]
    chunks = rows // _RELAYOUT_ROWS

    @pl.when(ob < n_plain)
    def _():
        def body(c, carry):
            r = pl.ds(pl.multiple_of(c * _RELAYOUT_ROWS, _RELAYOUT_ROWS), _RELAYOUT_ROWS)
            o_ref[r, :] = _bf(wa_ref[r, :])
            return carry

        lax.fori_loop(0, chunks, body, 0)

    @pl.when((ob >= n_plain) & (ob < n_plain + n_shift))
    def _():
        def body(c, carry):
            r = pl.ds(pl.multiple_of(c * _RELAYOUT_ROWS, _RELAYOUT_ROWS), _RELAYOUT_ROWS)
            wide = jnp.concatenate([wa_ref[r, :], wb_ref[r, :]], axis=1)
            o_ref[r, :] = _bf(wide[:, shift:shift + MXU_DIM])
            return carry

        lax.fori_loop(0, chunks, body, 0)

    @pl.when(ob == n_plain + n_shift)
    def _():
        lane = lax.broadcasted_iota(jnp.int32, (_RELAYOUT_ROWS, LANES), 1)

        def body(c, carry):
            r = pl.ds(pl.multiple_of(c * _RELAYOUT_ROWS, _RELAYOUT_ROWS), _RELAYOUT_ROWS)
            first = jnp.where(lane < shift, wc_ref[r, :], jnp.where(lane < gate_cols, wb_ref[r, :], 0.0))
            o_ref[r, :] = _bf(jnp.concatenate([first, jnp.zeros_like(first)], axis=1))
            return carry

        lax.fori_loop(0, chunks, body, 0)


def _even_weight(w_in, a_w, g_w, b_w, rank):
    d = w_in.shape[0]
    assert a_w % MXU_DIM == 0 and b_w % MXU_DIM == 0 and g_w + rank <= LANES and d % _RELAYOUT_ROWS == 0
    n_plain, n_shift = a_w // MXU_DIM, b_w // MXU_DIM
    n_out = n_plain + n_shift + 1
    last = n_out - 1

    def b_index(ob):
        return (0, jnp.where(ob < n_plain, 0, jnp.where(ob < last, 2 * (ob + 1), (a_w + g_w + b_w) // LANES)))

    return pl.pallas_call(
        functools.partial(_even_weight_kernel, n_plain=n_plain, n_shift=n_shift, shift=g_w, gate_cols=g_w + rank),
        grid=(n_out,),
        in_specs=[
            pl.BlockSpec((d, MXU_DIM), lambda ob: (0, jnp.minimum(ob, last - 1))),
            pl.BlockSpec((d, LANES), b_index),
            pl.BlockSpec((d, LANES), lambda ob: (0, a_w // LANES)),
        ],
        out_specs=pl.BlockSpec((d, MXU_DIM), lambda ob: (0, ob)),
        out_shape=jax.ShapeDtypeStruct((d, n_out * MXU_DIM), BF16),
        compiler_params=_params("parallel"),
        name="even_weight",
    )(w_in, w_in, w_in)


def _odd_weight_kernel(w_ref, o_ref, *, n_plain):
    ob = pl.program_id(0)
    chunks = o_ref.shape[0] // _RELAYOUT_ROWS
    half = D_ROPE // 2

    def body(c, carry):
        r = pl.ds(pl.multiple_of(c * _RELAYOUT_ROWS, _RELAYOUT_ROWS), _RELAYOUT_ROWS)
        w = w_ref[r, :]

        @pl.when(ob < n_plain)
        def _():
            o_ref[r, :] = _bf(w)

        @pl.when(ob == n_plain)
        def _():
            pair = jnp.concatenate([w[:, :D_ROPE], w[:, half:D_ROPE], w[:, :half]], axis=1)
            o_ref[r, :] = _bf(jnp.concatenate([pair, jnp.zeros_like(pair)], axis=1))

        return carry

    lax.fori_loop(0, chunks, body, 0)


def _odd_weight(w_in, kr0):
    d = w_in.shape[0]
    assert kr0 % MXU_DIM == 0 and w_in.shape[1] == kr0 + D_ROPE and d % _RELAYOUT_ROWS == 0
    n_plain = kr0 // MXU_DIM
    return pl.pallas_call(
        functools.partial(_odd_weight_kernel, n_plain=n_plain),
        grid=(n_plain + 1,),
        in_specs=[pl.BlockSpec((d, MXU_DIM), lambda ob: (0, ob))],
        out_specs=pl.BlockSpec((d, MXU_DIM), lambda ob: (0, ob)),
        out_shape=jax.ShapeDtypeStruct((d, kr0 + MXU_DIM), BF16),
        compiler_params=_params("parallel"),
        name="odd_weight",
    )(w_in)


def _normproj_kernel(x_ref, g_ref, w_ref, o_ref, xs_ref):
    tm = xs_ref.shape[0]

    @pl.when(pl.program_id(1) == 0)
    def _():
        def body(c, carry):
            r0 = pl.multiple_of(c * BF16_ROWS, BF16_ROWS)
            x = x_ref[pl.ds(r0, BF16_ROWS), :]
            ms = jnp.mean(x * x, axis=-1, keepdims=True)
            xs_ref[pl.ds(r0, BF16_ROWS), :] = _bf(x * lax.rsqrt(ms + EPS) * g_ref[...])
            return carry

        lax.fori_loop(0, tm // BF16_ROWS, body, 0, unroll=8)

    o_ref[...] = _dot(xs_ref[...], w_ref[...]).astype(o_ref.dtype)


def _normproj(x2d, gain, w):
    m, k = x2d.shape
    n = w.shape[1]
    tm = _row_tile(m, ROW_TILE)
    tn = _col_tile(n, COL_TILE)
    return pl.pallas_call(
        _normproj_kernel,
        grid=(m // tm, n // tn),
        in_specs=[
            pl.BlockSpec((tm, k), lambda i, j: (i, 0)),
            pl.BlockSpec((1, k), lambda i, j: (0, 0)),
            pl.BlockSpec((k, tn), lambda i, j: (0, j)),
        ],
        out_specs=pl.BlockSpec((tm, tn), lambda i, j: (i, j)),
        out_shape=jax.ShapeDtypeStruct((m, n), F32),
        scratch_shapes=[pltpu.VMEM((tm, k), BF16)],
        compiler_params=_params("parallel", "arbitrary"),
        name="normproj",
    )(x2d, gain.reshape(1, k).astype(F32), w)


def _outproj_kernel(ya_ref, yb_ref, w_ref, r_ref, o_ref):
    ka = ya_ref.shape[1]
    acc = _dot(ya_ref[...], w_ref[:ka, :]) + _dot(yb_ref[...], w_ref[ka:, :])
    o_ref[...] = r_ref[...] + acc


def _outproj(ya, yb, w, res):
    m, ka = ya.shape
    kb = yb.shape[1]
    n = w.shape[1]
    tm = _row_tile(m, ROW_TILE_BF16)
    tn = _col_tile(n, COL_TILE_OUT)
    return pl.pallas_call(
        _outproj_kernel,
        grid=(m // tm, n // tn),
        in_specs=[
            pl.BlockSpec((tm, ka), lambda i, j: (i, 0)),
            pl.BlockSpec((tm, kb), lambda i, j: (i, 0)),
            pl.BlockSpec((ka + kb, tn), lambda i, j: (0, j)),
            pl.BlockSpec((tm, tn), lambda i, j: (i, j)),
        ],
        out_specs=pl.BlockSpec((tm, tn), lambda i, j: (i, j)),
        out_shape=jax.ShapeDtypeStruct((m, n), F32),
        compiler_params=_params("parallel", "arbitrary"),
        name="outproj",
    )(ya, yb, w, res)


def _mlstm_kernel(bi_ref, bf_ref, q_ref, k_ref, v_ref, og_ref, gt_ref,
                  cwq_ref, cwk_ref, cbq_ref, cbk_ref, hn_ref, o_ref, c_ref, n_ref, m_ref, *, heads, group):
    t_total = q_ref.shape[1]
    n_chunks = (t_total - N_META) // CHUNK
    head0 = pl.program_id(1) * heads

    c_ref[...] = jnp.zeros_like(c_ref)
    n_ref[...] = jnp.zeros_like(n_ref)
    m_ref[...] = jnp.zeros_like(m_ref)

    def conv(taps, cw, cb, length):
        y = cb
        for j in range(CONV_K):
            y = y + taps[j] * cw[j:j + 1, :]
        return _silu(y)

    def window_taps(win, length):
        return [win[8 - (CONV_K - 1) + j:8 - (CONV_K - 1) + j + length, :] for j in range(CONV_K)]

    hs = range(heads)
    kcol = [slice(hh * A_DK, (hh + 1) * A_DK) for hh in hs]
    vcol = [slice(hh * A_DV, (hh + 1) * A_DV) for hh in hs]

    def gates(hh, blk, blk_parts, length, causal, upper):
        head = head0 + hh
        b_i = bi_ref[head]
        b_f = bf_ref[head]
        lane = lax.broadcasted_iota(jnp.int32, (length, LANES), 1)
        ig_c = jnp.sum(jnp.where(lane == head, blk, 0.0), axis=1, keepdims=True) + b_i
        lf_c = _log_sigmoid(jnp.sum(jnp.where(lane == A_HEADS + head, blk, 0.0), axis=1, keepdims=True) + b_f)
        sel_r = lax.broadcasted_iota(jnp.int32, (8, LANES), 0)
        sel_l = lax.broadcasted_iota(jnp.int32, (8, LANES), 1)
        sel = jnp.where(sel_l == head + A_HEADS * sel_r, 1.0, 0.0).astype(BF16)
        rows = _dot_nt(sel, blk_parts[0]) + (_dot_nt(sel, blk_parts[1]) + _dot_nt(sel, blk_parts[2]))
        ig_r = rows[0:1, :] + b_i
        lf_r = _log_sigmoid(rows[1:2, :] + b_f)
        b_c = jnp.sum(jnp.where(causal, lf_r, 0.0), axis=1, keepdims=True)
        b_r = jnp.sum(jnp.where(upper, lf_c, 0.0), axis=0, keepdims=True)
        b_end = b_c[length - 1:length, :]
        w_end = b_end - b_c + ig_c
        m_loc = jnp.max(w_end, axis=0, keepdims=True)
        d = jnp.where(causal, b_c - b_r + ig_r, -jnp.inf)
        return b_c, b_end, jnp.exp(w_end - m_loc), m_loc, d, jnp.max(d, axis=1, keepdims=True)

    def local_stage(o, length, qwin, kwin):
        row = lax.broadcasted_iota(jnp.int32, (length, length), 0)
        col = lax.broadcasted_iota(jnp.int32, (length, length), 1)
        causal = col <= row
        blk = gt_ref[0, pl.ds(o, length), 0:LANES]
        blk_parts = _split3(blk)
        gt = [gates(hh, blk, blk_parts, length, causal, row <= col) for hh in hs]
        q = [conv(qwin[hh], cwq_ref[:, kcol[hh]], cbq_ref[:, kcol[hh]], length) for hh in hs]
        k = [conv(kwin[hh], cwk_ref[:, kcol[hh]], cbk_ref[:, kcol[hh]], length) * (A_DK ** -0.5) for hh in hs]
        vb = [_bf(v_ref[0, pl.ds(o, length), vcol[hh]]) for hh in hs]
        qb = [_bf(q[hh]) for hh in hs]
        k_w = [k[hh] * gt[hh][2] for hh in hs]
        qk = [_dot_nt(qb[hh], _bf(k[hh])) for hh in hs]
        c_loc = [_dot_tn(_bf(k_w[hh]), vb[hh]) for hh in hs]
        n_loc = [jnp.sum(k_w[hh], axis=0, keepdims=True) for hh in hs]
        return gt, q, qb, vb, qk, c_loc, n_loc

    def state_stage(o, length, staged, state):
        gt, q, qb, vb, qk, c_loc, n_loc = staged
        c_in, n_in, m_in = state
        q_c = [_dot(qb[hh], _bf(c_in[hh])) for hh in hs]
        s, a_t, m_t, q_n, c_out, n_out, m_out = [], [], [], [], [], [], []
        for hh in hs:
            b_c, b_end, _, m_loc, d, d_max = gt[hh]
            inter = b_c + m_in[hh]
            m_t.append(jnp.maximum(inter, d_max))
            s.append(qk[hh] * jnp.exp(d - m_t[hh]))
            a_t.append(jnp.exp(inter - m_t[hh]))
            m_new = jnp.maximum(b_end + m_in[hh], m_loc)
            a = jnp.exp(b_end + m_in[hh] - m_new)
            c = jnp.exp(m_loc - m_new)
            c_out.append(a * c_in[hh] + c * c_loc[hh])
            n_out.append(a * n_in[hh] + c * n_loc[hh])
            m_out.append(m_new)
            q_n.append(jnp.sum(q[hh] * n_in[hh], axis=1, keepdims=True))
        num = [_dot(_bf(s[hh]), vb[hh]) + a_t[hh] * q_c[hh] for hh in hs]
        for hh in hs:
            den = jnp.sum(s[hh], axis=1, keepdims=True) + a_t[hh] * q_n[hh]
            h = num[hh] / jnp.maximum(jnp.abs(den), jnp.exp(-m_t[hh]))
            hn = h * lax.rsqrt(jnp.mean(h * h, axis=-1, keepdims=True) + EPS) * hn_ref[:, vcol[hh]]
            y = _sigmoid(og_ref[0, pl.ds(o, length), vcol[hh]]) * hn
            o_ref[0, pl.ds(o, length), vcol[hh]] = y.astype(o_ref.dtype)
        return c_out, n_out, m_out

    def sweep(offsets, length, qwins, kwins):
        staged = [local_stage(o, length, qw, kw) for o, qw, kw in zip(offsets, qwins, kwins)]
        state = ([c_ref[hh] for hh in hs], [n_ref[hh] for hh in hs], [m_ref[hh] for hh in hs])
        for o, stg in zip(offsets, staged):
            state = state_stage(o, length, stg, state)
        for hh in hs:
            c_ref[hh] = state[0][hh]
            n_ref[hh] = state[1][hh]
            m_ref[hh] = state[2][hh]

    zeros8 = jnp.zeros((8, A_DK), F32)
    sweep([0], N_META,
          [[window_taps(jnp.concatenate([zeros8, q_ref[0, 0:N_META, kcol[hh]]], axis=0), N_META) for hh in hs]],
          [[window_taps(jnp.concatenate([zeros8, k_ref[0, 0:N_META, kcol[hh]]], axis=0), N_META) for hh in hs]])
    assert n_chunks % group == 0

    def body(c, carry):
        offsets = [pl.multiple_of(N_META + (c * group + j) * CHUNK, BF16_ROWS) for j in range(group)]

        starts = [pl.multiple_of(N_META - 8 + (c * group + j) * CHUNK, 8) for j in range(group)]

        def shifted(ref, w0, hh):
            return window_taps(ref[0, pl.ds(w0, CHUNK + 8), kcol[hh]], CHUNK)

        sweep(offsets, CHUNK,
              [[shifted(q_ref, w0, hh) for hh in hs] for w0 in starts],
              [[shifted(k_ref, w0, hh) for hh in hs] for w0 in starts])
        return carry

    lax.fori_loop(0, n_chunks // group, body, 0)


def _mlstm(z, gate_col, conv_w, conv_b, b_i, b_f, head_norm, heads=2):
    b, t, _ = z.shape
    hk = A_HEADS * A_DK
    wk, wv = heads * A_DK, heads * A_DV
    smem = pl.BlockSpec(memory_space=pltpu.SMEM)
    col = lambda width, off: (lambda bi, g: (bi, 0, off // width + g))
    return pl.pallas_call(
        functools.partial(_mlstm_kernel, heads=heads, group=4),
        grid=(b, A_HEADS // heads),
        in_specs=[
            smem, smem,
            pl.BlockSpec((1, t, wk), col(wk, 0)),
            pl.BlockSpec((1, t, wk), col(wk, hk)),
            pl.BlockSpec((1, t, wv), col(wv, 2 * hk)),
            pl.BlockSpec((1, t, wv), col(wv, 2 * hk + A_HEADS * A_DV)),
            pl.BlockSpec((1, t, MXU_DIM), lambda bi, g: (bi, 0, gate_col // MXU_DIM)),
            pl.BlockSpec((CONV_K, wk), lambda bi, g: (0, g)),
            pl.BlockSpec((CONV_K, wk), lambda bi, g: (0, A_HEADS // heads + g)),
            pl.BlockSpec((1, wk), lambda bi, g: (0, g)),
            pl.BlockSpec((1, wk), lambda bi, g: (0, A_HEADS // heads + g)),
            pl.BlockSpec((1, wv), lambda bi, g: (0, g)),
        ],
        out_specs=pl.BlockSpec((1, t, wv), lambda bi, g: (bi, 0, g)),
        out_shape=jax.ShapeDtypeStruct((b, t, A_HEADS * A_DV), BF16),
        scratch_shapes=[pltpu.VMEM((heads, A_DK, A_DV), F32), pltpu.VMEM((heads, 1, A_DK), F32),
                        pltpu.VMEM((heads, 1, 1), F32)],
        compiler_params=_params("parallel", "parallel"),
        name="mlstm",
    )(b_i, b_f, z, z, z, z, z, conv_w, conv_w, conv_b.reshape(1, -1), conv_b.reshape(1, -1),
      head_norm.reshape(1, -1))


def _gla_kernel(q_ref, k_ref, v_ref, og_ref, g_ref, p0_ref, p1_ref, p2_ref, hn_ref, o_ref, st_ref, *pre_ref,
                mode, heads, group):
    t_total = q_ref.shape[1]
    n_chunks = (t_total - N_META) // CHUNK
    dv, dk = st_ref.shape[1:]
    st_ref[...] = jnp.zeros_like(st_ref)

    hs = range(heads)
    kcol = [slice(hh * dk, (hh + 1) * dk) for hh in hs]
    vcol = [slice(hh * dv, (hh + 1) * dv) for hh in hs]

    if mode == "gla":
        gate = g_ref[0]
        gate_hi = _bf(gate)
        gate_lo = _bf(gate - gate_hi.astype(F32))
        for hh in hs:
            w = p0_ref[hh]
            w_hi = _bf(w)
            w_lo = _bf(w - w_hi.astype(F32))
            both = _dot(gate_hi, jnp.concatenate([w_hi, w_lo], axis=1))
            pre_ref[0][hh] = (both[:, :dk] + both[:, dk:]) + _dot(gate_lo, w_hi) + p1_ref[:, kcol[hh]]

    def gate_inputs(hh, o, length):
        q = q_ref[0, pl.ds(o, length), kcol[hh]]
        if mode == "gla":
            pre = pre_ref[0][hh, pl.ds(o, length), :]
            return q * (dk ** -0.5), k_ref[0, pl.ds(o, length), kcol[hh]], _log_sigmoid(pre) / GATE_TAU
        fpre = g_ref[0, pl.ds(o, length), kcol[hh]]
        a = p0_ref[:, kcol[hh]]
        bb = p1_ref[:, kcol[hh]] + _log_sigmoid(fpre)
        lg = jnp.maximum(a, bb) + jnp.log1p(jnp.exp(-jnp.abs(a - bb)))
        return q, p2_ref[:, kcol[hh]] * _sigmoid(-fpre), lg

    def cumsum_time(tri, lg):
        parts = _dot(tri, jnp.concatenate(_split3(lg), axis=1))
        return parts[:, :dk] + (parts[:, dk:2 * dk] + parts[:, 2 * dk:])

    def local_stage(o, length):
        row = lax.broadcasted_iota(jnp.int32, (length, length), 0)
        col = lax.broadcasted_iota(jnp.int32, (length, length), 1)
        causal = col <= row
        tri = jnp.where(causal, 1.0, 0.0).astype(BF16)
        qkl = [gate_inputs(hh, o, length) for hh in hs]
        vb = [_bf(v_ref[0, pl.ds(o, length), vcol[hh]]) for hh in hs]
        g = [cumsum_time(tri, qkl[hh][2]) for hh in hs]
        g_end = [g[hh][length - 1:length, :] for hh in hs]
        g_mid = [g[hh][length // 2:length // 2 + 1, :] for hh in hs]
        s = [_dot_nt(_bf(qkl[hh][0] * jnp.exp(g[hh] - g_mid[hh])), _bf(qkl[hh][1] * jnp.exp(g_mid[hh] - g[hh])))
             for hh in hs]
        q_dec = [_bf(qkl[hh][0] * jnp.exp(g[hh])) for hh in hs]
        local = [_dot_tn(vb[hh], _bf(qkl[hh][1] * jnp.exp(g_end[hh] - g[hh]))) for hh in hs]
        intra = [_dot(_bf(jnp.where(causal, s[hh], 0.0)), vb[hh]) for hh in hs]
        return q_dec, [jnp.exp(ge) for ge in g_end], local, intra

    def state_stage(o, length, staged, st_in):
        q_dec, decay, local, intra = staged
        inter = [_dot_nt(q_dec[hh], _bf(st_in[hh])) for hh in hs]
        st_out = [st_in[hh] * decay[hh] + local[hh] for hh in hs]
        for hh in hs:
            out = intra[hh] + inter[hh]
            hn = out * lax.rsqrt(jnp.mean(out * out, axis=-1, keepdims=True) + EPS) * hn_ref[:, vcol[hh]]
            og = og_ref[0, pl.ds(o, length), vcol[hh]]
            gate = _silu(og) if mode == "gla" else _sigmoid(og)
            o_ref[0, pl.ds(o, length), vcol[hh]] = (gate * hn).astype(o_ref.dtype)
        return st_out

    def sweep(offsets, length):
        staged = [local_stage(o, length) for o in offsets]
        st = [st_ref[hh] for hh in hs]
        for o, stg in zip(offsets, staged):
            st = state_stage(o, length, stg, st)
        for hh in hs:
            st_ref[hh] = st[hh]

    sweep([0], N_META)
    assert n_chunks % group == 0

    def body(c, carry):
        o = pl.multiple_of(N_META + c * (group * CHUNK), BF16_ROWS)
        sweep([pl.multiple_of(o + j * CHUNK, BF16_ROWS) for j in range(group)], CHUNK)
        return carry

    lax.fori_loop(0, n_chunks // group, body, 0)


def _gla_call(z, n_heads, heads, dk, dv, blocks, gate_width, params, head_norm, mode):
    b, t, _ = z.shape
    q0, k0, v0, og0, g0 = blocks
    zspec = lambda width, off, grouped=True: pl.BlockSpec(
        (1, t, width), (lambda bi, g: (bi, 0, off // width + (g if grouped else 0))))
    (p0, s0), (p1, s1), (p2, s2) = params
    gate_spec = zspec(gate_width, g0, grouped=False) if mode == "gla" else zspec(heads * dk, g0)
    return pl.pallas_call(
        functools.partial(_gla_kernel, mode=mode, heads=heads, group=8),
        grid=(b, n_heads // heads),
        in_specs=[
            zspec(heads * dk, q0), zspec(heads * dk, k0), zspec(heads * dv, v0), zspec(heads * dv, og0),
            gate_spec, s0, s1, s2,
            pl.BlockSpec((1, heads * dv), lambda bi, g: (0, g)),
        ],
        out_specs=pl.BlockSpec((1, t, heads * dv), lambda bi, g: (bi, 0, g)),
        out_shape=jax.ShapeDtypeStruct((b, t, n_heads * dv), BF16),
        scratch_shapes=[pltpu.VMEM((heads, dv, dk), F32)] + (
            [pltpu.VMEM((heads, t, dk), F32)] if mode == "gla" else []),
        compiler_params=_params("parallel", "parallel"),
        name="gla_" + mode,
    )(z, z, z, z, z, p0, p1, p2, head_norm.reshape(1, -1))


def _mla_kernel(cq_ref, ckv_ref, kr_ref, tab_ref, gqa_ref, gkva_ref, wqn_ref, wqr_ref, wkn_ref, wv_ref,
                gqn_ref, gqr_ref, gkn_ref, gkr_ref, o_ref, qf_ref, kf_ref, vf_ref, *, heads):
    t_total = cq_ref.shape[1]
    n_blocks = (ATT_PAD + t_total) // ATT_BLOCK
    dqk = D_NOPE + D_ROPE
    scale = dqk ** -0.5
    rows = _row_tile(t_total, PREP_ROWS)

    hs = range(heads)
    col = [slice(hh * LANES, (hh + 1) * LANES) for hh in hs]
    for hh in hs:
        qf_ref[hh, 0:ATT_PAD, :] = jnp.zeros((ATT_PAD, 2 * LANES), BF16)
        kf_ref[hh, 0:ATT_PAD, :] = jnp.zeros((ATT_PAD, 2 * LANES), BF16)
        vf_ref[hh, 0:ATT_PAD, :] = jnp.zeros((ATT_PAD, D_V), BF16)

    def rope_pair(x, gains, tab):
        p = x * gains * tab
        return p + pltpu.roll(p, D_ROPE, 1)

    def latent_norm(ref, gain_ref, r0):
        u = ref[0, pl.ds(r0, rows), :]
        return _bf(u * lax.rsqrt(jnp.mean(u * u, axis=-1, keepdims=True) + EPS) * gain_ref[...])

    def up_project(c):
        r0 = c * rows
        cq = latent_norm(cq_ref, gqa_ref, r0)
        ckv = latent_norm(ckv_ref, gkva_ref, r0)
        return _dot(cq, wqn_ref[...]), _dot(cq, wqr_ref[...]), _dot(ckv, wkn_ref[...]), _dot(ckv, wv_ref[...])

    def prep(c, hh, projected):
        r0 = c * rows
        dst = ATT_PAD + c * rows
        tab = tab_ref[pl.ds(r0, rows), :]
        qn = projected[0][:, col[hh]]
        qr = projected[1][:, col[hh]]
        ssq = jnp.sum(qn * qn + 0.5 * (qr * qr), axis=-1, keepdims=True)
        rq = lax.rsqrt(ssq / dqk + EPS) * scale
        qf_ref[hh, pl.ds(dst, rows), 0:LANES] = _bf(qn * gqn_ref[...] * rq)
        qf_ref[hh, pl.ds(dst, rows), LANES:2 * LANES] = _bf(qr * gqr_ref[...] * tab * rq)
        kn = projected[2][:, col[hh]]
        kr = kr_ref[0, pl.ds(r0, rows), :]
        ssk = jnp.sum(kn * kn + 0.5 * (kr * kr), axis=-1, keepdims=True)
        rk = lax.rsqrt(ssk / dqk + EPS)
        kf_ref[hh, pl.ds(dst, rows), 0:LANES] = _bf(kn * gkn_ref[...] * rk)
        kf_ref[hh, pl.ds(dst, rows), LANES:2 * LANES] = _bf(rope_pair(kr, gkr_ref[...], tab) * rk)
        vf_ref[hh, pl.ds(dst, rows), :] = _bf(projected[3][:, col[hh]])

    for c in range(t_total // rows):
        projected = up_project(c)
        for hh in hs:
            prep(c, hh, projected)

    qpos = lax.broadcasted_iota(jnp.int32, (ATT_BLOCK, ATT_BLOCK), 0)
    kpos = lax.broadcasted_iota(jnp.int32, (ATT_BLOCK, ATT_BLOCK), 1)
    neg = -jnp.inf

    def scores(item):
        qi, hh = item
        q = qf_ref[hh, qi * ATT_BLOCK:(qi + 1) * ATT_BLOCK, :]
        return _dot_nt(q, kf_ref[hh, 0:(qi + 1) * ATT_BLOCK, :])

    items = [(qi, hh) for qi in range(n_blocks) for hh in hs]
    s_next = scores(items[0])
    for idx, (qi, hh) in enumerate(items):
        s = s_next
        if idx + 1 < len(items):
            s_next = scores(items[idx + 1])
        parts = [s[:, j * ATT_BLOCK:(j + 1) * ATT_BLOCK] for j in range(qi + 1)]
        parts[0] = jnp.where(kpos >= ATT_PAD, parts[0], neg)
        parts[qi] = jnp.where(kpos <= qpos, parts[qi], neg)
        top = functools.reduce(jnp.maximum, parts)
        m = jnp.max(top, axis=-1, keepdims=True)
        if qi == 0:
            m = jnp.where(m == neg, 0.0, m)
        probs = [jnp.exp(part - m) for part in parts]
        l = jnp.sum(functools.reduce(jnp.add, probs), axis=-1, keepdims=True)
        pv = _dot(jnp.concatenate([_bf(pr) for pr in probs], axis=1), vf_ref[hh, 0:(qi + 1) * ATT_BLOCK, :])
        if qi == 0:
            out = pv / jnp.where(l == 0.0, 1.0, l)
            o_ref[0, 0:N_META, col[hh]] = out[ATT_PAD:, :].astype(o_ref.dtype)
        else:
            dst = qi * ATT_BLOCK - ATT_PAD
            o_ref[0, dst:dst + ATT_BLOCK, col[hh]] = (pv / l).astype(o_ref.dtype)


def _mla(z, cq_col, tab, q_a_norm, kv_a_norm, wq, wkv, gqn, gqr, gkn, gkr):
    b, t, _ = z.shape
    assert (ATT_PAD + t) % ATT_BLOCK == 0 and t % BF16_ROWS == 0
    tp = ATT_PAD + t
    heads = 2
    groups = D_HEADS // heads
    ckv_col = cq_col + Q_LORA
    kr_col = ckv_col + KV_LORA
    wspec = lambda k, off: pl.BlockSpec((k, heads * LANES), lambda bi, g: (0, off + g))
    gspec = lambda width: pl.BlockSpec((1, width), lambda bi, g: (0, 0))
    return pl.pallas_call(
        functools.partial(_mla_kernel, heads=heads),
        grid=(b, groups),
        in_specs=[
            pl.BlockSpec((1, t, Q_LORA), lambda bi, g: (bi, 0, cq_col // Q_LORA)),
            pl.BlockSpec((1, t, KV_LORA), lambda bi, g: (bi, 0, ckv_col // KV_LORA)),
            pl.BlockSpec((1, t, LANES), lambda bi, g: (bi, 0, kr_col // LANES)),
            pl.BlockSpec((t, LANES), lambda bi, g: (0, 0)),
            gspec(Q_LORA), gspec(KV_LORA),
            wspec(Q_LORA, 0), wspec(Q_LORA, groups), wspec(KV_LORA, 0), wspec(KV_LORA, groups),
            gspec(LANES), gspec(LANES), gspec(LANES), gspec(LANES),
        ],
        out_specs=pl.BlockSpec((1, t, heads * D_V), lambda bi, g: (bi, 0, g)),
        out_shape=jax.ShapeDtypeStruct((b, t, D_HEADS * D_V), BF16),
        scratch_shapes=[pltpu.VMEM((heads, tp, 2 * LANES), BF16), pltpu.VMEM((heads, tp, 2 * LANES), BF16),
                        pltpu.VMEM((heads, tp, D_V), BF16)],
        compiler_params=_params("parallel", "parallel"),
        name="mla",
    )(z, z, z, tab, q_a_norm.reshape(1, -1), kv_a_norm.reshape(1, -1), wq, wq, wkv, wkv, gqn, gqr, gkn, gkr)


def _pack_bf16_pairs(v):
    w = v.shape[1] // 2
    bits = pltpu.bitcast(_bf(v).astype(F32), jnp.uint32)
    return (bits[:, :w] >> 16) | (bits[:, w:] & jnp.uint32(0xFFFF0000))


def _unpack_lo(words):
    return pltpu.bitcast(words << 16, F32)


def _unpack_hi(words):
    return pltpu.bitcast(words & jnp.uint32(0xFFFF0000), F32)


def _router_kernel(x_ref, g_ref, w_ref, b_ref, gate_ref, idx_ref, xg_ref, cnt_ref, carry_ref):
    tm = x_ref.shape[0]

    @pl.when(pl.program_id(0) == 0)
    def _():
        carry_ref[...] = jnp.zeros_like(carry_ref)

    x = x_ref[...]
    ms = jnp.mean(x * x, axis=-1, keepdims=True)
    xn = x * lax.rsqrt(ms + EPS) * g_ref[...]
    xh = _bf(xn)
    xl = _bf(xn - xh.astype(F32))
    w = w_ref[...]
    wh = _bf(w)
    wl = _bf(w - wh.astype(F32))
    both = _dot(xh, jnp.concatenate([wh, wl], axis=1))
    logits = (both[:, :LANES] + both[:, LANES:]) + _dot(xl, wh) + b_ref[...]
    lane = lax.broadcasted_iota(jnp.int32, logits.shape, 1)
    lane_f = lane.astype(F32)
    neg = -jnp.inf
    big = float(LANES)

    is_group = lane < N_GROUPS
    g_max = jnp.max(jnp.where(is_group, logits, neg), axis=-1, keepdims=True)
    g_sum = jnp.sum(jnp.where(is_group, jnp.exp(logits - g_max), 0.0), axis=-1, keepdims=True)
    p_top = 1.0 / g_sum
    grp = jnp.min(jnp.where(is_group & (logits == g_max), lane_f, big), axis=-1, keepdims=True)

    e_lo = N_GROUPS + grp * EXPERTS_PER_GROUP
    in_grp = (lane_f >= e_lo) & (lane_f < e_lo + EXPERTS_PER_GROUP)
    e_max = jnp.max(jnp.where(in_grp, logits, neg), axis=-1, keepdims=True)
    e_sum = jnp.sum(jnp.where(in_grp, jnp.exp(logits - e_max), 0.0), axis=-1, keepdims=True)
    i1 = jnp.min(jnp.where(in_grp & (logits == e_max), lane_f, big), axis=-1, keepdims=True)
    rest = in_grp & (lane_f != i1)
    e_2nd = jnp.max(jnp.where(rest, logits, neg), axis=-1, keepdims=True)
    i2 = jnp.min(jnp.where(rest & (logits == e_2nd), lane_f, big), axis=-1, keepdims=True)
    p1 = 1.0 / e_sum
    p2 = jnp.exp(e_2nd - e_max) / e_sum
    tot = p1 + p2
    gate_ref[...] = jnp.where(lane == 0, p_top * p1 / tot, jnp.where(lane == 1, p_top * p2 / tot, 0.0))

    e1 = i1 - N_GROUPS
    e2 = i2 - N_GROUPS
    hot = jnp.where((lane_f == e1) | (lane_f == e2), 1.0, 0.0)
    row = lax.broadcasted_iota(jnp.int32, (tm, tm), 0)
    col = lax.broadcasted_iota(jnp.int32, (tm, tm), 1)
    before = _dot(jnp.where(col < row, 1.0, 0.0).astype(BF16), _bf(hot)) + carry_ref[...]
    r1 = jnp.sum(jnp.where(lane_f == e1, before, 0.0), axis=-1, keepdims=True)
    r2 = jnp.sum(jnp.where(lane_f == e2, before, 0.0), axis=-1, keepdims=True)
    total = carry_ref[...] + jnp.sum(hot, axis=0, keepdims=True)
    carry_ref[...] = total
    cnt_ref[...] = jnp.broadcast_to(total, cnt_ref.shape).astype(jnp.int32)
    idx_ref[...] = jnp.where(lane == 0, e1, jnp.where(lane == 1, e2, jnp.where(lane == 2, r1, jnp.where(
        lane == 3, r2, 0.0)))).astype(jnp.int32)

    words = _pack_bf16_pairs(xn)
    for s in range(8):
        xg_ref[pl.ds(s, tm, stride=8), :] = words[:, s * LANES:(s + 1) * LANES]


def _router(x2d, gain, w_group, b_group, w_expert, b_expert):
    m, d = x2d.shape
    assert d == 2 * 8 * LANES
    tm = _row_tile(m, ROW_TILE)
    pad = LANES - N_GROUPS - N_EXPERTS
    w = jnp.concatenate([w_group, w_expert, jnp.zeros((d, pad), F32)], axis=1)
    bias = jnp.concatenate([b_group, b_expert, jnp.zeros((pad,), F32)]).reshape(1, LANES)
    return pl.pallas_call(
        _router_kernel,
        grid=(m // tm,),
        in_specs=[
            pl.BlockSpec((tm, d), lambda i: (i, 0)),
            pl.BlockSpec((1, d), lambda i: (0, 0)),
            pl.BlockSpec((d, LANES), lambda i: (0, 0)),
            pl.BlockSpec((1, LANES), lambda i: (0, 0)),
        ],
        out_specs=[pl.BlockSpec((tm, LANES), lambda i: (i, 0)), pl.BlockSpec((tm, LANES), lambda i: (i, 0)),
                   pl.BlockSpec((tm * 8, LANES), lambda i: (i, 0)), pl.BlockSpec((8, LANES), lambda i: (0, 0))],
        out_shape=[jax.ShapeDtypeStruct((m, LANES), F32), jax.ShapeDtypeStruct((m, LANES), jnp.int32),
                   jax.ShapeDtypeStruct((m * 8, LANES), jnp.uint32), jax.ShapeDtypeStruct((8, LANES), jnp.int32)],
        scratch_shapes=[pltpu.VMEM((1, LANES), F32)],
        compiler_params=_params("arbitrary"),
        name="router",
    )(x2d, gain.reshape(1, d), w, bias)


def _invert_kernel(dest_ref, inv_ref):
    def clear(s, carry):
        inv_ref[s] = -1
        return carry

    lax.fori_loop(0, inv_ref.shape[0], clear, 0, unroll=8)

    def put(f, carry):
        inv_ref[dest_ref[f]] = f
        return carry

    lax.fori_loop(0, dest_ref.shape[0], put, 0, unroll=8)


def _invert(dest, p):
    assert p % 8 == 0 and dest.shape[0] % 8 == 0
    smem = pl.BlockSpec(memory_space=pltpu.SMEM)
    return pl.pallas_call(
        _invert_kernel, in_specs=[smem], out_specs=smem,
        out_shape=jax.ShapeDtypeStruct((p,), jnp.int32), name="moe_invert",
    )(dest)


def _expert_kernel(be_ref, nxt_ref, run_ref, nu_ref, src_ref, dst_ref, xg_hbm, w1_hbm, w3_hbm, w2_hbm, o_hbm,
                   xbuf, ybuf, w1f, w3f, w2f, w1s, w3s, w2s, sem_in, sem_out, sem_w, *, layer):
    i = pl.program_id(0)
    n_used = nu_ref[0]
    par = i % 2
    half_e = D_EXPERT // 2

    def weight_copies(expert, slot):
        top = D_EXPERT // 2
        return [
            pltpu.make_async_copy(w1_hbm.at[layer, expert], w1f.at[slot], sem_w.at[slot, 0]),
            pltpu.make_async_copy(w3_hbm.at[layer, expert], w3f.at[slot], sem_w.at[slot, 1]),
            pltpu.make_async_copy(w2_hbm.at[layer, expert, pl.ds(0, top), :], w2f.at[slot, pl.ds(0, top), :],
                                  sem_w.at[slot, 2]),
            pltpu.make_async_copy(w2_hbm.at[layer, expert, pl.ds(top, top), :], w2f.at[slot, pl.ds(top, top), :],
                                  sem_w.at[slot, 3]),
        ]

    def gather(block, r, slot):
        src = pl.multiple_of(src_ref[block * MOE_BLOCK + r], 8)
        return pltpu.make_async_copy(xg_hbm.at[pl.ds(src, 8), :], xbuf.at[slot, pl.ds(r * 8, 8), :], sem_in.at[slot])

    def scatter(block, r, slot):
        dst = pl.multiple_of(dst_ref[(block + 1) * MOE_BLOCK + r], 8)
        return pltpu.make_async_copy(ybuf.at[slot, pl.ds(r * 8, 8), :], o_hbm.at[pl.ds(dst, 8), :], sem_out.at[slot])

    def wait_gathers(slot):
        pltpu.make_async_copy(xbuf.at[1 - slot], xbuf.at[slot], sem_in.at[slot]).wait()

    def wait_scatters(slot):
        pltpu.make_async_copy(ybuf.at[slot], ybuf.at[1 - slot], sem_out.at[slot]).wait()

    @pl.when(i == 0)
    def _():
        ybuf[1] = jnp.zeros(ybuf.shape[1:], ybuf.dtype)
        n_real = o_hbm.shape[0] - 2 * MOE_BLOCK * 8
        fill1 = pltpu.make_async_copy(ybuf.at[1], o_hbm.at[pl.ds(n_real + MOE_BLOCK * 8, MOE_BLOCK * 8), :],
                                      sem_out.at[1])
        fill1.start()
        fill1.wait()
        pltpu.make_async_copy(ybuf.at[1], o_hbm.at[pl.ds(n_real, MOE_BLOCK * 8), :], sem_out.at[0]).start()

        def first(r, carry):
            gather(0, r, 0).start()
            return carry

        lax.fori_loop(0, MOE_BLOCK, first, 0, unroll=8)

    @pl.when(i < n_used)
    def _():
        expert = be_ref[i]
        wslot = run_ref[i] % 2

        @pl.when(i == 0)
        def _():
            for j, cp in enumerate(weight_copies(expert, 0)):
                cp.start(priority=j % 2)

        @pl.when((i == 0) | (expert != be_ref[jnp.maximum(i - 1, 0)]))
        def _():
            for cp in weight_copies(expert, wslot):
                cp.wait()

            @pl.when(nxt_ref[i] != expert)
            def _():
                for j, cp in enumerate(weight_copies(nxt_ref[i], 1 - wslot)):
                    cp.start(priority=j % 2)

            w1s[...] = _bf(w1f[wslot])
            w3s[...] = _bf(w3f[wslot])
            w2s[...] = _bf(w2f[wslot])

        wait_gathers(par)

        def move_rows(group, n_groups=6):
            lo, hi = group * MOE_BLOCK // n_groups, (group + 1) * MOE_BLOCK // n_groups
            for r in range(lo, hi):
                gather(i + 1, r, 1 - par).start(priority=r % 2)
                scatter(i - 1, r, 1 - par).start(priority=(r + 1) % 2)

        tiles = [xbuf[par, pl.ds(s, MOE_BLOCK, stride=8), :] for s in range(8)]
        xb = jnp.concatenate([_bf(_unpack_lo(w)) for w in tiles] + [_bf(_unpack_hi(w)) for w in tiles], axis=1)
        move_rows(0)
        h1a = _dot(xb, w1s[:, :half_e])
        move_rows(1)
        h3a = _dot(xb, w3s[:, :half_e])
        move_rows(2)
        act_a = _bf(_silu(h1a) * h3a)
        h1b = _dot(xb, w1s[:, half_e:])
        move_rows(3)
        h3b = _dot(xb, w3s[:, half_e:])
        move_rows(4)
        act_b = _bf(_silu(h1b) * h3b)
        y = _dot(act_a, w2s[:half_e, :])
        move_rows(5)
        y = y + _dot(act_b, w2s[half_e:, :])

        wait_scatters(par)
        words = _pack_bf16_pairs(y)
        for s in range(8):
            ybuf[par, pl.ds(s, MOE_BLOCK, stride=8), :] = words[:, s * LANES:(s + 1) * LANES]

        @pl.when(i == n_used - 1)
        def _():
            def last(r, carry):
                scatter(i, r, par).start()
                return carry

            lax.fori_loop(0, MOE_BLOCK, last, 0, unroll=8)
            wait_scatters(par)
            wait_scatters(1 - par)
            wait_gathers(1 - par)


def _moe(x2d, gain, w_group, b_group, w_expert, b_expert, w1, w3, w2, layer, final_shape=None):
    n, d = x2d.shape
    gates_l, idx_l, xg, cnt = _router(x2d, gain, w_group, b_group, w_expert, b_expert)

    a = n * TOP_K
    n_blocks = -(-a // MOE_BLOCK) + N_EXPERTS
    p = n_blocks * MOE_BLOCK
    counts = cnt[0, :N_EXPERTS]
    padded = (counts + MOE_BLOCK - 1) // MOE_BLOCK * MOE_BLOCK
    pad_end = jnp.cumsum(padded)
    pad_start = pad_end - padded
    e_hot = idx_l[:, :TOP_K, None] == jnp.arange(N_EXPERTS, dtype=jnp.int32)
    dest = jnp.sum(jnp.where(e_hot, pad_start, 0), axis=-1) + idx_l[:, TOP_K:2 * TOP_K]
    dest = jnp.clip(dest.reshape(-1), 0, p - 1).astype(jnp.int32)
    blk0 = jnp.arange(n_blocks, dtype=jnp.int32) * MOE_BLOCK
    block_expert = jnp.minimum(jnp.searchsorted(pad_end, blk0, side="right"), N_EXPERTS - 1).astype(jnp.int32)
    n_used = (pad_end[-1] // MOE_BLOCK).astype(jnp.int32).reshape(1)
    experts = jnp.arange(N_EXPERTS, dtype=jnp.int32)
    later = jnp.where((experts[None, :] > experts[:, None]) & (counts[None, :] > 0), experts[None, :], N_EXPERTS)
    next_owner = jnp.min(later, axis=1)
    next_expert = jnp.where(next_owner < N_EXPERTS, next_owner, experts)[block_expert].astype(jnp.int32)
    run_index = (jnp.cumsum(jnp.concatenate([jnp.ones((1,), jnp.int32),
                                             (block_expert[1:] != block_expert[:-1]).astype(jnp.int32)])) - 1
                 ).astype(jnp.int32)
    codes = _invert(dest, p)
    slot = jnp.arange(p, dtype=jnp.int32)
    spare = TOP_K * n + (slot // MOE_BLOCK % 2) * MOE_BLOCK + slot % MOE_BLOCK
    src_tok = (jnp.maximum(codes, 0) >> 1) * 8
    dst_row = jnp.where(codes >= 0, (codes & 1) * n + (codes >> 1), spare) * 8
    lead = (TOP_K * n + MOE_BLOCK + jnp.arange(MOE_BLOCK, dtype=jnp.int32)) * 8
    dst_row = jnp.concatenate([lead, dst_row])

    hbm = pl.BlockSpec(memory_space=pl.ANY)
    out_rows = TOP_K * n + 2 * MOE_BLOCK
    out2 = pl.pallas_call(
        functools.partial(_expert_kernel, layer=layer),
        grid_spec=pltpu.PrefetchScalarGridSpec(
            num_scalar_prefetch=6,
            grid=(n_blocks,),
            in_specs=[hbm, hbm, hbm, hbm],
            out_specs=hbm,
            scratch_shapes=[
                pltpu.VMEM((2, MOE_BLOCK * 8, LANES), jnp.uint32), pltpu.VMEM((2, MOE_BLOCK * 8, LANES), jnp.uint32),
                pltpu.VMEM((2, d, D_EXPERT), F32), pltpu.VMEM((2, d, D_EXPERT), F32), pltpu.VMEM((2, D_EXPERT, d), F32),
                pltpu.VMEM((d, D_EXPERT), BF16), pltpu.VMEM((d, D_EXPERT), BF16), pltpu.VMEM((D_EXPERT, d), BF16),
                pltpu.SemaphoreType.DMA((2,)), pltpu.SemaphoreType.DMA((2,)), pltpu.SemaphoreType.DMA((2, 4)),
            ],
        ),
        out_shape=jax.ShapeDtypeStruct((out_rows * 8, LANES), jnp.uint32),
        compiler_params=_params("arbitrary"),
        name="moe_experts",
    )(block_expert, next_expert, run_index, n_used, src_tok, dst_row, xg, w1, w3, w2)
    return _combine(x2d, gates_l, out2, final_shape)


def _combine_tile(x_ref, gate_ref, a_ref, b_ref, o_ref):
    tm = x_ref.shape[0]
    half = x_ref.shape[1] // 2
    g0 = gate_ref[:, 0:1]
    g1 = gate_ref[:, 1:2]
    for s in range(8):
        wa = a_ref[pl.ds(s, tm, stride=8), :]
        wb = b_ref[pl.ds(s, tm, stride=8), :]
        lo = slice(s * LANES, (s + 1) * LANES)
        hi = slice(half + s * LANES, half + (s + 1) * LANES)
        o_ref[:, lo] = x_ref[:, lo] + (g0 * _unpack_lo(wa) + g1 * _unpack_lo(wb))
        o_ref[:, hi] = x_ref[:, hi] + (g0 * _unpack_hi(wa) + g1 * _unpack_hi(wb))


def _combine_kernel(x_ref, gate_ref, a_ref, b_ref, o_ref):
    _combine_tile(x_ref, gate_ref, a_ref, b_ref, o_ref)


def _combine_final_kernel(x_ref, gate_ref, a_ref, b_ref, o_hbm, obuf, sem, *, tiles_per_batch):
    i = pl.program_id(0)
    n_steps = pl.num_programs(0)
    tm = x_ref.shape[0]

    def copies(step, slot):
        batch, j = step // tiles_per_batch, step % tiles_per_batch
        row = pl.multiple_of(j * tm, 8)
        main = pltpu.make_async_copy(obuf.at[slot, pl.ds(N_META, tm - N_META), :],
                                     o_hbm.at[batch, pl.ds(row, tm - N_META), :], sem.at[slot, 0])
        head = pltpu.make_async_copy(obuf.at[slot, pl.ds(0, N_META), :],
                                     o_hbm.at[batch, pl.ds(pl.multiple_of(jnp.maximum(row - N_META, 0), 8), N_META), :],
                                     sem.at[slot, 1])
        return main, head, j > 0

    def wait_step(step, slot):
        main, head, has_head = copies(step, slot)
        main.wait()

        @pl.when(has_head)
        def _():
            head.wait()

    slot = i % 2

    @pl.when(i >= 2)
    def _():
        wait_step(i - 2, slot)

    _combine_tile(x_ref, gate_ref, a_ref, b_ref, obuf.at[slot])
    main, head, has_head = copies(i, slot)
    main.start()

    @pl.when(has_head)
    def _():
        head.start()

    @pl.when(i == n_steps - 1)
    def _():
        wait_step(i, slot)

        @pl.when(i >= 1)
        def _():
            wait_step(i - 1, 1 - slot)


def _combine(x2d, gates, out2, final_shape=None):
    n, d = x2d.shape
    tm = _row_tile(n, ROW_TILE)
    in_specs = [
        pl.BlockSpec((tm, d), lambda i: (i, 0)),
        pl.BlockSpec((tm, LANES), lambda i: (i, 0)),
        pl.BlockSpec((tm * 8, LANES), lambda i: (i, 0)),
        pl.BlockSpec((tm * 8, LANES), lambda i: (n // tm + i, 0)),
    ]
    if final_shape is None:
        return pl.pallas_call(
            _combine_kernel,
            grid=(n // tm,),
            in_specs=in_specs,
            out_specs=pl.BlockSpec((tm, d), lambda i: (i, 0)),
            out_shape=jax.ShapeDtypeStruct((n, d), F32),
            compiler_params=_params("parallel"),
            name="moe_combine",
        )(x2d, gates, out2, out2)
    b, t = final_shape
    assert t % tm == 0 and tm > N_META
    return pl.pallas_call(
        functools.partial(_combine_final_kernel, tiles_per_batch=t // tm),
        grid=(n // tm,),
        in_specs=in_specs,
        out_specs=pl.BlockSpec(memory_space=pl.ANY),
        out_shape=jax.ShapeDtypeStruct((b, t - N_META, d), F32),
        scratch_shapes=[pltpu.VMEM((2, tm, d), F32), pltpu.SemaphoreType.DMA((2, 2))],
        compiler_params=_params("arbitrary"),
        name="moe_combine_final",
    )(x2d, gates, out2, out2)


def _even_layer(x, norm_g, w_in, conv_w, conv_b, b_i, b_f, a_norm, w_gate2, b_gate, b_norm, w_out):
    b, t, d = x.shape
    n = b * t
    a_w = 2 * A_HEADS * A_DK + 2 * A_HEADS * A_DV
    g_w = 2 * A_HEADS
    b_w = 2 * B_HEADS * B_DK + 2 * B_HEADS * B_DV
    main = a_w + b_w
    w = _even_weight(w_in, a_w, g_w, b_w, GATE_RANK)
    z = _normproj(x.reshape(n, d), norm_g, w).reshape(b, t, main + MXU_DIM)

    ya = _mlstm(z, main, conv_w, conv_b, b_i, b_f, a_norm)

    wg = jnp.zeros((B_HEADS, MXU_DIM, B_DK), F32).at[:, g_w:g_w + GATE_RANK, :].set(
        w_gate2.reshape(GATE_RANK, B_HEADS, B_DK).transpose(1, 0, 2))
    dummy = jnp.zeros((1, B_HEADS * B_DK), F32)
    hp = 2
    hspec = pl.BlockSpec((1, hp * B_DK), lambda bi, g: (0, g))
    yb = _gla_call(
        z, B_HEADS, hp, B_DK, B_DV,
        (a_w, a_w + B_HEADS * B_DK, a_w + 2 * B_HEADS * B_DK, a_w + 2 * B_HEADS * B_DK + B_HEADS * B_DV, main),
        MXU_DIM,
        ((wg, pl.BlockSpec((hp, MXU_DIM, B_DK), lambda bi, g: (g, 0, 0))),
         (b_gate.reshape(1, -1), hspec), (dummy, hspec)),
        b_norm, "gla")
    return _outproj(ya.reshape(n, -1), yb.reshape(n, -1), w_out.astype(BF16), x.reshape(n, d)).reshape(b, t, d)


def _odd_layer(x, lb, norm_g, w_in, c_norm, q_a_norm, w_q_up, kv_a_norm, w_kv_up, q_norm, k_norm, w_out):
    b, t, d = x.shape
    n = b * t
    c_w = 2 * C_HEADS * C_DK + 2 * C_HEADS * C_DV
    swap = (jnp.arange(D_ROPE) + D_ROPE // 2) % D_ROPE
    kr0 = c_w + Q_LORA + KV_LORA
    used = kr0 + 2 * D_ROPE
    total = -(-used // MXU_DIM) * MXU_DIM
    z2 = _normproj(x.reshape(n, d), norm_g, _odd_weight(w_in, kr0))
    z = z2.reshape(b, t, total)

    hp = 4
    hspec = pl.BlockSpec((1, hp * C_DK), lambda bi, g: (0, g))
    yc = _gla_call(
        z, C_HEADS, hp, C_DK, C_DV,
        (0, C_HEADS * C_DK, 2 * C_HEADS * C_DK, 2 * C_HEADS * C_DK + C_HEADS * C_DV, C_HEADS * C_DK),
        C_DK,
        ((jnp.log(lb).reshape(1, -1), hspec), (jnp.log1p(-lb).reshape(1, -1), hspec), ((1.0 - lb).reshape(1, -1), hspec)),
        c_norm, "hgrn")

    dq = D_NOPE + D_ROPE
    wq = w_q_up.reshape(Q_LORA, D_HEADS, dq)
    wq_rope = wq[:, :, D_NOPE:]
    wq_p = jnp.concatenate([wq[:, :, :D_NOPE].reshape(Q_LORA, -1),
                            jnp.concatenate([wq_rope, wq_rope[:, :, swap]], axis=-1).reshape(Q_LORA, -1)],
                           axis=1).astype(BF16)
    wkv = w_kv_up.reshape(KV_LORA, D_HEADS, D_NOPE + D_V)
    wkv_p = jnp.concatenate([wkv[:, :, :D_NOPE].reshape(KV_LORA, -1), wkv[:, :, D_NOPE:].reshape(KV_LORA, -1)],
                            axis=1).astype(BF16)
    pos = jnp.arange(t, dtype=F32)
    half = D_ROPE // 2
    inv = ROPE_THETA ** (-jnp.arange(half, dtype=F32) / half)
    ang = pos[:, None] * inv[None, :]
    cos, sin = jnp.cos(ang), jnp.sin(ang)
    tab = jnp.concatenate([cos, cos, -sin, sin], axis=1)
    pair = lambda g: jnp.concatenate([g[D_NOPE:], g[D_NOPE:][swap]]).reshape(1, LANES)
    yd = _mla(z, c_w, tab, q_a_norm, kv_a_norm, wq_p, wkv_p, q_norm[:D_NOPE].reshape(1, LANES), pair(q_norm),
              k_norm[:D_NOPE].reshape(1, LANES), pair(k_norm))
    return _outproj(yc.reshape(n, -1), yd.reshape(n, -1), w_out.astype(BF16), x.reshape(n, d)).reshape(b, t, d)


def kernel(x, meta_tokens, ab_norm, ab_w_in, a_conv_w, a_conv_b, a_b_i, a_b_f, a_head_norm, b_w_gate2, b_b_gate, b_head_norm, ab_w_out, cd_norm, cd_w_in, c_lower_bound, c_head_norm, d_q_a_norm, d_w_q_up, d_kv_a_norm, d_w_kv_up, d_q_norm, d_k_norm, cd_w_out, moe_norm, moe_w_group, moe_b_group, moe_w_expert, moe_b_expert, moe_w1, moe_w3, moe_w2):
    b = x.shape[0]
    depth = moe_norm.shape[0]
    h = jnp.concatenate([jnp.broadcast_to(meta_tokens.astype(x.dtype)[None], (b, N_META, D_MODEL)), x], axis=1)
    t = h.shape[1]
    lb_cum = jnp.cumsum(jax.nn.softmax(c_lower_bound.astype(F32), axis=0), axis=0)
    lower_bounds = lb_cum - lb_cum[0]
    for layer in range(depth):
        j = layer // 2
        if layer % 2 == 0:
            h = _even_layer(h, ab_norm[j], ab_w_in[j], a_conv_w[j], a_conv_b[j], a_b_i[j], a_b_f[j], a_head_norm[j],
                            b_w_gate2[j], b_b_gate[j], b_head_norm[j], ab_w_out[j])
        else:
            h = _odd_layer(h, lower_bounds[layer], cd_norm[j], cd_w_in[j], c_head_norm[j], d_q_a_norm[j],
                           d_w_q_up[j], d_kv_a_norm[j], d_w_kv_up[j], d_q_norm[j], d_k_norm[j], cd_w_out[j])
        last = layer == depth - 1
        h = _moe(h.reshape(b * t, D_MODEL), moe_norm[layer], moe_w_group[layer], moe_b_group[layer],
                 moe_w_expert[layer], moe_b_expert[layer], moe_w1, moe_w3, moe_w2, layer,
                 final_shape=(b, t) if last else None)
        if not last:
            h = h.reshape(b, t, D_MODEL)
    return h
```

```python
import functools

import jax
import jax.numpy as jnp
from jax import lax
from jax.experimental import pallas as pl
from jax.experimental.pallas import tpu as pltpu

F32 = jnp.float32
BF16 = jnp.bfloat16

D_MODEL = 2048
N_META = 16
CHUNK = 64
CONV_K = 4
EPS = 1e-6
A_HEADS, A_DK, A_DV = 4, 128, 256
B_HEADS, B_DK, B_DV = 4, 128, 256
GATE_RANK = 16
GATE_TAU = 16.0
C_HEADS, C_DK, C_DV = 8, 128, 128
D_HEADS, D_NOPE, D_ROPE, D_V = 8, 128, 64, 128
Q_LORA, KV_LORA = 512, 256
ROPE_THETA = 10000.0
N_GROUPS, EXPERTS_PER_GROUP = 4, 8
N_EXPERTS = N_GROUPS * EXPERTS_PER_GROUP
TOP_K = 2
D_EXPERT = 512

LANES = 128
MXU_DIM = 256
BF16_ROWS = 16
VMEM_LIMIT = 56 * 1024 * 1024
MOE_BLOCK = MXU_DIM
ROW_TILE = 688
ROW_TILE_BF16 = ROW_TILE
COL_TILE = 1280
COL_TILE_OUT = D_MODEL
PREP_ROWS = 768
ATT_BLOCK = 256
ATT_PAD = ATT_BLOCK - N_META

_NT = (((1,), (1,)), ((), ()))
_TN = (((0,), (0,)), ((), ()))


def _dot(a, b):
    return jnp.dot(a, b, preferred_element_type=F32)


def _dot_nt(a, b):
    return lax.dot_general(a, b, _NT, preferred_element_type=F32)


def _dot_tn(a, b):
    return lax.dot_general(a, b, _TN, preferred_element_type=F32)


def _bf(x):
    return x.astype(BF16)


def _split3(x):
    hi = _bf(x)
    rest = x - hi.astype(F32)
    mid = _bf(rest)
    return hi, mid, _bf(rest - mid.astype(F32))


def _log_sigmoid(x):
    return jnp.minimum(x, 0.0) - jnp.log1p(jnp.exp(-jnp.abs(x)))


def _sigmoid(x):
    return 1.0 / (1.0 + jnp.exp(-x))


def _silu(x):
    return x * _sigmoid(x)


def _row_tile(m, cap):
    best = None
    for t in range(BF16_ROWS, min(m, cap) + 1, BF16_ROWS):
        if m % t == 0:
            best = t
    assert best is not None, m
    return best


def _col_tile(n, cap):
    best = None
    for t in range(MXU_DIM, min(n, cap) + 1, MXU_DIM):
        if n % t == 0:
            best = t
    assert best is not None, n
    return best


def _params(*sem):
    return pltpu.CompilerParams(dimension_semantics=sem, vmem_limit_bytes=VMEM_LIMIT)


_RELAYOUT_ROWS = 256


def _even_weight_kernel(wa_ref, wb_ref, wc_ref, o_ref, *, n_plain, n_shift, shift, gate_cols):
    ob = pl.program_id(0)
    rows = o_ref.shape[0]
    chunks = rows // _RELAYOUT_ROWS

    @pl.when(ob < n_plain)
    def _():
        def body(c, carry):
            r = pl.ds(pl.multiple_of(c * _RELAYOUT_ROWS, _RELAYOUT_ROWS), _RELAYOUT_ROWS)
            o_ref[r, :] = _bf(wa_ref[r, :])
            return carry

        lax.fori_loop(0, chunks, body, 0)

    @pl.when((ob >= n_plain) & (ob < n_plain + n_shift))
    def _():
        def body(c, carry):
            r = pl.ds(pl.multiple_of(c * _RELAYOUT_ROWS, _RELAYOUT_ROWS), _RELAYOUT_ROWS)
            wide = jnp.concatenate([wa_ref[r, :], wb_ref[r, :]], axis=1)
            o_ref[r, :] = _bf(wide[:, shift:shift + MXU_DIM])
            return carry

        lax.fori_loop(0, chunks, body, 0)

    @pl.when(ob == n_plain + n_shift)
    def _():
        lane = lax.broadcasted_iota(jnp.int32, (_RELAYOUT_ROWS, LANES), 1)

        def body(c, carry):
            r = pl.ds(pl.multiple_of(c * _RELAYOUT_ROWS, _RELAYOUT_ROWS), _RELAYOUT_ROWS)
            first = jnp.where(lane < shift, wc_ref[r, :], jnp.where(lane < gate_cols, wb_ref[r, :], 0.0))
            o_ref[r, :] = _bf(jnp.concatenate([first, jnp.zeros_like(first)], axis=1))
            return carry

        lax.fori_loop(0, chunks, body, 0)


def _even_weight(w_in, a_w, g_w, b_w, rank):
    d = w_in.shape[0]
    assert a_w % MXU_DIM == 0 and b_w % MXU_DIM == 0 and g_w + rank <= LANES and d % _RELAYOUT_ROWS == 0
    n_plain, n_shift = a_w // MXU_DIM, b_w // MXU_DIM
    n_out = n_plain + n_shift + 1
    last = n_out - 1

    def b_index(ob):
        return (0, jnp.where(ob < n_plain, 0, jnp.where(ob < last, 2 * (ob + 1), (a_w + g_w + b_w) // LANES)))

    return pl.pallas_call(
        functools.partial(_even_weight_kernel, n_plain=n_plain, n_shift=n_shift, shift=g_w, gate_cols=g_w + rank),
        grid=(n_out,),
        in_specs=[
            pl.BlockSpec((d, MXU_DIM), lambda ob: (0, jnp.minimum(ob, last - 1))),
            pl.BlockSpec((d, LANES), b_index),
            pl.BlockSpec((d, LANES), lambda ob: (0, a_w // LANES)),
        ],
        out_specs=pl.BlockSpec((d, MXU_DIM), lambda ob: (0, ob)),
        out_shape=jax.ShapeDtypeStruct((d, n_out * MXU_DIM), BF16),
        compiler_params=_params("parallel"),
        name="even_weight",
    )(w_in, w_in, w_in)


def _odd_weight_kernel(w_ref, o_ref, *, n_plain):
    ob = pl.program_id(0)
    chunks = o_ref.shape[0] // _RELAYOUT_ROWS
    half = D_ROPE // 2

    def body(c, carry):
        r = pl.ds(pl.multiple_of(c * _RELAYOUT_ROWS, _RELAYOUT_ROWS), _RELAYOUT_ROWS)
        w = w_ref[r, :]

        @pl.when(ob < n_plain)
        def _():
            o_ref[r, :] = _bf(w)

        @pl.when(ob == n_plain)
        def _():
            pair = jnp.concatenate([w[:, :D_ROPE], w[:, half:D_ROPE], w[:, :half]], axis=1)
            o_ref[r, :] = _bf(jnp.concatenate([pair, jnp.zeros_like(pair)], axis=1))

        return carry

    lax.fori_loop(0, chunks, body, 0)


def _odd_weight(w_in, kr0):
    d = w_in.shape[0]
    assert kr0 % MXU_DIM == 0 and w_in.shape[1] == kr0 + D_ROPE and d % _RELAYOUT_ROWS == 0
    n_plain = kr0 // MXU_DIM
    return pl.pallas_call(
        functools.partial(_odd_weight_kernel, n_plain=n_plain),
        grid=(n_plain + 1,),
        in_specs=[pl.BlockSpec((d, MXU_DIM), lambda ob: (0, ob))],
        out_specs=pl.BlockSpec((d, MXU_DIM), lambda ob: (0, ob)),
        out_shape=jax.ShapeDtypeStruct((d, kr0 + MXU_DIM), BF16),
        compiler_params=_params("parallel"),
        name="odd_weight",
    )(w_in)


def _normproj_kernel(x_ref, g_ref, w_ref, o_ref, xs_ref):
    tm = xs_ref.shape[0]

    @pl.when(pl.program_id(1) == 0)
    def _():
        def body(c, carry):
            r0 = pl.multiple_of(c * BF16_ROWS, BF16_ROWS)
            x = x_ref[pl.ds(r0, BF16_ROWS), :]
            ms = jnp.mean(x * x, axis=-1, keepdims=True)
            xs_ref[pl.ds(r0, BF16_ROWS), :] = _bf(x * lax.rsqrt(ms + EPS) * g_ref[...])
            return carry

        lax.fori_loop(0, tm // BF16_ROWS, body, 0, unroll=8)

    o_ref[...] = _dot(xs_ref[...], w_ref[...]).astype(o_ref.dtype)


def _normproj(x2d, gain, w):
    m, k = x2d.shape
    n = w.shape[1]
    tm = _row_tile(m, ROW_TILE)
    tn = _col_tile(n, COL_TILE)
    return pl.pallas_call(
        _normproj_kernel,
        grid=(m // tm, n // tn),
        in_specs=[
            pl.BlockSpec((tm, k), lambda i, j: (i, 0)),
            pl.BlockSpec((1, k), lambda i, j: (0, 0)),
            pl.BlockSpec((k, tn), lambda i, j: (0, j)),
        ],
        out_specs=pl.BlockSpec((tm, tn), lambda i, j: (i, j)),
        out_shape=jax.ShapeDtypeStruct((m, n), F32),
        scratch_shapes=[pltpu.VMEM((tm, k), BF16)],
        compiler_params=_params("parallel", "arbitrary"),
        name="normproj",
    )(x2d, gain.reshape(1, k).astype(F32), w)


def _outproj_kernel(ya_ref, yb_ref, w_ref, r_ref, o_ref):
    ka = ya_ref.shape[1]
    acc = _dot(ya_ref[...], w_ref[:ka, :]) + _dot(yb_ref[...], w_ref[ka:, :])
    o_ref[...] = r_ref[...] + acc


def _outproj(ya, yb, w, res):
    m, ka = ya.shape
    kb = yb.shape[1]
    n = w.shape[1]
    tm = _row_tile(m, ROW_TILE_BF16)
    tn = _col_tile(n, COL_TILE_OUT)
    return pl.pallas_call(
        _outproj_kernel,
        grid=(m // tm, n // tn),
        in_specs=[
            pl.BlockSpec((tm, ka), lambda i, j: (i, 0)),
            pl.BlockSpec((tm, kb), lambda i, j: (i, 0)),
            pl.BlockSpec((ka + kb, tn), lambda i, j: (0, j)),
            pl.BlockSpec((tm, tn), lambda i, j: (i, j)),
        ],
        out_specs=pl.BlockSpec((tm, tn), lambda i, j: (i, j)),
        out_shape=jax.ShapeDtypeStruct((m, n), F32),
        compiler_params=_params("parallel", "arbitrary"),
        name="outproj",
    )(ya, yb, w, res)


def _mlstm_kernel(bi_ref, bf_ref, q_ref, k_ref, v_ref, og_ref, gt_ref,
                  cwq_ref, cwk_ref, cbq_ref, cbk_ref, hn_ref, o_ref, c_ref, n_ref, m_ref, *, heads, group):
    t_total = q_ref.shape[1]
    n_chunks = (t_total - N_META) // CHUNK
    head0 = pl.program_id(1) * heads

    c_ref[...] = jnp.zeros_like(c_ref)
    n_ref[...] = jnp.zeros_like(n_ref)
    m_ref[...] = jnp.zeros_like(m_ref)

    def conv(taps, cw, cb, length):
        y = cb
        for j in range(CONV_K):
            y = y + taps[j] * cw[j:j + 1, :]
        return _silu(y)

    def window_taps(win, length):
        return [win[8 - (CONV_K - 1) + j:8 - (CONV_K - 1) + j + length, :] for j in range(CONV_K)]

    hs = range(heads)
    kcol = [slice(hh * A_DK, (hh + 1) * A_DK) for hh in hs]
    vcol = [slice(hh * A_DV, (hh + 1) * A_DV) for hh in hs]

    def gates(hh, blk, blk_parts, length, causal, upper):
        head = head0 + hh
        b_i = bi_ref[head]
        b_f = bf_ref[head]
        lane = lax.broadcasted_iota(jnp.int32, (length, LANES), 1)
        ig_c = jnp.sum(jnp.where(lane == head, blk, 0.0), axis=1, keepdims=True) + b_i
        lf_c = _log_sigmoid(jnp.sum(jnp.where(lane == A_HEADS + head, blk, 0.0), axis=1, keepdims=True) + b_f)
        sel_r = lax.broadcasted_iota(jnp.int32, (8, LANES), 0)
        sel_l = lax.broadcasted_iota(jnp.int32, (8, LANES), 1)
        sel = jnp.where(sel_l == head + A_HEADS * sel_r, 1.0, 0.0).astype(BF16)
        rows = _dot_nt(sel, blk_parts[0]) + (_dot_nt(sel, blk_parts[1]) + _dot_nt(sel, blk_parts[2]))
        ig_r = rows[0:1, :] + b_i
        lf_r = _log_sigmoid(rows[1:2, :] + b_f)
        b_c = jnp.sum(jnp.where(causal, lf_r, 0.0), axis=1, keepdims=True)
        b_r = jnp.sum(jnp.where(upper, lf_c, 0.0), axis=0, keepdims=True)
        b_end = b_c[length - 1:length, :]
        w_end = b_end - b_c + ig_c
        m_loc = jnp.max(w_end, axis=0, keepdims=True)
        d = jnp.where(causal, b_c - b_r + ig_r, -jnp.inf)
        return b_c, b_end, jnp.exp(w_end - m_loc), m_loc, d, jnp.max(d, axis=1, keepdims=True)

    def local_stage(o, length, qwin, kwin):
        row = lax.broadcasted_iota(jnp.int32, (length, length), 0)
        col = lax.broadcasted_iota(jnp.int32, (length, length), 1)
        causal = col <= row
        blk = gt_ref[0, pl.ds(o, length), 0:LANES]
        blk_parts = _split3(blk)
        gt = [gates(hh, blk, blk_parts, length, causal, row <= col) for hh in hs]
        q = [conv(qwin[hh], cwq_ref[:, kcol[hh]], cbq_ref[:, kcol[hh]], length) for hh in hs]
        k = [conv(kwin[hh], cwk_ref[:, kcol[hh]], cbk_ref[:, kcol[hh]], length) * (A_DK ** -0.5) for hh in hs]
        vb = [_bf(v_ref[0, pl.ds(o, length), vcol[hh]]) for hh in hs]
        qb = [_bf(q[hh]) for hh in hs]
        k_w = [k[hh] * gt[hh][2] for hh in hs]
        qk = [_dot_nt(qb[hh], _bf(k[hh])) for hh in hs]
        c_loc = [_dot_tn(_bf(k_w[hh]), vb[hh]) for hh in hs]
        n_loc = [jnp.sum(k_w[hh], axis=0, keepdims=True) for hh in hs]
        return gt, q, qb, vb, qk, c_loc, n_loc

    def state_stage(o, length, staged, state):
        gt, q, qb, vb, qk, c_loc, n_loc = staged
        c_in, n_in, m_in = state
        q_c = [_dot(qb[hh], _bf(c_in[hh])) for hh in hs]
        s, a_t, m_t, q_n, c_out, n_out, m_out = [], [], [], [], [], [], []
        for hh in hs:
            b_c, b_end, _, m_loc, d, d_max = gt[hh]
            inter = b_c + m_in[hh]
            m_t.append(jnp.maximum(inter, d_max))
            s.append(qk[hh] * jnp.exp(d - m_t[hh]))
            a_t.append(jnp.exp(inter - m_t[hh]))
            m_new = jnp.maximum(b_end + m_in[hh], m_loc)
            a = jnp.exp(b_end + m_in[hh] - m_new)
            c = jnp.exp(m_loc - m_new)
            c_out.append(a * c_in[hh] + c * c_loc[hh])
            n_out.append(a * n_in[hh] + c * n_loc[hh])
            m_out.append(m_new)
            q_n.append(jnp.sum(q[hh] * n_in[hh], axis=1, keepdims=True))
        num = [_dot(_bf(s[hh]), vb[hh]) + a_t[hh] * q_c[hh] for hh in hs]
        for hh in hs:
            den = jnp.sum(s[hh], axis=1, keepdims=True) + a_t[hh] * q_n[hh]
            h = num[hh] / jnp.maximum(jnp.abs(den), jnp.exp(-m_t[hh]))
            hn = h * lax.rsqrt(jnp.mean(h * h, axis=-1, keepdims=True) + EPS) * hn_ref[:, vcol[hh]]
            y = _sigmoid(og_ref[0, pl.ds(o, length), vcol[hh]]) * hn
            o_ref[0, pl.ds(o, length), vcol[hh]] = y.astype(o_ref.dtype)
        return c_out, n_out, m_out

    def sweep(offsets, length, qwins, kwins):
        staged = [local_stage(o, length, qw, kw) for o, qw, kw in zip(offsets, qwins, kwins)]
        state = ([c_ref[hh] for hh in hs], [n_ref[hh] for hh in hs], [m_ref[hh] for hh in hs])
        for o, stg in zip(offsets, staged):
            state = state_stage(o, length, stg, state)
        for hh in hs:
            c_ref[hh] = state[0][hh]
            n_ref[hh] = state[1][hh]
            m_ref[hh] = state[2][hh]

    zeros8 = jnp.zeros((8, A_DK), F32)
    sweep([0], N_META,
          [[window_taps(jnp.concatenate([zeros8, q_ref[0, 0:N_META, kcol[hh]]], axis=0), N_META) for hh in hs]],
          [[window_taps(jnp.concatenate([zeros8, k_ref[0, 0:N_META, kcol[hh]]], axis=0), N_META) for hh in hs]])
    assert n_chunks % group == 0

    def body(c, carry):
        offsets = [pl.multiple_of(N_META + (c * group + j) * CHUNK, BF16_ROWS) for j in range(group)]

        starts = [pl.multiple_of(N_META - 8 + (c * group + j) * CHUNK, 8) for j in range(group)]

        def shifted(ref, w0, hh):
            return window_taps(ref[0, pl.ds(w0, CHUNK + 8), kcol[hh]], CHUNK)

        sweep(offsets, CHUNK,
              [[shifted(q_ref, w0, hh) for hh in hs] for w0 in starts],
              [[shifted(k_ref, w0, hh) for hh in hs] for w0 in starts])
        return carry

    lax.fori_loop(0, n_chunks // group, body, 0)


def _mlstm(z, gate_col, conv_w, conv_b, b_i, b_f, head_norm, heads=2):
    b, t, _ = z.shape
    hk = A_HEADS * A_DK
    wk, wv = heads * A_DK, heads * A_DV
    smem = pl.BlockSpec(memory_space=pltpu.SMEM)
    col = lambda width, off: (lambda bi, g: (bi, 0, off // width + g))
    return pl.pallas_call(
        functools.partial(_mlstm_kernel, heads=heads, group=4),
        grid=(b, A_HEADS // heads),
        in_specs=[
            smem, smem,
            pl.BlockSpec((1, t, wk), col(wk, 0)),
            pl.BlockSpec((1, t, wk), col(wk, hk)),
            pl.BlockSpec((1, t, wv), col(wv, 2 * hk)),
            pl.BlockSpec((1, t, wv), col(wv, 2 * hk + A_HEADS * A_DV)),
            pl.BlockSpec((1, t, MXU_DIM), lambda bi, g: (bi, 0, gate_col // MXU_DIM)),
            pl.BlockSpec((CONV_K, wk), lambda bi, g: (0, g)),
            pl.BlockSpec((CONV_K, wk), lambda bi, g: (0, A_HEADS // heads + g)),
            pl.BlockSpec((1, wk), lambda bi, g: (0, g)),
            pl.BlockSpec((1, wk), lambda bi, g: (0, A_HEADS // heads + g)),
            pl.BlockSpec((1, wv), lambda bi, g: (0, g)),
        ],
        out_specs=pl.BlockSpec((1, t, wv), lambda bi, g: (bi, 0, g)),
        out_shape=jax.ShapeDtypeStruct((b, t, A_HEADS * A_DV), BF16),
        scratch_shapes=[pltpu.VMEM((heads, A_DK, A_DV), F32), pltpu.VMEM((heads, 1, A_DK), F32),
                        pltpu.VMEM((heads, 1, 1), F32)],
        compiler_params=_params("parallel", "parallel"),
        name="mlstm",
    )(b_i, b_f, z, z, z, z, z, conv_w, conv_w, conv_b.reshape(1, -1), conv_b.reshape(1, -1),
      head_norm.reshape(1, -1))


def _gla_kernel(q_ref, k_ref, v_ref, og_ref, g_ref, p0_ref, p1_ref, p2_ref, hn_ref, o_ref, st_ref, *pre_ref,
                mode, heads, group):
    t_total = q_ref.shape[1]
    n_chunks = (t_total - N_META) // CHUNK
    dv, dk = st_ref.shape[1:]
    st_ref[...] = jnp.zeros_like(st_ref)

    hs = range(heads)
    kcol = [slice(hh * dk, (hh + 1) * dk) for hh in hs]
    vcol = [slice(hh * dv, (hh + 1) * dv) for hh in hs]

    if mode == "gla":
        gate = g_ref[0]
        gate_hi = _bf(gate)
        gate_lo = _bf(gate - gate_hi.astype(F32))
        for hh in hs:
            w = p0_ref[hh]
            w_hi = _bf(w)
            w_lo = _bf(w - w_hi.astype(F32))
            both = _dot(gate_hi, jnp.concatenate([w_hi, w_lo], axis=1))
            pre_ref[0][hh] = (both[:, :dk] + both[:, dk:]) + _dot(gate_lo, w_hi) + p1_ref[:, kcol[hh]]

    def gate_inputs(hh, o, length):
        q = q_ref[0, pl.ds(o, length), kcol[hh]]
        if mode == "gla":
            pre = pre_ref[0][hh, pl.ds(o, length), :]
            return q * (dk ** -0.5), k_ref[0, pl.ds(o, length), kcol[hh]], _log_sigmoid(pre) / GATE_TAU
        fpre = g_ref[0, pl.ds(o, length), kcol[hh]]
        a = p0_ref[:, kcol[hh]]
        bb = p1_ref[:, kcol[hh]] + _log_sigmoid(fpre)
        lg = jnp.maximum(a, bb) + jnp.log1p(jnp.exp(-jnp.abs(a - bb)))
        return q, p2_ref[:, kcol[hh]] * _sigmoid(-fpre), lg

    def cumsum_time(tri, lg):
        parts = _dot(tri, jnp.concatenate(_split3(lg), axis=1))
        return parts[:, :dk] + (parts[:, dk:2 * dk] + parts[:, 2 * dk:])

    def local_stage(o, length):
        row = lax.broadcasted_iota(jnp.int32, (length, length), 0)
        col = lax.broadcasted_iota(jnp.int32, (length, length), 1)
        causal = col <= row
        tri = jnp.where(causal, 1.0, 0.0).astype(BF16)
        qkl = [gate_inputs(hh, o, length) for hh in hs]
        vb = [_bf(v_ref[0, pl.ds(o, length), vcol[hh]]) for hh in hs]
        g = [cumsum_time(tri, qkl[hh][2]) for hh in hs]
        g_end = [g[hh][length - 1:length, :] for hh in hs]
        g_mid = [g[hh][length // 2:length // 2 + 1, :] for hh in hs]
        s = [_dot_nt(_bf(qkl[hh][0] * jnp.exp(g[hh] - g_mid[hh])), _bf(qkl[hh][1] * jnp.exp(g_mid[hh] - g[hh])))
             for hh in hs]
        q_dec = [_bf(qkl[hh][0] * jnp.exp(g[hh])) for hh in hs]
        local = [_dot_tn(vb[hh], _bf(qkl[hh][1] * jnp.exp(g_end[hh] - g[hh]))) for hh in hs]
        intra = [_dot(_bf(jnp.where(causal, s[hh], 0.0)), vb[hh]) for hh in hs]
        return q_dec, [jnp.exp(ge) for ge in g_end], local, intra

    def state_stage(o, length, staged, st_in):
        q_dec, decay, local, intra = staged
        inter = [_dot_nt(q_dec[hh], _bf(st_in[hh])) for hh in hs]
        st_out = [st_in[hh] * decay[hh] + local[hh] for hh in hs]
        for hh in hs:
            out = intra[hh] + inter[hh]
            hn = out * lax.rsqrt(jnp.mean(out * out, axis=-1, keepdims=True) + EPS) * hn_ref[:, vcol[hh]]
            og = og_ref[0, pl.ds(o, length), vcol[hh]]
            gate = _silu(og) if mode == "gla" else _sigmoid(og)
            o_ref[0, pl.ds(o, length), vcol[hh]] = (gate * hn).astype(o_ref.dtype)
        return st_out

    def sweep(offsets, length):
        staged = [local_stage(o, length) for o in offsets]
        st = [st_ref[hh] for hh in hs]
        for o, stg in zip(offsets, staged):
            st = state_stage(o, length, stg, st)
        for hh in hs:
            st_ref[hh] = st[hh]

    sweep([0], N_META)
    assert n_chunks % group == 0

    def body(c, carry):
        o = pl.multiple_of(N_META + c * (group * CHUNK), BF16_ROWS)
        sweep([pl.multiple_of(o + j * CHUNK, BF16_ROWS) for j in range(group)], CHUNK)
        return carry

    lax.fori_loop(0, n_chunks // group, body, 0)


def _gla_call(z, n_heads, heads, dk, dv, blocks, gate_width, params, head_norm, mode):
    b, t, _ = z.shape
    q0, k0, v0, og0, g0 = blocks
    zspec = lambda width, off, grouped=True: pl.BlockSpec(
        (1, t, width), (lambda bi, g: (bi, 0, off // width + (g if grouped else 0))))
    (p0, s0), (p1, s1), (p2, s2) = params
    gate_spec = zspec(gate_width, g0, grouped=False) if mode == "gla" else zspec(heads * dk, g0)
    return pl.pallas_call(
        functools.partial(_gla_kernel, mode=mode, heads=heads, group=8),
        grid=(b, n_heads // heads),
        in_specs=[
            zspec(heads * dk, q0), zspec(heads * dk, k0), zspec(heads * dv, v0), zspec(heads * dv, og0),
            gate_spec, s0, s1, s2,
            pl.BlockSpec((1, heads * dv), lambda bi, g: (0, g)),
        ],
        out_specs=pl.BlockSpec((1, t, heads * dv), lambda bi, g: (bi, 0, g)),
        out_shape=jax.ShapeDtypeStruct((b, t, n_heads * dv), BF16),
        scratch_shapes=[pltpu.VMEM((heads, dv, dk), F32)] + (
            [pltpu.VMEM((heads, t, dk), F32)] if mode == "gla" else []),
        compiler_params=_params("parallel", "parallel"),
        name="gla_" + mode,
    )(z, z, z, z, z, p0, p1, p2, head_norm.reshape(1, -1))


def _mla_kernel(cq_ref, ckv_ref, kr_ref, tab_ref, gqa_ref, gkva_ref, wqn_ref, wqr_ref, wkn_ref, wv_ref,
                gqn_ref, gqr_ref, gkn_ref, gkr_ref, o_ref, qf_ref, kf_ref, vf_ref, *, heads):
    t_total = cq_ref.shape[1]
    n_blocks = (ATT_PAD + t_total) // ATT_BLOCK
    dqk = D_NOPE + D_ROPE
    scale = dqk ** -0.5
    rows = _row_tile(t_total, PREP_ROWS)

    hs = range(heads)
    col = [slice(hh * LANES, (hh + 1) * LANES) for hh in hs]
    for hh in hs:
        qf_ref[hh, 0:ATT_PAD, :] = jnp.zeros((ATT_PAD, 2 * LANES), BF16)
        kf_ref[hh, 0:ATT_PAD, :] = jnp.zeros((ATT_PAD, 2 * LANES), BF16)
        vf_ref[hh, 0:ATT_PAD, :] = jnp.zeros((ATT_PAD, D_V), BF16)

    def rope_pair(x, gains, tab):
        p = x * gains * tab
        return p + pltpu.roll(p, D_ROPE, 1)

    def latent_norm(ref, gain_ref, r0):
        u = ref[0, pl.ds(r0, rows), :]
        return _bf(u * lax.rsqrt(jnp.mean(u * u, axis=-1, keepdims=True) + EPS) * gain_ref[...])

    def up_project(c):
        r0 = c * rows
        cq = latent_norm(cq_ref, gqa_ref, r0)
        ckv = latent_norm(ckv_ref, gkva_ref, r0)
        return _dot(cq, wqn_ref[...]), _dot(cq, wqr_ref[...]), _dot(ckv, wkn_ref[...]), _dot(ckv, wv_ref[...])

    def prep(c, hh, projected):
        r0 = c * rows
        dst = ATT_PAD + c * rows
        tab = tab_ref[pl.ds(r0, rows), :]
        qn = projected[0][:, col[hh]]
        qr = projected[1][:, col[hh]]
        ssq = jnp.sum(qn * qn + 0.5 * (qr * qr), axis=-1, keepdims=True)
        rq = lax.rsqrt(ssq / dqk + EPS) * scale
        qf_ref[hh, pl.ds(dst, rows), 0:LANES] = _bf(qn * gqn_ref[...] * rq)
        qf_ref[hh, pl.ds(dst, rows), LANES:2 * LANES] = _bf(qr * gqr_ref[...] * tab * rq)
        kn = projected[2][:, col[hh]]
        kr = kr_ref[0, pl.ds(r0, rows), :]
        ssk = jnp.sum(kn * kn + 0.5 * (kr * kr), axis=-1, keepdims=True)
        rk = lax.rsqrt(ssk / dqk + EPS)
        kf_ref[hh, pl.ds(dst, rows), 0:LANES] = _bf(kn * gkn_ref[...] * rk)
        kf_ref[hh, pl.ds(dst, rows), LANES:2 * LANES] = _bf(rope_pair(kr, gkr_ref[...], tab) * rk)
        vf_ref[hh, pl.ds(dst, rows), :] = _bf(projected[3][:, col[hh]])

    for c in range(t_total // rows):
        projected = up_project(c)
        for hh in hs:
            prep(c, hh, projected)

    qpos = lax.broadcasted_iota(jnp.int32, (ATT_BLOCK, ATT_BLOCK), 0)
    kpos = lax.broadcasted_iota(jnp.int32, (ATT_BLOCK, ATT_BLOCK), 1)
    neg = -jnp.inf

    def scores(item):
        qi, hh = item
        q = qf_ref[hh, qi * ATT_BLOCK:(qi + 1) * ATT_BLOCK, :]
        return _dot_nt(q, kf_ref[hh, 0:(qi + 1) * ATT_BLOCK, :])

    items = [(qi, hh) for qi in range(n_blocks) for hh in hs]
    s_next = scores(items[0])
    for idx, (qi, hh) in enumerate(items):
        s = s_next
        if idx + 1 < len(items):
            s_next = scores(items[idx + 1])
        parts = [s[:, j * ATT_BLOCK:(j + 1) * ATT_BLOCK] for j in range(qi + 1)]
        parts[0] = jnp.where(kpos >= ATT_PAD, parts[0], neg)
        parts[qi] = jnp.where(kpos <= qpos, parts[qi], neg)
        top = functools.reduce(jnp.maximum, parts)
        m = jnp.max(top, axis=-1, keepdims=True)
        if qi == 0:
            m = jnp.where(m == neg, 0.0, m)
        probs = [jnp.exp(part - m) for part in parts]
        l = jnp.sum(functools.reduce(jnp.add, probs), axis=-1, keepdims=True)
        pv = _dot(jnp.concatenate([_bf(pr) for pr in probs], axis=1), vf_ref[hh, 0:(qi + 1) * ATT_BLOCK, :])
        if qi == 0:
            out = pv / jnp.where(l == 0.0, 1.0, l)
            o_ref[0, 0:N_META, col[hh]] = out[ATT_PAD:, :].astype(o_ref.dtype)
        else:
            dst = qi * ATT_BLOCK - ATT_PAD
            o_ref[0, dst:dst + ATT_BLOCK, col[hh]] = (pv / l).astype(o_ref.dtype)


def _mla(z, cq_col, tab, q_a_norm, kv_a_norm, wq, wkv, gqn, gqr, gkn, gkr):
    b, t, _ = z.shape
    assert (ATT_PAD + t) % ATT_BLOCK == 0 and t % BF16_ROWS == 0
    tp = ATT_PAD + t
    heads = 2
    groups = D_HEADS // heads
    ckv_col = cq_col + Q_LORA
    kr_col = ckv_col + KV_LORA
    wspec = lambda k, off: pl.BlockSpec((k, heads * LANES), lambda bi, g: (0, off + g))
    gspec = lambda width: pl.BlockSpec((1, width), lambda bi, g: (0, 0))
    return pl.pallas_call(
        functools.partial(_mla_kernel, heads=heads),
        grid=(b, groups),
        in_specs=[
            pl.BlockSpec((1, t, Q_LORA), lambda bi, g: (bi, 0, cq_col // Q_LORA)),
            pl.BlockSpec((1, t, KV_LORA), lambda bi, g: (bi, 0, ckv_col // KV_LORA)),
            pl.BlockSpec((1, t, LANES), lambda bi, g: (bi, 0, kr_col // LANES)),
            pl.BlockSpec((t, LANES), lambda bi, g: (0, 0)),
            gspec(Q_LORA), gspec(KV_LORA),
            wspec(Q_LORA, 0), wspec(Q_LORA, groups), wspec(KV_LORA, 0), wspec(KV_LORA, groups),
            gspec(LANES), gspec(LANES), gspec(LANES), gspec(LANES),
        ],
        out_specs=pl.BlockSpec((1, t, heads * D_V), lambda bi, g: (bi, 0, g)),
        out_shape=jax.ShapeDtypeStruct((b, t, D_HEADS * D_V), BF16),
        scratch_shapes=[pltpu.VMEM((heads, tp, 2 * LANES), BF16), pltpu.VMEM((heads, tp, 2 * LANES), BF16),
                        pltpu.VMEM((heads, tp, D_V), BF16)],
        compiler_params=_params("parallel", "parallel"),
        name="mla",
    )(z, z, z, tab, q_a_norm.reshape(1, -1), kv_a_norm.reshape(1, -1), wq, wq, wkv, wkv, gqn, gqr, gkn, gkr)


def _pack_bf16_pairs(v):
    w = v.shape[1] // 2
    bits = pltpu.bitcast(_bf(v).astype(F32), jnp.uint32)
    return (bits[:, :w] >> 16) | (bits[:, w:] & jnp.uint32(0xFFFF0000))


def _unpack_lo(words):
    return pltpu.bitcast(words << 16, F32)


def _unpack_hi(words):
    return pltpu.bitcast(words & jnp.uint32(0xFFFF0000), F32)


def _router_kernel(x_ref, g_ref, w_ref, b_ref, gate_ref, idx_ref, xg_ref, cnt_ref, carry_ref):
    tm = x_ref.shape[0]

    @pl.when(pl.program_id(0) == 0)
    def _():
        carry_ref[...] = jnp.zeros_like(carry_ref)

    x = x_ref[...]
    ms = jnp.mean(x * x, axis=-1, keepdims=True)
    xn = x * lax.rsqrt(ms + EPS) * g_ref[...]
    xh = _bf(xn)
    xl = _bf(xn - xh.astype(F32))
    w = w_ref[...]
    wh = _bf(w)
    wl = _bf(w - wh.astype(F32))
    both = _dot(xh, jnp.concatenate([wh, wl], axis=1))
    logits = (both[:, :LANES] + both[:, LANES:]) + _dot(xl, wh) + b_ref[...]
    lane = lax.broadcasted_iota(jnp.int32, logits.shape, 1)
    lane_f = lane.astype(F32)
    neg = -jnp.inf
    big = float(LANES)

    is_group = lane < N_GROUPS
    g_max = jnp.max(jnp.where(is_group, logits, neg), axis=-1, keepdims=True)
    g_sum = jnp.sum(jnp.where(is_group, jnp.exp(logits - g_max), 0.0), axis=-1, keepdims=True)
    p_top = 1.0 / g_sum
    grp = jnp.min(jnp.where(is_group & (logits == g_max), lane_f, big), axis=-1, keepdims=True)

    e_lo = N_GROUPS + grp * EXPERTS_PER_GROUP
    in_grp = (lane_f >= e_lo) & (lane_f < e_lo + EXPERTS_PER_GROUP)
    e_max = jnp.max(jnp.where(in_grp, logits, neg), axis=-1, keepdims=True)
    e_sum = jnp.sum(jnp.where(in_grp, jnp.exp(logits - e_max), 0.0), axis=-1, keepdims=True)
    i1 = jnp.min(jnp.where(in_grp & (logits == e_max), lane_f, big), axis=-1, keepdims=True)
    rest = in_grp & (lane_f != i1)
    e_2nd = jnp.max(jnp.where(rest, logits, neg), axis=-1, keepdims=True)
    i2 = jnp.min(jnp.where(rest & (logits == e_2nd), lane_f, big), axis=-1, keepdims=True)
    p1 = 1.0 / e_sum
    p2 = jnp.exp(e_2nd - e_max) / e_sum
    tot = p1 + p2
    gate_ref[...] = jnp.where(lane == 0, p_top * p1 / tot, jnp.where(lane == 1, p_top * p2 / tot, 0.0))

    e1 = i1 - N_GROUPS
    e2 = i2 - N_GROUPS
    hot = jnp.where((lane_f == e1) | (lane_f == e2), 1.0, 0.0)
    row = lax.broadcasted_iota(jnp.int32, (tm, tm), 0)
    col = lax.broadcasted_iota(jnp.int32, (tm, tm), 1)
    before = _dot(jnp.where(col < row, 1.0, 0.0).astype(BF16), _bf(hot)) + carry_ref[...]
    r1 = jnp.sum(jnp.where(lane_f == e1, before, 0.0), axis=-1, keepdims=True)
    r2 = jnp.sum(jnp.where(lane_f == e2, before, 0.0), axis=-1, keepdims=True)
    total = carry_ref[...] + jnp.sum(hot, axis=0, keepdims=True)
    carry_ref[...] = total
    cnt_ref[...] = jnp.broadcast_to(total, cnt_ref.shape).astype(jnp.int32)
    idx_ref[...] = jnp.where(lane == 0, e1, jnp.where(lane == 1, e2, jnp.where(lane == 2, r1, jnp.where(
        lane == 3, r2, 0.0)))).astype(jnp.int32)

    words = _pack_bf16_pairs(xn)
    for s in range(8):
        xg_ref[pl.ds(s, tm, stride=8), :] = words[:, s * LANES:(s + 1) * LANES]


def _router(x2d, gain, w_group, b_group, w_expert, b_expert):
    m, d = x2d.shape
    assert d == 2 * 8 * LANES
    tm = _row_tile(m, ROW_TILE)
    pad = LANES - N_GROUPS - N_EXPERTS
    w = jnp.concatenate([w_group, w_expert, jnp.zeros((d, pad), F32)], axis=1)
    bias = jnp.concatenate([b_group, b_expert, jnp.zeros((pad,), F32)]).reshape(1, LANES)
    return pl.pallas_call(
        _router_kernel,
        grid=(m // tm,),
        in_specs=[
            pl.BlockSpec((tm, d), lambda i: (i, 0)),
            pl.BlockSpec((1, d), lambda i: (0, 0)),
            pl.BlockSpec((d, LANES), lambda i: (0, 0)),
            pl.BlockSpec((1, LANES), lambda i: (0, 0)),
        ],
        out_specs=[pl.BlockSpec((tm, LANES), lambda i: (i, 0)), pl.BlockSpec((tm, LANES), lambda i: (i, 0)),
                   pl.BlockSpec((tm * 8, LANES), lambda i: (i, 0)), pl.BlockSpec((8, LANES), lambda i: (0, 0))],
        out_shape=[jax.ShapeDtypeStruct((m, LANES), F32), jax.ShapeDtypeStruct((m, LANES), jnp.int32),
                   jax.ShapeDtypeStruct((m * 8, LANES), jnp.uint32), jax.ShapeDtypeStruct((8, LANES), jnp.int32)],
        scratch_shapes=[pltpu.VMEM((1, LANES), F32)],
        compiler_params=_params("arbitrary"),
        name="router",
    )(x2d, gain.reshape(1, d), w, bias)


def _invert_kernel(dest_ref, inv_ref):
    def clear(s, carry):
        inv_ref[s] = -1
        return carry

    lax.fori_loop(0, inv_ref.shape[0], clear, 0, unroll=8)

    def put(f, carry):
        inv_ref[dest_ref[f]] = f
        return carry

    lax.fori_loop(0, dest_ref.shape[0], put, 0, unroll=8)


def _invert(dest, p):
    assert p % 8 == 0 and dest.shape[0] % 8 == 0
    smem = pl.BlockSpec(memory_space=pltpu.SMEM)
    return pl.pallas_call(
        _invert_kernel, in_specs=[smem], out_specs=smem,
        out_shape=jax.ShapeDtypeStruct((p,), jnp.int32), name="moe_invert",
    )(dest)


def _expert_kernel(be_ref, nxt_ref, run_ref, nu_ref, src_ref, dst_ref, xg_hbm, w1_hbm, w3_hbm, w2_hbm, o_hbm,
                   xbuf, ybuf, w1f, w3f, w2f, w1s, w3s, w2s, sem_in, sem_out, sem_w, *, layer):
    i = pl.program_id(0)
    n_used = nu_ref[0]
    par = i % 2
    half_e = D_EXPERT // 2

    def weight_copies(expert, slot):
        return [pltpu.make_async_copy(w_hbm.at[layer, expert], w_f.at[slot], sem_w.at[slot, j])
                for j, (w_hbm, w_f) in enumerate(((w1_hbm, w1f), (w3_hbm, w3f), (w2_hbm, w2f)))]

    def gather(block, r, slot):
        src = pl.multiple_of(src_ref[block * MOE_BLOCK + r], 8)
        return pltpu.make_async_copy(xg_hbm.at[pl.ds(src, 8), :], xbuf.at[slot, pl.ds(r * 8, 8), :], sem_in.at[slot])

    def scatter(block, r, slot):
        dst = pl.multiple_of(dst_ref[(block + 1) * MOE_BLOCK + r], 8)
        return pltpu.make_async_copy(ybuf.at[slot, pl.ds(r * 8, 8), :], o_hbm.at[pl.ds(dst, 8), :], sem_out.at[slot])

    def wait_gathers(slot):
        pltpu.make_async_copy(xbuf.at[1 - slot], xbuf.at[slot], sem_in.at[slot]).wait()

    def wait_scatters(slot):
        pltpu.make_async_copy(ybuf.at[slot], ybuf.at[1 - slot], sem_out.at[slot]).wait()

    @pl.when(i == 0)
    def _():
        ybuf[1] = jnp.zeros(ybuf.shape[1:], ybuf.dtype)
        n_real = o_hbm.shape[0] - 2 * MOE_BLOCK * 8
        fill1 = pltpu.make_async_copy(ybuf.at[1], o_hbm.at[pl.ds(n_real + MOE_BLOCK * 8, MOE_BLOCK * 8), :],
                                      sem_out.at[1])
        fill1.start()
        fill1.wait()
        pltpu.make_async_copy(ybuf.at[1], o_hbm.at[pl.ds(n_real, MOE_BLOCK * 8), :], sem_out.at[0]).start()

        def first(r, carry):
            gather(0, r, 0).start()
            return carry

        lax.fori_loop(0, MOE_BLOCK, first, 0, unroll=8)

    @pl.when(i < n_used)
    def _():
        expert = be_ref[i]
        wslot = run_ref[i] % 2

        @pl.when(i == 0)
        def _():
            for cp in weight_copies(expert, 0):
                cp.start()

        @pl.when((i == 0) | (expert != be_ref[jnp.maximum(i - 1, 0)]))
        def _():
            for cp in weight_copies(expert, wslot):
                cp.wait()

            @pl.when(nxt_ref[i] != expert)
            def _():
                for cp in weight_copies(nxt_ref[i], 1 - wslot):
                    cp.start()

            w1s[...] = _bf(w1f[wslot])
            w3s[...] = _bf(w3f[wslot])
            w2s[...] = _bf(w2f[wslot])

        wait_gathers(par)

        def move_rows(group, n_groups=6):
            lo, hi = group * MOE_BLOCK // n_groups, (group + 1) * MOE_BLOCK // n_groups
            for r in range(lo, hi):
                gather(i + 1, r, 1 - par).start(priority=r % 2)
                scatter(i - 1, r, 1 - par).start(priority=(r + 1) % 2)

        tiles = [xbuf[par, pl.ds(s, MOE_BLOCK, stride=8), :] for s in range(8)]
        xb = jnp.concatenate([_bf(_unpack_lo(w)) for w in tiles] + [_bf(_unpack_hi(w)) for w in tiles], axis=1)
        move_rows(0)
        h1a = _dot(xb, w1s[:, :half_e])
        move_rows(1)
        h3a = _dot(xb, w3s[:, :half_e])
        move_rows(2)
        act_a = _bf(_silu(h1a) * h3a)
        h1b = _dot(xb, w1s[:, half_e:])
        move_rows(3)
        h3b = _dot(xb, w3s[:, half_e:])
        move_rows(4)
        act_b = _bf(_silu(h1b) * h3b)
        y = _dot(act_a, w2s[:half_e, :])
        move_rows(5)
        y = y + _dot(act_b, w2s[half_e:, :])

        wait_scatters(par)
        words = _pack_bf16_pairs(y)
        for s in range(8):
            ybuf[par, pl.ds(s, MOE_BLOCK, stride=8), :] = words[:, s * LANES:(s + 1) * LANES]

        @pl.when(i == n_used - 1)
        def _():
            def last(r, carry):
                scatter(i, r, par).start()
                return carry

            lax.fori_loop(0, MOE_BLOCK, last, 0, unroll=8)
            wait_scatters(par)
            wait_scatters(1 - par)
            wait_gathers(1 - par)


def _moe(x2d, gain, w_group, b_group, w_expert, b_expert, w1, w3, w2, layer, final_shape=None):
    n, d = x2d.shape
    gates_l, idx_l, xg, cnt = _router(x2d, gain, w_group, b_group, w_expert, b_expert)

    a = n * TOP_K
    n_blocks = -(-a // MOE_BLOCK) + N_EXPERTS
    p = n_blocks * MOE_BLOCK
    counts = cnt[0, :N_EXPERTS]
    padded = (counts + MOE_BLOCK - 1) // MOE_BLOCK * MOE_BLOCK
    pad_end = jnp.cumsum(padded)
    pad_start = pad_end - padded
    e_hot = idx_l[:, :TOP_K, None] == jnp.arange(N_EXPERTS, dtype=jnp.int32)
    dest = jnp.sum(jnp.where(e_hot, pad_start, 0), axis=-1) + idx_l[:, TOP_K:2 * TOP_K]
    dest = jnp.clip(dest.reshape(-1), 0, p - 1).astype(jnp.int32)
    blk0 = jnp.arange(n_blocks, dtype=jnp.int32) * MOE_BLOCK
    block_expert = jnp.minimum(jnp.searchsorted(pad_end, blk0, side="right"), N_EXPERTS - 1).astype(jnp.int32)
    n_used = (pad_end[-1] // MOE_BLOCK).astype(jnp.int32).reshape(1)
    experts = jnp.arange(N_EXPERTS, dtype=jnp.int32)
    later = jnp.where((experts[None, :] > experts[:, None]) & (counts[None, :] > 0), experts[None, :], N_EXPERTS)
    next_owner = jnp.min(later, axis=1)
    next_expert = jnp.where(next_owner < N_EXPERTS, next_owner, experts)[block_expert].astype(jnp.int32)
    run_index = (jnp.cumsum(jnp.concatenate([jnp.ones((1,), jnp.int32),
                                             (block_expert[1:] != block_expert[:-1]).astype(jnp.int32)])) - 1
                 ).astype(jnp.int32)
    codes = _invert(dest, p)
    slot = jnp.arange(p, dtype=jnp.int32)
    spare = TOP_K * n + (slot // MOE_BLOCK % 2) * MOE_BLOCK + slot % MOE_BLOCK
    src_tok = (jnp.maximum(codes, 0) >> 1) * 8
    dst_row = jnp.where(codes >= 0, (codes & 1) * n + (codes >> 1), spare) * 8
    lead = (TOP_K * n + MOE_BLOCK + jnp.arange(MOE_BLOCK, dtype=jnp.int32)) * 8
    dst_row = jnp.concatenate([lead, dst_row])

    hbm = pl.BlockSpec(memory_space=pl.ANY)
    out_rows = TOP_K * n + 2 * MOE_BLOCK
    out2 = pl.pallas_call(
        functools.partial(_expert_kernel, layer=layer),
        grid_spec=pltpu.PrefetchScalarGridSpec(
            num_scalar_prefetch=6,
            grid=(n_blocks,),
            in_specs=[hbm, hbm, hbm, hbm],
            out_specs=hbm,
            scratch_shapes=[
                pltpu.VMEM((2, MOE_BLOCK * 8, LANES), jnp.uint32), pltpu.VMEM((2, MOE_BLOCK * 8, LANES), jnp.uint32),
                pltpu.VMEM((2, d, D_EXPERT), F32), pltpu.VMEM((2, d, D_EXPERT), F32), pltpu.VMEM((2, D_EXPERT, d), F32),
                pltpu.VMEM((d, D_EXPERT), BF16), pltpu.VMEM((d, D_EXPERT), BF16), pltpu.VMEM((D_EXPERT, d), BF16),
                pltpu.SemaphoreType.DMA((2,)), pltpu.SemaphoreType.DMA((2,)), pltpu.SemaphoreType.DMA((2, 3)),
            ],
        ),
        out_shape=jax.ShapeDtypeStruct((out_rows * 8, LANES), jnp.uint32),
        compiler_params=_params("arbitrary"),
        name="moe_experts",
    )(block_expert, next_expert, run_index, n_used, src_tok, dst_row, xg, w1, w3, w2)
    return _combine(x2d, gates_l, out2, final_shape)


def _combine_tile(x_ref, gate_ref, a_ref, b_ref, o_ref):
    tm = x_ref.shape[0]
    half = x_ref.shape[1] // 2
    g0 = gate_ref[:, 0:1]
    g1 = gate_ref[:, 1:2]
    for s in range(8):
        wa = a_ref[pl.ds(s, tm, stride=8), :]
        wb = b_ref[pl.ds(s, tm, stride=8), :]
        lo = slice(s * LANES, (s + 1) * LANES)
        hi = slice(half + s * LANES, half + (s + 1) * LANES)
        o_ref[:, lo] = x_ref[:, lo] + (g0 * _unpack_lo(wa) + g1 * _unpack_lo(wb))
        o_ref[:, hi] = x_ref[:, hi] + (g0 * _unpack_hi(wa) + g1 * _unpack_hi(wb))


def _combine_kernel(x_ref, gate_ref, a_ref, b_ref, o_ref):
    _combine_tile(x_ref, gate_ref, a_ref, b_ref, o_ref)


def _combine_final_kernel(x_ref, gate_ref, a_ref, b_ref, o_hbm, obuf, sem, *, tiles_per_batch):
    i = pl.program_id(0)
    n_steps = pl.num_programs(0)
    tm = x_ref.shape[0]

    def copies(step, slot):
        batch, j = step // tiles_per_batch, step % tiles_per_batch
        row = pl.multiple_of(j * tm, 8)
        main = pltpu.make_async_copy(obuf.at[slot, pl.ds(N_META, tm - N_META), :],
                                     o_hbm.at[batch, pl.ds(row, tm - N_META), :], sem.at[slot, 0])
        head = pltpu.make_async_copy(obuf.at[slot, pl.ds(0, N_META), :],
                                     o_hbm.at[batch, pl.ds(pl.multiple_of(jnp.maximum(row - N_META, 0), 8), N_META), :],
                                     sem.at[slot, 1])
        return main, head, j > 0

    def wait_step(step, slot):
        main, head, has_head = copies(step, slot)
        main.wait()

        @pl.when(has_head)
        def _():
            head.wait()

    slot = i % 2

    @pl.when(i >= 2)
    def _():
        wait_step(i - 2, slot)

    _combine_tile(x_ref, gate_ref, a_ref, b_ref, obuf.at[slot])
    main, head, has_head = copies(i, slot)
    main.start()

    @pl.when(has_head)
    def _():
        head.start()

    @pl.when(i == n_steps - 1)
    def _():
        wait_step(i, slot)

        @pl.when(i >= 1)
        def _():
            wait_step(i - 1, 1 - slot)


def _combine(x2d, gates, out2, final_shape=None):
    n, d = x2d.shape
    tm = _row_tile(n, ROW_TILE)
    in_specs = [
        pl.BlockSpec((tm, d), lambda i: (i, 0)),
        pl.BlockSpec((tm, LANES), lambda i: (i, 0)),
        pl.BlockSpec((tm * 8, LANES), lambda i: (i, 0)),
        pl.BlockSpec((tm * 8, LANES), lambda i: (n // tm + i, 0)),
    ]
    if final_shape is None:
        return pl.pallas_call(
            _combine_kernel,
            grid=(n // tm,),
            in_specs=in_specs,
            out_specs=pl.BlockSpec((tm, d), lambda i: (i, 0)),
            out_shape=jax.ShapeDtypeStruct((n, d), F32),
            compiler_params=_params("parallel"),
            name="moe_combine",
        )(x2d, gates, out2, out2)
    b, t = final_shape
    assert t % tm == 0 and tm > N_META
    return pl.pallas_call(
        functools.partial(_combine_final_kernel, tiles_per_batch=t // tm),
        grid=(n // tm,),
        in_specs=in_specs,
        out_specs=pl.BlockSpec(memory_space=pl.ANY),
        out_shape=jax.ShapeDtypeStruct((b, t - N_META, d), F32),
        scratch_shapes=[pltpu.VMEM((2, tm, d), F32), pltpu.SemaphoreType.DMA((2, 2))],
        compiler_params=_params("arbitrary"),
        name="moe_combine_final",
    )(x2d, gates, out2, out2)


def _even_layer(x, norm_g, w_in, conv_w, conv_b, b_i, b_f, a_norm, w_gate2, b_gate, b_norm, w_out):
    b, t, d = x.shape
    n = b * t
    a_w = 2 * A_HEADS * A_DK + 2 * A_HEADS * A_DV
    g_w = 2 * A_HEADS
    b_w = 2 * B_HEADS * B_DK + 2 * B_HEADS * B_DV
    main = a_w + b_w
    w = _even_weight(w_in, a_w, g_w, b_w, GATE_RANK)
    z = _normproj(x.reshape(n, d), norm_g, w).reshape(b, t, main + MXU_DIM)

    ya = _mlstm(z, main, conv_w, conv_b, b_i, b_f, a_norm)

    wg = jnp.zeros((B_HEADS, MXU_DIM, B_DK), F32).at[:, g_w:g_w + GATE_RANK, :].set(
        w_gate2.reshape(GATE_RANK, B_HEADS, B_DK).transpose(1, 0, 2))
    dummy = jnp.zeros((1, B_HEADS * B_DK), F32)
    hp = 2
    hspec = pl.BlockSpec((1, hp * B_DK), lambda bi, g: (0, g))
    yb = _gla_call(
        z, B_HEADS, hp, B_DK, B_DV,
        (a_w, a_w + B_HEADS * B_DK, a_w + 2 * B_HEADS * B_DK, a_w + 2 * B_HEADS * B_DK + B_HEADS * B_DV, main),
        MXU_DIM,
        ((wg, pl.BlockSpec((hp, MXU_DIM, B_DK), lambda bi, g: (g, 0, 0))),
         (b_gate.reshape(1, -1), hspec), (dummy, hspec)),
        b_norm, "gla")
    return _outproj(ya.reshape(n, -1), yb.reshape(n, -1), w_out.astype(BF16), x.reshape(n, d)).reshape(b, t, d)


def _odd_layer(x, lb, norm_g, w_in, c_norm, q_a_norm, w_q_up, kv_a_norm, w_kv_up, q_norm, k_norm, w_out):
    b, t, d = x.shape
    n = b * t
    c_w = 2 * C_HEADS * C_DK + 2 * C_HEADS * C_DV
    swap = (jnp.arange(D_ROPE) + D_ROPE // 2) % D_ROPE
    kr0 = c_w + Q_LORA + KV_LORA
    used = kr0 + 2 * D_ROPE
    total = -(-used // MXU_DIM) * MXU_DIM
    z2 = _normproj(x.reshape(n, d), norm_g, _odd_weight(w_in, kr0))
    z = z2.reshape(b, t, total)

    hp = 4
    hspec = pl.BlockSpec((1, hp * C_DK), lambda bi, g: (0, g))
    yc = _gla_call(
        z, C_HEADS, hp, C_DK, C_DV,
        (0, C_HEADS * C_DK, 2 * C_HEADS * C_DK, 2 * C_HEADS * C_DK + C_HEADS * C_DV, C_HEADS * C_DK),
        C_DK,
        ((jnp.log(lb).reshape(1, -1), hspec), (jnp.log1p(-lb).reshape(1, -1), hspec), ((1.0 - lb).reshape(1, -1), hspec)),
        c_norm, "hgrn")

    dq = D_NOPE + D_ROPE
    wq = w_q_up.reshape(Q_LORA, D_HEADS, dq)
    wq_rope = wq[:, :, D_NOPE:]
    wq_p = jnp.concatenate([wq[:, :, :D_NOPE].reshape(Q_LORA, -1),
                            jnp.concatenate([wq_rope, wq_rope[:, :, swap]], axis=-1).reshape(Q_LORA, -1)],
                           axis=1).astype(BF16)
    wkv = w_kv_up.reshape(KV_LORA, D_HEADS, D_NOPE + D_V)
    wkv_p = jnp.concatenate([wkv[:, :, :D_NOPE].reshape(KV_LORA, -1), wkv[:, :, D_NOPE:].reshape(KV_LORA, -1)],
                            axis=1).astype(BF16)
    pos = jnp.arange(t, dtype=F32)
    half = D_ROPE // 2
    inv = ROPE_THETA ** (-jnp.arange(half, dtype=F32) / half)
    ang = pos[:, None] * inv[None, :]
    cos, sin = jnp.cos(ang), jnp.sin(ang)
    tab = jnp.concatenate([cos, cos, -sin, sin], axis=1)
    pair = lambda g: jnp.concatenate([g[D_NOPE:], g[D_NOPE:][swap]]).reshape(1, LANES)
    yd = _mla(z, c_w, tab, q_a_norm, kv_a_norm, wq_p, wkv_p, q_norm[:D_NOPE].reshape(1, LANES), pair(q_norm),
              k_norm[:D_NOPE].reshape(1, LANES), pair(k_norm))
    return _outproj(yc.reshape(n, -1), yd.reshape(n, -1), w_out.astype(BF16), x.reshape(n, d)).reshape(b, t, d)


def kernel(x, meta_tokens, ab_norm, ab_w_in, a_conv_w, a_conv_b, a_b_i, a_b_f, a_head_norm, b_w_gate2, b_b_gate, b_head_norm, ab_w_out, cd_norm, cd_w_in, c_lower_bound, c_head_norm, d_q_a_norm, d_w_q_up, d_kv_a_norm, d_w_kv_up, d_q_norm, d_k_norm, cd_w_out, moe_norm, moe_w_group, moe_b_group, moe_w_expert, moe_b_expert, moe_w1, moe_w3, moe_w2):
    b = x.shape[0]
    depth = moe_norm.shape[0]
    h = jnp.concatenate([jnp.broadcast_to(meta_tokens.astype(x.dtype)[None], (b, N_META, D_MODEL)), x], axis=1)
    t = h.shape[1]
    lb_cum = jnp.cumsum(jax.nn.softmax(c_lower_bound.astype(F32), axis=0), axis=0)
    lower_bounds = lb_cum - lb_cum[0]
    for layer in range(depth):
        j = layer // 2
        if layer % 2 == 0:
            h = _even_layer(h, ab_norm[j], ab_w_in[j], a_conv_w[j], a_conv_b[j], a_b_i[j], a_b_f[j], a_head_norm[j],
                            b_w_gate2[j], b_b_gate[j], b_head_norm[j], ab_w_out[j])
        else:
            h = _odd_layer(h, lower_bounds[layer], cd_norm[j], cd_w_in[j], c_head_norm[j], d_q_a_norm[j],
                           d_w_q_up[j], d_kv_a_norm[j], d_w_kv_up[j], d_q_norm[j], d_k_norm[j], cd_w_out[j])
        last = layer == depth - 1
        h = _moe(h.reshape(b * t, D_MODEL), moe_norm[layer], moe_w_group[layer], moe_b_group[layer],
                 moe_w_expert[layer], moe_b_expert[layer], moe_w1, moe_w3, moe_w2, layer,
                 final_shape=(b, t) if last else None)
        if not last:
            h = h.reshape(b, t, D_MODEL)
    return h
```
